```python
import jax, jax.numpy as jnp
from jax import lax
import numpy as np

D_MODEL = 2048
BATCH = 8
SEQ = 8192
DEPTH = 1

N_MEM = 256
EPS = 1e-5
D_MIX = D_MODEL
D_A = D_MIX // 2
D_B = D_MIX - D_A
CHUNK = 128
A_GROUPS = 8
A_GROUP_DIM = D_A // A_GROUPS
HEAD_DIM = 64
B_Q_HEADS = D_B // HEAD_DIM
B_KV_HEADS = 2
B_GROUP = B_Q_HEADS // B_KV_HEADS
WINDOW = 128
BLOCK = 128
X_HEADS = 4
X_HEAD_DIM = D_MODEL // X_HEADS
D_FF = 5632
IN_COLS = 2 * D_A + D_B + 2 * B_KV_HEADS * HEAD_DIM
NEG = -1e30

kernel_name = "hybrid_sgu_swa_sink_macaron_layer"


def rmsnorm(x, g):
    xf = x.astype(jnp.float32)
    y = xf * lax.rsqrt(jnp.mean(xf * xf, axis=-1, keepdims=True) + EPS)
    return (y * g.astype(jnp.float32)).astype(x.dtype)


def swiglu(x, w_gate, w_up, w_down):
    return (jax.nn.silu(x @ w_gate) * (x @ w_up)) @ w_down


def spatial_gating(z_uv, g_v, w_s, b_s):
    b, s, _ = z_uv.shape
    z = jax.nn.gelu(z_uv, approximate=False)
    u, v = z[..., :D_A], z[..., D_A:]
    v = rmsnorm(v, g_v)
    v = v.reshape(b, s // CHUNK, CHUNK, A_GROUPS, A_GROUP_DIM)
    causal = jnp.tril(jnp.ones((CHUNK, CHUNK), dtype=bool))
    ws = jnp.where(causal[None], w_s, jnp.zeros_like(w_s))
    sv = jnp.einsum('gts,bnsgc->bntgc', ws, v) + jnp.transpose(b_s)[None, None, :, :, None]
    return u * sv.reshape(b, s, D_A)


def window_attention_sinks(q, k, v, sinks):
    b, s, _ = q.shape
    nb = s // BLOCK
    q = q.reshape(b, nb, BLOCK, B_KV_HEADS, B_GROUP, HEAD_DIM)
    k = k.reshape(b, nb, BLOCK, B_KV_HEADS, HEAD_DIM)
    v = v.reshape(b, nb, BLOCK, B_KV_HEADS, HEAD_DIM)
    pad = ((0, 0), (1, 0), (0, 0), (0, 0), (0, 0))
    kk = jnp.concatenate([jnp.pad(k, pad)[:, :-1], k], axis=2)
    vv = jnp.concatenate([jnp.pad(v, pad)[:, :-1], v], axis=2)
    scores = jnp.einsum('bnqkgd,bnskd->bnkgqs', q, kk,
                        preferred_element_type=jnp.float32) * (HEAD_DIM ** -0.5)
    qpos = jnp.arange(BLOCK)[:, None] + BLOCK
    kpos = jnp.arange(2 * BLOCK)[None, :]
    diff = qpos - kpos
    band = (diff >= 0) & (diff < WINDOW)
    first = (jnp.arange(nb)[:, None, None] == 0) & (kpos[None] < BLOCK)
    mask = band[None] & ~first
    scores = jnp.where(mask[None, :, None, None], scores, NEG)
    sink = jnp.broadcast_to(
        sinks.astype(jnp.float32).reshape(1, 1, B_KV_HEADS, B_GROUP, 1, 1),
        scores.shape[:-1] + (1,))
    probs = jax.nn.softmax(jnp.concatenate([scores, sink], axis=-1), axis=-1)[..., :-1]
    out = jnp.einsum('bnkgqs,bnskd->bnqkgd', probs.astype(vv.dtype), vv)
    return out.reshape(b, s, D_B)


def cross_attention(hn, memn, w_q, w_kv, w_o):
    b, s, _ = hn.shape
    m = memn.shape[1]
    q = (hn @ w_q).reshape(b, s, X_HEADS, X_HEAD_DIM)
    kv = memn @ w_kv
    k = kv[..., :D_MODEL].reshape(b, m, X_HEADS, X_HEAD_DIM)
    v = kv[..., D_MODEL:].reshape(b, m, X_HEADS, X_HEAD_DIM)
    scores = jnp.einsum('bshd,bmhd->bhsm', q, k,
                        preferred_element_type=jnp.float32) * (X_HEAD_DIM ** -0.5)
    probs = jax.nn.softmax(scores, axis=-1).astype(v.dtype)
    out = jnp.einsum('bhsm,bmhd->bshd', probs, v).reshape(b, s, D_MODEL)
    return out @ w_o


def _fwd_setup_inputs(seed: int = 0) -> dict:
    key = jax.random.key(seed)
    ks = jax.random.split(key, 32)
    L, D, F = DEPTH, D_MODEL, D_FF

    def nrm(k, shape, scale):
        return jax.random.normal(k, shape, dtype=jnp.float32) * scale

    def gain(k, shape):
        return 1.0 + 0.05 * jax.random.normal(k, shape, dtype=jnp.float32)

    return {
        "x": nrm(ks[0], (BATCH, SEQ, D), 1.0),
        "mem": nrm(ks[1], (BATCH, N_MEM, D), 1.0),
        "g_ffn1": gain(ks[2], (L, D)),
        "w1_gate": nrm(ks[3], (L, D, F), D ** -0.5),
        "w1_up": nrm(ks[4], (L, D, F), D ** -0.5),
        "w1_down": nrm(ks[5], (L, F, D), F ** -0.5),
        "g_mix": gain(ks[6], (L, D)),
        "w_in": nrm(ks[7], (L, D, IN_COLS), D ** -0.5),
        "g_v": gain(ks[8], (L, D_A)),
        "w_s": nrm(ks[9], (L, A_GROUPS, CHUNK, CHUNK), 0.5 * CHUNK ** -0.5),
        "b_s": 1.0 + 0.1 * jax.random.normal(ks[10], (L, A_GROUPS, CHUNK), dtype=jnp.float32),
        "sinks": nrm(ks[11], (L, B_Q_HEADS), 0.5),
        "g_a_out": gain(ks[12], (L, D_A)),
        "g_b_out": gain(ks[13], (L, D_B)),
        "w_out": nrm(ks[14], (L, D_MIX, D), D_MIX ** -0.5),
        "g_x": gain(ks[15], (L, D)),
        "g_mem": gain(ks[16], (L, D)),
        "w_xq": nrm(ks[17], (L, D, D), D ** -0.5),
        "w_xkv": nrm(ks[18], (L, D, 2 * D), D ** -0.5),
        "w_xo": nrm(ks[19], (L, D, D), D ** -0.5),
        "g_ffn2": gain(ks[20], (L, D)),
        "w2_gate": nrm(ks[21], (L, D, F), D ** -0.5),
        "w2_up": nrm(ks[22], (L, D, F), D ** -0.5),
        "w2_down": nrm(ks[23], (L, F, D), F ** -0.5),
        "g_final": gain(ks[24], (D,)),
    }


def _fwd_reference(x, mem, g_ffn1, w1_gate, w1_up, w1_down, g_mix, w_in, g_v, w_s, b_s,
              sinks, g_a_out, g_b_out, w_out, g_x, g_mem, w_xq, w_xkv, w_xo,
              g_ffn2, w2_gate, w2_up, w2_down, g_final):
    o_q = 2 * D_A
    o_k = o_q + D_B
    o_v = o_k + B_KV_HEADS * HEAD_DIM
    h = x
    for l in range(DEPTH):
        h = h + 0.5 * swiglu(rmsnorm(h, g_ffn1[l]), w1_gate[l], w1_up[l], w1_down[l])
        z = rmsnorm(h, g_mix[l]) @ w_in[l]
        y_a = spatial_gating(z[..., :o_q], g_v[l], w_s[l], b_s[l])
        y_b = window_attention_sinks(z[..., o_q:o_k], z[..., o_k:o_v], z[..., o_v:], sinks[l])
        y = jnp.concatenate([rmsnorm(y_a, g_a_out[l]), rmsnorm(y_b, g_b_out[l])], axis=-1)
        h = h + y @ w_out[l]
        h = h + cross_attention(rmsnorm(h, g_x[l]), rmsnorm(mem, g_mem[l]),
                                w_xq[l], w_xkv[l], w_xo[l])
        h = h + 0.5 * swiglu(rmsnorm(h, g_ffn2[l]), w2_gate[l], w2_up[l], w2_down[l])
    return rmsnorm(h, g_final)


import jax as _jax
import jax.numpy as _jnp

TWIN_FORMAT = 'train_step'
FWD_PARAMS = ['x', 'mem', 'g_ffn1', 'w1_gate', 'w1_up', 'w1_down', 'g_mix', 'w_in', 'g_v', 'w_s', 'b_s', 'sinks', 'g_a_out', 'g_b_out', 'w_out', 'g_x', 'g_mem', 'w_xq', 'w_xkv', 'w_xo', 'g_ffn2', 'w2_gate', 'w2_up', 'w2_down', 'g_final']
TWIN_WEIGHTS = ['g_ffn1', 'w1_gate', 'w1_up', 'w1_down', 'g_mix', 'w_in', 'g_v', 'w_s', 'b_s', 'sinks', 'g_a_out', 'g_b_out', 'w_out', 'g_x', 'g_mem', 'w_xq', 'w_xkv', 'w_xo', 'g_ffn2', 'w2_gate', 'w2_up', 'w2_down', 'g_final']
TWIN_DIFF_INPUT = 'x'
TWIN_INPUTS = ['x', 'mem', 'g_ffn1', 'w1_gate', 'w1_up', 'w1_down', 'g_mix', 'w_in', 'g_v', 'w_s', 'b_s', 'sinks', 'g_a_out', 'g_b_out', 'w_out', 'g_x', 'g_mem', 'w_xq', 'w_xkv', 'w_xo', 'g_ffn2', 'w2_gate', 'w2_up', 'w2_down', 'g_final', 'loss_target', 'm_g_ffn1', 'm_w1_gate', 'm_w1_up', 'm_w1_down', 'm_g_mix', 'm_w_in', 'm_g_v', 'm_w_s', 'm_b_s', 'm_sinks', 'm_g_a_out', 'm_g_b_out', 'm_w_out', 'm_g_x', 'm_g_mem', 'm_w_xq', 'm_w_xkv', 'm_w_xo', 'm_g_ffn2', 'm_w2_gate', 'm_w2_up', 'm_w2_down', 'm_g_final', 'v_g_ffn1', 'v_w1_gate', 'v_w1_up', 'v_w1_down', 'v_g_mix', 'v_w_in', 'v_g_v', 'v_w_s', 'v_b_s', 'v_sinks', 'v_g_a_out', 'v_g_b_out', 'v_w_out', 'v_g_x', 'v_g_mem', 'v_w_xq', 'v_w_xkv', 'v_w_xo', 'v_g_ffn2', 'v_w2_gate', 'v_w2_up', 'v_w2_down', 'v_g_final']
TWIN_OUTPUTS = ['loss', 'grad_x', 'grad_g_ffn1', 'grad_w1_gate', 'grad_w1_up', 'grad_w1_down', 'grad_g_mix', 'grad_w_in', 'grad_g_v', 'grad_w_s', 'grad_b_s', 'grad_sinks', 'grad_g_a_out', 'grad_g_b_out', 'grad_w_out', 'grad_g_x', 'grad_g_mem', 'grad_w_xq', 'grad_w_xkv', 'grad_w_xo', 'grad_g_ffn2', 'grad_w2_gate', 'grad_w2_up', 'grad_w2_down', 'grad_g_final', 'delta_g_ffn1', 'delta_w1_gate', 'delta_w1_up', 'delta_w1_down', 'delta_g_mix', 'delta_w_in', 'delta_g_v', 'delta_w_s', 'delta_b_s', 'delta_sinks', 'delta_g_a_out', 'delta_g_b_out', 'delta_w_out', 'delta_g_x', 'delta_g_mem', 'delta_w_xq', 'delta_w_xkv', 'delta_w_xo', 'delta_g_ffn2', 'delta_w2_gate', 'delta_w2_up', 'delta_w2_down', 'delta_g_final', 'new_m_g_ffn1', 'new_m_w1_gate', 'new_m_w1_up', 'new_m_w1_down', 'new_m_g_mix', 'new_m_w_in', 'new_m_g_v', 'new_m_w_s', 'new_m_b_s', 'new_m_sinks', 'new_m_g_a_out', 'new_m_g_b_out', 'new_m_w_out', 'new_m_g_x', 'new_m_g_mem', 'new_m_w_xq', 'new_m_w_xkv', 'new_m_w_xo', 'new_m_g_ffn2', 'new_m_w2_gate', 'new_m_w2_up', 'new_m_w2_down', 'new_m_g_final', 'new_v_g_ffn1', 'new_v_w1_gate', 'new_v_w1_up', 'new_v_w1_down', 'new_v_g_mix', 'new_v_w_in', 'new_v_g_v', 'new_v_w_s', 'new_v_b_s', 'new_v_sinks', 'new_v_g_a_out', 'new_v_g_b_out', 'new_v_w_out', 'new_v_g_x', 'new_v_g_mem', 'new_v_w_xq', 'new_v_w_xkv', 'new_v_w_xo', 'new_v_g_ffn2', 'new_v_w2_gate', 'new_v_w2_up', 'new_v_w2_down', 'new_v_g_final']
TWIN_LEAF_KINDS = {'loss': 'loss', 'grad_x': 'grad_x', 'grad_g_ffn1': 'grad_w', 'grad_w1_gate': 'grad_w', 'grad_w1_up': 'grad_w', 'grad_w1_down': 'grad_w', 'grad_g_mix': 'grad_w', 'grad_w_in': 'grad_w', 'grad_g_v': 'grad_w', 'grad_w_s': 'grad_w', 'grad_b_s': 'grad_w', 'grad_sinks': 'grad_w', 'grad_g_a_out': 'grad_w', 'grad_g_b_out': 'grad_w', 'grad_w_out': 'grad_w', 'grad_g_x': 'grad_w', 'grad_g_mem': 'grad_w', 'grad_w_xq': 'grad_w', 'grad_w_xkv': 'grad_w', 'grad_w_xo': 'grad_w', 'grad_g_ffn2': 'grad_w', 'grad_w2_gate': 'grad_w', 'grad_w2_up': 'grad_w', 'grad_w2_down': 'grad_w', 'grad_g_final': 'grad_w', 'delta_g_ffn1': 'delta_w', 'delta_w1_gate': 'delta_w', 'delta_w1_up': 'delta_w', 'delta_w1_down': 'delta_w', 'delta_g_mix': 'delta_w', 'delta_w_in': 'delta_w', 'delta_g_v': 'delta_w', 'delta_w_s': 'delta_w', 'delta_b_s': 'delta_w', 'delta_sinks': 'delta_w', 'delta_g_a_out': 'delta_w', 'delta_g_b_out': 'delta_w', 'delta_w_out': 'delta_w', 'delta_g_x': 'delta_w', 'delta_g_mem': 'delta_w', 'delta_w_xq': 'delta_w', 'delta_w_xkv': 'delta_w', 'delta_w_xo': 'delta_w', 'delta_g_ffn2': 'delta_w', 'delta_w2_gate': 'delta_w', 'delta_w2_up': 'delta_w', 'delta_w2_down': 'delta_w', 'delta_g_final': 'delta_w', 'new_m_g_ffn1': 'new_m', 'new_m_w1_gate': 'new_m', 'new_m_w1_up': 'new_m', 'new_m_w1_down': 'new_m', 'new_m_g_mix': 'new_m', 'new_m_w_in': 'new_m', 'new_m_g_v': 'new_m', 'new_m_w_s': 'new_m', 'new_m_b_s': 'new_m', 'new_m_sinks': 'new_m', 'new_m_g_a_out': 'new_m', 'new_m_g_b_out': 'new_m', 'new_m_w_out': 'new_m', 'new_m_g_x': 'new_m', 'new_m_g_mem': 'new_m', 'new_m_w_xq': 'new_m', 'new_m_w_xkv': 'new_m', 'new_m_w_xo': 'new_m', 'new_m_g_ffn2': 'new_m', 'new_m_w2_gate': 'new_m', 'new_m_w2_up': 'new_m', 'new_m_w2_down': 'new_m', 'new_m_g_final': 'new_m', 'new_v_g_ffn1': 'new_v', 'new_v_w1_gate': 'new_v', 'new_v_w1_up': 'new_v', 'new_v_w1_down': 'new_v', 'new_v_g_mix': 'new_v', 'new_v_w_in': 'new_v', 'new_v_g_v': 'new_v', 'new_v_w_s': 'new_v', 'new_v_b_s': 'new_v', 'new_v_sinks': 'new_v', 'new_v_g_a_out': 'new_v', 'new_v_g_b_out': 'new_v', 'new_v_w_out': 'new_v', 'new_v_g_x': 'new_v', 'new_v_g_mem': 'new_v', 'new_v_w_xq': 'new_v', 'new_v_w_xkv': 'new_v', 'new_v_w_xo': 'new_v', 'new_v_g_ffn2': 'new_v', 'new_v_w2_gate': 'new_v', 'new_v_w2_up': 'new_v', 'new_v_w2_down': 'new_v', 'new_v_g_final': 'new_v'}


def _forward(args):
    return _fwd_reference(*[args[k] for k in FWD_PARAMS])


def _output_shape():
    def fwd():
        inp = _fwd_setup_inputs(0)
        return _fwd_reference(*[inp[k] for k in FWD_PARAMS])
    out = _jax.eval_shape(fwd)
    return out.shape, out.dtype

N_MICROBATCH = 1
ADAM_LR = 0.001
ADAM_B1 = 0.9
ADAM_B2 = 0.999
ADAM_EPS = 1e-08
ADAM_WD = 0.01
ADAM_STEP = 10
PER_EXAMPLE_BATCH_AXIS = {'x': 0, 'mem': 0, 'loss_target': 0}
SHARED_INPUTS = []
_WEIGHT_DTYPES = {'g_ffn1': _jnp.float32, 'w1_gate': _jnp.float32, 'w1_up': _jnp.float32, 'w1_down': _jnp.float32, 'g_mix': _jnp.float32, 'w_in': _jnp.float32, 'g_v': _jnp.float32, 'w_s': _jnp.float32, 'b_s': _jnp.float32, 'sinks': _jnp.float32, 'g_a_out': _jnp.float32, 'g_b_out': _jnp.float32, 'w_out': _jnp.float32, 'g_x': _jnp.float32, 'g_mem': _jnp.float32, 'w_xq': _jnp.float32, 'w_xkv': _jnp.float32, 'w_xo': _jnp.float32, 'g_ffn2': _jnp.float32, 'w2_gate': _jnp.float32, 'w2_up': _jnp.float32, 'w2_down': _jnp.float32, 'g_final': _jnp.float32}
MOMENT_SCALE = {'g_ffn1': 6.730165e-02, 'w1_gate': 2.860451e-02, 'w1_up': 2.773546e-02, 'w1_down': 4.608579e-02, 'g_mix': 1.308614e-01, 'w_in': 1.020306e-01, 'g_v': 3.020497e-02, 'w_s': 5.773288e-02, 'b_s': 8.032191e-02, 'sinks': 3.716586e-02, 'g_a_out': 1.529143e-01, 'g_b_out': 9.680051e-02, 'w_out': 1.167271e-01, 'g_x': 9.328161e-03, 'g_mem': 1.559072e-02, 'w_xq': 9.305126e-03, 'w_xkv': 1.018023e-02, 'w_xo': 1.098921e-02, 'g_ffn2': 3.983076e-02, 'w2_gate': 1.612785e-02, 'w2_up': 1.600567e-02, 'w2_down': 2.653863e-02, 'g_final': 3.208368e+01}


def _to_microbatches(a, axis):
    t = _jnp.moveaxis(a, axis, 0)
    t = t.reshape((N_MICROBATCH, t.shape[0] // N_MICROBATCH) + t.shape[1:])
    return _jnp.moveaxis(t, 1, axis + 1)


def setup_inputs(seed: int = 0) -> dict:
    inp = _fwd_setup_inputs(seed)
    key = _jax.random.fold_in(_jax.random.key(seed), 7919)
    shape, _ = _output_shape()
    out = dict(inp)
    out["loss_target"] = _jax.random.normal(_jax.random.fold_in(key, 0), shape, _jnp.float32)
    for i, name in enumerate(TWIN_WEIGHTS):
        w = inp[name].astype(_jnp.float32)
        if MOMENT_SCALE is None:
            s = _jnp.sqrt(_jnp.mean(_jnp.square(w)) + 1e-30)
        else:
            s = MOMENT_SCALE[name]
        km, kv = _jax.random.split(_jax.random.fold_in(key, i + 1))
        out[name] = w
        out["m_" + name] = s * _jax.random.normal(km, w.shape, _jnp.float32)
        out["v_" + name] = (s * s) * _jax.random.uniform(kv, w.shape, _jnp.float32, 0.5, 1.5)
    if N_MICROBATCH > 1:
        for name, axis in PER_EXAMPLE_BATCH_AXIS.items():
            out[name] = _to_microbatches(out[name], axis)
    return {'x': out['x'], 'mem': out['mem'], 'g_ffn1': out['g_ffn1'], 'w1_gate': out['w1_gate'], 'w1_up': out['w1_up'], 'w1_down': out['w1_down'], 'g_mix': out['g_mix'], 'w_in': out['w_in'], 'g_v': out['g_v'], 'w_s': out['w_s'], 'b_s': out['b_s'], 'sinks': out['sinks'], 'g_a_out': out['g_a_out'], 'g_b_out': out['g_b_out'], 'w_out': out['w_out'], 'g_x': out['g_x'], 'g_mem': out['g_mem'], 'w_xq': out['w_xq'], 'w_xkv': out['w_xkv'], 'w_xo': out['w_xo'], 'g_ffn2': out['g_ffn2'], 'w2_gate': out['w2_gate'], 'w2_up': out['w2_up'], 'w2_down': out['w2_down'], 'g_final': out['g_final'], 'loss_target': out['loss_target'], 'm_g_ffn1': out['m_g_ffn1'], 'm_w1_gate': out['m_w1_gate'], 'm_w1_up': out['m_w1_up'], 'm_w1_down': out['m_w1_down'], 'm_g_mix': out['m_g_mix'], 'm_w_in': out['m_w_in'], 'm_g_v': out['m_g_v'], 'm_w_s': out['m_w_s'], 'm_b_s': out['m_b_s'], 'm_sinks': out['m_sinks'], 'm_g_a_out': out['m_g_a_out'], 'm_g_b_out': out['m_g_b_out'], 'm_w_out': out['m_w_out'], 'm_g_x': out['m_g_x'], 'm_g_mem': out['m_g_mem'], 'm_w_xq': out['m_w_xq'], 'm_w_xkv': out['m_w_xkv'], 'm_w_xo': out['m_w_xo'], 'm_g_ffn2': out['m_g_ffn2'], 'm_w2_gate': out['m_w2_gate'], 'm_w2_up': out['m_w2_up'], 'm_w2_down': out['m_w2_down'], 'm_g_final': out['m_g_final'], 'v_g_ffn1': out['v_g_ffn1'], 'v_w1_gate': out['v_w1_gate'], 'v_w1_up': out['v_w1_up'], 'v_w1_down': out['v_w1_down'], 'v_g_mix': out['v_g_mix'], 'v_w_in': out['v_w_in'], 'v_g_v': out['v_g_v'], 'v_w_s': out['v_w_s'], 'v_b_s': out['v_b_s'], 'v_sinks': out['v_sinks'], 'v_g_a_out': out['v_g_a_out'], 'v_g_b_out': out['v_g_b_out'], 'v_w_out': out['v_w_out'], 'v_g_x': out['v_g_x'], 'v_g_mem': out['v_g_mem'], 'v_w_xq': out['v_w_xq'], 'v_w_xkv': out['v_w_xkv'], 'v_w_xo': out['v_w_xo'], 'v_g_ffn2': out['v_g_ffn2'], 'v_w2_gate': out['v_w2_gate'], 'v_w2_up': out['v_w2_up'], 'v_w2_down': out['v_w2_down'], 'v_g_final': out['v_g_final']}


def _loss(weights, diff, rest, loss_target):
    with _jax.named_scope("forward"):
        args = {**rest, TWIN_DIFF_INPUT: diff, **{k: w.astype(_WEIGHT_DTYPES[k]) for k, w in weights.items()}}
        y = _forward(args)
    with _jax.named_scope("loss_head"):
        err = _jnp.square(y.astype(_jnp.float32) - loss_target)
        return 0.5 * _jnp.sum(_jnp.mean(err, axis=-1)) if err.ndim else 0.5 * err


def _adamw(w, g, m, v):
    m = ADAM_B1 * m + (1.0 - ADAM_B1) * g
    v = ADAM_B2 * v + (1.0 - ADAM_B2) * _jnp.square(g)
    m_hat = m / (1.0 - ADAM_B1 ** ADAM_STEP)
    v_hat = v / (1.0 - ADAM_B2 ** ADAM_STEP)
    delta = -ADAM_LR * (m_hat / (_jnp.sqrt(v_hat) + ADAM_EPS) + ADAM_WD * w)
    return delta, m, v


def reference(x, mem, g_ffn1, w1_gate, w1_up, w1_down, g_mix, w_in, g_v, w_s, b_s, sinks, g_a_out, g_b_out, w_out, g_x, g_mem, w_xq, w_xkv, w_xo, g_ffn2, w2_gate, w2_up, w2_down, g_final, loss_target, m_g_ffn1, m_w1_gate, m_w1_up, m_w1_down, m_g_mix, m_w_in, m_g_v, m_w_s, m_b_s, m_sinks, m_g_a_out, m_g_b_out, m_w_out, m_g_x, m_g_mem, m_w_xq, m_w_xkv, m_w_xo, m_g_ffn2, m_w2_gate, m_w2_up, m_w2_down, m_g_final, v_g_ffn1, v_w1_gate, v_w1_up, v_w1_down, v_g_mix, v_w_in, v_g_v, v_w_s, v_b_s, v_sinks, v_g_a_out, v_g_b_out, v_w_out, v_g_x, v_g_mem, v_w_xq, v_w_xkv, v_w_xo, v_g_ffn2, v_w2_gate, v_w2_up, v_w2_down, v_g_final):
    given = dict(x=x, mem=mem, g_ffn1=g_ffn1, w1_gate=w1_gate, w1_up=w1_up, w1_down=w1_down, g_mix=g_mix, w_in=w_in, g_v=g_v, w_s=w_s, b_s=b_s, sinks=sinks, g_a_out=g_a_out, g_b_out=g_b_out, w_out=w_out, g_x=g_x, g_mem=g_mem, w_xq=w_xq, w_xkv=w_xkv, w_xo=w_xo, g_ffn2=g_ffn2, w2_gate=w2_gate, w2_up=w2_up, w2_down=w2_down, g_final=g_final, loss_target=loss_target, m_g_ffn1=m_g_ffn1, m_w1_gate=m_w1_gate, m_w1_up=m_w1_up, m_w1_down=m_w1_down, m_g_mix=m_g_mix, m_w_in=m_w_in, m_g_v=m_g_v, m_w_s=m_w_s, m_b_s=m_b_s, m_sinks=m_sinks, m_g_a_out=m_g_a_out, m_g_b_out=m_g_b_out, m_w_out=m_w_out, m_g_x=m_g_x, m_g_mem=m_g_mem, m_w_xq=m_w_xq, m_w_xkv=m_w_xkv, m_w_xo=m_w_xo, m_g_ffn2=m_g_ffn2, m_w2_gate=m_w2_gate, m_w2_up=m_w2_up, m_w2_down=m_w2_down, m_g_final=m_g_final, v_g_ffn1=v_g_ffn1, v_w1_gate=v_w1_gate, v_w1_up=v_w1_up, v_w1_down=v_w1_down, v_g_mix=v_g_mix, v_w_in=v_w_in, v_g_v=v_g_v, v_w_s=v_w_s, v_b_s=v_b_s, v_sinks=v_sinks, v_g_a_out=v_g_a_out, v_g_b_out=v_g_b_out, v_w_out=v_w_out, v_g_x=v_g_x, v_g_mem=v_g_mem, v_w_xq=v_w_xq, v_w_xkv=v_w_xkv, v_w_xo=v_w_xo, v_g_ffn2=v_g_ffn2, v_w2_gate=v_w2_gate, v_w2_up=v_w2_up, v_w2_down=v_w2_down, v_g_final=v_g_final)
    weights = {n: given[n] for n in TWIN_WEIGHTS}
    shared = {n: given[n] for n in SHARED_INPUTS}
    per_example = {n: given[n] for n in ['x', 'mem']}
    grad_fn = _jax.value_and_grad(_loss, argnums=(0, 1))

    def one_microbatch(ex, loss_target):
        ex = dict(ex)
        diff = ex.pop(TWIN_DIFF_INPUT)
        return grad_fn(weights, diff, {**shared, **ex}, loss_target)

    if N_MICROBATCH == 1:
        loss, (grad_w, grad_x) = one_microbatch(per_example, given["loss_target"])
    else:
        def body(carry, xs):
            loss_sum, grad_sum = carry
            l_k, (gw_k, gx_k) = one_microbatch(xs[0], xs[1])
            with _jax.named_scope("update"):
                return (loss_sum + l_k, _jax.tree.map(_jnp.add, grad_sum, gw_k)), gx_k

        init = (_jnp.zeros((), _jnp.float32), _jax.tree.map(_jnp.zeros_like, weights))
        (loss, grad_w), grad_x = _jax.lax.scan(body, init, (per_example, given["loss_target"]))
    with _jax.named_scope("update"):
        delta_w, new_m, new_v = {}, {}, {}
        for n in TWIN_WEIGHTS:
            delta_w[n], new_m[n], new_v[n] = _adamw(weights[n], grad_w[n], given["m_" + n], given["v_" + n])
    return (loss, grad_x, *[grad_w[n] for n in TWIN_WEIGHTS], *[delta_w[n] for n in TWIN_WEIGHTS],
            *[new_m[n] for n in TWIN_WEIGHTS], *[new_v[n] for n in TWIN_WEIGHTS])
```

```python
import functools
import math

import jax
import jax.numpy as jnp
from jax import lax
from jax.experimental import pallas as pl
from jax.experimental.pallas import tpu as pltpu

F32 = jnp.float32
BF16 = jnp.bfloat16
MESH = pl.DeviceIdType.MESH

D_MODEL = 2048
D_FF = 5632
D_A = 1024
D_B = 1024
CHUNK = 128
A_GROUPS = 8
HEAD_DIM = 64
B_Q_HEADS = 16
B_KV_HEADS = 2
X_HEADS = 4
X_HEAD_DIM = 512
IN_COLS = 3328
O_Q = 2 * D_A
O_K = O_Q + D_B
O_V = O_K + B_KV_HEADS * HEAD_DIM
N_CHIPS = 4
EPS = 1e-5
NEG = -1e30
ADAM_LR = 0.001
ADAM_B1 = 0.9
ADAM_B2 = 0.999
ADAM_EPS = 1e-08
ADAM_WD = 0.01
ADAM_STEP = 10

V7X_VMEM_BYTES = 64 * 1024 * 1024
VMEM_LIMIT = 56 * 1024 * 1024
LANES = 128


def _params(sem, vmem=VMEM_LIMIT):
    return pltpu.CompilerParams(dimension_semantics=sem, vmem_limit_bytes=vmem)


def _matmul(pairs, *, M, N, K, tm, tn, tk, a_t=False, b_kind="n", out_kind="n", out_dtype=F32,
            scale=1.0, res=None, order="ij", name):
    tm, tn, tk = min(tm, M), min(tn, N), min(tk, K)
    assert M % tm == 0 and N % tn == 0 and K % tk == 0, (name, M, N, K, tm, tn, tk)
    nk = K // tk
    npairs = len(pairs)
    b_t = b_kind in ("t", "st")

    def ij(g0, g1):
        return (g0, g1) if order == "ij" else (g1, g0)

    def a_map(g0, g1, k):
        i, _ = ij(g0, g1)
        return (k, i) if a_t else (i, k)

    a_spec = pl.BlockSpec((tk, tm) if a_t else (tm, tk), a_map)

    b0 = pairs[0][1]
    if b_kind == "n":
        b_spec = pl.BlockSpec((tk, tn), lambda g0, g1, k: (k, ij(g0, g1)[1]))
    elif b_kind == "t":
        b_spec = pl.BlockSpec((tn, tk), lambda g0, g1, k: (ij(g0, g1)[1], k))
    elif b_kind == "sn":
        ns = b0.shape[2]
        assert ns % tn == 0
        nps = ns // tn
        b_spec = pl.BlockSpec((None, tk, tn), lambda g0, g1, k: (ij(g0, g1)[1] // nps, k, ij(g0, g1)[1] % nps))
    else:
        ks = b0.shape[2]
        assert ks % tk == 0
        kps = ks // tk
        b_spec = pl.BlockSpec((None, tn, tk), lambda g0, g1, k: (k // kps, ij(g0, g1)[1], k % kps))

    if out_kind == "n":
        o_spec = pl.BlockSpec((tm, tn), lambda g0, g1, k: ij(g0, g1))
        o_shape = jax.ShapeDtypeStruct((M, N), out_dtype)
    else:
        ns = N // N_CHIPS
        assert ns % tn == 0
        nps_o = ns // tn
        o_spec = pl.BlockSpec((None, tm, tn), lambda g0, g1, k: (ij(g0, g1)[1] // nps_o, ij(g0, g1)[0], ij(g0, g1)[1] % nps_o))
        o_shape = jax.ShapeDtypeStruct((N_CHIPS, M, ns), out_dtype)

    in_specs, args = [], []
    for a, b in pairs:
        in_specs += [a_spec, b_spec]
        args += [a, b]
    if res is not None:
        in_specs.append(pl.BlockSpec((tm, tn), lambda g0, g1, k: ij(g0, g1)))
        args.append(res)

    dn = (((0,) if a_t else (1,), (1,) if b_t else (0,)), ((), ()))

    def body(*refs):
        pos = 2 * npairs
        res_ref = refs[pos] if res is not None else None
        pos += res is not None
        o_ref = refs[pos]
        acc_ref = refs[pos + 1] if nk > 1 else None
        part = None
        for p in range(npairs):
            d = lax.dot_general(refs[2 * p][...], refs[2 * p + 1][...], dn, preferred_element_type=F32)
            part = d if part is None else part + d

        def finish(acc):
            r = acc * scale if scale != 1.0 else acc
            if res_ref is not None:
                r = res_ref[...] + r
            o_ref[...] = r.astype(out_dtype)

        if nk == 1:
            finish(part)
        else:
            k = pl.program_id(2)

            @pl.when(k == 0)
            def _():
                acc_ref[...] = part

            @pl.when(k > 0)
            def _():
                acc_ref[...] += part

            @pl.when(k == nk - 1)
            def _():
                finish(acc_ref[...])

    grid = (M // tm, N // tn, nk) if order == "ij" else (N // tn, M // tm, nk)
    return pl.pallas_call(
        body, name=name, grid=grid, in_specs=in_specs, out_specs=o_spec, out_shape=o_shape,
        scratch_shapes=[pltpu.VMEM((tm, tn), F32)] if nk > 1 else [],
        compiler_params=_params(("parallel", "parallel", "arbitrary")),
    )(*args)


def _rstd(x):
    return lax.rsqrt(jnp.mean(x * x, axis=-1, keepdims=True) + EPS)


def _rms_bwd_math(x, g, dy):
    r = _rstd(x)
    gy = dy * g
    xr = x * r
    dx = r * (gy - xr * jnp.mean(gy * xr, axis=-1, keepdims=True))
    return dx, dy * xr


def _rms_fwd(h, g, *, name, tm=512):
    T, Dm = h.shape
    tm = min(tm, T)

    def body(h_ref, g_ref, o_ref):
        x = h_ref[...]
        o_ref[...] = (x * _rstd(x) * g_ref[...]).astype(BF16)

    return pl.pallas_call(
        body, name=name, grid=(T // tm,),
        in_specs=[pl.BlockSpec((tm, Dm), lambda i: (i, 0)), pl.BlockSpec((1, Dm), lambda i: (0, 0))],
        out_specs=pl.BlockSpec((tm, Dm), lambda i: (i, 0)),
        out_shape=jax.ShapeDtypeStruct((T, Dm), BF16),
        compiler_params=_params(("parallel",)),
    )(h, g)


def _rms_bwd(h, g, dn, dres, *, name, tm=256):
    T, Dm = h.shape
    tm = min(tm, T)
    has_res = dres is not None

    def body(*refs):
        h_ref, g_ref, dn_ref = refs[:3]
        pos = 3
        dres_ref = refs[pos] if has_res else None
        pos += has_res
        dh_ref, dhb_ref, dg_ref = refs[pos:pos + 3]
        dx, dgr = _rms_bwd_math(h_ref[...], g_ref[...], dn_ref[...].astype(F32))
        if has_res:
            dx = dres_ref[...] + dx
        dh_ref[...] = dx
        dhb_ref[...] = dx.astype(BF16)
        part = jnp.sum(dgr, axis=0, keepdims=True)

        @pl.when(pl.program_id(0) == 0)
        def _():
            dg_ref[...] = part

        @pl.when(pl.program_id(0) > 0)
        def _():
            dg_ref[...] += part

    row = pl.BlockSpec((tm, Dm), lambda i: (i, 0))
    vec = pl.BlockSpec((1, Dm), lambda i: (0, 0))
    args = [h, g, dn] + ([dres] if has_res else [])
    return pl.pallas_call(
        body, name=name, grid=(T // tm,),
        in_specs=[row, vec, row] + ([row] if has_res else []),
        out_specs=[row, row, vec],
        out_shape=[jax.ShapeDtypeStruct((T, Dm), F32), jax.ShapeDtypeStruct((T, Dm), BF16),
                   jax.ShapeDtypeStruct((1, Dm), F32)],
        compiler_params=_params(("arbitrary",)),
    )(*args)


def _loss_head(h, g, tgt, *, name, tm=256):
    T, Dm = h.shape
    tm = min(tm, T)

    def body(h_ref, g_ref, t_ref, dh_ref, dhb_ref, dg_ref, loss_ref):
        x = h_ref[...]
        gv = g_ref[...]
        r = _rstd(x)
        diff = x * r * gv - t_ref[...]
        lpart = 0.5 * jnp.sum(jnp.mean(diff * diff, axis=-1, keepdims=True), axis=0, keepdims=True)
        dx, dgr = _rms_bwd_math(x, gv, diff * (1.0 / Dm))
        dh_ref[...] = dx
        dhb_ref[...] = dx.astype(BF16)
        part = jnp.sum(dgr, axis=0, keepdims=True)
        lrow = jnp.broadcast_to(lpart, (1, LANES))

        @pl.when(pl.program_id(0) == 0)
        def _():
            dg_ref[...] = part
            loss_ref[...] = lrow

        @pl.when(pl.program_id(0) > 0)
        def _():
            dg_ref[...] += part
            loss_ref[...] += lrow

    row = pl.BlockSpec((tm, Dm), lambda i: (i, 0))
    vec = pl.BlockSpec((1, Dm), lambda i: (0, 0))
    return pl.pallas_call(
        body, name=name, grid=(T // tm,),
        in_specs=[row, vec, row],
        out_specs=[row, row, vec, pl.BlockSpec((1, LANES), lambda i: (0, 0))],
        out_shape=[jax.ShapeDtypeStruct((T, Dm), F32), jax.ShapeDtypeStruct((T, Dm), BF16),
                   jax.ShapeDtypeStruct((1, Dm), F32), jax.ShapeDtypeStruct((1, LANES), F32)],
        compiler_params=_params(("arbitrary",)),
    )(h, g, tgt)


def _swiglu_up(n, wg, wu, *, name, tm=512):
    T, Dm = n.shape
    S, _, fs = wg.shape
    tm = min(tm, T)

    def body(n_ref, wg_ref, wu_ref, g_ref, u_ref, a_ref):
        x = n_ref[...]
        g = jnp.dot(x, wg_ref[...], preferred_element_type=F32)
        u = jnp.dot(x, wu_ref[...], preferred_element_type=F32)
        g_ref[...] = g.astype(BF16)
        u_ref[...] = u.astype(BF16)
        a_ref[...] = (g * jax.nn.sigmoid(g) * u).astype(BF16)

    wspec = pl.BlockSpec((None, Dm, fs), lambda j, i: (j, 0, 0))
    ospec = pl.BlockSpec((tm, fs), lambda j, i: (i, j))
    oshape = jax.ShapeDtypeStruct((T, S * fs), BF16)
    return pl.pallas_call(
        body, name=name, grid=(S, T // tm),
        in_specs=[pl.BlockSpec((tm, Dm), lambda j, i: (i, 0)), wspec, wspec],
        out_specs=[ospec, ospec, ospec], out_shape=[oshape, oshape, oshape],
        compiler_params=_params(("parallel", "parallel")),
    )(n, wg, wu)


def _swiglu_bwd_act(dhb, wd, G, U, *, name, tm=512, tn=1408):
    T, Dm = dhb.shape
    Fd = wd.shape[0]
    tm, tn = min(tm, T), min(tn, Fd)

    def body(dh_ref, wd_ref, g_ref, u_ref, dg_ref, du_ref):
        da = 0.5 * lax.dot_general(dh_ref[...], wd_ref[...], (((1,), (1,)), ((), ())), preferred_element_type=F32)
        g = g_ref[...].astype(F32)
        u = u_ref[...].astype(F32)
        sg = jax.nn.sigmoid(g)
        dg_ref[...] = (da * u * (sg * (1.0 + g * (1.0 - sg)))).astype(BF16)
        du_ref[...] = (da * (g * sg)).astype(BF16)

    blk = pl.BlockSpec((tm, tn), lambda j, i: (i, j))
    oshape = jax.ShapeDtypeStruct((T, Fd), BF16)
    return pl.pallas_call(
        body, name=name, grid=(Fd // tn, T // tm),
        in_specs=[pl.BlockSpec((tm, Dm), lambda j, i: (i, 0)), pl.BlockSpec((tn, Dm), lambda j, i: (j, 0)), blk, blk],
        out_specs=[blk, blk], out_shape=[oshape, oshape],
        compiler_params=_params(("parallel", "parallel")),
    )(dhb, wd, G, U)


_INV_SQRT2 = 0.7071067811865476
_INV_SQRT2PI = 0.3989422804014327


def _erf(x):
    ax = jnp.abs(x)
    t = 1.0 / (1.0 + 0.3275911 * ax)
    poly = t * (0.254829592 + t * (-0.284496736 + t * (1.421413741 + t * (-1.453152027 + t * 1.061405429))))
    y = 1.0 - poly * jnp.exp(-ax * ax)
    return jnp.where(x < 0, -y, y)


def _gelu_cdf(x):
    return 0.5 * (1.0 + _erf(x * _INV_SQRT2))


def _lane_lt64(shape):
    return lax.broadcasted_iota(jnp.int32, shape, len(shape) - 1) < HEAD_DIM


def _dup_half(x, kv):
    rolled = pltpu.roll(x, HEAD_DIM, 1)
    lo = _lane_lt64(x.shape)
    return jnp.where(lo, x, rolled) if kv == 0 else jnp.where(lo, rolled, x)


def _attn_mask(block):
    qpos = lax.broadcasted_iota(jnp.int32, (CHUNK, 2 * CHUNK), 0) + CHUNK
    kpos = lax.broadcasted_iota(jnp.int32, (CHUNK, 2 * CHUNK), 1)
    diff = qpos - kpos
    first_key = jnp.where(block == 0, CHUNK, 0)
    return (diff >= 0) & (diff < CHUNK) & (kpos >= first_key)


def _sgu_forward(z_ref, gv, wsm, bst):
    zu = z_ref[:, 0:D_A]
    zv = z_ref[:, D_A:2 * D_A]
    u = zu * _gelu_cdf(zu)
    v = zv * _gelu_cdf(zv)
    rv = _rstd(v)
    vn = (v * rv * gv).astype(BF16)
    svs = []
    for g in range(A_GROUPS):
        sl = slice(g * CHUNK, (g + 1) * CHUNK)
        svs.append(jnp.dot(wsm[g], vn[:, sl], preferred_element_type=F32) + bst[:, g:g + 1])
    sv = jnp.concatenate(svs, axis=1)
    return zu, zv, u, v, rv, vn, sv


def _masked_ws(ws_ref):
    tril = lax.broadcasted_iota(jnp.int32, (CHUNK, CHUNK), 0) >= lax.broadcasted_iota(jnp.int32, (CHUNK, CHUNK), 1)
    return [jnp.where(tril, ws_ref[g], 0.0).astype(BF16) for g in range(A_GROUPS)], tril


def _attn_probs(qm, kkd, sink, mask):
    s = lax.dot_general(qm, kkd, (((1,), (1,)), ((), ())), preferred_element_type=F32) * (HEAD_DIM ** -0.5)
    s = jnp.where(mask, s, NEG)
    m = jnp.maximum(jnp.max(s, axis=-1, keepdims=True), sink)
    e = jnp.exp(s - m)
    es = jnp.exp(sink - m)
    inv = 1.0 / (jnp.sum(e, axis=-1, keepdims=True) + es)
    return e * inv, es * inv


def _mixer_fwd(z, gv, ws, bst, sinks, ga, gb, *, name):
    T = z.shape[0]
    nb = T // CHUNK
    kvb = O_K // (2 * CHUNK)

    def body(z_ref, zp_ref, gv_ref, ws_ref, bst_ref, sk_ref, ga_ref, gb_ref, o_ref):
        i = pl.program_id(0)
        wsm, _ = _masked_ws(ws_ref)
        _, _, u, _, _, _, sv = _sgu_forward(z_ref, gv_ref[...], wsm, bst_ref[...])
        ya = u * sv
        o_ref[:, 0:D_A] = (ya * _rstd(ya) * ga_ref[...]).astype(BF16)

        mask = _attn_mask(i)
        kk = jnp.concatenate([zp_ref[:, 0:CHUNK], z_ref[:, O_K:O_V]], axis=0)
        vv = jnp.concatenate([zp_ref[:, CHUNK:2 * CHUNK], z_ref[:, O_V:IN_COLS]], axis=0)
        lo = _lane_lt64((CHUNK, LANES))
        outs = []
        for kv in range(B_KV_HEADS):
            kkd = _dup_half(kk, kv).astype(BF16)
            vvd = _dup_half(vv, kv).astype(BF16)
            for pr in range(B_Q_HEADS // B_KV_HEADS // 2):
                c0 = O_Q + (kv * 4 + pr) * LANES
                qp = z_ref[:, c0:c0 + LANES]
                halves = []
                for hh in range(2):
                    h = (kv * 4 + pr) * 2 + hh
                    qm = jnp.where(lo if hh == 0 else jnp.logical_not(lo), qp, 0.0).astype(BF16)
                    p, _ = _attn_probs(qm, kkd, sk_ref[:, h:h + 1], mask)
                    halves.append(jnp.dot(p.astype(BF16), vvd, preferred_element_type=F32))
                outs.append(jnp.where(lo, halves[0], halves[1]))
        yb = jnp.concatenate(outs, axis=1)
        o_ref[:, D_A:D_A + D_B] = (yb * _rstd(yb) * gb_ref[...]).astype(BF16)

    full = lambda shape: pl.BlockSpec(shape, lambda i: (0,) * len(shape))
    return pl.pallas_call(
        body, name=name, grid=(nb,),
        in_specs=[pl.BlockSpec((CHUNK, IN_COLS), lambda i: (i, 0)),
                  pl.BlockSpec((CHUNK, 2 * CHUNK), lambda i: (jnp.maximum(i - 1, 0), kvb)),
                  full((1, D_A)), full((A_GROUPS, CHUNK, CHUNK)), full((CHUNK, A_GROUPS)), full((1, B_Q_HEADS)),
                  full((1, D_A)), full((1, D_B))],
        out_specs=pl.BlockSpec((CHUNK, D_A + D_B), lambda i: (i, 0)),
        out_shape=jax.ShapeDtypeStruct((T, D_A + D_B), BF16),
        compiler_params=_params(("parallel",)),
    )(z, z, gv, ws, bst, sinks, ga, gb)


def _mixer_bwd(z, dyn, gv, ws, bst, sinks, ga, gb, *, name):
    T = z.shape[0]
    nb = T // CHUNK
    kvb = O_K // (2 * CHUNK)
    NT = (((0,), (0,)), ((), ()))

    def body(z_ref, zp_ref, dy_ref, gv_ref, ws_ref, bst_ref, sk_ref, ga_ref, gb_ref,
             dz_ref, dgv_ref, dws_ref, dbst_ref, dsk_ref, dga_ref, dgb_ref, carry_ref, p_ref):
        step = pl.program_id(0)
        i = nb - 1 - step

        @pl.when(step == 0)
        def _():
            carry_ref[...] = jnp.zeros_like(carry_ref)
            dgv_ref[...] = jnp.zeros_like(dgv_ref)
            dws_ref[...] = jnp.zeros_like(dws_ref)
            dbst_ref[...] = jnp.zeros_like(dbst_ref)
            dsk_ref[...] = jnp.zeros_like(dsk_ref)
            dga_ref[...] = jnp.zeros_like(dga_ref)
            dgb_ref[...] = jnp.zeros_like(dgb_ref)

        wsm, tril = _masked_ws(ws_ref)
        gvv = gv_ref[...]
        zu, zv, u, v, rv, vn, sv = _sgu_forward(z_ref, gvv, wsm, bst_ref[...])
        ya = u * sv
        dya, dga_rows = _rms_bwd_math(ya, ga_ref[...], dy_ref[:, 0:D_A].astype(F32))
        dga_ref[...] += jnp.sum(dga_rows, axis=0, keepdims=True)
        du = dya * sv
        dsv = dya * u
        dvn_parts = []
        for g in range(A_GROUPS):
            sl = slice(g * CHUNK, (g + 1) * CHUNK)
            dsv_g = dsv[:, sl]
            dsv_gb = dsv_g.astype(BF16)
            dw = lax.dot_general(dsv_gb, vn[:, sl], (((1,), (1,)), ((), ())), preferred_element_type=F32)
            dws_ref[g] += jnp.where(tril, dw, 0.0)
            dbst_ref[:, g:g + 1] += jnp.sum(dsv_g, axis=1, keepdims=True)
            dvn_parts.append(lax.dot_general(wsm[g], dsv_gb, NT, preferred_element_type=F32))
        dvn = jnp.concatenate(dvn_parts, axis=1)
        dv, dgv_rows = _rms_bwd_math(v, gvv, dvn)
        dgv_ref[...] += jnp.sum(dgv_rows, axis=0, keepdims=True)
        dz_ref[:, 0:D_A] = (du * (_gelu_cdf(zu) + zu * jnp.exp(-0.5 * zu * zu) * _INV_SQRT2PI)).astype(BF16)
        dz_ref[:, D_A:2 * D_A] = (dv * (_gelu_cdf(zv) + zv * jnp.exp(-0.5 * zv * zv) * _INV_SQRT2PI)).astype(BF16)

        mask = _attn_mask(i)
        kk = jnp.concatenate([zp_ref[:, 0:CHUNK], z_ref[:, O_K:O_V]], axis=0)
        vv = jnp.concatenate([zp_ref[:, CHUNK:2 * CHUNK], z_ref[:, O_V:IN_COLS]], axis=0)
        lo = _lane_lt64((CHUNK, LANES))
        kkd = [_dup_half(kk, kv).astype(BF16) for kv in range(B_KV_HEADS)]
        vvd = [_dup_half(vv, kv).astype(BF16) for kv in range(B_KV_HEADS)]
        outs, psinks = [], []
        for pr in range(B_Q_HEADS // 2):
            kv = pr // 4
            qp = z_ref[:, O_Q + pr * LANES:O_Q + (pr + 1) * LANES]
            halves = []
            for hh in range(2):
                h = 2 * pr + hh
                qm = jnp.where(lo if hh == 0 else jnp.logical_not(lo), qp, 0.0).astype(BF16)
                p, ps = _attn_probs(qm, kkd[kv], sk_ref[:, h:h + 1], mask)
                p_ref[h] = p
                psinks.append(ps)
                halves.append(jnp.dot(p.astype(BF16), vvd[kv], preferred_element_type=F32))
            outs.append(jnp.where(lo, halves[0], halves[1]))
        yb = jnp.concatenate(outs, axis=1)
        dyb, dgb_rows = _rms_bwd_math(yb, gb_ref[...], dy_ref[:, D_A:D_A + D_B].astype(F32))
        dgb_ref[...] += jnp.sum(dgb_rows, axis=0, keepdims=True)

        dkk = [jnp.zeros((2 * CHUNK, LANES), F32) for _ in range(B_KV_HEADS)]
        dvv = [jnp.zeros((2 * CHUNK, LANES), F32) for _ in range(B_KV_HEADS)]
        for pr in range(B_Q_HEADS // 2):
            kv = pr // 4
            qp = z_ref[:, O_Q + pr * LANES:O_Q + (pr + 1) * LANES]
            dop = dyb[:, pr * LANES:(pr + 1) * LANES]
            dqh = []
            for hh in range(2):
                h = 2 * pr + hh
                sel = lo if hh == 0 else jnp.logical_not(lo)
                qm = jnp.where(sel, qp, 0.0).astype(BF16)
                dom = jnp.where(sel, dop, 0.0).astype(BF16)
                p = p_ref[h]
                pb = p.astype(BF16)
                dvv[kv] = dvv[kv] + lax.dot_general(pb, dom, NT, preferred_element_type=F32)
                dp = lax.dot_general(dom, vvd[kv], (((1,), (1,)), ((), ())), preferred_element_type=F32)
                delta = jnp.sum(p * dp, axis=-1, keepdims=True)
                dsk_ref[:, h:h + 1] += jnp.sum(-psinks[h] * delta, axis=0, keepdims=True)
                ds = (p * (dp - delta) * (HEAD_DIM ** -0.5)).astype(BF16)
                dqh.append(jnp.dot(ds, kkd[kv], preferred_element_type=F32))
                dkk[kv] = dkk[kv] + lax.dot_general(ds, qm, NT, preferred_element_type=F32)
            c0 = O_Q + pr * LANES
            dz_ref[:, c0:c0 + LANES] = jnp.where(lo, dqh[0], dqh[1]).astype(BF16)

        def fold(parts):
            tot = [t + pltpu.roll(t, HEAD_DIM, 1) for t in parts]
            return jnp.where(_lane_lt64(tot[0].shape), tot[0], tot[1])

        dk_all = fold(dkk)
        dv_all = fold(dvv)
        dz_ref[:, O_K:O_V] = (dk_all[CHUNK:] + carry_ref[:, 0:CHUNK]).astype(BF16)
        dz_ref[:, O_V:IN_COLS] = (dv_all[CHUNK:] + carry_ref[:, CHUNK:2 * CHUNK]).astype(BF16)
        carry_ref[:, 0:CHUNK] = dk_all[:CHUNK]
        carry_ref[:, CHUNK:2 * CHUNK] = dv_all[:CHUNK]

    full = lambda shape: pl.BlockSpec(shape, lambda s: (0,) * len(shape))
    rev = lambda s: nb - 1 - s
    return pl.pallas_call(
        body, name=name, grid=(nb,),
        in_specs=[pl.BlockSpec((CHUNK, IN_COLS), lambda s: (rev(s), 0)),
                  pl.BlockSpec((CHUNK, 2 * CHUNK), lambda s: (jnp.maximum(rev(s) - 1, 0), kvb)),
                  pl.BlockSpec((CHUNK, D_A + D_B), lambda s: (rev(s), 0)),
                  full((1, D_A)), full((A_GROUPS, CHUNK, CHUNK)), full((CHUNK, A_GROUPS)), full((1, B_Q_HEADS)),
                  full((1, D_A)), full((1, D_B))],
        out_specs=[pl.BlockSpec((CHUNK, IN_COLS), lambda s: (rev(s), 0)),
                   full((1, D_A)), full((A_GROUPS, CHUNK, CHUNK)), full((CHUNK, A_GROUPS)), full((1, B_Q_HEADS)),
                   full((1, D_A)), full((1, D_B))],
        out_shape=[jax.ShapeDtypeStruct((T, IN_COLS), BF16), jax.ShapeDtypeStruct((1, D_A), F32),
                   jax.ShapeDtypeStruct((A_GROUPS, CHUNK, CHUNK), F32), jax.ShapeDtypeStruct((CHUNK, A_GROUPS), F32),
                   jax.ShapeDtypeStruct((1, B_Q_HEADS), F32), jax.ShapeDtypeStruct((1, D_A), F32),
                   jax.ShapeDtypeStruct((1, D_B), F32)],
        scratch_shapes=[pltpu.VMEM((CHUNK, 2 * CHUNK), F32), pltpu.VMEM((B_Q_HEADS, CHUNK, 2 * CHUNK), F32)],
        compiler_params=_params(("arbitrary",)),
    )(z, z, dyn, gv, ws, bst, sinks, ga, gb)


def _xattn_probs(qh, kh):
    s = lax.dot_general(qh, kh, (((1,), (1,)), ((), ())), preferred_element_type=F32) * (X_HEAD_DIM ** -0.5)
    e = jnp.exp(s - jnp.max(s, axis=-1, keepdims=True))
    return e / jnp.sum(e, axis=-1, keepdims=True)


def _xattn_fwd(q, kvm, *, name, tm=512):
    T = q.shape[0]
    Mm = kvm.shape[0]
    tm = min(tm, T)

    def body(q_ref, kv_ref, o_ref):
        for h in range(X_HEADS):
            sl = slice(h * X_HEAD_DIM, (h + 1) * X_HEAD_DIM)
            kh = kv_ref[:, sl].astype(BF16)
            vh = kv_ref[:, D_MODEL + h * X_HEAD_DIM:D_MODEL + (h + 1) * X_HEAD_DIM].astype(BF16)
            p = _xattn_probs(q_ref[:, sl], kh)
            o_ref[:, sl] = jnp.dot(p.astype(BF16), vh, preferred_element_type=F32).astype(BF16)

    return pl.pallas_call(
        body, name=name, grid=(T // tm,),
        in_specs=[pl.BlockSpec((tm, D_MODEL), lambda i: (i, 0)), pl.BlockSpec((Mm, 2 * D_MODEL), lambda i: (0, 0))],
        out_specs=pl.BlockSpec((tm, D_MODEL), lambda i: (i, 0)),
        out_shape=jax.ShapeDtypeStruct((T, D_MODEL), BF16),
        compiler_params=_params(("parallel",)),
    )(q, kvm)


def _xattn_bwd(q, kvm, do, *, name, tm=512):
    T = q.shape[0]
    Mm = kvm.shape[0]
    tm = min(tm, T)
    NT = (((0,), (0,)), ((), ()))

    def body(q_ref, kv_ref, do_ref, dq_ref, dkv_ref):
        @pl.when(pl.program_id(0) == 0)
        def _():
            dkv_ref[...] = jnp.zeros_like(dkv_ref)

        for h in range(X_HEADS):
            sl = slice(h * X_HEAD_DIM, (h + 1) * X_HEAD_DIM)
            slv = slice(D_MODEL + h * X_HEAD_DIM, D_MODEL + (h + 1) * X_HEAD_DIM)
            kh = kv_ref[:, sl].astype(BF16)
            vh = kv_ref[:, slv].astype(BF16)
            qh = q_ref[:, sl]
            doh = do_ref[:, sl]
            p = _xattn_probs(qh, kh)
            dkv_ref[:, slv] += lax.dot_general(p.astype(BF16), doh, NT, preferred_element_type=F32)
            dp = lax.dot_general(doh, vh, (((1,), (1,)), ((), ())), preferred_element_type=F32)
            ds = (p * (dp - jnp.sum(p * dp, axis=-1, keepdims=True)) * (X_HEAD_DIM ** -0.5)).astype(BF16)
            dq_ref[:, sl] = jnp.dot(ds, kh, preferred_element_type=F32).astype(BF16)
            dkv_ref[:, sl] += lax.dot_general(ds, qh, NT, preferred_element_type=F32)

    row = pl.BlockSpec((tm, D_MODEL), lambda i: (i, 0))
    kvs = pl.BlockSpec((Mm, 2 * D_MODEL), lambda i: (0, 0))
    return pl.pallas_call(
        body, name=name, grid=(T // tm,),
        in_specs=[row, kvs, row], out_specs=[row, kvs],
        out_shape=[jax.ShapeDtypeStruct((T, D_MODEL), BF16), jax.ShapeDtypeStruct((Mm, 2 * D_MODEL), F32)],
        compiler_params=_params(("arbitrary",)),
    )(q, kvm, do)


def _swiglu_block_bwd(tag, hin, g_norm, n, G, U, A, wg, wu, wd, dh, dhb):
    T = hin.shape[0]
    dG, dU = _swiglu_bwd_act(dhb, wd, G, U, name=f"{tag}_bwd_act")
    dwd = _matmul([(A, dhb)], M=D_FF, N=D_MODEL, K=T, tm=1408, tn=1024, tk=512, a_t=True, out_dtype=BF16,
                  scale=0.5, name=f"{tag}_dwd")
    dwg = _matmul([(n, dG)], M=D_MODEL, N=D_FF, K=T, tm=1024, tn=1408, tk=512, a_t=True, out_kind="s",
                  out_dtype=BF16, order="ji", name=f"{tag}_dwg")
    dwu = _matmul([(n, dU)], M=D_MODEL, N=D_FF, K=T, tm=1024, tn=1408, tk=512, a_t=True, out_kind="s",
                  out_dtype=BF16, order="ji", name=f"{tag}_dwu")
    dn = _matmul([(dG, wg), (dU, wu)], M=T, N=D_MODEL, K=D_FF, tm=512, tn=D_MODEL, tk=1408, b_kind="st",
                 name=f"{tag}_dn")
    dhin, dhinb, dg = _rms_bwd(hin, g_norm, dn, dh, name=f"{tag}_norm_bwd")
    return dhin, dhinb, dg, dwg, dwu, dwd


def _local_step(x, mem, tgt, W):
    T = x.shape[0]
    Mm = mem.shape[0]
    mm = functools.partial(_matmul)

    n1 = _rms_fwd(x, W["g_ffn1"], name="f_norm1")
    G1, U1, A1 = _swiglu_up(n1, W["w1_gate"], W["w1_up"], name="f_ffn1_up")
    h1 = mm([(A1, W["w1_down"])], M=T, N=D_MODEL, K=D_FF, tm=512, tn=D_MODEL, tk=512, scale=0.5, res=x,
            name="f_ffn1_down")
    n2 = _rms_fwd(h1, W["g_mix"], name="f_norm2")
    z = mm([(n2, W["w_in"])], M=T, N=IN_COLS, K=D_MODEL, tm=512, tn=IN_COLS // 2, tk=D_MODEL, name="f_w_in")
    bst = jnp.transpose(W["b_s"])
    yn = _mixer_fwd(z, W["g_v"], W["w_s"], bst, W["sinks"], W["g_a_out"], W["g_b_out"], name="f_mixer")
    h2 = mm([(yn, W["w_out"])], M=T, N=D_MODEL, K=D_MODEL, tm=512, tn=D_MODEL, tk=D_MODEL, res=h1, name="f_w_out")
    n3 = _rms_fwd(h2, W["g_x"], name="f_norm3")
    memn = _rms_fwd(mem, W["g_mem"], name="f_norm_mem")
    q3 = mm([(n3, W["w_xq"])], M=T, N=D_MODEL, K=D_MODEL, tm=512, tn=D_MODEL, tk=D_MODEL, out_dtype=BF16,
            name="f_w_xq")
    kvm = mm([(memn, W["w_xkv"])], M=Mm, N=2 * D_MODEL, K=D_MODEL, tm=Mm, tn=1024, tk=D_MODEL, b_kind="sn",
             name="f_w_xkv")
    o3 = _xattn_fwd(q3, kvm, name="f_xattn")
    h3 = mm([(o3, W["w_xo"])], M=T, N=D_MODEL, K=D_MODEL, tm=512, tn=D_MODEL, tk=D_MODEL, res=h2, name="f_w_xo")
    n4 = _rms_fwd(h3, W["g_ffn2"], name="f_norm4")
    G2, U2, A2 = _swiglu_up(n4, W["w2_gate"], W["w2_up"], name="f_ffn2_up")
    h4 = mm([(A2, W["w2_down"])], M=T, N=D_MODEL, K=D_FF, tm=512, tn=D_MODEL, tk=512, scale=0.5, res=h3,
            name="f_ffn2_down")

    grads = {}
    dh4, dh4b, grads["g_final"], loss = _loss_head(h4, W["g_final"], tgt, name="loss_head")
    dh3, dh3b, grads["g_ffn2"], grads["w2_gate"], grads["w2_up"], grads["w2_down"] = _swiglu_block_bwd(
        "b_ffn2", h3, W["g_ffn2"], n4, G2, U2, A2, W["w2_gate"], W["w2_up"], W["w2_down"], dh4, dh4b)

    do3 = mm([(dh3b, W["w_xo"])], M=T, N=D_MODEL, K=D_MODEL, tm=512, tn=D_MODEL, tk=D_MODEL, b_kind="t",
             out_dtype=BF16, name="b_do3")
    grads["w_xo"] = mm([(o3, dh3b)], M=D_MODEL, N=D_MODEL, K=T, tm=1024, tn=D_MODEL, tk=512, a_t=True,
                       out_dtype=BF16, name="b_dw_xo")
    dq3, dkvm = _xattn_bwd(q3, kvm, do3, name="b_xattn")
    grads["w_xq"] = mm([(n3, dq3)], M=D_MODEL, N=D_MODEL, K=T, tm=1024, tn=D_MODEL, tk=512, a_t=True,
                       out_dtype=BF16, name="b_dw_xq")
    dn3 = mm([(dq3, W["w_xq"])], M=T, N=D_MODEL, K=D_MODEL, tm=512, tn=D_MODEL, tk=D_MODEL, b_kind="t",
             name="b_dn3")
    dh2, dh2b, grads["g_x"] = _rms_bwd(h2, W["g_x"], dn3, dh3, name="b_norm3")
    dkvmb = dkvm.astype(BF16)
    grads["w_xkv"] = mm([(memn, dkvmb)], M=D_MODEL, N=2 * D_MODEL, K=Mm, tm=D_MODEL, tn=1024, tk=Mm, a_t=True,
                        out_kind="s", out_dtype=BF16, name="b_dw_xkv")
    dmemn = mm([(dkvmb, W["w_xkv"])], M=Mm, N=D_MODEL, K=2 * D_MODEL, tm=Mm, tn=D_MODEL, tk=1024, b_kind="st",
               name="b_dmemn")
    _, _, grads["g_mem"] = _rms_bwd(mem, W["g_mem"], dmemn, None, name="b_norm_mem")

    dyn = mm([(dh2b, W["w_out"])], M=T, N=D_MODEL, K=D_MODEL, tm=512, tn=D_MODEL, tk=D_MODEL, b_kind="t",
             out_dtype=BF16, name="b_dyn")
    grads["w_out"] = mm([(yn, dh2b)], M=D_MODEL, N=D_MODEL, K=T, tm=1024, tn=D_MODEL, tk=512, a_t=True,
                        out_dtype=BF16, name="b_dw_out")
    dz, grads["g_v"], grads["w_s"], dbst, grads["sinks"], grads["g_a_out"], grads["g_b_out"] = _mixer_bwd(
        z, dyn, W["g_v"], W["w_s"], bst, W["sinks"], W["g_a_out"], W["g_b_out"], name="b_mixer")
    grads["b_s"] = jnp.transpose(dbst)
    grads["w_in"] = mm([(n2, dz)], M=D_MODEL, N=IN_COLS, K=T, tm=1024, tn=IN_COLS, tk=512, a_t=True,
                       out_dtype=BF16, name="b_dw_in")
    dn2 = mm([(dz, W["w_in"])], M=T, N=D_MODEL, K=IN_COLS, tm=512, tn=D_MODEL, tk=IN_COLS, b_kind="t",
             name="b_dn2")
    dh1, dh1b, grads["g_mix"] = _rms_bwd(h1, W["g_mix"], dn2, dh2, name="b_norm2")

    dx, _, grads["g_ffn1"], grads["w1_gate"], grads["w1_up"], grads["w1_down"] = _swiglu_block_bwd(
        "b_ffn1", x, W["g_ffn1"], n1, G1, U1, A1, W["w1_gate"], W["w1_up"], W["w1_down"], dh1, dh1b)
    return loss, dx, grads


BIG = ["w1_gate", "w1_up", "w1_down", "w_in", "w_out", "w_xq", "w_xkv", "w_xo", "w2_gate", "w2_up", "w2_down"]
SMALL = ["g_ffn1", "g_mix", "g_v", "w_s", "b_s", "sinks", "g_a_out", "g_b_out", "g_x", "g_mem", "g_ffn2", "g_final"]
ALL_W = ["g_ffn1", "w1_gate", "w1_up", "w1_down", "g_mix", "w_in", "g_v", "w_s", "b_s", "sinks", "g_a_out",
         "g_b_out", "w_out", "g_x", "g_mem", "w_xq", "w_xkv", "w_xo", "g_ffn2", "w2_gate", "w2_up", "w2_down",
         "g_final"]
ANY = pl.BlockSpec(memory_space=pl.ANY)


def _place():
    x, y, c = lax.axis_index("x"), lax.axis_index("y"), lax.axis_index("c")
    chips = [(1 - x, y), (x, 1 - y), (1 - x, 1 - y)]
    return x, y, c, chips


def _allgather_weights(shards, *, name):
    n = len(shards)

    def body(*refs):
        ins, outs = refs[:n], refs[n:2 * n]
        send, recv, loc = refs[2 * n:]
        x, y, c, chips = _place()
        me = 2 * x + y
        sib = (x, y, 1 - c)

        def half(w, slot, hc):
            h = shards[w].shape[0] // 2
            return outs[w].at[slot, pl.ds(hc * h, h), :]

        def copy(w, k, slot, hc, to, src=None):
            return pltpu.make_async_remote_copy(
                src_ref=half(w, slot, hc) if src is None else src, dst_ref=half(w, slot, hc),
                send_sem=send.at[6 * w + k], recv_sem=recv.at[6 * w + k], device_id=to, device_id_type=MESH)

        local = [pltpu.make_async_copy(ins[w], outs[w].at[me], loc.at[w]) for w in range(n)]
        for cp in local:
            cp.start()
        first = []
        for w in range(n):
            h = shards[w].shape[0] // 2
            for j, (tx, ty) in enumerate(chips):
                first.append(copy(w, j, me, c, (tx, ty, c), src=ins[w].at[pl.ds(c * h, h), :]))
                first[-1].start()
        passed = []
        for w in range(n):
            for j, (tx, ty) in enumerate(chips):
                slot = 2 * tx + ty
                copy(w, j, slot, c, (tx, ty, c)).wait_recv()
                passed.append(copy(w, 3 + j, slot, c, sib))
                passed[-1].start()
        for w in range(n):
            for j, (tx, ty) in enumerate(chips):
                copy(w, 3 + j, 2 * tx + ty, 1 - c, sib).wait_recv()
        for cp in first + passed:
            cp.wait_send()
        for cp in local:
            cp.wait()

    return pl.pallas_call(
        body, name=name, in_specs=[ANY] * n, out_specs=[ANY] * n,
        out_shape=[jax.ShapeDtypeStruct((N_CHIPS,) + s.shape, s.dtype) for s in shards],
        scratch_shapes=[pltpu.SemaphoreType.DMA((6 * n,)), pltpu.SemaphoreType.DMA((6 * n,)),
                        pltpu.SemaphoreType.DMA((n,))],
    )(*shards)


def _pair_exchange(grads, *, name):
    n = len(grads)

    def body(*refs):
        ins, outs = refs[:n], refs[n:2 * n]
        send, recv = refs[2 * n:]
        x, y, c, _ = _place()
        cps = []
        for w in range(n):
            h = grads[w].shape[1] // 2
            cps.append(pltpu.make_async_remote_copy(
                src_ref=ins[w].at[:, pl.ds((1 - c) * h, h), :], dst_ref=outs[w],
                send_sem=send.at[w], recv_sem=recv.at[w], device_id=(x, y, 1 - c), device_id_type=MESH))
            cps[-1].start()
        for cp in cps:
            cp.wait()

    return pl.pallas_call(
        body, name=name, in_specs=[ANY] * n, out_specs=[ANY] * n,
        out_shape=[jax.ShapeDtypeStruct((N_CHIPS, g.shape[1] // 2, g.shape[2]), g.dtype) for g in grads],
        scratch_shapes=[pltpu.SemaphoreType.DMA((n,)), pltpu.SemaphoreType.DMA((n,))],
    )(*grads)


def _pair_sum(g, got, *, name):
    S, R, C = g.shape
    h = R // 2
    tr = math.gcd(h, 256)
    nr = h // tr
    c = lax.axis_index("c")

    def body(c_ref, a_ref, b_ref, o_ref):
        o_ref[...] = (a_ref[...].astype(F32) + b_ref[...].astype(F32)).astype(BF16)

    return pl.pallas_call(
        body, name=name,
        grid_spec=pltpu.PrefetchScalarGridSpec(
            num_scalar_prefetch=1, grid=(S, nr),
            in_specs=[pl.BlockSpec((None, tr, C), lambda s, r, cr: (s, cr[0] * nr + r, 0)),
                      pl.BlockSpec((None, tr, C), lambda s, r, cr: (s, r, 0))],
            out_specs=pl.BlockSpec((None, tr, C), lambda s, r, cr: (s, r, 0))),
        out_shape=jax.ShapeDtypeStruct((S, h, C), BF16),
        compiler_params=_params(("parallel", "parallel")),
    )(c.reshape(1).astype(jnp.int32), g, got)


def _chip_exchange(parts, *, name):
    n = len(parts)

    def body(*refs):
        ins, outs = refs[:n], refs[n:2 * n]
        send, recv = refs[2 * n:]
        x, y, c, chips = _place()
        cps = []
        for w in range(n):
            for j, (tx, ty) in enumerate(chips):
                cps.append(pltpu.make_async_remote_copy(
                    src_ref=ins[w].at[2 * tx + ty], dst_ref=outs[w].at[j],
                    send_sem=send.at[3 * w + j], recv_sem=recv.at[3 * w + j],
                    device_id=(tx, ty, c), device_id_type=MESH))
                cps[-1].start()
        for cp in cps:
            cp.wait()

    return pl.pallas_call(
        body, name=name, in_specs=[ANY] * n, out_specs=[ANY] * n,
        out_shape=[jax.ShapeDtypeStruct((3,) + p.shape[1:], p.dtype) for p in parts],
        scratch_shapes=[pltpu.SemaphoreType.DMA((3 * n,)), pltpu.SemaphoreType.DMA((3 * n,))],
    )(*parts)


def _chip_sum(part, got, *, name):
    S, h, C = part.shape
    tr = math.gcd(h, 256)
    me = 2 * lax.axis_index("x") + lax.axis_index("y")

    def body(m_ref, own_ref, g0_ref, g1_ref, g2_ref, o_ref):
        acc = own_ref[...].astype(F32) + g0_ref[...].astype(F32)
        o_ref[...] = (acc + g1_ref[...].astype(F32)) + g2_ref[...].astype(F32)

    def piece(j):
        return pl.BlockSpec((None, tr, C), lambda r, m: (j, r, 0))

    return pl.pallas_call(
        body, name=name,
        grid_spec=pltpu.PrefetchScalarGridSpec(
            num_scalar_prefetch=1, grid=(h // tr,),
            in_specs=[pl.BlockSpec((None, tr, C), lambda r, m: (m[0], r, 0)), piece(0), piece(1), piece(2)],
            out_specs=pl.BlockSpec((tr, C), lambda r, m: (r, 0))),
        out_shape=jax.ShapeDtypeStruct((h, C), F32),
        compiler_params=_params(("parallel",)),
    )(me.reshape(1).astype(jnp.int32), part, got, got, got)


def _pair_gather(halves, *, name):
    n = len(halves)

    def body(*refs):
        ins, outs = refs[:n], refs[n:2 * n]
        send, recv, loc = refs[2 * n:]
        x, y, c, _ = _place()
        cps, local = [], []
        for w in range(n):
            h = halves[w].shape[0]
            mine = outs[w].at[pl.ds(c * h, h), :]
            local.append(pltpu.make_async_copy(ins[w], mine, loc.at[w]))
            local[-1].start()
            cps.append(pltpu.make_async_remote_copy(
                src_ref=ins[w], dst_ref=mine, send_sem=send.at[w], recv_sem=recv.at[w],
                device_id=(x, y, 1 - c), device_id_type=MESH))
            cps[-1].start()
        for w in range(n):
            h = halves[w].shape[0]
            theirs = outs[w].at[pl.ds((1 - c) * h, h), :]
            pltpu.make_async_remote_copy(
                src_ref=ins[w], dst_ref=theirs, send_sem=send.at[w], recv_sem=recv.at[w],
                device_id=(x, y, 1 - c), device_id_type=MESH).wait_recv()
        for cp in cps:
            cp.wait_send()
        for cp in local:
            cp.wait()

    return pl.pallas_call(
        body, name=name, in_specs=[ANY] * n, out_specs=[ANY] * n,
        out_shape=[jax.ShapeDtypeStruct((2 * hv.shape[0], hv.shape[1]), hv.dtype) for hv in halves],
        scratch_shapes=[pltpu.SemaphoreType.DMA((n,)), pltpu.SemaphoreType.DMA((n,)), pltpu.SemaphoreType.DMA((n,))],
    )(*halves)


def _allreduce_small(v, *, name):
    R, C = v.shape
    ND = 8

    def body(v_ref, o_ref, all_ref, send, recv, loc):
        x, y, c, chips = _place()
        me, sib = (x, y, c), (x, y, 1 - c)

        def rows(px, py, pc):
            return all_ref.at[pl.ds((4 * px + 2 * py + pc) * R, R), :]

        def copy(k, block, to, src=None):
            return pltpu.make_async_remote_copy(
                src_ref=rows(*block) if src is None else src, dst_ref=rows(*block),
                send_sem=send.at[k], recv_sem=recv.at[k], device_id=to, device_id_type=MESH)

        mine = pltpu.make_async_copy(v_ref, rows(*me), loc)
        mine.start()
        first = [copy(0, me, sib, src=v_ref)]
        first += [copy(1 + j, me, (*chip, c), src=v_ref) for j, chip in enumerate(chips)]
        for cp in first:
            cp.start()
        passed = [copy(4 + j, (*chip, c), sib) for j, chip in enumerate(chips)]
        for j, chip in enumerate(chips):
            copy(1 + j, (*chip, c), me).wait_recv()
            passed[j].start()
        copy(0, sib, me).wait_recv()
        for j, chip in enumerate(chips):
            copy(4 + j, (*chip, 1 - c), me).wait_recv()
        for cp in first + passed:
            cp.wait_send()
        mine.wait()
        acc = all_ref[0:R, :]
        for d in range(1, ND):
            acc = acc + all_ref[d * R:(d + 1) * R, :]
        o_ref[...] = acc

    vm = pl.BlockSpec(memory_space=pltpu.VMEM)
    return pl.pallas_call(
        body, name=name, in_specs=[vm], out_specs=[vm, vm],
        out_shape=[jax.ShapeDtypeStruct((R, C), F32), jax.ShapeDtypeStruct((ND * R, C), F32)],
        scratch_shapes=[pltpu.SemaphoreType.DMA((7,)), pltpu.SemaphoreType.DMA((7,)), pltpu.SemaphoreType.DMA],
        compiler_params=pltpu.CompilerParams(vmem_limit_bytes=VMEM_LIMIT),
    )(v)[0]


def _adamw(w, g, m, v, *, name):
    R, C = w.shape
    tr = math.gcd(R, 256)

    def body(w_ref, g_ref, m_ref, v_ref, d_ref, nm_ref, nv_ref):
        gg = g_ref[...]
        m_new = ADAM_B1 * m_ref[...] + (1.0 - ADAM_B1) * gg
        v_new = ADAM_B2 * v_ref[...] + (1.0 - ADAM_B2) * (gg * gg)
        m_hat = m_new / (1.0 - ADAM_B1 ** ADAM_STEP)
        v_hat = v_new / (1.0 - ADAM_B2 ** ADAM_STEP)
        d_ref[...] = -ADAM_LR * (m_hat / (jnp.sqrt(v_hat) + ADAM_EPS) + ADAM_WD * w_ref[...])
        nm_ref[...] = m_new
        nv_ref[...] = v_new

    blk = pl.BlockSpec((tr, C), lambda i: (i, 0))
    shp = jax.ShapeDtypeStruct((R, C), F32)
    return pl.pallas_call(
        body, name=name, grid=(R // tr,), in_specs=[blk] * 4, out_specs=[blk] * 3, out_shape=[shp] * 3,
        compiler_params=_params(("parallel",)),
    )(w, g, m, v)


def _to2d(a):
    flat = a.reshape(-1)
    pad = (-flat.shape[0]) % LANES
    if pad:
        flat = jnp.concatenate([flat, jnp.zeros((pad,), flat.dtype)])
    return flat.reshape(-1, LANES)


def _pack_small(parts):
    rows = jnp.concatenate([_to2d(p) for p in parts], axis=0)
    pad = (-rows.shape[0]) % 256
    if pad:
        rows = jnp.concatenate([rows, jnp.zeros((pad, LANES), rows.dtype)], axis=0)
    return rows


def _unpack_small(rows, shapes):
    out, r = [], 0
    for shp in shapes:
        size = math.prod(shp)
        nrow = -(-size // LANES)
        out.append(rows[r:r + nrow].reshape(-1)[:size].reshape(shp))
        r += nrow
    return out


def kernel(x, mem, g_ffn1, w1_gate, w1_up, w1_down, g_mix, w_in, g_v, w_s, b_s, sinks, g_a_out, g_b_out, w_out, g_x, g_mem, w_xq, w_xkv, w_xo, g_ffn2, w2_gate, w2_up, w2_down, g_final, loss_target, m_g_ffn1, m_w1_gate, m_w1_up, m_w1_down, m_g_mix, m_w_in, m_g_v, m_w_s, m_b_s, m_sinks, m_g_a_out, m_g_b_out, m_w_out, m_g_x, m_g_mem, m_w_xq, m_w_xkv, m_w_xo, m_g_ffn2, m_w2_gate, m_w2_up, m_w2_down, m_g_final, v_g_ffn1, v_w1_gate, v_w1_up, v_w1_down, v_g_mix, v_w_in, v_g_v, v_w_s, v_b_s, v_sinks, v_g_a_out, v_g_b_out, v_w_out, v_g_x, v_g_mem, v_w_xq, v_w_xkv, v_w_xo, v_g_ffn2, v_w2_gate, v_w2_up, v_w2_down, v_g_final):
    args = dict(locals())
    Wp = {n: args[n] for n in ALL_W}
    Mp = {n: args["m_" + n] for n in ALL_W}
    Vp = {n: args["v_" + n] for n in ALL_W}

    shards = [Wp[n][0].astype(BF16) for n in BIG]
    gathered = dict(zip(BIG, _allgather_weights(shards, name="gather_weights")))
    W = {}
    for n in ("w1_gate", "w1_up", "w2_gate", "w2_up", "w_xkv"):
        W[n] = gathered[n]
    for n in ("w1_down", "w2_down", "w_out", "w_xq", "w_xo"):
        g = gathered[n]
        W[n] = g.reshape(g.shape[0] * g.shape[1], g.shape[2])
    W["w_in"] = jnp.transpose(gathered["w_in"], (1, 0, 2)).reshape(D_MODEL, IN_COLS)
    for n in SMALL:
        W[n] = Wp[n]
    W["g_final"] = Wp["g_final"].reshape(1, D_MODEL)
    for n in ("w_s", "b_s"):
        W[n] = Wp[n][0]

    loss, dx, grads = _local_step(x[0], mem[0], loss_target[0], W)

    def stacked(n):
        g = grads[n]
        if n == "w_in":
            return jnp.transpose(g.reshape(D_MODEL, N_CHIPS, IN_COLS // N_CHIPS), (1, 0, 2))
        if g.ndim == 2:
            return g.reshape(N_CHIPS, g.shape[0] // N_CHIPS, g.shape[1])
        return g

    local = [stacked(n) for n in BIG]
    from_sib = _pair_exchange(local, name="pair_exchange")
    chip_part = [_pair_sum(g, s, name=f"pair_sum_{n}") for n, g, s in zip(BIG, local, from_sib)]
    from_chips = _chip_exchange(chip_part, name="chip_exchange")
    halves = [_chip_sum(p, s, name=f"chip_sum_{n}") for n, p, s in zip(BIG, chip_part, from_chips)]
    big_grad = dict(zip(BIG, _pair_gather(halves, name="pair_gather")))

    small_shapes = [Wp[n].shape for n in SMALL]
    packed = _pack_small([grads[n].reshape(Wp[n].shape) for n in SMALL] + [loss])
    summed = _allreduce_small(packed, name="allreduce_small")
    small_grad = dict(zip(SMALL, _unpack_small(summed, small_shapes)))
    nrows = sum(-(-math.prod(s) // LANES) for s in small_shapes)
    loss_total = summed[nrows, 0]

    grad_out, delta, new_m, new_v = {}, {}, {}, {}
    for n in BIG:
        shp = Wp[n].shape
        grad_out[n] = big_grad[n].reshape(shp)
        d, nm, nv = _adamw(Wp[n][0], big_grad[n], Mp[n][0], Vp[n][0], name=f"adamw_{n}")
        delta[n], new_m[n], new_v[n] = d.reshape(shp), nm.reshape(shp), nv.reshape(shp)
    sw = _pack_small([Wp[n] for n in SMALL])
    sg = _pack_small([small_grad[n] for n in SMALL])
    sm = _pack_small([Mp[n] for n in SMALL])
    sv = _pack_small([Vp[n] for n in SMALL])
    d, nm, nv = _adamw(sw, sg, sm, sv, name="adamw_small")
    for n, dd, mm_, vv_ in zip(SMALL, _unpack_small(d, small_shapes), _unpack_small(nm, small_shapes),
                               _unpack_small(nv, small_shapes)):
        grad_out[n], delta[n], new_m[n], new_v[n] = small_grad[n], dd, mm_, vv_

    return (loss_total, dx[None], *[grad_out[n] for n in ALL_W], *[delta[n] for n in ALL_W],
            *[new_m[n] for n in ALL_W], *[new_v[n] for n in ALL_W])
```

```python
import functools
import math

import jax
import jax.numpy as jnp
from jax import lax
from jax.experimental import pallas as pl
from jax.experimental.pallas import tpu as pltpu

F32 = jnp.float32
BF16 = jnp.bfloat16
MESH = pl.DeviceIdType.MESH

D_MODEL = 2048
D_FF = 5632
D_A = 1024
D_B = 1024
CHUNK = 128
A_GROUPS = 8
HEAD_DIM = 64
B_Q_HEADS = 16
B_KV_HEADS = 2
X_HEADS = 4
X_HEAD_DIM = 512
IN_COLS = 3328
O_Q = 2 * D_A
O_K = O_Q + D_B
O_V = O_K + B_KV_HEADS * HEAD_DIM
N_CHIPS = 4
EPS = 1e-5
NEG = -1e30
ADAM_LR = 0.001
ADAM_B1 = 0.9
ADAM_B2 = 0.999
ADAM_EPS = 1e-08
ADAM_WD = 0.01
ADAM_STEP = 10

V7X_VMEM_BYTES = 64 * 1024 * 1024
VMEM_LIMIT = 56 * 1024 * 1024
LANES = 128
SUBLANES = 8


def _params(sem, vmem=VMEM_LIMIT):
    return pltpu.CompilerParams(dimension_semantics=sem, vmem_limit_bytes=vmem)


def _matmul(pairs, *, M, N, K, tm, tn, tk, a_t=False, b_kind="n", out_kind="n", out_dtype=F32,
            scale=1.0, res=None, order="ij", name):
    tm, tn, tk = min(tm, M), min(tn, N), min(tk, K)
    assert M % tm == 0 and N % tn == 0 and K % tk == 0, (name, M, N, K, tm, tn, tk)
    nk = K // tk
    npairs = len(pairs)
    b_t = b_kind in ("t", "st")

    def ij(g0, g1):
        return (g0, g1) if order == "ij" else (g1, g0)

    def a_map(g0, g1, k):
        i, _ = ij(g0, g1)
        return (k, i) if a_t else (i, k)

    a_spec = pl.BlockSpec((tk, tm) if a_t else (tm, tk), a_map)

    b0 = pairs[0][1]
    if b_kind == "n":
        b_spec = pl.BlockSpec((tk, tn), lambda g0, g1, k: (k, ij(g0, g1)[1]))
    elif b_kind == "t":
        b_spec = pl.BlockSpec((tn, tk), lambda g0, g1, k: (ij(g0, g1)[1], k))
    elif b_kind == "sn":
        ns = b0.shape[2]
        assert ns % tn == 0
        nps = ns // tn
        b_spec = pl.BlockSpec((None, tk, tn), lambda g0, g1, k: (ij(g0, g1)[1] // nps, k, ij(g0, g1)[1] % nps))
    else:
        ks = b0.shape[2]
        assert ks % tk == 0
        kps = ks // tk
        b_spec = pl.BlockSpec((None, tn, tk), lambda g0, g1, k: (k // kps, ij(g0, g1)[1], k % kps))

    if out_kind == "n":
        o_spec = pl.BlockSpec((tm, tn), lambda g0, g1, k: ij(g0, g1))
        o_shape = jax.ShapeDtypeStruct((M, N), out_dtype)
    else:
        ns = N // N_CHIPS
        assert ns % tn == 0
        nps_o = ns // tn
        o_spec = pl.BlockSpec((None, tm, tn), lambda g0, g1, k: (ij(g0, g1)[1] // nps_o, ij(g0, g1)[0], ij(g0, g1)[1] % nps_o))
        o_shape = jax.ShapeDtypeStruct((N_CHIPS, M, ns), out_dtype)

    in_specs, args = [], []
    for a, b in pairs:
        in_specs += [a_spec, b_spec]
        args += [a, b]
    if res is not None:
        in_specs.append(pl.BlockSpec((tm, tn), lambda g0, g1, k: ij(g0, g1)))
        args.append(res)

    dn = (((0,) if a_t else (1,), (1,) if b_t else (0,)), ((), ()))

    def body(*refs):
        pos = 2 * npairs
        res_ref = refs[pos] if res is not None else None
        pos += res is not None
        o_ref = refs[pos]
        acc_ref = refs[pos + 1] if nk > 1 else None
        part = None
        for p in range(npairs):
            d = lax.dot_general(refs[2 * p][...], refs[2 * p + 1][...], dn, preferred_element_type=F32)
            part = d if part is None else part + d

        def finish(acc):
            r = acc * scale if scale != 1.0 else acc
            if res_ref is not None:
                r = res_ref[...] + r
            o_ref[...] = r.astype(out_dtype)

        if nk == 1:
            finish(part)
        else:
            k = pl.program_id(2)

            @pl.when(k == 0)
            def _():
                acc_ref[...] = part

            @pl.when(k > 0)
            def _():
                acc_ref[...] += part

            @pl.when(k == nk - 1)
            def _():
                finish(acc_ref[...])

    grid = (M // tm, N // tn, nk) if order == "ij" else (N // tn, M // tm, nk)
    return pl.pallas_call(
        body, name=name, grid=grid, in_specs=in_specs, out_specs=o_spec, out_shape=o_shape,
        scratch_shapes=[pltpu.VMEM((tm, tn), F32)] if nk > 1 else [],
        compiler_params=_params(("parallel", "parallel", "arbitrary")),
    )(*args)


def _rstd(x):
    return lax.rsqrt(jnp.mean(x * x, axis=-1, keepdims=True) + EPS)


def _rms_bwd_math(x, g, dy):
    r = _rstd(x)
    gy = dy * g
    xr = x * r
    dx = r * (gy - xr * jnp.mean(gy * xr, axis=-1, keepdims=True))
    return dx, dy * xr


def _rms_fwd(h, g, *, name, tm=512):
    T, Dm = h.shape
    tm = min(tm, T)

    def body(h_ref, g_ref, o_ref):
        x = h_ref[...]
        o_ref[...] = (x * _rstd(x) * g_ref[...]).astype(BF16)

    return pl.pallas_call(
        body, name=name, grid=(T // tm,),
        in_specs=[pl.BlockSpec((tm, Dm), lambda i: (i, 0)), pl.BlockSpec((1, Dm), lambda i: (0, 0))],
        out_specs=pl.BlockSpec((tm, Dm), lambda i: (i, 0)),
        out_shape=jax.ShapeDtypeStruct((T, Dm), BF16),
        compiler_params=_params(("parallel",)),
    )(h, g)


def _rms_bwd(h, g, dn, dres, *, name, tm=256):
    T, Dm = h.shape
    tm = min(tm, T)
    has_res = dres is not None

    def body(*refs):
        h_ref, g_ref, dn_ref = refs[:3]
        pos = 3
        dres_ref = refs[pos] if has_res else None
        pos += has_res
        dh_ref, dhb_ref, dg_ref = refs[pos:pos + 3]
        dx, dgr = _rms_bwd_math(h_ref[...], g_ref[...], dn_ref[...].astype(F32))
        if has_res:
            dx = dres_ref[...] + dx
        dh_ref[...] = dx
        dhb_ref[...] = dx.astype(BF16)
        part = jnp.sum(dgr, axis=0, keepdims=True)

        @pl.when(pl.program_id(0) == 0)
        def _():
            dg_ref[...] = part

        @pl.when(pl.program_id(0) > 0)
        def _():
            dg_ref[...] += part

    row = pl.BlockSpec((tm, Dm), lambda i: (i, 0))
    vec = pl.BlockSpec((1, Dm), lambda i: (0, 0))
    args = [h, g, dn] + ([dres] if has_res else [])
    return pl.pallas_call(
        body, name=name, grid=(T // tm,),
        in_specs=[row, vec, row] + ([row] if has_res else []),
        out_specs=[row, row, vec],
        out_shape=[jax.ShapeDtypeStruct((T, Dm), F32), jax.ShapeDtypeStruct((T, Dm), BF16),
                   jax.ShapeDtypeStruct((1, Dm), F32)],
        compiler_params=_params(("arbitrary",)),
    )(*args)


def _loss_head(h, g, tgt, *, name, tm=256):
    T, Dm = h.shape
    tm = min(tm, T)

    def body(h_ref, g_ref, t_ref, dh_ref, dhb_ref, dg_ref, loss_ref):
        x = h_ref[...]
        gv = g_ref[...]
        r = _rstd(x)
        diff = x * r * gv - t_ref[...]
        lpart = 0.5 * jnp.sum(jnp.mean(diff * diff, axis=-1, keepdims=True), axis=0, keepdims=True)
        dx, dgr = _rms_bwd_math(x, gv, diff * (1.0 / Dm))
        dh_ref[...] = dx
        dhb_ref[...] = dx.astype(BF16)
        part = jnp.sum(dgr, axis=0, keepdims=True)
        lrow = jnp.broadcast_to(lpart, (1, LANES))

        @pl.when(pl.program_id(0) == 0)
        def _():
            dg_ref[...] = part
            loss_ref[...] = lrow

        @pl.when(pl.program_id(0) > 0)
        def _():
            dg_ref[...] += part
            loss_ref[...] += lrow

    row = pl.BlockSpec((tm, Dm), lambda i: (i, 0))
    vec = pl.BlockSpec((1, Dm), lambda i: (0, 0))
    return pl.pallas_call(
        body, name=name, grid=(T // tm,),
        in_specs=[row, vec, row],
        out_specs=[row, row, vec, pl.BlockSpec((1, LANES), lambda i: (0, 0))],
        out_shape=[jax.ShapeDtypeStruct((T, Dm), F32), jax.ShapeDtypeStruct((T, Dm), BF16),
                   jax.ShapeDtypeStruct((1, Dm), F32), jax.ShapeDtypeStruct((1, LANES), F32)],
        compiler_params=_params(("arbitrary",)),
    )(h, g, tgt)


def _swiglu_up(n, wg, wu, *, name, tm=512):
    T, Dm = n.shape
    S, _, fs = wg.shape
    tm = min(tm, T)

    def body(n_ref, wg_ref, wu_ref, g_ref, u_ref, a_ref):
        x = n_ref[...]
        g = jnp.dot(x, wg_ref[...], preferred_element_type=F32)
        u = jnp.dot(x, wu_ref[...], preferred_element_type=F32)
        g_ref[...] = g.astype(BF16)
        u_ref[...] = u.astype(BF16)
        a_ref[...] = (g * jax.nn.sigmoid(g) * u).astype(BF16)

    wspec = pl.BlockSpec((None, Dm, fs), lambda j, i: (j, 0, 0))
    ospec = pl.BlockSpec((tm, fs), lambda j, i: (i, j))
    oshape = jax.ShapeDtypeStruct((T, S * fs), BF16)
    return pl.pallas_call(
        body, name=name, grid=(S, T // tm),
        in_specs=[pl.BlockSpec((tm, Dm), lambda j, i: (i, 0)), wspec, wspec],
        out_specs=[ospec, ospec, ospec], out_shape=[oshape, oshape, oshape],
        compiler_params=_params(("parallel", "parallel")),
    )(n, wg, wu)


def _swiglu_bwd_act(dhb, wd, G, U, *, name, tm=512, tn=1408):
    T, Dm = dhb.shape
    Fd = wd.shape[0]
    tm, tn = min(tm, T), min(tn, Fd)

    def body(dh_ref, wd_ref, g_ref, u_ref, dg_ref, du_ref):
        da = 0.5 * lax.dot_general(dh_ref[...], wd_ref[...], (((1,), (1,)), ((), ())), preferred_element_type=F32)
        g = g_ref[...].astype(F32)
        u = u_ref[...].astype(F32)
        sg = jax.nn.sigmoid(g)
        dg_ref[...] = (da * u * (sg * (1.0 + g * (1.0 - sg)))).astype(BF16)
        du_ref[...] = (da * (g * sg)).astype(BF16)

    blk = pl.BlockSpec((tm, tn), lambda j, i: (i, j))
    oshape = jax.ShapeDtypeStruct((T, Fd), BF16)
    return pl.pallas_call(
        body, name=name, grid=(Fd // tn, T // tm),
        in_specs=[pl.BlockSpec((tm, Dm), lambda j, i: (i, 0)), pl.BlockSpec((tn, Dm), lambda j, i: (j, 0)), blk, blk],
        out_specs=[blk, blk], out_shape=[oshape, oshape],
        compiler_params=_params(("parallel", "parallel")),
    )(dhb, wd, G, U)


_INV_SQRT2 = 0.7071067811865476
_INV_SQRT2PI = 0.3989422804014327


def _erf(x):
    ax = jnp.abs(x)
    t = 1.0 / (1.0 + 0.3275911 * ax)
    poly = t * (0.254829592 + t * (-0.284496736 + t * (1.421413741 + t * (-1.453152027 + t * 1.061405429))))
    y = 1.0 - poly * jnp.exp(-ax * ax)
    return jnp.where(x < 0, -y, y)


def _gelu_cdf(x):
    return 0.5 * (1.0 + _erf(x * _INV_SQRT2))


def _lane_lt64(shape):
    return lax.broadcasted_iota(jnp.int32, shape, len(shape) - 1) < HEAD_DIM


def _dup_half(x, kv):
    rolled = pltpu.roll(x, HEAD_DIM, 1)
    lo = _lane_lt64(x.shape)
    return jnp.where(lo, x, rolled) if kv == 0 else jnp.where(lo, rolled, x)


HEADS_PER_KV = B_Q_HEADS // B_KV_HEADS
PAIRS = HEADS_PER_KV // 2


def _attn_mask(block):
    shape = (HEADS_PER_KV * CHUNK, 2 * CHUNK)
    qpos = (lax.broadcasted_iota(jnp.int32, shape, 0) & (CHUNK - 1)) + CHUNK
    kpos = lax.broadcasted_iota(jnp.int32, shape, 1)
    diff = qpos - kpos
    first_key = jnp.where(block == 0, CHUNK, 0)
    return (diff >= 0) & (diff < CHUNK) & (kpos >= first_key)


def _stack_heads(tiles, lo):
    parts = []
    for t in tiles:
        parts += [jnp.where(lo, t, 0.0), jnp.where(lo, 0.0, t)]
    return jnp.concatenate(parts, axis=0)


def _unstack_heads(s, lo):
    return [jnp.where(lo, s[2 * p * CHUNK:(2 * p + 1) * CHUNK], s[(2 * p + 1) * CHUNK:(2 * p + 2) * CHUNK])
            for p in range(PAIRS)]


def _stack_sinks(sk_ref, kv):
    return jnp.concatenate([jnp.broadcast_to(sk_ref[:, h:h + 1], (CHUNK, 1))
                            for h in range(kv * HEADS_PER_KV, (kv + 1) * HEADS_PER_KV)], axis=0)


def _sgu_forward(z_ref, gv, wsm, bst):
    zu = z_ref[:, 0:D_A]
    zv = z_ref[:, D_A:2 * D_A]
    u = zu * _gelu_cdf(zu)
    v = zv * _gelu_cdf(zv)
    rv = _rstd(v)
    vn = (v * rv * gv).astype(BF16)
    svs = []
    for g in range(A_GROUPS):
        sl = slice(g * CHUNK, (g + 1) * CHUNK)
        svs.append(jnp.dot(wsm[g], vn[:, sl], preferred_element_type=F32) + bst[:, g:g + 1])
    sv = jnp.concatenate(svs, axis=1)
    return zu, zv, u, v, rv, vn, sv


def _masked_ws(ws_ref):
    tril = lax.broadcasted_iota(jnp.int32, (CHUNK, CHUNK), 0) >= lax.broadcasted_iota(jnp.int32, (CHUNK, CHUNK), 1)
    return [jnp.where(tril, ws_ref[g], 0.0).astype(BF16) for g in range(A_GROUPS)], tril


def _attn_probs(qm, kkd, sink, mask):
    s = lax.dot_general(qm, kkd, (((1,), (1,)), ((), ())), preferred_element_type=F32) * (HEAD_DIM ** -0.5)
    s = jnp.where(mask, s, NEG)
    m = jnp.maximum(jnp.max(s, axis=-1, keepdims=True), sink)
    e = jnp.exp(s - m)
    es = jnp.exp(sink - m)
    inv = 1.0 / (jnp.sum(e, axis=-1, keepdims=True) + es)
    return e * inv, es * inv


def _mixer_fwd(z, gv, ws, bst, sinks, ga, gb, *, name):
    T = z.shape[0]
    nb = T // CHUNK
    kvb = O_K // (2 * CHUNK)

    def body(z_ref, zp_ref, gv_ref, ws_ref, bst_ref, sk_ref, ga_ref, gb_ref, o_ref):
        i = pl.program_id(0)
        wsm, _ = _masked_ws(ws_ref)
        _, _, u, _, _, _, sv = _sgu_forward(z_ref, gv_ref[...], wsm, bst_ref[...])
        ya = u * sv
        o_ref[:, 0:D_A] = (ya * _rstd(ya) * ga_ref[...]).astype(BF16)

        mask = _attn_mask(i)
        kk = jnp.concatenate([zp_ref[:, 0:CHUNK], z_ref[:, O_K:O_V]], axis=0)
        vv = jnp.concatenate([zp_ref[:, CHUNK:2 * CHUNK], z_ref[:, O_V:IN_COLS]], axis=0)
        lo = _lane_lt64((CHUNK, LANES))
        outs = []
        for kv in range(B_KV_HEADS):
            kkd = _dup_half(kk, kv).astype(BF16)
            vvd = _dup_half(vv, kv).astype(BF16)
            q = _stack_heads([z_ref[:, O_Q + (kv * PAIRS + pr) * LANES:O_Q + (kv * PAIRS + pr + 1) * LANES]
                              for pr in range(PAIRS)], lo).astype(BF16)
            p, _ = _attn_probs(q, kkd, _stack_sinks(sk_ref, kv), mask)
            outs += _unstack_heads(jnp.dot(p.astype(BF16), vvd, preferred_element_type=F32), lo)
        yb = jnp.concatenate(outs, axis=1)
        o_ref[:, D_A:D_A + D_B] = (yb * _rstd(yb) * gb_ref[...]).astype(BF16)

    full = lambda shape: pl.BlockSpec(shape, lambda i: (0,) * len(shape))
    return pl.pallas_call(
        body, name=name, grid=(nb,),
        in_specs=[pl.BlockSpec((CHUNK, IN_COLS), lambda i: (i, 0)),
                  pl.BlockSpec((CHUNK, 2 * CHUNK), lambda i: (jnp.maximum(i - 1, 0), kvb)),
                  full((1, D_A)), full((A_GROUPS, CHUNK, CHUNK)), full((CHUNK, A_GROUPS)), full((1, B_Q_HEADS)),
                  full((1, D_A)), full((1, D_B))],
        out_specs=pl.BlockSpec((CHUNK, D_A + D_B), lambda i: (i, 0)),
        out_shape=jax.ShapeDtypeStruct((T, D_A + D_B), BF16),
        compiler_params=_params(("parallel",)),
    )(z, z, gv, ws, bst, sinks, ga, gb)


def _mixer_bwd(z, dyn, gv, ws, bst, sinks, ga, gb, *, name):
    T = z.shape[0]
    nb = T // CHUNK
    kvb = O_K // (2 * CHUNK)
    NT = (((0,), (0,)), ((), ()))

    def body(z_ref, zp_ref, dy_ref, gv_ref, ws_ref, bst_ref, sk_ref, ga_ref, gb_ref,
             dz_ref, dgv_ref, dws_ref, dbst_ref, dsk_ref, dga_ref, dgb_ref, carry_ref, p_ref):
        step = pl.program_id(0)
        i = nb - 1 - step

        @pl.when(step == 0)
        def _():
            carry_ref[...] = jnp.zeros_like(carry_ref)
            dgv_ref[...] = jnp.zeros_like(dgv_ref)
            dws_ref[...] = jnp.zeros_like(dws_ref)
            dbst_ref[...] = jnp.zeros_like(dbst_ref)
            dsk_ref[...] = jnp.zeros_like(dsk_ref)
            dga_ref[...] = jnp.zeros_like(dga_ref)
            dgb_ref[...] = jnp.zeros_like(dgb_ref)

        wsm, tril = _masked_ws(ws_ref)
        gvv = gv_ref[...]
        zu, zv, u, v, rv, vn, sv = _sgu_forward(z_ref, gvv, wsm, bst_ref[...])
        ya = u * sv
        dya, dga_rows = _rms_bwd_math(ya, ga_ref[...], dy_ref[:, 0:D_A].astype(F32))
        dga_ref[...] += jnp.sum(dga_rows, axis=0, keepdims=True)
        du = dya * sv
        dsv = dya * u
        dvn_parts = []
        for g in range(A_GROUPS):
            sl = slice(g * CHUNK, (g + 1) * CHUNK)
            dsv_g = dsv[:, sl]
            dsv_gb = dsv_g.astype(BF16)
            dw = lax.dot_general(dsv_gb, vn[:, sl], (((1,), (1,)), ((), ())), preferred_element_type=F32)
            dws_ref[g] += jnp.where(tril, dw, 0.0)
            dbst_ref[:, g:g + 1] += jnp.sum(dsv_g, axis=1, keepdims=True)
            dvn_parts.append(lax.dot_general(wsm[g], dsv_gb, NT, preferred_element_type=F32))
        dvn = jnp.concatenate(dvn_parts, axis=1)
        dv, dgv_rows = _rms_bwd_math(v, gvv, dvn)
        dgv_ref[...] += jnp.sum(dgv_rows, axis=0, keepdims=True)
        dz_ref[:, 0:D_A] = (du * (_gelu_cdf(zu) + zu * jnp.exp(-0.5 * zu * zu) * _INV_SQRT2PI)).astype(BF16)
        dz_ref[:, D_A:2 * D_A] = (dv * (_gelu_cdf(zv) + zv * jnp.exp(-0.5 * zv * zv) * _INV_SQRT2PI)).astype(BF16)

        mask = _attn_mask(i)
        kk = jnp.concatenate([zp_ref[:, 0:CHUNK], z_ref[:, O_K:O_V]], axis=0)
        vv = jnp.concatenate([zp_ref[:, CHUNK:2 * CHUNK], z_ref[:, O_V:IN_COLS]], axis=0)
        lo = _lane_lt64((CHUNK, LANES))
        kkd = [_dup_half(kk, kv).astype(BF16) for kv in range(B_KV_HEADS)]
        vvd = [_dup_half(vv, kv).astype(BF16) for kv in range(B_KV_HEADS)]
        outs, qs, psinks = [], [], []
        for kv in range(B_KV_HEADS):
            qs.append(_stack_heads([z_ref[:, O_Q + (kv * PAIRS + pr) * LANES:O_Q + (kv * PAIRS + pr + 1) * LANES]
                                    for pr in range(PAIRS)], lo).astype(BF16))
            p, ps = _attn_probs(qs[kv], kkd[kv], _stack_sinks(sk_ref, kv), mask)
            p_ref[kv] = p
            psinks.append(ps)
            outs += _unstack_heads(jnp.dot(p.astype(BF16), vvd[kv], preferred_element_type=F32), lo)
        yb = jnp.concatenate(outs, axis=1)
        dyb, dgb_rows = _rms_bwd_math(yb, gb_ref[...], dy_ref[:, D_A:D_A + D_B].astype(F32))
        dgb_ref[...] += jnp.sum(dgb_rows, axis=0, keepdims=True)

        dkk, dvv = [], []
        for kv in range(B_KV_HEADS):
            do = _stack_heads([dyb[:, (kv * PAIRS + pr) * LANES:(kv * PAIRS + pr + 1) * LANES]
                               for pr in range(PAIRS)], lo).astype(BF16)
            p = p_ref[kv]
            dvv.append(lax.dot_general(p.astype(BF16), do, NT, preferred_element_type=F32))
            dp = lax.dot_general(do, vvd[kv], (((1,), (1,)), ((), ())), preferred_element_type=F32)
            delta = jnp.sum(p * dp, axis=-1, keepdims=True)
            dsink = -psinks[kv] * delta
            for g in range(HEADS_PER_KV):
                h = kv * HEADS_PER_KV + g
                dsk_ref[:, h:h + 1] += jnp.sum(dsink[g * CHUNK:(g + 1) * CHUNK], axis=0, keepdims=True)
            ds = (p * (dp - delta) * (HEAD_DIM ** -0.5)).astype(BF16)
            dq = _unstack_heads(jnp.dot(ds, kkd[kv], preferred_element_type=F32), lo)
            for pr in range(PAIRS):
                c0 = O_Q + (kv * PAIRS + pr) * LANES
                dz_ref[:, c0:c0 + LANES] = dq[pr].astype(BF16)
            dkk.append(lax.dot_general(ds, qs[kv], NT, preferred_element_type=F32))

        def fold(parts):
            tot = [t + pltpu.roll(t, HEAD_DIM, 1) for t in parts]
            return jnp.where(_lane_lt64(tot[0].shape), tot[0], tot[1])

        dk_all = fold(dkk)
        dv_all = fold(dvv)
        dz_ref[:, O_K:O_V] = (dk_all[CHUNK:] + carry_ref[:, 0:CHUNK]).astype(BF16)
        dz_ref[:, O_V:IN_COLS] = (dv_all[CHUNK:] + carry_ref[:, CHUNK:2 * CHUNK]).astype(BF16)
        carry_ref[:, 0:CHUNK] = dk_all[:CHUNK]
        carry_ref[:, CHUNK:2 * CHUNK] = dv_all[:CHUNK]

    full = lambda shape: pl.BlockSpec(shape, lambda s: (0,) * len(shape))
    rev = lambda s: nb - 1 - s
    return pl.pallas_call(
        body, name=name, grid=(nb,),
        in_specs=[pl.BlockSpec((CHUNK, IN_COLS), lambda s: (rev(s), 0)),
                  pl.BlockSpec((CHUNK, 2 * CHUNK), lambda s: (jnp.maximum(rev(s) - 1, 0), kvb)),
                  pl.BlockSpec((CHUNK, D_A + D_B), lambda s: (rev(s), 0)),
                  full((1, D_A)), full((A_GROUPS, CHUNK, CHUNK)), full((CHUNK, A_GROUPS)), full((1, B_Q_HEADS)),
                  full((1, D_A)), full((1, D_B))],
        out_specs=[pl.BlockSpec((CHUNK, IN_COLS), lambda s: (rev(s), 0)),
                   full((1, D_A)), full((A_GROUPS, CHUNK, CHUNK)), full((CHUNK, A_GROUPS)), full((1, B_Q_HEADS)),
                   full((1, D_A)), full((1, D_B))],
        out_shape=[jax.ShapeDtypeStruct((T, IN_COLS), BF16), jax.ShapeDtypeStruct((1, D_A), F32),
                   jax.ShapeDtypeStruct((A_GROUPS, CHUNK, CHUNK), F32), jax.ShapeDtypeStruct((CHUNK, A_GROUPS), F32),
                   jax.ShapeDtypeStruct((1, B_Q_HEADS), F32), jax.ShapeDtypeStruct((1, D_A), F32),
                   jax.ShapeDtypeStruct((1, D_B), F32)],
        scratch_shapes=[pltpu.VMEM((CHUNK, 2 * CHUNK), F32), pltpu.VMEM((B_KV_HEADS, HEADS_PER_KV * CHUNK, 2 * CHUNK), F32)],
        compiler_params=_params(("arbitrary",)),
    )(z, z, dyn, gv, ws, bst, sinks, ga, gb)


def _xattn_probs(qh, kh):
    s = lax.dot_general(qh, kh, (((1,), (1,)), ((), ())), preferred_element_type=F32) * (X_HEAD_DIM ** -0.5)
    e = jnp.exp(s - jnp.max(s, axis=-1, keepdims=True))
    return e / jnp.sum(e, axis=-1, keepdims=True)


def _xattn_fwd(q, kvm, *, name, tm=512):
    T = q.shape[0]
    Mm = kvm.shape[0]
    tm = min(tm, T)

    def body(q_ref, kv_ref, o_ref):
        for h in range(X_HEADS):
            sl = slice(h * X_HEAD_DIM, (h + 1) * X_HEAD_DIM)
            kh = kv_ref[:, sl].astype(BF16)
            vh = kv_ref[:, D_MODEL + h * X_HEAD_DIM:D_MODEL + (h + 1) * X_HEAD_DIM].astype(BF16)
            p = _xattn_probs(q_ref[:, sl], kh)
            o_ref[:, sl] = jnp.dot(p.astype(BF16), vh, preferred_element_type=F32).astype(BF16)

    return pl.pallas_call(
        body, name=name, grid=(T // tm,),
        in_specs=[pl.BlockSpec((tm, D_MODEL), lambda i: (i, 0)), pl.BlockSpec((Mm, 2 * D_MODEL), lambda i: (0, 0))],
        out_specs=pl.BlockSpec((tm, D_MODEL), lambda i: (i, 0)),
        out_shape=jax.ShapeDtypeStruct((T, D_MODEL), BF16),
        compiler_params=_params(("parallel",)),
    )(q, kvm)


def _xattn_bwd(q, kvm, do, *, name, tm=512):
    T = q.shape[0]
    Mm = kvm.shape[0]
    tm = min(tm, T)
    NT = (((0,), (0,)), ((), ()))

    def body(q_ref, kv_ref, do_ref, dq_ref, dkv_ref):
        @pl.when(pl.program_id(0) == 0)
        def _():
            dkv_ref[...] = jnp.zeros_like(dkv_ref)

        for h in range(X_HEADS):
            sl = slice(h * X_HEAD_DIM, (h + 1) * X_HEAD_DIM)
            slv = slice(D_MODEL + h * X_HEAD_DIM, D_MODEL + (h + 1) * X_HEAD_DIM)
            kh = kv_ref[:, sl].astype(BF16)
            vh = kv_ref[:, slv].astype(BF16)
            qh = q_ref[:, sl]
            doh = do_ref[:, sl]
            p = _xattn_probs(qh, kh)
            dkv_ref[:, slv] += lax.dot_general(p.astype(BF16), doh, NT, preferred_element_type=F32)
            dp = lax.dot_general(doh, vh, (((1,), (1,)), ((), ())), preferred_element_type=F32)
            ds = (p * (dp - jnp.sum(p * dp, axis=-1, keepdims=True)) * (X_HEAD_DIM ** -0.5)).astype(BF16)
            dq_ref[:, sl] = jnp.dot(ds, kh, preferred_element_type=F32).astype(BF16)
            dkv_ref[:, sl] += lax.dot_general(ds, qh, NT, preferred_element_type=F32)

    row = pl.BlockSpec((tm, D_MODEL), lambda i: (i, 0))
    kvs = pl.BlockSpec((Mm, 2 * D_MODEL), lambda i: (0, 0))
    return pl.pallas_call(
        body, name=name, grid=(T // tm,),
        in_specs=[row, kvs, row], out_specs=[row, kvs],
        out_shape=[jax.ShapeDtypeStruct((T, D_MODEL), BF16), jax.ShapeDtypeStruct((Mm, 2 * D_MODEL), F32)],
        compiler_params=_params(("arbitrary",)),
    )(q, kvm, do)


def _swiglu_block_bwd(tag, hin, g_norm, n, G, U, A, wg, wu, wd, dh, dhb):
    T = hin.shape[0]
    dG, dU = _swiglu_bwd_act(dhb, wd, G, U, name=f"{tag}_bwd_act")
    dwd = _matmul([(A, dhb)], M=D_FF, N=D_MODEL, K=T, tm=1408, tn=1024, tk=2048, a_t=True, out_dtype=BF16,
                  scale=0.5, name=f"{tag}_dwd")
    dwg = _matmul([(n, dG)], M=D_MODEL, N=D_FF, K=T, tm=1024, tn=1408, tk=2048, a_t=True, out_kind="s",
                  out_dtype=BF16, order="ji", name=f"{tag}_dwg")
    dwu = _matmul([(n, dU)], M=D_MODEL, N=D_FF, K=T, tm=1024, tn=1408, tk=2048, a_t=True, out_kind="s",
                  out_dtype=BF16, order="ji", name=f"{tag}_dwu")
    dn = _matmul([(dG, wg), (dU, wu)], M=T, N=D_MODEL, K=D_FF, tm=512, tn=D_MODEL, tk=1408, b_kind="st",
                 name=f"{tag}_dn")
    dhin, dhinb, dg = _rms_bwd(hin, g_norm, dn, dh, name=f"{tag}_norm_bwd")
    return dhin, dhinb, dg, dwg, dwu, dwd


def _local_step(x, mem, tgt, W):
    T = x.shape[0]
    Mm = mem.shape[0]
    mm = functools.partial(_matmul)

    n1 = _rms_fwd(x, W["g_ffn1"], name="f_norm1")
    G1, U1, A1 = _swiglu_up(n1, W["w1_gate"], W["w1_up"], name="f_ffn1_up")
    h1 = mm([(A1, W["w1_down"])], M=T, N=D_MODEL, K=D_FF, tm=512, tn=D_MODEL, tk=1408, scale=0.5, res=x,
            name="f_ffn1_down")
    n2 = _rms_fwd(h1, W["g_mix"], name="f_norm2")
    z = mm([(n2, W["w_in"])], M=T, N=IN_COLS, K=D_MODEL, tm=512, tn=IN_COLS // 2, tk=D_MODEL, name="f_w_in")
    bst = jnp.transpose(W["b_s"])
    yn = _mixer_fwd(z, W["g_v"], W["w_s"], bst, W["sinks"], W["g_a_out"], W["g_b_out"], name="f_mixer")
    h2 = mm([(yn, W["w_out"])], M=T, N=D_MODEL, K=D_MODEL, tm=512, tn=D_MODEL, tk=D_MODEL, res=h1, name="f_w_out")
    n3 = _rms_fwd(h2, W["g_x"], name="f_norm3")
    memn = _rms_fwd(mem, W["g_mem"], name="f_norm_mem")
    q3 = mm([(n3, W["w_xq"])], M=T, N=D_MODEL, K=D_MODEL, tm=512, tn=D_MODEL, tk=D_MODEL, out_dtype=BF16,
            name="f_w_xq")
    kvm = mm([(memn, W["w_xkv"])], M=Mm, N=2 * D_MODEL, K=D_MODEL, tm=Mm, tn=1024, tk=D_MODEL, b_kind="sn",
             name="f_w_xkv")
    o3 = _xattn_fwd(q3, kvm, name="f_xattn")
    h3 = mm([(o3, W["w_xo"])], M=T, N=D_MODEL, K=D_MODEL, tm=512, tn=D_MODEL, tk=D_MODEL, res=h2, name="f_w_xo")
    n4 = _rms_fwd(h3, W["g_ffn2"], name="f_norm4")
    G2, U2, A2 = _swiglu_up(n4, W["w2_gate"], W["w2_up"], name="f_ffn2_up")
    h4 = mm([(A2, W["w2_down"])], M=T, N=D_MODEL, K=D_FF, tm=512, tn=D_MODEL, tk=1408, scale=0.5, res=h3,
            name="f_ffn2_down")

    grads = {}
    dh4, dh4b, grads["g_final"], loss = _loss_head(h4, W["g_final"], tgt, name="loss_head")
    dh3, dh3b, grads["g_ffn2"], grads["w2_gate"], grads["w2_up"], grads["w2_down"] = _swiglu_block_bwd(
        "b_ffn2", h3, W["g_ffn2"], n4, G2, U2, A2, W["w2_gate"], W["w2_up"], W["w2_down"], dh4, dh4b)

    do3 = mm([(dh3b, W["w_xo"])], M=T, N=D_MODEL, K=D_MODEL, tm=512, tn=D_MODEL, tk=D_MODEL, b_kind="t",
             out_dtype=BF16, name="b_do3")
    grads["w_xo"] = mm([(o3, dh3b)], M=D_MODEL, N=D_MODEL, K=T, tm=1024, tn=D_MODEL, tk=1024, a_t=True,
                       out_dtype=BF16, name="b_dw_xo")
    dq3, dkvm = _xattn_bwd(q3, kvm, do3, name="b_xattn")
    grads["w_xq"] = mm([(n3, dq3)], M=D_MODEL, N=D_MODEL, K=T, tm=1024, tn=D_MODEL, tk=1024, a_t=True,
                       out_dtype=BF16, name="b_dw_xq")
    dn3 = mm([(dq3, W["w_xq"])], M=T, N=D_MODEL, K=D_MODEL, tm=512, tn=D_MODEL, tk=D_MODEL, b_kind="t",
             name="b_dn3")
    dh2, dh2b, grads["g_x"] = _rms_bwd(h2, W["g_x"], dn3, dh3, name="b_norm3")
    dkvmb = dkvm.astype(BF16)
    grads["w_xkv"] = mm([(memn, dkvmb)], M=D_MODEL, N=2 * D_MODEL, K=Mm, tm=D_MODEL, tn=1024, tk=Mm, a_t=True,
                        out_kind="s", out_dtype=BF16, name="b_dw_xkv")
    dmemn = mm([(dkvmb, W["w_xkv"])], M=Mm, N=D_MODEL, K=2 * D_MODEL, tm=Mm, tn=D_MODEL, tk=1024, b_kind="st",
               name="b_dmemn")
    _, _, grads["g_mem"] = _rms_bwd(mem, W["g_mem"], dmemn, None, name="b_norm_mem")

    dyn = mm([(dh2b, W["w_out"])], M=T, N=D_MODEL, K=D_MODEL, tm=512, tn=D_MODEL, tk=D_MODEL, b_kind="t",
             out_dtype=BF16, name="b_dyn")
    grads["w_out"] = mm([(yn, dh2b)], M=D_MODEL, N=D_MODEL, K=T, tm=1024, tn=D_MODEL, tk=1024, a_t=True,
                        out_dtype=BF16, name="b_dw_out")
    dz, grads["g_v"], grads["w_s"], dbst, grads["sinks"], grads["g_a_out"], grads["g_b_out"] = _mixer_bwd(
        z, dyn, W["g_v"], W["w_s"], bst, W["sinks"], W["g_a_out"], W["g_b_out"], name="b_mixer")
    grads["b_s"] = jnp.transpose(dbst)
    grads["w_in"] = mm([(n2, dz)], M=D_MODEL, N=IN_COLS, K=T, tm=1024, tn=IN_COLS, tk=512, a_t=True,
                       out_dtype=BF16, name="b_dw_in")
    dn2 = mm([(dz, W["w_in"])], M=T, N=D_MODEL, K=IN_COLS, tm=512, tn=D_MODEL, tk=IN_COLS, b_kind="t",
             name="b_dn2")
    dh1, dh1b, grads["g_mix"] = _rms_bwd(h1, W["g_mix"], dn2, dh2, name="b_norm2")

    dx, _, grads["g_ffn1"], grads["w1_gate"], grads["w1_up"], grads["w1_down"] = _swiglu_block_bwd(
        "b_ffn1", x, W["g_ffn1"], n1, G1, U1, A1, W["w1_gate"], W["w1_up"], W["w1_down"], dh1, dh1b)
    return loss, dx, grads


BIG = ["w1_gate", "w1_up", "w1_down", "w_in", "w_out", "w_xq", "w_xkv", "w_xo", "w2_gate", "w2_up", "w2_down"]
SMALL = ["g_ffn1", "g_mix", "g_v", "w_s", "b_s", "sinks", "g_a_out", "g_b_out", "g_x", "g_mem", "g_ffn2", "g_final"]
ALL_W = ["g_ffn1", "w1_gate", "w1_up", "w1_down", "g_mix", "w_in", "g_v", "w_s", "b_s", "sinks", "g_a_out",
         "g_b_out", "w_out", "g_x", "g_mem", "w_xq", "w_xkv", "w_xo", "g_ffn2", "w2_gate", "w2_up", "w2_down",
         "g_final"]
ANY = pl.BlockSpec(memory_space=pl.ANY)


def _place():
    x, y, c = lax.axis_index("x"), lax.axis_index("y"), lax.axis_index("c")
    chips = [(1 - x, y), (x, 1 - y), (1 - x, 1 - y)]
    return x, y, c, chips


def _allgather_weights(shards, *, name):
    n = len(shards)

    def body(*refs):
        ins, outs = refs[:n], refs[n:2 * n]
        send, recv, loc = refs[2 * n:]
        x, y, c, chips = _place()
        me = 2 * x + y
        sib = (x, y, 1 - c)

        def half(w, slot, hc):
            h = shards[w].shape[0] // 2
            return outs[w].at[slot, pl.ds(hc * h, h), :]

        def copy(w, k, slot, hc, to, src=None):
            return pltpu.make_async_remote_copy(
                src_ref=half(w, slot, hc) if src is None else src, dst_ref=half(w, slot, hc),
                send_sem=send.at[6 * w + k], recv_sem=recv.at[6 * w + k], device_id=to, device_id_type=MESH)

        own = [pltpu.make_async_remote_copy(
            src_ref=ins[w], dst_ref=outs[w].at[me], send_sem=loc.at[w], recv_sem=loc.at[n + w],
            device_id=sib, device_id_type=MESH) for w in range(n)]
        for cp in own:
            cp.start()
        first = []
        for w in range(n):
            h = shards[w].shape[0] // 2
            for j, (tx, ty) in enumerate(chips):
                first.append(copy(w, j, me, c, (tx, ty, c), src=ins[w].at[pl.ds(c * h, h), :]))
                first[-1].start()
        passed = []
        for w in range(n):
            for j, (tx, ty) in enumerate(chips):
                slot = 2 * tx + ty
                copy(w, j, slot, c, (tx, ty, c)).wait_recv()
                passed.append(copy(w, 3 + j, slot, c, sib))
                passed[-1].start()
        for w in range(n):
            for j, (tx, ty) in enumerate(chips):
                copy(w, 3 + j, 2 * tx + ty, 1 - c, sib).wait_recv()
        for cp in first + passed:
            cp.wait_send()
        for cp in own:
            cp.wait()

    return pl.pallas_call(
        body, name=name, in_specs=[ANY] * n, out_specs=[ANY] * n,
        out_shape=[jax.ShapeDtypeStruct((N_CHIPS,) + s.shape, s.dtype) for s in shards],
        scratch_shapes=[pltpu.SemaphoreType.DMA((6 * n,)), pltpu.SemaphoreType.DMA((6 * n,)),
                        pltpu.SemaphoreType.DMA((2 * n,))],
    )(*shards)


def _pair_exchange(grads, *, name):
    n = len(grads)

    def body(*refs):
        ins, outs = refs[:n], refs[n:2 * n]
        send, recv = refs[2 * n:]
        x, y, c, _ = _place()
        cps = []
        for w in range(n):
            h = grads[w].shape[1] // 2
            cps.append(pltpu.make_async_remote_copy(
                src_ref=ins[w].at[:, pl.ds((1 - c) * h, h), :], dst_ref=outs[w],
                send_sem=send.at[w], recv_sem=recv.at[w], device_id=(x, y, 1 - c), device_id_type=MESH))
            cps[-1].start()
        for cp in cps:
            cp.wait()

    return pl.pallas_call(
        body, name=name, in_specs=[ANY] * n, out_specs=[ANY] * n,
        out_shape=[jax.ShapeDtypeStruct((N_CHIPS, g.shape[1] // 2, g.shape[2]), g.dtype) for g in grads],
        scratch_shapes=[pltpu.SemaphoreType.DMA((n,)), pltpu.SemaphoreType.DMA((n,))],
    )(*grads)


def _pair_sum(g, got, *, name):
    S, R, C = g.shape
    h = R // 2
    tr = math.gcd(h, 256)
    nr = h // tr
    c = lax.axis_index("c")

    def body(c_ref, a_ref, b_ref, o_ref):
        o_ref[...] = (a_ref[...].astype(F32) + b_ref[...].astype(F32)).astype(BF16)

    return pl.pallas_call(
        body, name=name,
        grid_spec=pltpu.PrefetchScalarGridSpec(
            num_scalar_prefetch=1, grid=(S, nr),
            in_specs=[pl.BlockSpec((None, tr, C), lambda s, r, cr: (s, cr[0] * nr + r, 0)),
                      pl.BlockSpec((None, tr, C), lambda s, r, cr: (s, r, 0))],
            out_specs=pl.BlockSpec((None, tr, C), lambda s, r, cr: (s, r, 0))),
        out_shape=jax.ShapeDtypeStruct((S, h, C), BF16),
        compiler_params=_params(("parallel", "parallel")),
    )(c.reshape(1).astype(jnp.int32), g, got)


def _chip_exchange(parts, *, name):
    n = len(parts)

    def body(*refs):
        ins, outs = refs[:n], refs[n:2 * n]
        send, recv = refs[2 * n:]
        x, y, c, chips = _place()
        cps = []
        for w in range(n):
            for j, (tx, ty) in enumerate(chips):
                cps.append(pltpu.make_async_remote_copy(
                    src_ref=ins[w].at[2 * tx + ty], dst_ref=outs[w].at[j],
                    send_sem=send.at[3 * w + j], recv_sem=recv.at[3 * w + j],
                    device_id=(tx, ty, c), device_id_type=MESH))
                cps[-1].start()
        for cp in cps:
            cp.wait()

    return pl.pallas_call(
        body, name=name, in_specs=[ANY] * n, out_specs=[ANY] * n,
        out_shape=[jax.ShapeDtypeStruct((3,) + p.shape[1:], p.dtype) for p in parts],
        scratch_shapes=[pltpu.SemaphoreType.DMA((3 * n,)), pltpu.SemaphoreType.DMA((3 * n,))],
    )(*parts)


def _chip_sum(part, got, *, name):
    S, h, C = part.shape
    tr = math.gcd(h, 256)
    nr = h // tr
    meta = jnp.stack([2 * lax.axis_index("x") + lax.axis_index("y"), lax.axis_index("c")]).astype(jnp.int32)

    def body(m_ref, own_ref, g0_ref, g1_ref, g2_ref, o_ref):
        acc = own_ref[...].astype(F32) + g0_ref[...].astype(F32)
        o_ref[...] = (acc + g1_ref[...].astype(F32)) + g2_ref[...].astype(F32)

    def piece(j):
        return pl.BlockSpec((None, tr, C), lambda r, m: (j, r, 0))

    return pl.pallas_call(
        body, name=name,
        grid_spec=pltpu.PrefetchScalarGridSpec(
            num_scalar_prefetch=1, grid=(nr,),
            in_specs=[pl.BlockSpec((None, tr, C), lambda r, m: (m[0], r, 0)), piece(0), piece(1), piece(2)],
            out_specs=pl.BlockSpec((tr, C), lambda r, m: (m[1] * nr + r, 0))),
        out_shape=jax.ShapeDtypeStruct((2 * h, C), F32),
        compiler_params=_params(("parallel",)),
    )(meta, part, got, got, got)


def _pair_gather(totals, *, name):
    n = len(totals)

    def body(*refs):
        ins, outs = refs[:n], refs[n:2 * n]
        send, recv = refs[2 * n:]
        x, y, c, _ = _place()
        cps = []
        for w in range(n):
            h = totals[w].shape[0] // 2
            cps.append(pltpu.make_async_remote_copy(
                src_ref=ins[w].at[pl.ds(c * h, h), :], dst_ref=outs[w].at[pl.ds(c * h, h), :],
                send_sem=send.at[w], recv_sem=recv.at[w], device_id=(x, y, 1 - c), device_id_type=MESH))
            cps[-1].start()
        for w in range(n):
            h = totals[w].shape[0] // 2
            theirs = outs[w].at[pl.ds((1 - c) * h, h), :]
            pltpu.make_async_remote_copy(
                src_ref=theirs, dst_ref=theirs, send_sem=send.at[w], recv_sem=recv.at[w],
                device_id=(x, y, 1 - c), device_id_type=MESH).wait_recv()
        for cp in cps:
            cp.wait_send()

    return pl.pallas_call(
        body, name=name, in_specs=[ANY] * n, out_specs=[ANY] * n,
        out_shape=[jax.ShapeDtypeStruct(t.shape, t.dtype) for t in totals],
        input_output_aliases={w: w for w in range(n)},
        scratch_shapes=[pltpu.SemaphoreType.DMA((n,)), pltpu.SemaphoreType.DMA((n,))],
    )(*totals)


def _allreduce_small(v, *, name):
    R, C = v.shape
    ND = 8

    def body(v_ref, o_ref, all_ref, send, recv, loc):
        x, y, c, chips = _place()
        me, sib = (x, y, c), (x, y, 1 - c)

        def rows(px, py, pc):
            return all_ref.at[pl.ds((4 * px + 2 * py + pc) * R, R), :]

        def copy(k, block, to, src=None):
            return pltpu.make_async_remote_copy(
                src_ref=rows(*block) if src is None else src, dst_ref=rows(*block),
                send_sem=send.at[k], recv_sem=recv.at[k], device_id=to, device_id_type=MESH)

        mine = pltpu.make_async_copy(v_ref, rows(*me), loc)
        mine.start()
        first = [copy(0, me, sib, src=v_ref)]
        first += [copy(1 + j, me, (*chip, c), src=v_ref) for j, chip in enumerate(chips)]
        for cp in first:
            cp.start()
        passed = [copy(4 + j, (*chip, c), sib) for j, chip in enumerate(chips)]
        for j, chip in enumerate(chips):
            copy(1 + j, (*chip, c), me).wait_recv()
            passed[j].start()
        copy(0, sib, me).wait_recv()
        for j, chip in enumerate(chips):
            copy(4 + j, (*chip, 1 - c), me).wait_recv()
        for cp in first + passed:
            cp.wait_send()
        mine.wait()
        acc = all_ref[0:R, :]
        for d in range(1, ND):
            acc = acc + all_ref[d * R:(d + 1) * R, :]
        o_ref[...] = acc

    vm = pl.BlockSpec(memory_space=pltpu.VMEM)
    return pl.pallas_call(
        body, name=name, in_specs=[vm], out_specs=[vm, vm],
        out_shape=[jax.ShapeDtypeStruct((R, C), F32), jax.ShapeDtypeStruct((ND * R, C), F32)],
        scratch_shapes=[pltpu.SemaphoreType.DMA((7,)), pltpu.SemaphoreType.DMA((7,)), pltpu.SemaphoreType.DMA],
        compiler_params=pltpu.CompilerParams(vmem_limit_bytes=VMEM_LIMIT),
    )(v)[0]


def _adamw(w, g, m, v, *, name):
    R, C = w.shape
    tr = math.gcd(R, 256)

    def body(w_ref, g_ref, m_ref, v_ref, d_ref, nm_ref, nv_ref):
        gg = g_ref[...]
        m_new = ADAM_B1 * m_ref[...] + (1.0 - ADAM_B1) * gg
        v_new = ADAM_B2 * v_ref[...] + (1.0 - ADAM_B2) * (gg * gg)
        m_hat = m_new / (1.0 - ADAM_B1 ** ADAM_STEP)
        v_hat = v_new / (1.0 - ADAM_B2 ** ADAM_STEP)
        d_ref[...] = -ADAM_LR * (m_hat / (jnp.sqrt(v_hat) + ADAM_EPS) + ADAM_WD * w_ref[...])
        nm_ref[...] = m_new
        nv_ref[...] = v_new

    blk = pl.BlockSpec((tr, C), lambda i: (i, 0))
    shp = jax.ShapeDtypeStruct((R, C), F32)
    return pl.pallas_call(
        body, name=name, grid=(R // tr,), in_specs=[blk] * 4, out_specs=[blk] * 3, out_shape=[shp] * 3,
        compiler_params=_params(("parallel",)),
    )(w, g, m, v)


def _to2d(a):
    flat = a.reshape(-1)
    pad = (-flat.shape[0]) % (SUBLANES * LANES)
    if pad:
        flat = jnp.pad(flat, (0, pad))
    return flat.reshape(-1, LANES)


def _small_rows(shape):
    return -(-math.prod(shape) // (SUBLANES * LANES)) * SUBLANES


def _pack_small(parts):
    rows = jnp.concatenate([_to2d(p) for p in parts], axis=0)
    pad = (-rows.shape[0]) % 256
    if pad:
        rows = jnp.concatenate([rows, jnp.zeros((pad, LANES), rows.dtype)], axis=0)
    return rows


def _unpack_small(rows, shapes):
    out, r = [], 0
    for shp in shapes:
        size = math.prod(shp)
        nrow = _small_rows(shp)
        out.append(rows[r:r + nrow].reshape(-1)[:size].reshape(shp))
        r += nrow
    return out


def kernel(x, mem, g_ffn1, w1_gate, w1_up, w1_down, g_mix, w_in, g_v, w_s, b_s, sinks, g_a_out, g_b_out, w_out, g_x, g_mem, w_xq, w_xkv, w_xo, g_ffn2, w2_gate, w2_up, w2_down, g_final, loss_target, m_g_ffn1, m_w1_gate, m_w1_up, m_w1_down, m_g_mix, m_w_in, m_g_v, m_w_s, m_b_s, m_sinks, m_g_a_out, m_g_b_out, m_w_out, m_g_x, m_g_mem, m_w_xq, m_w_xkv, m_w_xo, m_g_ffn2, m_w2_gate, m_w2_up, m_w2_down, m_g_final, v_g_ffn1, v_w1_gate, v_w1_up, v_w1_down, v_g_mix, v_w_in, v_g_v, v_w_s, v_b_s, v_sinks, v_g_a_out, v_g_b_out, v_w_out, v_g_x, v_g_mem, v_w_xq, v_w_xkv, v_w_xo, v_g_ffn2, v_w2_gate, v_w2_up, v_w2_down, v_g_final):
    args = dict(locals())
    Wp = {n: args[n] for n in ALL_W}
    Mp = {n: args["m_" + n] for n in ALL_W}
    Vp = {n: args["v_" + n] for n in ALL_W}

    shards = [Wp[n][0].astype(BF16) for n in BIG]
    gathered = dict(zip(BIG, _allgather_weights(shards, name="gather_weights")))
    W = {}
    for n in ("w1_gate", "w1_up", "w2_gate", "w2_up", "w_xkv"):
        W[n] = gathered[n]
    for n in ("w1_down", "w2_down", "w_out", "w_xq", "w_xo"):
        g = gathered[n]
        W[n] = g.reshape(g.shape[0] * g.shape[1], g.shape[2])
    W["w_in"] = jnp.transpose(gathered["w_in"], (1, 0, 2)).reshape(D_MODEL, IN_COLS)
    for n in SMALL:
        W[n] = Wp[n]
    W["g_final"] = Wp["g_final"].reshape(1, D_MODEL)
    for n in ("w_s", "b_s"):
        W[n] = Wp[n][0]

    loss, dx, grads = _local_step(x[0], mem[0], loss_target[0], W)

    def stacked(n):
        g = grads[n]
        if n == "w_in":
            return jnp.transpose(g.reshape(D_MODEL, N_CHIPS, IN_COLS // N_CHIPS), (1, 0, 2))
        if g.ndim == 2:
            return g.reshape(N_CHIPS, g.shape[0] // N_CHIPS, g.shape[1])
        return g

    local = [stacked(n) for n in BIG]
    from_sib = _pair_exchange(local, name="pair_exchange")
    chip_part = [_pair_sum(g, s, name=f"pair_sum_{n}") for n, g, s in zip(BIG, local, from_sib)]
    from_chips = _chip_exchange(chip_part, name="chip_exchange")
    halves = [_chip_sum(p, s, name=f"chip_sum_{n}") for n, p, s in zip(BIG, chip_part, from_chips)]
    big_grad = dict(zip(BIG, _pair_gather(halves, name="pair_gather")))

    small_shapes = [Wp[n].shape for n in SMALL]
    packed = _pack_small([grads[n].reshape(Wp[n].shape) for n in SMALL] + [loss])
    summed = _allreduce_small(packed, name="allreduce_small")
    small_grad = dict(zip(SMALL, _unpack_small(summed, small_shapes)))
    nrows = sum(_small_rows(s) for s in small_shapes)
    loss_total = summed[nrows, 0]

    grad_out, delta, new_m, new_v = {}, {}, {}, {}
    for n in BIG:
        shp = Wp[n].shape
        grad_out[n] = big_grad[n].reshape(shp)
        d, nm, nv = _adamw(Wp[n][0], big_grad[n], Mp[n][0], Vp[n][0], name=f"adamw_{n}")
        delta[n], new_m[n], new_v[n] = d.reshape(shp), nm.reshape(shp), nv.reshape(shp)
    sw = _pack_small([Wp[n] for n in SMALL])
    sg = _pack_small([small_grad[n] for n in SMALL])
    sm = _pack_small([Mp[n] for n in SMALL])
    sv = _pack_small([Vp[n] for n in SMALL])
    d, nm, nv = _adamw(sw, sg, sm, sv, name="adamw_small")
    for n, dd, mm_, vv_ in zip(SMALL, _unpack_small(d, small_shapes), _unpack_small(nm, small_shapes),
                               _unpack_small(nv, small_shapes)):
        grad_out[n], delta[n], new_m[n], new_v[n] = small_grad[n], dd, mm_, vv_

    return (loss_total, dx[None], *[grad_out[n] for n in ALL_W], *[delta[n] for n in ALL_W],
            *[new_m[n] for n in ALL_W], *[new_v[n] for n in ALL_W])
```

```python
import functools
import math

import jax
import jax.numpy as jnp
from jax import lax
from jax.experimental import pallas as pl
from jax.experimental.pallas import tpu as pltpu

F32 = jnp.float32
BF16 = jnp.bfloat16
MESH = pl.DeviceIdType.MESH

D_MODEL = 2048
D_FF = 5632
D_A = 1024
D_B = 1024
CHUNK = 128
A_GROUPS = 8
HEAD_DIM = 64
B_Q_HEADS = 16
B_KV_HEADS = 2
X_HEADS = 4
X_HEAD_DIM = 512
IN_COLS = 3328
O_Q = 2 * D_A
O_K = O_Q + D_B
O_V = O_K + B_KV_HEADS * HEAD_DIM
N_CHIPS = 4
EPS = 1e-5
NEG = -1e30
ADAM_LR = 0.001
ADAM_B1 = 0.9
ADAM_B2 = 0.999
ADAM_EPS = 1e-08
ADAM_WD = 0.01
ADAM_STEP = 10

V7X_VMEM_BYTES = 64 * 1024 * 1024
VMEM_LIMIT = 56 * 1024 * 1024
LANES = 128
SUBLANES = 8


ANY = pl.BlockSpec(memory_space=pl.ANY)


def _params(sem, vmem=VMEM_LIMIT):
    return pltpu.CompilerParams(dimension_semantics=sem, vmem_limit_bytes=vmem)


def _matmul(pairs, *, M, N, K, tm, tn, tk, a_t=False, b_kind="n", out_kind="n", out_dtype=F32,
            scale=1.0, res=None, order="ij", dep=None, name):
    tm, tn, tk = min(tm, M), min(tn, N), min(tk, K)
    assert M % tm == 0 and N % tn == 0 and K % tk == 0, (name, M, N, K, tm, tn, tk)
    nk = K // tk
    npairs = len(pairs)
    b_t = b_kind in ("t", "st")

    def ij(g0, g1):
        return (g0, g1) if order == "ij" else (g1, g0)

    def a_map(g0, g1, k):
        i, _ = ij(g0, g1)
        return (k, i) if a_t else (i, k)

    a_spec = pl.BlockSpec((tk, tm) if a_t else (tm, tk), a_map)

    b0 = pairs[0][1]
    if b_kind == "n":
        b_spec = pl.BlockSpec((tk, tn), lambda g0, g1, k: (k, ij(g0, g1)[1]))
    elif b_kind == "t":
        b_spec = pl.BlockSpec((tn, tk), lambda g0, g1, k: (ij(g0, g1)[1], k))
    elif b_kind == "sn":
        ns = b0.shape[2]
        assert ns % tn == 0
        nps = ns // tn
        b_spec = pl.BlockSpec((None, tk, tn), lambda g0, g1, k: (ij(g0, g1)[1] // nps, k, ij(g0, g1)[1] % nps))
    else:
        ks = b0.shape[2]
        assert ks % tk == 0
        kps = ks // tk
        b_spec = pl.BlockSpec((None, tn, tk), lambda g0, g1, k: (k // kps, ij(g0, g1)[1], k % kps))

    if out_kind == "n":
        o_spec = pl.BlockSpec((tm, tn), lambda g0, g1, k: ij(g0, g1))
        o_shape = jax.ShapeDtypeStruct((M, N), out_dtype)
    else:
        ns = N // N_CHIPS
        assert ns % tn == 0
        nps_o = ns // tn
        o_spec = pl.BlockSpec((None, tm, tn), lambda g0, g1, k: (ij(g0, g1)[1] // nps_o, ij(g0, g1)[0], ij(g0, g1)[1] % nps_o))
        o_shape = jax.ShapeDtypeStruct((N_CHIPS, M, ns), out_dtype)

    in_specs, args = [], []
    for a, b in pairs:
        in_specs += [a_spec, b_spec]
        args += [a, b]
    if res is not None:
        in_specs.append(pl.BlockSpec((tm, tn), lambda g0, g1, k: ij(g0, g1)))
        args.append(res)
    if dep is not None:
        in_specs.append(ANY)
        args.append(dep)

    dn = (((0,) if a_t else (1,), (1,) if b_t else (0,)), ((), ()))

    def body(*refs):
        pos = 2 * npairs
        res_ref = refs[pos] if res is not None else None
        pos += (res is not None) + (dep is not None)
        o_ref = refs[pos]
        acc_ref = refs[pos + 1] if nk > 1 else None
        part = None
        for p in range(npairs):
            d = lax.dot_general(refs[2 * p][...], refs[2 * p + 1][...], dn, preferred_element_type=F32)
            part = d if part is None else part + d

        def finish(acc):
            r = acc * scale if scale != 1.0 else acc
            if res_ref is not None:
                r = res_ref[...] + r
            o_ref[...] = r.astype(out_dtype)

        if nk == 1:
            finish(part)
        else:
            k = pl.program_id(2)

            @pl.when(k == 0)
            def _():
                acc_ref[...] = part

            @pl.when(k > 0)
            def _():
                acc_ref[...] += part

            @pl.when(k == nk - 1)
            def _():
                finish(acc_ref[...])

    grid = (M // tm, N // tn, nk) if order == "ij" else (N // tn, M // tm, nk)
    return pl.pallas_call(
        body, name=name, grid=grid, in_specs=in_specs, out_specs=o_spec, out_shape=o_shape,
        scratch_shapes=[pltpu.VMEM((tm, tn), F32)] if nk > 1 else [],
        compiler_params=_params(("parallel", "parallel", "arbitrary")),
    )(*args)


def _rstd(x):
    return lax.rsqrt(jnp.mean(x * x, axis=-1, keepdims=True) + EPS)


def _rms_bwd_math(x, g, dy):
    r = _rstd(x)
    gy = dy * g
    xr = x * r
    dx = r * (gy - xr * jnp.mean(gy * xr, axis=-1, keepdims=True))
    return dx, dy * xr


def _rms_fwd(h, g, *, name, tm=512, dep=None):
    T, Dm = h.shape
    tm = min(tm, T)

    def body(h_ref, g_ref, *rest):
        x = h_ref[...]
        rest[-1][...] = (x * _rstd(x) * g_ref[...]).astype(BF16)

    return pl.pallas_call(
        body, name=name, grid=(T // tm,),
        in_specs=[pl.BlockSpec((tm, Dm), lambda i: (i, 0)), pl.BlockSpec((1, Dm), lambda i: (0, 0))]
        + ([ANY] if dep is not None else []),
        out_specs=pl.BlockSpec((tm, Dm), lambda i: (i, 0)),
        out_shape=jax.ShapeDtypeStruct((T, Dm), BF16),
        compiler_params=_params(("parallel",)),
    )(h, g, *([dep] if dep is not None else []))


def _rms_bwd(h, g, dn, dres, *, name, tm=256):
    T, Dm = h.shape
    tm = min(tm, T)
    has_res = dres is not None

    def body(*refs):
        h_ref, g_ref, dn_ref = refs[:3]
        pos = 3
        dres_ref = refs[pos] if has_res else None
        pos += has_res
        dh_ref, dhb_ref, dg_ref = refs[pos:pos + 3]
        dx, dgr = _rms_bwd_math(h_ref[...], g_ref[...], dn_ref[...].astype(F32))
        if has_res:
            dx = dres_ref[...] + dx
        dh_ref[...] = dx
        dhb_ref[...] = dx.astype(BF16)
        part = jnp.sum(dgr, axis=0, keepdims=True)

        @pl.when(pl.program_id(0) == 0)
        def _():
            dg_ref[...] = part

        @pl.when(pl.program_id(0) > 0)
        def _():
            dg_ref[...] += part

    row = pl.BlockSpec((tm, Dm), lambda i: (i, 0))
    vec = pl.BlockSpec((1, Dm), lambda i: (0, 0))
    args = [h, g, dn] + ([dres] if has_res else [])
    return pl.pallas_call(
        body, name=name, grid=(T // tm,),
        in_specs=[row, vec, row] + ([row] if has_res else []),
        out_specs=[row, row, vec],
        out_shape=[jax.ShapeDtypeStruct((T, Dm), F32), jax.ShapeDtypeStruct((T, Dm), BF16),
                   jax.ShapeDtypeStruct((1, Dm), F32)],
        compiler_params=_params(("arbitrary",)),
    )(*args)


def _loss_head(h, g, tgt, *, name, tm=256):
    T, Dm = h.shape
    tm = min(tm, T)

    def body(h_ref, g_ref, t_ref, dh_ref, dhb_ref, dg_ref, loss_ref):
        x = h_ref[...]
        gv = g_ref[...]
        r = _rstd(x)
        diff = x * r * gv - t_ref[...]
        lpart = 0.5 * jnp.sum(jnp.mean(diff * diff, axis=-1, keepdims=True), axis=0, keepdims=True)
        dx, dgr = _rms_bwd_math(x, gv, diff * (1.0 / Dm))
        dh_ref[...] = dx
        dhb_ref[...] = dx.astype(BF16)
        part = jnp.sum(dgr, axis=0, keepdims=True)
        lrow = jnp.broadcast_to(lpart, (1, LANES))

        @pl.when(pl.program_id(0) == 0)
        def _():
            dg_ref[...] = part
            loss_ref[...] = lrow

        @pl.when(pl.program_id(0) > 0)
        def _():
            dg_ref[...] += part
            loss_ref[...] += lrow

    row = pl.BlockSpec((tm, Dm), lambda i: (i, 0))
    vec = pl.BlockSpec((1, Dm), lambda i: (0, 0))
    return pl.pallas_call(
        body, name=name, grid=(T // tm,),
        in_specs=[row, vec, row],
        out_specs=[row, row, vec, pl.BlockSpec((1, LANES), lambda i: (0, 0))],
        out_shape=[jax.ShapeDtypeStruct((T, Dm), F32), jax.ShapeDtypeStruct((T, Dm), BF16),
                   jax.ShapeDtypeStruct((1, Dm), F32), jax.ShapeDtypeStruct((1, LANES), F32)],
        compiler_params=_params(("arbitrary",)),
    )(h, g, tgt)


def _swiglu_up(n, wg, wu, *, name, tm=512):
    T, Dm = n.shape
    S, _, fs = wg.shape
    tm = min(tm, T)

    def body(n_ref, wg_ref, wu_ref, g_ref, u_ref, a_ref):
        x = n_ref[...]
        g = jnp.dot(x, wg_ref[...], preferred_element_type=F32)
        u = jnp.dot(x, wu_ref[...], preferred_element_type=F32)
        g_ref[...] = g.astype(BF16)
        u_ref[...] = u.astype(BF16)
        a_ref[...] = (g * jax.nn.sigmoid(g) * u).astype(BF16)

    wspec = pl.BlockSpec((None, Dm, fs), lambda j, i: (j, 0, 0))
    ospec = pl.BlockSpec((tm, fs), lambda j, i: (i, j))
    oshape = jax.ShapeDtypeStruct((T, S * fs), BF16)
    return pl.pallas_call(
        body, name=name, grid=(S, T // tm),
        in_specs=[pl.BlockSpec((tm, Dm), lambda j, i: (i, 0)), wspec, wspec],
        out_specs=[ospec, ospec, ospec], out_shape=[oshape, oshape, oshape],
        compiler_params=_params(("parallel", "parallel")),
    )(n, wg, wu)


def _swiglu_bwd_act(dhb, wd, G, U, *, name, tm=512, tn=1408):
    T, Dm = dhb.shape
    Fd = wd.shape[0]
    tm, tn = min(tm, T), min(tn, Fd)

    def body(dh_ref, wd_ref, g_ref, u_ref, dg_ref, du_ref):
        da = 0.5 * lax.dot_general(dh_ref[...], wd_ref[...], (((1,), (1,)), ((), ())), preferred_element_type=F32)
        g = g_ref[...].astype(F32)
        u = u_ref[...].astype(F32)
        sg = jax.nn.sigmoid(g)
        dg_ref[...] = (da * u * (sg * (1.0 + g * (1.0 - sg)))).astype(BF16)
        du_ref[...] = (da * (g * sg)).astype(BF16)

    blk = pl.BlockSpec((tm, tn), lambda j, i: (i, j))
    oshape = jax.ShapeDtypeStruct((T, Fd), BF16)
    return pl.pallas_call(
        body, name=name, grid=(Fd // tn, T // tm),
        in_specs=[pl.BlockSpec((tm, Dm), lambda j, i: (i, 0)), pl.BlockSpec((tn, Dm), lambda j, i: (j, 0)), blk, blk],
        out_specs=[blk, blk], out_shape=[oshape, oshape],
        compiler_params=_params(("parallel", "parallel")),
    )(dhb, wd, G, U)


_INV_SQRT2 = 0.7071067811865476
_INV_SQRT2PI = 0.3989422804014327


def _erf(x):
    ax = jnp.abs(x)
    t = 1.0 / (1.0 + 0.3275911 * ax)
    poly = t * (0.254829592 + t * (-0.284496736 + t * (1.421413741 + t * (-1.453152027 + t * 1.061405429))))
    y = 1.0 - poly * jnp.exp(-ax * ax)
    return jnp.where(x < 0, -y, y)


def _gelu_cdf(x):
    return 0.5 * (1.0 + _erf(x * _INV_SQRT2))


def _lane_lt64(shape):
    return lax.broadcasted_iota(jnp.int32, shape, len(shape) - 1) < HEAD_DIM


def _dup_half(x, kv):
    rolled = pltpu.roll(x, HEAD_DIM, 1)
    lo = _lane_lt64(x.shape)
    return jnp.where(lo, x, rolled) if kv == 0 else jnp.where(lo, rolled, x)


HEADS_PER_KV = B_Q_HEADS // B_KV_HEADS
PAIRS = HEADS_PER_KV // 2


def _attn_mask(block):
    shape = (HEADS_PER_KV * CHUNK, 2 * CHUNK)
    qpos = (lax.broadcasted_iota(jnp.int32, shape, 0) & (CHUNK - 1)) + CHUNK
    kpos = lax.broadcasted_iota(jnp.int32, shape, 1)
    diff = qpos - kpos
    first_key = jnp.where(block == 0, CHUNK, 0)
    return (diff >= 0) & (diff < CHUNK) & (kpos >= first_key)


def _stack_heads(tiles, lo):
    parts = []
    for t in tiles:
        parts += [jnp.where(lo, t, 0.0), jnp.where(lo, 0.0, t)]
    return jnp.concatenate(parts, axis=0)


def _unstack_heads(s, lo):
    return [jnp.where(lo, s[2 * p * CHUNK:(2 * p + 1) * CHUNK], s[(2 * p + 1) * CHUNK:(2 * p + 2) * CHUNK])
            for p in range(PAIRS)]


def _stack_sinks(sk_ref, kv):
    return jnp.concatenate([jnp.broadcast_to(sk_ref[:, h:h + 1], (CHUNK, 1))
                            for h in range(kv * HEADS_PER_KV, (kv + 1) * HEADS_PER_KV)], axis=0)


def _sgu_forward(z_ref, gv, wsm, bst):
    zu = z_ref[:, 0:D_A]
    zv = z_ref[:, D_A:2 * D_A]
    u = zu * _gelu_cdf(zu)
    v = zv * _gelu_cdf(zv)
    rv = _rstd(v)
    vn = (v * rv * gv).astype(BF16)
    svs = []
    for g in range(A_GROUPS):
        sl = slice(g * CHUNK, (g + 1) * CHUNK)
        svs.append(jnp.dot(wsm[g], vn[:, sl], preferred_element_type=F32) + bst[:, g:g + 1])
    sv = jnp.concatenate(svs, axis=1)
    return zu, zv, u, v, rv, vn, sv


def _masked_ws(ws_ref):
    tril = lax.broadcasted_iota(jnp.int32, (CHUNK, CHUNK), 0) >= lax.broadcasted_iota(jnp.int32, (CHUNK, CHUNK), 1)
    return [jnp.where(tril, ws_ref[g], 0.0).astype(BF16) for g in range(A_GROUPS)], tril


def _attn_probs(qm, kkd, sink, mask):
    s = lax.dot_general(qm, kkd, (((1,), (1,)), ((), ())), preferred_element_type=F32) * (HEAD_DIM ** -0.5)
    s = jnp.where(mask, s, NEG)
    m = jnp.maximum(jnp.max(s, axis=-1, keepdims=True), sink)
    e = jnp.exp(s - m)
    es = jnp.exp(sink - m)
    inv = 1.0 / (jnp.sum(e, axis=-1, keepdims=True) + es)
    return e * inv, es * inv


def _mixer_fwd(z, gv, ws, bst, sinks, ga, gb, *, name):
    T = z.shape[0]
    nb = T // CHUNK
    kvb = O_K // (2 * CHUNK)

    def body(z_ref, zp_ref, gv_ref, ws_ref, bst_ref, sk_ref, ga_ref, gb_ref, o_ref):
        i = pl.program_id(0)
        wsm, _ = _masked_ws(ws_ref)
        _, _, u, _, _, _, sv = _sgu_forward(z_ref, gv_ref[...], wsm, bst_ref[...])
        ya = u * sv
        o_ref[:, 0:D_A] = (ya * _rstd(ya) * ga_ref[...]).astype(BF16)

        mask = _attn_mask(i)
        kk = jnp.concatenate([zp_ref[:, 0:CHUNK], z_ref[:, O_K:O_V]], axis=0)
        vv = jnp.concatenate([zp_ref[:, CHUNK:2 * CHUNK], z_ref[:, O_V:IN_COLS]], axis=0)
        lo = _lane_lt64((CHUNK, LANES))
        outs = []
        for kv in range(B_KV_HEADS):
            kkd = _dup_half(kk, kv).astype(BF16)
            vvd = _dup_half(vv, kv).astype(BF16)
            q = _stack_heads([z_ref[:, O_Q + (kv * PAIRS + pr) * LANES:O_Q + (kv * PAIRS + pr + 1) * LANES]
                              for pr in range(PAIRS)], lo).astype(BF16)
            p, _ = _attn_probs(q, kkd, _stack_sinks(sk_ref, kv), mask)
            outs += _unstack_heads(jnp.dot(p.astype(BF16), vvd, preferred_element_type=F32), lo)
        yb = jnp.concatenate(outs, axis=1)
        o_ref[:, D_A:D_A + D_B] = (yb * _rstd(yb) * gb_ref[...]).astype(BF16)

    full = lambda shape: pl.BlockSpec(shape, lambda i: (0,) * len(shape))
    return pl.pallas_call(
        body, name=name, grid=(nb,),
        in_specs=[pl.BlockSpec((CHUNK, IN_COLS), lambda i: (i, 0)),
                  pl.BlockSpec((CHUNK, 2 * CHUNK), lambda i: (jnp.maximum(i - 1, 0), kvb)),
                  full((1, D_A)), full((A_GROUPS, CHUNK, CHUNK)), full((CHUNK, A_GROUPS)), full((1, B_Q_HEADS)),
                  full((1, D_A)), full((1, D_B))],
        out_specs=pl.BlockSpec((CHUNK, D_A + D_B), lambda i: (i, 0)),
        out_shape=jax.ShapeDtypeStruct((T, D_A + D_B), BF16),
        compiler_params=_params(("parallel",)),
    )(z, z, gv, ws, bst, sinks, ga, gb)


def _mixer_bwd(z, dyn, gv, ws, bst, sinks, ga, gb, *, name):
    T = z.shape[0]
    nb = T // CHUNK
    kvb = O_K // (2 * CHUNK)
    NT = (((0,), (0,)), ((), ()))

    def body(z_ref, zp_ref, dy_ref, gv_ref, ws_ref, bst_ref, sk_ref, ga_ref, gb_ref,
             dz_ref, dgv_ref, dws_ref, dbst_ref, dsk_ref, dga_ref, dgb_ref, carry_ref, p_ref):
        step = pl.program_id(0)
        i = nb - 1 - step

        @pl.when(step == 0)
        def _():
            carry_ref[...] = jnp.zeros_like(carry_ref)
            dgv_ref[...] = jnp.zeros_like(dgv_ref)
            dws_ref[...] = jnp.zeros_like(dws_ref)
            dbst_ref[...] = jnp.zeros_like(dbst_ref)
            dsk_ref[...] = jnp.zeros_like(dsk_ref)
            dga_ref[...] = jnp.zeros_like(dga_ref)
            dgb_ref[...] = jnp.zeros_like(dgb_ref)

        wsm, tril = _masked_ws(ws_ref)
        gvv = gv_ref[...]
        zu, zv, u, v, rv, vn, sv = _sgu_forward(z_ref, gvv, wsm, bst_ref[...])
        ya = u * sv
        dya, dga_rows = _rms_bwd_math(ya, ga_ref[...], dy_ref[:, 0:D_A].astype(F32))
        dga_ref[...] += jnp.sum(dga_rows, axis=0, keepdims=True)
        du = dya * sv
        dsv = dya * u
        dvn_parts = []
        for g in range(A_GROUPS):
            sl = slice(g * CHUNK, (g + 1) * CHUNK)
            dsv_g = dsv[:, sl]
            dsv_gb = dsv_g.astype(BF16)
            dw = lax.dot_general(dsv_gb, vn[:, sl], (((1,), (1,)), ((), ())), preferred_element_type=F32)
            dws_ref[g] += jnp.where(tril, dw, 0.0)
            dbst_ref[:, g:g + 1] += jnp.sum(dsv_g, axis=1, keepdims=True)
            dvn_parts.append(lax.dot_general(wsm[g], dsv_gb, NT, preferred_element_type=F32))
        dvn = jnp.concatenate(dvn_parts, axis=1)
        dv, dgv_rows = _rms_bwd_math(v, gvv, dvn)
        dgv_ref[...] += jnp.sum(dgv_rows, axis=0, keepdims=True)
        dz_ref[:, 0:D_A] = (du * (_gelu_cdf(zu) + zu * jnp.exp(-0.5 * zu * zu) * _INV_SQRT2PI)).astype(BF16)
        dz_ref[:, D_A:2 * D_A] = (dv * (_gelu_cdf(zv) + zv * jnp.exp(-0.5 * zv * zv) * _INV_SQRT2PI)).astype(BF16)

        mask = _attn_mask(i)
        kk = jnp.concatenate([zp_ref[:, 0:CHUNK], z_ref[:, O_K:O_V]], axis=0)
        vv = jnp.concatenate([zp_ref[:, CHUNK:2 * CHUNK], z_ref[:, O_V:IN_COLS]], axis=0)
        lo = _lane_lt64((CHUNK, LANES))
        kkd = [_dup_half(kk, kv).astype(BF16) for kv in range(B_KV_HEADS)]
        vvd = [_dup_half(vv, kv).astype(BF16) for kv in range(B_KV_HEADS)]
        outs, qs, psinks = [], [], []
        for kv in range(B_KV_HEADS):
            qs.append(_stack_heads([z_ref[:, O_Q + (kv * PAIRS + pr) * LANES:O_Q + (kv * PAIRS + pr + 1) * LANES]
                                    for pr in range(PAIRS)], lo).astype(BF16))
            p, ps = _attn_probs(qs[kv], kkd[kv], _stack_sinks(sk_ref, kv), mask)
            p_ref[kv] = p
            psinks.append(ps)
            outs += _unstack_heads(jnp.dot(p.astype(BF16), vvd[kv], preferred_element_type=F32), lo)
        yb = jnp.concatenate(outs, axis=1)
        dyb, dgb_rows = _rms_bwd_math(yb, gb_ref[...], dy_ref[:, D_A:D_A + D_B].astype(F32))
        dgb_ref[...] += jnp.sum(dgb_rows, axis=0, keepdims=True)

        dkk, dvv = [], []
        for kv in range(B_KV_HEADS):
            do = _stack_heads([dyb[:, (kv * PAIRS + pr) * LANES:(kv * PAIRS + pr + 1) * LANES]
                               for pr in range(PAIRS)], lo).astype(BF16)
            p = p_ref[kv]
            dvv.append(lax.dot_general(p.astype(BF16), do, NT, preferred_element_type=F32))
            dp = lax.dot_general(do, vvd[kv], (((1,), (1,)), ((), ())), preferred_element_type=F32)
            delta = jnp.sum(p * dp, axis=-1, keepdims=True)
            dsink = -psinks[kv] * delta
            for g in range(HEADS_PER_KV):
                h = kv * HEADS_PER_KV + g
                dsk_ref[:, h:h + 1] += jnp.sum(dsink[g * CHUNK:(g + 1) * CHUNK], axis=0, keepdims=True)
            ds = (p * (dp - delta) * (HEAD_DIM ** -0.5)).astype(BF16)
            dq = _unstack_heads(jnp.dot(ds, kkd[kv], preferred_element_type=F32), lo)
            for pr in range(PAIRS):
                c0 = O_Q + (kv * PAIRS + pr) * LANES
                dz_ref[:, c0:c0 + LANES] = dq[pr].astype(BF16)
            dkk.append(lax.dot_general(ds, qs[kv], NT, preferred_element_type=F32))

        def fold(parts):
            tot = [t + pltpu.roll(t, HEAD_DIM, 1) for t in parts]
            return jnp.where(_lane_lt64(tot[0].shape), tot[0], tot[1])

        dk_all = fold(dkk)
        dv_all = fold(dvv)
        dz_ref[:, O_K:O_V] = (dk_all[CHUNK:] + carry_ref[:, 0:CHUNK]).astype(BF16)
        dz_ref[:, O_V:IN_COLS] = (dv_all[CHUNK:] + carry_ref[:, CHUNK:2 * CHUNK]).astype(BF16)
        carry_ref[:, 0:CHUNK] = dk_all[:CHUNK]
        carry_ref[:, CHUNK:2 * CHUNK] = dv_all[:CHUNK]

    full = lambda shape: pl.BlockSpec(shape, lambda s: (0,) * len(shape))
    rev = lambda s: nb - 1 - s
    return pl.pallas_call(
        body, name=name, grid=(nb,),
        in_specs=[pl.BlockSpec((CHUNK, IN_COLS), lambda s: (rev(s), 0)),
                  pl.BlockSpec((CHUNK, 2 * CHUNK), lambda s: (jnp.maximum(rev(s) - 1, 0), kvb)),
                  pl.BlockSpec((CHUNK, D_A + D_B), lambda s: (rev(s), 0)),
                  full((1, D_A)), full((A_GROUPS, CHUNK, CHUNK)), full((CHUNK, A_GROUPS)), full((1, B_Q_HEADS)),
                  full((1, D_A)), full((1, D_B))],
        out_specs=[pl.BlockSpec((CHUNK, IN_COLS), lambda s: (rev(s), 0)),
                   full((1, D_A)), full((A_GROUPS, CHUNK, CHUNK)), full((CHUNK, A_GROUPS)), full((1, B_Q_HEADS)),
                   full((1, D_A)), full((1, D_B))],
        out_shape=[jax.ShapeDtypeStruct((T, IN_COLS), BF16), jax.ShapeDtypeStruct((1, D_A), F32),
                   jax.ShapeDtypeStruct((A_GROUPS, CHUNK, CHUNK), F32), jax.ShapeDtypeStruct((CHUNK, A_GROUPS), F32),
                   jax.ShapeDtypeStruct((1, B_Q_HEADS), F32), jax.ShapeDtypeStruct((1, D_A), F32),
                   jax.ShapeDtypeStruct((1, D_B), F32)],
        scratch_shapes=[pltpu.VMEM((CHUNK, 2 * CHUNK), F32), pltpu.VMEM((B_KV_HEADS, HEADS_PER_KV * CHUNK, 2 * CHUNK), F32)],
        compiler_params=_params(("arbitrary",)),
    )(z, z, dyn, gv, ws, bst, sinks, ga, gb)


def _xattn_probs(qh, kh):
    s = lax.dot_general(qh, kh, (((1,), (1,)), ((), ())), preferred_element_type=F32) * (X_HEAD_DIM ** -0.5)
    e = jnp.exp(s - jnp.max(s, axis=-1, keepdims=True))
    return e / jnp.sum(e, axis=-1, keepdims=True)


def _xattn_fwd(q, kvm, *, name, tm=512):
    T = q.shape[0]
    Mm = kvm.shape[0]
    tm = min(tm, T)

    def body(q_ref, kv_ref, o_ref):
        for h in range(X_HEADS):
            sl = slice(h * X_HEAD_DIM, (h + 1) * X_HEAD_DIM)
            kh = kv_ref[:, sl].astype(BF16)
            vh = kv_ref[:, D_MODEL + h * X_HEAD_DIM:D_MODEL + (h + 1) * X_HEAD_DIM].astype(BF16)
            p = _xattn_probs(q_ref[:, sl], kh)
            o_ref[:, sl] = jnp.dot(p.astype(BF16), vh, preferred_element_type=F32).astype(BF16)

    return pl.pallas_call(
        body, name=name, grid=(T // tm,),
        in_specs=[pl.BlockSpec((tm, D_MODEL), lambda i: (i, 0)), pl.BlockSpec((Mm, 2 * D_MODEL), lambda i: (0, 0))],
        out_specs=pl.BlockSpec((tm, D_MODEL), lambda i: (i, 0)),
        out_shape=jax.ShapeDtypeStruct((T, D_MODEL), BF16),
        compiler_params=_params(("parallel",)),
    )(q, kvm)


def _xattn_bwd(q, kvm, do, *, name, tm=512):
    T = q.shape[0]
    Mm = kvm.shape[0]
    tm = min(tm, T)
    NT = (((0,), (0,)), ((), ()))

    def body(q_ref, kv_ref, do_ref, dq_ref, dkv_ref):
        @pl.when(pl.program_id(0) == 0)
        def _():
            dkv_ref[...] = jnp.zeros_like(dkv_ref)

        for h in range(X_HEADS):
            sl = slice(h * X_HEAD_DIM, (h + 1) * X_HEAD_DIM)
            slv = slice(D_MODEL + h * X_HEAD_DIM, D_MODEL + (h + 1) * X_HEAD_DIM)
            kh = kv_ref[:, sl].astype(BF16)
            vh = kv_ref[:, slv].astype(BF16)
            qh = q_ref[:, sl]
            doh = do_ref[:, sl]
            p = _xattn_probs(qh, kh)
            dkv_ref[:, slv] += lax.dot_general(p.astype(BF16), doh, NT, preferred_element_type=F32)
            dp = lax.dot_general(doh, vh, (((1,), (1,)), ((), ())), preferred_element_type=F32)
            ds = (p * (dp - jnp.sum(p * dp, axis=-1, keepdims=True)) * (X_HEAD_DIM ** -0.5)).astype(BF16)
            dq_ref[:, sl] = jnp.dot(ds, kh, preferred_element_type=F32).astype(BF16)
            dkv_ref[:, sl] += lax.dot_general(ds, qh, NT, preferred_element_type=F32)

    row = pl.BlockSpec((tm, D_MODEL), lambda i: (i, 0))
    kvs = pl.BlockSpec((Mm, 2 * D_MODEL), lambda i: (0, 0))
    return pl.pallas_call(
        body, name=name, grid=(T // tm,),
        in_specs=[row, kvs, row], out_specs=[row, kvs],
        out_shape=[jax.ShapeDtypeStruct((T, D_MODEL), BF16), jax.ShapeDtypeStruct((Mm, 2 * D_MODEL), F32)],
        compiler_params=_params(("arbitrary",)),
    )(q, kvm, do)


def _swiglu_bwd_weights(tag, n, G, U, A, wd, dhb):
    T = n.shape[0]
    dG, dU = _swiglu_bwd_act(dhb, wd, G, U, name=f"{tag}_bwd_act")
    dwd = _matmul([(A, dhb)], M=D_FF, N=D_MODEL, K=T, tm=1408, tn=1024, tk=2048, a_t=True, out_dtype=BF16,
                  scale=0.5, name=f"{tag}_dwd")
    dwg = _matmul([(n, dG)], M=D_MODEL, N=D_FF, K=T, tm=1024, tn=1408, tk=2048, a_t=True, out_kind="s",
                  out_dtype=BF16, order="ji", name=f"{tag}_dwg")
    dwu = _matmul([(n, dU)], M=D_MODEL, N=D_FF, K=T, tm=1024, tn=1408, tk=2048, a_t=True, out_kind="s",
                  out_dtype=BF16, order="ji", name=f"{tag}_dwu")
    return dG, dU, dwg, dwu, dwd


def _swiglu_bwd_input(tag, hin, g_norm, dG, dU, wg, wu, dh, dep):
    T = hin.shape[0]
    dn = _matmul([(dG, wg), (dU, wu)], M=T, N=D_MODEL, K=D_FF, tm=512, tn=D_MODEL, tk=1408, b_kind="st",
                 dep=dep, name=f"{tag}_dn")
    return _rms_bwd(hin, g_norm, dn, dh, name=f"{tag}_norm_bwd")


GROUP_FFN1 = ["w1_gate", "w1_up", "w1_down"]
GROUP_MID = ["w_in", "w_out", "w_xq", "w_xkv", "w_xo"]
GROUP_FFN2 = ["w2_gate", "w2_up", "w2_down"]


def _local_step(x, mem, tgt, W, comm):
    T = x.shape[0]
    Mm = mem.shape[0]
    mm = functools.partial(_matmul)

    W = {**W, **comm.weights_ffn1()}
    n1 = _rms_fwd(x, W["g_ffn1"], dep=comm.start_rest(), name="f_norm1")
    G1, U1, A1 = _swiglu_up(n1, W["w1_gate"], W["w1_up"], name="f_ffn1_up")
    h1 = mm([(A1, W["w1_down"])], M=T, N=D_MODEL, K=D_FF, tm=512, tn=D_MODEL, tk=1408, scale=0.5, res=x,
            name="f_ffn1_down")
    n2 = _rms_fwd(h1, W["g_mix"], dep=comm.rest_landed(after=h1), name="f_norm2")
    W = {**W, **comm.rest_ready(after=n2)}
    z = mm([(n2, W["w_in"])], M=T, N=IN_COLS, K=D_MODEL, tm=512, tn=IN_COLS // 2, tk=D_MODEL, name="f_w_in")
    bst = jnp.transpose(W["b_s"])
    yn = _mixer_fwd(z, W["g_v"], W["w_s"], bst, W["sinks"], W["g_a_out"], W["g_b_out"], name="f_mixer")
    h2 = mm([(yn, W["w_out"])], M=T, N=D_MODEL, K=D_MODEL, tm=512, tn=D_MODEL, tk=D_MODEL, res=h1, name="f_w_out")
    n3 = _rms_fwd(h2, W["g_x"], name="f_norm3")
    memn = _rms_fwd(mem, W["g_mem"], name="f_norm_mem")
    q3 = mm([(n3, W["w_xq"])], M=T, N=D_MODEL, K=D_MODEL, tm=512, tn=D_MODEL, tk=D_MODEL, out_dtype=BF16,
            name="f_w_xq")
    kvm = mm([(memn, W["w_xkv"])], M=Mm, N=2 * D_MODEL, K=D_MODEL, tm=Mm, tn=1024, tk=D_MODEL, b_kind="sn",
             name="f_w_xkv")
    o3 = _xattn_fwd(q3, kvm, name="f_xattn")
    h3 = mm([(o3, W["w_xo"])], M=T, N=D_MODEL, K=D_MODEL, tm=512, tn=D_MODEL, tk=D_MODEL, res=h2, name="f_w_xo")
    n4 = _rms_fwd(h3, W["g_ffn2"], name="f_norm4")
    G2, U2, A2 = _swiglu_up(n4, W["w2_gate"], W["w2_up"], name="f_ffn2_up")
    h4 = mm([(A2, W["w2_down"])], M=T, N=D_MODEL, K=D_FF, tm=512, tn=D_MODEL, tk=1408, scale=0.5, res=h3,
            name="f_ffn2_down")

    grads = {}
    dh4, dh4b, grads["g_final"], loss = _loss_head(h4, W["g_final"], tgt, name="loss_head")
    dG2, dU2, dwg, dwu, dwd = _swiglu_bwd_weights("b_ffn2", n4, G2, U2, A2, W["w2_down"], dh4b)
    tok = comm.reduce_start("ffn2", {"w2_gate": dwg, "w2_up": dwu, "w2_down": dwd})
    dh3, dh3b, grads["g_ffn2"] = _swiglu_bwd_input("b_ffn2", h3, W["g_ffn2"], dG2, dU2, W["w2_gate"], W["w2_up"],
                                                   dh4, tok)

    mid = {}
    do3 = mm([(dh3b, W["w_xo"])], M=T, N=D_MODEL, K=D_MODEL, tm=512, tn=D_MODEL, tk=D_MODEL, b_kind="t",
             out_dtype=BF16, name="b_do3")
    mid["w_xo"] = mm([(o3, dh3b)], M=D_MODEL, N=D_MODEL, K=T, tm=1024, tn=D_MODEL, tk=1024, a_t=True,
                       out_dtype=BF16, name="b_dw_xo")
    dq3, dkvm = _xattn_bwd(q3, kvm, do3, name="b_xattn")
    mid["w_xq"] = mm([(n3, dq3)], M=D_MODEL, N=D_MODEL, K=T, tm=1024, tn=D_MODEL, tk=1024, a_t=True,
                       out_dtype=BF16, name="b_dw_xq")
    dn3 = mm([(dq3, W["w_xq"])], M=T, N=D_MODEL, K=D_MODEL, tm=512, tn=D_MODEL, tk=D_MODEL, b_kind="t",
             name="b_dn3")
    dh2, dh2b, grads["g_x"] = _rms_bwd(h2, W["g_x"], dn3, dh3, name="b_norm3")
    dkvmb = dkvm.astype(BF16)
    mid["w_xkv"] = mm([(memn, dkvmb)], M=D_MODEL, N=2 * D_MODEL, K=Mm, tm=D_MODEL, tn=1024, tk=Mm, a_t=True,
                        out_kind="s", out_dtype=BF16, name="b_dw_xkv")
    dmemn = mm([(dkvmb, W["w_xkv"])], M=Mm, N=D_MODEL, K=2 * D_MODEL, tm=Mm, tn=D_MODEL, tk=1024, b_kind="st",
               name="b_dmemn")
    _, _, grads["g_mem"] = _rms_bwd(mem, W["g_mem"], dmemn, None, name="b_norm_mem")
    comm.reduce_finish("ffn2", after=dh2b)

    dyn = mm([(dh2b, W["w_out"])], M=T, N=D_MODEL, K=D_MODEL, tm=512, tn=D_MODEL, tk=D_MODEL, b_kind="t",
             out_dtype=BF16, name="b_dyn")
    mid["w_out"] = mm([(yn, dh2b)], M=D_MODEL, N=D_MODEL, K=T, tm=1024, tn=D_MODEL, tk=1024, a_t=True,
                        out_dtype=BF16, name="b_dw_out")
    dz, grads["g_v"], grads["w_s"], dbst, grads["sinks"], grads["g_a_out"], grads["g_b_out"] = _mixer_bwd(
        z, dyn, W["g_v"], W["w_s"], bst, W["sinks"], W["g_a_out"], W["g_b_out"], name="b_mixer")
    grads["b_s"] = jnp.transpose(dbst)
    mid["w_in"] = mm([(n2, dz)], M=D_MODEL, N=IN_COLS, K=T, tm=1024, tn=IN_COLS, tk=512, a_t=True,
                     out_dtype=BF16, name="b_dw_in")
    tok = comm.reduce_start("mid", mid)
    dn2 = mm([(dz, W["w_in"])], M=T, N=D_MODEL, K=IN_COLS, tm=512, tn=D_MODEL, tk=IN_COLS, b_kind="t",
             dep=tok, name="b_dn2")
    dh1, dh1b, grads["g_mix"] = _rms_bwd(h1, W["g_mix"], dn2, dh2, name="b_norm2")

    dG1, dU1, dwg, dwu, dwd = _swiglu_bwd_weights("b_ffn1", n1, G1, U1, A1, W["w1_down"], dh1b)
    comm.reduce_finish("mid", after=dwu)
    tok = comm.reduce_start("ffn1", {"w1_gate": dwg, "w1_up": dwu, "w1_down": dwd})
    dx, _, grads["g_ffn1"] = _swiglu_bwd_input("b_ffn1", x, W["g_ffn1"], dG1, dU1, W["w1_gate"], W["w1_up"], dh1, tok)
    comm.reduce_finish("ffn1", after=dx)
    return loss, dx, grads


BIG = ["w1_gate", "w1_up", "w1_down", "w_in", "w_out", "w_xq", "w_xkv", "w_xo", "w2_gate", "w2_up", "w2_down"]
SMALL = ["g_ffn1", "g_mix", "g_v", "w_s", "b_s", "sinks", "g_a_out", "g_b_out", "g_x", "g_mem", "g_ffn2", "g_final"]
ALL_W = ["g_ffn1", "w1_gate", "w1_up", "w1_down", "g_mix", "w_in", "g_v", "w_s", "b_s", "sinks", "g_a_out",
         "g_b_out", "w_out", "g_x", "g_mem", "w_xq", "w_xkv", "w_xo", "g_ffn2", "w2_gate", "w2_up", "w2_down",
         "g_final"]
ANY = pl.BlockSpec(memory_space=pl.ANY)


def _place():
    x, y, c = lax.axis_index("x"), lax.axis_index("y"), lax.axis_index("c")
    chips = [(1 - x, y), (x, 1 - y), (1 - x, 1 - y)]
    return x, y, c, chips


def _allgather_weights(shards, *, name):
    n = len(shards)

    def body(*refs):
        ins, outs = refs[:n], refs[n:2 * n]
        send, recv, loc = refs[2 * n:]
        x, y, c, chips = _place()
        me = 2 * x + y
        sib = (x, y, 1 - c)

        def half(w, slot, hc):
            h = shards[w].shape[0] // 2
            return outs[w].at[slot, pl.ds(hc * h, h), :]

        def copy(w, k, slot, hc, to, src=None):
            return pltpu.make_async_remote_copy(
                src_ref=half(w, slot, hc) if src is None else src, dst_ref=half(w, slot, hc),
                send_sem=send.at[6 * w + k], recv_sem=recv.at[6 * w + k], device_id=to, device_id_type=MESH)

        own = [pltpu.make_async_remote_copy(
            src_ref=ins[w], dst_ref=outs[w].at[me], send_sem=loc.at[w], recv_sem=loc.at[n + w],
            device_id=sib, device_id_type=MESH) for w in range(n)]
        for cp in own:
            cp.start()
        first = []
        for w in range(n):
            h = shards[w].shape[0] // 2
            for j, (tx, ty) in enumerate(chips):
                first.append(copy(w, j, me, c, (tx, ty, c), src=ins[w].at[pl.ds(c * h, h), :]))
                first[-1].start()
        passed = []
        for w in range(n):
            for j, (tx, ty) in enumerate(chips):
                slot = 2 * tx + ty
                copy(w, j, slot, c, (tx, ty, c)).wait_recv()
                passed.append(copy(w, 3 + j, slot, c, sib))
                passed[-1].start()
        for w in range(n):
            for j, (tx, ty) in enumerate(chips):
                copy(w, 3 + j, 2 * tx + ty, 1 - c, sib).wait_recv()
        for cp in first + passed:
            cp.wait_send()
        for cp in own:
            cp.wait()

    return pl.pallas_call(
        body, name=name, in_specs=[ANY] * n, out_specs=[ANY] * n,
        out_shape=[jax.ShapeDtypeStruct((N_CHIPS,) + s.shape, s.dtype) for s in shards],
        scratch_shapes=[pltpu.SemaphoreType.DMA((6 * n,)), pltpu.SemaphoreType.DMA((6 * n,)),
                        pltpu.SemaphoreType.DMA((2 * n,))],
    )(*shards)


def _pair_exchange(grads, *, name):
    n = len(grads)

    def body(*refs):
        ins, outs = refs[:n], refs[n:2 * n]
        send, recv = refs[2 * n:]
        x, y, c, _ = _place()
        cps = []
        for w in range(n):
            h = grads[w].shape[1] // 2
            cps.append(pltpu.make_async_remote_copy(
                src_ref=ins[w].at[:, pl.ds((1 - c) * h, h), :], dst_ref=outs[w],
                send_sem=send.at[w], recv_sem=recv.at[w], device_id=(x, y, 1 - c), device_id_type=MESH))
            cps[-1].start()
        for cp in cps:
            cp.wait()

    return pl.pallas_call(
        body, name=name, in_specs=[ANY] * n, out_specs=[ANY] * n,
        out_shape=[jax.ShapeDtypeStruct((N_CHIPS, g.shape[1] // 2, g.shape[2]), g.dtype) for g in grads],
        scratch_shapes=[pltpu.SemaphoreType.DMA((n,)), pltpu.SemaphoreType.DMA((n,))],
    )(*grads)


def _pair_sum(g, got, *, name):
    S, R, C = g.shape
    h = R // 2
    tr = math.gcd(h, 256)
    nr = h // tr

    def body(a_ref, b_ref, o_ref):
        o_ref[...] = (a_ref[...].astype(F32) + b_ref[...].astype(F32)).astype(BF16)

    return pl.pallas_call(
        body, name=name, grid=(S, nr),
        in_specs=[pl.BlockSpec((None, tr, C), lambda s, r: (s, lax.axis_index("c") * nr + r, 0)),
                  pl.BlockSpec((None, tr, C), lambda s, r: (s, r, 0))],
        out_specs=pl.BlockSpec((None, tr, C), lambda s, r: (s, r, 0)),
        out_shape=jax.ShapeDtypeStruct((S, h, C), BF16),
        compiler_params=_params(("parallel", "parallel")),
    )(g, got)


def _chip_sum(part, got, *, name):
    S, h, C = part.shape
    tr = math.gcd(h, 256)
    nr = h // tr

    def body(own_ref, g0_ref, g1_ref, g2_ref, o_ref):
        acc = own_ref[...].astype(F32) + g0_ref[...].astype(F32)
        o_ref[...] = (acc + g1_ref[...].astype(F32)) + g2_ref[...].astype(F32)

    def piece(j):
        return pl.BlockSpec((None, tr, C), lambda r: (j, r, 0))

    return pl.pallas_call(
        body, name=name, grid=(nr,),
        in_specs=[pl.BlockSpec((None, tr, C), lambda r: (2 * lax.axis_index("x") + lax.axis_index("y"), r, 0)),
                  piece(0), piece(1), piece(2)],
        out_specs=pl.BlockSpec((tr, C), lambda r: (lax.axis_index("c") * nr + r, 0)),
        out_shape=jax.ShapeDtypeStruct((2 * h, C), F32),
        compiler_params=_params(("parallel",)),
    )(part, got, got, got)


def _pair_gather(totals, *, name):
    n = len(totals)

    def body(*refs):
        ins, outs = refs[:n], refs[n:2 * n]
        send, recv = refs[2 * n:]
        x, y, c, _ = _place()
        cps = []
        for w in range(n):
            h = totals[w].shape[0] // 2
            cps.append(pltpu.make_async_remote_copy(
                src_ref=ins[w].at[pl.ds(c * h, h), :], dst_ref=outs[w].at[pl.ds(c * h, h), :],
                send_sem=send.at[w], recv_sem=recv.at[w], device_id=(x, y, 1 - c), device_id_type=MESH))
            cps[-1].start()
        for w in range(n):
            h = totals[w].shape[0] // 2
            theirs = outs[w].at[pl.ds((1 - c) * h, h), :]
            pltpu.make_async_remote_copy(
                src_ref=theirs, dst_ref=theirs, send_sem=send.at[w], recv_sem=recv.at[w],
                device_id=(x, y, 1 - c), device_id_type=MESH).wait_recv()
        for cp in cps:
            cp.wait_send()

    return pl.pallas_call(
        body, name=name, in_specs=[ANY] * n, out_specs=[ANY] * n,
        out_shape=[jax.ShapeDtypeStruct(t.shape, t.dtype) for t in totals],
        input_output_aliases={w: w for w in range(n)},
        scratch_shapes=[pltpu.SemaphoreType.DMA((n,)), pltpu.SemaphoreType.DMA((n,))],
    )(*totals)


def _allreduce_small(v, *, name):
    R, C = v.shape
    ND = 8

    def body(v_ref, o_ref, all_ref, send, recv, loc):
        x, y, c, chips = _place()
        me, sib = (x, y, c), (x, y, 1 - c)

        def rows(px, py, pc):
            return all_ref.at[pl.ds((4 * px + 2 * py + pc) * R, R), :]

        def copy(k, block, to, src=None):
            return pltpu.make_async_remote_copy(
                src_ref=rows(*block) if src is None else src, dst_ref=rows(*block),
                send_sem=send.at[k], recv_sem=recv.at[k], device_id=to, device_id_type=MESH)

        mine = pltpu.make_async_copy(v_ref, rows(*me), loc)
        mine.start()
        first = [copy(0, me, sib, src=v_ref)]
        first += [copy(1 + j, me, (*chip, c), src=v_ref) for j, chip in enumerate(chips)]
        for cp in first:
            cp.start()
        passed = [copy(4 + j, (*chip, c), sib) for j, chip in enumerate(chips)]
        for j, chip in enumerate(chips):
            copy(1 + j, (*chip, c), me).wait_recv()
            passed[j].start()
        copy(0, sib, me).wait_recv()
        for j, chip in enumerate(chips):
            copy(4 + j, (*chip, 1 - c), me).wait_recv()
        for cp in first + passed:
            cp.wait_send()
        mine.wait()
        acc = all_ref[0:R, :]
        for d in range(1, ND):
            acc = acc + all_ref[d * R:(d + 1) * R, :]
        o_ref[...] = acc

    vm = pl.BlockSpec(memory_space=pltpu.VMEM)
    return pl.pallas_call(
        body, name=name, in_specs=[vm], out_specs=[vm, vm],
        out_shape=[jax.ShapeDtypeStruct((R, C), F32), jax.ShapeDtypeStruct((ND * R, C), F32)],
        scratch_shapes=[pltpu.SemaphoreType.DMA((7,)), pltpu.SemaphoreType.DMA((7,)), pltpu.SemaphoreType.DMA],
        compiler_params=pltpu.CompilerParams(vmem_limit_bytes=VMEM_LIMIT),
    )(v)[0]


HBM = pl.BlockSpec(memory_space=pltpu.HBM)
SEM = pl.BlockSpec(memory_space=pltpu.SEMAPHORE)
EFFECT = pltpu.SideEffectType.DATAFLOW_SIDE_EFFECTING


def _remote(src, dst, send, recv, k, to):
    return pltpu.make_async_remote_copy(src_ref=src, dst_ref=dst, send_sem=send.at[k], recv_sem=recv.at[k],
                                        device_id=to, device_id_type=MESH)


def _split_start(bufs, plan, ncopies, *, name):
    nb = len(bufs)

    def body(*refs):
        send, recv, token = refs[nb], refs[nb + 1], refs[-1]
        for k, (src, dst, to) in enumerate(plan(refs[:nb])):
            _remote(src, dst, send, recv, k, to).start()
        token[...] = jnp.zeros_like(token)

    outs = pl.pallas_call(
        body, name=name,
        out_shape=(pltpu.SemaphoreType.DMA((ncopies,)), pltpu.SemaphoreType.DMA((ncopies,)),
                   *[pltpu.HBM(b.shape, b.dtype) for b in bufs], jax.ShapeDtypeStruct((SUBLANES, LANES), F32)),
        in_specs=[HBM] * nb,
        out_specs=(SEM, SEM, *[HBM] * nb, pl.BlockSpec(memory_space=pltpu.VMEM)),
        input_output_aliases={i: 2 + i for i in range(nb)},
        compiler_params=pltpu.CompilerParams(has_side_effects=EFFECT),
    )(*[pltpu.with_memory_space_constraint(b, pltpu.HBM) for b in bufs])
    return outs[0], outs[1], list(outs[2:2 + nb]), outs[-1]


def _split_wait(started, plan, after, *, name):
    send, recv, bufs, _ = started
    nb = len(bufs)

    def body(*refs):
        send_sem, recv_sem = refs[nb], refs[nb + 1]
        for k, (src, dst, to) in enumerate(plan(refs[:nb])):
            cp = _remote(src, dst, send_sem, recv_sem, k, to)
            cp.wait_send()
            cp.wait_recv()

    outs = pl.pallas_call(
        body, name=name,
        out_shape=tuple(pltpu.HBM(b.shape, b.dtype) for b in bufs),
        in_specs=[HBM] * nb + [SEM, SEM, ANY], out_specs=tuple([HBM] * nb),
        input_output_aliases={i: i for i in range(nb)},
        compiler_params=pltpu.CompilerParams(has_side_effects=EFFECT),
    )(*bufs, send, recv, after)
    return list(outs)


def _gather_chip_plan(shapes):
    n = len(shapes)

    def plan(refs):
        srcs, lands = refs[:n], refs[n:]
        x, y, c, chips = _place()
        out = []
        for w in range(n):
            h = shapes[w][0] // 2
            for tx, ty in chips:
                out.append((srcs[w].at[pl.ds(c * h, h), :], lands[w].at[2 * x + y, pl.ds(c * h, h), :], (tx, ty, c)))
        return out

    return plan


def _gather_pair_plan(shapes):
    n = len(shapes)

    def plan(refs):
        srcs, lands = refs[:n], refs[n:]
        x, y, c, chips = _place()
        out = []
        for w in range(n):
            h = shapes[w][0] // 2
            for tx, ty in chips:
                half = lands[w].at[2 * tx + ty, pl.ds(c * h, h), :]
                out.append((half, half, (x, y, 1 - c)))
            out.append((srcs[w], lands[w].at[2 * x + y], (x, y, 1 - c)))
        return out

    return plan


def _reduce_chip_plan(n):
    def plan(refs):
        parts, lands = refs[:n], refs[n:]
        x, y, c, chips = _place()
        return [(parts[w].at[2 * tx + ty], lands[w].at[j], (tx, ty, c))
                for w in range(n) for j, (tx, ty) in enumerate(chips)]

    return plan


def _as_operands(gathered):
    out = {}
    for n, g in gathered.items():
        if n in ("w1_gate", "w1_up", "w2_gate", "w2_up", "w_xkv"):
            out[n] = g
        elif n == "w_in":
            out[n] = jnp.transpose(g, (1, 0, 2)).reshape(D_MODEL, IN_COLS)
        else:
            out[n] = g.reshape(g.shape[0] * g.shape[1], g.shape[2])
    return out


def _by_owner(n, g):
    if n == "w_in":
        return jnp.transpose(g.reshape(D_MODEL, N_CHIPS, IN_COLS // N_CHIPS), (1, 0, 2))
    if g.ndim == 2:
        return g.reshape(N_CHIPS, g.shape[0] // N_CHIPS, g.shape[1])
    return g


class _Comm:
    def __init__(self, shards):
        self.shards = shards
        self.total = {}
        self._rest = GROUP_MID + GROUP_FFN2
        self._shapes = [shards[n].shape for n in self._rest]
        self._flight = {}

    def weights_ffn1(self):
        got = _allgather_weights([self.shards[n] for n in GROUP_FFN1], name="gather_ffn1")
        return _as_operands(dict(zip(GROUP_FFN1, got)))

    def start_rest(self):
        srcs = [self.shards[n] for n in self._rest]
        lands = [lax.empty((N_CHIPS,) + s.shape, s.dtype) for s in srcs]
        self._flight["rest"] = _split_start(srcs + lands, _gather_chip_plan(self._shapes), 3 * len(srcs),
                                            name="gather_rest_chips_start")
        return self._flight["rest"][3]

    def rest_landed(self, after):
        bufs = _split_wait(self._flight["rest"], _gather_chip_plan(self._shapes), after,
                           name="gather_rest_chips_wait")
        self._flight["rest"] = _split_start(bufs, _gather_pair_plan(self._shapes), 4 * len(self._rest),
                                            name="gather_rest_pair_start")
        return self._flight["rest"][3]

    def rest_ready(self, after):
        bufs = _split_wait(self._flight.pop("rest"), _gather_pair_plan(self._shapes), after,
                           name="gather_rest_pair_wait")
        return _as_operands(dict(zip(self._rest, bufs[len(self._rest):])))

    def reduce_start(self, tag, grads):
        names = list(grads)
        local = [_by_owner(n, grads[n]) for n in names]
        from_sib = _pair_exchange(local, name=f"pair_exchange_{tag}")
        parts = [_pair_sum(g, s, name=f"pair_sum_{n}") for n, g, s in zip(names, local, from_sib)]
        lands = [lax.empty((N_CHIPS - 1,) + p.shape[1:], p.dtype) for p in parts]
        self._flight[tag] = (names, _split_start(parts + lands, _reduce_chip_plan(len(names)), 3 * len(names),
                                                 name=f"chip_exchange_{tag}_start"))
        return self._flight[tag][1][3]

    def reduce_finish(self, tag, after):
        names, started = self._flight.pop(tag)
        n = len(names)
        bufs = _split_wait(started, _reduce_chip_plan(n), after, name=f"chip_exchange_{tag}_wait")
        totals = [_chip_sum(p, s, name=f"chip_sum_{nm}") for nm, p, s in zip(names, bufs[:n], bufs[n:])]
        self.total.update(zip(names, _pair_gather(totals, name=f"pair_gather_{tag}")))


def _adamw(w, g, m, v, *, name):
    R, C = w.shape
    tr = math.gcd(R, 256)

    def body(w_ref, g_ref, m_ref, v_ref, d_ref, nm_ref, nv_ref):
        gg = g_ref[...]
        m_new = ADAM_B1 * m_ref[...] + (1.0 - ADAM_B1) * gg
        v_new = ADAM_B2 * v_ref[...] + (1.0 - ADAM_B2) * (gg * gg)
        m_hat = m_new / (1.0 - ADAM_B1 ** ADAM_STEP)
        v_hat = v_new / (1.0 - ADAM_B2 ** ADAM_STEP)
        d_ref[...] = -ADAM_LR * (m_hat / (jnp.sqrt(v_hat) + ADAM_EPS) + ADAM_WD * w_ref[...])
        nm_ref[...] = m_new
        nv_ref[...] = v_new

    blk = pl.BlockSpec((tr, C), lambda i: (i, 0))
    shp = jax.ShapeDtypeStruct((R, C), F32)
    return pl.pallas_call(
        body, name=name, grid=(R // tr,), in_specs=[blk] * 4, out_specs=[blk] * 3, out_shape=[shp] * 3,
        compiler_params=_params(("parallel",)),
    )(w, g, m, v)


def _to2d(a):
    flat = a.reshape(-1)
    pad = (-flat.shape[0]) % (SUBLANES * LANES)
    if pad:
        flat = jnp.pad(flat, (0, pad))
    return flat.reshape(-1, LANES)


def _small_rows(shape):
    return -(-math.prod(shape) // (SUBLANES * LANES)) * SUBLANES


def _pack_small(parts):
    rows = jnp.concatenate([_to2d(p) for p in parts], axis=0)
    pad = (-rows.shape[0]) % 256
    if pad:
        rows = jnp.concatenate([rows, jnp.zeros((pad, LANES), rows.dtype)], axis=0)
    return rows


def _unpack_small(rows, shapes):
    out, r = [], 0
    for shp in shapes:
        size = math.prod(shp)
        nrow = _small_rows(shp)
        out.append(rows[r:r + nrow].reshape(-1)[:size].reshape(shp))
        r += nrow
    return out


def kernel(x, mem, g_ffn1, w1_gate, w1_up, w1_down, g_mix, w_in, g_v, w_s, b_s, sinks, g_a_out, g_b_out, w_out, g_x, g_mem, w_xq, w_xkv, w_xo, g_ffn2, w2_gate, w2_up, w2_down, g_final, loss_target, m_g_ffn1, m_w1_gate, m_w1_up, m_w1_down, m_g_mix, m_w_in, m_g_v, m_w_s, m_b_s, m_sinks, m_g_a_out, m_g_b_out, m_w_out, m_g_x, m_g_mem, m_w_xq, m_w_xkv, m_w_xo, m_g_ffn2, m_w2_gate, m_w2_up, m_w2_down, m_g_final, v_g_ffn1, v_w1_gate, v_w1_up, v_w1_down, v_g_mix, v_w_in, v_g_v, v_w_s, v_b_s, v_sinks, v_g_a_out, v_g_b_out, v_w_out, v_g_x, v_g_mem, v_w_xq, v_w_xkv, v_w_xo, v_g_ffn2, v_w2_gate, v_w2_up, v_w2_down, v_g_final):
    args = dict(locals())
    Wp = {n: args[n] for n in ALL_W}
    Mp = {n: args["m_" + n] for n in ALL_W}
    Vp = {n: args["v_" + n] for n in ALL_W}

    comm = _Comm({n: Wp[n][0].astype(BF16) for n in BIG})
    W = {n: Wp[n] for n in SMALL}
    W["g_final"] = Wp["g_final"].reshape(1, D_MODEL)
    for n in ("w_s", "b_s"):
        W[n] = Wp[n][0]
    loss, dx, grads = _local_step(x[0], mem[0], loss_target[0], W, comm)
    big_grad = comm.total

    small_shapes = [Wp[n].shape for n in SMALL]
    packed = _pack_small([grads[n].reshape(Wp[n].shape) for n in SMALL] + [loss])
    summed = _allreduce_small(packed, name="allreduce_small")
    small_grad = dict(zip(SMALL, _unpack_small(summed, small_shapes)))
    nrows = sum(_small_rows(s) for s in small_shapes)
    loss_total = summed[nrows, 0]

    grad_out, delta, new_m, new_v = {}, {}, {}, {}
    for n in BIG:
        shp = Wp[n].shape
        grad_out[n] = big_grad[n].reshape(shp)
        d, nm, nv = _adamw(Wp[n][0], big_grad[n], Mp[n][0], Vp[n][0], name=f"adamw_{n}")
        delta[n], new_m[n], new_v[n] = d.reshape(shp), nm.reshape(shp), nv.reshape(shp)
    sw = _pack_small([Wp[n] for n in SMALL])
    sg = _pack_small([small_grad[n] for n in SMALL])
    sm = _pack_small([Mp[n] for n in SMALL])
    sv = _pack_small([Vp[n] for n in SMALL])
    d, nm, nv = _adamw(sw, sg, sm, sv, name="adamw_small")
    for n, dd, mm_, vv_ in zip(SMALL, _unpack_small(d, small_shapes), _unpack_small(nm, small_shapes),
                               _unpack_small(nv, small_shapes)):
        grad_out[n], delta[n], new_m[n], new_v[n] = small_grad[n], dd, mm_, vv_

    return (loss_total, dx[None], *[grad_out[n] for n in ALL_W], *[delta[n] for n in ALL_W],
            *[new_m[n] for n in ALL_W], *[new_v[n] for n in ALL_W])
```

```python
import functools
import math

import jax
import jax.numpy as jnp
from jax import lax
from jax.experimental import pallas as pl
from jax.experimental.pallas import tpu as pltpu

F32 = jnp.float32
BF16 = jnp.bfloat16
MESH = pl.DeviceIdType.MESH

D_MODEL = 2048
D_FF = 5632
D_A = 1024
D_B = 1024
CHUNK = 128
A_GROUPS = 8
HEAD_DIM = 64
B_Q_HEADS = 16
B_KV_HEADS = 2
X_HEADS = 4
X_HEAD_DIM = 512
IN_COLS = 3328
O_Q = 2 * D_A
O_K = O_Q + D_B
O_V = O_K + B_KV_HEADS * HEAD_DIM
N_CHIPS = 4
EPS = 1e-5
NEG = -1e30
ADAM_LR = 0.001
ADAM_B1 = 0.9
ADAM_B2 = 0.999
ADAM_EPS = 1e-08
ADAM_WD = 0.01
ADAM_STEP = 10

V7X_VMEM_BYTES = 64 * 1024 * 1024
VMEM_LIMIT = 56 * 1024 * 1024
LANES = 128
SUBLANES = 8


ANY = pl.BlockSpec(memory_space=pl.ANY)


def _params(sem, vmem=VMEM_LIMIT):
    return pltpu.CompilerParams(dimension_semantics=sem, vmem_limit_bytes=vmem)


def _matmul(pairs, *, M, N, K, tm, tn, tk, a_t=False, b_kind="n", out_kind="n", out_dtype=F32,
            scale=1.0, res=None, norm_g=None, order="ij", dep=None, name):
    tm, tn, tk = min(tm, M), min(tn, N), min(tk, K)
    assert M % tm == 0 and N % tn == 0 and K % tk == 0, (name, M, N, K, tm, tn, tk)
    nk = K // tk
    npairs = len(pairs)
    b_t = b_kind in ("t", "st")

    def ij(g0, g1):
        return (g0, g1) if order == "ij" else (g1, g0)

    def a_map(g0, g1, k):
        i, _ = ij(g0, g1)
        return (k, i) if a_t else (i, k)

    a_spec = pl.BlockSpec((tk, tm) if a_t else (tm, tk), a_map)

    b0 = pairs[0][1]
    if b_kind == "n":
        b_spec = pl.BlockSpec((tk, tn), lambda g0, g1, k: (k, ij(g0, g1)[1]))
    elif b_kind == "t":
        b_spec = pl.BlockSpec((tn, tk), lambda g0, g1, k: (ij(g0, g1)[1], k))
    elif b_kind == "sn":
        ns = b0.shape[2]
        assert ns % tn == 0
        nps = ns // tn
        b_spec = pl.BlockSpec((None, tk, tn), lambda g0, g1, k: (ij(g0, g1)[1] // nps, k, ij(g0, g1)[1] % nps))
    else:
        ks = b0.shape[2]
        assert ks % tk == 0
        kps = ks // tk
        b_spec = pl.BlockSpec((None, tn, tk), lambda g0, g1, k: (k // kps, ij(g0, g1)[1], k % kps))

    if out_kind == "n":
        o_spec = pl.BlockSpec((tm, tn), lambda g0, g1, k: ij(g0, g1))
        o_shape = jax.ShapeDtypeStruct((M, N), out_dtype)
    else:
        ns = N // N_CHIPS
        assert ns % tn == 0
        nps_o = ns // tn
        o_spec = pl.BlockSpec((None, tm, tn), lambda g0, g1, k: (ij(g0, g1)[1] // nps_o, ij(g0, g1)[0], ij(g0, g1)[1] % nps_o))
        o_shape = jax.ShapeDtypeStruct((N_CHIPS, M, ns), out_dtype)

    in_specs, args = [], []
    for a, b in pairs:
        in_specs += [a_spec, b_spec]
        args += [a, b]
    if res is not None:
        in_specs.append(pl.BlockSpec((tm, tn), lambda g0, g1, k: ij(g0, g1)))
        args.append(res)
    if norm_g is not None:
        assert tn == N and out_kind == "n"
        in_specs.append(pl.BlockSpec((1, N), lambda g0, g1, k: (0, 0)))
        args.append(norm_g)
    if dep is not None:
        in_specs.append(ANY)
        args.append(dep)

    dn = (((0,) if a_t else (1,), (1,) if b_t else (0,)), ((), ()))

    def body(*refs):
        pos = 2 * npairs
        res_ref = refs[pos] if res is not None else None
        pos += res is not None
        g_ref = refs[pos] if norm_g is not None else None
        pos += (norm_g is not None) + (dep is not None)
        o_ref = refs[pos]
        n_ref = refs[pos + 1] if norm_g is not None else None
        acc_ref = refs[-1] if nk > 1 else None
        part = None
        for p in range(npairs):
            d = lax.dot_general(refs[2 * p][...], refs[2 * p + 1][...], dn, preferred_element_type=F32)
            part = d if part is None else part + d

        def finish(acc):
            r = acc * scale if scale != 1.0 else acc
            if res_ref is not None:
                r = res_ref[...] + r
            o_ref[...] = r.astype(out_dtype)
            if n_ref is not None:
                n_ref[...] = (r * _rstd(r) * g_ref[...]).astype(BF16)

        if nk == 1:
            finish(part)
        else:
            k = pl.program_id(2)

            @pl.when(k == 0)
            def _():
                acc_ref[...] = part

            @pl.when((k > 0) & (k < nk - 1))
            def _():
                acc_ref[...] += part

            @pl.when(k == nk - 1)
            def _():
                finish(acc_ref[...] + part)

    grid = (M // tm, N // tn, nk) if order == "ij" else (N // tn, M // tm, nk)
    out_specs, out_shape = o_spec, o_shape
    if norm_g is not None:
        out_specs = [o_spec, pl.BlockSpec((tm, tn), lambda g0, g1, k: ij(g0, g1))]
        out_shape = [o_shape, jax.ShapeDtypeStruct((M, N), BF16)]
    return pl.pallas_call(
        body, name=name, grid=grid, in_specs=in_specs, out_specs=out_specs, out_shape=out_shape,
        scratch_shapes=[pltpu.VMEM((tm, tn), F32)] if nk > 1 else [],
        compiler_params=_params(("parallel", "parallel", "arbitrary")),
    )(*args)


def _rstd(x):
    return lax.rsqrt(jnp.mean(x * x, axis=-1, keepdims=True) + EPS)


def _rms_bwd_math(x, g, dy):
    r = _rstd(x)
    gy = dy * g
    xr = x * r
    dx = r * (gy - xr * jnp.mean(gy * xr, axis=-1, keepdims=True))
    return dx, dy * xr


def _rms_fwd(h, g, *, name, tm=512, dep=None):
    T, Dm = h.shape
    tm = min(tm, T)

    def body(h_ref, g_ref, *rest):
        x = h_ref[...]
        rest[-1][...] = (x * _rstd(x) * g_ref[...]).astype(BF16)

    return pl.pallas_call(
        body, name=name, grid=(T // tm,),
        in_specs=[pl.BlockSpec((tm, Dm), lambda i: (i, 0)), pl.BlockSpec((1, Dm), lambda i: (0, 0))]
        + ([ANY] if dep is not None else []),
        out_specs=pl.BlockSpec((tm, Dm), lambda i: (i, 0)),
        out_shape=jax.ShapeDtypeStruct((T, Dm), BF16),
        compiler_params=_params(("parallel",)),
    )(h, g, *([dep] if dep is not None else []))


def _rms_bwd(h, g, dn, dres, *, name, tm=256):
    T, Dm = h.shape
    tm = min(tm, T)
    has_res = dres is not None

    def body(*refs):
        h_ref, g_ref, dn_ref = refs[:3]
        pos = 3
        dres_ref = refs[pos] if has_res else None
        pos += has_res
        dh_ref, dhb_ref, dg_ref = refs[pos:pos + 3]
        dx, dgr = _rms_bwd_math(h_ref[...], g_ref[...], dn_ref[...].astype(F32))
        if has_res:
            dx = dres_ref[...] + dx
        dh_ref[...] = dx
        dhb_ref[...] = dx.astype(BF16)
        part = jnp.sum(dgr, axis=0, keepdims=True)

        @pl.when(pl.program_id(0) == 0)
        def _():
            dg_ref[...] = part

        @pl.when(pl.program_id(0) > 0)
        def _():
            dg_ref[...] += part

    row = pl.BlockSpec((tm, Dm), lambda i: (i, 0))
    vec = pl.BlockSpec((1, Dm), lambda i: (0, 0))
    args = [h, g, dn] + ([dres] if has_res else [])
    return pl.pallas_call(
        body, name=name, grid=(T // tm,),
        in_specs=[row, vec, row] + ([row] if has_res else []),
        out_specs=[row, row, vec],
        out_shape=[jax.ShapeDtypeStruct((T, Dm), F32), jax.ShapeDtypeStruct((T, Dm), BF16),
                   jax.ShapeDtypeStruct((1, Dm), F32)],
        compiler_params=_params(("arbitrary",)),
    )(*args)


def _loss_head(h, g, tgt, *, name, tm=256):
    T, Dm = h.shape
    tm = min(tm, T)

    def body(h_ref, g_ref, t_ref, dh_ref, dhb_ref, dg_ref, loss_ref):
        x = h_ref[...]
        gv = g_ref[...]
        r = _rstd(x)
        diff = x * r * gv - t_ref[...]
        lpart = 0.5 * jnp.sum(jnp.mean(diff * diff, axis=-1, keepdims=True), axis=0, keepdims=True)
        dx, dgr = _rms_bwd_math(x, gv, diff * (1.0 / Dm))
        dh_ref[...] = dx
        dhb_ref[...] = dx.astype(BF16)
        part = jnp.sum(dgr, axis=0, keepdims=True)
        lrow = jnp.broadcast_to(lpart, (1, LANES))

        @pl.when(pl.program_id(0) == 0)
        def _():
            dg_ref[...] = part
            loss_ref[...] = lrow

        @pl.when(pl.program_id(0) > 0)
        def _():
            dg_ref[...] += part
            loss_ref[...] += lrow

    row = pl.BlockSpec((tm, Dm), lambda i: (i, 0))
    vec = pl.BlockSpec((1, Dm), lambda i: (0, 0))
    return pl.pallas_call(
        body, name=name, grid=(T // tm,),
        in_specs=[row, vec, row],
        out_specs=[row, row, vec, pl.BlockSpec((1, LANES), lambda i: (0, 0))],
        out_shape=[jax.ShapeDtypeStruct((T, Dm), F32), jax.ShapeDtypeStruct((T, Dm), BF16),
                   jax.ShapeDtypeStruct((1, Dm), F32), jax.ShapeDtypeStruct((1, LANES), F32)],
        compiler_params=_params(("arbitrary",)),
    )(h, g, tgt)


def _swiglu_up(n, wg, wu, *, name, tm=512):
    T, Dm = n.shape
    S, _, fs = wg.shape
    tm = min(tm, T)

    def body(n_ref, wg_ref, wu_ref, g_ref, u_ref, a_ref):
        x = n_ref[...]
        g = jnp.dot(x, wg_ref[...], preferred_element_type=F32)
        u = jnp.dot(x, wu_ref[...], preferred_element_type=F32)
        g_ref[...] = g.astype(BF16)
        u_ref[...] = u.astype(BF16)
        a_ref[...] = (g * jax.nn.sigmoid(g) * u).astype(BF16)

    wspec = pl.BlockSpec((None, Dm, fs), lambda j, i: (j, 0, 0))
    ospec = pl.BlockSpec((tm, fs), lambda j, i: (i, j))
    oshape = jax.ShapeDtypeStruct((T, S * fs), BF16)
    return pl.pallas_call(
        body, name=name, grid=(S, T // tm),
        in_specs=[pl.BlockSpec((tm, Dm), lambda j, i: (i, 0)), wspec, wspec],
        out_specs=[ospec, ospec, ospec], out_shape=[oshape, oshape, oshape],
        compiler_params=_params(("parallel", "parallel")),
    )(n, wg, wu)


def _swiglu_bwd_act(dhb, wd, G, U, *, name, tm=512, tn=1408):
    T, Dm = dhb.shape
    Fd = wd.shape[0]
    tm, tn = min(tm, T), min(tn, Fd)

    def body(dh_ref, wd_ref, g_ref, u_ref, dg_ref, du_ref):
        da = 0.5 * lax.dot_general(dh_ref[...], wd_ref[...], (((1,), (1,)), ((), ())), preferred_element_type=F32)
        g = g_ref[...].astype(F32)
        u = u_ref[...].astype(F32)
        sg = jax.nn.sigmoid(g)
        dg_ref[...] = (da * u * (sg * (1.0 + g * (1.0 - sg)))).astype(BF16)
        du_ref[...] = (da * (g * sg)).astype(BF16)

    blk = pl.BlockSpec((tm, tn), lambda j, i: (i, j))
    oshape = jax.ShapeDtypeStruct((T, Fd), BF16)
    return pl.pallas_call(
        body, name=name, grid=(Fd // tn, T // tm),
        in_specs=[pl.BlockSpec((tm, Dm), lambda j, i: (i, 0)), pl.BlockSpec((tn, Dm), lambda j, i: (j, 0)), blk, blk],
        out_specs=[blk, blk], out_shape=[oshape, oshape],
        compiler_params=_params(("parallel", "parallel")),
    )(dhb, wd, G, U)


_INV_SQRT2 = 0.7071067811865476
_INV_SQRT2PI = 0.3989422804014327


def _erf(x):
    ax = jnp.abs(x)
    t = 1.0 / (1.0 + 0.3275911 * ax)
    poly = t * (0.254829592 + t * (-0.284496736 + t * (1.421413741 + t * (-1.453152027 + t * 1.061405429))))
    y = 1.0 - poly * jnp.exp(-ax * ax)
    return jnp.where(x < 0, -y, y)


def _gelu_cdf(x):
    return 0.5 * (1.0 + _erf(x * _INV_SQRT2))


def _lane_lt64(shape):
    return lax.broadcasted_iota(jnp.int32, shape, len(shape) - 1) < HEAD_DIM


def _dup_half(x, kv):
    rolled = pltpu.roll(x, HEAD_DIM, 1)
    lo = _lane_lt64(x.shape)
    return jnp.where(lo, x, rolled) if kv == 0 else jnp.where(lo, rolled, x)


HEADS_PER_KV = B_Q_HEADS // B_KV_HEADS
PAIRS = HEADS_PER_KV // 2


def _attn_mask(block):
    shape = (HEADS_PER_KV * CHUNK, 2 * CHUNK)
    qpos = (lax.broadcasted_iota(jnp.int32, shape, 0) & (CHUNK - 1)) + CHUNK
    kpos = lax.broadcasted_iota(jnp.int32, shape, 1)
    diff = qpos - kpos
    first_key = jnp.where(block == 0, CHUNK, 0)
    return (diff >= 0) & (diff < CHUNK) & (kpos >= first_key)


def _stack_heads(tiles, lo):
    parts = []
    for t in tiles:
        parts += [jnp.where(lo, t, 0.0), jnp.where(lo, 0.0, t)]
    return jnp.concatenate(parts, axis=0)


def _unstack_heads(s, lo):
    return [jnp.where(lo, s[2 * p * CHUNK:(2 * p + 1) * CHUNK], s[(2 * p + 1) * CHUNK:(2 * p + 2) * CHUNK])
            for p in range(PAIRS)]


def _stack_sinks(sk_ref, kv):
    return jnp.concatenate([jnp.broadcast_to(sk_ref[:, h:h + 1], (CHUNK, 1))
                            for h in range(kv * HEADS_PER_KV, (kv + 1) * HEADS_PER_KV)], axis=0)


def _sgu_forward(z_ref, gv, wsm, bst):
    zu = z_ref[:, 0:D_A]
    zv = z_ref[:, D_A:2 * D_A]
    u = zu * _gelu_cdf(zu)
    v = zv * _gelu_cdf(zv)
    rv = _rstd(v)
    vn = (v * rv * gv).astype(BF16)
    svs = []
    for g in range(A_GROUPS):
        sl = slice(g * CHUNK, (g + 1) * CHUNK)
        svs.append(jnp.dot(wsm[g], vn[:, sl], preferred_element_type=F32) + bst[:, g:g + 1])
    sv = jnp.concatenate(svs, axis=1)
    return zu, zv, u, v, rv, vn, sv


def _masked_ws(ws_ref):
    tril = lax.broadcasted_iota(jnp.int32, (CHUNK, CHUNK), 0) >= lax.broadcasted_iota(jnp.int32, (CHUNK, CHUNK), 1)
    return [jnp.where(tril, ws_ref[g], 0.0).astype(BF16) for g in range(A_GROUPS)], tril


def _attn_probs(qm, kkd, sink, mask):
    s = lax.dot_general(qm, kkd, (((1,), (1,)), ((), ())), preferred_element_type=F32) * (HEAD_DIM ** -0.5)
    s = jnp.where(mask, s, NEG)
    m = jnp.maximum(jnp.max(s, axis=-1, keepdims=True), sink)
    e = jnp.exp(s - m)
    es = jnp.exp(sink - m)
    inv = 1.0 / (jnp.sum(e, axis=-1, keepdims=True) + es)
    return e * inv, es * inv


def _mixer_fwd(z, gv, ws, bst, sinks, ga, gb, *, name):
    T = z.shape[0]
    nb = T // CHUNK
    kvb = O_K // (2 * CHUNK)

    def body(z_ref, zp_ref, gv_ref, ws_ref, bst_ref, sk_ref, ga_ref, gb_ref, o_ref):
        i = pl.program_id(0)
        wsm, _ = _masked_ws(ws_ref)
        _, _, u, _, _, _, sv = _sgu_forward(z_ref, gv_ref[...], wsm, bst_ref[...])
        ya = u * sv
        o_ref[:, 0:D_A] = (ya * _rstd(ya) * ga_ref[...]).astype(BF16)

        mask = _attn_mask(i)
        kk = jnp.concatenate([zp_ref[:, 0:CHUNK], z_ref[:, O_K:O_V]], axis=0)
        vv = jnp.concatenate([zp_ref[:, CHUNK:2 * CHUNK], z_ref[:, O_V:IN_COLS]], axis=0)
        lo = _lane_lt64((CHUNK, LANES))
        outs = []
        for kv in range(B_KV_HEADS):
            kkd = _dup_half(kk, kv).astype(BF16)
            vvd = _dup_half(vv, kv).astype(BF16)
            q = _stack_heads([z_ref[:, O_Q + (kv * PAIRS + pr) * LANES:O_Q + (kv * PAIRS + pr + 1) * LANES]
                              for pr in range(PAIRS)], lo).astype(BF16)
            p, _ = _attn_probs(q, kkd, _stack_sinks(sk_ref, kv), mask)
            outs += _unstack_heads(jnp.dot(p.astype(BF16), vvd, preferred_element_type=F32), lo)
        yb = jnp.concatenate(outs, axis=1)
        o_ref[:, D_A:D_A + D_B] = (yb * _rstd(yb) * gb_ref[...]).astype(BF16)

    full = lambda shape: pl.BlockSpec(shape, lambda i: (0,) * len(shape))
    return pl.pallas_call(
        body, name=name, grid=(nb,),
        in_specs=[pl.BlockSpec((CHUNK, IN_COLS), lambda i: (i, 0)),
                  pl.BlockSpec((CHUNK, 2 * CHUNK), lambda i: (jnp.maximum(i - 1, 0), kvb)),
                  full((1, D_A)), full((A_GROUPS, CHUNK, CHUNK)), full((CHUNK, A_GROUPS)), full((1, B_Q_HEADS)),
                  full((1, D_A)), full((1, D_B))],
        out_specs=pl.BlockSpec((CHUNK, D_A + D_B), lambda i: (i, 0)),
        out_shape=jax.ShapeDtypeStruct((T, D_A + D_B), BF16),
        compiler_params=_params(("parallel",)),
    )(z, z, gv, ws, bst, sinks, ga, gb)


def _mixer_bwd(z, dyn, gv, ws, bst, sinks, ga, gb, *, name):
    T = z.shape[0]
    nb = T // CHUNK
    kvb = O_K // (2 * CHUNK)
    NT = (((0,), (0,)), ((), ()))

    def body(z_ref, zp_ref, dy_ref, gv_ref, ws_ref, bst_ref, sk_ref, ga_ref, gb_ref,
             dz_ref, dgv_ref, dws_ref, dbst_ref, dsk_ref, dga_ref, dgb_ref, carry_ref, p_ref):
        step = pl.program_id(0)
        i = nb - 1 - step

        @pl.when(step == 0)
        def _():
            carry_ref[...] = jnp.zeros_like(carry_ref)
            dgv_ref[...] = jnp.zeros_like(dgv_ref)
            dws_ref[...] = jnp.zeros_like(dws_ref)
            dbst_ref[...] = jnp.zeros_like(dbst_ref)
            dsk_ref[...] = jnp.zeros_like(dsk_ref)
            dga_ref[...] = jnp.zeros_like(dga_ref)
            dgb_ref[...] = jnp.zeros_like(dgb_ref)

        wsm, tril = _masked_ws(ws_ref)
        gvv = gv_ref[...]
        zu, zv, u, v, rv, vn, sv = _sgu_forward(z_ref, gvv, wsm, bst_ref[...])
        ya = u * sv
        dya, dga_rows = _rms_bwd_math(ya, ga_ref[...], dy_ref[:, 0:D_A].astype(F32))
        dga_ref[...] += jnp.sum(dga_rows, axis=0, keepdims=True)
        du = dya * sv
        dsv = dya * u
        dvn_parts = []
        for g in range(A_GROUPS):
            sl = slice(g * CHUNK, (g + 1) * CHUNK)
            dsv_g = dsv[:, sl]
            dsv_gb = dsv_g.astype(BF16)
            dw = lax.dot_general(dsv_gb, vn[:, sl], (((1,), (1,)), ((), ())), preferred_element_type=F32)
            dws_ref[g] += jnp.where(tril, dw, 0.0)
            dbst_ref[:, g:g + 1] += jnp.sum(dsv_g, axis=1, keepdims=True)
            dvn_parts.append(lax.dot_general(wsm[g], dsv_gb, NT, preferred_element_type=F32))
        dvn = jnp.concatenate(dvn_parts, axis=1)
        dv, dgv_rows = _rms_bwd_math(v, gvv, dvn)
        dgv_ref[...] += jnp.sum(dgv_rows, axis=0, keepdims=True)
        dz_ref[:, 0:D_A] = (du * (_gelu_cdf(zu) + zu * jnp.exp(-0.5 * zu * zu) * _INV_SQRT2PI)).astype(BF16)
        dz_ref[:, D_A:2 * D_A] = (dv * (_gelu_cdf(zv) + zv * jnp.exp(-0.5 * zv * zv) * _INV_SQRT2PI)).astype(BF16)

        mask = _attn_mask(i)
        kk = jnp.concatenate([zp_ref[:, 0:CHUNK], z_ref[:, O_K:O_V]], axis=0)
        vv = jnp.concatenate([zp_ref[:, CHUNK:2 * CHUNK], z_ref[:, O_V:IN_COLS]], axis=0)
        lo = _lane_lt64((CHUNK, LANES))
        kkd = [_dup_half(kk, kv).astype(BF16) for kv in range(B_KV_HEADS)]
        vvd = [_dup_half(vv, kv).astype(BF16) for kv in range(B_KV_HEADS)]
        outs, qs, psinks = [], [], []
        for kv in range(B_KV_HEADS):
            qs.append(_stack_heads([z_ref[:, O_Q + (kv * PAIRS + pr) * LANES:O_Q + (kv * PAIRS + pr + 1) * LANES]
                                    for pr in range(PAIRS)], lo).astype(BF16))
            p, ps = _attn_probs(qs[kv], kkd[kv], _stack_sinks(sk_ref, kv), mask)
            p_ref[kv] = p
            psinks.append(ps)
            outs += _unstack_heads(jnp.dot(p.astype(BF16), vvd[kv], preferred_element_type=F32), lo)
        yb = jnp.concatenate(outs, axis=1)
        dyb, dgb_rows = _rms_bwd_math(yb, gb_ref[...], dy_ref[:, D_A:D_A + D_B].astype(F32))
        dgb_ref[...] += jnp.sum(dgb_rows, axis=0, keepdims=True)

        dkk, dvv = [], []
        for kv in range(B_KV_HEADS):
            do = _stack_heads([dyb[:, (kv * PAIRS + pr) * LANES:(kv * PAIRS + pr + 1) * LANES]
                               for pr in range(PAIRS)], lo).astype(BF16)
            p = p_ref[kv]
            dvv.append(lax.dot_general(p.astype(BF16), do, NT, preferred_element_type=F32))
            dp = lax.dot_general(do, vvd[kv], (((1,), (1,)), ((), ())), preferred_element_type=F32)
            delta = jnp.sum(p * dp, axis=-1, keepdims=True)
            dsink = -psinks[kv] * delta
            for g in range(HEADS_PER_KV):
                h = kv * HEADS_PER_KV + g
                dsk_ref[:, h:h + 1] += jnp.sum(dsink[g * CHUNK:(g + 1) * CHUNK], axis=0, keepdims=True)
            ds = (p * (dp - delta) * (HEAD_DIM ** -0.5)).astype(BF16)
            dq = _unstack_heads(jnp.dot(ds, kkd[kv], preferred_element_type=F32), lo)
            for pr in range(PAIRS):
                c0 = O_Q + (kv * PAIRS + pr) * LANES
                dz_ref[:, c0:c0 + LANES] = dq[pr].astype(BF16)
            dkk.append(lax.dot_general(ds, qs[kv], NT, preferred_element_type=F32))

        def fold(parts):
            tot = [t + pltpu.roll(t, HEAD_DIM, 1) for t in parts]
            return jnp.where(_lane_lt64(tot[0].shape), tot[0], tot[1])

        dk_all = fold(dkk)
        dv_all = fold(dvv)
        dz_ref[:, O_K:O_V] = (dk_all[CHUNK:] + carry_ref[:, 0:CHUNK]).astype(BF16)
        dz_ref[:, O_V:IN_COLS] = (dv_all[CHUNK:] + carry_ref[:, CHUNK:2 * CHUNK]).astype(BF16)
        carry_ref[:, 0:CHUNK] = dk_all[:CHUNK]
        carry_ref[:, CHUNK:2 * CHUNK] = dv_all[:CHUNK]

    full = lambda shape: pl.BlockSpec(shape, lambda s: (0,) * len(shape))
    rev = lambda s: nb - 1 - s
    return pl.pallas_call(
        body, name=name, grid=(nb,),
        in_specs=[pl.BlockSpec((CHUNK, IN_COLS), lambda s: (rev(s), 0)),
                  pl.BlockSpec((CHUNK, 2 * CHUNK), lambda s: (jnp.maximum(rev(s) - 1, 0), kvb)),
                  pl.BlockSpec((CHUNK, D_A + D_B), lambda s: (rev(s), 0)),
                  full((1, D_A)), full((A_GROUPS, CHUNK, CHUNK)), full((CHUNK, A_GROUPS)), full((1, B_Q_HEADS)),
                  full((1, D_A)), full((1, D_B))],
        out_specs=[pl.BlockSpec((CHUNK, IN_COLS), lambda s: (rev(s), 0)),
                   full((1, D_A)), full((A_GROUPS, CHUNK, CHUNK)), full((CHUNK, A_GROUPS)), full((1, B_Q_HEADS)),
                   full((1, D_A)), full((1, D_B))],
        out_shape=[jax.ShapeDtypeStruct((T, IN_COLS), BF16), jax.ShapeDtypeStruct((1, D_A), F32),
                   jax.ShapeDtypeStruct((A_GROUPS, CHUNK, CHUNK), F32), jax.ShapeDtypeStruct((CHUNK, A_GROUPS), F32),
                   jax.ShapeDtypeStruct((1, B_Q_HEADS), F32), jax.ShapeDtypeStruct((1, D_A), F32),
                   jax.ShapeDtypeStruct((1, D_B), F32)],
        scratch_shapes=[pltpu.VMEM((CHUNK, 2 * CHUNK), F32), pltpu.VMEM((B_KV_HEADS, HEADS_PER_KV * CHUNK, 2 * CHUNK), F32)],
        compiler_params=_params(("arbitrary",)),
    )(z, z, dyn, gv, ws, bst, sinks, ga, gb)


def _xattn_probs(qh, kh):
    s = lax.dot_general(qh, kh, (((1,), (1,)), ((), ())), preferred_element_type=F32) * (X_HEAD_DIM ** -0.5)
    e = jnp.exp(s - jnp.max(s, axis=-1, keepdims=True))
    return e / jnp.sum(e, axis=-1, keepdims=True)


def _xattn_fwd(q, kvm, *, name, tm=512):
    T = q.shape[0]
    Mm = kvm.shape[0]
    tm = min(tm, T)

    def body(q_ref, kv_ref, o_ref):
        for h in range(X_HEADS):
            sl = slice(h * X_HEAD_DIM, (h + 1) * X_HEAD_DIM)
            kh = kv_ref[:, sl].astype(BF16)
            vh = kv_ref[:, D_MODEL + h * X_HEAD_DIM:D_MODEL + (h + 1) * X_HEAD_DIM].astype(BF16)
            p = _xattn_probs(q_ref[:, sl], kh)
            o_ref[:, sl] = jnp.dot(p.astype(BF16), vh, preferred_element_type=F32).astype(BF16)

    return pl.pallas_call(
        body, name=name, grid=(T // tm,),
        in_specs=[pl.BlockSpec((tm, D_MODEL), lambda i: (i, 0)), pl.BlockSpec((Mm, 2 * D_MODEL), lambda i: (0, 0))],
        out_specs=pl.BlockSpec((tm, D_MODEL), lambda i: (i, 0)),
        out_shape=jax.ShapeDtypeStruct((T, D_MODEL), BF16),
        compiler_params=_params(("parallel",)),
    )(q, kvm)


def _xattn_bwd(q, kvm, do, *, name, tm=512):
    T = q.shape[0]
    Mm = kvm.shape[0]
    tm = min(tm, T)
    NT = (((0,), (0,)), ((), ()))

    def body(q_ref, kv_ref, do_ref, dq_ref, dkv_ref):
        @pl.when(pl.program_id(0) == 0)
        def _():
            dkv_ref[...] = jnp.zeros_like(dkv_ref)

        for h in range(X_HEADS):
            sl = slice(h * X_HEAD_DIM, (h + 1) * X_HEAD_DIM)
            slv = slice(D_MODEL + h * X_HEAD_DIM, D_MODEL + (h + 1) * X_HEAD_DIM)
            kh = kv_ref[:, sl].astype(BF16)
            vh = kv_ref[:, slv].astype(BF16)
            qh = q_ref[:, sl]
            doh = do_ref[:, sl]
            p = _xattn_probs(qh, kh)
            dkv_ref[:, slv] += lax.dot_general(p.astype(BF16), doh, NT, preferred_element_type=F32)
            dp = lax.dot_general(doh, vh, (((1,), (1,)), ((), ())), preferred_element_type=F32)
            ds = (p * (dp - jnp.sum(p * dp, axis=-1, keepdims=True)) * (X_HEAD_DIM ** -0.5)).astype(BF16)
            dq_ref[:, sl] = jnp.dot(ds, kh, preferred_element_type=F32).astype(BF16)
            dkv_ref[:, sl] += lax.dot_general(ds, qh, NT, preferred_element_type=F32)

    row = pl.BlockSpec((tm, D_MODEL), lambda i: (i, 0))
    kvs = pl.BlockSpec((Mm, 2 * D_MODEL), lambda i: (0, 0))
    return pl.pallas_call(
        body, name=name, grid=(T // tm,),
        in_specs=[row, kvs, row], out_specs=[row, kvs],
        out_shape=[jax.ShapeDtypeStruct((T, D_MODEL), BF16), jax.ShapeDtypeStruct((Mm, 2 * D_MODEL), F32)],
        compiler_params=_params(("arbitrary",)),
    )(q, kvm, do)


def _swiglu_bwd_weights(tag, n, G, U, A, wd, dhb):
    T = n.shape[0]
    dG, dU = _swiglu_bwd_act(dhb, wd, G, U, name=f"{tag}_bwd_act")
    dwd = _matmul([(A, dhb)], M=D_FF, N=D_MODEL, K=T, tm=1408, tn=1024, tk=2048, a_t=True, out_dtype=BF16,
                  scale=0.5, name=f"{tag}_dwd")
    dwg = _matmul([(n, dG)], M=D_MODEL, N=D_FF, K=T, tm=1024, tn=1408, tk=2048, a_t=True, out_kind="s",
                  out_dtype=BF16, order="ji", name=f"{tag}_dwg")
    dwu = _matmul([(n, dU)], M=D_MODEL, N=D_FF, K=T, tm=1024, tn=1408, tk=2048, a_t=True, out_kind="s",
                  out_dtype=BF16, order="ji", name=f"{tag}_dwu")
    return dG, dU, dwg, dwu, dwd


def _swiglu_bwd_input(tag, hin, g_norm, dG, dU, wg, wu, dh, dep):
    T = hin.shape[0]
    dn = _matmul([(dG, wg), (dU, wu)], M=T, N=D_MODEL, K=D_FF, tm=512, tn=D_MODEL, tk=1408, b_kind="st",
                 dep=dep, name=f"{tag}_dn")
    return _rms_bwd(hin, g_norm, dn, dh, name=f"{tag}_norm_bwd")


GROUP_FFN1 = ["w1_gate", "w1_up", "w1_down"]
GROUP_MID = ["w_in", "w_out", "w_xq", "w_xkv", "w_xo"]
GROUP_FFN2 = ["w2_gate", "w2_up", "w2_down"]


def _local_step(x, mem, tgt, W, comm):
    T = x.shape[0]
    Mm = mem.shape[0]
    mm = functools.partial(_matmul)

    W = {**W, **comm.gather_now("ffn1_up", ["w1_gate", "w1_up"])}
    tok = comm.gather_start("ffn1_down", ["w1_down"], after=W["w1_up"])
    tok = comm.gather_start("mid", GROUP_MID, after=tok)
    tok = comm.gather_start("ffn2", GROUP_FFN2, after=tok)
    n1 = _rms_fwd(x, W["g_ffn1"], dep=tok, name="f_norm1")
    G1, U1, A1 = _swiglu_up(n1, W["w1_gate"], W["w1_up"], name="f_ffn1_up")
    tok = comm.gather_landed("ffn1_down", after=A1)
    tok = comm.gather_landed("mid", after=tok)
    W = {**W, **comm.gather_ready("ffn1_down", after=tok)}
    h1, n2 = mm([(A1, W["w1_down"])], M=T, N=D_MODEL, K=D_FF, tm=512, tn=D_MODEL, tk=1408, scale=0.5, res=x,
                norm_g=W["g_mix"], name="f_ffn1_down")
    W = {**W, **comm.gather_ready("mid", after=n2)}
    z = mm([(n2, W["w_in"])], M=T, N=IN_COLS, K=D_MODEL, tm=512, tn=IN_COLS // 2, tk=D_MODEL, name="f_w_in")
    bst = jnp.transpose(W["b_s"])
    yn = _mixer_fwd(z, W["g_v"], W["w_s"], bst, W["sinks"], W["g_a_out"], W["g_b_out"], name="f_mixer")
    tok = comm.gather_landed("ffn2", after=yn)
    h2, n3 = mm([(yn, W["w_out"])], M=T, N=D_MODEL, K=D_MODEL, tm=512, tn=D_MODEL, tk=D_MODEL, res=h1,
                norm_g=W["g_x"], dep=tok, name="f_w_out")
    memn = _rms_fwd(mem, W["g_mem"], name="f_norm_mem")
    q3 = mm([(n3, W["w_xq"])], M=T, N=D_MODEL, K=D_MODEL, tm=512, tn=D_MODEL, tk=D_MODEL, out_dtype=BF16,
            name="f_w_xq")
    kvm = mm([(memn, W["w_xkv"])], M=Mm, N=2 * D_MODEL, K=D_MODEL, tm=Mm, tn=1024, tk=D_MODEL, b_kind="sn",
             name="f_w_xkv")
    o3 = _xattn_fwd(q3, kvm, name="f_xattn")
    h3, n4 = mm([(o3, W["w_xo"])], M=T, N=D_MODEL, K=D_MODEL, tm=512, tn=D_MODEL, tk=D_MODEL, res=h2,
                norm_g=W["g_ffn2"], name="f_w_xo")
    W = {**W, **comm.gather_ready("ffn2", after=n4)}
    G2, U2, A2 = _swiglu_up(n4, W["w2_gate"], W["w2_up"], name="f_ffn2_up")
    h4 = mm([(A2, W["w2_down"])], M=T, N=D_MODEL, K=D_FF, tm=512, tn=D_MODEL, tk=1408, scale=0.5, res=h3,
            name="f_ffn2_down")

    grads = {}
    dh4, dh4b, grads["g_final"], loss = _loss_head(h4, W["g_final"], tgt, name="loss_head")
    dG2, dU2, dwg, dwu, dwd = _swiglu_bwd_weights("b_ffn2", n4, G2, U2, A2, W["w2_down"], dh4b)
    tok = comm.reduce_start("ffn2", {"w2_gate": dwg, "w2_up": dwu, "w2_down": dwd})
    dh3, dh3b, grads["g_ffn2"] = _swiglu_bwd_input("b_ffn2", h3, W["g_ffn2"], dG2, dU2, W["w2_gate"], W["w2_up"],
                                                   dh4, tok)

    mid = {}
    do3 = mm([(dh3b, W["w_xo"])], M=T, N=D_MODEL, K=D_MODEL, tm=512, tn=D_MODEL, tk=D_MODEL, b_kind="t",
             out_dtype=BF16, name="b_do3")
    mid["w_xo"] = mm([(o3, dh3b)], M=D_MODEL, N=D_MODEL, K=T, tm=1024, tn=D_MODEL, tk=1024, a_t=True,
                       out_dtype=BF16, name="b_dw_xo")
    dq3, dkvm = _xattn_bwd(q3, kvm, do3, name="b_xattn")
    mid["w_xq"] = mm([(n3, dq3)], M=D_MODEL, N=D_MODEL, K=T, tm=1024, tn=D_MODEL, tk=1024, a_t=True,
                       out_dtype=BF16, name="b_dw_xq")
    dn3 = mm([(dq3, W["w_xq"])], M=T, N=D_MODEL, K=D_MODEL, tm=512, tn=D_MODEL, tk=D_MODEL, b_kind="t",
             name="b_dn3")
    dh2, dh2b, grads["g_x"] = _rms_bwd(h2, W["g_x"], dn3, dh3, name="b_norm3")
    dkvmb = dkvm.astype(BF16)
    mid["w_xkv"] = mm([(memn, dkvmb)], M=D_MODEL, N=2 * D_MODEL, K=Mm, tm=D_MODEL, tn=1024, tk=Mm, a_t=True,
                        out_kind="s", out_dtype=BF16, name="b_dw_xkv")
    dmemn = mm([(dkvmb, W["w_xkv"])], M=Mm, N=D_MODEL, K=2 * D_MODEL, tm=Mm, tn=D_MODEL, tk=1024, b_kind="st",
               name="b_dmemn")
    _, _, grads["g_mem"] = _rms_bwd(mem, W["g_mem"], dmemn, None, name="b_norm_mem")
    comm.reduce_finish("ffn2", after=dh2b)

    dyn = mm([(dh2b, W["w_out"])], M=T, N=D_MODEL, K=D_MODEL, tm=512, tn=D_MODEL, tk=D_MODEL, b_kind="t",
             out_dtype=BF16, name="b_dyn")
    mid["w_out"] = mm([(yn, dh2b)], M=D_MODEL, N=D_MODEL, K=T, tm=1024, tn=D_MODEL, tk=1024, a_t=True,
                        out_dtype=BF16, name="b_dw_out")
    dz, grads["g_v"], grads["w_s"], dbst, grads["sinks"], grads["g_a_out"], grads["g_b_out"] = _mixer_bwd(
        z, dyn, W["g_v"], W["w_s"], bst, W["sinks"], W["g_a_out"], W["g_b_out"], name="b_mixer")
    grads["b_s"] = jnp.transpose(dbst)
    mid["w_in"] = mm([(n2, dz)], M=D_MODEL, N=IN_COLS, K=T, tm=1024, tn=IN_COLS, tk=512, a_t=True,
                     out_dtype=BF16, name="b_dw_in")
    tok = comm.reduce_start("mid", mid)
    dn2 = mm([(dz, W["w_in"])], M=T, N=D_MODEL, K=IN_COLS, tm=512, tn=D_MODEL, tk=IN_COLS, b_kind="t",
             dep=tok, name="b_dn2")
    dh1, dh1b, grads["g_mix"] = _rms_bwd(h1, W["g_mix"], dn2, dh2, name="b_norm2")

    dG1, dU1, dwg, dwu, dwd = _swiglu_bwd_weights("b_ffn1", n1, G1, U1, A1, W["w1_down"], dh1b)
    comm.reduce_finish("mid", after=dwu)
    tok = comm.reduce_start("ffn1", {"w1_gate": dwg, "w1_up": dwu, "w1_down": dwd})
    dx, _, grads["g_ffn1"] = _swiglu_bwd_input("b_ffn1", x, W["g_ffn1"], dG1, dU1, W["w1_gate"], W["w1_up"], dh1, tok)
    comm.reduce_finish("ffn1", after=dx)
    return loss, dx, grads


BIG = ["w1_gate", "w1_up", "w1_down", "w_in", "w_out", "w_xq", "w_xkv", "w_xo", "w2_gate", "w2_up", "w2_down"]
SMALL = ["g_ffn1", "g_mix", "g_v", "w_s", "b_s", "sinks", "g_a_out", "g_b_out", "g_x", "g_mem", "g_ffn2", "g_final"]
ALL_W = ["g_ffn1", "w1_gate", "w1_up", "w1_down", "g_mix", "w_in", "g_v", "w_s", "b_s", "sinks", "g_a_out",
         "g_b_out", "w_out", "g_x", "g_mem", "w_xq", "w_xkv", "w_xo", "g_ffn2", "w2_gate", "w2_up", "w2_down",
         "g_final"]
ANY = pl.BlockSpec(memory_space=pl.ANY)


def _place():
    x, y, c = lax.axis_index("x"), lax.axis_index("y"), lax.axis_index("c")
    chips = [(1 - x, y), (x, 1 - y), (1 - x, 1 - y)]
    return x, y, c, chips


def _allgather_weights(shards, *, name):
    n = len(shards)

    def body(*refs):
        ins, outs = refs[:n], refs[n:2 * n]
        send, recv, loc = refs[2 * n:]
        x, y, c, chips = _place()
        me = 2 * x + y
        sib = (x, y, 1 - c)

        def half(w, slot, hc):
            h = shards[w].shape[0] // 2
            return outs[w].at[slot, pl.ds(hc * h, h), :]

        def copy(w, k, slot, hc, to, src=None):
            return pltpu.make_async_remote_copy(
                src_ref=half(w, slot, hc) if src is None else src, dst_ref=half(w, slot, hc),
                send_sem=send.at[6 * w + k], recv_sem=recv.at[6 * w + k], device_id=to, device_id_type=MESH)

        own = [pltpu.make_async_remote_copy(
            src_ref=ins[w], dst_ref=outs[w].at[me], send_sem=loc.at[w], recv_sem=loc.at[n + w],
            device_id=sib, device_id_type=MESH) for w in range(n)]
        for cp in own:
            cp.start()
        first = []
        for w in range(n):
            h = shards[w].shape[0] // 2
            for j, (tx, ty) in enumerate(chips):
                first.append(copy(w, j, me, c, (tx, ty, c), src=ins[w].at[pl.ds(c * h, h), :]))
                first[-1].start()
        passed = []
        for w in range(n):
            for j, (tx, ty) in enumerate(chips):
                slot = 2 * tx + ty
                copy(w, j, slot, c, (tx, ty, c)).wait_recv()
                passed.append(copy(w, 3 + j, slot, c, sib))
                passed[-1].start()
        for w in range(n):
            for j, (tx, ty) in enumerate(chips):
                copy(w, 3 + j, 2 * tx + ty, 1 - c, sib).wait_recv()
        for cp in first + passed:
            cp.wait_send()
        for cp in own:
            cp.wait()

    return pl.pallas_call(
        body, name=name, in_specs=[ANY] * n, out_specs=[ANY] * n,
        out_shape=[jax.ShapeDtypeStruct((N_CHIPS,) + s.shape, s.dtype) for s in shards],
        scratch_shapes=[pltpu.SemaphoreType.DMA((6 * n,)), pltpu.SemaphoreType.DMA((6 * n,)),
                        pltpu.SemaphoreType.DMA((2 * n,))],
    )(*shards)


def _pair_exchange(grads, *, name):
    n = len(grads)

    def body(*refs):
        ins, outs = refs[:n], refs[n:2 * n]
        send, recv = refs[2 * n:]
        x, y, c, _ = _place()
        cps = []
        for w in range(n):
            h = grads[w].shape[1] // 2
            cps.append(pltpu.make_async_remote_copy(
                src_ref=ins[w].at[:, pl.ds((1 - c) * h, h), :], dst_ref=outs[w],
                send_sem=send.at[w], recv_sem=recv.at[w], device_id=(x, y, 1 - c), device_id_type=MESH))
            cps[-1].start()
        for cp in cps:
            cp.wait()

    return pl.pallas_call(
        body, name=name, in_specs=[ANY] * n, out_specs=[ANY] * n,
        out_shape=[jax.ShapeDtypeStruct((N_CHIPS, g.shape[1] // 2, g.shape[2]), g.dtype) for g in grads],
        scratch_shapes=[pltpu.SemaphoreType.DMA((n,)), pltpu.SemaphoreType.DMA((n,))],
    )(*grads)


def _pair_sum(g, got, *, name):
    S, R, C = g.shape
    h = R // 2
    tr = math.gcd(h, 256)
    nr = h // tr

    def body(a_ref, b_ref, o_ref):
        o_ref[...] = (a_ref[...].astype(F32) + b_ref[...].astype(F32)).astype(BF16)

    return pl.pallas_call(
        body, name=name, grid=(S, nr),
        in_specs=[pl.BlockSpec((None, tr, C), lambda s, r: (s, lax.axis_index("c") * nr + r, 0)),
                  pl.BlockSpec((None, tr, C), lambda s, r: (s, r, 0))],
        out_specs=pl.BlockSpec((None, tr, C), lambda s, r: (s, r, 0)),
        out_shape=jax.ShapeDtypeStruct((S, h, C), BF16),
        compiler_params=_params(("parallel", "parallel")),
    )(g, got)


def _chip_sum(part, got, *, name):
    S, h, C = part.shape
    tr = math.gcd(h, 256)
    nr = h // tr

    def body(own_ref, g0_ref, g1_ref, g2_ref, o_ref):
        acc = own_ref[...].astype(F32) + g0_ref[...].astype(F32)
        o_ref[...] = (acc + g1_ref[...].astype(F32)) + g2_ref[...].astype(F32)

    def piece(j):
        return pl.BlockSpec((None, tr, C), lambda r: (j, r, 0))

    return pl.pallas_call(
        body, name=name, grid=(nr,),
        in_specs=[pl.BlockSpec((None, tr, C), lambda r: (2 * lax.axis_index("x") + lax.axis_index("y"), r, 0)),
                  piece(0), piece(1), piece(2)],
        out_specs=pl.BlockSpec((tr, C), lambda r: (lax.axis_index("c") * nr + r, 0)),
        out_shape=jax.ShapeDtypeStruct((2 * h, C), F32),
        compiler_params=_params(("parallel",)),
    )(part, got, got, got)


def _pair_gather(totals, *, name):
    n = len(totals)

    def body(*refs):
        ins, outs = refs[:n], refs[n:2 * n]
        send, recv = refs[2 * n:]
        x, y, c, _ = _place()
        cps = []
        for w in range(n):
            h = totals[w].shape[0] // 2
            cps.append(pltpu.make_async_remote_copy(
                src_ref=ins[w].at[pl.ds(c * h, h), :], dst_ref=outs[w].at[pl.ds(c * h, h), :],
                send_sem=send.at[w], recv_sem=recv.at[w], device_id=(x, y, 1 - c), device_id_type=MESH))
            cps[-1].start()
        for w in range(n):
            h = totals[w].shape[0] // 2
            theirs = outs[w].at[pl.ds((1 - c) * h, h), :]
            pltpu.make_async_remote_copy(
                src_ref=theirs, dst_ref=theirs, send_sem=send.at[w], recv_sem=recv.at[w],
                device_id=(x, y, 1 - c), device_id_type=MESH).wait_recv()
        for cp in cps:
            cp.wait_send()

    return pl.pallas_call(
        body, name=name, in_specs=[ANY] * n, out_specs=[ANY] * n,
        out_shape=[jax.ShapeDtypeStruct(t.shape, t.dtype) for t in totals],
        input_output_aliases={w: w for w in range(n)},
        scratch_shapes=[pltpu.SemaphoreType.DMA((n,)), pltpu.SemaphoreType.DMA((n,))],
    )(*totals)


def _allreduce_small(v, *, name):
    R, C = v.shape
    ND = 8

    def body(v_ref, o_ref, all_ref, send, recv, loc):
        x, y, c, chips = _place()
        me, sib = (x, y, c), (x, y, 1 - c)

        def rows(px, py, pc):
            return all_ref.at[pl.ds((4 * px + 2 * py + pc) * R, R), :]

        def copy(k, block, to, src=None):
            return pltpu.make_async_remote_copy(
                src_ref=rows(*block) if src is None else src, dst_ref=rows(*block),
                send_sem=send.at[k], recv_sem=recv.at[k], device_id=to, device_id_type=MESH)

        mine = pltpu.make_async_copy(v_ref, rows(*me), loc)
        mine.start()
        first = [copy(0, me, sib, src=v_ref)]
        first += [copy(1 + j, me, (*chip, c), src=v_ref) for j, chip in enumerate(chips)]
        for cp in first:
            cp.start()
        passed = [copy(4 + j, (*chip, c), sib) for j, chip in enumerate(chips)]
        for j, chip in enumerate(chips):
            copy(1 + j, (*chip, c), me).wait_recv()
            passed[j].start()
        copy(0, sib, me).wait_recv()
        for j, chip in enumerate(chips):
            copy(4 + j, (*chip, 1 - c), me).wait_recv()
        for cp in first + passed:
            cp.wait_send()
        mine.wait()
        acc = all_ref[0:R, :]
        for d in range(1, ND):
            acc = acc + all_ref[d * R:(d + 1) * R, :]
        o_ref[...] = acc

    vm = pl.BlockSpec(memory_space=pltpu.VMEM)
    return pl.pallas_call(
        body, name=name, in_specs=[vm], out_specs=[vm, vm],
        out_shape=[jax.ShapeDtypeStruct((R, C), F32), jax.ShapeDtypeStruct((ND * R, C), F32)],
        scratch_shapes=[pltpu.SemaphoreType.DMA((7,)), pltpu.SemaphoreType.DMA((7,)), pltpu.SemaphoreType.DMA],
        compiler_params=pltpu.CompilerParams(vmem_limit_bytes=VMEM_LIMIT),
    )(v)[0]


HBM = pl.BlockSpec(memory_space=pltpu.HBM)
SEM = pl.BlockSpec(memory_space=pltpu.SEMAPHORE)
EFFECT = pltpu.SideEffectType.DATAFLOW_SIDE_EFFECTING


def _remote(src, dst, send, recv, k, to):
    return pltpu.make_async_remote_copy(src_ref=src, dst_ref=dst, send_sem=send.at[k], recv_sem=recv.at[k],
                                        device_id=to, device_id_type=MESH)


def _split_start(bufs, plan, ncopies, *, name, after=None):
    nb = len(bufs)
    extra = [] if after is None else [after]

    def body(*refs):
        pos = nb + len(extra)
        send, recv, token = refs[pos], refs[pos + 1], refs[-1]
        for k, (src, dst, to) in enumerate(plan(refs[:nb])):
            _remote(src, dst, send, recv, k, to).start()
        token[...] = jnp.zeros_like(token)

    outs = pl.pallas_call(
        body, name=name,
        out_shape=(pltpu.SemaphoreType.DMA((ncopies,)), pltpu.SemaphoreType.DMA((ncopies,)),
                   *[pltpu.HBM(b.shape, b.dtype) for b in bufs], jax.ShapeDtypeStruct((SUBLANES, LANES), F32)),
        in_specs=[HBM] * nb + [ANY] * len(extra),
        out_specs=(SEM, SEM, *[HBM] * nb, pl.BlockSpec(memory_space=pltpu.VMEM)),
        input_output_aliases={i: 2 + i for i in range(nb)},
        compiler_params=pltpu.CompilerParams(has_side_effects=EFFECT),
    )(*[pltpu.with_memory_space_constraint(b, pltpu.HBM) for b in bufs], *extra)
    return outs[0], outs[1], list(outs[2:2 + nb]), outs[-1]


def _split_wait(started, plan, after, *, name):
    send, recv, bufs, _ = started
    nb = len(bufs)

    def body(*refs):
        send_sem, recv_sem = refs[nb], refs[nb + 1]
        for k, (src, dst, to) in enumerate(plan(refs[:nb])):
            cp = _remote(src, dst, send_sem, recv_sem, k, to)
            cp.wait_send()
            cp.wait_recv()

    outs = pl.pallas_call(
        body, name=name,
        out_shape=tuple(pltpu.HBM(b.shape, b.dtype) for b in bufs),
        in_specs=[HBM] * nb + [SEM, SEM, ANY], out_specs=tuple([HBM] * nb),
        input_output_aliases={i: i for i in range(nb)},
        compiler_params=pltpu.CompilerParams(has_side_effects=EFFECT),
    )(*bufs, send, recv, after)
    return list(outs)


def _gather_chip_plan(shapes):
    n = len(shapes)

    def plan(refs):
        srcs, lands = refs[:n], refs[n:]
        x, y, c, chips = _place()
        out = []
        for w in range(n):
            h = shapes[w][0] // 2
            for tx, ty in chips:
                out.append((srcs[w].at[pl.ds(c * h, h), :], lands[w].at[2 * x + y, pl.ds(c * h, h), :], (tx, ty, c)))
        return out

    return plan


def _gather_pair_plan(shapes):
    n = len(shapes)

    def plan(refs):
        srcs, lands = refs[:n], refs[n:]
        x, y, c, chips = _place()
        out = []
        for w in range(n):
            h = shapes[w][0] // 2
            for tx, ty in chips:
                half = lands[w].at[2 * tx + ty, pl.ds(c * h, h), :]
                out.append((half, half, (x, y, 1 - c)))
            out.append((srcs[w], lands[w].at[2 * x + y], (x, y, 1 - c)))
        return out

    return plan


def _reduce_chip_plan(n):
    def plan(refs):
        parts, lands = refs[:n], refs[n:]
        x, y, c, chips = _place()
        return [(parts[w].at[2 * tx + ty], lands[w].at[j], (tx, ty, c))
                for w in range(n) for j, (tx, ty) in enumerate(chips)]

    return plan


def _as_operands(gathered):
    out = {}
    for n, g in gathered.items():
        if n in ("w1_gate", "w1_up", "w2_gate", "w2_up", "w_xkv"):
            out[n] = g
        elif n == "w_in":
            out[n] = jnp.transpose(g, (1, 0, 2)).reshape(D_MODEL, IN_COLS)
        else:
            out[n] = g.reshape(g.shape[0] * g.shape[1], g.shape[2])
    return out


def _by_owner(n, g):
    if n == "w_in":
        return jnp.transpose(g.reshape(D_MODEL, N_CHIPS, IN_COLS // N_CHIPS), (1, 0, 2))
    if g.ndim == 2:
        return g.reshape(N_CHIPS, g.shape[0] // N_CHIPS, g.shape[1])
    return g


class _Comm:
    def __init__(self, shards):
        self.shards = shards
        self.total = {}
        self._flight = {}

    def gather_now(self, tag, names):
        got = _allgather_weights([self.shards[n] for n in names], name=f"gather_{tag}")
        return _as_operands(dict(zip(names, got)))

    def gather_start(self, tag, names, after):
        srcs = [self.shards[n] for n in names]
        lands = [lax.empty((N_CHIPS,) + s.shape, s.dtype) for s in srcs]
        started = _split_start(srcs + lands, _gather_chip_plan([s.shape for s in srcs]), 3 * len(srcs),
                               after=after, name=f"gather_{tag}_chips_start")
        self._flight[tag] = (names, started)
        return started[3]

    def gather_landed(self, tag, after):
        names, started = self._flight[tag]
        shapes = [self.shards[n].shape for n in names]
        bufs = _split_wait(started, _gather_chip_plan(shapes), after, name=f"gather_{tag}_chips_wait")
        started = _split_start(bufs, _gather_pair_plan(shapes), 4 * len(names), name=f"gather_{tag}_pair_start")
        self._flight[tag] = (names, started)
        return started[3]

    def gather_ready(self, tag, after):
        names, started = self._flight.pop(tag)
        shapes = [self.shards[n].shape for n in names]
        bufs = _split_wait(started, _gather_pair_plan(shapes), after, name=f"gather_{tag}_pair_wait")
        return _as_operands(dict(zip(names, bufs[len(names):])))

    def reduce_start(self, tag, grads):
        names = list(grads)
        local = [_by_owner(n, grads[n]) for n in names]
        from_sib = _pair_exchange(local, name=f"pair_exchange_{tag}")
        parts = [_pair_sum(g, s, name=f"pair_sum_{n}") for n, g, s in zip(names, local, from_sib)]
        lands = [lax.empty((N_CHIPS - 1,) + p.shape[1:], p.dtype) for p in parts]
        self._flight[tag] = (names, _split_start(parts + lands, _reduce_chip_plan(len(names)), 3 * len(names),
                                                 name=f"chip_exchange_{tag}_start"))
        return self._flight[tag][1][3]

    def reduce_finish(self, tag, after):
        names, started = self._flight.pop(tag)
        n = len(names)
        bufs = _split_wait(started, _reduce_chip_plan(n), after, name=f"chip_exchange_{tag}_wait")
        totals = [_chip_sum(p, s, name=f"chip_sum_{nm}") for nm, p, s in zip(names, bufs[:n], bufs[n:])]
        self.total.update(zip(names, _pair_gather(totals, name=f"pair_gather_{tag}")))


def _adamw(w, g, m, v, *, name):
    R, C = w.shape
    tr = math.gcd(R, 256)

    def body(w_ref, g_ref, m_ref, v_ref, d_ref, nm_ref, nv_ref):
        gg = g_ref[...]
        m_new = ADAM_B1 * m_ref[...] + (1.0 - ADAM_B1) * gg
        v_new = ADAM_B2 * v_ref[...] + (1.0 - ADAM_B2) * (gg * gg)
        m_hat = m_new / (1.0 - ADAM_B1 ** ADAM_STEP)
        v_hat = v_new / (1.0 - ADAM_B2 ** ADAM_STEP)
        d_ref[...] = -ADAM_LR * (m_hat / (jnp.sqrt(v_hat) + ADAM_EPS) + ADAM_WD * w_ref[...])
        nm_ref[...] = m_new
        nv_ref[...] = v_new

    blk = pl.BlockSpec((tr, C), lambda i: (i, 0))
    shp = jax.ShapeDtypeStruct((R, C), F32)
    return pl.pallas_call(
        body, name=name, grid=(R // tr,), in_specs=[blk] * 4, out_specs=[blk] * 3, out_shape=[shp] * 3,
        compiler_params=_params(("parallel",)),
    )(w, g, m, v)


def _to2d(a):
    flat = a.reshape(-1)
    pad = (-flat.shape[0]) % (SUBLANES * LANES)
    if pad:
        flat = jnp.pad(flat, (0, pad))
    return flat.reshape(-1, LANES)


def _small_rows(shape):
    return -(-math.prod(shape) // (SUBLANES * LANES)) * SUBLANES


def _pack_small(parts):
    rows = jnp.concatenate([_to2d(p) for p in parts], axis=0)
    pad = (-rows.shape[0]) % 256
    if pad:
        rows = jnp.concatenate([rows, jnp.zeros((pad, LANES), rows.dtype)], axis=0)
    return rows


def _unpack_small(rows, shapes):
    out, r = [], 0
    for shp in shapes:
        size = math.prod(shp)
        nrow = _small_rows(shp)
        out.append(rows[r:r + nrow].reshape(-1)[:size].reshape(shp))
        r += nrow
    return out


def kernel(x, mem, g_ffn1, w1_gate, w1_up, w1_down, g_mix, w_in, g_v, w_s, b_s, sinks, g_a_out, g_b_out, w_out, g_x, g_mem, w_xq, w_xkv, w_xo, g_ffn2, w2_gate, w2_up, w2_down, g_final, loss_target, m_g_ffn1, m_w1_gate, m_w1_up, m_w1_down, m_g_mix, m_w_in, m_g_v, m_w_s, m_b_s, m_sinks, m_g_a_out, m_g_b_out, m_w_out, m_g_x, m_g_mem, m_w_xq, m_w_xkv, m_w_xo, m_g_ffn2, m_w2_gate, m_w2_up, m_w2_down, m_g_final, v_g_ffn1, v_w1_gate, v_w1_up, v_w1_down, v_g_mix, v_w_in, v_g_v, v_w_s, v_b_s, v_sinks, v_g_a_out, v_g_b_out, v_w_out, v_g_x, v_g_mem, v_w_xq, v_w_xkv, v_w_xo, v_g_ffn2, v_w2_gate, v_w2_up, v_w2_down, v_g_final):
    args = dict(locals())
    Wp = {n: args[n] for n in ALL_W}
    Mp = {n: args["m_" + n] for n in ALL_W}
    Vp = {n: args["v_" + n] for n in ALL_W}

    comm = _Comm({n: Wp[n][0].astype(BF16) for n in BIG})
    W = {n: Wp[n] for n in SMALL}
    W["g_final"] = Wp["g_final"].reshape(1, D_MODEL)
    for n in ("w_s", "b_s"):
        W[n] = Wp[n][0]
    loss, dx, grads = _local_step(x[0], mem[0], loss_target[0], W, comm)
    big_grad = comm.total

    small_shapes = [Wp[n].shape for n in SMALL]
    packed = _pack_small([grads[n].reshape(Wp[n].shape) for n in SMALL] + [loss])
    summed = _allreduce_small(packed, name="allreduce_small")
    small_grad = dict(zip(SMALL, _unpack_small(summed, small_shapes)))
    nrows = sum(_small_rows(s) for s in small_shapes)
    loss_total = summed[nrows, 0]

    grad_out, delta, new_m, new_v = {}, {}, {}, {}
    for n in BIG:
        shp = Wp[n].shape
        grad_out[n] = big_grad[n].reshape(shp)
        d, nm, nv = _adamw(Wp[n][0], big_grad[n], Mp[n][0], Vp[n][0], name=f"adamw_{n}")
        delta[n], new_m[n], new_v[n] = d.reshape(shp), nm.reshape(shp), nv.reshape(shp)
    sw = _pack_small([Wp[n] for n in SMALL])
    sg = _pack_small([small_grad[n] for n in SMALL])
    sm = _pack_small([Mp[n] for n in SMALL])
    sv = _pack_small([Vp[n] for n in SMALL])
    d, nm, nv = _adamw(sw, sg, sm, sv, name="adamw_small")
    for n, dd, mm_, vv_ in zip(SMALL, _unpack_small(d, small_shapes), _unpack_small(nm, small_shapes),
                               _unpack_small(nv, small_shapes)):
        grad_out[n], delta[n], new_m[n], new_v[n] = small_grad[n], dd, mm_, vv_

    return (loss_total, dx[None], *[grad_out[n] for n in ALL_W], *[delta[n] for n in ALL_W],
            *[new_m[n] for n in ALL_W], *[new_v[n] for n in ALL_W])
```

```python
import functools
import math

import jax
import jax.numpy as jnp
from jax import lax
from jax.experimental import pallas as pl
from jax.experimental.pallas import tpu as pltpu

F32 = jnp.float32
BF16 = jnp.bfloat16
MESH = pl.DeviceIdType.MESH

D_MODEL = 2048
D_FF = 5632
D_A = 1024
D_B = 1024
CHUNK = 128
A_GROUPS = 8
HEAD_DIM = 64
B_Q_HEADS = 16
B_KV_HEADS = 2
X_HEADS = 4
X_HEAD_DIM = 512
IN_COLS = 3328
O_Q = 2 * D_A
O_K = O_Q + D_B
O_V = O_K + B_KV_HEADS * HEAD_DIM
N_CHIPS = 4
EPS = 1e-5
NEG = -1e30
ADAM_LR = 0.001
ADAM_B1 = 0.9
ADAM_B2 = 0.999
ADAM_EPS = 1e-08
ADAM_WD = 0.01
ADAM_STEP = 10

V7X_VMEM_BYTES = 64 * 1024 * 1024
VMEM_LIMIT = 56 * 1024 * 1024
LANES = 128
SUBLANES = 8


ANY = pl.BlockSpec(memory_space=pl.ANY)


def _params(sem, vmem=VMEM_LIMIT):
    return pltpu.CompilerParams(dimension_semantics=sem, vmem_limit_bytes=vmem)


def _matmul(pairs, *, M, N, K, tm, tn, tk, a_t=False, b_kind="n", out_kind="n", out_dtype=F32,
            scale=1.0, res=None, norm_g=None, order="ij", dep=None, name):
    tm, tn, tk = min(tm, M), min(tn, N), min(tk, K)
    assert M % tm == 0 and N % tn == 0 and K % tk == 0, (name, M, N, K, tm, tn, tk)
    nk = K // tk
    npairs = len(pairs)
    b_t = b_kind in ("t", "st")

    def ij(g0, g1):
        return (g0, g1) if order == "ij" else (g1, g0)

    def a_map(g0, g1, k):
        i, _ = ij(g0, g1)
        return (k, i) if a_t else (i, k)

    a_spec = pl.BlockSpec((tk, tm) if a_t else (tm, tk), a_map)

    b0 = pairs[0][1]
    if b_kind == "n":
        b_spec = pl.BlockSpec((tk, tn), lambda g0, g1, k: (k, ij(g0, g1)[1]))
    elif b_kind == "t":
        b_spec = pl.BlockSpec((tn, tk), lambda g0, g1, k: (ij(g0, g1)[1], k))
    elif b_kind == "sn":
        ns = b0.shape[2]
        assert ns % tn == 0
        nps = ns // tn
        b_spec = pl.BlockSpec((None, tk, tn), lambda g0, g1, k: (ij(g0, g1)[1] // nps, k, ij(g0, g1)[1] % nps))
    else:
        ks = b0.shape[2]
        assert ks % tk == 0
        kps = ks // tk
        b_spec = pl.BlockSpec((None, tn, tk), lambda g0, g1, k: (k // kps, ij(g0, g1)[1], k % kps))

    if out_kind == "n":
        o_spec = pl.BlockSpec((tm, tn), lambda g0, g1, k: ij(g0, g1))
        o_shape = jax.ShapeDtypeStruct((M, N), out_dtype)
    else:
        ns = N // N_CHIPS
        assert ns % tn == 0
        nps_o = ns // tn
        o_spec = pl.BlockSpec((None, tm, tn), lambda g0, g1, k: (ij(g0, g1)[1] // nps_o, ij(g0, g1)[0], ij(g0, g1)[1] % nps_o))
        o_shape = jax.ShapeDtypeStruct((N_CHIPS, M, ns), out_dtype)

    in_specs, args = [], []
    for a, b in pairs:
        in_specs += [a_spec, b_spec]
        args += [a, b]
    if res is not None:
        in_specs.append(pl.BlockSpec((tm, tn), lambda g0, g1, k: ij(g0, g1)))
        args.append(res)
    if norm_g is not None:
        assert tn == N and out_kind == "n"
        in_specs.append(pl.BlockSpec((1, N), lambda g0, g1, k: (0, 0)))
        args.append(norm_g)
    if dep is not None:
        in_specs.append(ANY)
        args.append(dep)

    dn = (((0,) if a_t else (1,), (1,) if b_t else (0,)), ((), ()))

    def body(*refs):
        pos = 2 * npairs
        res_ref = refs[pos] if res is not None else None
        pos += res is not None
        g_ref = refs[pos] if norm_g is not None else None
        pos += (norm_g is not None) + (dep is not None)
        o_ref = refs[pos]
        n_ref = refs[pos + 1] if norm_g is not None else None
        acc_ref = refs[-1] if nk > 1 else None
        part = None
        for p in range(npairs):
            d = lax.dot_general(refs[2 * p][...], refs[2 * p + 1][...], dn, preferred_element_type=F32)
            part = d if part is None else part + d

        def finish(acc):
            r = acc * scale if scale != 1.0 else acc
            if res_ref is not None:
                r = res_ref[...] + r
            o_ref[...] = r.astype(out_dtype)
            if n_ref is not None:
                n_ref[...] = (r * _rstd(r) * g_ref[...]).astype(BF16)

        if nk == 1:
            finish(part)
        else:
            k = pl.program_id(2)

            @pl.when(k == 0)
            def _():
                acc_ref[...] = part

            @pl.when((k > 0) & (k < nk - 1))
            def _():
                acc_ref[...] += part

            @pl.when(k == nk - 1)
            def _():
                finish(acc_ref[...] + part)

    grid = (M // tm, N // tn, nk) if order == "ij" else (N // tn, M // tm, nk)
    out_specs, out_shape = o_spec, o_shape
    if norm_g is not None:
        out_specs = [o_spec, pl.BlockSpec((tm, tn), lambda g0, g1, k: ij(g0, g1))]
        out_shape = [o_shape, jax.ShapeDtypeStruct((M, N), BF16)]
    return pl.pallas_call(
        body, name=name, grid=grid, in_specs=in_specs, out_specs=out_specs, out_shape=out_shape,
        scratch_shapes=[pltpu.VMEM((tm, tn), F32)] if nk > 1 else [],
        compiler_params=_params(("parallel", "parallel", "arbitrary")),
    )(*args)


def _rstd(x):
    return lax.rsqrt(jnp.mean(x * x, axis=-1, keepdims=True) + EPS)


def _rms_bwd_math(x, g, dy):
    r = _rstd(x)
    gy = dy * g
    xr = x * r
    dx = r * (gy - xr * jnp.mean(gy * xr, axis=-1, keepdims=True))
    return dx, dy * xr


def _rms_fwd(h, g, *, name, tm=512, dep=None):
    T, Dm = h.shape
    tm = min(tm, T)

    def body(h_ref, g_ref, *rest):
        x = h_ref[...]
        rest[-1][...] = (x * _rstd(x) * g_ref[...]).astype(BF16)

    return pl.pallas_call(
        body, name=name, grid=(T // tm,),
        in_specs=[pl.BlockSpec((tm, Dm), lambda i: (i, 0)), pl.BlockSpec((1, Dm), lambda i: (0, 0))]
        + ([ANY] if dep is not None else []),
        out_specs=pl.BlockSpec((tm, Dm), lambda i: (i, 0)),
        out_shape=jax.ShapeDtypeStruct((T, Dm), BF16),
        compiler_params=_params(("parallel",)),
    )(h, g, *([dep] if dep is not None else []))


def _rms_bwd(h, g, dn, dres, *, name, tm=256):
    T, Dm = h.shape
    tm = min(tm, T)
    has_res = dres is not None

    def body(*refs):
        h_ref, g_ref, dn_ref = refs[:3]
        pos = 3
        dres_ref = refs[pos] if has_res else None
        pos += has_res
        dh_ref, dhb_ref, dg_ref = refs[pos:pos + 3]
        dx, dgr = _rms_bwd_math(h_ref[...], g_ref[...], dn_ref[...].astype(F32))
        if has_res:
            dx = dres_ref[...] + dx
        dh_ref[...] = dx
        dhb_ref[...] = dx.astype(BF16)
        part = jnp.sum(dgr, axis=0, keepdims=True)

        @pl.when(pl.program_id(0) == 0)
        def _():
            dg_ref[...] = part

        @pl.when(pl.program_id(0) > 0)
        def _():
            dg_ref[...] += part

    row = pl.BlockSpec((tm, Dm), lambda i: (i, 0))
    vec = pl.BlockSpec((1, Dm), lambda i: (0, 0))
    args = [h, g, dn] + ([dres] if has_res else [])
    return pl.pallas_call(
        body, name=name, grid=(T // tm,),
        in_specs=[row, vec, row] + ([row] if has_res else []),
        out_specs=[row, row, vec],
        out_shape=[jax.ShapeDtypeStruct((T, Dm), F32), jax.ShapeDtypeStruct((T, Dm), BF16),
                   jax.ShapeDtypeStruct((1, Dm), F32)],
        compiler_params=_params(("arbitrary",)),
    )(*args)


def _loss_head(h, g, tgt, *, name, tm=256):
    T, Dm = h.shape
    tm = min(tm, T)

    def body(h_ref, g_ref, t_ref, dh_ref, dhb_ref, dg_ref, loss_ref):
        x = h_ref[...]
        gv = g_ref[...]
        r = _rstd(x)
        diff = x * r * gv - t_ref[...]
        lpart = 0.5 * jnp.sum(jnp.mean(diff * diff, axis=-1, keepdims=True), axis=0, keepdims=True)
        dx, dgr = _rms_bwd_math(x, gv, diff * (1.0 / Dm))
        dh_ref[...] = dx
        dhb_ref[...] = dx.astype(BF16)
        part = jnp.sum(dgr, axis=0, keepdims=True)
        lrow = jnp.broadcast_to(lpart, (1, LANES))

        @pl.when(pl.program_id(0) == 0)
        def _():
            dg_ref[...] = part
            loss_ref[...] = lrow

        @pl.when(pl.program_id(0) > 0)
        def _():
            dg_ref[...] += part
            loss_ref[...] += lrow

    row = pl.BlockSpec((tm, Dm), lambda i: (i, 0))
    vec = pl.BlockSpec((1, Dm), lambda i: (0, 0))
    return pl.pallas_call(
        body, name=name, grid=(T // tm,),
        in_specs=[row, vec, row],
        out_specs=[row, row, vec, pl.BlockSpec((1, LANES), lambda i: (0, 0))],
        out_shape=[jax.ShapeDtypeStruct((T, Dm), F32), jax.ShapeDtypeStruct((T, Dm), BF16),
                   jax.ShapeDtypeStruct((1, Dm), F32), jax.ShapeDtypeStruct((1, LANES), F32)],
        compiler_params=_params(("arbitrary",)),
    )(h, g, tgt)


MXU_COLS = 256
CHUNK_COLS = 2 * MXU_COLS


def _col_chunks(width):
    return [(c0, min(CHUNK_COLS, width - c0)) for c0 in range(0, width, CHUNK_COLS)]


def _row_block(rows, row_bytes, align, budget=24 * 1024 * 1024):
    fits = [d for d in range(align, rows + 1, align) if rows % d == 0 and 2 * d * row_bytes <= budget]
    assert fits, (rows, row_bytes)
    return fits[-1]


def _swiglu_up(n, wg, wu, *, name, tm=512):
    T, Dm = n.shape
    S, _, fs = wg.shape
    tm = min(tm, T)

    def body(n_ref, wg_ref, wu_ref, g_ref, u_ref, a_ref):
        x = n_ref[...]
        for c0, cw in _col_chunks(fs):
            cols = slice(c0, c0 + cw)
            g = jnp.dot(x, wg_ref[:, cols], preferred_element_type=F32)
            u = jnp.dot(x, wu_ref[:, cols], preferred_element_type=F32)
            g_ref[:, cols] = g.astype(BF16)
            u_ref[:, cols] = u.astype(BF16)
            a_ref[:, cols] = (g * jax.nn.sigmoid(g) * u).astype(BF16)

    wspec = pl.BlockSpec((None, Dm, fs), lambda j, i: (j, 0, 0))
    ospec = pl.BlockSpec((tm, fs), lambda j, i: (i, j))
    oshape = jax.ShapeDtypeStruct((T, S * fs), BF16)
    return pl.pallas_call(
        body, name=name, grid=(S, T // tm),
        in_specs=[pl.BlockSpec((tm, Dm), lambda j, i: (i, 0)), wspec, wspec],
        out_specs=[ospec, ospec, ospec], out_shape=[oshape, oshape, oshape],
        compiler_params=_params(("parallel", "parallel")),
    )(n, wg, wu)


def _swiglu_bwd_act(dhb, wd, G, U, *, name, tm=512, tn=1408):
    T, Dm = dhb.shape
    Fd = wd.shape[0]
    tm, tn = min(tm, T), min(tn, Fd)

    def body(dh_ref, wd_ref, g_ref, u_ref, dg_ref, du_ref):
        dh = dh_ref[...]
        for c0, cw in _col_chunks(tn):
            cols = slice(c0, c0 + cw)
            da = 0.5 * lax.dot_general(dh, wd_ref[cols, :], (((1,), (1,)), ((), ())), preferred_element_type=F32)
            g = g_ref[:, cols].astype(F32)
            u = u_ref[:, cols].astype(F32)
            sg = jax.nn.sigmoid(g)
            dg_ref[:, cols] = (da * u * (sg * (1.0 + g * (1.0 - sg)))).astype(BF16)
            du_ref[:, cols] = (da * (g * sg)).astype(BF16)

    blk = pl.BlockSpec((tm, tn), lambda j, i: (i, j))
    oshape = jax.ShapeDtypeStruct((T, Fd), BF16)
    return pl.pallas_call(
        body, name=name, grid=(Fd // tn, T // tm),
        in_specs=[pl.BlockSpec((tm, Dm), lambda j, i: (i, 0)), pl.BlockSpec((tn, Dm), lambda j, i: (j, 0)), blk, blk],
        out_specs=[blk, blk], out_shape=[oshape, oshape],
        compiler_params=_params(("parallel", "parallel")),
    )(dhb, wd, G, U)


_INV_SQRT2 = 0.7071067811865476
_INV_SQRT2PI = 0.3989422804014327


def _erf(x):
    ax = jnp.abs(x)
    t = 1.0 / (1.0 + 0.3275911 * ax)
    poly = t * (0.254829592 + t * (-0.284496736 + t * (1.421413741 + t * (-1.453152027 + t * 1.061405429))))
    y = 1.0 - poly * jnp.exp(-ax * ax)
    return jnp.where(x < 0, -y, y)


def _gelu_cdf(x):
    return 0.5 * (1.0 + _erf(x * _INV_SQRT2))


def _lane_lt64(shape):
    return lax.broadcasted_iota(jnp.int32, shape, len(shape) - 1) < HEAD_DIM


def _dup_half(x, kv):
    rolled = pltpu.roll(x, HEAD_DIM, 1)
    lo = _lane_lt64(x.shape)
    return jnp.where(lo, x, rolled) if kv == 0 else jnp.where(lo, rolled, x)


HEADS_PER_KV = B_Q_HEADS // B_KV_HEADS
PAIRS = HEADS_PER_KV // 2


def _attn_mask(block):
    shape = (HEADS_PER_KV * CHUNK, 2 * CHUNK)
    qpos = (lax.broadcasted_iota(jnp.int32, shape, 0) & (CHUNK - 1)) + CHUNK
    kpos = lax.broadcasted_iota(jnp.int32, shape, 1)
    diff = qpos - kpos
    first_key = jnp.where(block == 0, CHUNK, 0)
    return (diff >= 0) & (diff < CHUNK) & (kpos >= first_key)


def _stack_heads(tiles, lo):
    parts = []
    for t in tiles:
        parts += [jnp.where(lo, t, 0.0), jnp.where(lo, 0.0, t)]
    return jnp.concatenate(parts, axis=0)


def _unstack_heads(s, lo):
    return [jnp.where(lo, s[2 * p * CHUNK:(2 * p + 1) * CHUNK], s[(2 * p + 1) * CHUNK:(2 * p + 2) * CHUNK])
            for p in range(PAIRS)]


def _stack_sinks(sk_ref, kv):
    return jnp.concatenate([jnp.broadcast_to(sk_ref[:, h:h + 1], (CHUNK, 1))
                            for h in range(kv * HEADS_PER_KV, (kv + 1) * HEADS_PER_KV)], axis=0)


def _sgu_forward(z_ref, gv, wsm, bst):
    zu = z_ref[:, 0:D_A]
    zv = z_ref[:, D_A:2 * D_A]
    u = zu * _gelu_cdf(zu)
    v = zv * _gelu_cdf(zv)
    rv = _rstd(v)
    vn = (v * rv * gv).astype(BF16)
    svs = []
    for g in range(A_GROUPS):
        sl = slice(g * CHUNK, (g + 1) * CHUNK)
        svs.append(jnp.dot(wsm[g], vn[:, sl], preferred_element_type=F32) + bst[:, g:g + 1])
    sv = jnp.concatenate(svs, axis=1)
    return zu, zv, u, v, rv, vn, sv


def _masked_ws(ws_ref):
    tril = lax.broadcasted_iota(jnp.int32, (CHUNK, CHUNK), 0) >= lax.broadcasted_iota(jnp.int32, (CHUNK, CHUNK), 1)
    return [jnp.where(tril, ws_ref[g], 0.0).astype(BF16) for g in range(A_GROUPS)], tril


def _attn_probs(qm, kkd, sink, mask):
    s = lax.dot_general(qm, kkd, (((1,), (1,)), ((), ())), preferred_element_type=F32) * (HEAD_DIM ** -0.5)
    s = jnp.where(mask, s, NEG)
    m = jnp.maximum(jnp.max(s, axis=-1, keepdims=True), sink)
    e = jnp.exp(s - m)
    es = jnp.exp(sink - m)
    inv = 1.0 / (jnp.sum(e, axis=-1, keepdims=True) + es)
    return e * inv, es * inv


def _mixer_fwd(z, gv, ws, bst, sinks, ga, gb, *, name):
    T = z.shape[0]
    nb = T // CHUNK
    kvb = O_K // (2 * CHUNK)

    def body(z_ref, zp_ref, gv_ref, ws_ref, bst_ref, sk_ref, ga_ref, gb_ref, o_ref):
        i = pl.program_id(0)
        wsm, _ = _masked_ws(ws_ref)
        _, _, u, _, _, _, sv = _sgu_forward(z_ref, gv_ref[...], wsm, bst_ref[...])
        ya = u * sv
        o_ref[:, 0:D_A] = (ya * _rstd(ya) * ga_ref[...]).astype(BF16)

        mask = _attn_mask(i)
        kk = jnp.concatenate([zp_ref[:, 0:CHUNK], z_ref[:, O_K:O_V]], axis=0)
        vv = jnp.concatenate([zp_ref[:, CHUNK:2 * CHUNK], z_ref[:, O_V:IN_COLS]], axis=0)
        lo = _lane_lt64((CHUNK, LANES))
        outs = []
        for kv in range(B_KV_HEADS):
            kkd = _dup_half(kk, kv).astype(BF16)
            vvd = _dup_half(vv, kv).astype(BF16)
            q = _stack_heads([z_ref[:, O_Q + (kv * PAIRS + pr) * LANES:O_Q + (kv * PAIRS + pr + 1) * LANES]
                              for pr in range(PAIRS)], lo).astype(BF16)
            p, _ = _attn_probs(q, kkd, _stack_sinks(sk_ref, kv), mask)
            outs += _unstack_heads(jnp.dot(p.astype(BF16), vvd, preferred_element_type=F32), lo)
        yb = jnp.concatenate(outs, axis=1)
        o_ref[:, D_A:D_A + D_B] = (yb * _rstd(yb) * gb_ref[...]).astype(BF16)

    full = lambda shape: pl.BlockSpec(shape, lambda i: (0,) * len(shape))
    return pl.pallas_call(
        body, name=name, grid=(nb,),
        in_specs=[pl.BlockSpec((CHUNK, IN_COLS), lambda i: (i, 0)),
                  pl.BlockSpec((CHUNK, 2 * CHUNK), lambda i: (jnp.maximum(i - 1, 0), kvb)),
                  full((1, D_A)), full((A_GROUPS, CHUNK, CHUNK)), full((CHUNK, A_GROUPS)), full((1, B_Q_HEADS)),
                  full((1, D_A)), full((1, D_B))],
        out_specs=pl.BlockSpec((CHUNK, D_A + D_B), lambda i: (i, 0)),
        out_shape=jax.ShapeDtypeStruct((T, D_A + D_B), BF16),
        compiler_params=_params(("parallel",)),
    )(z, z, gv, ws, bst, sinks, ga, gb)


def _mixer_bwd(z, dyn, gv, ws, bst, sinks, ga, gb, *, name):
    T = z.shape[0]
    nb = T // CHUNK
    kvb = O_K // (2 * CHUNK)
    NT = (((0,), (0,)), ((), ()))

    def body(z_ref, zp_ref, dy_ref, gv_ref, ws_ref, bst_ref, sk_ref, ga_ref, gb_ref,
             dz_ref, dgv_ref, dws_ref, dbst_ref, dsk_ref, dga_ref, dgb_ref, carry_ref, p_ref):
        step = pl.program_id(0)
        i = nb - 1 - step

        @pl.when(step == 0)
        def _():
            carry_ref[...] = jnp.zeros_like(carry_ref)
            dgv_ref[...] = jnp.zeros_like(dgv_ref)
            dws_ref[...] = jnp.zeros_like(dws_ref)
            dbst_ref[...] = jnp.zeros_like(dbst_ref)
            dsk_ref[...] = jnp.zeros_like(dsk_ref)
            dga_ref[...] = jnp.zeros_like(dga_ref)
            dgb_ref[...] = jnp.zeros_like(dgb_ref)

        wsm, tril = _masked_ws(ws_ref)
        gvv = gv_ref[...]
        zu, zv, u, v, rv, vn, sv = _sgu_forward(z_ref, gvv, wsm, bst_ref[...])
        ya = u * sv
        dya, dga_rows = _rms_bwd_math(ya, ga_ref[...], dy_ref[:, 0:D_A].astype(F32))
        dga_ref[...] += jnp.sum(dga_rows, axis=0, keepdims=True)
        du = dya * sv
        dsv = dya * u
        dvn_parts = []
        for g in range(A_GROUPS):
            sl = slice(g * CHUNK, (g + 1) * CHUNK)
            dsv_g = dsv[:, sl]
            dsv_gb = dsv_g.astype(BF16)
            dw = lax.dot_general(dsv_gb, vn[:, sl], (((1,), (1,)), ((), ())), preferred_element_type=F32)
            dws_ref[g] += jnp.where(tril, dw, 0.0)
            dbst_ref[:, g:g + 1] += jnp.sum(dsv_g, axis=1, keepdims=True)
            dvn_parts.append(lax.dot_general(wsm[g], dsv_gb, NT, preferred_element_type=F32))
        dvn = jnp.concatenate(dvn_parts, axis=1)
        dv, dgv_rows = _rms_bwd_math(v, gvv, dvn)
        dgv_ref[...] += jnp.sum(dgv_rows, axis=0, keepdims=True)
        dz_ref[:, 0:D_A] = (du * (_gelu_cdf(zu) + zu * jnp.exp(-0.5 * zu * zu) * _INV_SQRT2PI)).astype(BF16)
        dz_ref[:, D_A:2 * D_A] = (dv * (_gelu_cdf(zv) + zv * jnp.exp(-0.5 * zv * zv) * _INV_SQRT2PI)).astype(BF16)

        mask = _attn_mask(i)
        kk = jnp.concatenate([zp_ref[:, 0:CHUNK], z_ref[:, O_K:O_V]], axis=0)
        vv = jnp.concatenate([zp_ref[:, CHUNK:2 * CHUNK], z_ref[:, O_V:IN_COLS]], axis=0)
        lo = _lane_lt64((CHUNK, LANES))
        kkd = [_dup_half(kk, kv).astype(BF16) for kv in range(B_KV_HEADS)]
        vvd = [_dup_half(vv, kv).astype(BF16) for kv in range(B_KV_HEADS)]
        outs, qs, psinks = [], [], []
        for kv in range(B_KV_HEADS):
            qs.append(_stack_heads([z_ref[:, O_Q + (kv * PAIRS + pr) * LANES:O_Q + (kv * PAIRS + pr + 1) * LANES]
                                    for pr in range(PAIRS)], lo).astype(BF16))
            p, ps = _attn_probs(qs[kv], kkd[kv], _stack_sinks(sk_ref, kv), mask)
            p_ref[kv] = p
            psinks.append(ps)
            outs += _unstack_heads(jnp.dot(p.astype(BF16), vvd[kv], preferred_element_type=F32), lo)
        yb = jnp.concatenate(outs, axis=1)
        dyb, dgb_rows = _rms_bwd_math(yb, gb_ref[...], dy_ref[:, D_A:D_A + D_B].astype(F32))
        dgb_ref[...] += jnp.sum(dgb_rows, axis=0, keepdims=True)

        dkk, dvv = [], []
        for kv in range(B_KV_HEADS):
            do = _stack_heads([dyb[:, (kv * PAIRS + pr) * LANES:(kv * PAIRS + pr + 1) * LANES]
                               for pr in range(PAIRS)], lo).astype(BF16)
            p = p_ref[kv]
            dvv.append(lax.dot_general(p.astype(BF16), do, NT, preferred_element_type=F32))
            dp = lax.dot_general(do, vvd[kv], (((1,), (1,)), ((), ())), preferred_element_type=F32)
            delta = jnp.sum(p * dp, axis=-1, keepdims=True)
            dsink = -psinks[kv] * delta
            for g in range(HEADS_PER_KV):
                h = kv * HEADS_PER_KV + g
                dsk_ref[:, h:h + 1] += jnp.sum(dsink[g * CHUNK:(g + 1) * CHUNK], axis=0, keepdims=True)
            ds = (p * (dp - delta) * (HEAD_DIM ** -0.5)).astype(BF16)
            dq = _unstack_heads(jnp.dot(ds, kkd[kv], preferred_element_type=F32), lo)
            for pr in range(PAIRS):
                c0 = O_Q + (kv * PAIRS + pr) * LANES
                dz_ref[:, c0:c0 + LANES] = dq[pr].astype(BF16)
            dkk.append(lax.dot_general(ds, qs[kv], NT, preferred_element_type=F32))

        def fold(parts):
            tot = [t + pltpu.roll(t, HEAD_DIM, 1) for t in parts]
            return jnp.where(_lane_lt64(tot[0].shape), tot[0], tot[1])

        dk_all = fold(dkk)
        dv_all = fold(dvv)
        dz_ref[:, O_K:O_V] = (dk_all[CHUNK:] + carry_ref[:, 0:CHUNK]).astype(BF16)
        dz_ref[:, O_V:IN_COLS] = (dv_all[CHUNK:] + carry_ref[:, CHUNK:2 * CHUNK]).astype(BF16)
        carry_ref[:, 0:CHUNK] = dk_all[:CHUNK]
        carry_ref[:, CHUNK:2 * CHUNK] = dv_all[:CHUNK]

    full = lambda shape: pl.BlockSpec(shape, lambda s: (0,) * len(shape))
    rev = lambda s: nb - 1 - s
    return pl.pallas_call(
        body, name=name, grid=(nb,),
        in_specs=[pl.BlockSpec((CHUNK, IN_COLS), lambda s: (rev(s), 0)),
                  pl.BlockSpec((CHUNK, 2 * CHUNK), lambda s: (jnp.maximum(rev(s) - 1, 0), kvb)),
                  pl.BlockSpec((CHUNK, D_A + D_B), lambda s: (rev(s), 0)),
                  full((1, D_A)), full((A_GROUPS, CHUNK, CHUNK)), full((CHUNK, A_GROUPS)), full((1, B_Q_HEADS)),
                  full((1, D_A)), full((1, D_B))],
        out_specs=[pl.BlockSpec((CHUNK, IN_COLS), lambda s: (rev(s), 0)),
                   full((1, D_A)), full((A_GROUPS, CHUNK, CHUNK)), full((CHUNK, A_GROUPS)), full((1, B_Q_HEADS)),
                   full((1, D_A)), full((1, D_B))],
        out_shape=[jax.ShapeDtypeStruct((T, IN_COLS), BF16), jax.ShapeDtypeStruct((1, D_A), F32),
                   jax.ShapeDtypeStruct((A_GROUPS, CHUNK, CHUNK), F32), jax.ShapeDtypeStruct((CHUNK, A_GROUPS), F32),
                   jax.ShapeDtypeStruct((1, B_Q_HEADS), F32), jax.ShapeDtypeStruct((1, D_A), F32),
                   jax.ShapeDtypeStruct((1, D_B), F32)],
        scratch_shapes=[pltpu.VMEM((CHUNK, 2 * CHUNK), F32), pltpu.VMEM((B_KV_HEADS, HEADS_PER_KV * CHUNK, 2 * CHUNK), F32)],
        compiler_params=_params(("arbitrary",)),
    )(z, z, dyn, gv, ws, bst, sinks, ga, gb)


def _xattn_probs(qh, kh):
    s = lax.dot_general(qh, kh, (((1,), (1,)), ((), ())), preferred_element_type=F32) * (X_HEAD_DIM ** -0.5)
    e = jnp.exp(s - jnp.max(s, axis=-1, keepdims=True))
    return e / jnp.sum(e, axis=-1, keepdims=True)


def _xattn_fwd(q, kvm, *, name, tm=512):
    T = q.shape[0]
    Mm = kvm.shape[0]
    tm = min(tm, T)

    def body(q_ref, kv_ref, o_ref):
        for h in range(X_HEADS):
            sl = slice(h * X_HEAD_DIM, (h + 1) * X_HEAD_DIM)
            kh = kv_ref[:, sl].astype(BF16)
            vh = kv_ref[:, D_MODEL + h * X_HEAD_DIM:D_MODEL + (h + 1) * X_HEAD_DIM].astype(BF16)
            p = _xattn_probs(q_ref[:, sl], kh)
            o_ref[:, sl] = jnp.dot(p.astype(BF16), vh, preferred_element_type=F32).astype(BF16)

    return pl.pallas_call(
        body, name=name, grid=(T // tm,),
        in_specs=[pl.BlockSpec((tm, D_MODEL), lambda i: (i, 0)), pl.BlockSpec((Mm, 2 * D_MODEL), lambda i: (0, 0))],
        out_specs=pl.BlockSpec((tm, D_MODEL), lambda i: (i, 0)),
        out_shape=jax.ShapeDtypeStruct((T, D_MODEL), BF16),
        compiler_params=_params(("parallel",)),
    )(q, kvm)


def _xattn_bwd(q, kvm, do, *, name, tm=512):
    T = q.shape[0]
    Mm = kvm.shape[0]
    tm = min(tm, T)
    NT = (((0,), (0,)), ((), ()))

    def body(q_ref, kv_ref, do_ref, dq_ref, dkv_ref):
        @pl.when(pl.program_id(0) == 0)
        def _():
            dkv_ref[...] = jnp.zeros_like(dkv_ref)

        for h in range(X_HEADS):
            sl = slice(h * X_HEAD_DIM, (h + 1) * X_HEAD_DIM)
            slv = slice(D_MODEL + h * X_HEAD_DIM, D_MODEL + (h + 1) * X_HEAD_DIM)
            kh = kv_ref[:, sl].astype(BF16)
            vh = kv_ref[:, slv].astype(BF16)
            qh = q_ref[:, sl]
            doh = do_ref[:, sl]
            p = _xattn_probs(qh, kh)
            dkv_ref[:, slv] += lax.dot_general(p.astype(BF16), doh, NT, preferred_element_type=F32)
            dp = lax.dot_general(doh, vh, (((1,), (1,)), ((), ())), preferred_element_type=F32)
            ds = (p * (dp - jnp.sum(p * dp, axis=-1, keepdims=True)) * (X_HEAD_DIM ** -0.5)).astype(BF16)
            dq_ref[:, sl] = jnp.dot(ds, kh, preferred_element_type=F32).astype(BF16)
            dkv_ref[:, sl] += lax.dot_general(ds, qh, NT, preferred_element_type=F32)

    row = pl.BlockSpec((tm, D_MODEL), lambda i: (i, 0))
    kvs = pl.BlockSpec((Mm, 2 * D_MODEL), lambda i: (0, 0))
    return pl.pallas_call(
        body, name=name, grid=(T // tm,),
        in_specs=[row, kvs, row], out_specs=[row, kvs],
        out_shape=[jax.ShapeDtypeStruct((T, D_MODEL), BF16), jax.ShapeDtypeStruct((Mm, 2 * D_MODEL), F32)],
        compiler_params=_params(("arbitrary",)),
    )(q, kvm, do)


def _swiglu_bwd_weights(tag, n, G, U, A, wd, dhb):
    T = n.shape[0]
    dG, dU = _swiglu_bwd_act(dhb, wd, G, U, name=f"{tag}_bwd_act")
    dwd = _matmul([(A, dhb)], M=D_FF, N=D_MODEL, K=T, tm=1408, tn=1024, tk=2048, a_t=True, out_dtype=BF16,
                  scale=0.5, name=f"{tag}_dwd")
    dwg = _matmul([(n, dG)], M=D_MODEL, N=D_FF, K=T, tm=1024, tn=1408, tk=2048, a_t=True, out_kind="s",
                  out_dtype=BF16, order="ji", name=f"{tag}_dwg")
    dwu = _matmul([(n, dU)], M=D_MODEL, N=D_FF, K=T, tm=1024, tn=1408, tk=2048, a_t=True, out_kind="s",
                  out_dtype=BF16, order="ji", name=f"{tag}_dwu")
    return dG, dU, dwg, dwu, dwd


def _swiglu_bwd_input(tag, hin, g_norm, dG, dU, wg, wu, dh, dep):
    T = hin.shape[0]
    dn = _matmul([(dG, wg), (dU, wu)], M=T, N=D_MODEL, K=D_FF, tm=512, tn=D_MODEL, tk=1408, b_kind="st",
                 dep=dep, name=f"{tag}_dn")
    return _rms_bwd(hin, g_norm, dn, dh, name=f"{tag}_norm_bwd")


GROUP_FFN1 = ["w1_gate", "w1_up", "w1_down"]
GROUP_MID = ["w_in", "w_out", "w_xq", "w_xkv", "w_xo"]
GROUP_FFN2 = ["w2_gate", "w2_up", "w2_down"]


def _local_step(x, mem, tgt, W, comm):
    T = x.shape[0]
    Mm = mem.shape[0]
    mm = functools.partial(_matmul)

    W = {**W, **comm.gather_now("ffn1_up", ["w1_gate", "w1_up"])}
    tok = comm.gather_start("ffn1_down", ["w1_down"], after=W["w1_up"])
    tok = comm.gather_start("mid", GROUP_MID, after=tok)
    tok = comm.gather_start("ffn2", GROUP_FFN2, after=tok)
    n1 = _rms_fwd(x, W["g_ffn1"], dep=tok, name="f_norm1")
    G1, U1, A1 = _swiglu_up(n1, W["w1_gate"], W["w1_up"], name="f_ffn1_up")
    tok = comm.gather_landed("ffn1_down", after=A1)
    tok = comm.gather_landed("mid", after=tok)
    W = {**W, **comm.gather_ready("ffn1_down", after=tok)}
    h1, n2 = mm([(A1, W["w1_down"])], M=T, N=D_MODEL, K=D_FF, tm=512, tn=D_MODEL, tk=1408, scale=0.5, res=x,
                norm_g=W["g_mix"], name="f_ffn1_down")
    W = {**W, **comm.gather_ready("mid", after=n2)}
    z = mm([(n2, W["w_in"])], M=T, N=IN_COLS, K=D_MODEL, tm=512, tn=IN_COLS // 2, tk=D_MODEL, name="f_w_in")
    bst = jnp.transpose(W["b_s"])
    yn = _mixer_fwd(z, W["g_v"], W["w_s"], bst, W["sinks"], W["g_a_out"], W["g_b_out"], name="f_mixer")
    tok = comm.gather_landed("ffn2", after=yn)
    h2, n3 = mm([(yn, W["w_out"])], M=T, N=D_MODEL, K=D_MODEL, tm=512, tn=D_MODEL, tk=D_MODEL, res=h1,
                norm_g=W["g_x"], dep=tok, name="f_w_out")
    memn = _rms_fwd(mem, W["g_mem"], name="f_norm_mem")
    q3 = mm([(n3, W["w_xq"])], M=T, N=D_MODEL, K=D_MODEL, tm=512, tn=D_MODEL, tk=D_MODEL, out_dtype=BF16,
            name="f_w_xq")
    kvm = mm([(memn, W["w_xkv"])], M=Mm, N=2 * D_MODEL, K=D_MODEL, tm=Mm, tn=1024, tk=D_MODEL, b_kind="sn",
             name="f_w_xkv")
    o3 = _xattn_fwd(q3, kvm, name="f_xattn")
    h3, n4 = mm([(o3, W["w_xo"])], M=T, N=D_MODEL, K=D_MODEL, tm=512, tn=D_MODEL, tk=D_MODEL, res=h2,
                norm_g=W["g_ffn2"], name="f_w_xo")
    W = {**W, **comm.gather_ready("ffn2", after=n4)}
    G2, U2, A2 = _swiglu_up(n4, W["w2_gate"], W["w2_up"], name="f_ffn2_up")
    h4 = mm([(A2, W["w2_down"])], M=T, N=D_MODEL, K=D_FF, tm=512, tn=D_MODEL, tk=1408, scale=0.5, res=h3,
            name="f_ffn2_down")

    grads = {}
    dh4, dh4b, grads["g_final"], loss = _loss_head(h4, W["g_final"], tgt, name="loss_head")
    dG2, dU2, dwg, dwu, dwd = _swiglu_bwd_weights("b_ffn2", n4, G2, U2, A2, W["w2_down"], dh4b)
    tok = comm.reduce_start("ffn2", {"w2_gate": dwg, "w2_up": dwu, "w2_down": dwd})
    dh3, dh3b, grads["g_ffn2"] = _swiglu_bwd_input("b_ffn2", h3, W["g_ffn2"], dG2, dU2, W["w2_gate"], W["w2_up"],
                                                   dh4, tok)

    mid = {}
    do3 = mm([(dh3b, W["w_xo"])], M=T, N=D_MODEL, K=D_MODEL, tm=512, tn=D_MODEL, tk=D_MODEL, b_kind="t",
             out_dtype=BF16, name="b_do3")
    mid["w_xo"] = mm([(o3, dh3b)], M=D_MODEL, N=D_MODEL, K=T, tm=1024, tn=D_MODEL, tk=1024, a_t=True,
                       out_dtype=BF16, name="b_dw_xo")
    dq3, dkvm = _xattn_bwd(q3, kvm, do3, name="b_xattn")
    mid["w_xq"] = mm([(n3, dq3)], M=D_MODEL, N=D_MODEL, K=T, tm=1024, tn=D_MODEL, tk=1024, a_t=True,
                       out_dtype=BF16, name="b_dw_xq")
    dn3 = mm([(dq3, W["w_xq"])], M=T, N=D_MODEL, K=D_MODEL, tm=512, tn=D_MODEL, tk=D_MODEL, b_kind="t",
             name="b_dn3")
    dh2, dh2b, grads["g_x"] = _rms_bwd(h2, W["g_x"], dn3, dh3, name="b_norm3")
    dkvmb = dkvm.astype(BF16)
    mid["w_xkv"] = mm([(memn, dkvmb)], M=D_MODEL, N=2 * D_MODEL, K=Mm, tm=D_MODEL, tn=1024, tk=Mm, a_t=True,
                        out_kind="s", out_dtype=BF16, name="b_dw_xkv")
    dmemn = mm([(dkvmb, W["w_xkv"])], M=Mm, N=D_MODEL, K=2 * D_MODEL, tm=Mm, tn=D_MODEL, tk=1024, b_kind="st",
               name="b_dmemn")
    _, _, grads["g_mem"] = _rms_bwd(mem, W["g_mem"], dmemn, None, name="b_norm_mem")
    comm.reduce_finish("ffn2", after=dh2b)

    dyn = mm([(dh2b, W["w_out"])], M=T, N=D_MODEL, K=D_MODEL, tm=512, tn=D_MODEL, tk=D_MODEL, b_kind="t",
             out_dtype=BF16, name="b_dyn")
    mid["w_out"] = mm([(yn, dh2b)], M=D_MODEL, N=D_MODEL, K=T, tm=1024, tn=D_MODEL, tk=1024, a_t=True,
                        out_dtype=BF16, name="b_dw_out")
    dz, grads["g_v"], grads["w_s"], dbst, grads["sinks"], grads["g_a_out"], grads["g_b_out"] = _mixer_bwd(
        z, dyn, W["g_v"], W["w_s"], bst, W["sinks"], W["g_a_out"], W["g_b_out"], name="b_mixer")
    grads["b_s"] = jnp.transpose(dbst)
    mid["w_in"] = mm([(n2, dz)], M=D_MODEL, N=IN_COLS, K=T, tm=1024, tn=IN_COLS, tk=512, a_t=True,
                     out_dtype=BF16, name="b_dw_in")
    tok = comm.reduce_start("mid", mid)
    dn2 = mm([(dz, W["w_in"])], M=T, N=D_MODEL, K=IN_COLS, tm=512, tn=D_MODEL, tk=IN_COLS, b_kind="t",
             dep=tok, name="b_dn2")
    dh1, dh1b, grads["g_mix"] = _rms_bwd(h1, W["g_mix"], dn2, dh2, name="b_norm2")

    dG1, dU1, dwg, dwu, dwd = _swiglu_bwd_weights("b_ffn1", n1, G1, U1, A1, W["w1_down"], dh1b)
    comm.reduce_finish("mid", after=dwu)
    tok = comm.reduce_start("ffn1", {"w1_gate": dwg, "w1_up": dwu, "w1_down": dwd})
    dx, _, grads["g_ffn1"] = _swiglu_bwd_input("b_ffn1", x, W["g_ffn1"], dG1, dU1, W["w1_gate"], W["w1_up"], dh1, tok)
    comm.reduce_finish("ffn1", after=dx)
    return loss, dx, grads


BIG = ["w1_gate", "w1_up", "w1_down", "w_in", "w_out", "w_xq", "w_xkv", "w_xo", "w2_gate", "w2_up", "w2_down"]
SMALL = ["g_ffn1", "g_mix", "g_v", "w_s", "b_s", "sinks", "g_a_out", "g_b_out", "g_x", "g_mem", "g_ffn2", "g_final"]
ALL_W = ["g_ffn1", "w1_gate", "w1_up", "w1_down", "g_mix", "w_in", "g_v", "w_s", "b_s", "sinks", "g_a_out",
         "g_b_out", "w_out", "g_x", "g_mem", "w_xq", "w_xkv", "w_xo", "g_ffn2", "w2_gate", "w2_up", "w2_down",
         "g_final"]
ANY = pl.BlockSpec(memory_space=pl.ANY)


def _place():
    x, y, c = lax.axis_index("x"), lax.axis_index("y"), lax.axis_index("c")
    chips = [(1 - x, y), (x, 1 - y), (1 - x, 1 - y)]
    return x, y, c, chips


def _allgather_weights(shards, *, name):
    n = len(shards)

    def body(*refs):
        ins, outs = refs[:n], refs[n:2 * n]
        send, recv, loc = refs[2 * n:]
        x, y, c, chips = _place()
        me = 2 * x + y
        sib = (x, y, 1 - c)

        def half(w, slot, hc):
            h = shards[w].shape[0] // 2
            return outs[w].at[slot, pl.ds(hc * h, h), :]

        def copy(w, k, slot, hc, to, src=None):
            return pltpu.make_async_remote_copy(
                src_ref=half(w, slot, hc) if src is None else src, dst_ref=half(w, slot, hc),
                send_sem=send.at[6 * w + k], recv_sem=recv.at[6 * w + k], device_id=to, device_id_type=MESH)

        own = [pltpu.make_async_remote_copy(
            src_ref=ins[w], dst_ref=outs[w].at[me], send_sem=loc.at[w], recv_sem=loc.at[n + w],
            device_id=sib, device_id_type=MESH) for w in range(n)]
        for cp in own:
            cp.start()
        first = []
        for w in range(n):
            h = shards[w].shape[0] // 2
            for j, (tx, ty) in enumerate(chips):
                first.append(copy(w, j, me, c, (tx, ty, c), src=ins[w].at[pl.ds(c * h, h), :]))
                first[-1].start()
        passed = []
        for w in range(n):
            for j, (tx, ty) in enumerate(chips):
                slot = 2 * tx + ty
                copy(w, j, slot, c, (tx, ty, c)).wait_recv()
                passed.append(copy(w, 3 + j, slot, c, sib))
                passed[-1].start()
        for w in range(n):
            for j, (tx, ty) in enumerate(chips):
                copy(w, 3 + j, 2 * tx + ty, 1 - c, sib).wait_recv()
        for cp in first + passed:
            cp.wait_send()
        for cp in own:
            cp.wait()

    return pl.pallas_call(
        body, name=name, in_specs=[ANY] * n, out_specs=[ANY] * n,
        out_shape=[jax.ShapeDtypeStruct((N_CHIPS,) + s.shape, s.dtype) for s in shards],
        scratch_shapes=[pltpu.SemaphoreType.DMA((6 * n,)), pltpu.SemaphoreType.DMA((6 * n,)),
                        pltpu.SemaphoreType.DMA((2 * n,))],
    )(*shards)


def _pair_exchange(grads, *, name):
    n = len(grads)

    def body(*refs):
        ins, outs = refs[:n], refs[n:2 * n]
        send, recv = refs[2 * n:]
        x, y, c, _ = _place()
        cps = []
        for w in range(n):
            h = grads[w].shape[1] // 2
            cps.append(pltpu.make_async_remote_copy(
                src_ref=ins[w].at[:, pl.ds((1 - c) * h, h), :], dst_ref=outs[w],
                send_sem=send.at[w], recv_sem=recv.at[w], device_id=(x, y, 1 - c), device_id_type=MESH))
            cps[-1].start()
        for cp in cps:
            cp.wait()

    return pl.pallas_call(
        body, name=name, in_specs=[ANY] * n, out_specs=[ANY] * n,
        out_shape=[jax.ShapeDtypeStruct((N_CHIPS, g.shape[1] // 2, g.shape[2]), g.dtype) for g in grads],
        scratch_shapes=[pltpu.SemaphoreType.DMA((n,)), pltpu.SemaphoreType.DMA((n,))],
    )(*grads)


def _pair_sum(g, got, *, name):
    S, R, C = g.shape
    h = R // 2
    tr = _row_block(h, 3 * C * 2, 16)
    nr = h // tr

    def body(a_ref, b_ref, o_ref):
        o_ref[...] = (a_ref[...].astype(F32) + b_ref[...].astype(F32)).astype(BF16)

    return pl.pallas_call(
        body, name=name, grid=(S, nr),
        in_specs=[pl.BlockSpec((None, tr, C), lambda s, r: (s, lax.axis_index("c") * nr + r, 0)),
                  pl.BlockSpec((None, tr, C), lambda s, r: (s, r, 0))],
        out_specs=pl.BlockSpec((None, tr, C), lambda s, r: (s, r, 0)),
        out_shape=jax.ShapeDtypeStruct((S, h, C), BF16),
        compiler_params=_params(("parallel", "parallel")),
    )(g, got)


def _chip_sum(part, got, *, name):
    S, h, C = part.shape
    tr = _row_block(h, 4 * C * 2 + C * 4, 16)
    nr = h // tr

    def body(own_ref, g0_ref, g1_ref, g2_ref, o_ref):
        acc = own_ref[...].astype(F32) + g0_ref[...].astype(F32)
        o_ref[...] = (acc + g1_ref[...].astype(F32)) + g2_ref[...].astype(F32)

    def piece(j):
        return pl.BlockSpec((None, tr, C), lambda r: (j, r, 0))

    return pl.pallas_call(
        body, name=name, grid=(nr,),
        in_specs=[pl.BlockSpec((None, tr, C), lambda r: (2 * lax.axis_index("x") + lax.axis_index("y"), r, 0)),
                  piece(0), piece(1), piece(2)],
        out_specs=pl.BlockSpec((tr, C), lambda r: (lax.axis_index("c") * nr + r, 0)),
        out_shape=jax.ShapeDtypeStruct((2 * h, C), F32),
        compiler_params=_params(("parallel",)),
    )(part, got, got, got)


def _pair_gather(totals, *, name):
    n = len(totals)

    def body(*refs):
        ins, outs = refs[:n], refs[n:2 * n]
        send, recv = refs[2 * n:]
        x, y, c, _ = _place()
        cps = []
        for w in range(n):
            h = totals[w].shape[0] // 2
            cps.append(pltpu.make_async_remote_copy(
                src_ref=ins[w].at[pl.ds(c * h, h), :], dst_ref=outs[w].at[pl.ds(c * h, h), :],
                send_sem=send.at[w], recv_sem=recv.at[w], device_id=(x, y, 1 - c), device_id_type=MESH))
            cps[-1].start()
        for w in range(n):
            h = totals[w].shape[0] // 2
            theirs = outs[w].at[pl.ds((1 - c) * h, h), :]
            pltpu.make_async_remote_copy(
                src_ref=theirs, dst_ref=theirs, send_sem=send.at[w], recv_sem=recv.at[w],
                device_id=(x, y, 1 - c), device_id_type=MESH).wait_recv()
        for cp in cps:
            cp.wait_send()

    return pl.pallas_call(
        body, name=name, in_specs=[ANY] * n, out_specs=[ANY] * n,
        out_shape=[jax.ShapeDtypeStruct(t.shape, t.dtype) for t in totals],
        input_output_aliases={w: w for w in range(n)},
        scratch_shapes=[pltpu.SemaphoreType.DMA((n,)), pltpu.SemaphoreType.DMA((n,))],
    )(*totals)


def _allreduce_small(v, *, name):
    R, C = v.shape
    ND = 8

    def body(v_ref, o_ref, all_ref, send, recv, loc):
        x, y, c, chips = _place()
        me, sib = (x, y, c), (x, y, 1 - c)

        def rows(px, py, pc):
            return all_ref.at[pl.ds((4 * px + 2 * py + pc) * R, R), :]

        def copy(k, block, to, src=None):
            return pltpu.make_async_remote_copy(
                src_ref=rows(*block) if src is None else src, dst_ref=rows(*block),
                send_sem=send.at[k], recv_sem=recv.at[k], device_id=to, device_id_type=MESH)

        mine = pltpu.make_async_copy(v_ref, rows(*me), loc)
        mine.start()
        first = [copy(0, me, sib, src=v_ref)]
        first += [copy(1 + j, me, (*chip, c), src=v_ref) for j, chip in enumerate(chips)]
        for cp in first:
            cp.start()
        passed = [copy(4 + j, (*chip, c), sib) for j, chip in enumerate(chips)]
        for j, chip in enumerate(chips):
            copy(1 + j, (*chip, c), me).wait_recv()
            passed[j].start()
        copy(0, sib, me).wait_recv()
        for j, chip in enumerate(chips):
            copy(4 + j, (*chip, 1 - c), me).wait_recv()
        for cp in first + passed:
            cp.wait_send()
        mine.wait()
        acc = all_ref[0:R, :]
        for d in range(1, ND):
            acc = acc + all_ref[d * R:(d + 1) * R, :]
        o_ref[...] = acc

    vm = pl.BlockSpec(memory_space=pltpu.VMEM)
    return pl.pallas_call(
        body, name=name, in_specs=[vm], out_specs=[vm, vm],
        out_shape=[jax.ShapeDtypeStruct((R, C), F32), jax.ShapeDtypeStruct((ND * R, C), F32)],
        scratch_shapes=[pltpu.SemaphoreType.DMA((7,)), pltpu.SemaphoreType.DMA((7,)), pltpu.SemaphoreType.DMA],
        compiler_params=pltpu.CompilerParams(vmem_limit_bytes=VMEM_LIMIT),
    )(v)[0]


HBM = pl.BlockSpec(memory_space=pltpu.HBM)
SEM = pl.BlockSpec(memory_space=pltpu.SEMAPHORE)
EFFECT = pltpu.SideEffectType.DATAFLOW_SIDE_EFFECTING


def _remote(src, dst, send, recv, k, to):
    return pltpu.make_async_remote_copy(src_ref=src, dst_ref=dst, send_sem=send.at[k], recv_sem=recv.at[k],
                                        device_id=to, device_id_type=MESH)


def _split_start(bufs, plan, ncopies, *, name, after=None):
    nb = len(bufs)
    extra = [] if after is None else [after]

    def body(*refs):
        pos = nb + len(extra)
        send, recv, token = refs[pos], refs[pos + 1], refs[-1]
        for k, (src, dst, to) in enumerate(plan(refs[:nb])):
            _remote(src, dst, send, recv, k, to).start()
        token[...] = jnp.zeros_like(token)

    outs = pl.pallas_call(
        body, name=name,
        out_shape=(pltpu.SemaphoreType.DMA((ncopies,)), pltpu.SemaphoreType.DMA((ncopies,)),
                   *[pltpu.HBM(b.shape, b.dtype) for b in bufs], jax.ShapeDtypeStruct((SUBLANES, LANES), F32)),
        in_specs=[HBM] * nb + [ANY] * len(extra),
        out_specs=(SEM, SEM, *[HBM] * nb, pl.BlockSpec(memory_space=pltpu.VMEM)),
        input_output_aliases={i: 2 + i for i in range(nb)},
        compiler_params=pltpu.CompilerParams(has_side_effects=EFFECT),
    )(*[pltpu.with_memory_space_constraint(b, pltpu.HBM) for b in bufs], *extra)
    return outs[0], outs[1], list(outs[2:2 + nb]), outs[-1]


def _split_wait(started, plan, after, *, name):
    send, recv, bufs, _ = started
    nb = len(bufs)

    def body(*refs):
        send_sem, recv_sem = refs[nb], refs[nb + 1]
        for k, (src, dst, to) in enumerate(plan(refs[:nb])):
            cp = _remote(src, dst, send_sem, recv_sem, k, to)
            cp.wait_send()
            cp.wait_recv()

    outs = pl.pallas_call(
        body, name=name,
        out_shape=tuple(pltpu.HBM(b.shape, b.dtype) for b in bufs),
        in_specs=[HBM] * nb + [SEM, SEM, ANY], out_specs=tuple([HBM] * nb),
        input_output_aliases={i: i for i in range(nb)},
        compiler_params=pltpu.CompilerParams(has_side_effects=EFFECT),
    )(*bufs, send, recv, after)
    return list(outs)


def _gather_chip_plan(shapes):
    n = len(shapes)

    def plan(refs):
        srcs, lands = refs[:n], refs[n:]
        x, y, c, chips = _place()
        out = []
        for w in range(n):
            h = shapes[w][0] // 2
            for tx, ty in chips:
                out.append((srcs[w].at[pl.ds(c * h, h), :], lands[w].at[2 * x + y, pl.ds(c * h, h), :], (tx, ty, c)))
        return out

    return plan


def _gather_pair_plan(shapes):
    n = len(shapes)

    def plan(refs):
        srcs, lands = refs[:n], refs[n:]
        x, y, c, chips = _place()
        out = []
        for w in range(n):
            h = shapes[w][0] // 2
            for tx, ty in chips:
                half = lands[w].at[2 * tx + ty, pl.ds(c * h, h), :]
                out.append((half, half, (x, y, 1 - c)))
            out.append((srcs[w], lands[w].at[2 * x + y], (x, y, 1 - c)))
        return out

    return plan


def _reduce_chip_plan(n):
    def plan(refs):
        parts, lands = refs[:n], refs[n:]
        x, y, c, chips = _place()
        return [(parts[w].at[2 * tx + ty], lands[w].at[j], (tx, ty, c))
                for w in range(n) for j, (tx, ty) in enumerate(chips)]

    return plan


def _as_operands(gathered):
    out = {}
    for n, g in gathered.items():
        if n in ("w1_gate", "w1_up", "w2_gate", "w2_up", "w_xkv"):
            out[n] = g
        elif n == "w_in":
            out[n] = jnp.transpose(g, (1, 0, 2)).reshape(D_MODEL, IN_COLS)
        else:
            out[n] = g.reshape(g.shape[0] * g.shape[1], g.shape[2])
    return out


def _by_owner(n, g):
    if n == "w_in":
        return jnp.transpose(g.reshape(D_MODEL, N_CHIPS, IN_COLS // N_CHIPS), (1, 0, 2))
    if g.ndim == 2:
        return g.reshape(N_CHIPS, g.shape[0] // N_CHIPS, g.shape[1])
    return g


class _Comm:
    def __init__(self, shards):
        self.shards = shards
        self.total = {}
        self._flight = {}

    def gather_now(self, tag, names):
        got = _allgather_weights([self.shards[n] for n in names], name=f"gather_{tag}")
        return _as_operands(dict(zip(names, got)))

    def gather_start(self, tag, names, after):
        srcs = [self.shards[n] for n in names]
        lands = [lax.empty((N_CHIPS,) + s.shape, s.dtype) for s in srcs]
        started = _split_start(srcs + lands, _gather_chip_plan([s.shape for s in srcs]), 3 * len(srcs),
                               after=after, name=f"gather_{tag}_chips_start")
        self._flight[tag] = (names, started)
        return started[3]

    def gather_landed(self, tag, after):
        names, started = self._flight[tag]
        shapes = [self.shards[n].shape for n in names]
        bufs = _split_wait(started, _gather_chip_plan(shapes), after, name=f"gather_{tag}_chips_wait")
        started = _split_start(bufs, _gather_pair_plan(shapes), 4 * len(names), name=f"gather_{tag}_pair_start")
        self._flight[tag] = (names, started)
        return started[3]

    def gather_ready(self, tag, after):
        names, started = self._flight.pop(tag)
        shapes = [self.shards[n].shape for n in names]
        bufs = _split_wait(started, _gather_pair_plan(shapes), after, name=f"gather_{tag}_pair_wait")
        return _as_operands(dict(zip(names, bufs[len(names):])))

    def reduce_start(self, tag, grads):
        names = list(grads)
        local = [_by_owner(n, grads[n]) for n in names]
        from_sib = _pair_exchange(local, name=f"pair_exchange_{tag}")
        parts = [_pair_sum(g, s, name=f"pair_sum_{n}") for n, g, s in zip(names, local, from_sib)]
        lands = [lax.empty((N_CHIPS - 1,) + p.shape[1:], p.dtype) for p in parts]
        self._flight[tag] = (names, _split_start(parts + lands, _reduce_chip_plan(len(names)), 3 * len(names),
                                                 name=f"chip_exchange_{tag}_start"))
        return self._flight[tag][1][3]

    def reduce_finish(self, tag, after):
        names, started = self._flight.pop(tag)
        n = len(names)
        bufs = _split_wait(started, _reduce_chip_plan(n), after, name=f"chip_exchange_{tag}_wait")
        totals = [_chip_sum(p, s, name=f"chip_sum_{nm}") for nm, p, s in zip(names, bufs[:n], bufs[n:])]
        self.total.update(zip(names, _pair_gather(totals, name=f"pair_gather_{tag}")))


def _adamw(w, g, m, v, *, name):
    R, C = w.shape
    tr = _row_block(R, 8 * C * 4, SUBLANES)

    def body(w_ref, g_ref, m_ref, v_ref, go_ref, d_ref, nm_ref, nv_ref):
        gg = g_ref[...]
        go_ref[...] = gg
        m_new = ADAM_B1 * m_ref[...] + (1.0 - ADAM_B1) * gg
        v_new = ADAM_B2 * v_ref[...] + (1.0 - ADAM_B2) * (gg * gg)
        m_hat = m_new / (1.0 - ADAM_B1 ** ADAM_STEP)
        v_hat = v_new / (1.0 - ADAM_B2 ** ADAM_STEP)
        d_ref[...] = -ADAM_LR * (m_hat / (jnp.sqrt(v_hat) + ADAM_EPS) + ADAM_WD * w_ref[...])
        nm_ref[...] = m_new
        nv_ref[...] = v_new

    blk = pl.BlockSpec((tr, C), lambda i: (i, 0))
    shp = jax.ShapeDtypeStruct((R, C), F32)
    return pl.pallas_call(
        body, name=name, grid=(R // tr,), in_specs=[blk] * 4, out_specs=[blk] * 4, out_shape=[shp] * 4,
        compiler_params=_params(("parallel",)),
    )(w, g, m, v)


def _to2d(a):
    flat = a.reshape(-1)
    pad = (-flat.shape[0]) % (SUBLANES * LANES)
    if pad:
        flat = jnp.pad(flat, (0, pad))
    return flat.reshape(-1, LANES)


def _small_rows(shape):
    return -(-math.prod(shape) // (SUBLANES * LANES)) * SUBLANES


def _pack_small(parts):
    rows = jnp.concatenate([_to2d(p) for p in parts], axis=0)
    pad = (-rows.shape[0]) % 256
    if pad:
        rows = jnp.concatenate([rows, jnp.zeros((pad, LANES), rows.dtype)], axis=0)
    return rows


def _unpack_small(rows, shapes):
    out, r = [], 0
    for shp in shapes:
        size = math.prod(shp)
        nrow = _small_rows(shp)
        out.append(rows[r:r + nrow].reshape(-1)[:size].reshape(shp))
        r += nrow
    return out


def kernel(x, mem, g_ffn1, w1_gate, w1_up, w1_down, g_mix, w_in, g_v, w_s, b_s, sinks, g_a_out, g_b_out, w_out, g_x, g_mem, w_xq, w_xkv, w_xo, g_ffn2, w2_gate, w2_up, w2_down, g_final, loss_target, m_g_ffn1, m_w1_gate, m_w1_up, m_w1_down, m_g_mix, m_w_in, m_g_v, m_w_s, m_b_s, m_sinks, m_g_a_out, m_g_b_out, m_w_out, m_g_x, m_g_mem, m_w_xq, m_w_xkv, m_w_xo, m_g_ffn2, m_w2_gate, m_w2_up, m_w2_down, m_g_final, v_g_ffn1, v_w1_gate, v_w1_up, v_w1_down, v_g_mix, v_w_in, v_g_v, v_w_s, v_b_s, v_sinks, v_g_a_out, v_g_b_out, v_w_out, v_g_x, v_g_mem, v_w_xq, v_w_xkv, v_w_xo, v_g_ffn2, v_w2_gate, v_w2_up, v_w2_down, v_g_final):
    args = dict(locals())
    Wp = {n: args[n] for n in ALL_W}
    Mp = {n: args["m_" + n] for n in ALL_W}
    Vp = {n: args["v_" + n] for n in ALL_W}

    comm = _Comm({n: Wp[n][0].astype(BF16) for n in BIG})
    W = {n: Wp[n] for n in SMALL}
    W["g_final"] = Wp["g_final"].reshape(1, D_MODEL)
    for n in ("w_s", "b_s"):
        W[n] = Wp[n][0]
    loss, dx, grads = _local_step(x[0], mem[0], loss_target[0], W, comm)
    big_grad = comm.total

    small_shapes = [Wp[n].shape for n in SMALL]
    packed = _pack_small([grads[n].reshape(Wp[n].shape) for n in SMALL] + [loss])
    summed = _allreduce_small(packed, name="allreduce_small")
    small_grad = dict(zip(SMALL, _unpack_small(summed, small_shapes)))
    nrows = sum(_small_rows(s) for s in small_shapes)
    loss_total = summed[nrows, 0]

    grad_out, delta, new_m, new_v = {}, {}, {}, {}
    for n in BIG:
        shp = Wp[n].shape
        g, d, nm, nv = _adamw(Wp[n][0], big_grad[n], Mp[n][0], Vp[n][0], name=f"adamw_{n}")
        grad_out[n], delta[n], new_m[n], new_v[n] = g.reshape(shp), d.reshape(shp), nm.reshape(shp), nv.reshape(shp)
    sw = _pack_small([Wp[n] for n in SMALL])
    sg = _pack_small([small_grad[n] for n in SMALL])
    sm = _pack_small([Mp[n] for n in SMALL])
    sv = _pack_small([Vp[n] for n in SMALL])
    _, d, nm, nv = _adamw(sw, sg, sm, sv, name="adamw_small")
    for n, dd, mm_, vv_ in zip(SMALL, _unpack_small(d, small_shapes), _unpack_small(nm, small_shapes),
                               _unpack_small(nv, small_shapes)):
        grad_out[n], delta[n], new_m[n], new_v[n] = small_grad[n], dd, mm_, vv_

    return (loss_total, dx[None], *[grad_out[n] for n in ALL_W], *[delta[n] for n in ALL_W],
            *[new_m[n] for n in ALL_W], *[new_v[n] for n in ALL_W])
```

```python
import functools
import math

import jax
import jax.numpy as jnp
from jax import lax
from jax.experimental import pallas as pl
from jax.experimental.pallas import tpu as pltpu

F32 = jnp.float32
BF16 = jnp.bfloat16
MESH = pl.DeviceIdType.MESH

D_MODEL = 2048
D_FF = 5632
D_A = 1024
D_B = 1024
CHUNK = 128
A_GROUPS = 8
HEAD_DIM = 64
B_Q_HEADS = 16
B_KV_HEADS = 2
X_HEADS = 4
X_HEAD_DIM = 512
IN_COLS = 3328
O_Q = 2 * D_A
O_K = O_Q + D_B
O_V = O_K + B_KV_HEADS * HEAD_DIM
N_CHIPS = 4
EPS = 1e-5
NEG = -1e30
ADAM_LR = 0.001
ADAM_B1 = 0.9
ADAM_B2 = 0.999
ADAM_EPS = 1e-08
ADAM_WD = 0.01
ADAM_STEP = 10

V7X_VMEM_BYTES = 64 * 1024 * 1024
VMEM_LIMIT = 56 * 1024 * 1024
LANES = 128
SUBLANES = 8


ANY = pl.BlockSpec(memory_space=pl.ANY)


def _params(sem, vmem=VMEM_LIMIT):
    return pltpu.CompilerParams(dimension_semantics=sem, vmem_limit_bytes=vmem)


def _matmul(pairs, *, M, N, K, tm, tn, tk, a_t=False, b_kind="n", out_kind="n", out_dtype=F32,
            scale=1.0, res=None, norm_g=None, order="ij", dep=None, name):
    tm, tn, tk = min(tm, M), min(tn, N), min(tk, K)
    assert M % tm == 0 and N % tn == 0 and K % tk == 0, (name, M, N, K, tm, tn, tk)
    nk = K // tk
    npairs = len(pairs)
    b_t = b_kind in ("t", "st")

    def ij(g0, g1):
        return (g0, g1) if order == "ij" else (g1, g0)

    def a_map(g0, g1, k):
        i, _ = ij(g0, g1)
        return (k, i) if a_t else (i, k)

    a_spec = pl.BlockSpec((tk, tm) if a_t else (tm, tk), a_map)

    b0 = pairs[0][1]
    if b_kind == "n":
        b_spec = pl.BlockSpec((tk, tn), lambda g0, g1, k: (k, ij(g0, g1)[1]))
    elif b_kind == "t":
        b_spec = pl.BlockSpec((tn, tk), lambda g0, g1, k: (ij(g0, g1)[1], k))
    elif b_kind == "sn":
        ns = b0.shape[2]
        assert ns % tn == 0
        nps = ns // tn
        b_spec = pl.BlockSpec((None, tk, tn), lambda g0, g1, k: (ij(g0, g1)[1] // nps, k, ij(g0, g1)[1] % nps))
    else:
        ks = b0.shape[2]
        assert ks % tk == 0
        kps = ks // tk
        b_spec = pl.BlockSpec((None, tn, tk), lambda g0, g1, k: (k // kps, ij(g0, g1)[1], k % kps))

    if out_kind == "n":
        o_spec = pl.BlockSpec((tm, tn), lambda g0, g1, k: ij(g0, g1))
        o_shape = jax.ShapeDtypeStruct((M, N), out_dtype)
    else:
        ns = N // N_CHIPS
        assert ns % tn == 0
        nps_o = ns // tn
        o_spec = pl.BlockSpec((None, tm, tn), lambda g0, g1, k: (ij(g0, g1)[1] // nps_o, ij(g0, g1)[0], ij(g0, g1)[1] % nps_o))
        o_shape = jax.ShapeDtypeStruct((N_CHIPS, M, ns), out_dtype)

    in_specs, args = [], []
    for a, b in pairs:
        in_specs += [a_spec, b_spec]
        args += [a, b]
    if res is not None:
        in_specs.append(pl.BlockSpec((tm, tn), lambda g0, g1, k: ij(g0, g1)))
        args.append(res)
    if norm_g is not None:
        assert tn == N and out_kind == "n"
        in_specs.append(pl.BlockSpec((1, N), lambda g0, g1, k: (0, 0)))
        args.append(norm_g)
    if dep is not None:
        in_specs.append(ANY)
        args.append(dep)

    dn = (((0,) if a_t else (1,), (1,) if b_t else (0,)), ((), ()))

    def body(*refs):
        pos = 2 * npairs
        res_ref = refs[pos] if res is not None else None
        pos += res is not None
        g_ref = refs[pos] if norm_g is not None else None
        pos += (norm_g is not None) + (dep is not None)
        o_ref = refs[pos]
        n_ref = refs[pos + 1] if norm_g is not None else None
        acc_ref = refs[-1] if nk > 1 else None
        part = None
        for p in range(npairs):
            d = lax.dot_general(refs[2 * p][...], refs[2 * p + 1][...], dn, preferred_element_type=F32)
            part = d if part is None else part + d

        def finish(acc):
            r = acc * scale if scale != 1.0 else acc
            if res_ref is not None:
                r = res_ref[...] + r
            o_ref[...] = r.astype(out_dtype)
            if n_ref is not None:
                n_ref[...] = (r * _rstd(r) * g_ref[...]).astype(BF16)

        if nk == 1:
            finish(part)
        else:
            k = pl.program_id(2)

            @pl.when(k == 0)
            def _():
                acc_ref[...] = part

            @pl.when((k > 0) & (k < nk - 1))
            def _():
                acc_ref[...] += part

            @pl.when(k == nk - 1)
            def _():
                finish(acc_ref[...] + part)

    grid = (M // tm, N // tn, nk) if order == "ij" else (N // tn, M // tm, nk)
    out_specs, out_shape = o_spec, o_shape
    if norm_g is not None:
        out_specs = [o_spec, pl.BlockSpec((tm, tn), lambda g0, g1, k: ij(g0, g1))]
        out_shape = [o_shape, jax.ShapeDtypeStruct((M, N), BF16)]
    return pl.pallas_call(
        body, name=name, grid=grid, in_specs=in_specs, out_specs=out_specs, out_shape=out_shape,
        scratch_shapes=[pltpu.VMEM((tm, tn), F32)] if nk > 1 else [],
        compiler_params=_params(("parallel", "parallel", "arbitrary")),
    )(*args)


def _rstd(x):
    return lax.rsqrt(jnp.mean(x * x, axis=-1, keepdims=True) + EPS)


def _rms_bwd_math(x, g, dy):
    r = _rstd(x)
    gy = dy * g
    xr = x * r
    dx = r * (gy - xr * jnp.mean(gy * xr, axis=-1, keepdims=True))
    return dx, dy * xr


def _rms_fwd(h, g, *, name, tm=512, dep=None):
    T, Dm = h.shape
    tm = min(tm, T)

    def body(h_ref, g_ref, *rest):
        x = h_ref[...]
        rest[-1][...] = (x * _rstd(x) * g_ref[...]).astype(BF16)

    return pl.pallas_call(
        body, name=name, grid=(T // tm,),
        in_specs=[pl.BlockSpec((tm, Dm), lambda i: (i, 0)), pl.BlockSpec((1, Dm), lambda i: (0, 0))]
        + ([ANY] if dep is not None else []),
        out_specs=pl.BlockSpec((tm, Dm), lambda i: (i, 0)),
        out_shape=jax.ShapeDtypeStruct((T, Dm), BF16),
        compiler_params=_params(("parallel",)),
    )(h, g, *([dep] if dep is not None else []))


def _rms_bwd(h, g, dn, dres, *, name, tm=256):
    T, Dm = h.shape
    tm = min(tm, T)
    has_res = dres is not None

    def body(*refs):
        h_ref, g_ref, dn_ref = refs[:3]
        pos = 3
        dres_ref = refs[pos] if has_res else None
        pos += has_res
        dh_ref, dhb_ref, dg_ref = refs[pos:pos + 3]
        dx, dgr = _rms_bwd_math(h_ref[...], g_ref[...], dn_ref[...].astype(F32))
        if has_res:
            dx = dres_ref[...] + dx
        dh_ref[...] = dx
        dhb_ref[...] = dx.astype(BF16)
        part = jnp.sum(dgr, axis=0, keepdims=True)

        @pl.when(pl.program_id(0) == 0)
        def _():
            dg_ref[...] = part

        @pl.when(pl.program_id(0) > 0)
        def _():
            dg_ref[...] += part

    row = pl.BlockSpec((tm, Dm), lambda i: (i, 0))
    vec = pl.BlockSpec((1, Dm), lambda i: (0, 0))
    args = [h, g, dn] + ([dres] if has_res else [])
    return pl.pallas_call(
        body, name=name, grid=(T // tm,),
        in_specs=[row, vec, row] + ([row] if has_res else []),
        out_specs=[row, row, vec],
        out_shape=[jax.ShapeDtypeStruct((T, Dm), F32), jax.ShapeDtypeStruct((T, Dm), BF16),
                   jax.ShapeDtypeStruct((1, Dm), F32)],
        compiler_params=_params(("arbitrary",)),
    )(*args)


def _loss_head(h, g, tgt, *, name, tm=256):
    T, Dm = h.shape
    tm = min(tm, T)

    def body(h_ref, g_ref, t_ref, dh_ref, dhb_ref, dg_ref, loss_ref):
        x = h_ref[...]
        gv = g_ref[...]
        r = _rstd(x)
        diff = x * r * gv - t_ref[...]
        lpart = 0.5 * jnp.sum(jnp.mean(diff * diff, axis=-1, keepdims=True), axis=0, keepdims=True)
        dx, dgr = _rms_bwd_math(x, gv, diff * (1.0 / Dm))
        dh_ref[...] = dx
        dhb_ref[...] = dx.astype(BF16)
        part = jnp.sum(dgr, axis=0, keepdims=True)
        lrow = jnp.broadcast_to(lpart, (1, LANES))

        @pl.when(pl.program_id(0) == 0)
        def _():
            dg_ref[...] = part
            loss_ref[...] = lrow

        @pl.when(pl.program_id(0) > 0)
        def _():
            dg_ref[...] += part
            loss_ref[...] += lrow

    row = pl.BlockSpec((tm, Dm), lambda i: (i, 0))
    vec = pl.BlockSpec((1, Dm), lambda i: (0, 0))
    return pl.pallas_call(
        body, name=name, grid=(T // tm,),
        in_specs=[row, vec, row],
        out_specs=[row, row, vec, pl.BlockSpec((1, LANES), lambda i: (0, 0))],
        out_shape=[jax.ShapeDtypeStruct((T, Dm), F32), jax.ShapeDtypeStruct((T, Dm), BF16),
                   jax.ShapeDtypeStruct((1, Dm), F32), jax.ShapeDtypeStruct((1, LANES), F32)],
        compiler_params=_params(("arbitrary",)),
    )(h, g, tgt)


MXU_COLS = 256
CHUNK_COLS = 2 * MXU_COLS


def _col_chunks(width):
    return [(c0, min(CHUNK_COLS, width - c0)) for c0 in range(0, width, CHUNK_COLS)]


def _row_block(rows, row_bytes, align, budget=24 * 1024 * 1024):
    fits = [d for d in range(align, rows + 1, align) if rows % d == 0 and 2 * d * row_bytes <= budget]
    assert fits, (rows, row_bytes)
    return fits[-1]


def _swiglu_up(n, wg, wu, *, name, tm=512):
    T, Dm = n.shape
    S, _, fs = wg.shape
    tm = min(tm, T)

    def body(n_ref, wg_ref, wu_ref, g_ref, u_ref, a_ref):
        x = n_ref[...]
        for c0, cw in _col_chunks(fs):
            cols = slice(c0, c0 + cw)
            g = jnp.dot(x, wg_ref[:, cols], preferred_element_type=F32)
            u = jnp.dot(x, wu_ref[:, cols], preferred_element_type=F32)
            g_ref[:, cols] = g.astype(BF16)
            u_ref[:, cols] = u.astype(BF16)
            a_ref[:, cols] = (g * jax.nn.sigmoid(g) * u).astype(BF16)

    wspec = pl.BlockSpec((None, Dm, fs), lambda j, i: (j, 0, 0))
    ospec = pl.BlockSpec((tm, fs), lambda j, i: (i, j))
    oshape = jax.ShapeDtypeStruct((T, S * fs), BF16)
    return pl.pallas_call(
        body, name=name, grid=(S, T // tm),
        in_specs=[pl.BlockSpec((tm, Dm), lambda j, i: (i, 0)), wspec, wspec],
        out_specs=[ospec, ospec, ospec], out_shape=[oshape, oshape, oshape],
        compiler_params=_params(("parallel", "parallel")),
    )(n, wg, wu)


def _swiglu_bwd_act(dhb, wd, G, U, *, name, tm=512, tn=1408):
    T, Dm = dhb.shape
    Fd = wd.shape[0]
    tm, tn = min(tm, T), min(tn, Fd)

    def body(dh_ref, wd_ref, g_ref, u_ref, dg_ref, du_ref):
        dh = dh_ref[...]
        for c0, cw in _col_chunks(tn):
            cols = slice(c0, c0 + cw)
            da = 0.5 * lax.dot_general(dh, wd_ref[cols, :], (((1,), (1,)), ((), ())), preferred_element_type=F32)
            g = g_ref[:, cols].astype(F32)
            u = u_ref[:, cols].astype(F32)
            sg = jax.nn.sigmoid(g)
            dg_ref[:, cols] = (da * u * (sg * (1.0 + g * (1.0 - sg)))).astype(BF16)
            du_ref[:, cols] = (da * (g * sg)).astype(BF16)

    blk = pl.BlockSpec((tm, tn), lambda j, i: (i, j))
    oshape = jax.ShapeDtypeStruct((T, Fd), BF16)
    return pl.pallas_call(
        body, name=name, grid=(Fd // tn, T // tm),
        in_specs=[pl.BlockSpec((tm, Dm), lambda j, i: (i, 0)), pl.BlockSpec((tn, Dm), lambda j, i: (j, 0)), blk, blk],
        out_specs=[blk, blk], out_shape=[oshape, oshape],
        compiler_params=_params(("parallel", "parallel")),
    )(dhb, wd, G, U)


_INV_SQRT2 = 0.7071067811865476
_INV_SQRT2PI = 0.3989422804014327


def _erf(x):
    ax = jnp.abs(x)
    t = 1.0 / (1.0 + 0.3275911 * ax)
    poly = t * (0.254829592 + t * (-0.284496736 + t * (1.421413741 + t * (-1.453152027 + t * 1.061405429))))
    y = 1.0 - poly * jnp.exp(-ax * ax)
    return jnp.where(x < 0, -y, y)


def _gelu_cdf(x):
    return 0.5 * (1.0 + _erf(x * _INV_SQRT2))


def _lane_lt64(shape):
    return lax.broadcasted_iota(jnp.int32, shape, len(shape) - 1) < HEAD_DIM


def _dup_half(x, kv):
    rolled = pltpu.roll(x, HEAD_DIM, 1)
    lo = _lane_lt64(x.shape)
    return jnp.where(lo, x, rolled) if kv == 0 else jnp.where(lo, rolled, x)


HEADS_PER_KV = B_Q_HEADS // B_KV_HEADS
PAIRS = HEADS_PER_KV // 2


def _attn_bias():
    shape = (HEADS_PER_KV * CHUNK, 2 * CHUNK)
    qpos = (lax.broadcasted_iota(jnp.int32, shape, 0) & (CHUNK - 1)) + CHUNK
    kpos = lax.broadcasted_iota(jnp.int32, shape, 1)
    diff = qpos - kpos
    band = (diff >= 0) & (diff < CHUNK)
    return jnp.stack([jnp.where(band & (kpos >= CHUNK), 0.0, NEG), jnp.where(band, 0.0, NEG)]).astype(F32)


def _stack_heads(tiles, lo):
    parts = []
    for t in tiles:
        parts += [jnp.where(lo, t, 0.0), jnp.where(lo, 0.0, t)]
    return jnp.concatenate(parts, axis=0)


def _unstack_heads(s, lo):
    return [jnp.where(lo, s[2 * p * CHUNK:(2 * p + 1) * CHUNK], s[(2 * p + 1) * CHUNK:(2 * p + 2) * CHUNK])
            for p in range(PAIRS)]


def _stack_sinks(sk_ref, kv):
    return jnp.concatenate([jnp.broadcast_to(sk_ref[:, h:h + 1], (CHUNK, 1))
                            for h in range(kv * HEADS_PER_KV, (kv + 1) * HEADS_PER_KV)], axis=0)


def _sgu_forward(z_ref, gv, wsm, bst):
    zu = z_ref[:, 0:D_A]
    zv = z_ref[:, D_A:2 * D_A]
    u = zu * _gelu_cdf(zu)
    v = zv * _gelu_cdf(zv)
    rv = _rstd(v)
    vn = (v * rv * gv).astype(BF16)
    svs = []
    for g in range(A_GROUPS):
        sl = slice(g * CHUNK, (g + 1) * CHUNK)
        svs.append(jnp.dot(wsm[g], vn[:, sl], preferred_element_type=F32) + bst[:, g:g + 1])
    sv = jnp.concatenate(svs, axis=1)
    return zu, zv, u, v, rv, vn, sv


def _masked_ws(ws_ref):
    tril = lax.broadcasted_iota(jnp.int32, (CHUNK, CHUNK), 0) >= lax.broadcasted_iota(jnp.int32, (CHUNK, CHUNK), 1)
    return [jnp.where(tril, ws_ref[g], 0.0).astype(BF16) for g in range(A_GROUPS)], tril


def _attn_probs(qm, kkd, sink, bias):
    s = lax.dot_general(qm, kkd, (((1,), (1,)), ((), ())), preferred_element_type=F32) * (HEAD_DIM ** -0.5) + bias
    m = jnp.maximum(jnp.max(s, axis=-1, keepdims=True), sink)
    e = jnp.exp(s - m)
    es = jnp.exp(sink - m)
    inv = 1.0 / (jnp.sum(e, axis=-1, keepdims=True) + es)
    return e * inv, es * inv


def _mixer_fwd(z, gv, ws, bst, sinks, ga, gb, *, name):
    T = z.shape[0]
    nb = T // CHUNK
    kvb = O_K // (2 * CHUNK)

    def body(z_ref, zp_ref, bias_ref, gv_ref, ws_ref, bst_ref, sk_ref, ga_ref, gb_ref, o_ref):
        wsm, _ = _masked_ws(ws_ref)
        _, _, u, _, _, _, sv = _sgu_forward(z_ref, gv_ref[...], wsm, bst_ref[...])
        ya = u * sv
        o_ref[:, 0:D_A] = (ya * _rstd(ya) * ga_ref[...]).astype(BF16)

        mask = bias_ref[...]
        kk = jnp.concatenate([zp_ref[:, 0:CHUNK], z_ref[:, O_K:O_V]], axis=0)
        vv = jnp.concatenate([zp_ref[:, CHUNK:2 * CHUNK], z_ref[:, O_V:IN_COLS]], axis=0)
        lo = _lane_lt64((CHUNK, LANES))
        outs = []
        for kv in range(B_KV_HEADS):
            kkd = _dup_half(kk, kv).astype(BF16)
            vvd = _dup_half(vv, kv).astype(BF16)
            q = _stack_heads([z_ref[:, O_Q + (kv * PAIRS + pr) * LANES:O_Q + (kv * PAIRS + pr + 1) * LANES]
                              for pr in range(PAIRS)], lo).astype(BF16)
            p, _ = _attn_probs(q, kkd, _stack_sinks(sk_ref, kv), mask)
            outs += _unstack_heads(jnp.dot(p.astype(BF16), vvd, preferred_element_type=F32), lo)
        yb = jnp.concatenate(outs, axis=1)
        o_ref[:, D_A:D_A + D_B] = (yb * _rstd(yb) * gb_ref[...]).astype(BF16)

    full = lambda shape: pl.BlockSpec(shape, lambda i: (0,) * len(shape))
    return pl.pallas_call(
        body, name=name, grid=(nb,),
        in_specs=[pl.BlockSpec((CHUNK, IN_COLS), lambda i: (i, 0)),
                  pl.BlockSpec((CHUNK, 2 * CHUNK), lambda i: (jnp.maximum(i - 1, 0), kvb)),
                  pl.BlockSpec((None, HEADS_PER_KV * CHUNK, 2 * CHUNK), lambda i: (jnp.minimum(i, 1), 0, 0)),
                  full((1, D_A)), full((A_GROUPS, CHUNK, CHUNK)), full((CHUNK, A_GROUPS)), full((1, B_Q_HEADS)),
                  full((1, D_A)), full((1, D_B))],
        out_specs=pl.BlockSpec((CHUNK, D_A + D_B), lambda i: (i, 0)),
        out_shape=jax.ShapeDtypeStruct((T, D_A + D_B), BF16),
        compiler_params=_params(("parallel",)),
    )(z, z, _attn_bias(), gv, ws, bst, sinks, ga, gb)


def _mixer_bwd(z, dyn, gv, ws, bst, sinks, ga, gb, *, name):
    T = z.shape[0]
    nb = T // CHUNK
    kvb = O_K // (2 * CHUNK)
    NT = (((0,), (0,)), ((), ()))

    def body(z_ref, zp_ref, dy_ref, bias_ref, gv_ref, ws_ref, bst_ref, sk_ref, ga_ref, gb_ref,
             dz_ref, dgv_ref, dws_ref, dbst_ref, dsk_ref, dga_ref, dgb_ref, carry_ref, p_ref):
        step = pl.program_id(0)

        @pl.when(step == 0)
        def _():
            carry_ref[...] = jnp.zeros_like(carry_ref)
            dgv_ref[...] = jnp.zeros_like(dgv_ref)
            dws_ref[...] = jnp.zeros_like(dws_ref)
            dbst_ref[...] = jnp.zeros_like(dbst_ref)
            dsk_ref[...] = jnp.zeros_like(dsk_ref)
            dga_ref[...] = jnp.zeros_like(dga_ref)
            dgb_ref[...] = jnp.zeros_like(dgb_ref)

        wsm, tril = _masked_ws(ws_ref)
        gvv = gv_ref[...]
        zu, zv, u, v, rv, vn, sv = _sgu_forward(z_ref, gvv, wsm, bst_ref[...])
        ya = u * sv
        dya, dga_rows = _rms_bwd_math(ya, ga_ref[...], dy_ref[:, 0:D_A].astype(F32))
        dga_ref[...] += jnp.sum(dga_rows, axis=0, keepdims=True)
        du = dya * sv
        dsv = dya * u
        dvn_parts = []
        for g in range(A_GROUPS):
            sl = slice(g * CHUNK, (g + 1) * CHUNK)
            dsv_g = dsv[:, sl]
            dsv_gb = dsv_g.astype(BF16)
            dw = lax.dot_general(dsv_gb, vn[:, sl], (((1,), (1,)), ((), ())), preferred_element_type=F32)
            dws_ref[g] += jnp.where(tril, dw, 0.0)
            dbst_ref[:, g:g + 1] += jnp.sum(dsv_g, axis=1, keepdims=True)
            dvn_parts.append(lax.dot_general(wsm[g], dsv_gb, NT, preferred_element_type=F32))
        dvn = jnp.concatenate(dvn_parts, axis=1)
        dv, dgv_rows = _rms_bwd_math(v, gvv, dvn)
        dgv_ref[...] += jnp.sum(dgv_rows, axis=0, keepdims=True)
        dz_ref[:, 0:D_A] = (du * (_gelu_cdf(zu) + zu * jnp.exp(-0.5 * zu * zu) * _INV_SQRT2PI)).astype(BF16)
        dz_ref[:, D_A:2 * D_A] = (dv * (_gelu_cdf(zv) + zv * jnp.exp(-0.5 * zv * zv) * _INV_SQRT2PI)).astype(BF16)

        mask = bias_ref[...]
        kk = jnp.concatenate([zp_ref[:, 0:CHUNK], z_ref[:, O_K:O_V]], axis=0)
        vv = jnp.concatenate([zp_ref[:, CHUNK:2 * CHUNK], z_ref[:, O_V:IN_COLS]], axis=0)
        lo = _lane_lt64((CHUNK, LANES))
        kkd = [_dup_half(kk, kv).astype(BF16) for kv in range(B_KV_HEADS)]
        vvd = [_dup_half(vv, kv).astype(BF16) for kv in range(B_KV_HEADS)]
        outs, qs, psinks = [], [], []
        for kv in range(B_KV_HEADS):
            qs.append(_stack_heads([z_ref[:, O_Q + (kv * PAIRS + pr) * LANES:O_Q + (kv * PAIRS + pr + 1) * LANES]
                                    for pr in range(PAIRS)], lo).astype(BF16))
            p, ps = _attn_probs(qs[kv], kkd[kv], _stack_sinks(sk_ref, kv), mask)
            p_ref[kv] = p
            psinks.append(ps)
            outs += _unstack_heads(jnp.dot(p.astype(BF16), vvd[kv], preferred_element_type=F32), lo)
        yb = jnp.concatenate(outs, axis=1)
        dyb, dgb_rows = _rms_bwd_math(yb, gb_ref[...], dy_ref[:, D_A:D_A + D_B].astype(F32))
        dgb_ref[...] += jnp.sum(dgb_rows, axis=0, keepdims=True)

        dkk, dvv = [], []
        for kv in range(B_KV_HEADS):
            do = _stack_heads([dyb[:, (kv * PAIRS + pr) * LANES:(kv * PAIRS + pr + 1) * LANES]
                               for pr in range(PAIRS)], lo).astype(BF16)
            p = p_ref[kv]
            dvv.append(lax.dot_general(p.astype(BF16), do, NT, preferred_element_type=F32))
            dp = lax.dot_general(do, vvd[kv], (((1,), (1,)), ((), ())), preferred_element_type=F32)
            delta = jnp.sum(p * dp, axis=-1, keepdims=True)
            dsink = -psinks[kv] * delta
            for g in range(HEADS_PER_KV):
                h = kv * HEADS_PER_KV + g
                dsk_ref[:, h:h + 1] += jnp.sum(dsink[g * CHUNK:(g + 1) * CHUNK], axis=0, keepdims=True)
            ds = (p * (dp - delta) * (HEAD_DIM ** -0.5)).astype(BF16)
            dq = _unstack_heads(jnp.dot(ds, kkd[kv], preferred_element_type=F32), lo)
            for pr in range(PAIRS):
                c0 = O_Q + (kv * PAIRS + pr) * LANES
                dz_ref[:, c0:c0 + LANES] = dq[pr].astype(BF16)
            dkk.append(lax.dot_general(ds, qs[kv], NT, preferred_element_type=F32))

        def fold(parts):
            tot = [t + pltpu.roll(t, HEAD_DIM, 1) for t in parts]
            return jnp.where(_lane_lt64(tot[0].shape), tot[0], tot[1])

        dk_all = fold(dkk)
        dv_all = fold(dvv)
        dz_ref[:, O_K:O_V] = (dk_all[CHUNK:] + carry_ref[:, 0:CHUNK]).astype(BF16)
        dz_ref[:, O_V:IN_COLS] = (dv_all[CHUNK:] + carry_ref[:, CHUNK:2 * CHUNK]).astype(BF16)
        carry_ref[:, 0:CHUNK] = dk_all[:CHUNK]
        carry_ref[:, CHUNK:2 * CHUNK] = dv_all[:CHUNK]

    full = lambda shape: pl.BlockSpec(shape, lambda s: (0,) * len(shape))
    rev = lambda s: nb - 1 - s
    return pl.pallas_call(
        body, name=name, grid=(nb,),
        in_specs=[pl.BlockSpec((CHUNK, IN_COLS), lambda s: (rev(s), 0)),
                  pl.BlockSpec((CHUNK, 2 * CHUNK), lambda s: (jnp.maximum(rev(s) - 1, 0), kvb)),
                  pl.BlockSpec((CHUNK, D_A + D_B), lambda s: (rev(s), 0)),
                  pl.BlockSpec((None, HEADS_PER_KV * CHUNK, 2 * CHUNK), lambda s: (jnp.minimum(rev(s), 1), 0, 0)),
                  full((1, D_A)), full((A_GROUPS, CHUNK, CHUNK)), full((CHUNK, A_GROUPS)), full((1, B_Q_HEADS)),
                  full((1, D_A)), full((1, D_B))],
        out_specs=[pl.BlockSpec((CHUNK, IN_COLS), lambda s: (rev(s), 0)),
                   full((1, D_A)), full((A_GROUPS, CHUNK, CHUNK)), full((CHUNK, A_GROUPS)), full((1, B_Q_HEADS)),
                   full((1, D_A)), full((1, D_B))],
        out_shape=[jax.ShapeDtypeStruct((T, IN_COLS), BF16), jax.ShapeDtypeStruct((1, D_A), F32),
                   jax.ShapeDtypeStruct((A_GROUPS, CHUNK, CHUNK), F32), jax.ShapeDtypeStruct((CHUNK, A_GROUPS), F32),
                   jax.ShapeDtypeStruct((1, B_Q_HEADS), F32), jax.ShapeDtypeStruct((1, D_A), F32),
                   jax.ShapeDtypeStruct((1, D_B), F32)],
        scratch_shapes=[pltpu.VMEM((CHUNK, 2 * CHUNK), F32), pltpu.VMEM((B_KV_HEADS, HEADS_PER_KV * CHUNK, 2 * CHUNK), F32)],
        compiler_params=_params(("arbitrary",)),
    )(z, z, dyn, _attn_bias(), gv, ws, bst, sinks, ga, gb)


def _xattn_probs(qh, kh):
    s = lax.dot_general(qh, kh, (((1,), (1,)), ((), ())), preferred_element_type=F32) * (X_HEAD_DIM ** -0.5)
    e = jnp.exp(s - jnp.max(s, axis=-1, keepdims=True))
    return e / jnp.sum(e, axis=-1, keepdims=True)


def _xattn_fwd(q, kvm, *, name, tm=512):
    T = q.shape[0]
    Mm = kvm.shape[0]
    tm = min(tm, T)

    def body(q_ref, kv_ref, o_ref):
        for h in range(X_HEADS):
            sl = slice(h * X_HEAD_DIM, (h + 1) * X_HEAD_DIM)
            kh = kv_ref[:, sl].astype(BF16)
            vh = kv_ref[:, D_MODEL + h * X_HEAD_DIM:D_MODEL + (h + 1) * X_HEAD_DIM].astype(BF16)
            p = _xattn_probs(q_ref[:, sl], kh)
            o_ref[:, sl] = jnp.dot(p.astype(BF16), vh, preferred_element_type=F32).astype(BF16)

    return pl.pallas_call(
        body, name=name, grid=(T // tm,),
        in_specs=[pl.BlockSpec((tm, D_MODEL), lambda i: (i, 0)), pl.BlockSpec((Mm, 2 * D_MODEL), lambda i: (0, 0))],
        out_specs=pl.BlockSpec((tm, D_MODEL), lambda i: (i, 0)),
        out_shape=jax.ShapeDtypeStruct((T, D_MODEL), BF16),
        compiler_params=_params(("parallel",)),
    )(q, kvm)


def _xattn_bwd(q, kvm, do, *, name, tm=512):
    T = q.shape[0]
    Mm = kvm.shape[0]
    tm = min(tm, T)
    NT = (((0,), (0,)), ((), ()))

    def body(q_ref, kv_ref, do_ref, dq_ref, dkv_ref):
        @pl.when(pl.program_id(0) == 0)
        def _():
            dkv_ref[...] = jnp.zeros_like(dkv_ref)

        for h in range(X_HEADS):
            sl = slice(h * X_HEAD_DIM, (h + 1) * X_HEAD_DIM)
            slv = slice(D_MODEL + h * X_HEAD_DIM, D_MODEL + (h + 1) * X_HEAD_DIM)
            kh = kv_ref[:, sl].astype(BF16)
            vh = kv_ref[:, slv].astype(BF16)
            qh = q_ref[:, sl]
            doh = do_ref[:, sl]
            p = _xattn_probs(qh, kh)
            dkv_ref[:, slv] += lax.dot_general(p.astype(BF16), doh, NT, preferred_element_type=F32)
            dp = lax.dot_general(doh, vh, (((1,), (1,)), ((), ())), preferred_element_type=F32)
            ds = (p * (dp - jnp.sum(p * dp, axis=-1, keepdims=True)) * (X_HEAD_DIM ** -0.5)).astype(BF16)
            dq_ref[:, sl] = jnp.dot(ds, kh, preferred_element_type=F32).astype(BF16)
            dkv_ref[:, sl] += lax.dot_general(ds, qh, NT, preferred_element_type=F32)

    row = pl.BlockSpec((tm, D_MODEL), lambda i: (i, 0))
    kvs = pl.BlockSpec((Mm, 2 * D_MODEL), lambda i: (0, 0))
    return pl.pallas_call(
        body, name=name, grid=(T // tm,),
        in_specs=[row, kvs, row], out_specs=[row, kvs],
        out_shape=[jax.ShapeDtypeStruct((T, D_MODEL), BF16), jax.ShapeDtypeStruct((Mm, 2 * D_MODEL), F32)],
        compiler_params=_params(("arbitrary",)),
    )(q, kvm, do)


def _swiglu_bwd_weights(tag, n, G, U, A, wd, dhb):
    T = n.shape[0]
    dG, dU = _swiglu_bwd_act(dhb, wd, G, U, name=f"{tag}_bwd_act", tm=1024)
    dwd = _matmul([(A, dhb)], M=D_FF, N=D_MODEL, K=T, tm=1408, tn=1024, tk=2048, a_t=True, out_dtype=BF16,
                  scale=0.5, name=f"{tag}_dwd")
    dwg = _matmul([(n, dG)], M=D_MODEL, N=D_FF, K=T, tm=1024, tn=1408, tk=2048, a_t=True, out_kind="s",
                  out_dtype=BF16, order="ji", name=f"{tag}_dwg")
    dwu = _matmul([(n, dU)], M=D_MODEL, N=D_FF, K=T, tm=1024, tn=1408, tk=2048, a_t=True, out_kind="s",
                  out_dtype=BF16, order="ji", name=f"{tag}_dwu")
    return dG, dU, dwg, dwu, dwd


def _swiglu_bwd_input(tag, hin, g_norm, dG, dU, wg, wu, dh, dep):
    T = hin.shape[0]
    dn = _matmul([(dG, wg), (dU, wu)], M=T, N=D_MODEL, K=D_FF, tm=512, tn=D_MODEL, tk=1408, b_kind="st",
                 out_dtype=BF16, dep=dep, name=f"{tag}_dn")
    return _rms_bwd(hin, g_norm, dn, dh, name=f"{tag}_norm_bwd")


GROUP_FFN1 = ["w1_gate", "w1_up", "w1_down"]
GROUP_MID = ["w_in", "w_out", "w_xq", "w_xkv", "w_xo"]
GROUP_FFN2 = ["w2_gate", "w2_up", "w2_down"]


def _local_step(x, mem, tgt, W, comm):
    T = x.shape[0]
    Mm = mem.shape[0]
    mm = functools.partial(_matmul)

    W = {**W, **comm.gather_now("ffn1_up", ["w1_gate", "w1_up"])}
    tok = comm.gather_start("ffn1_down", ["w1_down"], after=W["w1_up"])
    tok = comm.gather_start("mid", GROUP_MID, after=tok)
    tok = comm.gather_start("ffn2", GROUP_FFN2, after=tok)
    n1 = _rms_fwd(x, W["g_ffn1"], dep=tok, name="f_norm1")
    G1, U1, A1 = _swiglu_up(n1, W["w1_gate"], W["w1_up"], name="f_ffn1_up")
    tok = comm.gather_landed("ffn1_down", after=A1)
    tok = comm.gather_landed("mid", after=tok)
    W = {**W, **comm.gather_ready("ffn1_down", after=tok)}
    h1, n2 = mm([(A1, W["w1_down"])], M=T, N=D_MODEL, K=D_FF, tm=512, tn=D_MODEL, tk=1408, scale=0.5, res=x,
                norm_g=W["g_mix"], name="f_ffn1_down")
    W = {**W, **comm.gather_ready("mid", after=n2)}
    z = mm([(n2, W["w_in"])], M=T, N=IN_COLS, K=D_MODEL, tm=512, tn=IN_COLS // 2, tk=D_MODEL, name="f_w_in")
    bst = jnp.transpose(W["b_s"])
    yn = _mixer_fwd(z, W["g_v"], W["w_s"], bst, W["sinks"], W["g_a_out"], W["g_b_out"], name="f_mixer")
    tok = comm.gather_landed("ffn2", after=yn)
    h2, n3 = mm([(yn, W["w_out"])], M=T, N=D_MODEL, K=D_MODEL, tm=512, tn=D_MODEL, tk=D_MODEL, res=h1,
                norm_g=W["g_x"], dep=tok, name="f_w_out")
    memn = _rms_fwd(mem, W["g_mem"], name="f_norm_mem")
    q3 = mm([(n3, W["w_xq"])], M=T, N=D_MODEL, K=D_MODEL, tm=1024, tn=D_MODEL, tk=D_MODEL, out_dtype=BF16,
            name="f_w_xq")
    kvm = mm([(memn, W["w_xkv"])], M=Mm, N=2 * D_MODEL, K=D_MODEL, tm=Mm, tn=1024, tk=D_MODEL, b_kind="sn",
             name="f_w_xkv")
    o3 = _xattn_fwd(q3, kvm, name="f_xattn")
    h3, n4 = mm([(o3, W["w_xo"])], M=T, N=D_MODEL, K=D_MODEL, tm=512, tn=D_MODEL, tk=D_MODEL, res=h2,
                norm_g=W["g_ffn2"], name="f_w_xo")
    W = {**W, **comm.gather_ready("ffn2", after=n4)}
    G2, U2, A2 = _swiglu_up(n4, W["w2_gate"], W["w2_up"], name="f_ffn2_up")
    h4 = mm([(A2, W["w2_down"])], M=T, N=D_MODEL, K=D_FF, tm=512, tn=D_MODEL, tk=1408, scale=0.5, res=h3,
            name="f_ffn2_down")

    grads = {}
    dh4, dh4b, grads["g_final"], loss = _loss_head(h4, W["g_final"], tgt, name="loss_head")
    dG2, dU2, dwg, dwu, dwd = _swiglu_bwd_weights("b_ffn2", n4, G2, U2, A2, W["w2_down"], dh4b)
    tok = comm.reduce_pair_start("ffn2", {"w2_gate": dwg, "w2_up": dwu, "w2_down": dwd})
    dh3, dh3b, grads["g_ffn2"] = _swiglu_bwd_input("b_ffn2", h3, W["g_ffn2"], dG2, dU2, W["w2_gate"], W["w2_up"],
                                                   dh4, tok)
    tok = comm.reduce_pair_done("ffn2", after=dh3b)

    mid = {}
    do3 = mm([(dh3b, W["w_xo"])], M=T, N=D_MODEL, K=D_MODEL, tm=512, tn=D_MODEL, tk=D_MODEL, b_kind="t",
             out_dtype=BF16, dep=tok, name="b_do3")
    mid["w_xo"] = mm([(o3, dh3b)], M=D_MODEL, N=D_MODEL, K=T, tm=1024, tn=D_MODEL, tk=1024, a_t=True,
                       out_dtype=BF16, name="b_dw_xo")
    dq3, dkvm = _xattn_bwd(q3, kvm, do3, name="b_xattn")
    mid["w_xq"] = mm([(n3, dq3)], M=D_MODEL, N=D_MODEL, K=T, tm=1024, tn=D_MODEL, tk=1024, a_t=True,
                       out_dtype=BF16, name="b_dw_xq")
    dn3 = mm([(dq3, W["w_xq"])], M=T, N=D_MODEL, K=D_MODEL, tm=512, tn=D_MODEL, tk=D_MODEL, b_kind="t",
             out_dtype=BF16, name="b_dn3")
    dh2, dh2b, grads["g_x"] = _rms_bwd(h2, W["g_x"], dn3, dh3, name="b_norm3")
    dkvmb = dkvm.astype(BF16)
    mid["w_xkv"] = mm([(memn, dkvmb)], M=D_MODEL, N=2 * D_MODEL, K=Mm, tm=D_MODEL, tn=1024, tk=Mm, a_t=True,
                        out_kind="s", out_dtype=BF16, name="b_dw_xkv")
    dmemn = mm([(dkvmb, W["w_xkv"])], M=Mm, N=D_MODEL, K=2 * D_MODEL, tm=Mm, tn=D_MODEL, tk=1024, b_kind="st",
               name="b_dmemn")
    _, _, grads["g_mem"] = _rms_bwd(mem, W["g_mem"], dmemn, None, name="b_norm_mem")
    comm.reduce_finish("ffn2", after=dh2b)

    dyn = mm([(dh2b, W["w_out"])], M=T, N=D_MODEL, K=D_MODEL, tm=1024, tn=D_MODEL, tk=D_MODEL, b_kind="t",
             out_dtype=BF16, name="b_dyn")
    mid["w_out"] = mm([(yn, dh2b)], M=D_MODEL, N=D_MODEL, K=T, tm=1024, tn=D_MODEL, tk=1024, a_t=True,
                        out_dtype=BF16, name="b_dw_out")
    dz, grads["g_v"], grads["w_s"], dbst, grads["sinks"], grads["g_a_out"], grads["g_b_out"] = _mixer_bwd(
        z, dyn, W["g_v"], W["w_s"], bst, W["sinks"], W["g_a_out"], W["g_b_out"], name="b_mixer")
    grads["b_s"] = jnp.transpose(dbst)
    mid["w_in"] = mm([(n2, dz)], M=D_MODEL, N=IN_COLS, K=T, tm=1024, tn=IN_COLS, tk=1024, a_t=True,
                     out_dtype=BF16, name="b_dw_in")
    tok = comm.reduce_pair_start("mid", mid)
    dn2 = mm([(dz, W["w_in"])], M=T, N=D_MODEL, K=IN_COLS, tm=512, tn=D_MODEL, tk=IN_COLS, b_kind="t",
             out_dtype=BF16, dep=tok, name="b_dn2")
    comm.reduce_pair_done("mid", after=dn2)
    dh1, dh1b, grads["g_mix"] = _rms_bwd(h1, W["g_mix"], dn2, dh2, name="b_norm2")

    dG1, dU1, dwg, dwu, dwd = _swiglu_bwd_weights("b_ffn1", n1, G1, U1, A1, W["w1_down"], dh1b)
    comm.reduce_finish("mid", after=dwu)
    tok = comm.reduce_start("ffn1", {"w1_gate": dwg, "w1_up": dwu, "w1_down": dwd})
    dx, _, grads["g_ffn1"] = _swiglu_bwd_input("b_ffn1", x, W["g_ffn1"], dG1, dU1, W["w1_gate"], W["w1_up"], dh1, tok)
    comm.reduce_finish("ffn1", after=dx)
    return loss, dx, grads


BIG = ["w1_gate", "w1_up", "w1_down", "w_in", "w_out", "w_xq", "w_xkv", "w_xo", "w2_gate", "w2_up", "w2_down"]
SMALL = ["g_ffn1", "g_mix", "g_v", "w_s", "b_s", "sinks", "g_a_out", "g_b_out", "g_x", "g_mem", "g_ffn2", "g_final"]
ALL_W = ["g_ffn1", "w1_gate", "w1_up", "w1_down", "g_mix", "w_in", "g_v", "w_s", "b_s", "sinks", "g_a_out",
         "g_b_out", "w_out", "g_x", "g_mem", "w_xq", "w_xkv", "w_xo", "g_ffn2", "w2_gate", "w2_up", "w2_down",
         "g_final"]
ANY = pl.BlockSpec(memory_space=pl.ANY)


def _place():
    x, y, c = lax.axis_index("x"), lax.axis_index("y"), lax.axis_index("c")
    chips = [(1 - x, y), (x, 1 - y), (1 - x, 1 - y)]
    return x, y, c, chips


def _allgather_weights(shards, *, name):
    n = len(shards)

    def body(*refs):
        ins, outs = refs[:n], refs[n:2 * n]
        send, recv, loc = refs[2 * n:]
        x, y, c, chips = _place()
        me = 2 * x + y
        sib = (x, y, 1 - c)

        def half(w, slot, hc):
            h = shards[w].shape[0] // 2
            return outs[w].at[slot, pl.ds(hc * h, h), :]

        def copy(w, k, slot, hc, to, src=None):
            return pltpu.make_async_remote_copy(
                src_ref=half(w, slot, hc) if src is None else src, dst_ref=half(w, slot, hc),
                send_sem=send.at[6 * w + k], recv_sem=recv.at[6 * w + k], device_id=to, device_id_type=MESH)

        own = [pltpu.make_async_remote_copy(
            src_ref=ins[w], dst_ref=outs[w].at[me], send_sem=loc.at[w], recv_sem=loc.at[n + w],
            device_id=sib, device_id_type=MESH) for w in range(n)]
        for cp in own:
            cp.start()
        first = []
        for w in range(n):
            h = shards[w].shape[0] // 2
            for j, (tx, ty) in enumerate(chips):
                first.append(copy(w, j, me, c, (tx, ty, c), src=ins[w].at[pl.ds(c * h, h), :]))
                first[-1].start()
        passed = []
        for w in range(n):
            for j, (tx, ty) in enumerate(chips):
                slot = 2 * tx + ty
                copy(w, j, slot, c, (tx, ty, c)).wait_recv()
                passed.append(copy(w, 3 + j, slot, c, sib))
                passed[-1].start()
        for w in range(n):
            for j, (tx, ty) in enumerate(chips):
                copy(w, 3 + j, 2 * tx + ty, 1 - c, sib).wait_recv()
        for cp in first + passed:
            cp.wait_send()
        for cp in own:
            cp.wait()

    return pl.pallas_call(
        body, name=name, in_specs=[ANY] * n, out_specs=[ANY] * n,
        out_shape=[jax.ShapeDtypeStruct((N_CHIPS,) + s.shape, s.dtype) for s in shards],
        scratch_shapes=[pltpu.SemaphoreType.DMA((6 * n,)), pltpu.SemaphoreType.DMA((6 * n,)),
                        pltpu.SemaphoreType.DMA((2 * n,))],
    )(*shards)


def _pair_exchange(grads, *, name):
    n = len(grads)

    def body(*refs):
        ins, outs = refs[:n], refs[n:2 * n]
        send, recv = refs[2 * n:]
        x, y, c, _ = _place()
        cps = []
        for w in range(n):
            h = grads[w].shape[1] // 2
            cps.append(pltpu.make_async_remote_copy(
                src_ref=ins[w].at[:, pl.ds((1 - c) * h, h), :], dst_ref=outs[w],
                send_sem=send.at[w], recv_sem=recv.at[w], device_id=(x, y, 1 - c), device_id_type=MESH))
            cps[-1].start()
        for cp in cps:
            cp.wait()

    return pl.pallas_call(
        body, name=name, in_specs=[ANY] * n, out_specs=[ANY] * n,
        out_shape=[jax.ShapeDtypeStruct((N_CHIPS, g.shape[1] // 2, g.shape[2]), g.dtype) for g in grads],
        scratch_shapes=[pltpu.SemaphoreType.DMA((n,)), pltpu.SemaphoreType.DMA((n,))],
    )(*grads)


def _pair_sum(g, got, *, name):
    S, R, C = g.shape
    h = R // 2
    tr = _row_block(h, 3 * C * 2, 16)
    nr = h // tr

    def body(a_ref, b_ref, o_ref):
        o_ref[...] = (a_ref[...].astype(F32) + b_ref[...].astype(F32)).astype(BF16)

    return pl.pallas_call(
        body, name=name, grid=(S, nr),
        in_specs=[pl.BlockSpec((None, tr, C), lambda s, r: (s, lax.axis_index("c") * nr + r, 0)),
                  pl.BlockSpec((None, tr, C), lambda s, r: (s, r, 0))],
        out_specs=pl.BlockSpec((None, tr, C), lambda s, r: (s, r, 0)),
        out_shape=jax.ShapeDtypeStruct((S, h, C), BF16),
        compiler_params=_params(("parallel", "parallel")),
    )(g, got)


def _chip_sum(part, got, *, name):
    S, h, C = part.shape
    tr = _row_block(h, 4 * C * 2 + C * 4, 16)
    nr = h // tr

    def body(own_ref, g0_ref, g1_ref, g2_ref, o_ref):
        acc = own_ref[...].astype(F32) + g0_ref[...].astype(F32)
        o_ref[...] = (acc + g1_ref[...].astype(F32)) + g2_ref[...].astype(F32)

    def piece(j):
        return pl.BlockSpec((None, tr, C), lambda r: (j, r, 0))

    return pl.pallas_call(
        body, name=name, grid=(nr,),
        in_specs=[pl.BlockSpec((None, tr, C), lambda r: (2 * lax.axis_index("x") + lax.axis_index("y"), r, 0)),
                  piece(0), piece(1), piece(2)],
        out_specs=pl.BlockSpec((tr, C), lambda r: (lax.axis_index("c") * nr + r, 0)),
        out_shape=jax.ShapeDtypeStruct((2 * h, C), F32),
        compiler_params=_params(("parallel",)),
    )(part, got, got, got)


def _pair_gather(totals, *, name):
    n = len(totals)

    def body(*refs):
        ins, outs = refs[:n], refs[n:2 * n]
        send, recv = refs[2 * n:]
        x, y, c, _ = _place()
        cps = []
        for w in range(n):
            h = totals[w].shape[0] // 2
            cps.append(pltpu.make_async_remote_copy(
                src_ref=ins[w].at[pl.ds(c * h, h), :], dst_ref=outs[w].at[pl.ds(c * h, h), :],
                send_sem=send.at[w], recv_sem=recv.at[w], device_id=(x, y, 1 - c), device_id_type=MESH))
            cps[-1].start()
        for w in range(n):
            h = totals[w].shape[0] // 2
            theirs = outs[w].at[pl.ds((1 - c) * h, h), :]
            pltpu.make_async_remote_copy(
                src_ref=theirs, dst_ref=theirs, send_sem=send.at[w], recv_sem=recv.at[w],
                device_id=(x, y, 1 - c), device_id_type=MESH).wait_recv()
        for cp in cps:
            cp.wait_send()

    return pl.pallas_call(
        body, name=name, in_specs=[ANY] * n, out_specs=[ANY] * n,
        out_shape=[jax.ShapeDtypeStruct(t.shape, t.dtype) for t in totals],
        input_output_aliases={w: w for w in range(n)},
        scratch_shapes=[pltpu.SemaphoreType.DMA((n,)), pltpu.SemaphoreType.DMA((n,))],
    )(*totals)


def _allreduce_small(v, *, name):
    R, C = v.shape
    ND = 8

    def body(v_ref, o_ref, all_ref, send, recv, loc):
        x, y, c, chips = _place()
        me, sib = (x, y, c), (x, y, 1 - c)

        def rows(px, py, pc):
            return all_ref.at[pl.ds((4 * px + 2 * py + pc) * R, R), :]

        def copy(k, block, to, src=None):
            return pltpu.make_async_remote_copy(
                src_ref=rows(*block) if src is None else src, dst_ref=rows(*block),
                send_sem=send.at[k], recv_sem=recv.at[k], device_id=to, device_id_type=MESH)

        mine = pltpu.make_async_copy(v_ref, rows(*me), loc)
        mine.start()
        first = [copy(0, me, sib, src=v_ref)]
        first += [copy(1 + j, me, (*chip, c), src=v_ref) for j, chip in enumerate(chips)]
        for cp in first:
            cp.start()
        passed = [copy(4 + j, (*chip, c), sib) for j, chip in enumerate(chips)]
        for j, chip in enumerate(chips):
            copy(1 + j, (*chip, c), me).wait_recv()
            passed[j].start()
        copy(0, sib, me).wait_recv()
        for j, chip in enumerate(chips):
            copy(4 + j, (*chip, 1 - c), me).wait_recv()
        for cp in first + passed:
            cp.wait_send()
        mine.wait()
        acc = all_ref[0:R, :]
        for d in range(1, ND):
            acc = acc + all_ref[d * R:(d + 1) * R, :]
        o_ref[...] = acc

    vm = pl.BlockSpec(memory_space=pltpu.VMEM)
    return pl.pallas_call(
        body, name=name, in_specs=[vm], out_specs=[vm, vm],
        out_shape=[jax.ShapeDtypeStruct((R, C), F32), jax.ShapeDtypeStruct((ND * R, C), F32)],
        scratch_shapes=[pltpu.SemaphoreType.DMA((7,)), pltpu.SemaphoreType.DMA((7,)), pltpu.SemaphoreType.DMA],
        compiler_params=pltpu.CompilerParams(vmem_limit_bytes=VMEM_LIMIT),
    )(v)[0]


HBM = pl.BlockSpec(memory_space=pltpu.HBM)
SEM = pl.BlockSpec(memory_space=pltpu.SEMAPHORE)
EFFECT = pltpu.SideEffectType.DATAFLOW_SIDE_EFFECTING


def _remote(src, dst, send, recv, k, to):
    return pltpu.make_async_remote_copy(src_ref=src, dst_ref=dst, send_sem=send.at[k], recv_sem=recv.at[k],
                                        device_id=to, device_id_type=MESH)


def _split_start(bufs, plan, ncopies, *, name, after=None):
    nb = len(bufs)
    extra = [] if after is None else [after]

    def body(*refs):
        pos = nb + len(extra)
        send, recv, token = refs[pos], refs[pos + 1], refs[-1]
        for k, (src, dst, to) in enumerate(plan(refs[:nb])):
            _remote(src, dst, send, recv, k, to).start()
        token[...] = jnp.zeros_like(token)

    outs = pl.pallas_call(
        body, name=name,
        out_shape=(pltpu.SemaphoreType.DMA((ncopies,)), pltpu.SemaphoreType.DMA((ncopies,)),
                   *[pltpu.HBM(b.shape, b.dtype) for b in bufs], jax.ShapeDtypeStruct((SUBLANES, LANES), F32)),
        in_specs=[HBM] * nb + [ANY] * len(extra),
        out_specs=(SEM, SEM, *[HBM] * nb, pl.BlockSpec(memory_space=pltpu.VMEM)),
        input_output_aliases={i: 2 + i for i in range(nb)},
        compiler_params=pltpu.CompilerParams(has_side_effects=EFFECT),
    )(*[pltpu.with_memory_space_constraint(b, pltpu.HBM) for b in bufs], *extra)
    return outs[0], outs[1], list(outs[2:2 + nb]), outs[-1]


def _split_wait(started, plan, after, *, name):
    send, recv, bufs, _ = started
    nb = len(bufs)

    def body(*refs):
        send_sem, recv_sem = refs[nb], refs[nb + 1]
        for k, (src, dst, to) in enumerate(plan(refs[:nb])):
            cp = _remote(src, dst, send_sem, recv_sem, k, to)
            cp.wait_send()
            cp.wait_recv()

    outs = pl.pallas_call(
        body, name=name,
        out_shape=tuple(pltpu.HBM(b.shape, b.dtype) for b in bufs),
        in_specs=[HBM] * nb + [SEM, SEM, ANY], out_specs=tuple([HBM] * nb),
        input_output_aliases={i: i for i in range(nb)},
        compiler_params=pltpu.CompilerParams(has_side_effects=EFFECT),
    )(*bufs, send, recv, after)
    return list(outs)


def _gather_chip_plan(shapes):
    n = len(shapes)

    def plan(refs):
        srcs, lands = refs[:n], refs[n:]
        x, y, c, chips = _place()
        out = []
        for w in range(n):
            h = shapes[w][0] // 2
            for tx, ty in chips:
                out.append((srcs[w].at[pl.ds(c * h, h), :], lands[w].at[2 * x + y, pl.ds(c * h, h), :], (tx, ty, c)))
        return out

    return plan


def _gather_pair_plan(shapes):
    n = len(shapes)

    def plan(refs):
        srcs, lands = refs[:n], refs[n:]
        x, y, c, chips = _place()
        out = []
        for w in range(n):
            h = shapes[w][0] // 2
            for tx, ty in chips:
                half = lands[w].at[2 * tx + ty, pl.ds(c * h, h), :]
                out.append((half, half, (x, y, 1 - c)))
            out.append((srcs[w], lands[w].at[2 * x + y], (x, y, 1 - c)))
        return out

    return plan


def _reduce_pair_plan(shapes):
    n = len(shapes)

    def plan(refs):
        local, lands = refs[:n], refs[n:]
        x, y, c, _ = _place()
        out = []
        for w in range(n):
            h = shapes[w][1] // 2
            out.append((local[w].at[:, pl.ds((1 - c) * h, h), :], lands[w], (x, y, 1 - c)))
        return out

    return plan


def _reduce_chip_plan(n):
    def plan(refs):
        parts, lands = refs[:n], refs[n:]
        x, y, c, chips = _place()
        return [(parts[w].at[2 * tx + ty], lands[w].at[j], (tx, ty, c))
                for w in range(n) for j, (tx, ty) in enumerate(chips)]

    return plan


def _as_operands(gathered):
    out = {}
    for n, g in gathered.items():
        if n in ("w1_gate", "w1_up", "w2_gate", "w2_up", "w_xkv"):
            out[n] = g
        elif n == "w_in":
            out[n] = jnp.transpose(g, (1, 0, 2)).reshape(D_MODEL, IN_COLS)
        else:
            out[n] = g.reshape(g.shape[0] * g.shape[1], g.shape[2])
    return out


def _by_owner(n, g):
    if n == "w_in":
        return jnp.transpose(g.reshape(D_MODEL, N_CHIPS, IN_COLS // N_CHIPS), (1, 0, 2))
    if g.ndim == 2:
        return g.reshape(N_CHIPS, g.shape[0] // N_CHIPS, g.shape[1])
    return g


class _Comm:
    def __init__(self, shards):
        self.shards = shards
        self.total = {}
        self._flight = {}

    def gather_now(self, tag, names):
        got = _allgather_weights([self.shards[n] for n in names], name=f"gather_{tag}")
        return _as_operands(dict(zip(names, got)))

    def gather_start(self, tag, names, after):
        srcs = [self.shards[n] for n in names]
        lands = [lax.empty((N_CHIPS,) + s.shape, s.dtype) for s in srcs]
        started = _split_start(srcs + lands, _gather_chip_plan([s.shape for s in srcs]), 3 * len(srcs),
                               after=after, name=f"gather_{tag}_chips_start")
        self._flight[tag] = (names, started)
        return started[3]

    def gather_landed(self, tag, after):
        names, started = self._flight[tag]
        shapes = [self.shards[n].shape for n in names]
        bufs = _split_wait(started, _gather_chip_plan(shapes), after, name=f"gather_{tag}_chips_wait")
        started = _split_start(bufs, _gather_pair_plan(shapes), 4 * len(names), name=f"gather_{tag}_pair_start")
        self._flight[tag] = (names, started)
        return started[3]

    def gather_ready(self, tag, after):
        names, started = self._flight.pop(tag)
        shapes = [self.shards[n].shape for n in names]
        bufs = _split_wait(started, _gather_pair_plan(shapes), after, name=f"gather_{tag}_pair_wait")
        return _as_operands(dict(zip(names, bufs[len(names):])))

    def reduce_start(self, tag, grads):
        names = list(grads)
        local = [_by_owner(n, grads[n]) for n in names]
        return self._chip_start(tag, names, local, _pair_exchange(local, name=f"pair_exchange_{tag}"))

    def reduce_pair_start(self, tag, grads):
        names = list(grads)
        local = [_by_owner(n, grads[n]) for n in names]
        lands = [lax.empty((N_CHIPS, g.shape[1] // 2, g.shape[2]), g.dtype) for g in local]
        started = _split_start(local + lands, _reduce_pair_plan([g.shape for g in local]), len(names),
                               name=f"pair_exchange_{tag}_start")
        self._flight[tag] = (names, started)
        return started[3]

    def reduce_pair_done(self, tag, after):
        names, started = self._flight.pop(tag)
        n = len(names)
        bufs = _split_wait(started, _reduce_pair_plan([b.shape for b in started[2][:n]]), after,
                           name=f"pair_exchange_{tag}_wait")
        return self._chip_start(tag, names, bufs[:n], bufs[n:])

    def _chip_start(self, tag, names, local, from_sib):
        parts = [_pair_sum(g, s, name=f"pair_sum_{n}") for n, g, s in zip(names, local, from_sib)]
        lands = [lax.empty((N_CHIPS - 1,) + p.shape[1:], p.dtype) for p in parts]
        self._flight[tag] = (names, _split_start(parts + lands, _reduce_chip_plan(len(names)), 3 * len(names),
                                                 name=f"chip_exchange_{tag}_start"))
        return self._flight[tag][1][3]

    def reduce_finish(self, tag, after):
        names, started = self._flight.pop(tag)
        n = len(names)
        bufs = _split_wait(started, _reduce_chip_plan(n), after, name=f"chip_exchange_{tag}_wait")
        totals = [_chip_sum(p, s, name=f"chip_sum_{nm}") for nm, p, s in zip(names, bufs[:n], bufs[n:])]
        self.total.update(zip(names, _pair_gather(totals, name=f"pair_gather_{tag}")))


def _adamw(w, g, m, v, *, name):
    R, C = w.shape
    tr = _row_block(R, 8 * C * 4, SUBLANES)

    def body(w_ref, g_ref, m_ref, v_ref, go_ref, d_ref, nm_ref, nv_ref):
        gg = g_ref[...]
        go_ref[...] = gg
        m_new = ADAM_B1 * m_ref[...] + (1.0 - ADAM_B1) * gg
        v_new = ADAM_B2 * v_ref[...] + (1.0 - ADAM_B2) * (gg * gg)
        m_hat = m_new / (1.0 - ADAM_B1 ** ADAM_STEP)
        v_hat = v_new / (1.0 - ADAM_B2 ** ADAM_STEP)
        d_ref[...] = -ADAM_LR * (m_hat / (jnp.sqrt(v_hat) + ADAM_EPS) + ADAM_WD * w_ref[...])
        nm_ref[...] = m_new
        nv_ref[...] = v_new

    blk = pl.BlockSpec((tr, C), lambda i: (i, 0))
    shp = jax.ShapeDtypeStruct((R, C), F32)
    return pl.pallas_call(
        body, name=name, grid=(R // tr,), in_specs=[blk] * 4, out_specs=[blk] * 4, out_shape=[shp] * 4,
        compiler_params=_params(("parallel",)),
    )(w, g, m, v)


def _to2d(a):
    flat = a.reshape(-1)
    pad = (-flat.shape[0]) % (SUBLANES * LANES)
    if pad:
        flat = jnp.pad(flat, (0, pad))
    return flat.reshape(-1, LANES)


def _small_rows(shape):
    return -(-math.prod(shape) // (SUBLANES * LANES)) * SUBLANES


def _pack_small(parts):
    rows = jnp.concatenate([_to2d(p) for p in parts], axis=0)
    pad = (-rows.shape[0]) % 256
    if pad:
        rows = jnp.concatenate([rows, jnp.zeros((pad, LANES), rows.dtype)], axis=0)
    return rows


def _unpack_small(rows, shapes):
    out, r = [], 0
    for shp in shapes:
        size = math.prod(shp)
        nrow = _small_rows(shp)
        out.append(rows[r:r + nrow].reshape(-1)[:size].reshape(shp))
        r += nrow
    return out


def kernel(x, mem, g_ffn1, w1_gate, w1_up, w1_down, g_mix, w_in, g_v, w_s, b_s, sinks, g_a_out, g_b_out, w_out, g_x, g_mem, w_xq, w_xkv, w_xo, g_ffn2, w2_gate, w2_up, w2_down, g_final, loss_target, m_g_ffn1, m_w1_gate, m_w1_up, m_w1_down, m_g_mix, m_w_in, m_g_v, m_w_s, m_b_s, m_sinks, m_g_a_out, m_g_b_out, m_w_out, m_g_x, m_g_mem, m_w_xq, m_w_xkv, m_w_xo, m_g_ffn2, m_w2_gate, m_w2_up, m_w2_down, m_g_final, v_g_ffn1, v_w1_gate, v_w1_up, v_w1_down, v_g_mix, v_w_in, v_g_v, v_w_s, v_b_s, v_sinks, v_g_a_out, v_g_b_out, v_w_out, v_g_x, v_g_mem, v_w_xq, v_w_xkv, v_w_xo, v_g_ffn2, v_w2_gate, v_w2_up, v_w2_down, v_g_final):
    args = dict(locals())
    Wp = {n: args[n] for n in ALL_W}
    Mp = {n: args["m_" + n] for n in ALL_W}
    Vp = {n: args["v_" + n] for n in ALL_W}

    comm = _Comm({n: Wp[n][0].astype(BF16) for n in BIG})
    W = {n: Wp[n] for n in SMALL}
    W["g_final"] = Wp["g_final"].reshape(1, D_MODEL)
    for n in ("w_s", "b_s"):
        W[n] = Wp[n][0]
    loss, dx, grads = _local_step(x[0], mem[0], loss_target[0], W, comm)
    big_grad = comm.total

    small_shapes = [Wp[n].shape for n in SMALL]
    packed = _pack_small([grads[n].reshape(Wp[n].shape) for n in SMALL] + [loss])
    summed = _allreduce_small(packed, name="allreduce_small")
    small_grad = dict(zip(SMALL, _unpack_small(summed, small_shapes)))
    nrows = sum(_small_rows(s) for s in small_shapes)
    loss_total = summed[nrows, 0]

    grad_out, delta, new_m, new_v = {}, {}, {}, {}
    for n in BIG:
        shp = Wp[n].shape
        g, d, nm, nv = _adamw(Wp[n][0], big_grad[n], Mp[n][0], Vp[n][0], name=f"adamw_{n}")
        grad_out[n], delta[n], new_m[n], new_v[n] = g.reshape(shp), d.reshape(shp), nm.reshape(shp), nv.reshape(shp)
    sw = _pack_small([Wp[n] for n in SMALL])
    sg = _pack_small([small_grad[n] for n in SMALL])
    sm = _pack_small([Mp[n] for n in SMALL])
    sv = _pack_small([Vp[n] for n in SMALL])
    _, d, nm, nv = _adamw(sw, sg, sm, sv, name="adamw_small")
    for n, dd, mm_, vv_ in zip(SMALL, _unpack_small(d, small_shapes), _unpack_small(nm, small_shapes),
                               _unpack_small(nv, small_shapes)):
        grad_out[n], delta[n], new_m[n], new_v[n] = small_grad[n], dd, mm_, vv_

    return (loss_total, dx[None], *[grad_out[n] for n in ALL_W], *[delta[n] for n in ALL_W],
            *[new_m[n] for n in ALL_W], *[new_v[n] for n in ALL_W])
```

```python
import functools
import math

import jax
import jax.numpy as jnp
from jax import lax
from jax.experimental import pallas as pl
from jax.experimental.pallas import tpu as pltpu

F32 = jnp.float32
BF16 = jnp.bfloat16
MESH = pl.DeviceIdType.MESH

D_MODEL = 2048
D_FF = 5632
D_A = 1024
D_B = 1024
CHUNK = 128
A_GROUPS = 8
HEAD_DIM = 64
B_Q_HEADS = 16
B_KV_HEADS = 2
X_HEADS = 4
X_HEAD_DIM = 512
IN_COLS = 3328
O_Q = 2 * D_A
O_K = O_Q + D_B
O_V = O_K + B_KV_HEADS * HEAD_DIM
N_CHIPS = 4
EPS = 1e-5
NEG = -1e30
ADAM_LR = 0.001
ADAM_B1 = 0.9
ADAM_B2 = 0.999
ADAM_EPS = 1e-08
ADAM_WD = 0.01
ADAM_STEP = 10

V7X_VMEM_BYTES = 64 * 1024 * 1024
VMEM_LIMIT = 56 * 1024 * 1024
LANES = 128
SUBLANES = 8


ANY = pl.BlockSpec(memory_space=pl.ANY)


def _params(sem, vmem=VMEM_LIMIT):
    return pltpu.CompilerParams(dimension_semantics=sem, vmem_limit_bytes=vmem)


def _matmul(pairs, *, M, N, K, tm, tn, tk, a_t=False, b_kind="n", out_kind="n", out_dtype=F32,
            scale=1.0, res=None, norm_g=None, order="ij", dep=None, name):
    tm, tn, tk = min(tm, M), min(tn, N), min(tk, K)
    assert M % tm == 0 and N % tn == 0 and K % tk == 0, (name, M, N, K, tm, tn, tk)
    nk = K // tk
    npairs = len(pairs)
    b_t = b_kind in ("t", "st")

    def ij(g0, g1):
        return (g0, g1) if order == "ij" else (g1, g0)

    def a_map(g0, g1, k):
        i, _ = ij(g0, g1)
        return (k, i) if a_t else (i, k)

    a_spec = pl.BlockSpec((tk, tm) if a_t else (tm, tk), a_map)

    b0 = pairs[0][1]
    if b_kind == "n":
        b_spec = pl.BlockSpec((tk, tn), lambda g0, g1, k: (k, ij(g0, g1)[1]))
    elif b_kind == "t":
        b_spec = pl.BlockSpec((tn, tk), lambda g0, g1, k: (ij(g0, g1)[1], k))
    elif b_kind == "sn":
        ns = b0.shape[2]
        assert ns % tn == 0
        nps = ns // tn
        b_spec = pl.BlockSpec((None, tk, tn), lambda g0, g1, k: (ij(g0, g1)[1] // nps, k, ij(g0, g1)[1] % nps))
    else:
        ks = b0.shape[2]
        assert ks % tk == 0
        kps = ks // tk
        b_spec = pl.BlockSpec((None, tn, tk), lambda g0, g1, k: (k // kps, ij(g0, g1)[1], k % kps))

    if out_kind == "n":
        o_spec = pl.BlockSpec((tm, tn), lambda g0, g1, k: ij(g0, g1))
        o_shape = jax.ShapeDtypeStruct((M, N), out_dtype)
    else:
        ns = N // N_CHIPS
        assert ns % tn == 0
        nps_o = ns // tn
        o_spec = pl.BlockSpec((None, tm, tn), lambda g0, g1, k: (ij(g0, g1)[1] // nps_o, ij(g0, g1)[0], ij(g0, g1)[1] % nps_o))
        o_shape = jax.ShapeDtypeStruct((N_CHIPS, M, ns), out_dtype)

    in_specs, args = [], []
    for a, b in pairs:
        in_specs += [a_spec, b_spec]
        args += [a, b]
    if res is not None:
        in_specs.append(pl.BlockSpec((tm, tn), lambda g0, g1, k: ij(g0, g1)))
        args.append(res)
    if norm_g is not None:
        assert tn == N and out_kind == "n"
        in_specs.append(pl.BlockSpec((1, N), lambda g0, g1, k: (0, 0)))
        args.append(norm_g)
    if dep is not None:
        in_specs.append(ANY)
        args.append(dep)

    dn = (((0,) if a_t else (1,), (1,) if b_t else (0,)), ((), ()))

    def body(*refs):
        pos = 2 * npairs
        res_ref = refs[pos] if res is not None else None
        pos += res is not None
        g_ref = refs[pos] if norm_g is not None else None
        pos += (norm_g is not None) + (dep is not None)
        o_ref = refs[pos]
        n_ref = refs[pos + 1] if norm_g is not None else None
        acc_ref = refs[-1] if nk > 1 else None
        part = None
        for p in range(npairs):
            d = lax.dot_general(refs[2 * p][...], refs[2 * p + 1][...], dn, preferred_element_type=F32)
            part = d if part is None else part + d

        def finish(acc):
            r = acc * scale if scale != 1.0 else acc
            if res_ref is not None:
                r = res_ref[...] + r
            o_ref[...] = r.astype(out_dtype)
            if n_ref is not None:
                n_ref[...] = (r * _rstd(r) * g_ref[...]).astype(BF16)

        if nk == 1:
            finish(part)
        else:
            k = pl.program_id(2)

            @pl.when(k == 0)
            def _():
                acc_ref[...] = part

            @pl.when((k > 0) & (k < nk - 1))
            def _():
                acc_ref[...] += part

            @pl.when(k == nk - 1)
            def _():
                finish(acc_ref[...] + part)

    grid = (M // tm, N // tn, nk) if order == "ij" else (N // tn, M // tm, nk)
    out_specs, out_shape = o_spec, o_shape
    if norm_g is not None:
        out_specs = [o_spec, pl.BlockSpec((tm, tn), lambda g0, g1, k: ij(g0, g1))]
        out_shape = [o_shape, jax.ShapeDtypeStruct((M, N), BF16)]
    return pl.pallas_call(
        body, name=name, grid=grid, in_specs=in_specs, out_specs=out_specs, out_shape=out_shape,
        scratch_shapes=[pltpu.VMEM((tm, tn), F32)] if nk > 1 else [],
        compiler_params=_params(("parallel", "parallel", "arbitrary")),
    )(*args)


def _rstd(x):
    return lax.rsqrt(jnp.mean(x * x, axis=-1, keepdims=True) + EPS)


def _rms_bwd_math(x, g, dy):
    r = _rstd(x)
    gy = dy * g
    xr = x * r
    dx = r * (gy - xr * jnp.mean(gy * xr, axis=-1, keepdims=True))
    return dx, dy * xr


def _rms_fwd(h, g, *, name, tm=512, dep=None):
    T, Dm = h.shape
    tm = min(tm, T)

    def body(h_ref, g_ref, *rest):
        x = h_ref[...]
        rest[-1][...] = (x * _rstd(x) * g_ref[...]).astype(BF16)

    return pl.pallas_call(
        body, name=name, grid=(T // tm,),
        in_specs=[pl.BlockSpec((tm, Dm), lambda i: (i, 0)), pl.BlockSpec((1, Dm), lambda i: (0, 0))]
        + ([ANY] if dep is not None else []),
        out_specs=pl.BlockSpec((tm, Dm), lambda i: (i, 0)),
        out_shape=jax.ShapeDtypeStruct((T, Dm), BF16),
        compiler_params=_params(("parallel",)),
    )(h, g, *([dep] if dep is not None else []))


def _rms_bwd(h, g, dn, dres, *, name, tm=256, dep=None):
    T, Dm = h.shape
    tm = min(tm, T)
    has_res = dres is not None

    def body(*refs):
        h_ref, g_ref, dn_ref = refs[:3]
        pos = 3
        dres_ref = refs[pos] if has_res else None
        pos += has_res + (dep is not None)
        dh_ref, dhb_ref, dg_ref = refs[pos:pos + 3]
        dx, dgr = _rms_bwd_math(h_ref[...], g_ref[...], dn_ref[...].astype(F32))
        if has_res:
            dx = dres_ref[...] + dx
        dh_ref[...] = dx
        dhb_ref[...] = dx.astype(BF16)
        part = jnp.sum(dgr, axis=0, keepdims=True)

        @pl.when(pl.program_id(0) == 0)
        def _():
            dg_ref[...] = part

        @pl.when(pl.program_id(0) > 0)
        def _():
            dg_ref[...] += part

    row = pl.BlockSpec((tm, Dm), lambda i: (i, 0))
    vec = pl.BlockSpec((1, Dm), lambda i: (0, 0))
    args = [h, g, dn] + ([dres] if has_res else []) + ([dep] if dep is not None else [])
    return pl.pallas_call(
        body, name=name, grid=(T // tm,),
        in_specs=[row, vec, row] + ([row] if has_res else []) + ([ANY] if dep is not None else []),
        out_specs=[row, row, vec],
        out_shape=[jax.ShapeDtypeStruct((T, Dm), F32), jax.ShapeDtypeStruct((T, Dm), BF16),
                   jax.ShapeDtypeStruct((1, Dm), F32)],
        compiler_params=_params(("arbitrary",)),
    )(*args)


def _loss_head(h, g, tgt, *, name, tm=256):
    T, Dm = h.shape
    tm = min(tm, T)

    def body(h_ref, g_ref, t_ref, dh_ref, dhb_ref, dg_ref, loss_ref):
        x = h_ref[...]
        gv = g_ref[...]
        r = _rstd(x)
        diff = x * r * gv - t_ref[...]
        lpart = 0.5 * jnp.sum(jnp.mean(diff * diff, axis=-1, keepdims=True), axis=0, keepdims=True)
        dx, dgr = _rms_bwd_math(x, gv, diff * (1.0 / Dm))
        dh_ref[...] = dx
        dhb_ref[...] = dx.astype(BF16)
        part = jnp.sum(dgr, axis=0, keepdims=True)
        lrow = jnp.broadcast_to(lpart, (1, LANES))

        @pl.when(pl.program_id(0) == 0)
        def _():
            dg_ref[...] = part
            loss_ref[...] = lrow

        @pl.when(pl.program_id(0) > 0)
        def _():
            dg_ref[...] += part
            loss_ref[...] += lrow

    row = pl.BlockSpec((tm, Dm), lambda i: (i, 0))
    vec = pl.BlockSpec((1, Dm), lambda i: (0, 0))
    return pl.pallas_call(
        body, name=name, grid=(T // tm,),
        in_specs=[row, vec, row],
        out_specs=[row, row, vec, pl.BlockSpec((1, LANES), lambda i: (0, 0))],
        out_shape=[jax.ShapeDtypeStruct((T, Dm), F32), jax.ShapeDtypeStruct((T, Dm), BF16),
                   jax.ShapeDtypeStruct((1, Dm), F32), jax.ShapeDtypeStruct((1, LANES), F32)],
        compiler_params=_params(("arbitrary",)),
    )(h, g, tgt)


MXU_COLS = 256
CHUNK_COLS = 2 * MXU_COLS


def _col_chunks(width):
    return [(c0, min(CHUNK_COLS, width - c0)) for c0 in range(0, width, CHUNK_COLS)]


def _row_block(rows, row_bytes, align, budget=24 * 1024 * 1024):
    fits = [d for d in range(align, rows + 1, align) if rows % d == 0 and 2 * d * row_bytes <= budget]
    assert fits, (rows, row_bytes)
    return fits[-1]


def _swiglu_up(n, wg, wu, *, name, tm=512):
    T, Dm = n.shape
    S, _, fs = wg.shape
    tm = min(tm, T)

    def body(n_ref, wg_ref, wu_ref, g_ref, u_ref, a_ref):
        x = n_ref[...]
        for c0, cw in _col_chunks(fs):
            cols = slice(c0, c0 + cw)
            g = jnp.dot(x, wg_ref[:, cols], preferred_element_type=F32)
            u = jnp.dot(x, wu_ref[:, cols], preferred_element_type=F32)
            g_ref[:, cols] = g.astype(BF16)
            u_ref[:, cols] = u.astype(BF16)
            a_ref[:, cols] = (g * jax.nn.sigmoid(g) * u).astype(BF16)

    wspec = pl.BlockSpec((None, Dm, fs), lambda j, i: (j, 0, 0))
    ospec = pl.BlockSpec((tm, fs), lambda j, i: (i, j))
    oshape = jax.ShapeDtypeStruct((T, S * fs), BF16)
    return pl.pallas_call(
        body, name=name, grid=(S, T // tm),
        in_specs=[pl.BlockSpec((tm, Dm), lambda j, i: (i, 0)), wspec, wspec],
        out_specs=[ospec, ospec, ospec], out_shape=[oshape, oshape, oshape],
        compiler_params=_params(("parallel", "parallel")),
    )(n, wg, wu)


def _swiglu_bwd_act(dhb, wd, G, U, *, name, tm=512, tn=1408):
    T, Dm = dhb.shape
    Fd = wd.shape[0]
    tm, tn = min(tm, T), min(tn, Fd)

    def body(dh_ref, wd_ref, g_ref, u_ref, dg_ref, du_ref):
        dh = dh_ref[...]
        for c0, cw in _col_chunks(tn):
            cols = slice(c0, c0 + cw)
            da = 0.5 * lax.dot_general(dh, wd_ref[cols, :], (((1,), (1,)), ((), ())), preferred_element_type=F32)
            g = g_ref[:, cols].astype(F32)
            u = u_ref[:, cols].astype(F32)
            sg = jax.nn.sigmoid(g)
            dg_ref[:, cols] = (da * u * (sg * (1.0 + g * (1.0 - sg)))).astype(BF16)
            du_ref[:, cols] = (da * (g * sg)).astype(BF16)

    blk = pl.BlockSpec((tm, tn), lambda j, i: (i, j))
    oshape = jax.ShapeDtypeStruct((T, Fd), BF16)
    return pl.pallas_call(
        body, name=name, grid=(Fd // tn, T // tm),
        in_specs=[pl.BlockSpec((tm, Dm), lambda j, i: (i, 0)), pl.BlockSpec((tn, Dm), lambda j, i: (j, 0)), blk, blk],
        out_specs=[blk, blk], out_shape=[oshape, oshape],
        compiler_params=_params(("parallel", "parallel")),
    )(dhb, wd, G, U)


_INV_SQRT2 = 0.7071067811865476
_INV_SQRT2PI = 0.3989422804014327


def _erf(x):
    ax = jnp.abs(x)
    t = 1.0 / (1.0 + 0.3275911 * ax)
    poly = t * (0.254829592 + t * (-0.284496736 + t * (1.421413741 + t * (-1.453152027 + t * 1.061405429))))
    y = 1.0 - poly * jnp.exp(-ax * ax)
    return jnp.where(x < 0, -y, y)


def _gelu_cdf(x):
    return 0.5 * (1.0 + _erf(x * _INV_SQRT2))


def _lane_lt64(shape):
    return lax.broadcasted_iota(jnp.int32, shape, len(shape) - 1) < HEAD_DIM


def _dup_half(x, kv):
    rolled = pltpu.roll(x, HEAD_DIM, 1)
    lo = _lane_lt64(x.shape)
    return jnp.where(lo, x, rolled) if kv == 0 else jnp.where(lo, rolled, x)


HEADS_PER_KV = B_Q_HEADS // B_KV_HEADS
PAIRS = HEADS_PER_KV // 2


def _attn_bias():
    shape = (HEADS_PER_KV * CHUNK, 2 * CHUNK)
    qpos = (lax.broadcasted_iota(jnp.int32, shape, 0) & (CHUNK - 1)) + CHUNK
    kpos = lax.broadcasted_iota(jnp.int32, shape, 1)
    diff = qpos - kpos
    band = (diff >= 0) & (diff < CHUNK)
    return jnp.stack([jnp.where(band & (kpos >= CHUNK), 0.0, NEG), jnp.where(band, 0.0, NEG)]).astype(F32)


def _stack_heads(tiles, lo):
    parts = []
    for t in tiles:
        parts += [jnp.where(lo, t, 0.0), jnp.where(lo, 0.0, t)]
    return jnp.concatenate(parts, axis=0)


def _unstack_heads(s, lo):
    return [jnp.where(lo, s[2 * p * CHUNK:(2 * p + 1) * CHUNK], s[(2 * p + 1) * CHUNK:(2 * p + 2) * CHUNK])
            for p in range(PAIRS)]


def _stack_sinks(sk_ref, kv):
    return jnp.concatenate([jnp.broadcast_to(sk_ref[:, h:h + 1], (CHUNK, 1))
                            for h in range(kv * HEADS_PER_KV, (kv + 1) * HEADS_PER_KV)], axis=0)


def _sgu_forward(z_ref, gv, wsm, bst):
    zu = z_ref[:, 0:D_A]
    zv = z_ref[:, D_A:2 * D_A]
    u = zu * _gelu_cdf(zu)
    v = zv * _gelu_cdf(zv)
    rv = _rstd(v)
    vn = (v * rv * gv).astype(BF16)
    svs = []
    for g in range(A_GROUPS):
        sl = slice(g * CHUNK, (g + 1) * CHUNK)
        svs.append(jnp.dot(wsm[g], vn[:, sl], preferred_element_type=F32) + bst[:, g:g + 1])
    sv = jnp.concatenate(svs, axis=1)
    return zu, zv, u, v, rv, vn, sv


def _masked_ws(ws_ref):
    tril = lax.broadcasted_iota(jnp.int32, (CHUNK, CHUNK), 0) >= lax.broadcasted_iota(jnp.int32, (CHUNK, CHUNK), 1)
    return [jnp.where(tril, ws_ref[g], 0.0).astype(BF16) for g in range(A_GROUPS)], tril


def _attn_probs(qm, kkd, sink, bias):
    s = lax.dot_general(qm, kkd, (((1,), (1,)), ((), ())), preferred_element_type=F32) * (HEAD_DIM ** -0.5) + bias
    m = jnp.maximum(jnp.max(s, axis=-1, keepdims=True), sink)
    e = jnp.exp(s - m)
    es = jnp.exp(sink - m)
    inv = 1.0 / (jnp.sum(e, axis=-1, keepdims=True) + es)
    return e * inv, es * inv


def _mixer_fwd(z, gv, ws, bst, sinks, ga, gb, *, name):
    T = z.shape[0]
    nb = T // CHUNK
    kvb = O_K // (2 * CHUNK)

    def body(z_ref, zp_ref, bias_ref, gv_ref, ws_ref, bst_ref, sk_ref, ga_ref, gb_ref, o_ref):
        wsm, _ = _masked_ws(ws_ref)
        _, _, u, _, _, _, sv = _sgu_forward(z_ref, gv_ref[...], wsm, bst_ref[...])
        ya = u * sv
        o_ref[:, 0:D_A] = (ya * _rstd(ya) * ga_ref[...]).astype(BF16)

        mask = bias_ref[...]
        kk = jnp.concatenate([zp_ref[:, 0:CHUNK], z_ref[:, O_K:O_V]], axis=0)
        vv = jnp.concatenate([zp_ref[:, CHUNK:2 * CHUNK], z_ref[:, O_V:IN_COLS]], axis=0)
        lo = _lane_lt64((CHUNK, LANES))
        outs = []
        for kv in range(B_KV_HEADS):
            kkd = _dup_half(kk, kv).astype(BF16)
            vvd = _dup_half(vv, kv).astype(BF16)
            q = _stack_heads([z_ref[:, O_Q + (kv * PAIRS + pr) * LANES:O_Q + (kv * PAIRS + pr + 1) * LANES]
                              for pr in range(PAIRS)], lo).astype(BF16)
            p, _ = _attn_probs(q, kkd, _stack_sinks(sk_ref, kv), mask)
            outs += _unstack_heads(jnp.dot(p.astype(BF16), vvd, preferred_element_type=F32), lo)
        yb = jnp.concatenate(outs, axis=1)
        o_ref[:, D_A:D_A + D_B] = (yb * _rstd(yb) * gb_ref[...]).astype(BF16)

    full = lambda shape: pl.BlockSpec(shape, lambda i: (0,) * len(shape))
    return pl.pallas_call(
        body, name=name, grid=(nb,),
        in_specs=[pl.BlockSpec((CHUNK, IN_COLS), lambda i: (i, 0)),
                  pl.BlockSpec((CHUNK, 2 * CHUNK), lambda i: (jnp.maximum(i - 1, 0), kvb)),
                  pl.BlockSpec((None, HEADS_PER_KV * CHUNK, 2 * CHUNK), lambda i: (jnp.minimum(i, 1), 0, 0)),
                  full((1, D_A)), full((A_GROUPS, CHUNK, CHUNK)), full((CHUNK, A_GROUPS)), full((1, B_Q_HEADS)),
                  full((1, D_A)), full((1, D_B))],
        out_specs=pl.BlockSpec((CHUNK, D_A + D_B), lambda i: (i, 0)),
        out_shape=jax.ShapeDtypeStruct((T, D_A + D_B), BF16),
        compiler_params=_params(("parallel",)),
    )(z, z, _attn_bias(), gv, ws, bst, sinks, ga, gb)


def _mixer_bwd(z, dyn, gv, ws, bst, sinks, ga, gb, *, name):
    T = z.shape[0]
    nb = T // CHUNK
    kvb = O_K // (2 * CHUNK)
    NT = (((0,), (0,)), ((), ()))

    def body(z_ref, zp_ref, dy_ref, bias_ref, gv_ref, ws_ref, bst_ref, sk_ref, ga_ref, gb_ref,
             dz_ref, dgv_ref, dws_ref, dbst_ref, dsk_ref, dga_ref, dgb_ref, carry_ref, p_ref):
        step = pl.program_id(0)

        @pl.when(step == 0)
        def _():
            carry_ref[...] = jnp.zeros_like(carry_ref)
            dgv_ref[...] = jnp.zeros_like(dgv_ref)
            dws_ref[...] = jnp.zeros_like(dws_ref)
            dbst_ref[...] = jnp.zeros_like(dbst_ref)
            dsk_ref[...] = jnp.zeros_like(dsk_ref)
            dga_ref[...] = jnp.zeros_like(dga_ref)
            dgb_ref[...] = jnp.zeros_like(dgb_ref)

        wsm, tril = _masked_ws(ws_ref)
        gvv = gv_ref[...]
        zu, zv, u, v, rv, vn, sv = _sgu_forward(z_ref, gvv, wsm, bst_ref[...])
        ya = u * sv
        dya, dga_rows = _rms_bwd_math(ya, ga_ref[...], dy_ref[:, 0:D_A].astype(F32))
        dga_ref[...] += jnp.sum(dga_rows, axis=0, keepdims=True)
        du = dya * sv
        dsv = dya * u
        dvn_parts = []
        for g in range(A_GROUPS):
            sl = slice(g * CHUNK, (g + 1) * CHUNK)
            dsv_g = dsv[:, sl]
            dsv_gb = dsv_g.astype(BF16)
            dw = lax.dot_general(dsv_gb, vn[:, sl], (((1,), (1,)), ((), ())), preferred_element_type=F32)
            dws_ref[g] += jnp.where(tril, dw, 0.0)
            dbst_ref[:, g:g + 1] += jnp.sum(dsv_g, axis=1, keepdims=True)
            dvn_parts.append(lax.dot_general(wsm[g], dsv_gb, NT, preferred_element_type=F32))
        dvn = jnp.concatenate(dvn_parts, axis=1)
        dv, dgv_rows = _rms_bwd_math(v, gvv, dvn)
        dgv_ref[...] += jnp.sum(dgv_rows, axis=0, keepdims=True)
        dz_ref[:, 0:D_A] = (du * (_gelu_cdf(zu) + zu * jnp.exp(-0.5 * zu * zu) * _INV_SQRT2PI)).astype(BF16)
        dz_ref[:, D_A:2 * D_A] = (dv * (_gelu_cdf(zv) + zv * jnp.exp(-0.5 * zv * zv) * _INV_SQRT2PI)).astype(BF16)

        mask = bias_ref[...]
        kk = jnp.concatenate([zp_ref[:, 0:CHUNK], z_ref[:, O_K:O_V]], axis=0)
        vv = jnp.concatenate([zp_ref[:, CHUNK:2 * CHUNK], z_ref[:, O_V:IN_COLS]], axis=0)
        lo = _lane_lt64((CHUNK, LANES))
        kkd = [_dup_half(kk, kv).astype(BF16) for kv in range(B_KV_HEADS)]
        vvd = [_dup_half(vv, kv).astype(BF16) for kv in range(B_KV_HEADS)]
        outs, qs, psinks = [], [], []
        for kv in range(B_KV_HEADS):
            qs.append(_stack_heads([z_ref[:, O_Q + (kv * PAIRS + pr) * LANES:O_Q + (kv * PAIRS + pr + 1) * LANES]
                                    for pr in range(PAIRS)], lo).astype(BF16))
            p, ps = _attn_probs(qs[kv], kkd[kv], _stack_sinks(sk_ref, kv), mask)
            p_ref[kv] = p
            psinks.append(ps)
            outs += _unstack_heads(jnp.dot(p.astype(BF16), vvd[kv], preferred_element_type=F32), lo)
        yb = jnp.concatenate(outs, axis=1)
        dyb, dgb_rows = _rms_bwd_math(yb, gb_ref[...], dy_ref[:, D_A:D_A + D_B].astype(F32))
        dgb_ref[...] += jnp.sum(dgb_rows, axis=0, keepdims=True)

        dkk, dvv = [], []
        for kv in range(B_KV_HEADS):
            do = _stack_heads([dyb[:, (kv * PAIRS + pr) * LANES:(kv * PAIRS + pr + 1) * LANES]
                               for pr in range(PAIRS)], lo).astype(BF16)
            p = p_ref[kv]
            dvv.append(lax.dot_general(p.astype(BF16), do, NT, preferred_element_type=F32))
            dp = lax.dot_general(do, vvd[kv], (((1,), (1,)), ((), ())), preferred_element_type=F32)
            delta = jnp.sum(p * dp, axis=-1, keepdims=True)
            dsink = -psinks[kv] * delta
            for g in range(HEADS_PER_KV):
                h = kv * HEADS_PER_KV + g
                dsk_ref[:, h:h + 1] += jnp.sum(dsink[g * CHUNK:(g + 1) * CHUNK], axis=0, keepdims=True)
            ds = (p * (dp - delta) * (HEAD_DIM ** -0.5)).astype(BF16)
            dq = _unstack_heads(jnp.dot(ds, kkd[kv], preferred_element_type=F32), lo)
            for pr in range(PAIRS):
                c0 = O_Q + (kv * PAIRS + pr) * LANES
                dz_ref[:, c0:c0 + LANES] = dq[pr].astype(BF16)
            dkk.append(lax.dot_general(ds, qs[kv], NT, preferred_element_type=F32))

        def fold(parts):
            tot = [t + pltpu.roll(t, HEAD_DIM, 1) for t in parts]
            return jnp.where(_lane_lt64(tot[0].shape), tot[0], tot[1])

        dk_all = fold(dkk)
        dv_all = fold(dvv)
        dz_ref[:, O_K:O_V] = (dk_all[CHUNK:] + carry_ref[:, 0:CHUNK]).astype(BF16)
        dz_ref[:, O_V:IN_COLS] = (dv_all[CHUNK:] + carry_ref[:, CHUNK:2 * CHUNK]).astype(BF16)
        carry_ref[:, 0:CHUNK] = dk_all[:CHUNK]
        carry_ref[:, CHUNK:2 * CHUNK] = dv_all[:CHUNK]

    full = lambda shape: pl.BlockSpec(shape, lambda s: (0,) * len(shape))
    rev = lambda s: nb - 1 - s
    return pl.pallas_call(
        body, name=name, grid=(nb,),
        in_specs=[pl.BlockSpec((CHUNK, IN_COLS), lambda s: (rev(s), 0)),
                  pl.BlockSpec((CHUNK, 2 * CHUNK), lambda s: (jnp.maximum(rev(s) - 1, 0), kvb)),
                  pl.BlockSpec((CHUNK, D_A + D_B), lambda s: (rev(s), 0)),
                  pl.BlockSpec((None, HEADS_PER_KV * CHUNK, 2 * CHUNK), lambda s: (jnp.minimum(rev(s), 1), 0, 0)),
                  full((1, D_A)), full((A_GROUPS, CHUNK, CHUNK)), full((CHUNK, A_GROUPS)), full((1, B_Q_HEADS)),
                  full((1, D_A)), full((1, D_B))],
        out_specs=[pl.BlockSpec((CHUNK, IN_COLS), lambda s: (rev(s), 0)),
                   full((1, D_A)), full((A_GROUPS, CHUNK, CHUNK)), full((CHUNK, A_GROUPS)), full((1, B_Q_HEADS)),
                   full((1, D_A)), full((1, D_B))],
        out_shape=[jax.ShapeDtypeStruct((T, IN_COLS), BF16), jax.ShapeDtypeStruct((1, D_A), F32),
                   jax.ShapeDtypeStruct((A_GROUPS, CHUNK, CHUNK), F32), jax.ShapeDtypeStruct((CHUNK, A_GROUPS), F32),
                   jax.ShapeDtypeStruct((1, B_Q_HEADS), F32), jax.ShapeDtypeStruct((1, D_A), F32),
                   jax.ShapeDtypeStruct((1, D_B), F32)],
        scratch_shapes=[pltpu.VMEM((CHUNK, 2 * CHUNK), F32), pltpu.VMEM((B_KV_HEADS, HEADS_PER_KV * CHUNK, 2 * CHUNK), F32)],
        compiler_params=_params(("arbitrary",)),
    )(z, z, dyn, _attn_bias(), gv, ws, bst, sinks, ga, gb)


def _xattn_probs(qh, kh):
    s = lax.dot_general(qh, kh, (((1,), (1,)), ((), ())), preferred_element_type=F32) * (X_HEAD_DIM ** -0.5)
    e = jnp.exp(s - jnp.max(s, axis=-1, keepdims=True))
    return e / jnp.sum(e, axis=-1, keepdims=True)


def _xattn_fwd(q, kvm, *, name, tm=512):
    T = q.shape[0]
    Mm = kvm.shape[0]
    tm = min(tm, T)

    def body(q_ref, kv_ref, o_ref):
        for h in range(X_HEADS):
            sl = slice(h * X_HEAD_DIM, (h + 1) * X_HEAD_DIM)
            kh = kv_ref[:, sl].astype(BF16)
            vh = kv_ref[:, D_MODEL + h * X_HEAD_DIM:D_MODEL + (h + 1) * X_HEAD_DIM].astype(BF16)
            p = _xattn_probs(q_ref[:, sl], kh)
            o_ref[:, sl] = jnp.dot(p.astype(BF16), vh, preferred_element_type=F32).astype(BF16)

    return pl.pallas_call(
        body, name=name, grid=(T // tm,),
        in_specs=[pl.BlockSpec((tm, D_MODEL), lambda i: (i, 0)), pl.BlockSpec((Mm, 2 * D_MODEL), lambda i: (0, 0))],
        out_specs=pl.BlockSpec((tm, D_MODEL), lambda i: (i, 0)),
        out_shape=jax.ShapeDtypeStruct((T, D_MODEL), BF16),
        compiler_params=_params(("parallel",)),
    )(q, kvm)


def _xattn_bwd(q, kvm, do, *, name, tm=512):
    T = q.shape[0]
    Mm = kvm.shape[0]
    tm = min(tm, T)
    NT = (((0,), (0,)), ((), ()))

    def body(q_ref, kv_ref, do_ref, dq_ref, dkv_ref):
        @pl.when(pl.program_id(0) == 0)
        def _():
            dkv_ref[...] = jnp.zeros_like(dkv_ref)

        for h in range(X_HEADS):
            sl = slice(h * X_HEAD_DIM, (h + 1) * X_HEAD_DIM)
            slv = slice(D_MODEL + h * X_HEAD_DIM, D_MODEL + (h + 1) * X_HEAD_DIM)
            kh = kv_ref[:, sl].astype(BF16)
            vh = kv_ref[:, slv].astype(BF16)
            qh = q_ref[:, sl]
            doh = do_ref[:, sl]
            p = _xattn_probs(qh, kh)
            dkv_ref[:, slv] += lax.dot_general(p.astype(BF16), doh, NT, preferred_element_type=F32)
            dp = lax.dot_general(doh, vh, (((1,), (1,)), ((), ())), preferred_element_type=F32)
            ds = (p * (dp - jnp.sum(p * dp, axis=-1, keepdims=True)) * (X_HEAD_DIM ** -0.5)).astype(BF16)
            dq_ref[:, sl] = jnp.dot(ds, kh, preferred_element_type=F32).astype(BF16)
            dkv_ref[:, sl] += lax.dot_general(ds, qh, NT, preferred_element_type=F32)

    row = pl.BlockSpec((tm, D_MODEL), lambda i: (i, 0))
    kvs = pl.BlockSpec((Mm, 2 * D_MODEL), lambda i: (0, 0))
    return pl.pallas_call(
        body, name=name, grid=(T // tm,),
        in_specs=[row, kvs, row], out_specs=[row, kvs],
        out_shape=[jax.ShapeDtypeStruct((T, D_MODEL), BF16), jax.ShapeDtypeStruct((Mm, 2 * D_MODEL), F32)],
        compiler_params=_params(("arbitrary",)),
    )(q, kvm, do)


def _swiglu_bwd_weights(tag, n, G, U, A, wd, dhb):
    T = n.shape[0]
    dG, dU = _swiglu_bwd_act(dhb, wd, G, U, name=f"{tag}_bwd_act", tm=1024)
    dwd = _matmul([(A, dhb)], M=D_FF, N=D_MODEL, K=T, tm=1408, tn=1024, tk=2048, a_t=True, out_dtype=BF16,
                  scale=0.5, name=f"{tag}_dwd")
    dwg = _matmul([(n, dG)], M=D_MODEL, N=D_FF, K=T, tm=1024, tn=1408, tk=2048, a_t=True, out_kind="s",
                  out_dtype=BF16, order="ji", name=f"{tag}_dwg")
    dwu = _matmul([(n, dU)], M=D_MODEL, N=D_FF, K=T, tm=1024, tn=1408, tk=2048, a_t=True, out_kind="s",
                  out_dtype=BF16, order="ji", name=f"{tag}_dwu")
    return dG, dU, dwg, dwu, dwd


def _swiglu_bwd_input(tag, hin, g_norm, dG, dU, wg, wu, dh, dep):
    T = hin.shape[0]
    dn = _matmul([(dG, wg), (dU, wu)], M=T, N=D_MODEL, K=D_FF, tm=512, tn=D_MODEL, tk=1408, b_kind="st",
                 out_dtype=BF16, dep=dep, name=f"{tag}_dn")
    return _rms_bwd(hin, g_norm, dn, dh, name=f"{tag}_norm_bwd")


GROUP_FFN1 = ["w1_gate", "w1_up", "w1_down"]
GROUP_MID = ["w_in", "w_out", "w_xq", "w_xkv", "w_xo"]
GROUP_FFN2 = ["w2_gate", "w2_up", "w2_down"]


def _local_step(x, mem, tgt, W, comm):
    T = x.shape[0]
    Mm = mem.shape[0]
    mm = functools.partial(_matmul)

    W = {**W, **comm.gather_now("ffn1_up", ["w1_gate", "w1_up"])}
    tok = comm.gather_start("ffn1_down", ["w1_down"], after=W["w1_up"])
    tok = comm.gather_start("mid", GROUP_MID, after=tok)
    tok = comm.gather_start("ffn2", GROUP_FFN2, after=tok)
    n1 = _rms_fwd(x, W["g_ffn1"], dep=tok, name="f_norm1")
    G1, U1, A1 = _swiglu_up(n1, W["w1_gate"], W["w1_up"], name="f_ffn1_up")
    tok = comm.gather_landed("ffn1_down", after=A1)
    tok = comm.gather_landed("mid", after=tok)
    W = {**W, **comm.gather_ready("ffn1_down", after=tok)}
    h1, n2 = mm([(A1, W["w1_down"])], M=T, N=D_MODEL, K=D_FF, tm=512, tn=D_MODEL, tk=1408, scale=0.5, res=x,
                norm_g=W["g_mix"], name="f_ffn1_down")
    W = {**W, **comm.gather_ready("mid", after=n2)}
    z = mm([(n2, W["w_in"])], M=T, N=IN_COLS, K=D_MODEL, tm=512, tn=IN_COLS // 2, tk=D_MODEL, name="f_w_in")
    bst = jnp.transpose(W["b_s"])
    yn = _mixer_fwd(z, W["g_v"], W["w_s"], bst, W["sinks"], W["g_a_out"], W["g_b_out"], name="f_mixer")
    tok = comm.gather_landed("ffn2", after=yn)
    h2, n3 = mm([(yn, W["w_out"])], M=T, N=D_MODEL, K=D_MODEL, tm=512, tn=D_MODEL, tk=D_MODEL, res=h1,
                norm_g=W["g_x"], dep=tok, name="f_w_out")
    memn = _rms_fwd(mem, W["g_mem"], name="f_norm_mem")
    q3 = mm([(n3, W["w_xq"])], M=T, N=D_MODEL, K=D_MODEL, tm=1024, tn=D_MODEL, tk=D_MODEL, out_dtype=BF16,
            name="f_w_xq")
    kvm = mm([(memn, W["w_xkv"])], M=Mm, N=2 * D_MODEL, K=D_MODEL, tm=Mm, tn=1024, tk=D_MODEL, b_kind="sn",
             name="f_w_xkv")
    o3 = _xattn_fwd(q3, kvm, name="f_xattn")
    h3, n4 = mm([(o3, W["w_xo"])], M=T, N=D_MODEL, K=D_MODEL, tm=512, tn=D_MODEL, tk=D_MODEL, res=h2,
                norm_g=W["g_ffn2"], name="f_w_xo")
    W = {**W, **comm.gather_ready("ffn2", after=n4)}
    G2, U2, A2 = _swiglu_up(n4, W["w2_gate"], W["w2_up"], name="f_ffn2_up")
    h4 = mm([(A2, W["w2_down"])], M=T, N=D_MODEL, K=D_FF, tm=512, tn=D_MODEL, tk=1408, scale=0.5, res=h3,
            name="f_ffn2_down")

    grads = {}
    dh4, dh4b, grads["g_final"], loss = _loss_head(h4, W["g_final"], tgt, name="loss_head")
    dG2, dU2, dwg, dwu, dwd = _swiglu_bwd_weights("b_ffn2", n4, G2, U2, A2, W["w2_down"], dh4b)
    tok = comm.reduce_pair_start("ffn2", {"w2_gate": dwg, "w2_up": dwu, "w2_down": dwd})
    dh3, dh3b, grads["g_ffn2"] = _swiglu_bwd_input("b_ffn2", h3, W["g_ffn2"], dG2, dU2, W["w2_gate"], W["w2_up"],
                                                   dh4, tok)
    tok = comm.reduce_pair_done("ffn2", after=dh3b)

    mid = {}
    do3 = mm([(dh3b, W["w_xo"])], M=T, N=D_MODEL, K=D_MODEL, tm=512, tn=D_MODEL, tk=D_MODEL, b_kind="t",
             out_dtype=BF16, dep=tok, name="b_do3")
    mid["w_xo"] = mm([(o3, dh3b)], M=D_MODEL, N=D_MODEL, K=T, tm=1024, tn=D_MODEL, tk=1024, a_t=True,
                       out_dtype=BF16, name="b_dw_xo")
    dq3, dkvm = _xattn_bwd(q3, kvm, do3, name="b_xattn")
    mid["w_xq"] = mm([(n3, dq3)], M=D_MODEL, N=D_MODEL, K=T, tm=1024, tn=D_MODEL, tk=1024, a_t=True,
                       out_dtype=BF16, name="b_dw_xq")
    dn3 = mm([(dq3, W["w_xq"])], M=T, N=D_MODEL, K=D_MODEL, tm=512, tn=D_MODEL, tk=D_MODEL, b_kind="t",
             out_dtype=BF16, name="b_dn3")
    dh2, dh2b, grads["g_x"] = _rms_bwd(h2, W["g_x"], dn3, dh3, name="b_norm3")
    dkvmb = dkvm.astype(BF16)
    mid["w_xkv"] = mm([(memn, dkvmb)], M=D_MODEL, N=2 * D_MODEL, K=Mm, tm=D_MODEL, tn=1024, tk=Mm, a_t=True,
                        out_kind="s", out_dtype=BF16, name="b_dw_xkv")
    dmemn = mm([(dkvmb, W["w_xkv"])], M=Mm, N=D_MODEL, K=2 * D_MODEL, tm=Mm, tn=D_MODEL, tk=1024, b_kind="st",
               name="b_dmemn")
    _, _, grads["g_mem"] = _rms_bwd(mem, W["g_mem"], dmemn, None, name="b_norm_mem")
    comm.reduce_finish("ffn2", after=dh2b)

    dyn = mm([(dh2b, W["w_out"])], M=T, N=D_MODEL, K=D_MODEL, tm=1024, tn=D_MODEL, tk=D_MODEL, b_kind="t",
             out_dtype=BF16, name="b_dyn")
    mid["w_out"] = mm([(yn, dh2b)], M=D_MODEL, N=D_MODEL, K=T, tm=1024, tn=D_MODEL, tk=1024, a_t=True,
                        out_dtype=BF16, name="b_dw_out")
    dz, grads["g_v"], grads["w_s"], dbst, grads["sinks"], grads["g_a_out"], grads["g_b_out"] = _mixer_bwd(
        z, dyn, W["g_v"], W["w_s"], bst, W["sinks"], W["g_a_out"], W["g_b_out"], name="b_mixer")
    grads["b_s"] = jnp.transpose(dbst)
    mid["w_in"] = mm([(n2, dz)], M=D_MODEL, N=IN_COLS, K=T, tm=1024, tn=IN_COLS, tk=1024, a_t=True,
                     out_dtype=BF16, name="b_dw_in")
    tok = comm.reduce_pair_start("mid", mid)
    dn2 = mm([(dz, W["w_in"])], M=T, N=D_MODEL, K=IN_COLS, tm=512, tn=D_MODEL, tk=IN_COLS, b_kind="t",
             out_dtype=BF16, dep=tok, name="b_dn2")
    tok = comm.reduce_pair_done("mid", after=dn2)
    dh1, dh1b, grads["g_mix"] = _rms_bwd(h1, W["g_mix"], dn2, dh2, dep=tok, name="b_norm2")

    dG1, dU1, dwg, dwu, dwd = _swiglu_bwd_weights("b_ffn1", n1, G1, U1, A1, W["w1_down"], dh1b)
    comm.reduce_finish("mid", after=dwu)
    tok = comm.reduce_start("ffn1", {"w1_gate": dwg, "w1_up": dwu, "w1_down": dwd})
    dx, _, grads["g_ffn1"] = _swiglu_bwd_input("b_ffn1", x, W["g_ffn1"], dG1, dU1, W["w1_gate"], W["w1_up"], dh1, tok)
    comm.reduce_finish("ffn1", after=dx)
    return loss, dx, grads


BIG = ["w1_gate", "w1_up", "w1_down", "w_in", "w_out", "w_xq", "w_xkv", "w_xo", "w2_gate", "w2_up", "w2_down"]
SMALL = ["g_ffn1", "g_mix", "g_v", "w_s", "b_s", "sinks", "g_a_out", "g_b_out", "g_x", "g_mem", "g_ffn2", "g_final"]
ALL_W = ["g_ffn1", "w1_gate", "w1_up", "w1_down", "g_mix", "w_in", "g_v", "w_s", "b_s", "sinks", "g_a_out",
         "g_b_out", "w_out", "g_x", "g_mem", "w_xq", "w_xkv", "w_xo", "g_ffn2", "w2_gate", "w2_up", "w2_down",
         "g_final"]
ANY = pl.BlockSpec(memory_space=pl.ANY)


def _place():
    x, y, c = lax.axis_index("x"), lax.axis_index("y"), lax.axis_index("c")
    chips = [(1 - x, y), (x, 1 - y), (1 - x, 1 - y)]
    return x, y, c, chips


def _allgather_weights(shards, *, name):
    n = len(shards)

    def body(*refs):
        ins, outs = refs[:n], refs[n:2 * n]
        send, recv, loc = refs[2 * n:]
        x, y, c, chips = _place()
        me = 2 * x + y
        sib = (x, y, 1 - c)

        def half(w, slot, hc):
            h = shards[w].shape[0] // 2
            return outs[w].at[slot, pl.ds(hc * h, h), :]

        def copy(w, k, slot, hc, to, src=None):
            return pltpu.make_async_remote_copy(
                src_ref=half(w, slot, hc) if src is None else src, dst_ref=half(w, slot, hc),
                send_sem=send.at[6 * w + k], recv_sem=recv.at[6 * w + k], device_id=to, device_id_type=MESH)

        own = [pltpu.make_async_remote_copy(
            src_ref=ins[w], dst_ref=outs[w].at[me], send_sem=loc.at[w], recv_sem=loc.at[n + w],
            device_id=sib, device_id_type=MESH) for w in range(n)]
        for cp in own:
            cp.start()
        first = []
        for w in range(n):
            h = shards[w].shape[0] // 2
            for j, (tx, ty) in enumerate(chips):
                first.append(copy(w, j, me, c, (tx, ty, c), src=ins[w].at[pl.ds(c * h, h), :]))
                first[-1].start()
        passed = []
        for w in range(n):
            for j, (tx, ty) in enumerate(chips):
                slot = 2 * tx + ty
                copy(w, j, slot, c, (tx, ty, c)).wait_recv()
                passed.append(copy(w, 3 + j, slot, c, sib))
                passed[-1].start()
        for w in range(n):
            for j, (tx, ty) in enumerate(chips):
                copy(w, 3 + j, 2 * tx + ty, 1 - c, sib).wait_recv()
        for cp in first + passed:
            cp.wait_send()
        for cp in own:
            cp.wait()

    return pl.pallas_call(
        body, name=name, in_specs=[ANY] * n, out_specs=[ANY] * n,
        out_shape=[jax.ShapeDtypeStruct((N_CHIPS,) + s.shape, s.dtype) for s in shards],
        scratch_shapes=[pltpu.SemaphoreType.DMA((6 * n,)), pltpu.SemaphoreType.DMA((6 * n,)),
                        pltpu.SemaphoreType.DMA((2 * n,))],
    )(*shards)


def _pair_exchange(grads, *, name):
    n = len(grads)

    def body(*refs):
        ins, outs = refs[:n], refs[n:2 * n]
        send, recv = refs[2 * n:]
        x, y, c, _ = _place()
        cps = []
        for w in range(n):
            h = grads[w].shape[1] // 2
            cps.append(pltpu.make_async_remote_copy(
                src_ref=ins[w].at[:, pl.ds((1 - c) * h, h), :], dst_ref=outs[w],
                send_sem=send.at[w], recv_sem=recv.at[w], device_id=(x, y, 1 - c), device_id_type=MESH))
            cps[-1].start()
        for cp in cps:
            cp.wait()

    return pl.pallas_call(
        body, name=name, in_specs=[ANY] * n, out_specs=[ANY] * n,
        out_shape=[jax.ShapeDtypeStruct((N_CHIPS, g.shape[1] // 2, g.shape[2]), g.dtype) for g in grads],
        scratch_shapes=[pltpu.SemaphoreType.DMA((n,)), pltpu.SemaphoreType.DMA((n,))],
    )(*grads)


def _pair_sum(g, got, *, name):
    S, R, C = g.shape
    h = R // 2
    tr = _row_block(h, 3 * C * 2, 16)
    nr = h // tr

    def body(a_ref, b_ref, o_ref):
        o_ref[...] = (a_ref[...].astype(F32) + b_ref[...].astype(F32)).astype(BF16)

    return pl.pallas_call(
        body, name=name, grid=(S, nr),
        in_specs=[pl.BlockSpec((None, tr, C), lambda s, r: (s, lax.axis_index("c") * nr + r, 0)),
                  pl.BlockSpec((None, tr, C), lambda s, r: (s, r, 0))],
        out_specs=pl.BlockSpec((None, tr, C), lambda s, r: (s, r, 0)),
        out_shape=jax.ShapeDtypeStruct((S, h, C), BF16),
        compiler_params=_params(("parallel", "parallel")),
    )(g, got)


def _chip_sum(part, got, *, name):
    S, h, C = part.shape
    tr = _row_block(h, 4 * C * 2 + C * 4, 16)
    nr = h // tr

    def body(own_ref, g0_ref, g1_ref, g2_ref, o_ref):
        acc = own_ref[...].astype(F32) + g0_ref[...].astype(F32)
        o_ref[...] = (acc + g1_ref[...].astype(F32)) + g2_ref[...].astype(F32)

    def piece(j):
        return pl.BlockSpec((None, tr, C), lambda r: (j, r, 0))

    return pl.pallas_call(
        body, name=name, grid=(nr,),
        in_specs=[pl.BlockSpec((None, tr, C), lambda r: (2 * lax.axis_index("x") + lax.axis_index("y"), r, 0)),
                  piece(0), piece(1), piece(2)],
        out_specs=pl.BlockSpec((tr, C), lambda r: (lax.axis_index("c") * nr + r, 0)),
        out_shape=jax.ShapeDtypeStruct((2 * h, C), F32),
        compiler_params=_params(("parallel",)),
    )(part, got, got, got)


def _pair_gather(totals, *, name):
    n = len(totals)

    def body(*refs):
        ins, outs = refs[:n], refs[n:2 * n]
        send, recv = refs[2 * n:]
        x, y, c, _ = _place()
        cps = []
        for w in range(n):
            h = totals[w].shape[0] // 2
            cps.append(pltpu.make_async_remote_copy(
                src_ref=ins[w].at[pl.ds(c * h, h), :], dst_ref=outs[w].at[pl.ds(c * h, h), :],
                send_sem=send.at[w], recv_sem=recv.at[w], device_id=(x, y, 1 - c), device_id_type=MESH))
            cps[-1].start()
        for w in range(n):
            h = totals[w].shape[0] // 2
            theirs = outs[w].at[pl.ds((1 - c) * h, h), :]
            pltpu.make_async_remote_copy(
                src_ref=theirs, dst_ref=theirs, send_sem=send.at[w], recv_sem=recv.at[w],
                device_id=(x, y, 1 - c), device_id_type=MESH).wait_recv()
        for cp in cps:
            cp.wait_send()

    return pl.pallas_call(
        body, name=name, in_specs=[ANY] * n, out_specs=[ANY] * n,
        out_shape=[jax.ShapeDtypeStruct(t.shape, t.dtype) for t in totals],
        input_output_aliases={w: w for w in range(n)},
        scratch_shapes=[pltpu.SemaphoreType.DMA((n,)), pltpu.SemaphoreType.DMA((n,))],
    )(*totals)


def _allreduce_small(v, *, name):
    R, C = v.shape
    ND = 8

    def body(v_ref, o_ref, all_ref, send, recv, loc):
        x, y, c, chips = _place()
        me, sib = (x, y, c), (x, y, 1 - c)

        def rows(px, py, pc):
            return all_ref.at[pl.ds((4 * px + 2 * py + pc) * R, R), :]

        def copy(k, block, to, src=None):
            return pltpu.make_async_remote_copy(
                src_ref=rows(*block) if src is None else src, dst_ref=rows(*block),
                send_sem=send.at[k], recv_sem=recv.at[k], device_id=to, device_id_type=MESH)

        mine = pltpu.make_async_copy(v_ref, rows(*me), loc)
        mine.start()
        first = [copy(0, me, sib, src=v_ref)]
        first += [copy(1 + j, me, (*chip, c), src=v_ref) for j, chip in enumerate(chips)]
        for cp in first:
            cp.start()
        passed = [copy(4 + j, (*chip, c), sib) for j, chip in enumerate(chips)]
        for j, chip in enumerate(chips):
            copy(1 + j, (*chip, c), me).wait_recv()
            passed[j].start()
        copy(0, sib, me).wait_recv()
        for j, chip in enumerate(chips):
            copy(4 + j, (*chip, 1 - c), me).wait_recv()
        for cp in first + passed:
            cp.wait_send()
        mine.wait()
        acc = all_ref[0:R, :]
        for d in range(1, ND):
            acc = acc + all_ref[d * R:(d + 1) * R, :]
        o_ref[...] = acc

    vm = pl.BlockSpec(memory_space=pltpu.VMEM)
    return pl.pallas_call(
        body, name=name, in_specs=[vm], out_specs=[vm, vm],
        out_shape=[jax.ShapeDtypeStruct((R, C), F32), jax.ShapeDtypeStruct((ND * R, C), F32)],
        scratch_shapes=[pltpu.SemaphoreType.DMA((7,)), pltpu.SemaphoreType.DMA((7,)), pltpu.SemaphoreType.DMA],
        compiler_params=pltpu.CompilerParams(vmem_limit_bytes=VMEM_LIMIT),
    )(v)[0]


HBM = pl.BlockSpec(memory_space=pltpu.HBM)
SEM = pl.BlockSpec(memory_space=pltpu.SEMAPHORE)
EFFECT = pltpu.SideEffectType.DATAFLOW_SIDE_EFFECTING


def _remote(src, dst, send, recv, k, to):
    return pltpu.make_async_remote_copy(src_ref=src, dst_ref=dst, send_sem=send.at[k], recv_sem=recv.at[k],
                                        device_id=to, device_id_type=MESH)


def _split_start(bufs, plan, ncopies, *, name, after=None):
    nb = len(bufs)
    extra = [] if after is None else [after]

    def body(*refs):
        pos = nb + len(extra)
        send, recv, token = refs[pos], refs[pos + 1], refs[-1]
        for k, (src, dst, to) in enumerate(plan(refs[:nb])):
            _remote(src, dst, send, recv, k, to).start()
        token[...] = jnp.zeros_like(token)

    outs = pl.pallas_call(
        body, name=name,
        out_shape=(pltpu.SemaphoreType.DMA((ncopies,)), pltpu.SemaphoreType.DMA((ncopies,)),
                   *[pltpu.HBM(b.shape, b.dtype) for b in bufs], jax.ShapeDtypeStruct((SUBLANES, LANES), F32)),
        in_specs=[HBM] * nb + [ANY] * len(extra),
        out_specs=(SEM, SEM, *[HBM] * nb, pl.BlockSpec(memory_space=pltpu.VMEM)),
        input_output_aliases={i: 2 + i for i in range(nb)},
        compiler_params=pltpu.CompilerParams(has_side_effects=EFFECT),
    )(*[pltpu.with_memory_space_constraint(b, pltpu.HBM) for b in bufs], *extra)
    return outs[0], outs[1], list(outs[2:2 + nb]), outs[-1]


def _split_wait(started, plan, after, *, name):
    send, recv, bufs, _ = started
    nb = len(bufs)

    def body(*refs):
        send_sem, recv_sem = refs[nb], refs[nb + 1]
        for k, (src, dst, to) in enumerate(plan(refs[:nb])):
            cp = _remote(src, dst, send_sem, recv_sem, k, to)
            cp.wait_send()
            cp.wait_recv()

    outs = pl.pallas_call(
        body, name=name,
        out_shape=tuple(pltpu.HBM(b.shape, b.dtype) for b in bufs),
        in_specs=[HBM] * nb + [SEM, SEM, ANY], out_specs=tuple([HBM] * nb),
        input_output_aliases={i: i for i in range(nb)},
        compiler_params=pltpu.CompilerParams(has_side_effects=EFFECT),
    )(*bufs, send, recv, after)
    return list(outs)


def _gather_chip_plan(shapes):
    n = len(shapes)

    def plan(refs):
        srcs, lands = refs[:n], refs[n:]
        x, y, c, chips = _place()
        out = []
        for w in range(n):
            h = shapes[w][0] // 2
            for tx, ty in chips:
                out.append((srcs[w].at[pl.ds(c * h, h), :], lands[w].at[2 * x + y, pl.ds(c * h, h), :], (tx, ty, c)))
        return out

    return plan


def _gather_pair_plan(shapes):
    n = len(shapes)

    def plan(refs):
        srcs, lands = refs[:n], refs[n:]
        x, y, c, chips = _place()
        out = []
        for w in range(n):
            h = shapes[w][0] // 2
            for tx, ty in chips:
                half = lands[w].at[2 * tx + ty, pl.ds(c * h, h), :]
                out.append((half, half, (x, y, 1 - c)))
            out.append((srcs[w], lands[w].at[2 * x + y], (x, y, 1 - c)))
        return out

    return plan


def _reduce_pair_plan(shapes):
    n = len(shapes)

    def plan(refs):
        local, lands = refs[:n], refs[n:]
        x, y, c, _ = _place()
        out = []
        for w in range(n):
            h = shapes[w][1] // 2
            out.append((local[w].at[:, pl.ds((1 - c) * h, h), :], lands[w], (x, y, 1 - c)))
        return out

    return plan


def _reduce_chip_plan(n):
    def plan(refs):
        parts, lands = refs[:n], refs[n:]
        x, y, c, chips = _place()
        return [(parts[w].at[2 * tx + ty], lands[w].at[j], (tx, ty, c))
                for w in range(n) for j, (tx, ty) in enumerate(chips)]

    return plan


def _as_operands(gathered):
    out = {}
    for n, g in gathered.items():
        if n in ("w1_gate", "w1_up", "w2_gate", "w2_up", "w_xkv"):
            out[n] = g
        elif n == "w_in":
            out[n] = jnp.transpose(g, (1, 0, 2)).reshape(D_MODEL, IN_COLS)
        else:
            out[n] = g.reshape(g.shape[0] * g.shape[1], g.shape[2])
    return out


def _by_owner(n, g):
    if n == "w_in":
        return jnp.transpose(g.reshape(D_MODEL, N_CHIPS, IN_COLS // N_CHIPS), (1, 0, 2))
    if g.ndim == 2:
        return g.reshape(N_CHIPS, g.shape[0] // N_CHIPS, g.shape[1])
    return g


class _Comm:
    def __init__(self, shards):
        self.shards = shards
        self.total = {}
        self._flight = {}

    def gather_now(self, tag, names):
        got = _allgather_weights([self.shards[n] for n in names], name=f"gather_{tag}")
        return _as_operands(dict(zip(names, got)))

    def gather_start(self, tag, names, after):
        srcs = [self.shards[n] for n in names]
        lands = [lax.empty((N_CHIPS,) + s.shape, s.dtype) for s in srcs]
        started = _split_start(srcs + lands, _gather_chip_plan([s.shape for s in srcs]), 3 * len(srcs),
                               after=after, name=f"gather_{tag}_chips_start")
        self._flight[tag] = (names, started)
        return started[3]

    def gather_landed(self, tag, after):
        names, started = self._flight[tag]
        shapes = [self.shards[n].shape for n in names]
        bufs = _split_wait(started, _gather_chip_plan(shapes), after, name=f"gather_{tag}_chips_wait")
        started = _split_start(bufs, _gather_pair_plan(shapes), 4 * len(names), name=f"gather_{tag}_pair_start")
        self._flight[tag] = (names, started)
        return started[3]

    def gather_ready(self, tag, after):
        names, started = self._flight.pop(tag)
        shapes = [self.shards[n].shape for n in names]
        bufs = _split_wait(started, _gather_pair_plan(shapes), after, name=f"gather_{tag}_pair_wait")
        return _as_operands(dict(zip(names, bufs[len(names):])))

    def reduce_start(self, tag, grads):
        names = list(grads)
        local = [_by_owner(n, grads[n]) for n in names]
        return self._chip_start(tag, names, local, _pair_exchange(local, name=f"pair_exchange_{tag}"))

    def reduce_pair_start(self, tag, grads):
        names = list(grads)
        local = [_by_owner(n, grads[n]) for n in names]
        lands = [lax.empty((N_CHIPS, g.shape[1] // 2, g.shape[2]), g.dtype) for g in local]
        started = _split_start(local + lands, _reduce_pair_plan([g.shape for g in local]), len(names),
                               name=f"pair_exchange_{tag}_start")
        self._flight[tag] = (names, started)
        return started[3]

    def reduce_pair_done(self, tag, after):
        names, started = self._flight.pop(tag)
        n = len(names)
        bufs = _split_wait(started, _reduce_pair_plan([b.shape for b in started[2][:n]]), after,
                           name=f"pair_exchange_{tag}_wait")
        return self._chip_start(tag, names, bufs[:n], bufs[n:])

    def _chip_start(self, tag, names, local, from_sib):
        parts = [_pair_sum(g, s, name=f"pair_sum_{n}") for n, g, s in zip(names, local, from_sib)]
        lands = [lax.empty((N_CHIPS - 1,) + p.shape[1:], p.dtype) for p in parts]
        self._flight[tag] = (names, _split_start(parts + lands, _reduce_chip_plan(len(names)), 3 * len(names),
                                                 name=f"chip_exchange_{tag}_start"))
        return self._flight[tag][1][3]

    def reduce_finish(self, tag, after):
        names, started = self._flight.pop(tag)
        n = len(names)
        bufs = _split_wait(started, _reduce_chip_plan(n), after, name=f"chip_exchange_{tag}_wait")
        totals = [_chip_sum(p, s, name=f"chip_sum_{nm}") for nm, p, s in zip(names, bufs[:n], bufs[n:])]
        self.total.update(zip(names, _pair_gather(totals, name=f"pair_gather_{tag}")))


def _adamw(w, g, m, v, *, name):
    R, C = w.shape
    tr = _row_block(R, 8 * C * 4, SUBLANES)

    def body(w_ref, g_ref, m_ref, v_ref, go_ref, d_ref, nm_ref, nv_ref):
        gg = g_ref[...]
        go_ref[...] = gg
        m_new = ADAM_B1 * m_ref[...] + (1.0 - ADAM_B1) * gg
        v_new = ADAM_B2 * v_ref[...] + (1.0 - ADAM_B2) * (gg * gg)
        m_hat = m_new / (1.0 - ADAM_B1 ** ADAM_STEP)
        v_hat = v_new / (1.0 - ADAM_B2 ** ADAM_STEP)
        d_ref[...] = -ADAM_LR * (m_hat / (jnp.sqrt(v_hat) + ADAM_EPS) + ADAM_WD * w_ref[...])
        nm_ref[...] = m_new
        nv_ref[...] = v_new

    blk = pl.BlockSpec((tr, C), lambda i: (i, 0))
    shp = jax.ShapeDtypeStruct((R, C), F32)
    return pl.pallas_call(
        body, name=name, grid=(R // tr,), in_specs=[blk] * 4, out_specs=[blk] * 4, out_shape=[shp] * 4,
        compiler_params=_params(("parallel",)),
    )(w, g, m, v)


def _to2d(a):
    flat = a.reshape(-1)
    pad = (-flat.shape[0]) % (SUBLANES * LANES)
    if pad:
        flat = jnp.pad(flat, (0, pad))
    return flat.reshape(-1, LANES)


def _small_rows(shape):
    return -(-math.prod(shape) // (SUBLANES * LANES)) * SUBLANES


def _pack_small(parts):
    rows = jnp.concatenate([_to2d(p) for p in parts], axis=0)
    pad = (-rows.shape[0]) % 256
    if pad:
        rows = jnp.concatenate([rows, jnp.zeros((pad, LANES), rows.dtype)], axis=0)
    return rows


def _unpack_small(rows, shapes):
    out, r = [], 0
    for shp in shapes:
        size = math.prod(shp)
        nrow = _small_rows(shp)
        out.append(rows[r:r + nrow].reshape(-1)[:size].reshape(shp))
        r += nrow
    return out


def kernel(x, mem, g_ffn1, w1_gate, w1_up, w1_down, g_mix, w_in, g_v, w_s, b_s, sinks, g_a_out, g_b_out, w_out, g_x, g_mem, w_xq, w_xkv, w_xo, g_ffn2, w2_gate, w2_up, w2_down, g_final, loss_target, m_g_ffn1, m_w1_gate, m_w1_up, m_w1_down, m_g_mix, m_w_in, m_g_v, m_w_s, m_b_s, m_sinks, m_g_a_out, m_g_b_out, m_w_out, m_g_x, m_g_mem, m_w_xq, m_w_xkv, m_w_xo, m_g_ffn2, m_w2_gate, m_w2_up, m_w2_down, m_g_final, v_g_ffn1, v_w1_gate, v_w1_up, v_w1_down, v_g_mix, v_w_in, v_g_v, v_w_s, v_b_s, v_sinks, v_g_a_out, v_g_b_out, v_w_out, v_g_x, v_g_mem, v_w_xq, v_w_xkv, v_w_xo, v_g_ffn2, v_w2_gate, v_w2_up, v_w2_down, v_g_final):
    args = dict(locals())
    Wp = {n: args[n] for n in ALL_W}
    Mp = {n: args["m_" + n] for n in ALL_W}
    Vp = {n: args["v_" + n] for n in ALL_W}

    comm = _Comm({n: Wp[n][0].astype(BF16) for n in BIG})
    W = {n: Wp[n] for n in SMALL}
    W["g_final"] = Wp["g_final"].reshape(1, D_MODEL)
    for n in ("w_s", "b_s"):
        W[n] = Wp[n][0]
    loss, dx, grads = _local_step(x[0], mem[0], loss_target[0], W, comm)
    big_grad = comm.total

    small_shapes = [Wp[n].shape for n in SMALL]
    packed = _pack_small([grads[n].reshape(Wp[n].shape) for n in SMALL] + [loss])
    summed = _allreduce_small(packed, name="allreduce_small")
    small_grad = dict(zip(SMALL, _unpack_small(summed, small_shapes)))
    nrows = sum(_small_rows(s) for s in small_shapes)
    loss_total = summed[nrows, 0]

    grad_out, delta, new_m, new_v = {}, {}, {}, {}
    for n in BIG:
        shp = Wp[n].shape
        g, d, nm, nv = _adamw(Wp[n][0], big_grad[n], Mp[n][0], Vp[n][0], name=f"adamw_{n}")
        grad_out[n], delta[n], new_m[n], new_v[n] = g.reshape(shp), d.reshape(shp), nm.reshape(shp), nv.reshape(shp)
    sw = _pack_small([Wp[n] for n in SMALL])
    sg = _pack_small([small_grad[n] for n in SMALL])
    sm = _pack_small([Mp[n] for n in SMALL])
    sv = _pack_small([Vp[n] for n in SMALL])
    _, d, nm, nv = _adamw(sw, sg, sm, sv, name="adamw_small")
    for n, dd, mm_, vv_ in zip(SMALL, _unpack_small(d, small_shapes), _unpack_small(nm, small_shapes),
                               _unpack_small(nv, small_shapes)):
        grad_out[n], delta[n], new_m[n], new_v[n] = small_grad[n], dd, mm_, vv_

    return (loss_total, dx[None], *[grad_out[n] for n in ALL_W], *[delta[n] for n in ALL_W],
            *[new_m[n] for n in ALL_W], *[new_v[n] for n in ALL_W])
```

```python
import functools
import math

import jax
import jax.numpy as jnp
from jax import lax
from jax.experimental import pallas as pl
from jax.experimental.pallas import tpu as pltpu

F32 = jnp.float32
BF16 = jnp.bfloat16
MESH = pl.DeviceIdType.MESH

D_MODEL = 2048
D_FF = 5632
D_A = 1024
D_B = 1024
CHUNK = 128
A_GROUPS = 8
HEAD_DIM = 64
B_Q_HEADS = 16
B_KV_HEADS = 2
X_HEADS = 4
X_HEAD_DIM = 512
IN_COLS = 3328
O_Q = 2 * D_A
O_K = O_Q + D_B
O_V = O_K + B_KV_HEADS * HEAD_DIM
N_CHIPS = 4
EPS = 1e-5
NEG = -1e30
ADAM_LR = 0.001
ADAM_B1 = 0.9
ADAM_B2 = 0.999
ADAM_EPS = 1e-08
ADAM_WD = 0.01
ADAM_STEP = 10

V7X_VMEM_BYTES = 64 * 1024 * 1024
VMEM_LIMIT = 56 * 1024 * 1024
LANES = 128
SUBLANES = 8


ANY = pl.BlockSpec(memory_space=pl.ANY)


def _params(sem, vmem=VMEM_LIMIT):
    return pltpu.CompilerParams(dimension_semantics=sem, vmem_limit_bytes=vmem)


def _matmul(pairs, *, M, N, K, tm, tn, tk, a_t=False, b_kind="n", out_kind="n", out_dtype=F32,
            scale=1.0, res=None, norm_g=None, order="ij", dep=None, name):
    tm, tn, tk = min(tm, M), min(tn, N), min(tk, K)
    assert M % tm == 0 and N % tn == 0 and K % tk == 0, (name, M, N, K, tm, tn, tk)
    nk = K // tk
    npairs = len(pairs)
    b_t = b_kind in ("t", "st")

    def ij(g0, g1):
        return (g0, g1) if order == "ij" else (g1, g0)

    def a_map(g0, g1, k):
        i, _ = ij(g0, g1)
        return (k, i) if a_t else (i, k)

    a_spec = pl.BlockSpec((tk, tm) if a_t else (tm, tk), a_map)

    b0 = pairs[0][1]
    if b_kind == "n":
        b_spec = pl.BlockSpec((tk, tn), lambda g0, g1, k: (k, ij(g0, g1)[1]))
    elif b_kind == "t":
        b_spec = pl.BlockSpec((tn, tk), lambda g0, g1, k: (ij(g0, g1)[1], k))
    elif b_kind == "sn":
        ns = b0.shape[2]
        assert ns % tn == 0
        nps = ns // tn
        b_spec = pl.BlockSpec((None, tk, tn), lambda g0, g1, k: (ij(g0, g1)[1] // nps, k, ij(g0, g1)[1] % nps))
    else:
        ks = b0.shape[2]
        assert ks % tk == 0
        kps = ks // tk
        b_spec = pl.BlockSpec((None, tn, tk), lambda g0, g1, k: (k // kps, ij(g0, g1)[1], k % kps))

    if out_kind == "n":
        o_spec = pl.BlockSpec((tm, tn), lambda g0, g1, k: ij(g0, g1))
        o_shape = jax.ShapeDtypeStruct((M, N), out_dtype)
    else:
        ns = N // N_CHIPS
        assert ns % tn == 0
        nps_o = ns // tn
        o_spec = pl.BlockSpec((None, tm, tn), lambda g0, g1, k: (ij(g0, g1)[1] // nps_o, ij(g0, g1)[0], ij(g0, g1)[1] % nps_o))
        o_shape = jax.ShapeDtypeStruct((N_CHIPS, M, ns), out_dtype)

    in_specs, args = [], []
    for a, b in pairs:
        in_specs += [a_spec, b_spec]
        args += [a, b]
    if res is not None:
        in_specs.append(pl.BlockSpec((tm, tn), lambda g0, g1, k: ij(g0, g1)))
        args.append(res)
    if norm_g is not None:
        assert tn == N and out_kind == "n"
        in_specs.append(pl.BlockSpec((1, N), lambda g0, g1, k: (0, 0)))
        args.append(norm_g)
    if dep is not None:
        in_specs.append(ANY)
        args.append(dep)

    dn = (((0,) if a_t else (1,), (1,) if b_t else (0,)), ((), ()))

    def body(*refs):
        pos = 2 * npairs
        res_ref = refs[pos] if res is not None else None
        pos += res is not None
        g_ref = refs[pos] if norm_g is not None else None
        pos += (norm_g is not None) + (dep is not None)
        o_ref = refs[pos]
        n_ref = refs[pos + 1] if norm_g is not None else None
        acc_ref = refs[-1] if nk > 1 else None
        part = None
        for p in range(npairs):
            d = lax.dot_general(refs[2 * p][...], refs[2 * p + 1][...], dn, preferred_element_type=F32)
            part = d if part is None else part + d

        def finish(acc):
            r = acc * scale if scale != 1.0 else acc
            if res_ref is not None:
                r = res_ref[...] + r
            o_ref[...] = r.astype(out_dtype)
            if n_ref is not None:
                n_ref[...] = (r * _rstd(r) * g_ref[...]).astype(BF16)

        if nk == 1:
            finish(part)
        else:
            k = pl.program_id(2)

            @pl.when(k == 0)
            def _():
                acc_ref[...] = part

            @pl.when((k > 0) & (k < nk - 1))
            def _():
                acc_ref[...] += part

            @pl.when(k == nk - 1)
            def _():
                finish(acc_ref[...] + part)

    grid = (M // tm, N // tn, nk) if order == "ij" else (N // tn, M // tm, nk)
    out_specs, out_shape = o_spec, o_shape
    if norm_g is not None:
        out_specs = [o_spec, pl.BlockSpec((tm, tn), lambda g0, g1, k: ij(g0, g1))]
        out_shape = [o_shape, jax.ShapeDtypeStruct((M, N), BF16)]
    return pl.pallas_call(
        body, name=name, grid=grid, in_specs=in_specs, out_specs=out_specs, out_shape=out_shape,
        scratch_shapes=[pltpu.VMEM((tm, tn), F32)] if nk > 1 else [],
        compiler_params=_params(("parallel", "parallel", "arbitrary")),
    )(*args)


def _rstd(x):
    return lax.rsqrt(jnp.mean(x * x, axis=-1, keepdims=True) + EPS)


def _rms_bwd_math(x, g, dy):
    r = _rstd(x)
    gy = dy * g
    xr = x * r
    dx = r * (gy - xr * jnp.mean(gy * xr, axis=-1, keepdims=True))
    return dx, dy * xr


def _rms_fwd(h, g, *, name, tm=512, dep=None):
    T, Dm = h.shape
    tm = min(tm, T)

    def body(h_ref, g_ref, *rest):
        x = h_ref[...]
        rest[-1][...] = (x * _rstd(x) * g_ref[...]).astype(BF16)

    return pl.pallas_call(
        body, name=name, grid=(T // tm,),
        in_specs=[pl.BlockSpec((tm, Dm), lambda i: (i, 0)), pl.BlockSpec((1, Dm), lambda i: (0, 0))]
        + ([ANY] if dep is not None else []),
        out_specs=pl.BlockSpec((tm, Dm), lambda i: (i, 0)),
        out_shape=jax.ShapeDtypeStruct((T, Dm), BF16),
        compiler_params=_params(("parallel",)),
    )(h, g, *([dep] if dep is not None else []))


def _rms_bwd(h, g, dn, dres, *, name, tm=256, dep=None):
    T, Dm = h.shape
    tm = min(tm, T)
    has_res = dres is not None

    def body(*refs):
        h_ref, g_ref, dn_ref = refs[:3]
        pos = 3
        dres_ref = refs[pos] if has_res else None
        pos += has_res + (dep is not None)
        dh_ref, dhb_ref, dg_ref = refs[pos:pos + 3]
        dx, dgr = _rms_bwd_math(h_ref[...], g_ref[...], dn_ref[...].astype(F32))
        if has_res:
            dx = dres_ref[...] + dx
        dh_ref[...] = dx
        dhb_ref[...] = dx.astype(BF16)
        part = jnp.sum(dgr, axis=0, keepdims=True)

        @pl.when(pl.program_id(0) == 0)
        def _():
            dg_ref[...] = part

        @pl.when(pl.program_id(0) > 0)
        def _():
            dg_ref[...] += part

    row = pl.BlockSpec((tm, Dm), lambda i: (i, 0))
    vec = pl.BlockSpec((1, Dm), lambda i: (0, 0))
    args = [h, g, dn] + ([dres] if has_res else []) + ([dep] if dep is not None else [])
    return pl.pallas_call(
        body, name=name, grid=(T // tm,),
        in_specs=[row, vec, row] + ([row] if has_res else []) + ([ANY] if dep is not None else []),
        out_specs=[row, row, vec],
        out_shape=[jax.ShapeDtypeStruct((T, Dm), F32), jax.ShapeDtypeStruct((T, Dm), BF16),
                   jax.ShapeDtypeStruct((1, Dm), F32)],
        compiler_params=_params(("arbitrary",)),
    )(*args)


def _loss_head(h, g, tgt, *, name, tm=256):
    T, Dm = h.shape
    tm = min(tm, T)

    def body(h_ref, g_ref, t_ref, dh_ref, dhb_ref, dg_ref, loss_ref):
        x = h_ref[...]
        gv = g_ref[...]
        r = _rstd(x)
        diff = x * r * gv - t_ref[...]
        lpart = 0.5 * jnp.sum(jnp.mean(diff * diff, axis=-1, keepdims=True), axis=0, keepdims=True)
        dx, dgr = _rms_bwd_math(x, gv, diff * (1.0 / Dm))
        dh_ref[...] = dx
        dhb_ref[...] = dx.astype(BF16)
        part = jnp.sum(dgr, axis=0, keepdims=True)
        lrow = jnp.broadcast_to(lpart, (1, LANES))

        @pl.when(pl.program_id(0) == 0)
        def _():
            dg_ref[...] = part
            loss_ref[...] = lrow

        @pl.when(pl.program_id(0) > 0)
        def _():
            dg_ref[...] += part
            loss_ref[...] += lrow

    row = pl.BlockSpec((tm, Dm), lambda i: (i, 0))
    vec = pl.BlockSpec((1, Dm), lambda i: (0, 0))
    return pl.pallas_call(
        body, name=name, grid=(T // tm,),
        in_specs=[row, vec, row],
        out_specs=[row, row, vec, pl.BlockSpec((1, LANES), lambda i: (0, 0))],
        out_shape=[jax.ShapeDtypeStruct((T, Dm), F32), jax.ShapeDtypeStruct((T, Dm), BF16),
                   jax.ShapeDtypeStruct((1, Dm), F32), jax.ShapeDtypeStruct((1, LANES), F32)],
        compiler_params=_params(("arbitrary",)),
    )(h, g, tgt)


MXU_COLS = 256
FF_TILE = 2 * MXU_COLS


def _row_block(rows, row_bytes, align, budget=24 * 1024 * 1024):
    fits = [d for d in range(align, rows + 1, align) if rows % d == 0 and 2 * d * row_bytes <= budget]
    assert fits, (rows, row_bytes)
    return fits[-1]


def _swiglu_up(n, wg, wu, *, name, tm=1024, tn=FF_TILE):
    T, Dm = n.shape
    Fd = wg.shape[1]
    tm = min(tm, T)

    def body(n_ref, wg_ref, wu_ref, g_ref, u_ref, a_ref):
        x = n_ref[...]
        g = jnp.dot(x, wg_ref[...], preferred_element_type=F32)
        u = jnp.dot(x, wu_ref[...], preferred_element_type=F32)
        g_ref[...] = g.astype(BF16)
        u_ref[...] = u.astype(BF16)
        a_ref[...] = (g * jax.nn.sigmoid(g) * u).astype(BF16)

    wspec = pl.BlockSpec((Dm, tn), lambda j, i: (0, j))
    ospec = pl.BlockSpec((tm, tn), lambda j, i: (i, j))
    oshape = jax.ShapeDtypeStruct((T, Fd), BF16)
    return pl.pallas_call(
        body, name=name, grid=(Fd // tn, T // tm),
        in_specs=[pl.BlockSpec((tm, Dm), lambda j, i: (i, 0)), wspec, wspec],
        out_specs=[ospec, ospec, ospec], out_shape=[oshape, oshape, oshape],
        compiler_params=_params(("parallel", "parallel")),
    )(n, wg, wu)


def _swiglu_bwd_act(dhb, wd, G, U, *, name, tm=1024, tn=FF_TILE):
    T, Dm = dhb.shape
    Fd = wd.shape[0]
    tm, tn = min(tm, T), min(tn, Fd)

    def body(dh_ref, wd_ref, g_ref, u_ref, dg_ref, du_ref):
        da = 0.5 * lax.dot_general(dh_ref[...], wd_ref[...], (((1,), (1,)), ((), ())), preferred_element_type=F32)
        g = g_ref[...].astype(F32)
        u = u_ref[...].astype(F32)
        sg = jax.nn.sigmoid(g)
        dg_ref[...] = (da * u * (sg * (1.0 + g * (1.0 - sg)))).astype(BF16)
        du_ref[...] = (da * (g * sg)).astype(BF16)

    blk = pl.BlockSpec((tm, tn), lambda j, i: (i, j))
    oshape = jax.ShapeDtypeStruct((T, Fd), BF16)
    return pl.pallas_call(
        body, name=name, grid=(Fd // tn, T // tm),
        in_specs=[pl.BlockSpec((tm, Dm), lambda j, i: (i, 0)), pl.BlockSpec((tn, Dm), lambda j, i: (j, 0)), blk, blk],
        out_specs=[blk, blk], out_shape=[oshape, oshape],
        compiler_params=_params(("parallel", "parallel")),
    )(dhb, wd, G, U)


_INV_SQRT2 = 0.7071067811865476
_INV_SQRT2PI = 0.3989422804014327


def _erf(x):
    ax = jnp.abs(x)
    t = 1.0 / (1.0 + 0.3275911 * ax)
    poly = t * (0.254829592 + t * (-0.284496736 + t * (1.421413741 + t * (-1.453152027 + t * 1.061405429))))
    y = 1.0 - poly * jnp.exp(-ax * ax)
    return jnp.where(x < 0, -y, y)


def _gelu_cdf(x):
    return 0.5 * (1.0 + _erf(x * _INV_SQRT2))


def _lane_lt64(shape):
    return lax.broadcasted_iota(jnp.int32, shape, len(shape) - 1) < HEAD_DIM


def _dup_half(x, kv):
    rolled = pltpu.roll(x, HEAD_DIM, 1)
    lo = _lane_lt64(x.shape)
    return jnp.where(lo, x, rolled) if kv == 0 else jnp.where(lo, rolled, x)


HEADS_PER_KV = B_Q_HEADS // B_KV_HEADS
PAIRS = HEADS_PER_KV // 2


def _attn_bias():
    shape = (HEADS_PER_KV * CHUNK, 2 * CHUNK)
    qpos = (lax.broadcasted_iota(jnp.int32, shape, 0) & (CHUNK - 1)) + CHUNK
    kpos = lax.broadcasted_iota(jnp.int32, shape, 1)
    diff = qpos - kpos
    band = (diff >= 0) & (diff < CHUNK)
    return jnp.stack([jnp.where(band & (kpos >= CHUNK), 0.0, NEG), jnp.where(band, 0.0, NEG)]).astype(F32)


def _stack_heads(tiles, lo):
    parts = []
    for t in tiles:
        parts += [jnp.where(lo, t, 0.0), jnp.where(lo, 0.0, t)]
    return jnp.concatenate(parts, axis=0)


def _unstack_heads(s, lo):
    return [jnp.where(lo, s[2 * p * CHUNK:(2 * p + 1) * CHUNK], s[(2 * p + 1) * CHUNK:(2 * p + 2) * CHUNK])
            for p in range(PAIRS)]


def _stack_sinks(sk_ref, kv):
    return jnp.concatenate([jnp.broadcast_to(sk_ref[:, h:h + 1], (CHUNK, 1))
                            for h in range(kv * HEADS_PER_KV, (kv + 1) * HEADS_PER_KV)], axis=0)


def _sgu_forward(z_ref, gv, wsm, bst):
    zu = z_ref[:, 0:D_A]
    zv = z_ref[:, D_A:2 * D_A]
    u = zu * _gelu_cdf(zu)
    v = zv * _gelu_cdf(zv)
    rv = _rstd(v)
    vn = (v * rv * gv).astype(BF16)
    svs = []
    for g in range(A_GROUPS):
        sl = slice(g * CHUNK, (g + 1) * CHUNK)
        svs.append(jnp.dot(wsm[g], vn[:, sl], preferred_element_type=F32) + bst[:, g:g + 1])
    sv = jnp.concatenate(svs, axis=1)
    return zu, zv, u, v, rv, vn, sv


def _masked_ws(ws_ref):
    tril = lax.broadcasted_iota(jnp.int32, (CHUNK, CHUNK), 0) >= lax.broadcasted_iota(jnp.int32, (CHUNK, CHUNK), 1)
    return [jnp.where(tril, ws_ref[g], 0.0).astype(BF16) for g in range(A_GROUPS)], tril


def _attn_probs(qm, kkd, sink, bias):
    s = lax.dot_general(qm, kkd, (((1,), (1,)), ((), ())), preferred_element_type=F32) * (HEAD_DIM ** -0.5) + bias
    m = jnp.maximum(jnp.max(s, axis=-1, keepdims=True), sink)
    e = jnp.exp(s - m)
    es = jnp.exp(sink - m)
    inv = 1.0 / (jnp.sum(e, axis=-1, keepdims=True) + es)
    return e * inv, es * inv


def _mixer_fwd(z, gv, ws, bst, sinks, ga, gb, *, name):
    T = z.shape[0]
    nb = T // CHUNK
    kvb = O_K // (2 * CHUNK)

    def body(z_ref, zp_ref, bias_ref, gv_ref, ws_ref, bst_ref, sk_ref, ga_ref, gb_ref, o_ref):
        wsm, _ = _masked_ws(ws_ref)
        _, _, u, _, _, _, sv = _sgu_forward(z_ref, gv_ref[...], wsm, bst_ref[...])
        ya = u * sv
        o_ref[:, 0:D_A] = (ya * _rstd(ya) * ga_ref[...]).astype(BF16)

        mask = bias_ref[...]
        kk = jnp.concatenate([zp_ref[:, 0:CHUNK], z_ref[:, O_K:O_V]], axis=0)
        vv = jnp.concatenate([zp_ref[:, CHUNK:2 * CHUNK], z_ref[:, O_V:IN_COLS]], axis=0)
        lo = _lane_lt64((CHUNK, LANES))
        outs = []
        for kv in range(B_KV_HEADS):
            kkd = _dup_half(kk, kv).astype(BF16)
            vvd = _dup_half(vv, kv).astype(BF16)
            q = _stack_heads([z_ref[:, O_Q + (kv * PAIRS + pr) * LANES:O_Q + (kv * PAIRS + pr + 1) * LANES]
                              for pr in range(PAIRS)], lo).astype(BF16)
            p, _ = _attn_probs(q, kkd, _stack_sinks(sk_ref, kv), mask)
            outs += _unstack_heads(jnp.dot(p.astype(BF16), vvd, preferred_element_type=F32), lo)
        yb = jnp.concatenate(outs, axis=1)
        o_ref[:, D_A:D_A + D_B] = (yb * _rstd(yb) * gb_ref[...]).astype(BF16)

    full = lambda shape: pl.BlockSpec(shape, lambda i: (0,) * len(shape))
    return pl.pallas_call(
        body, name=name, grid=(nb,),
        in_specs=[pl.BlockSpec((CHUNK, IN_COLS), lambda i: (i, 0)),
                  pl.BlockSpec((CHUNK, 2 * CHUNK), lambda i: (jnp.maximum(i - 1, 0), kvb)),
                  pl.BlockSpec((None, HEADS_PER_KV * CHUNK, 2 * CHUNK), lambda i: (jnp.minimum(i, 1), 0, 0)),
                  full((1, D_A)), full((A_GROUPS, CHUNK, CHUNK)), full((CHUNK, A_GROUPS)), full((1, B_Q_HEADS)),
                  full((1, D_A)), full((1, D_B))],
        out_specs=pl.BlockSpec((CHUNK, D_A + D_B), lambda i: (i, 0)),
        out_shape=jax.ShapeDtypeStruct((T, D_A + D_B), BF16),
        compiler_params=_params(("parallel",)),
    )(z, z, _attn_bias(), gv, ws, bst, sinks, ga, gb)


def _mixer_bwd(z, dyn, gv, ws, bst, sinks, ga, gb, *, name):
    T = z.shape[0]
    nb = T // CHUNK
    kvb = O_K // (2 * CHUNK)
    NT = (((0,), (0,)), ((), ()))

    def body(z_ref, zp_ref, dy_ref, bias_ref, gv_ref, ws_ref, bst_ref, sk_ref, ga_ref, gb_ref,
             dz_ref, dgv_ref, dws_ref, dbst_ref, dsk_ref, dga_ref, dgb_ref, carry_ref, p_ref):
        step = pl.program_id(0)

        @pl.when(step == 0)
        def _():
            carry_ref[...] = jnp.zeros_like(carry_ref)
            dgv_ref[...] = jnp.zeros_like(dgv_ref)
            dws_ref[...] = jnp.zeros_like(dws_ref)
            dbst_ref[...] = jnp.zeros_like(dbst_ref)
            dsk_ref[...] = jnp.zeros_like(dsk_ref)
            dga_ref[...] = jnp.zeros_like(dga_ref)
            dgb_ref[...] = jnp.zeros_like(dgb_ref)

        wsm, tril = _masked_ws(ws_ref)
        gvv = gv_ref[...]
        zu, zv, u, v, rv, vn, sv = _sgu_forward(z_ref, gvv, wsm, bst_ref[...])
        ya = u * sv
        dya, dga_rows = _rms_bwd_math(ya, ga_ref[...], dy_ref[:, 0:D_A].astype(F32))
        dga_ref[...] += jnp.sum(dga_rows, axis=0, keepdims=True)
        du = dya * sv
        dsv = dya * u
        dvn_parts = []
        for g in range(A_GROUPS):
            sl = slice(g * CHUNK, (g + 1) * CHUNK)
            dsv_g = dsv[:, sl]
            dsv_gb = dsv_g.astype(BF16)
            dw = lax.dot_general(dsv_gb, vn[:, sl], (((1,), (1,)), ((), ())), preferred_element_type=F32)
            dws_ref[g] += jnp.where(tril, dw, 0.0)
            dbst_ref[:, g:g + 1] += jnp.sum(dsv_g, axis=1, keepdims=True)
            dvn_parts.append(lax.dot_general(wsm[g], dsv_gb, NT, preferred_element_type=F32))
        dvn = jnp.concatenate(dvn_parts, axis=1)
        dv, dgv_rows = _rms_bwd_math(v, gvv, dvn)
        dgv_ref[...] += jnp.sum(dgv_rows, axis=0, keepdims=True)
        dz_ref[:, 0:D_A] = (du * (_gelu_cdf(zu) + zu * jnp.exp(-0.5 * zu * zu) * _INV_SQRT2PI)).astype(BF16)
        dz_ref[:, D_A:2 * D_A] = (dv * (_gelu_cdf(zv) + zv * jnp.exp(-0.5 * zv * zv) * _INV_SQRT2PI)).astype(BF16)

        mask = bias_ref[...]
        kk = jnp.concatenate([zp_ref[:, 0:CHUNK], z_ref[:, O_K:O_V]], axis=0)
        vv = jnp.concatenate([zp_ref[:, CHUNK:2 * CHUNK], z_ref[:, O_V:IN_COLS]], axis=0)
        lo = _lane_lt64((CHUNK, LANES))
        kkd = [_dup_half(kk, kv).astype(BF16) for kv in range(B_KV_HEADS)]
        vvd = [_dup_half(vv, kv).astype(BF16) for kv in range(B_KV_HEADS)]
        outs, qs, psinks = [], [], []
        for kv in range(B_KV_HEADS):
            qs.append(_stack_heads([z_ref[:, O_Q + (kv * PAIRS + pr) * LANES:O_Q + (kv * PAIRS + pr + 1) * LANES]
                                    for pr in range(PAIRS)], lo).astype(BF16))
            p, ps = _attn_probs(qs[kv], kkd[kv], _stack_sinks(sk_ref, kv), mask)
            p_ref[kv] = p
            psinks.append(ps)
            outs += _unstack_heads(jnp.dot(p.astype(BF16), vvd[kv], preferred_element_type=F32), lo)
        yb = jnp.concatenate(outs, axis=1)
        dyb, dgb_rows = _rms_bwd_math(yb, gb_ref[...], dy_ref[:, D_A:D_A + D_B].astype(F32))
        dgb_ref[...] += jnp.sum(dgb_rows, axis=0, keepdims=True)

        dkk, dvv = [], []
        for kv in range(B_KV_HEADS):
            do = _stack_heads([dyb[:, (kv * PAIRS + pr) * LANES:(kv * PAIRS + pr + 1) * LANES]
                               for pr in range(PAIRS)], lo).astype(BF16)
            p = p_ref[kv]
            dvv.append(lax.dot_general(p.astype(BF16), do, NT, preferred_element_type=F32))
            dp = lax.dot_general(do, vvd[kv], (((1,), (1,)), ((), ())), preferred_element_type=F32)
            delta = jnp.sum(p * dp, axis=-1, keepdims=True)
            dsink = -psinks[kv] * delta
            for g in range(HEADS_PER_KV):
                h = kv * HEADS_PER_KV + g
                dsk_ref[:, h:h + 1] += jnp.sum(dsink[g * CHUNK:(g + 1) * CHUNK], axis=0, keepdims=True)
            ds = (p * (dp - delta) * (HEAD_DIM ** -0.5)).astype(BF16)
            dq = _unstack_heads(jnp.dot(ds, kkd[kv], preferred_element_type=F32), lo)
            for pr in range(PAIRS):
                c0 = O_Q + (kv * PAIRS + pr) * LANES
                dz_ref[:, c0:c0 + LANES] = dq[pr].astype(BF16)
            dkk.append(lax.dot_general(ds, qs[kv], NT, preferred_element_type=F32))

        def fold(parts):
            tot = [t + pltpu.roll(t, HEAD_DIM, 1) for t in parts]
            return jnp.where(_lane_lt64(tot[0].shape), tot[0], tot[1])

        dk_all = fold(dkk)
        dv_all = fold(dvv)
        dz_ref[:, O_K:O_V] = (dk_all[CHUNK:] + carry_ref[:, 0:CHUNK]).astype(BF16)
        dz_ref[:, O_V:IN_COLS] = (dv_all[CHUNK:] + carry_ref[:, CHUNK:2 * CHUNK]).astype(BF16)
        carry_ref[:, 0:CHUNK] = dk_all[:CHUNK]
        carry_ref[:, CHUNK:2 * CHUNK] = dv_all[:CHUNK]

    full = lambda shape: pl.BlockSpec(shape, lambda s: (0,) * len(shape))
    rev = lambda s: nb - 1 - s
    return pl.pallas_call(
        body, name=name, grid=(nb,),
        in_specs=[pl.BlockSpec((CHUNK, IN_COLS), lambda s: (rev(s), 0)),
                  pl.BlockSpec((CHUNK, 2 * CHUNK), lambda s: (jnp.maximum(rev(s) - 1, 0), kvb)),
                  pl.BlockSpec((CHUNK, D_A + D_B), lambda s: (rev(s), 0)),
                  pl.BlockSpec((None, HEADS_PER_KV * CHUNK, 2 * CHUNK), lambda s: (jnp.minimum(rev(s), 1), 0, 0)),
                  full((1, D_A)), full((A_GROUPS, CHUNK, CHUNK)), full((CHUNK, A_GROUPS)), full((1, B_Q_HEADS)),
                  full((1, D_A)), full((1, D_B))],
        out_specs=[pl.BlockSpec((CHUNK, IN_COLS), lambda s: (rev(s), 0)),
                   full((1, D_A)), full((A_GROUPS, CHUNK, CHUNK)), full((CHUNK, A_GROUPS)), full((1, B_Q_HEADS)),
                   full((1, D_A)), full((1, D_B))],
        out_shape=[jax.ShapeDtypeStruct((T, IN_COLS), BF16), jax.ShapeDtypeStruct((1, D_A), F32),
                   jax.ShapeDtypeStruct((A_GROUPS, CHUNK, CHUNK), F32), jax.ShapeDtypeStruct((CHUNK, A_GROUPS), F32),
                   jax.ShapeDtypeStruct((1, B_Q_HEADS), F32), jax.ShapeDtypeStruct((1, D_A), F32),
                   jax.ShapeDtypeStruct((1, D_B), F32)],
        scratch_shapes=[pltpu.VMEM((CHUNK, 2 * CHUNK), F32), pltpu.VMEM((B_KV_HEADS, HEADS_PER_KV * CHUNK, 2 * CHUNK), F32)],
        compiler_params=_params(("arbitrary",)),
    )(z, z, dyn, _attn_bias(), gv, ws, bst, sinks, ga, gb)


def _xattn_probs(qh, kh):
    s = lax.dot_general(qh, kh, (((1,), (1,)), ((), ())), preferred_element_type=F32) * (X_HEAD_DIM ** -0.5)
    e = jnp.exp(s - jnp.max(s, axis=-1, keepdims=True))
    return e / jnp.sum(e, axis=-1, keepdims=True)


def _xattn_fwd(q, kvm, *, name, tm=512):
    T = q.shape[0]
    Mm = kvm.shape[0]
    tm = min(tm, T)

    def body(q_ref, kv_ref, o_ref):
        for h in range(X_HEADS):
            sl = slice(h * X_HEAD_DIM, (h + 1) * X_HEAD_DIM)
            kh = kv_ref[:, sl].astype(BF16)
            vh = kv_ref[:, D_MODEL + h * X_HEAD_DIM:D_MODEL + (h + 1) * X_HEAD_DIM].astype(BF16)
            p = _xattn_probs(q_ref[:, sl], kh)
            o_ref[:, sl] = jnp.dot(p.astype(BF16), vh, preferred_element_type=F32).astype(BF16)

    return pl.pallas_call(
        body, name=name, grid=(T // tm,),
        in_specs=[pl.BlockSpec((tm, D_MODEL), lambda i: (i, 0)), pl.BlockSpec((Mm, 2 * D_MODEL), lambda i: (0, 0))],
        out_specs=pl.BlockSpec((tm, D_MODEL), lambda i: (i, 0)),
        out_shape=jax.ShapeDtypeStruct((T, D_MODEL), BF16),
        compiler_params=_params(("parallel",)),
    )(q, kvm)


def _xattn_bwd(q, kvm, do, *, name, tm=512):
    T = q.shape[0]
    Mm = kvm.shape[0]
    tm = min(tm, T)
    NT = (((0,), (0,)), ((), ()))

    def body(q_ref, kv_ref, do_ref, dq_ref, dkv_ref):
        @pl.when(pl.program_id(0) == 0)
        def _():
            dkv_ref[...] = jnp.zeros_like(dkv_ref)

        for h in range(X_HEADS):
            sl = slice(h * X_HEAD_DIM, (h + 1) * X_HEAD_DIM)
            slv = slice(D_MODEL + h * X_HEAD_DIM, D_MODEL + (h + 1) * X_HEAD_DIM)
            kh = kv_ref[:, sl].astype(BF16)
            vh = kv_ref[:, slv].astype(BF16)
            qh = q_ref[:, sl]
            doh = do_ref[:, sl]
            p = _xattn_probs(qh, kh)
            dkv_ref[:, slv] += lax.dot_general(p.astype(BF16), doh, NT, preferred_element_type=F32)
            dp = lax.dot_general(doh, vh, (((1,), (1,)), ((), ())), preferred_element_type=F32)
            ds = (p * (dp - jnp.sum(p * dp, axis=-1, keepdims=True)) * (X_HEAD_DIM ** -0.5)).astype(BF16)
            dq_ref[:, sl] = jnp.dot(ds, kh, preferred_element_type=F32).astype(BF16)
            dkv_ref[:, sl] += lax.dot_general(ds, qh, NT, preferred_element_type=F32)

    row = pl.BlockSpec((tm, D_MODEL), lambda i: (i, 0))
    kvs = pl.BlockSpec((Mm, 2 * D_MODEL), lambda i: (0, 0))
    return pl.pallas_call(
        body, name=name, grid=(T // tm,),
        in_specs=[row, kvs, row], out_specs=[row, kvs],
        out_shape=[jax.ShapeDtypeStruct((T, D_MODEL), BF16), jax.ShapeDtypeStruct((Mm, 2 * D_MODEL), F32)],
        compiler_params=_params(("arbitrary",)),
    )(q, kvm, do)


def _swiglu_bwd_weights(tag, n, G, U, A, wd, dhb):
    T = n.shape[0]
    dG, dU = _swiglu_bwd_act(dhb, wd, G, U, name=f"{tag}_bwd_act", tm=1024)
    dwd = _matmul([(A, dhb)], M=D_FF, N=D_MODEL, K=T, tm=1408, tn=1024, tk=2048, a_t=True, out_dtype=BF16,
                  scale=0.5, name=f"{tag}_dwd")
    dwg = _matmul([(n, dG)], M=D_MODEL, N=D_FF, K=T, tm=1024, tn=1408, tk=2048, a_t=True, out_kind="s",
                  out_dtype=BF16, order="ji", name=f"{tag}_dwg")
    dwu = _matmul([(n, dU)], M=D_MODEL, N=D_FF, K=T, tm=1024, tn=1408, tk=2048, a_t=True, out_kind="s",
                  out_dtype=BF16, order="ji", name=f"{tag}_dwu")
    return dG, dU, dwg, dwu, dwd


def _swiglu_bwd_input(tag, hin, g_norm, dG, dU, wg, wu, dh, dep):
    T = hin.shape[0]
    dn = _matmul([(dG, wg), (dU, wu)], M=T, N=D_MODEL, K=D_FF, tm=512, tn=D_MODEL // 2, tk=D_FF // 2, b_kind="t",
                 out_dtype=BF16, dep=dep, name=f"{tag}_dn")
    return _rms_bwd(hin, g_norm, dn, dh, name=f"{tag}_norm_bwd")


GROUP_FFN1 = ["w1_gate", "w1_up", "w1_down"]
GROUP_MID = ["w_in", "w_out", "w_xq", "w_xkv", "w_xo"]
GROUP_FFN2 = ["w2_gate", "w2_up", "w2_down"]


def _local_step(x, mem, tgt, W, comm):
    T = x.shape[0]
    Mm = mem.shape[0]
    mm = functools.partial(_matmul)

    W = {**W, **comm.gather_now("ffn1_up", ["w1_gate", "w1_up"])}
    tok = comm.gather_start("ffn1_down", ["w1_down"], after=W["w1_up"])
    tok = comm.gather_start("mid", GROUP_MID, after=tok)
    tok = comm.gather_start("ffn2", GROUP_FFN2, after=tok)
    n1 = _rms_fwd(x, W["g_ffn1"], dep=tok, name="f_norm1")
    G1, U1, A1 = _swiglu_up(n1, W["w1_gate"], W["w1_up"], name="f_ffn1_up")
    tok = comm.gather_landed("ffn1_down", after=A1)
    tok = comm.gather_landed("mid", after=tok)
    W = {**W, **comm.gather_ready("ffn1_down", after=tok)}
    h1, n2 = mm([(A1, W["w1_down"])], M=T, N=D_MODEL, K=D_FF, tm=512, tn=D_MODEL, tk=1408, scale=0.5, res=x,
                norm_g=W["g_mix"], name="f_ffn1_down")
    W = {**W, **comm.gather_ready("mid", after=n2)}
    z = mm([(n2, W["w_in"])], M=T, N=IN_COLS, K=D_MODEL, tm=512, tn=IN_COLS // 2, tk=D_MODEL, name="f_w_in")
    bst = jnp.transpose(W["b_s"])
    yn = _mixer_fwd(z, W["g_v"], W["w_s"], bst, W["sinks"], W["g_a_out"], W["g_b_out"], name="f_mixer")
    tok = comm.gather_landed("ffn2", after=yn)
    h2, n3 = mm([(yn, W["w_out"])], M=T, N=D_MODEL, K=D_MODEL, tm=512, tn=D_MODEL, tk=D_MODEL, res=h1,
                norm_g=W["g_x"], dep=tok, name="f_w_out")
    memn = _rms_fwd(mem, W["g_mem"], name="f_norm_mem")
    q3 = mm([(n3, W["w_xq"])], M=T, N=D_MODEL, K=D_MODEL, tm=1024, tn=D_MODEL, tk=D_MODEL, out_dtype=BF16,
            name="f_w_xq")
    kvm = mm([(memn, W["w_xkv"])], M=Mm, N=2 * D_MODEL, K=D_MODEL, tm=Mm, tn=1024, tk=D_MODEL, b_kind="n",
             name="f_w_xkv")
    o3 = _xattn_fwd(q3, kvm, name="f_xattn")
    h3, n4 = mm([(o3, W["w_xo"])], M=T, N=D_MODEL, K=D_MODEL, tm=512, tn=D_MODEL, tk=D_MODEL, res=h2,
                norm_g=W["g_ffn2"], name="f_w_xo")
    W = {**W, **comm.gather_ready("ffn2", after=n4)}
    G2, U2, A2 = _swiglu_up(n4, W["w2_gate"], W["w2_up"], name="f_ffn2_up")
    h4 = mm([(A2, W["w2_down"])], M=T, N=D_MODEL, K=D_FF, tm=512, tn=D_MODEL, tk=1408, scale=0.5, res=h3,
            name="f_ffn2_down")

    grads = {}
    dh4, dh4b, grads["g_final"], loss = _loss_head(h4, W["g_final"], tgt, name="loss_head")
    dG2, dU2, dwg, dwu, dwd = _swiglu_bwd_weights("b_ffn2", n4, G2, U2, A2, W["w2_down"], dh4b)
    tok = comm.reduce_pair_start("ffn2", {"w2_gate": dwg, "w2_up": dwu, "w2_down": dwd})
    dh3, dh3b, grads["g_ffn2"] = _swiglu_bwd_input("b_ffn2", h3, W["g_ffn2"], dG2, dU2, W["w2_gate"], W["w2_up"],
                                                   dh4, tok)
    tok = comm.reduce_pair_done("ffn2", after=dh3b)

    mid = {}
    do3 = mm([(dh3b, W["w_xo"])], M=T, N=D_MODEL, K=D_MODEL, tm=512, tn=D_MODEL, tk=D_MODEL, b_kind="t",
             out_dtype=BF16, dep=tok, name="b_do3")
    mid["w_xo"] = mm([(o3, dh3b)], M=D_MODEL, N=D_MODEL, K=T, tm=1024, tn=D_MODEL, tk=1024, a_t=True,
                       out_dtype=BF16, name="b_dw_xo")
    dq3, dkvm = _xattn_bwd(q3, kvm, do3, name="b_xattn")
    mid["w_xq"] = mm([(n3, dq3)], M=D_MODEL, N=D_MODEL, K=T, tm=1024, tn=D_MODEL, tk=1024, a_t=True,
                       out_dtype=BF16, name="b_dw_xq")
    dn3 = mm([(dq3, W["w_xq"])], M=T, N=D_MODEL, K=D_MODEL, tm=512, tn=D_MODEL, tk=D_MODEL, b_kind="t",
             out_dtype=BF16, name="b_dn3")
    dh2, dh2b, grads["g_x"] = _rms_bwd(h2, W["g_x"], dn3, dh3, name="b_norm3")
    dkvmb = dkvm.astype(BF16)
    mid["w_xkv"] = mm([(memn, dkvmb)], M=D_MODEL, N=2 * D_MODEL, K=Mm, tm=D_MODEL, tn=1024, tk=Mm, a_t=True,
                        out_kind="s", out_dtype=BF16, name="b_dw_xkv")
    dmemn = mm([(dkvmb, W["w_xkv"])], M=Mm, N=D_MODEL, K=2 * D_MODEL, tm=Mm, tn=D_MODEL, tk=1024, b_kind="t",
               name="b_dmemn")
    _, _, grads["g_mem"] = _rms_bwd(mem, W["g_mem"], dmemn, None, name="b_norm_mem")
    comm.reduce_finish("ffn2", after=dh2b)

    dyn = mm([(dh2b, W["w_out"])], M=T, N=D_MODEL, K=D_MODEL, tm=1024, tn=D_MODEL, tk=D_MODEL, b_kind="t",
             out_dtype=BF16, name="b_dyn")
    mid["w_out"] = mm([(yn, dh2b)], M=D_MODEL, N=D_MODEL, K=T, tm=1024, tn=D_MODEL, tk=1024, a_t=True,
                        out_dtype=BF16, name="b_dw_out")
    dz, grads["g_v"], grads["w_s"], dbst, grads["sinks"], grads["g_a_out"], grads["g_b_out"] = _mixer_bwd(
        z, dyn, W["g_v"], W["w_s"], bst, W["sinks"], W["g_a_out"], W["g_b_out"], name="b_mixer")
    grads["b_s"] = jnp.transpose(dbst)
    mid["w_in"] = mm([(n2, dz)], M=D_MODEL, N=IN_COLS, K=T, tm=1024, tn=IN_COLS, tk=1024, a_t=True,
                     out_dtype=BF16, name="b_dw_in")
    tok = comm.reduce_pair_start("mid", mid)
    dn2 = mm([(dz, W["w_in"])], M=T, N=D_MODEL, K=IN_COLS, tm=512, tn=D_MODEL, tk=IN_COLS, b_kind="t",
             out_dtype=BF16, dep=tok, name="b_dn2")
    tok = comm.reduce_pair_done("mid", after=dn2)
    dh1, dh1b, grads["g_mix"] = _rms_bwd(h1, W["g_mix"], dn2, dh2, dep=tok, name="b_norm2")

    dG1, dU1, dwg, dwu, dwd = _swiglu_bwd_weights("b_ffn1", n1, G1, U1, A1, W["w1_down"], dh1b)
    comm.reduce_finish("mid", after=dwu)
    tok = comm.reduce_start("ffn1", {"w1_gate": dwg, "w1_up": dwu, "w1_down": dwd})
    dx, _, grads["g_ffn1"] = _swiglu_bwd_input("b_ffn1", x, W["g_ffn1"], dG1, dU1, W["w1_gate"], W["w1_up"], dh1, tok)
    comm.reduce_finish("ffn1", after=dx)
    return loss, dx, grads


BIG = ["w1_gate", "w1_up", "w1_down", "w_in", "w_out", "w_xq", "w_xkv", "w_xo", "w2_gate", "w2_up", "w2_down"]
SMALL = ["g_ffn1", "g_mix", "g_v", "w_s", "b_s", "sinks", "g_a_out", "g_b_out", "g_x", "g_mem", "g_ffn2", "g_final"]
ALL_W = ["g_ffn1", "w1_gate", "w1_up", "w1_down", "g_mix", "w_in", "g_v", "w_s", "b_s", "sinks", "g_a_out",
         "g_b_out", "w_out", "g_x", "g_mem", "w_xq", "w_xkv", "w_xo", "g_ffn2", "w2_gate", "w2_up", "w2_down",
         "g_final"]
ANY = pl.BlockSpec(memory_space=pl.ANY)


def _place():
    x, y, c = lax.axis_index("x"), lax.axis_index("y"), lax.axis_index("c")
    chips = [(1 - x, y), (x, 1 - y), (1 - x, 1 - y)]
    return x, y, c, chips


COL_SHARDED = ("w1_gate", "w1_up", "w2_gate", "w2_up", "w_xkv")


def _gathered_shape(shape, by_cols):
    rows, cols = shape
    return (rows, N_CHIPS * cols) if by_cols else (N_CHIPS, rows, cols)


def _owner_rows(ref, shape, by_cols, slot, r0, rows):
    cols = shape[1]
    if by_cols:
        return ref.at[pl.ds(r0, rows), pl.ds(pl.multiple_of(slot * cols, LANES), cols)]
    return ref.at[slot, pl.ds(r0, rows), :]


def _allgather_weights(shards, by_cols, *, name):
    n = len(shards)

    def body(*refs):
        ins, outs = refs[:n], refs[n:2 * n]
        send, recv, loc = refs[2 * n:]
        x, y, c, chips = _place()
        me = 2 * x + y
        sib = (x, y, 1 - c)

        def half(w, slot, hc):
            h = shards[w].shape[0] // 2
            return _owner_rows(outs[w], shards[w].shape, by_cols[w], slot, hc * h, h)

        def copy(w, k, slot, hc, to, src=None):
            return pltpu.make_async_remote_copy(
                src_ref=half(w, slot, hc) if src is None else src, dst_ref=half(w, slot, hc),
                send_sem=send.at[6 * w + k], recv_sem=recv.at[6 * w + k], device_id=to, device_id_type=MESH)

        own = [pltpu.make_async_remote_copy(
            src_ref=ins[w], dst_ref=_owner_rows(outs[w], shards[w].shape, by_cols[w], me, 0, shards[w].shape[0]),
            send_sem=loc.at[w], recv_sem=loc.at[n + w], device_id=sib, device_id_type=MESH) for w in range(n)]
        for cp in own:
            cp.start()
        first = []
        for w in range(n):
            h = shards[w].shape[0] // 2
            for j, (tx, ty) in enumerate(chips):
                first.append(copy(w, j, me, c, (tx, ty, c), src=ins[w].at[pl.ds(c * h, h), :]))
                first[-1].start()
        passed = []
        for w in range(n):
            for j, (tx, ty) in enumerate(chips):
                slot = 2 * tx + ty
                copy(w, j, slot, c, (tx, ty, c)).wait_recv()
                passed.append(copy(w, 3 + j, slot, c, sib))
                passed[-1].start()
        for w in range(n):
            for j, (tx, ty) in enumerate(chips):
                copy(w, 3 + j, 2 * tx + ty, 1 - c, sib).wait_recv()
        for cp in first + passed:
            cp.wait_send()
        for cp in own:
            cp.wait()

    return pl.pallas_call(
        body, name=name, in_specs=[ANY] * n, out_specs=[ANY] * n,
        out_shape=[jax.ShapeDtypeStruct(_gathered_shape(s.shape, bc), s.dtype) for s, bc in zip(shards, by_cols)],
        scratch_shapes=[pltpu.SemaphoreType.DMA((6 * n,)), pltpu.SemaphoreType.DMA((6 * n,)),
                        pltpu.SemaphoreType.DMA((2 * n,))],
    )(*shards)


def _pair_exchange(grads, *, name):
    n = len(grads)

    def body(*refs):
        ins, outs = refs[:n], refs[n:2 * n]
        send, recv = refs[2 * n:]
        x, y, c, _ = _place()
        cps = []
        for w in range(n):
            h = grads[w].shape[1] // 2
            cps.append(pltpu.make_async_remote_copy(
                src_ref=ins[w].at[:, pl.ds((1 - c) * h, h), :], dst_ref=outs[w],
                send_sem=send.at[w], recv_sem=recv.at[w], device_id=(x, y, 1 - c), device_id_type=MESH))
            cps[-1].start()
        for cp in cps:
            cp.wait()

    return pl.pallas_call(
        body, name=name, in_specs=[ANY] * n, out_specs=[ANY] * n,
        out_shape=[jax.ShapeDtypeStruct((N_CHIPS, g.shape[1] // 2, g.shape[2]), g.dtype) for g in grads],
        scratch_shapes=[pltpu.SemaphoreType.DMA((n,)), pltpu.SemaphoreType.DMA((n,))],
    )(*grads)


def _pair_sum(g, got, *, name):
    S, R, C = g.shape
    h = R // 2
    tr = _row_block(h, 3 * C * 2, 16)
    nr = h // tr

    def body(a_ref, b_ref, o_ref):
        o_ref[...] = (a_ref[...].astype(F32) + b_ref[...].astype(F32)).astype(BF16)

    return pl.pallas_call(
        body, name=name, grid=(S, nr),
        in_specs=[pl.BlockSpec((None, tr, C), lambda s, r: (s, lax.axis_index("c") * nr + r, 0)),
                  pl.BlockSpec((None, tr, C), lambda s, r: (s, r, 0))],
        out_specs=pl.BlockSpec((None, tr, C), lambda s, r: (s, r, 0)),
        out_shape=jax.ShapeDtypeStruct((S, h, C), BF16),
        compiler_params=_params(("parallel", "parallel")),
    )(g, got)


def _chip_sum(part, got, *, name):
    S, h, C = part.shape
    tr = _row_block(h, 4 * C * 2 + C * 4, 16)
    nr = h // tr

    def body(own_ref, g0_ref, g1_ref, g2_ref, o_ref):
        acc = own_ref[...].astype(F32) + g0_ref[...].astype(F32)
        o_ref[...] = (acc + g1_ref[...].astype(F32)) + g2_ref[...].astype(F32)

    def piece(j):
        return pl.BlockSpec((None, tr, C), lambda r: (j, r, 0))

    return pl.pallas_call(
        body, name=name, grid=(nr,),
        in_specs=[pl.BlockSpec((None, tr, C), lambda r: (2 * lax.axis_index("x") + lax.axis_index("y"), r, 0)),
                  piece(0), piece(1), piece(2)],
        out_specs=pl.BlockSpec((tr, C), lambda r: (lax.axis_index("c") * nr + r, 0)),
        out_shape=jax.ShapeDtypeStruct((2 * h, C), F32),
        compiler_params=_params(("parallel",)),
    )(part, got, got, got)


def _pair_gather(totals, *, name):
    n = len(totals)

    def body(*refs):
        ins, outs = refs[:n], refs[n:2 * n]
        send, recv = refs[2 * n:]
        x, y, c, _ = _place()
        cps = []
        for w in range(n):
            h = totals[w].shape[0] // 2
            cps.append(pltpu.make_async_remote_copy(
                src_ref=ins[w].at[pl.ds(c * h, h), :], dst_ref=outs[w].at[pl.ds(c * h, h), :],
                send_sem=send.at[w], recv_sem=recv.at[w], device_id=(x, y, 1 - c), device_id_type=MESH))
            cps[-1].start()
        for w in range(n):
            h = totals[w].shape[0] // 2
            theirs = outs[w].at[pl.ds((1 - c) * h, h), :]
            pltpu.make_async_remote_copy(
                src_ref=theirs, dst_ref=theirs, send_sem=send.at[w], recv_sem=recv.at[w],
                device_id=(x, y, 1 - c), device_id_type=MESH).wait_recv()
        for cp in cps:
            cp.wait_send()

    return pl.pallas_call(
        body, name=name, in_specs=[ANY] * n, out_specs=[ANY] * n,
        out_shape=[jax.ShapeDtypeStruct(t.shape, t.dtype) for t in totals],
        input_output_aliases={w: w for w in range(n)},
        scratch_shapes=[pltpu.SemaphoreType.DMA((n,)), pltpu.SemaphoreType.DMA((n,))],
    )(*totals)


def _allreduce_small(v, *, name):
    R, C = v.shape
    ND = 8

    def body(v_ref, o_ref, all_ref, send, recv, loc):
        x, y, c, chips = _place()
        me, sib = (x, y, c), (x, y, 1 - c)

        def rows(px, py, pc):
            return all_ref.at[pl.ds((4 * px + 2 * py + pc) * R, R), :]

        def copy(k, block, to, src=None):
            return pltpu.make_async_remote_copy(
                src_ref=rows(*block) if src is None else src, dst_ref=rows(*block),
                send_sem=send.at[k], recv_sem=recv.at[k], device_id=to, device_id_type=MESH)

        mine = pltpu.make_async_copy(v_ref, rows(*me), loc)
        mine.start()
        first = [copy(0, me, sib, src=v_ref)]
        first += [copy(1 + j, me, (*chip, c), src=v_ref) for j, chip in enumerate(chips)]
        for cp in first:
            cp.start()
        passed = [copy(4 + j, (*chip, c), sib) for j, chip in enumerate(chips)]
        for j, chip in enumerate(chips):
            copy(1 + j, (*chip, c), me).wait_recv()
            passed[j].start()
        copy(0, sib, me).wait_recv()
        for j, chip in enumerate(chips):
            copy(4 + j, (*chip, 1 - c), me).wait_recv()
        for cp in first + passed:
            cp.wait_send()
        mine.wait()
        acc = all_ref[0:R, :]
        for d in range(1, ND):
            acc = acc + all_ref[d * R:(d + 1) * R, :]
        o_ref[...] = acc

    vm = pl.BlockSpec(memory_space=pltpu.VMEM)
    return pl.pallas_call(
        body, name=name, in_specs=[vm], out_specs=[vm, vm],
        out_shape=[jax.ShapeDtypeStruct((R, C), F32), jax.ShapeDtypeStruct((ND * R, C), F32)],
        scratch_shapes=[pltpu.SemaphoreType.DMA((7,)), pltpu.SemaphoreType.DMA((7,)), pltpu.SemaphoreType.DMA],
        compiler_params=pltpu.CompilerParams(vmem_limit_bytes=VMEM_LIMIT),
    )(v)[0]


HBM = pl.BlockSpec(memory_space=pltpu.HBM)
SEM = pl.BlockSpec(memory_space=pltpu.SEMAPHORE)
EFFECT = pltpu.SideEffectType.DATAFLOW_SIDE_EFFECTING


def _remote(src, dst, send, recv, k, to):
    return pltpu.make_async_remote_copy(src_ref=src, dst_ref=dst, send_sem=send.at[k], recv_sem=recv.at[k],
                                        device_id=to, device_id_type=MESH)


def _split_start(bufs, plan, ncopies, *, name, after=None):
    nb = len(bufs)
    extra = [] if after is None else [after]

    def body(*refs):
        pos = nb + len(extra)
        send, recv, token = refs[pos], refs[pos + 1], refs[-1]
        for k, (src, dst, to) in enumerate(plan(refs[:nb])):
            _remote(src, dst, send, recv, k, to).start()
        token[...] = jnp.zeros_like(token)

    outs = pl.pallas_call(
        body, name=name,
        out_shape=(pltpu.SemaphoreType.DMA((ncopies,)), pltpu.SemaphoreType.DMA((ncopies,)),
                   *[pltpu.HBM(b.shape, b.dtype) for b in bufs], jax.ShapeDtypeStruct((SUBLANES, LANES), F32)),
        in_specs=[HBM] * nb + [ANY] * len(extra),
        out_specs=(SEM, SEM, *[HBM] * nb, pl.BlockSpec(memory_space=pltpu.VMEM)),
        input_output_aliases={i: 2 + i for i in range(nb)},
        compiler_params=pltpu.CompilerParams(has_side_effects=EFFECT),
    )(*[pltpu.with_memory_space_constraint(b, pltpu.HBM) for b in bufs], *extra)
    return outs[0], outs[1], list(outs[2:2 + nb]), outs[-1]


def _split_wait(started, plan, after, *, name):
    send, recv, bufs, _ = started
    nb = len(bufs)

    def body(*refs):
        send_sem, recv_sem = refs[nb], refs[nb + 1]
        for k, (src, dst, to) in enumerate(plan(refs[:nb])):
            cp = _remote(src, dst, send_sem, recv_sem, k, to)
            cp.wait_send()
            cp.wait_recv()

    outs = pl.pallas_call(
        body, name=name,
        out_shape=tuple(pltpu.HBM(b.shape, b.dtype) for b in bufs),
        in_specs=[HBM] * nb + [SEM, SEM, ANY], out_specs=tuple([HBM] * nb),
        input_output_aliases={i: i for i in range(nb)},
        compiler_params=pltpu.CompilerParams(has_side_effects=EFFECT),
    )(*bufs, send, recv, after)
    return list(outs)


def _gather_chip_plan(shapes, by_cols):
    n = len(shapes)

    def plan(refs):
        srcs, lands = refs[:n], refs[n:]
        x, y, c, chips = _place()
        out = []
        for w in range(n):
            h = shapes[w][0] // 2
            mine = _owner_rows(lands[w], shapes[w], by_cols[w], 2 * x + y, c * h, h)
            for tx, ty in chips:
                out.append((srcs[w].at[pl.ds(c * h, h), :], mine, (tx, ty, c)))
        return out

    return plan


def _gather_pair_plan(shapes, by_cols):
    n = len(shapes)

    def plan(refs):
        srcs, lands = refs[:n], refs[n:]
        x, y, c, chips = _place()
        out = []
        for w in range(n):
            h = shapes[w][0] // 2
            for tx, ty in chips:
                half = _owner_rows(lands[w], shapes[w], by_cols[w], 2 * tx + ty, c * h, h)
                out.append((half, half, (x, y, 1 - c)))
            own = _owner_rows(lands[w], shapes[w], by_cols[w], 2 * x + y, 0, shapes[w][0])
            out.append((srcs[w], own, (x, y, 1 - c)))
        return out

    return plan


def _reduce_pair_plan(shapes):
    n = len(shapes)

    def plan(refs):
        local, lands = refs[:n], refs[n:]
        x, y, c, _ = _place()
        out = []
        for w in range(n):
            h = shapes[w][1] // 2
            out.append((local[w].at[:, pl.ds((1 - c) * h, h), :], lands[w], (x, y, 1 - c)))
        return out

    return plan


def _reduce_chip_plan(n):
    def plan(refs):
        parts, lands = refs[:n], refs[n:]
        x, y, c, chips = _place()
        return [(parts[w].at[2 * tx + ty], lands[w].at[j], (tx, ty, c))
                for w in range(n) for j, (tx, ty) in enumerate(chips)]

    return plan


def _as_operands(gathered):
    out = {}
    for n, g in gathered.items():
        if n in COL_SHARDED:
            out[n] = g
        elif n == "w_in":
            out[n] = jnp.transpose(g, (1, 0, 2)).reshape(D_MODEL, IN_COLS)
        else:
            out[n] = g.reshape(g.shape[0] * g.shape[1], g.shape[2])
    return out


def _by_owner(n, g):
    if n == "w_in":
        return jnp.transpose(g.reshape(D_MODEL, N_CHIPS, IN_COLS // N_CHIPS), (1, 0, 2))
    if g.ndim == 2:
        return g.reshape(N_CHIPS, g.shape[0] // N_CHIPS, g.shape[1])
    return g


class _Comm:
    def __init__(self, shards):
        self.shards = shards
        self.total = {}
        self._flight = {}

    def _layout(self, names):
        return [self.shards[n].shape for n in names], [n in COL_SHARDED for n in names]

    def gather_now(self, tag, names):
        _, by_cols = self._layout(names)
        got = _allgather_weights([self.shards[n] for n in names], by_cols, name=f"gather_{tag}")
        return _as_operands(dict(zip(names, got)))

    def gather_start(self, tag, names, after):
        shapes, by_cols = self._layout(names)
        srcs = [self.shards[n] for n in names]
        lands = [lax.empty(_gathered_shape(s.shape, bc), s.dtype) for s, bc in zip(srcs, by_cols)]
        started = _split_start(srcs + lands, _gather_chip_plan(shapes, by_cols), 3 * len(srcs),
                               after=after, name=f"gather_{tag}_chips_start")
        self._flight[tag] = (names, started)
        return started[3]

    def gather_landed(self, tag, after):
        names, started = self._flight[tag]
        shapes, by_cols = self._layout(names)
        bufs = _split_wait(started, _gather_chip_plan(shapes, by_cols), after, name=f"gather_{tag}_chips_wait")
        started = _split_start(bufs, _gather_pair_plan(shapes, by_cols), 4 * len(names),
                               name=f"gather_{tag}_pair_start")
        self._flight[tag] = (names, started)
        return started[3]

    def gather_ready(self, tag, after):
        names, started = self._flight.pop(tag)
        shapes, by_cols = self._layout(names)
        bufs = _split_wait(started, _gather_pair_plan(shapes, by_cols), after, name=f"gather_{tag}_pair_wait")
        return _as_operands(dict(zip(names, bufs[len(names):])))

    def reduce_start(self, tag, grads):
        names = list(grads)
        local = [_by_owner(n, grads[n]) for n in names]
        return self._chip_start(tag, names, local, _pair_exchange(local, name=f"pair_exchange_{tag}"))

    def reduce_pair_start(self, tag, grads):
        names = list(grads)
        local = [_by_owner(n, grads[n]) for n in names]
        lands = [lax.empty((N_CHIPS, g.shape[1] // 2, g.shape[2]), g.dtype) for g in local]
        started = _split_start(local + lands, _reduce_pair_plan([g.shape for g in local]), len(names),
                               name=f"pair_exchange_{tag}_start")
        self._flight[tag] = (names, started)
        return started[3]

    def reduce_pair_done(self, tag, after):
        names, started = self._flight.pop(tag)
        n = len(names)
        bufs = _split_wait(started, _reduce_pair_plan([b.shape for b in started[2][:n]]), after,
                           name=f"pair_exchange_{tag}_wait")
        return self._chip_start(tag, names, bufs[:n], bufs[n:])

    def _chip_start(self, tag, names, local, from_sib):
        parts = [_pair_sum(g, s, name=f"pair_sum_{n}") for n, g, s in zip(names, local, from_sib)]
        lands = [lax.empty((N_CHIPS - 1,) + p.shape[1:], p.dtype) for p in parts]
        self._flight[tag] = (names, _split_start(parts + lands, _reduce_chip_plan(len(names)), 3 * len(names),
                                                 name=f"chip_exchange_{tag}_start"))
        return self._flight[tag][1][3]

    def reduce_finish(self, tag, after):
        names, started = self._flight.pop(tag)
        n = len(names)
        bufs = _split_wait(started, _reduce_chip_plan(n), after, name=f"chip_exchange_{tag}_wait")
        totals = [_chip_sum(p, s, name=f"chip_sum_{nm}") for nm, p, s in zip(names, bufs[:n], bufs[n:])]
        self.total.update(zip(names, _pair_gather(totals, name=f"pair_gather_{tag}")))


def _adamw(w, g, m, v, *, name):
    R, C = w.shape
    tr = _row_block(R, 8 * C * 4, SUBLANES)

    def body(w_ref, g_ref, m_ref, v_ref, go_ref, d_ref, nm_ref, nv_ref):
        gg = g_ref[...]
        go_ref[...] = gg
        m_new = ADAM_B1 * m_ref[...] + (1.0 - ADAM_B1) * gg
        v_new = ADAM_B2 * v_ref[...] + (1.0 - ADAM_B2) * (gg * gg)
        m_hat = m_new / (1.0 - ADAM_B1 ** ADAM_STEP)
        v_hat = v_new / (1.0 - ADAM_B2 ** ADAM_STEP)
        d_ref[...] = -ADAM_LR * (m_hat / (jnp.sqrt(v_hat) + ADAM_EPS) + ADAM_WD * w_ref[...])
        nm_ref[...] = m_new
        nv_ref[...] = v_new

    blk = pl.BlockSpec((tr, C), lambda i: (i, 0))
    shp = jax.ShapeDtypeStruct((R, C), F32)
    return pl.pallas_call(
        body, name=name, grid=(R // tr,), in_specs=[blk] * 4, out_specs=[blk] * 4, out_shape=[shp] * 4,
        compiler_params=_params(("parallel",)),
    )(w, g, m, v)


def _to2d(a):
    flat = a.reshape(-1)
    pad = (-flat.shape[0]) % (SUBLANES * LANES)
    if pad:
        flat = jnp.pad(flat, (0, pad))
    return flat.reshape(-1, LANES)


def _small_rows(shape):
    return -(-math.prod(shape) // (SUBLANES * LANES)) * SUBLANES


def _pack_small(parts):
    rows = jnp.concatenate([_to2d(p) for p in parts], axis=0)
    pad = (-rows.shape[0]) % 256
    if pad:
        rows = jnp.concatenate([rows, jnp.zeros((pad, LANES), rows.dtype)], axis=0)
    return rows


def _unpack_small(rows, shapes):
    out, r = [], 0
    for shp in shapes:
        size = math.prod(shp)
        nrow = _small_rows(shp)
        out.append(rows[r:r + nrow].reshape(-1)[:size].reshape(shp))
        r += nrow
    return out


def kernel(x, mem, g_ffn1, w1_gate, w1_up, w1_down, g_mix, w_in, g_v, w_s, b_s, sinks, g_a_out, g_b_out, w_out, g_x, g_mem, w_xq, w_xkv, w_xo, g_ffn2, w2_gate, w2_up, w2_down, g_final, loss_target, m_g_ffn1, m_w1_gate, m_w1_up, m_w1_down, m_g_mix, m_w_in, m_g_v, m_w_s, m_b_s, m_sinks, m_g_a_out, m_g_b_out, m_w_out, m_g_x, m_g_mem, m_w_xq, m_w_xkv, m_w_xo, m_g_ffn2, m_w2_gate, m_w2_up, m_w2_down, m_g_final, v_g_ffn1, v_w1_gate, v_w1_up, v_w1_down, v_g_mix, v_w_in, v_g_v, v_w_s, v_b_s, v_sinks, v_g_a_out, v_g_b_out, v_w_out, v_g_x, v_g_mem, v_w_xq, v_w_xkv, v_w_xo, v_g_ffn2, v_w2_gate, v_w2_up, v_w2_down, v_g_final):
    args = dict(locals())
    Wp = {n: args[n] for n in ALL_W}
    Mp = {n: args["m_" + n] for n in ALL_W}
    Vp = {n: args["v_" + n] for n in ALL_W}

    comm = _Comm({n: Wp[n][0].astype(BF16) for n in BIG})
    W = {n: Wp[n] for n in SMALL}
    W["g_final"] = Wp["g_final"].reshape(1, D_MODEL)
    for n in ("w_s", "b_s"):
        W[n] = Wp[n][0]
    loss, dx, grads = _local_step(x[0], mem[0], loss_target[0], W, comm)
    big_grad = comm.total

    small_shapes = [Wp[n].shape for n in SMALL]
    packed = _pack_small([grads[n].reshape(Wp[n].shape) for n in SMALL] + [loss])
    summed = _allreduce_small(packed, name="allreduce_small")
    small_grad = dict(zip(SMALL, _unpack_small(summed, small_shapes)))
    nrows = sum(_small_rows(s) for s in small_shapes)
    loss_total = summed[nrows, 0]

    grad_out, delta, new_m, new_v = {}, {}, {}, {}
    for n in BIG:
        shp = Wp[n].shape
        g, d, nm, nv = _adamw(Wp[n][0], big_grad[n], Mp[n][0], Vp[n][0], name=f"adamw_{n}")
        grad_out[n], delta[n], new_m[n], new_v[n] = g.reshape(shp), d.reshape(shp), nm.reshape(shp), nv.reshape(shp)
    sw = _pack_small([Wp[n] for n in SMALL])
    sg = _pack_small([small_grad[n] for n in SMALL])
    sm = _pack_small([Mp[n] for n in SMALL])
    sv = _pack_small([Vp[n] for n in SMALL])
    _, d, nm, nv = _adamw(sw, sg, sm, sv, name="adamw_small")
    for n, dd, mm_, vv_ in zip(SMALL, _unpack_small(d, small_shapes), _unpack_small(nm, small_shapes),
                               _unpack_small(nv, small_shapes)):
        grad_out[n], delta[n], new_m[n], new_v[n] = small_grad[n], dd, mm_, vv_

    return (loss_total, dx[None], *[grad_out[n] for n in ALL_W], *[delta[n] for n in ALL_W],
            *[new_m[n] for n in ALL_W], *[new_v[n] for n in ALL_W])
```

```python
import functools
import math

import jax
import jax.numpy as jnp
from jax import lax
from jax.experimental import pallas as pl
from jax.experimental.pallas import tpu as pltpu

F32 = jnp.float32
BF16 = jnp.bfloat16
MESH = pl.DeviceIdType.MESH

D_MODEL = 2048
D_FF = 5632
D_A = 1024
D_B = 1024
CHUNK = 128
A_GROUPS = 8
HEAD_DIM = 64
B_Q_HEADS = 16
B_KV_HEADS = 2
X_HEADS = 4
X_HEAD_DIM = 512
IN_COLS = 3328
O_Q = 2 * D_A
O_K = O_Q + D_B
O_V = O_K + B_KV_HEADS * HEAD_DIM
N_CHIPS = 4
EPS = 1e-5
NEG = -1e30
ADAM_LR = 0.001
ADAM_B1 = 0.9
ADAM_B2 = 0.999
ADAM_EPS = 1e-08
ADAM_WD = 0.01
ADAM_STEP = 10

V7X_VMEM_BYTES = 64 * 1024 * 1024
VMEM_LIMIT = 56 * 1024 * 1024
LANES = 128
SUBLANES = 8


ANY = pl.BlockSpec(memory_space=pl.ANY)


def _params(sem, vmem=VMEM_LIMIT):
    return pltpu.CompilerParams(dimension_semantics=sem, vmem_limit_bytes=vmem)


def _matmul(pairs, *, M, N, K, tm, tn, tk, a_t=False, b_kind="n", out_kind="n", out_dtype=F32,
            scale=1.0, res=None, norm_g=None, order="ij", dep=None, name):
    tm, tn, tk = min(tm, M), min(tn, N), min(tk, K)
    assert M % tm == 0 and N % tn == 0 and K % tk == 0, (name, M, N, K, tm, tn, tk)
    nk = K // tk
    npairs = len(pairs)
    b_t = b_kind in ("t", "st")

    def ij(g0, g1):
        return (g0, g1) if order == "ij" else (g1, g0)

    def a_map(g0, g1, k):
        i, _ = ij(g0, g1)
        return (k, i) if a_t else (i, k)

    a_spec = pl.BlockSpec((tk, tm) if a_t else (tm, tk), a_map)

    b0 = pairs[0][1]
    if b_kind == "n":
        b_spec = pl.BlockSpec((tk, tn), lambda g0, g1, k: (k, ij(g0, g1)[1]))
    elif b_kind == "t":
        b_spec = pl.BlockSpec((tn, tk), lambda g0, g1, k: (ij(g0, g1)[1], k))
    elif b_kind == "sn":
        ns = b0.shape[2]
        assert ns % tn == 0
        nps = ns // tn
        b_spec = pl.BlockSpec((None, tk, tn), lambda g0, g1, k: (ij(g0, g1)[1] // nps, k, ij(g0, g1)[1] % nps))
    else:
        ks = b0.shape[2]
        assert ks % tk == 0
        kps = ks // tk
        b_spec = pl.BlockSpec((None, tn, tk), lambda g0, g1, k: (k // kps, ij(g0, g1)[1], k % kps))

    if out_kind == "n":
        o_spec = pl.BlockSpec((tm, tn), lambda g0, g1, k: ij(g0, g1))
        o_shape = jax.ShapeDtypeStruct((M, N), out_dtype)
    else:
        ns = N // N_CHIPS
        assert ns % tn == 0
        nps_o = ns // tn
        o_spec = pl.BlockSpec((None, tm, tn), lambda g0, g1, k: (ij(g0, g1)[1] // nps_o, ij(g0, g1)[0], ij(g0, g1)[1] % nps_o))
        o_shape = jax.ShapeDtypeStruct((N_CHIPS, M, ns), out_dtype)

    in_specs, args = [], []
    for a, b in pairs:
        in_specs += [a_spec, b_spec]
        args += [a, b]
    if res is not None:
        in_specs.append(pl.BlockSpec((tm, tn), lambda g0, g1, k: ij(g0, g1)))
        args.append(res)
    if norm_g is not None:
        assert tn == N and out_kind == "n"
        in_specs.append(pl.BlockSpec((1, N), lambda g0, g1, k: (0, 0)))
        args.append(norm_g)
    if dep is not None:
        in_specs.append(ANY)
        args.append(dep)

    dn = (((0,) if a_t else (1,), (1,) if b_t else (0,)), ((), ()))

    def body(*refs):
        pos = 2 * npairs
        res_ref = refs[pos] if res is not None else None
        pos += res is not None
        g_ref = refs[pos] if norm_g is not None else None
        pos += (norm_g is not None) + (dep is not None)
        o_ref = refs[pos]
        n_ref = refs[pos + 1] if norm_g is not None else None
        acc_ref = refs[-1] if nk > 1 else None
        part = None
        for p in range(npairs):
            d = lax.dot_general(refs[2 * p][...], refs[2 * p + 1][...], dn, preferred_element_type=F32)
            part = d if part is None else part + d

        def finish(acc):
            r = acc * scale if scale != 1.0 else acc
            if res_ref is not None:
                r = res_ref[...] + r
            o_ref[...] = r.astype(out_dtype)
            if n_ref is not None:
                n_ref[...] = (r * _rstd(r) * g_ref[...]).astype(BF16)

        if nk == 1:
            finish(part)
        else:
            k = pl.program_id(2)

            @pl.when(k == 0)
            def _():
                acc_ref[...] = part

            @pl.when((k > 0) & (k < nk - 1))
            def _():
                acc_ref[...] += part

            @pl.when(k == nk - 1)
            def _():
                finish(acc_ref[...] + part)

    grid = (M // tm, N // tn, nk) if order == "ij" else (N // tn, M // tm, nk)
    out_specs, out_shape = o_spec, o_shape
    if norm_g is not None:
        out_specs = [o_spec, pl.BlockSpec((tm, tn), lambda g0, g1, k: ij(g0, g1))]
        out_shape = [o_shape, jax.ShapeDtypeStruct((M, N), BF16)]
    return pl.pallas_call(
        body, name=name, grid=grid, in_specs=in_specs, out_specs=out_specs, out_shape=out_shape,
        scratch_shapes=[pltpu.VMEM((tm, tn), F32)] if nk > 1 else [],
        compiler_params=_params(("parallel", "parallel", "arbitrary")),
    )(*args)


def _rstd(x):
    return lax.rsqrt(jnp.mean(x * x, axis=-1, keepdims=True) + EPS)


def _rms_bwd_math(x, g, dy):
    r = _rstd(x)
    gy = dy * g
    xr = x * r
    dx = r * (gy - xr * jnp.mean(gy * xr, axis=-1, keepdims=True))
    return dx, dy * xr


def _rms_fwd(h, g, *, name, tm=512, dep=None):
    T, Dm = h.shape
    tm = min(tm, T)

    def body(h_ref, g_ref, *rest):
        x = h_ref[...]
        rest[-1][...] = (x * _rstd(x) * g_ref[...]).astype(BF16)

    return pl.pallas_call(
        body, name=name, grid=(T // tm,),
        in_specs=[pl.BlockSpec((tm, Dm), lambda i: (i, 0)), pl.BlockSpec((1, Dm), lambda i: (0, 0))]
        + ([ANY] if dep is not None else []),
        out_specs=pl.BlockSpec((tm, Dm), lambda i: (i, 0)),
        out_shape=jax.ShapeDtypeStruct((T, Dm), BF16),
        compiler_params=_params(("parallel",)),
    )(h, g, *([dep] if dep is not None else []))


def _rms_bwd(h, g, dn, dres, *, name, tm=256, dep=None):
    T, Dm = h.shape
    tm = min(tm, T)
    has_res = dres is not None

    def body(*refs):
        h_ref, g_ref, dn_ref = refs[:3]
        pos = 3
        dres_ref = refs[pos] if has_res else None
        pos += has_res + (dep is not None)
        dh_ref, dhb_ref, dg_ref = refs[pos:pos + 3]
        dx, dgr = _rms_bwd_math(h_ref[...], g_ref[...], dn_ref[...].astype(F32))
        if has_res:
            dx = dres_ref[...] + dx
        dh_ref[...] = dx
        dhb_ref[...] = dx.astype(BF16)
        part = jnp.sum(dgr, axis=0, keepdims=True)

        @pl.when(pl.program_id(0) == 0)
        def _():
            dg_ref[...] = part

        @pl.when(pl.program_id(0) > 0)
        def _():
            dg_ref[...] += part

    row = pl.BlockSpec((tm, Dm), lambda i: (i, 0))
    vec = pl.BlockSpec((1, Dm), lambda i: (0, 0))
    args = [h, g, dn] + ([dres] if has_res else []) + ([dep] if dep is not None else [])
    return pl.pallas_call(
        body, name=name, grid=(T // tm,),
        in_specs=[row, vec, row] + ([row] if has_res else []) + ([ANY] if dep is not None else []),
        out_specs=[row, row, vec],
        out_shape=[jax.ShapeDtypeStruct((T, Dm), F32), jax.ShapeDtypeStruct((T, Dm), BF16),
                   jax.ShapeDtypeStruct((1, Dm), F32)],
        compiler_params=_params(("arbitrary",)),
    )(*args)


def _loss_head(h, g, tgt, *, name, tm=256):
    T, Dm = h.shape
    tm = min(tm, T)

    def body(h_ref, g_ref, t_ref, dh_ref, dhb_ref, dg_ref, loss_ref):
        x = h_ref[...]
        gv = g_ref[...]
        r = _rstd(x)
        diff = x * r * gv - t_ref[...]
        lpart = 0.5 * jnp.sum(jnp.mean(diff * diff, axis=-1, keepdims=True), axis=0, keepdims=True)
        dx, dgr = _rms_bwd_math(x, gv, diff * (1.0 / Dm))
        dh_ref[...] = dx
        dhb_ref[...] = dx.astype(BF16)
        part = jnp.sum(dgr, axis=0, keepdims=True)
        lrow = jnp.broadcast_to(lpart, (1, LANES))

        @pl.when(pl.program_id(0) == 0)
        def _():
            dg_ref[...] = part
            loss_ref[...] = lrow

        @pl.when(pl.program_id(0) > 0)
        def _():
            dg_ref[...] += part
            loss_ref[...] += lrow

    row = pl.BlockSpec((tm, Dm), lambda i: (i, 0))
    vec = pl.BlockSpec((1, Dm), lambda i: (0, 0))
    return pl.pallas_call(
        body, name=name, grid=(T // tm,),
        in_specs=[row, vec, row],
        out_specs=[row, row, vec, pl.BlockSpec((1, LANES), lambda i: (0, 0))],
        out_shape=[jax.ShapeDtypeStruct((T, Dm), F32), jax.ShapeDtypeStruct((T, Dm), BF16),
                   jax.ShapeDtypeStruct((1, Dm), F32), jax.ShapeDtypeStruct((1, LANES), F32)],
        compiler_params=_params(("arbitrary",)),
    )(h, g, tgt)


MXU_COLS = 256
FF_TILE = 2 * MXU_COLS


def _row_block(rows, row_bytes, align, budget=24 * 1024 * 1024):
    fits = [d for d in range(align, rows + 1, align) if rows % d == 0 and 2 * d * row_bytes <= budget]
    assert fits, (rows, row_bytes)
    return fits[-1]


def _swiglu_up(n, wg, wu, *, name, tm=1024, tn=FF_TILE):
    T, Dm = n.shape
    Fd = wg.shape[1]
    tm = min(tm, T)

    def body(n_ref, wg_ref, wu_ref, g_ref, u_ref, a_ref):
        x = n_ref[...]
        g = jnp.dot(x, wg_ref[...], preferred_element_type=F32)
        u = jnp.dot(x, wu_ref[...], preferred_element_type=F32)
        g_ref[...] = g.astype(BF16)
        u_ref[...] = u.astype(BF16)
        a_ref[...] = (g * jax.nn.sigmoid(g) * u).astype(BF16)

    wspec = pl.BlockSpec((Dm, tn), lambda j, i: (0, j))
    ospec = pl.BlockSpec((tm, tn), lambda j, i: (i, j))
    oshape = jax.ShapeDtypeStruct((T, Fd), BF16)
    return pl.pallas_call(
        body, name=name, grid=(Fd // tn, T // tm),
        in_specs=[pl.BlockSpec((tm, Dm), lambda j, i: (i, 0)), wspec, wspec],
        out_specs=[ospec, ospec, ospec], out_shape=[oshape, oshape, oshape],
        compiler_params=_params(("parallel", "parallel")),
    )(n, wg, wu)


def _swiglu_bwd_act(dhb, wd, G, U, *, name, tm=1024, tn=D_FF // N_CHIPS):
    T, Dm = dhb.shape
    Fd = wd.shape[0]
    tm, tn = min(tm, T), min(tn, Fd)

    def body(dh_ref, wd_ref, g_ref, u_ref, dg_ref, du_ref):
        da = 0.5 * lax.dot_general(dh_ref[...], wd_ref[...], (((1,), (1,)), ((), ())), preferred_element_type=F32)
        g = g_ref[...].astype(F32)
        u = u_ref[...].astype(F32)
        sg = jax.nn.sigmoid(g)
        dg_ref[...] = (da * u * (sg * (1.0 + g * (1.0 - sg)))).astype(BF16)
        du_ref[...] = (da * (g * sg)).astype(BF16)

    blk = pl.BlockSpec((tm, tn), lambda j, i: (i, j))
    oshape = jax.ShapeDtypeStruct((T, Fd), BF16)
    return pl.pallas_call(
        body, name=name, grid=(Fd // tn, T // tm),
        in_specs=[pl.BlockSpec((tm, Dm), lambda j, i: (i, 0)), pl.BlockSpec((tn, Dm), lambda j, i: (j, 0)), blk, blk],
        out_specs=[blk, blk], out_shape=[oshape, oshape],
        compiler_params=_params(("parallel", "parallel")),
    )(dhb, wd, G, U)


_INV_SQRT2 = 0.7071067811865476
_INV_SQRT2PI = 0.3989422804014327


def _erf(x):
    ax = jnp.abs(x)
    t = 1.0 / (1.0 + 0.3275911 * ax)
    poly = t * (0.254829592 + t * (-0.284496736 + t * (1.421413741 + t * (-1.453152027 + t * 1.061405429))))
    y = 1.0 - poly * jnp.exp(-ax * ax)
    return jnp.where(x < 0, -y, y)


def _gelu_cdf(x):
    return 0.5 * (1.0 + _erf(x * _INV_SQRT2))


def _lane_lt64(shape):
    return lax.broadcasted_iota(jnp.int32, shape, len(shape) - 1) < HEAD_DIM


def _dup_half(x, kv):
    rolled = pltpu.roll(x, HEAD_DIM, 1)
    lo = _lane_lt64(x.shape)
    return jnp.where(lo, x, rolled) if kv == 0 else jnp.where(lo, rolled, x)


HEADS_PER_KV = B_Q_HEADS // B_KV_HEADS
PAIRS = HEADS_PER_KV // 2


def _attn_bias():
    shape = (HEADS_PER_KV * CHUNK, 2 * CHUNK)
    qpos = (lax.broadcasted_iota(jnp.int32, shape, 0) & (CHUNK - 1)) + CHUNK
    kpos = lax.broadcasted_iota(jnp.int32, shape, 1)
    diff = qpos - kpos
    band = (diff >= 0) & (diff < CHUNK)
    return jnp.stack([jnp.where(band & (kpos >= CHUNK), 0.0, NEG), jnp.where(band, 0.0, NEG)]).astype(F32)


def _stack_heads(tiles, lo):
    parts = []
    for t in tiles:
        parts += [jnp.where(lo, t, 0.0), jnp.where(lo, 0.0, t)]
    return jnp.concatenate(parts, axis=0)


def _unstack_heads(s, lo):
    return [jnp.where(lo, s[2 * p * CHUNK:(2 * p + 1) * CHUNK], s[(2 * p + 1) * CHUNK:(2 * p + 2) * CHUNK])
            for p in range(PAIRS)]


def _stack_sinks(sk_ref, kv):
    return jnp.concatenate([jnp.broadcast_to(sk_ref[:, h:h + 1], (CHUNK, 1))
                            for h in range(kv * HEADS_PER_KV, (kv + 1) * HEADS_PER_KV)], axis=0)


def _sgu_forward(z_ref, gv, wsm, bst):
    zu = z_ref[:, 0:D_A]
    zv = z_ref[:, D_A:2 * D_A]
    u = zu * _gelu_cdf(zu)
    v = zv * _gelu_cdf(zv)
    rv = _rstd(v)
    vn = (v * rv * gv).astype(BF16)
    svs = []
    for g in range(A_GROUPS):
        sl = slice(g * CHUNK, (g + 1) * CHUNK)
        svs.append(jnp.dot(wsm[g], vn[:, sl], preferred_element_type=F32) + bst[:, g:g + 1])
    sv = jnp.concatenate(svs, axis=1)
    return zu, zv, u, v, rv, vn, sv


def _masked_ws(ws_ref):
    tril = lax.broadcasted_iota(jnp.int32, (CHUNK, CHUNK), 0) >= lax.broadcasted_iota(jnp.int32, (CHUNK, CHUNK), 1)
    return [jnp.where(tril, ws_ref[g], 0.0).astype(BF16) for g in range(A_GROUPS)], tril


def _attn_probs(qm, kkd, sink, bias):
    s = lax.dot_general(qm, kkd, (((1,), (1,)), ((), ())), preferred_element_type=F32) * (HEAD_DIM ** -0.5) + bias
    m = jnp.maximum(jnp.max(s, axis=-1, keepdims=True), sink)
    e = jnp.exp(s - m)
    es = jnp.exp(sink - m)
    inv = 1.0 / (jnp.sum(e, axis=-1, keepdims=True) + es)
    return e * inv, es * inv


def _mixer_fwd(z, gv, ws, bst, sinks, ga, gb, *, name):
    T = z.shape[0]
    nb = T // CHUNK
    kvb = O_K // (2 * CHUNK)

    def body(z_ref, zp_ref, bias_ref, gv_ref, ws_ref, bst_ref, sk_ref, ga_ref, gb_ref, o_ref):
        wsm, _ = _masked_ws(ws_ref)
        _, _, u, _, _, _, sv = _sgu_forward(z_ref, gv_ref[...], wsm, bst_ref[...])
        ya = u * sv
        o_ref[:, 0:D_A] = (ya * _rstd(ya) * ga_ref[...]).astype(BF16)

        mask = bias_ref[...]
        kk = jnp.concatenate([zp_ref[:, 0:CHUNK], z_ref[:, O_K:O_V]], axis=0)
        vv = jnp.concatenate([zp_ref[:, CHUNK:2 * CHUNK], z_ref[:, O_V:IN_COLS]], axis=0)
        lo = _lane_lt64((CHUNK, LANES))
        outs = []
        for kv in range(B_KV_HEADS):
            kkd = _dup_half(kk, kv).astype(BF16)
            vvd = _dup_half(vv, kv).astype(BF16)
            q = _stack_heads([z_ref[:, O_Q + (kv * PAIRS + pr) * LANES:O_Q + (kv * PAIRS + pr + 1) * LANES]
                              for pr in range(PAIRS)], lo).astype(BF16)
            p, _ = _attn_probs(q, kkd, _stack_sinks(sk_ref, kv), mask)
            outs += _unstack_heads(jnp.dot(p.astype(BF16), vvd, preferred_element_type=F32), lo)
        yb = jnp.concatenate(outs, axis=1)
        o_ref[:, D_A:D_A + D_B] = (yb * _rstd(yb) * gb_ref[...]).astype(BF16)

    full = lambda shape: pl.BlockSpec(shape, lambda i: (0,) * len(shape))
    return pl.pallas_call(
        body, name=name, grid=(nb,),
        in_specs=[pl.BlockSpec((CHUNK, IN_COLS), lambda i: (i, 0)),
                  pl.BlockSpec((CHUNK, 2 * CHUNK), lambda i: (jnp.maximum(i - 1, 0), kvb)),
                  pl.BlockSpec((None, HEADS_PER_KV * CHUNK, 2 * CHUNK), lambda i: (jnp.minimum(i, 1), 0, 0)),
                  full((1, D_A)), full((A_GROUPS, CHUNK, CHUNK)), full((CHUNK, A_GROUPS)), full((1, B_Q_HEADS)),
                  full((1, D_A)), full((1, D_B))],
        out_specs=pl.BlockSpec((CHUNK, D_A + D_B), lambda i: (i, 0)),
        out_shape=jax.ShapeDtypeStruct((T, D_A + D_B), BF16),
        compiler_params=_params(("parallel",)),
    )(z, z, _attn_bias(), gv, ws, bst, sinks, ga, gb)


def _mixer_bwd(z, dyn, gv, ws, bst, sinks, ga, gb, *, name):
    T = z.shape[0]
    nb = T // CHUNK
    kvb = O_K // (2 * CHUNK)
    NT = (((0,), (0,)), ((), ()))

    def body(z_ref, zp_ref, dy_ref, bias_ref, gv_ref, ws_ref, bst_ref, sk_ref, ga_ref, gb_ref,
             dz_ref, dgv_ref, dws_ref, dbst_ref, dsk_ref, dga_ref, dgb_ref, carry_ref, p_ref):
        step = pl.program_id(0)

        @pl.when(step == 0)
        def _():
            carry_ref[...] = jnp.zeros_like(carry_ref)
            dgv_ref[...] = jnp.zeros_like(dgv_ref)
            dws_ref[...] = jnp.zeros_like(dws_ref)
            dbst_ref[...] = jnp.zeros_like(dbst_ref)
            dsk_ref[...] = jnp.zeros_like(dsk_ref)
            dga_ref[...] = jnp.zeros_like(dga_ref)
            dgb_ref[...] = jnp.zeros_like(dgb_ref)

        wsm, tril = _masked_ws(ws_ref)
        gvv = gv_ref[...]
        zu, zv, u, v, rv, vn, sv = _sgu_forward(z_ref, gvv, wsm, bst_ref[...])
        ya = u * sv
        dya, dga_rows = _rms_bwd_math(ya, ga_ref[...], dy_ref[:, 0:D_A].astype(F32))
        dga_ref[...] += jnp.sum(dga_rows, axis=0, keepdims=True)
        du = dya * sv
        dsv = dya * u
        dvn_parts = []
        for g in range(A_GROUPS):
            sl = slice(g * CHUNK, (g + 1) * CHUNK)
            dsv_g = dsv[:, sl]
            dsv_gb = dsv_g.astype(BF16)
            dw = lax.dot_general(dsv_gb, vn[:, sl], (((1,), (1,)), ((), ())), preferred_element_type=F32)
            dws_ref[g] += jnp.where(tril, dw, 0.0)
            dbst_ref[:, g:g + 1] += jnp.sum(dsv_g, axis=1, keepdims=True)
            dvn_parts.append(lax.dot_general(wsm[g], dsv_gb, NT, preferred_element_type=F32))
        dvn = jnp.concatenate(dvn_parts, axis=1)
        dv, dgv_rows = _rms_bwd_math(v, gvv, dvn)
        dgv_ref[...] += jnp.sum(dgv_rows, axis=0, keepdims=True)
        dz_ref[:, 0:D_A] = (du * (_gelu_cdf(zu) + zu * jnp.exp(-0.5 * zu * zu) * _INV_SQRT2PI)).astype(BF16)
        dz_ref[:, D_A:2 * D_A] = (dv * (_gelu_cdf(zv) + zv * jnp.exp(-0.5 * zv * zv) * _INV_SQRT2PI)).astype(BF16)

        mask = bias_ref[...]
        kk = jnp.concatenate([zp_ref[:, 0:CHUNK], z_ref[:, O_K:O_V]], axis=0)
        vv = jnp.concatenate([zp_ref[:, CHUNK:2 * CHUNK], z_ref[:, O_V:IN_COLS]], axis=0)
        lo = _lane_lt64((CHUNK, LANES))
        kkd = [_dup_half(kk, kv).astype(BF16) for kv in range(B_KV_HEADS)]
        vvd = [_dup_half(vv, kv).astype(BF16) for kv in range(B_KV_HEADS)]
        outs, qs, psinks = [], [], []
        for kv in range(B_KV_HEADS):
            qs.append(_stack_heads([z_ref[:, O_Q + (kv * PAIRS + pr) * LANES:O_Q + (kv * PAIRS + pr + 1) * LANES]
                                    for pr in range(PAIRS)], lo).astype(BF16))
            p, ps = _attn_probs(qs[kv], kkd[kv], _stack_sinks(sk_ref, kv), mask)
            p_ref[kv] = p
            psinks.append(ps)
            outs += _unstack_heads(jnp.dot(p.astype(BF16), vvd[kv], preferred_element_type=F32), lo)
        yb = jnp.concatenate(outs, axis=1)
        dyb, dgb_rows = _rms_bwd_math(yb, gb_ref[...], dy_ref[:, D_A:D_A + D_B].astype(F32))
        dgb_ref[...] += jnp.sum(dgb_rows, axis=0, keepdims=True)

        dkk, dvv = [], []
        for kv in range(B_KV_HEADS):
            do = _stack_heads([dyb[:, (kv * PAIRS + pr) * LANES:(kv * PAIRS + pr + 1) * LANES]
                               for pr in range(PAIRS)], lo).astype(BF16)
            p = p_ref[kv]
            dvv.append(lax.dot_general(p.astype(BF16), do, NT, preferred_element_type=F32))
            dp = lax.dot_general(do, vvd[kv], (((1,), (1,)), ((), ())), preferred_element_type=F32)
            delta = jnp.sum(p * dp, axis=-1, keepdims=True)
            dsink = -psinks[kv] * delta
            for g in range(HEADS_PER_KV):
                h = kv * HEADS_PER_KV + g
                dsk_ref[:, h:h + 1] += jnp.sum(dsink[g * CHUNK:(g + 1) * CHUNK], axis=0, keepdims=True)
            ds = (p * (dp - delta) * (HEAD_DIM ** -0.5)).astype(BF16)
            dq = _unstack_heads(jnp.dot(ds, kkd[kv], preferred_element_type=F32), lo)
            for pr in range(PAIRS):
                c0 = O_Q + (kv * PAIRS + pr) * LANES
                dz_ref[:, c0:c0 + LANES] = dq[pr].astype(BF16)
            dkk.append(lax.dot_general(ds, qs[kv], NT, preferred_element_type=F32))

        def fold(parts):
            tot = [t + pltpu.roll(t, HEAD_DIM, 1) for t in parts]
            return jnp.where(_lane_lt64(tot[0].shape), tot[0], tot[1])

        dk_all = fold(dkk)
        dv_all = fold(dvv)
        dz_ref[:, O_K:O_V] = (dk_all[CHUNK:] + carry_ref[:, 0:CHUNK]).astype(BF16)
        dz_ref[:, O_V:IN_COLS] = (dv_all[CHUNK:] + carry_ref[:, CHUNK:2 * CHUNK]).astype(BF16)
        carry_ref[:, 0:CHUNK] = dk_all[:CHUNK]
        carry_ref[:, CHUNK:2 * CHUNK] = dv_all[:CHUNK]

    full = lambda shape: pl.BlockSpec(shape, lambda s: (0,) * len(shape))
    rev = lambda s: nb - 1 - s
    return pl.pallas_call(
        body, name=name, grid=(nb,),
        in_specs=[pl.BlockSpec((CHUNK, IN_COLS), lambda s: (rev(s), 0)),
                  pl.BlockSpec((CHUNK, 2 * CHUNK), lambda s: (jnp.maximum(rev(s) - 1, 0), kvb)),
                  pl.BlockSpec((CHUNK, D_A + D_B), lambda s: (rev(s), 0)),
                  pl.BlockSpec((None, HEADS_PER_KV * CHUNK, 2 * CHUNK), lambda s: (jnp.minimum(rev(s), 1), 0, 0)),
                  full((1, D_A)), full((A_GROUPS, CHUNK, CHUNK)), full((CHUNK, A_GROUPS)), full((1, B_Q_HEADS)),
                  full((1, D_A)), full((1, D_B))],
        out_specs=[pl.BlockSpec((CHUNK, IN_COLS), lambda s: (rev(s), 0)),
                   full((1, D_A)), full((A_GROUPS, CHUNK, CHUNK)), full((CHUNK, A_GROUPS)), full((1, B_Q_HEADS)),
                   full((1, D_A)), full((1, D_B))],
        out_shape=[jax.ShapeDtypeStruct((T, IN_COLS), BF16), jax.ShapeDtypeStruct((1, D_A), F32),
                   jax.ShapeDtypeStruct((A_GROUPS, CHUNK, CHUNK), F32), jax.ShapeDtypeStruct((CHUNK, A_GROUPS), F32),
                   jax.ShapeDtypeStruct((1, B_Q_HEADS), F32), jax.ShapeDtypeStruct((1, D_A), F32),
                   jax.ShapeDtypeStruct((1, D_B), F32)],
        scratch_shapes=[pltpu.VMEM((CHUNK, 2 * CHUNK), F32), pltpu.VMEM((B_KV_HEADS, HEADS_PER_KV * CHUNK, 2 * CHUNK), F32)],
        compiler_params=_params(("arbitrary",)),
    )(z, z, dyn, _attn_bias(), gv, ws, bst, sinks, ga, gb)


def _xattn_probs(qh, kh):
    s = lax.dot_general(qh, kh, (((1,), (1,)), ((), ())), preferred_element_type=F32) * (X_HEAD_DIM ** -0.5)
    e = jnp.exp(s - jnp.max(s, axis=-1, keepdims=True))
    return e / jnp.sum(e, axis=-1, keepdims=True)


def _xattn_fwd(q, kvm, *, name, tm=512):
    T = q.shape[0]
    Mm = kvm.shape[0]
    tm = min(tm, T)

    def body(q_ref, kv_ref, o_ref):
        for h in range(X_HEADS):
            sl = slice(h * X_HEAD_DIM, (h + 1) * X_HEAD_DIM)
            kh = kv_ref[:, sl].astype(BF16)
            vh = kv_ref[:, D_MODEL + h * X_HEAD_DIM:D_MODEL + (h + 1) * X_HEAD_DIM].astype(BF16)
            p = _xattn_probs(q_ref[:, sl], kh)
            o_ref[:, sl] = jnp.dot(p.astype(BF16), vh, preferred_element_type=F32).astype(BF16)

    return pl.pallas_call(
        body, name=name, grid=(T // tm,),
        in_specs=[pl.BlockSpec((tm, D_MODEL), lambda i: (i, 0)), pl.BlockSpec((Mm, 2 * D_MODEL), lambda i: (0, 0))],
        out_specs=pl.BlockSpec((tm, D_MODEL), lambda i: (i, 0)),
        out_shape=jax.ShapeDtypeStruct((T, D_MODEL), BF16),
        compiler_params=_params(("parallel",)),
    )(q, kvm)


def _xattn_bwd(q, kvm, do, *, name, tm=512):
    T = q.shape[0]
    Mm = kvm.shape[0]
    tm = min(tm, T)
    NT = (((0,), (0,)), ((), ()))

    def body(q_ref, kv_ref, do_ref, dq_ref, dkv_ref):
        @pl.when(pl.program_id(0) == 0)
        def _():
            dkv_ref[...] = jnp.zeros_like(dkv_ref)

        for h in range(X_HEADS):
            sl = slice(h * X_HEAD_DIM, (h + 1) * X_HEAD_DIM)
            slv = slice(D_MODEL + h * X_HEAD_DIM, D_MODEL + (h + 1) * X_HEAD_DIM)
            kh = kv_ref[:, sl].astype(BF16)
            vh = kv_ref[:, slv].astype(BF16)
            qh = q_ref[:, sl]
            doh = do_ref[:, sl]
            p = _xattn_probs(qh, kh)
            dkv_ref[:, slv] += lax.dot_general(p.astype(BF16), doh, NT, preferred_element_type=F32)
            dp = lax.dot_general(doh, vh, (((1,), (1,)), ((), ())), preferred_element_type=F32)
            ds = (p * (dp - jnp.sum(p * dp, axis=-1, keepdims=True)) * (X_HEAD_DIM ** -0.5)).astype(BF16)
            dq_ref[:, sl] = jnp.dot(ds, kh, preferred_element_type=F32).astype(BF16)
            dkv_ref[:, sl] += lax.dot_general(ds, qh, NT, preferred_element_type=F32)

    row = pl.BlockSpec((tm, D_MODEL), lambda i: (i, 0))
    kvs = pl.BlockSpec((Mm, 2 * D_MODEL), lambda i: (0, 0))
    return pl.pallas_call(
        body, name=name, grid=(T // tm,),
        in_specs=[row, kvs, row], out_specs=[row, kvs],
        out_shape=[jax.ShapeDtypeStruct((T, D_MODEL), BF16), jax.ShapeDtypeStruct((Mm, 2 * D_MODEL), F32)],
        compiler_params=_params(("arbitrary",)),
    )(q, kvm, do)


def _swiglu_bwd_weights(tag, n, G, U, A, wd, dhb):
    T = n.shape[0]
    dG, dU = _swiglu_bwd_act(dhb, wd, G, U, name=f"{tag}_bwd_act", tm=1024)
    dwd = _matmul([(A, dhb)], M=D_FF, N=D_MODEL, K=T, tm=1408, tn=1024, tk=2048, a_t=True, out_dtype=BF16,
                  scale=0.5, name=f"{tag}_dwd")
    dwg = _matmul([(n, dG)], M=D_MODEL, N=D_FF, K=T, tm=1024, tn=1408, tk=2048, a_t=True, out_kind="s",
                  out_dtype=BF16, order="ji", name=f"{tag}_dwg")
    dwu = _matmul([(n, dU)], M=D_MODEL, N=D_FF, K=T, tm=1024, tn=1408, tk=2048, a_t=True, out_kind="s",
                  out_dtype=BF16, order="ji", name=f"{tag}_dwu")
    return dG, dU, dwg, dwu, dwd


def _swiglu_bwd_input(tag, hin, g_norm, dG, dU, wg, wu, dh, dep):
    T = hin.shape[0]
    dn = _matmul([(dG, wg), (dU, wu)], M=T, N=D_MODEL, K=D_FF, tm=512, tn=D_MODEL // 2, tk=D_FF // 2, b_kind="t",
                 out_dtype=BF16, dep=dep, name=f"{tag}_dn")
    return _rms_bwd(hin, g_norm, dn, dh, name=f"{tag}_norm_bwd")


GROUP_FFN1 = ["w1_gate", "w1_up", "w1_down"]
GROUP_MID = ["w_in", "w_out", "w_xq", "w_xkv", "w_xo"]
GROUP_FFN2 = ["w2_gate", "w2_up", "w2_down"]


def _local_step(x, mem, tgt, W, comm):
    T = x.shape[0]
    Mm = mem.shape[0]
    mm = functools.partial(_matmul)

    W = {**W, **comm.gather_now("ffn1_up", ["w1_gate", "w1_up"])}
    tok = comm.gather_start("ffn1_down", ["w1_down"], after=W["w1_up"])
    tok = comm.gather_start("mid", GROUP_MID, after=tok)
    tok = comm.gather_start("ffn2", GROUP_FFN2, after=tok)
    n1 = _rms_fwd(x, W["g_ffn1"], dep=tok, name="f_norm1")
    G1, U1, A1 = _swiglu_up(n1, W["w1_gate"], W["w1_up"], name="f_ffn1_up")
    tok = comm.gather_landed("ffn1_down", after=A1)
    tok = comm.gather_landed("mid", after=tok)
    W = {**W, **comm.gather_ready("ffn1_down", after=tok)}
    h1, n2 = mm([(A1, W["w1_down"])], M=T, N=D_MODEL, K=D_FF, tm=256, tn=D_MODEL, tk=D_FF // 2, scale=0.5, res=x,
                norm_g=W["g_mix"], name="f_ffn1_down")
    W = {**W, **comm.gather_ready("mid", after=n2)}
    z = mm([(n2, W["w_in"])], M=T, N=IN_COLS, K=D_MODEL, tm=512, tn=IN_COLS // 2, tk=D_MODEL, name="f_w_in")
    bst = jnp.transpose(W["b_s"])
    yn = _mixer_fwd(z, W["g_v"], W["w_s"], bst, W["sinks"], W["g_a_out"], W["g_b_out"], name="f_mixer")
    tok = comm.gather_landed("ffn2", after=yn)
    h2, n3 = mm([(yn, W["w_out"])], M=T, N=D_MODEL, K=D_MODEL, tm=512, tn=D_MODEL, tk=D_MODEL, res=h1,
                norm_g=W["g_x"], dep=tok, name="f_w_out")
    memn = _rms_fwd(mem, W["g_mem"], name="f_norm_mem")
    q3 = mm([(n3, W["w_xq"])], M=T, N=D_MODEL, K=D_MODEL, tm=1024, tn=D_MODEL, tk=D_MODEL, out_dtype=BF16,
            name="f_w_xq")
    kvm = mm([(memn, W["w_xkv"])], M=Mm, N=2 * D_MODEL, K=D_MODEL, tm=Mm, tn=1024, tk=D_MODEL, b_kind="n",
             name="f_w_xkv")
    o3 = _xattn_fwd(q3, kvm, name="f_xattn")
    h3, n4 = mm([(o3, W["w_xo"])], M=T, N=D_MODEL, K=D_MODEL, tm=512, tn=D_MODEL, tk=D_MODEL, res=h2,
                norm_g=W["g_ffn2"], name="f_w_xo")
    W = {**W, **comm.gather_ready("ffn2", after=n4)}
    G2, U2, A2 = _swiglu_up(n4, W["w2_gate"], W["w2_up"], name="f_ffn2_up")
    h4 = mm([(A2, W["w2_down"])], M=T, N=D_MODEL, K=D_FF, tm=512, tn=D_MODEL // 2, tk=D_FF // 2, scale=0.5, res=h3,
            name="f_ffn2_down")

    grads = {}
    dh4, dh4b, grads["g_final"], loss = _loss_head(h4, W["g_final"], tgt, name="loss_head")
    dG2, dU2, dwg, dwu, dwd = _swiglu_bwd_weights("b_ffn2", n4, G2, U2, A2, W["w2_down"], dh4b)
    tok = comm.reduce_pair_start("ffn2", {"w2_gate": dwg, "w2_up": dwu, "w2_down": dwd})
    dh3, dh3b, grads["g_ffn2"] = _swiglu_bwd_input("b_ffn2", h3, W["g_ffn2"], dG2, dU2, W["w2_gate"], W["w2_up"],
                                                   dh4, tok)
    tok = comm.reduce_pair_done("ffn2", after=dh3b)

    mid = {}
    do3 = mm([(dh3b, W["w_xo"])], M=T, N=D_MODEL, K=D_MODEL, tm=512, tn=D_MODEL, tk=D_MODEL, b_kind="t",
             out_dtype=BF16, dep=tok, name="b_do3")
    mid["w_xo"] = mm([(o3, dh3b)], M=D_MODEL, N=D_MODEL, K=T, tm=1024, tn=D_MODEL, tk=1024, a_t=True,
                       out_dtype=BF16, name="b_dw_xo")
    dq3, dkvm = _xattn_bwd(q3, kvm, do3, name="b_xattn")
    mid["w_xq"] = mm([(n3, dq3)], M=D_MODEL, N=D_MODEL, K=T, tm=1024, tn=D_MODEL, tk=1024, a_t=True,
                       out_dtype=BF16, name="b_dw_xq")
    dn3 = mm([(dq3, W["w_xq"])], M=T, N=D_MODEL, K=D_MODEL, tm=512, tn=D_MODEL, tk=D_MODEL, b_kind="t",
             out_dtype=BF16, name="b_dn3")
    dh2, dh2b, grads["g_x"] = _rms_bwd(h2, W["g_x"], dn3, dh3, name="b_norm3")
    dkvmb = dkvm.astype(BF16)
    mid["w_xkv"] = mm([(memn, dkvmb)], M=D_MODEL, N=2 * D_MODEL, K=Mm, tm=D_MODEL, tn=1024, tk=Mm, a_t=True,
                        out_kind="s", out_dtype=BF16, name="b_dw_xkv")
    dmemn = mm([(dkvmb, W["w_xkv"])], M=Mm, N=D_MODEL, K=2 * D_MODEL, tm=Mm, tn=D_MODEL, tk=1024, b_kind="t",
               name="b_dmemn")
    _, _, grads["g_mem"] = _rms_bwd(mem, W["g_mem"], dmemn, None, name="b_norm_mem")
    comm.reduce_finish("ffn2", after=dh2b)

    dyn = mm([(dh2b, W["w_out"])], M=T, N=D_MODEL, K=D_MODEL, tm=1024, tn=D_MODEL, tk=D_MODEL, b_kind="t",
             out_dtype=BF16, name="b_dyn")
    mid["w_out"] = mm([(yn, dh2b)], M=D_MODEL, N=D_MODEL, K=T, tm=1024, tn=D_MODEL, tk=1024, a_t=True,
                        out_dtype=BF16, name="b_dw_out")
    dz, grads["g_v"], grads["w_s"], dbst, grads["sinks"], grads["g_a_out"], grads["g_b_out"] = _mixer_bwd(
        z, dyn, W["g_v"], W["w_s"], bst, W["sinks"], W["g_a_out"], W["g_b_out"], name="b_mixer")
    grads["b_s"] = jnp.transpose(dbst)
    mid["w_in"] = mm([(n2, dz)], M=D_MODEL, N=IN_COLS, K=T, tm=1024, tn=IN_COLS, tk=1024, a_t=True,
                     out_dtype=BF16, name="b_dw_in")
    tok = comm.reduce_pair_start("mid", mid)
    dn2 = mm([(dz, W["w_in"])], M=T, N=D_MODEL, K=IN_COLS, tm=512, tn=D_MODEL, tk=IN_COLS, b_kind="t",
             out_dtype=BF16, dep=tok, name="b_dn2")
    tok = comm.reduce_pair_done("mid", after=dn2)
    dh1, dh1b, grads["g_mix"] = _rms_bwd(h1, W["g_mix"], dn2, dh2, dep=tok, name="b_norm2")

    dG1, dU1, dwg, dwu, dwd = _swiglu_bwd_weights("b_ffn1", n1, G1, U1, A1, W["w1_down"], dh1b)
    comm.reduce_finish("mid", after=dwu)
    tok = comm.reduce_start("ffn1", {"w1_gate": dwg, "w1_up": dwu, "w1_down": dwd})
    dx, _, grads["g_ffn1"] = _swiglu_bwd_input("b_ffn1", x, W["g_ffn1"], dG1, dU1, W["w1_gate"], W["w1_up"], dh1, tok)
    comm.reduce_finish("ffn1", after=dx)
    return loss, dx, grads


BIG = ["w1_gate", "w1_up", "w1_down", "w_in", "w_out", "w_xq", "w_xkv", "w_xo", "w2_gate", "w2_up", "w2_down"]
SMALL = ["g_ffn1", "g_mix", "g_v", "w_s", "b_s", "sinks", "g_a_out", "g_b_out", "g_x", "g_mem", "g_ffn2", "g_final"]
ALL_W = ["g_ffn1", "w1_gate", "w1_up", "w1_down", "g_mix", "w_in", "g_v", "w_s", "b_s", "sinks", "g_a_out",
         "g_b_out", "w_out", "g_x", "g_mem", "w_xq", "w_xkv", "w_xo", "g_ffn2", "w2_gate", "w2_up", "w2_down",
         "g_final"]
ANY = pl.BlockSpec(memory_space=pl.ANY)


def _place():
    x, y, c = lax.axis_index("x"), lax.axis_index("y"), lax.axis_index("c")
    chips = [(1 - x, y), (x, 1 - y), (1 - x, 1 - y)]
    return x, y, c, chips


COL_SHARDED = ("w1_gate", "w1_up", "w2_gate", "w2_up", "w_xkv")


def _gathered_shape(shape, by_cols):
    rows, cols = shape
    return (rows, N_CHIPS * cols) if by_cols else (N_CHIPS, rows, cols)


def _owner_rows(ref, shape, by_cols, slot, r0, rows):
    cols = shape[1]
    if by_cols:
        return ref.at[pl.ds(r0, rows), pl.ds(pl.multiple_of(slot * cols, LANES), cols)]
    return ref.at[slot, pl.ds(r0, rows), :]


def _allgather_weights(shards, by_cols, *, name):
    n = len(shards)

    def body(*refs):
        ins, outs = refs[:n], refs[n:2 * n]
        send, recv, loc = refs[2 * n:]
        x, y, c, chips = _place()
        me = 2 * x + y
        sib = (x, y, 1 - c)

        def half(w, slot, hc):
            h = shards[w].shape[0] // 2
            return _owner_rows(outs[w], shards[w].shape, by_cols[w], slot, hc * h, h)

        def copy(w, k, slot, hc, to, src=None):
            return pltpu.make_async_remote_copy(
                src_ref=half(w, slot, hc) if src is None else src, dst_ref=half(w, slot, hc),
                send_sem=send.at[6 * w + k], recv_sem=recv.at[6 * w + k], device_id=to, device_id_type=MESH)

        own = [pltpu.make_async_remote_copy(
            src_ref=ins[w], dst_ref=_owner_rows(outs[w], shards[w].shape, by_cols[w], me, 0, shards[w].shape[0]),
            send_sem=loc.at[w], recv_sem=loc.at[n + w], device_id=sib, device_id_type=MESH) for w in range(n)]
        for cp in own:
            cp.start()
        first = []
        for w in range(n):
            h = shards[w].shape[0] // 2
            for j, (tx, ty) in enumerate(chips):
                first.append(copy(w, j, me, c, (tx, ty, c), src=ins[w].at[pl.ds(c * h, h), :]))
                first[-1].start()
        passed = []
        for w in range(n):
            for j, (tx, ty) in enumerate(chips):
                slot = 2 * tx + ty
                copy(w, j, slot, c, (tx, ty, c)).wait_recv()
                passed.append(copy(w, 3 + j, slot, c, sib))
                passed[-1].start()
        for w in range(n):
            for j, (tx, ty) in enumerate(chips):
                copy(w, 3 + j, 2 * tx + ty, 1 - c, sib).wait_recv()
        for cp in first + passed:
            cp.wait_send()
        for cp in own:
            cp.wait()

    return pl.pallas_call(
        body, name=name, in_specs=[ANY] * n, out_specs=[ANY] * n,
        out_shape=[jax.ShapeDtypeStruct(_gathered_shape(s.shape, bc), s.dtype) for s, bc in zip(shards, by_cols)],
        scratch_shapes=[pltpu.SemaphoreType.DMA((6 * n,)), pltpu.SemaphoreType.DMA((6 * n,)),
                        pltpu.SemaphoreType.DMA((2 * n,))],
    )(*shards)


def _pair_exchange(grads, *, name):
    n = len(grads)

    def body(*refs):
        ins, outs = refs[:n], refs[n:2 * n]
        send, recv = refs[2 * n:]
        x, y, c, _ = _place()
        cps = []
        for w in range(n):
            h = grads[w].shape[1] // 2
            cps.append(pltpu.make_async_remote_copy(
                src_ref=ins[w].at[:, pl.ds((1 - c) * h, h), :], dst_ref=outs[w],
                send_sem=send.at[w], recv_sem=recv.at[w], device_id=(x, y, 1 - c), device_id_type=MESH))
            cps[-1].start()
        for cp in cps:
            cp.wait()

    return pl.pallas_call(
        body, name=name, in_specs=[ANY] * n, out_specs=[ANY] * n,
        out_shape=[jax.ShapeDtypeStruct((N_CHIPS, g.shape[1] // 2, g.shape[2]), g.dtype) for g in grads],
        scratch_shapes=[pltpu.SemaphoreType.DMA((n,)), pltpu.SemaphoreType.DMA((n,))],
    )(*grads)


def _pair_sum(g, got, *, name):
    S, R, C = g.shape
    h = R // 2
    tr = _row_block(h, 3 * C * 2, 16)
    nr = h // tr

    def body(a_ref, b_ref, o_ref):
        o_ref[...] = (a_ref[...].astype(F32) + b_ref[...].astype(F32)).astype(BF16)

    return pl.pallas_call(
        body, name=name, grid=(S, nr),
        in_specs=[pl.BlockSpec((None, tr, C), lambda s, r: (s, lax.axis_index("c") * nr + r, 0)),
                  pl.BlockSpec((None, tr, C), lambda s, r: (s, r, 0))],
        out_specs=pl.BlockSpec((None, tr, C), lambda s, r: (s, r, 0)),
        out_shape=jax.ShapeDtypeStruct((S, h, C), BF16),
        compiler_params=_params(("parallel", "parallel")),
    )(g, got)


def _chip_sum(part, got, *, name):
    S, h, C = part.shape
    tr = _row_block(h, 4 * C * 2 + C * 4, 16)
    nr = h // tr

    def body(own_ref, g0_ref, g1_ref, g2_ref, o_ref):
        acc = own_ref[...].astype(F32) + g0_ref[...].astype(F32)
        o_ref[...] = (acc + g1_ref[...].astype(F32)) + g2_ref[...].astype(F32)

    def piece(j):
        return pl.BlockSpec((None, tr, C), lambda r: (j, r, 0))

    return pl.pallas_call(
        body, name=name, grid=(nr,),
        in_specs=[pl.BlockSpec((None, tr, C), lambda r: (2 * lax.axis_index("x") + lax.axis_index("y"), r, 0)),
                  piece(0), piece(1), piece(2)],
        out_specs=pl.BlockSpec((tr, C), lambda r: (lax.axis_index("c") * nr + r, 0)),
        out_shape=jax.ShapeDtypeStruct((2 * h, C), F32),
        compiler_params=_params(("parallel",)),
    )(part, got, got, got)


def _pair_gather(totals, *, name):
    n = len(totals)

    def body(*refs):
        ins, outs = refs[:n], refs[n:2 * n]
        send, recv = refs[2 * n:]
        x, y, c, _ = _place()
        cps = []
        for w in range(n):
            h = totals[w].shape[0] // 2
            cps.append(pltpu.make_async_remote_copy(
                src_ref=ins[w].at[pl.ds(c * h, h), :], dst_ref=outs[w].at[pl.ds(c * h, h), :],
                send_sem=send.at[w], recv_sem=recv.at[w], device_id=(x, y, 1 - c), device_id_type=MESH))
            cps[-1].start()
        for w in range(n):
            h = totals[w].shape[0] // 2
            theirs = outs[w].at[pl.ds((1 - c) * h, h), :]
            pltpu.make_async_remote_copy(
                src_ref=theirs, dst_ref=theirs, send_sem=send.at[w], recv_sem=recv.at[w],
                device_id=(x, y, 1 - c), device_id_type=MESH).wait_recv()
        for cp in cps:
            cp.wait_send()

    return pl.pallas_call(
        body, name=name, in_specs=[ANY] * n, out_specs=[ANY] * n,
        out_shape=[jax.ShapeDtypeStruct(t.shape, t.dtype) for t in totals],
        input_output_aliases={w: w for w in range(n)},
        scratch_shapes=[pltpu.SemaphoreType.DMA((n,)), pltpu.SemaphoreType.DMA((n,))],
    )(*totals)


def _allreduce_small(v, *, name):
    R, C = v.shape
    ND = 8

    def body(v_ref, o_ref, all_ref, send, recv, loc):
        x, y, c, chips = _place()
        me, sib = (x, y, c), (x, y, 1 - c)

        def rows(px, py, pc):
            return all_ref.at[pl.ds((4 * px + 2 * py + pc) * R, R), :]

        def copy(k, block, to, src=None):
            return pltpu.make_async_remote_copy(
                src_ref=rows(*block) if src is None else src, dst_ref=rows(*block),
                send_sem=send.at[k], recv_sem=recv.at[k], device_id=to, device_id_type=MESH)

        mine = pltpu.make_async_copy(v_ref, rows(*me), loc)
        mine.start()
        first = [copy(0, me, sib, src=v_ref)]
        first += [copy(1 + j, me, (*chip, c), src=v_ref) for j, chip in enumerate(chips)]
        for cp in first:
            cp.start()
        passed = [copy(4 + j, (*chip, c), sib) for j, chip in enumerate(chips)]
        for j, chip in enumerate(chips):
            copy(1 + j, (*chip, c), me).wait_recv()
            passed[j].start()
        copy(0, sib, me).wait_recv()
        for j, chip in enumerate(chips):
            copy(4 + j, (*chip, 1 - c), me).wait_recv()
        for cp in first + passed:
            cp.wait_send()
        mine.wait()
        acc = all_ref[0:R, :]
        for d in range(1, ND):
            acc = acc + all_ref[d * R:(d + 1) * R, :]
        o_ref[...] = acc

    vm = pl.BlockSpec(memory_space=pltpu.VMEM)
    return pl.pallas_call(
        body, name=name, in_specs=[vm], out_specs=[vm, vm],
        out_shape=[jax.ShapeDtypeStruct((R, C), F32), jax.ShapeDtypeStruct((ND * R, C), F32)],
        scratch_shapes=[pltpu.SemaphoreType.DMA((7,)), pltpu.SemaphoreType.DMA((7,)), pltpu.SemaphoreType.DMA],
        compiler_params=pltpu.CompilerParams(vmem_limit_bytes=VMEM_LIMIT),
    )(v)[0]


HBM = pl.BlockSpec(memory_space=pltpu.HBM)
SEM = pl.BlockSpec(memory_space=pltpu.SEMAPHORE)
EFFECT = pltpu.SideEffectType.DATAFLOW_SIDE_EFFECTING


def _remote(src, dst, send, recv, k, to):
    return pltpu.make_async_remote_copy(src_ref=src, dst_ref=dst, send_sem=send.at[k], recv_sem=recv.at[k],
                                        device_id=to, device_id_type=MESH)


def _split_start(bufs, plan, ncopies, *, name, after=None):
    nb = len(bufs)
    extra = [] if after is None else [after]

    def body(*refs):
        pos = nb + len(extra)
        send, recv, token = refs[pos], refs[pos + 1], refs[-1]
        for k, (src, dst, to) in enumerate(plan(refs[:nb])):
            _remote(src, dst, send, recv, k, to).start()
        token[...] = jnp.zeros_like(token)

    outs = pl.pallas_call(
        body, name=name,
        out_shape=(pltpu.SemaphoreType.DMA((ncopies,)), pltpu.SemaphoreType.DMA((ncopies,)),
                   *[pltpu.HBM(b.shape, b.dtype) for b in bufs], jax.ShapeDtypeStruct((SUBLANES, LANES), F32)),
        in_specs=[HBM] * nb + [ANY] * len(extra),
        out_specs=(SEM, SEM, *[HBM] * nb, pl.BlockSpec(memory_space=pltpu.VMEM)),
        input_output_aliases={i: 2 + i for i in range(nb)},
        compiler_params=pltpu.CompilerParams(has_side_effects=EFFECT),
    )(*[pltpu.with_memory_space_constraint(b, pltpu.HBM) for b in bufs], *extra)
    return outs[0], outs[1], list(outs[2:2 + nb]), outs[-1]


def _split_wait(started, plan, after, *, name):
    send, recv, bufs, _ = started
    nb = len(bufs)

    def body(*refs):
        send_sem, recv_sem = refs[nb], refs[nb + 1]
        for k, (src, dst, to) in enumerate(plan(refs[:nb])):
            cp = _remote(src, dst, send_sem, recv_sem, k, to)
            cp.wait_send()
            cp.wait_recv()

    outs = pl.pallas_call(
        body, name=name,
        out_shape=tuple(pltpu.HBM(b.shape, b.dtype) for b in bufs),
        in_specs=[HBM] * nb + [SEM, SEM, ANY], out_specs=tuple([HBM] * nb),
        input_output_aliases={i: i for i in range(nb)},
        compiler_params=pltpu.CompilerParams(has_side_effects=EFFECT),
    )(*bufs, send, recv, after)
    return list(outs)


def _gather_chip_plan(shapes, by_cols):
    n = len(shapes)

    def plan(refs):
        srcs, lands = refs[:n], refs[n:]
        x, y, c, chips = _place()
        out = []
        for w in range(n):
            h = shapes[w][0] // 2
            mine = _owner_rows(lands[w], shapes[w], by_cols[w], 2 * x + y, c * h, h)
            for tx, ty in chips:
                out.append((srcs[w].at[pl.ds(c * h, h), :], mine, (tx, ty, c)))
        return out

    return plan


def _gather_pair_plan(shapes, by_cols):
    n = len(shapes)

    def plan(refs):
        srcs, lands = refs[:n], refs[n:]
        x, y, c, chips = _place()
        out = []
        for w in range(n):
            h = shapes[w][0] // 2
            for tx, ty in chips:
                half = _owner_rows(lands[w], shapes[w], by_cols[w], 2 * tx + ty, c * h, h)
                out.append((half, half, (x, y, 1 - c)))
            own = _owner_rows(lands[w], shapes[w], by_cols[w], 2 * x + y, 0, shapes[w][0])
            out.append((srcs[w], own, (x, y, 1 - c)))
        return out

    return plan


def _reduce_pair_plan(shapes):
    n = len(shapes)

    def plan(refs):
        local, lands = refs[:n], refs[n:]
        x, y, c, _ = _place()
        out = []
        for w in range(n):
            h = shapes[w][1] // 2
            out.append((local[w].at[:, pl.ds((1 - c) * h, h), :], lands[w], (x, y, 1 - c)))
        return out

    return plan


def _reduce_chip_plan(n):
    def plan(refs):
        parts, lands = refs[:n], refs[n:]
        x, y, c, chips = _place()
        return [(parts[w].at[2 * tx + ty], lands[w].at[j], (tx, ty, c))
                for w in range(n) for j, (tx, ty) in enumerate(chips)]

    return plan


def _as_operands(gathered):
    out = {}
    for n, g in gathered.items():
        if n in COL_SHARDED:
            out[n] = g
        elif n == "w_in":
            out[n] = jnp.transpose(g, (1, 0, 2)).reshape(D_MODEL, IN_COLS)
        else:
            out[n] = g.reshape(g.shape[0] * g.shape[1], g.shape[2])
    return out


def _by_owner(n, g):
    if n == "w_in":
        return jnp.transpose(g.reshape(D_MODEL, N_CHIPS, IN_COLS // N_CHIPS), (1, 0, 2))
    if g.ndim == 2:
        return g.reshape(N_CHIPS, g.shape[0] // N_CHIPS, g.shape[1])
    return g


class _Comm:
    def __init__(self, shards):
        self.shards = shards
        self.total = {}
        self._flight = {}

    def _layout(self, names):
        return [self.shards[n].shape for n in names], [n in COL_SHARDED for n in names]

    def gather_now(self, tag, names):
        _, by_cols = self._layout(names)
        got = _allgather_weights([self.shards[n] for n in names], by_cols, name=f"gather_{tag}")
        return _as_operands(dict(zip(names, got)))

    def gather_start(self, tag, names, after):
        shapes, by_cols = self._layout(names)
        srcs = [self.shards[n] for n in names]
        lands = [lax.empty(_gathered_shape(s.shape, bc), s.dtype) for s, bc in zip(srcs, by_cols)]
        started = _split_start(srcs + lands, _gather_chip_plan(shapes, by_cols), 3 * len(srcs),
                               after=after, name=f"gather_{tag}_chips_start")
        self._flight[tag] = (names, started)
        return started[3]

    def gather_landed(self, tag, after):
        names, started = self._flight[tag]
        shapes, by_cols = self._layout(names)
        bufs = _split_wait(started, _gather_chip_plan(shapes, by_cols), after, name=f"gather_{tag}_chips_wait")
        started = _split_start(bufs, _gather_pair_plan(shapes, by_cols), 4 * len(names),
                               name=f"gather_{tag}_pair_start")
        self._flight[tag] = (names, started)
        return started[3]

    def gather_ready(self, tag, after):
        names, started = self._flight.pop(tag)
        shapes, by_cols = self._layout(names)
        bufs = _split_wait(started, _gather_pair_plan(shapes, by_cols), after, name=f"gather_{tag}_pair_wait")
        return _as_operands(dict(zip(names, bufs[len(names):])))

    def reduce_start(self, tag, grads):
        names = list(grads)
        local = [_by_owner(n, grads[n]) for n in names]
        return self._chip_start(tag, names, local, _pair_exchange(local, name=f"pair_exchange_{tag}"))

    def reduce_pair_start(self, tag, grads):
        names = list(grads)
        local = [_by_owner(n, grads[n]) for n in names]
        lands = [lax.empty((N_CHIPS, g.shape[1] // 2, g.shape[2]), g.dtype) for g in local]
        started = _split_start(local + lands, _reduce_pair_plan([g.shape for g in local]), len(names),
                               name=f"pair_exchange_{tag}_start")
        self._flight[tag] = (names, started)
        return started[3]

    def reduce_pair_done(self, tag, after):
        names, started = self._flight.pop(tag)
        n = len(names)
        bufs = _split_wait(started, _reduce_pair_plan([b.shape for b in started[2][:n]]), after,
                           name=f"pair_exchange_{tag}_wait")
        return self._chip_start(tag, names, bufs[:n], bufs[n:])

    def _chip_start(self, tag, names, local, from_sib):
        parts = [_pair_sum(g, s, name=f"pair_sum_{n}") for n, g, s in zip(names, local, from_sib)]
        lands = [lax.empty((N_CHIPS - 1,) + p.shape[1:], p.dtype) for p in parts]
        self._flight[tag] = (names, _split_start(parts + lands, _reduce_chip_plan(len(names)), 3 * len(names),
                                                 name=f"chip_exchange_{tag}_start"))
        return self._flight[tag][1][3]

    def reduce_finish(self, tag, after):
        names, started = self._flight.pop(tag)
        n = len(names)
        bufs = _split_wait(started, _reduce_chip_plan(n), after, name=f"chip_exchange_{tag}_wait")
        totals = [_chip_sum(p, s, name=f"chip_sum_{nm}") for nm, p, s in zip(names, bufs[:n], bufs[n:])]
        self.total.update(zip(names, _pair_gather(totals, name=f"pair_gather_{tag}")))


def _adamw(w, g, m, v, *, name):
    R, C = w.shape
    tr = _row_block(R, 8 * C * 4, SUBLANES)

    def body(w_ref, g_ref, m_ref, v_ref, go_ref, d_ref, nm_ref, nv_ref):
        gg = g_ref[...]
        go_ref[...] = gg
        m_new = ADAM_B1 * m_ref[...] + (1.0 - ADAM_B1) * gg
        v_new = ADAM_B2 * v_ref[...] + (1.0 - ADAM_B2) * (gg * gg)
        m_hat = m_new / (1.0 - ADAM_B1 ** ADAM_STEP)
        v_hat = v_new / (1.0 - ADAM_B2 ** ADAM_STEP)
        d_ref[...] = -ADAM_LR * (m_hat / (jnp.sqrt(v_hat) + ADAM_EPS) + ADAM_WD * w_ref[...])
        nm_ref[...] = m_new
        nv_ref[...] = v_new

    blk = pl.BlockSpec((tr, C), lambda i: (i, 0))
    shp = jax.ShapeDtypeStruct((R, C), F32)
    return pl.pallas_call(
        body, name=name, grid=(R // tr,), in_specs=[blk] * 4, out_specs=[blk] * 4, out_shape=[shp] * 4,
        compiler_params=_params(("parallel",)),
    )(w, g, m, v)


def _to2d(a):
    flat = a.reshape(-1)
    pad = (-flat.shape[0]) % (SUBLANES * LANES)
    if pad:
        flat = jnp.pad(flat, (0, pad))
    return flat.reshape(-1, LANES)


def _small_rows(shape):
    return -(-math.prod(shape) // (SUBLANES * LANES)) * SUBLANES


def _pack_small(parts):
    rows = jnp.concatenate([_to2d(p) for p in parts], axis=0)
    pad = (-rows.shape[0]) % 256
    if pad:
        rows = jnp.concatenate([rows, jnp.zeros((pad, LANES), rows.dtype)], axis=0)
    return rows


def _unpack_small(rows, shapes):
    out, r = [], 0
    for shp in shapes:
        size = math.prod(shp)
        nrow = _small_rows(shp)
        out.append(rows[r:r + nrow].reshape(-1)[:size].reshape(shp))
        r += nrow
    return out


def kernel(x, mem, g_ffn1, w1_gate, w1_up, w1_down, g_mix, w_in, g_v, w_s, b_s, sinks, g_a_out, g_b_out, w_out, g_x, g_mem, w_xq, w_xkv, w_xo, g_ffn2, w2_gate, w2_up, w2_down, g_final, loss_target, m_g_ffn1, m_w1_gate, m_w1_up, m_w1_down, m_g_mix, m_w_in, m_g_v, m_w_s, m_b_s, m_sinks, m_g_a_out, m_g_b_out, m_w_out, m_g_x, m_g_mem, m_w_xq, m_w_xkv, m_w_xo, m_g_ffn2, m_w2_gate, m_w2_up, m_w2_down, m_g_final, v_g_ffn1, v_w1_gate, v_w1_up, v_w1_down, v_g_mix, v_w_in, v_g_v, v_w_s, v_b_s, v_sinks, v_g_a_out, v_g_b_out, v_w_out, v_g_x, v_g_mem, v_w_xq, v_w_xkv, v_w_xo, v_g_ffn2, v_w2_gate, v_w2_up, v_w2_down, v_g_final):
    args = dict(locals())
    Wp = {n: args[n] for n in ALL_W}
    Mp = {n: args["m_" + n] for n in ALL_W}
    Vp = {n: args["v_" + n] for n in ALL_W}

    comm = _Comm({n: Wp[n][0].astype(BF16) for n in BIG})
    W = {n: Wp[n] for n in SMALL}
    W["g_final"] = Wp["g_final"].reshape(1, D_MODEL)
    for n in ("w_s", "b_s"):
        W[n] = Wp[n][0]
    loss, dx, grads = _local_step(x[0], mem[0], loss_target[0], W, comm)
    big_grad = comm.total

    small_shapes = [Wp[n].shape for n in SMALL]
    packed = _pack_small([grads[n].reshape(Wp[n].shape) for n in SMALL] + [loss])
    summed = _allreduce_small(packed, name="allreduce_small")
    small_grad = dict(zip(SMALL, _unpack_small(summed, small_shapes)))
    nrows = sum(_small_rows(s) for s in small_shapes)
    loss_total = summed[nrows, 0]

    grad_out, delta, new_m, new_v = {}, {}, {}, {}
    for n in BIG:
        shp = Wp[n].shape
        g, d, nm, nv = _adamw(Wp[n][0], big_grad[n], Mp[n][0], Vp[n][0], name=f"adamw_{n}")
        grad_out[n], delta[n], new_m[n], new_v[n] = g.reshape(shp), d.reshape(shp), nm.reshape(shp), nv.reshape(shp)
    sw = _pack_small([Wp[n] for n in SMALL])
    sg = _pack_small([small_grad[n] for n in SMALL])
    sm = _pack_small([Mp[n] for n in SMALL])
    sv = _pack_small([Vp[n] for n in SMALL])
    _, d, nm, nv = _adamw(sw, sg, sm, sv, name="adamw_small")
    for n, dd, mm_, vv_ in zip(SMALL, _unpack_small(d, small_shapes), _unpack_small(nm, small_shapes),
                               _unpack_small(nv, small_shapes)):
        grad_out[n], delta[n], new_m[n], new_v[n] = small_grad[n], dd, mm_, vv_

    return (loss_total, dx[None], *[grad_out[n] for n in ALL_W], *[delta[n] for n in ALL_W],
            *[new_m[n] for n in ALL_W], *[new_v[n] for n in ALL_W])
```

```python
import functools
import math

import jax
import jax.numpy as jnp
from jax import lax
from jax.experimental import pallas as pl
from jax.experimental.pallas import tpu as pltpu

F32 = jnp.float32
BF16 = jnp.bfloat16
MESH = pl.DeviceIdType.MESH

D_MODEL = 2048
D_FF = 5632
D_A = 1024
D_B = 1024
CHUNK = 128
A_GROUPS = 8
HEAD_DIM = 64
B_Q_HEADS = 16
B_KV_HEADS = 2
X_HEADS = 4
X_HEAD_DIM = 512
IN_COLS = 3328
O_Q = 2 * D_A
O_K = O_Q + D_B
O_V = O_K + B_KV_HEADS * HEAD_DIM
N_CHIPS = 4
EPS = 1e-5
NEG = -1e30
ADAM_LR = 0.001
ADAM_B1 = 0.9
ADAM_B2 = 0.999
ADAM_EPS = 1e-08
ADAM_WD = 0.01
ADAM_STEP = 10

V7X_VMEM_BYTES = 64 * 1024 * 1024
VMEM_LIMIT = 56 * 1024 * 1024
LANES = 128
SUBLANES = 8


ANY = pl.BlockSpec(memory_space=pl.ANY)


def _params(sem, vmem=VMEM_LIMIT):
    return pltpu.CompilerParams(dimension_semantics=sem, vmem_limit_bytes=vmem)


def _matmul(pairs, *, M, N, K, tm, tn, tk, a_t=False, b_kind="n", out_kind="n", out_dtype=F32,
            scale=1.0, res=None, norm_g=None, order="ij", dep=None, name):
    tm, tn, tk = min(tm, M), min(tn, N), min(tk, K)
    assert M % tm == 0 and N % tn == 0 and K % tk == 0, (name, M, N, K, tm, tn, tk)
    nk = K // tk
    npairs = len(pairs)
    b_t = b_kind == "t"
    ns = N // N_CHIPS

    def ij(g0, g1):
        return (g0, g1) if order == "ij" else (g1, g0)

    def a_map(g0, g1, k):
        i, _ = ij(g0, g1)
        return (k, i) if a_t else (i, k)

    a_spec = pl.BlockSpec((tk, tm) if a_t else (tm, tk), a_map)

    if b_kind == "n":
        b_spec = pl.BlockSpec((tk, tn), lambda g0, g1, k: (k, ij(g0, g1)[1]))
    else:
        b_spec = pl.BlockSpec((tn, tk), lambda g0, g1, k: (ij(g0, g1)[1], k))

    if out_kind == "n":
        o_spec = pl.BlockSpec((tm, tn), lambda g0, g1, k: ij(g0, g1))
        o_shape = jax.ShapeDtypeStruct((M, N), out_dtype)
    else:
        assert tn % ns == 0
        o_spec = pl.BlockSpec((tn // ns, tm, ns), lambda g0, g1, k: (ij(g0, g1)[1], ij(g0, g1)[0], 0))
        o_shape = jax.ShapeDtypeStruct((N_CHIPS, M, ns), out_dtype)

    in_specs, args = [], []
    for a, b in pairs:
        in_specs += [a_spec, b_spec]
        args += [a, b]
    if res is not None:
        in_specs.append(pl.BlockSpec((tm, tn), lambda g0, g1, k: ij(g0, g1)))
        args.append(res)
    if norm_g is not None:
        assert tn == N and out_kind == "n"
        in_specs.append(pl.BlockSpec((1, N), lambda g0, g1, k: (0, 0)))
        args.append(norm_g)
    if dep is not None:
        in_specs.append(ANY)
        args.append(dep)

    dn = (((0,) if a_t else (1,), (1,) if b_t else (0,)), ((), ()))

    def body(*refs):
        pos = 2 * npairs
        res_ref = refs[pos] if res is not None else None
        pos += res is not None
        g_ref = refs[pos] if norm_g is not None else None
        pos += (norm_g is not None) + (dep is not None)
        o_ref = refs[pos]
        n_ref = refs[pos + 1] if norm_g is not None else None
        acc_ref = refs[-1] if nk > 1 else None
        part = None
        for p in range(npairs):
            d = lax.dot_general(refs[2 * p][...], refs[2 * p + 1][...], dn, preferred_element_type=F32)
            part = d if part is None else part + d

        def finish(acc):
            r = acc * scale if scale != 1.0 else acc
            if res_ref is not None:
                r = res_ref[...] + r
            if out_kind == "n":
                o_ref[...] = r.astype(out_dtype)
            else:
                for s in range(tn // ns):
                    o_ref[s] = r[:, s * ns:(s + 1) * ns].astype(out_dtype)
            if n_ref is not None:
                n_ref[...] = (r * _rstd(r) * g_ref[...]).astype(BF16)

        if nk == 1:
            finish(part)
        else:
            k = pl.program_id(2)

            @pl.when(k == 0)
            def _():
                acc_ref[...] = part

            @pl.when((k > 0) & (k < nk - 1))
            def _():
                acc_ref[...] += part

            @pl.when(k == nk - 1)
            def _():
                finish(acc_ref[...] + part)

    grid = (M // tm, N // tn, nk) if order == "ij" else (N // tn, M // tm, nk)
    out_specs, out_shape = o_spec, o_shape
    if norm_g is not None:
        out_specs = [o_spec, pl.BlockSpec((tm, tn), lambda g0, g1, k: ij(g0, g1))]
        out_shape = [o_shape, jax.ShapeDtypeStruct((M, N), BF16)]
    return pl.pallas_call(
        body, name=name, grid=grid, in_specs=in_specs, out_specs=out_specs, out_shape=out_shape,
        scratch_shapes=[pltpu.VMEM((tm, tn), F32)] if nk > 1 else [],
        compiler_params=_params(("parallel", "parallel", "arbitrary")),
    )(*args)


def _rstd(x):
    return lax.rsqrt(jnp.mean(x * x, axis=-1, keepdims=True) + EPS)


def _rms_bwd_math(x, g, dy):
    r = _rstd(x)
    gy = dy * g
    xr = x * r
    dx = r * (gy - xr * jnp.mean(gy * xr, axis=-1, keepdims=True))
    return dx, dy * xr


def _rms_fwd(h, g, *, name, tm=512, dep=None):
    T, Dm = h.shape
    tm = min(tm, T)

    def body(h_ref, g_ref, *rest):
        x = h_ref[...]
        rest[-1][...] = (x * _rstd(x) * g_ref[...]).astype(BF16)

    return pl.pallas_call(
        body, name=name, grid=(T // tm,),
        in_specs=[pl.BlockSpec((tm, Dm), lambda i: (i, 0)), pl.BlockSpec((1, Dm), lambda i: (0, 0))]
        + ([ANY] if dep is not None else []),
        out_specs=pl.BlockSpec((tm, Dm), lambda i: (i, 0)),
        out_shape=jax.ShapeDtypeStruct((T, Dm), BF16),
        compiler_params=_params(("parallel",)),
    )(h, g, *([dep] if dep is not None else []))


def _rms_bwd(h, g, dn, dres, *, name, tm=256, dep=None):
    T, Dm = h.shape
    tm = min(tm, T)
    has_res = dres is not None

    def body(*refs):
        h_ref, g_ref, dn_ref = refs[:3]
        pos = 3
        dres_ref = refs[pos] if has_res else None
        pos += has_res + (dep is not None)
        dh_ref, dhb_ref, dg_ref = refs[pos:pos + 3]
        dx, dgr = _rms_bwd_math(h_ref[...], g_ref[...], dn_ref[...].astype(F32))
        if has_res:
            dx = dres_ref[...] + dx
        dh_ref[...] = dx
        dhb_ref[...] = dx.astype(BF16)
        part = jnp.sum(dgr, axis=0, keepdims=True)

        @pl.when(pl.program_id(0) == 0)
        def _():
            dg_ref[...] = part

        @pl.when(pl.program_id(0) > 0)
        def _():
            dg_ref[...] += part

    row = pl.BlockSpec((tm, Dm), lambda i: (i, 0))
    vec = pl.BlockSpec((1, Dm), lambda i: (0, 0))
    args = [h, g, dn] + ([dres] if has_res else []) + ([dep] if dep is not None else [])
    return pl.pallas_call(
        body, name=name, grid=(T // tm,),
        in_specs=[row, vec, row] + ([row] if has_res else []) + ([ANY] if dep is not None else []),
        out_specs=[row, row, vec],
        out_shape=[jax.ShapeDtypeStruct((T, Dm), F32), jax.ShapeDtypeStruct((T, Dm), BF16),
                   jax.ShapeDtypeStruct((1, Dm), F32)],
        compiler_params=_params(("arbitrary",)),
    )(*args)


def _loss_head(h, g, tgt, *, name, tm=256):
    T, Dm = h.shape
    tm = min(tm, T)

    def body(h_ref, g_ref, t_ref, dh_ref, dhb_ref, dg_ref, loss_ref):
        x = h_ref[...]
        gv = g_ref[...]
        r = _rstd(x)
        diff = x * r * gv - t_ref[...]
        lpart = 0.5 * jnp.sum(jnp.mean(diff * diff, axis=-1, keepdims=True), axis=0, keepdims=True)
        dx, dgr = _rms_bwd_math(x, gv, diff * (1.0 / Dm))
        dh_ref[...] = dx
        dhb_ref[...] = dx.astype(BF16)
        part = jnp.sum(dgr, axis=0, keepdims=True)
        lrow = jnp.broadcast_to(lpart, (1, LANES))

        @pl.when(pl.program_id(0) == 0)
        def _():
            dg_ref[...] = part
            loss_ref[...] = lrow

        @pl.when(pl.program_id(0) > 0)
        def _():
            dg_ref[...] += part
            loss_ref[...] += lrow

    row = pl.BlockSpec((tm, Dm), lambda i: (i, 0))
    vec = pl.BlockSpec((1, Dm), lambda i: (0, 0))
    return pl.pallas_call(
        body, name=name, grid=(T // tm,),
        in_specs=[row, vec, row],
        out_specs=[row, row, vec, pl.BlockSpec((1, LANES), lambda i: (0, 0))],
        out_shape=[jax.ShapeDtypeStruct((T, Dm), F32), jax.ShapeDtypeStruct((T, Dm), BF16),
                   jax.ShapeDtypeStruct((1, Dm), F32), jax.ShapeDtypeStruct((1, LANES), F32)],
        compiler_params=_params(("arbitrary",)),
    )(h, g, tgt)


MXU_COLS = 256
FF_TILE = 2 * MXU_COLS


def _row_block(rows, row_bytes, align, budget=24 * 1024 * 1024):
    fits = [d for d in range(align, rows + 1, align) if rows % d == 0 and 2 * d * row_bytes <= budget]
    assert fits, (rows, row_bytes)
    return fits[-1]


def _swiglu_up(n, wg, wu, *, name, tm=1024, tn=FF_TILE):
    T, Dm = n.shape
    Fd = wg.shape[1]
    tm = min(tm, T)

    def body(n_ref, wg_ref, wu_ref, g_ref, u_ref, a_ref):
        x = n_ref[...]
        g = jnp.dot(x, wg_ref[...], preferred_element_type=F32)
        u = jnp.dot(x, wu_ref[...], preferred_element_type=F32)
        g_ref[...] = g.astype(BF16)
        u_ref[...] = u.astype(BF16)
        a_ref[...] = (g * jax.nn.sigmoid(g) * u).astype(BF16)

    wspec = pl.BlockSpec((Dm, tn), lambda j, i: (0, j))
    ospec = pl.BlockSpec((tm, tn), lambda j, i: (i, j))
    oshape = jax.ShapeDtypeStruct((T, Fd), BF16)
    return pl.pallas_call(
        body, name=name, grid=(Fd // tn, T // tm),
        in_specs=[pl.BlockSpec((tm, Dm), lambda j, i: (i, 0)), wspec, wspec],
        out_specs=[ospec, ospec, ospec], out_shape=[oshape, oshape, oshape],
        compiler_params=_params(("parallel", "parallel")),
    )(n, wg, wu)


def _swiglu_bwd_act(dhb, wd, G, U, *, name, tm=1024, tn=D_FF // N_CHIPS):
    T, Dm = dhb.shape
    Fd = wd.shape[0]
    tm, tn = min(tm, T), min(tn, Fd)

    def body(dh_ref, wd_ref, g_ref, u_ref, dg_ref, du_ref):
        da = 0.5 * lax.dot_general(dh_ref[...], wd_ref[...], (((1,), (1,)), ((), ())), preferred_element_type=F32)
        g = g_ref[...].astype(F32)
        u = u_ref[...].astype(F32)
        sg = jax.nn.sigmoid(g)
        dg_ref[...] = (da * u * (sg * (1.0 + g * (1.0 - sg)))).astype(BF16)
        du_ref[...] = (da * (g * sg)).astype(BF16)

    blk = pl.BlockSpec((tm, tn), lambda j, i: (i, j))
    oshape = jax.ShapeDtypeStruct((T, Fd), BF16)
    return pl.pallas_call(
        body, name=name, grid=(Fd // tn, T // tm),
        in_specs=[pl.BlockSpec((tm, Dm), lambda j, i: (i, 0)), pl.BlockSpec((tn, Dm), lambda j, i: (j, 0)), blk, blk],
        out_specs=[blk, blk], out_shape=[oshape, oshape],
        compiler_params=_params(("parallel", "parallel")),
    )(dhb, wd, G, U)


_INV_SQRT2 = 0.7071067811865476
_INV_SQRT2PI = 0.3989422804014327


def _erf(x):
    ax = jnp.abs(x)
    t = 1.0 / (1.0 + 0.3275911 * ax)
    poly = t * (0.254829592 + t * (-0.284496736 + t * (1.421413741 + t * (-1.453152027 + t * 1.061405429))))
    y = 1.0 - poly * jnp.exp(-ax * ax)
    return jnp.where(x < 0, -y, y)


def _gelu_cdf(x):
    return 0.5 * (1.0 + _erf(x * _INV_SQRT2))


def _lane_lt64(shape):
    return lax.broadcasted_iota(jnp.int32, shape, len(shape) - 1) < HEAD_DIM


def _dup_half(x, kv):
    rolled = pltpu.roll(x, HEAD_DIM, 1)
    lo = _lane_lt64(x.shape)
    return jnp.where(lo, x, rolled) if kv == 0 else jnp.where(lo, rolled, x)


HEADS_PER_KV = B_Q_HEADS // B_KV_HEADS
PAIRS = HEADS_PER_KV // 2


def _attn_bias():
    shape = (HEADS_PER_KV * CHUNK, 2 * CHUNK)
    qpos = (lax.broadcasted_iota(jnp.int32, shape, 0) & (CHUNK - 1)) + CHUNK
    kpos = lax.broadcasted_iota(jnp.int32, shape, 1)
    diff = qpos - kpos
    band = (diff >= 0) & (diff < CHUNK)
    return jnp.stack([jnp.where(band & (kpos >= CHUNK), 0.0, NEG), jnp.where(band, 0.0, NEG)]).astype(F32)


def _stack_heads(tiles, lo):
    parts = []
    for t in tiles:
        parts += [jnp.where(lo, t, 0.0), jnp.where(lo, 0.0, t)]
    return jnp.concatenate(parts, axis=0)


def _unstack_heads(s, lo):
    return [jnp.where(lo, s[2 * p * CHUNK:(2 * p + 1) * CHUNK], s[(2 * p + 1) * CHUNK:(2 * p + 2) * CHUNK])
            for p in range(PAIRS)]


def _stack_sinks(sk_ref, kv):
    return jnp.concatenate([jnp.broadcast_to(sk_ref[:, h:h + 1], (CHUNK, 1))
                            for h in range(kv * HEADS_PER_KV, (kv + 1) * HEADS_PER_KV)], axis=0)


def _sgu_forward(z_ref, gv, wsm, bst):
    zu = z_ref[:, 0:D_A]
    zv = z_ref[:, D_A:2 * D_A]
    u = zu * _gelu_cdf(zu)
    v = zv * _gelu_cdf(zv)
    rv = _rstd(v)
    vn = (v * rv * gv).astype(BF16)
    svs = []
    for g in range(A_GROUPS):
        sl = slice(g * CHUNK, (g + 1) * CHUNK)
        svs.append(jnp.dot(wsm[g], vn[:, sl], preferred_element_type=F32) + bst[:, g:g + 1])
    sv = jnp.concatenate(svs, axis=1)
    return zu, zv, u, v, rv, vn, sv


def _masked_ws(ws_ref):
    tril = lax.broadcasted_iota(jnp.int32, (CHUNK, CHUNK), 0) >= lax.broadcasted_iota(jnp.int32, (CHUNK, CHUNK), 1)
    return [jnp.where(tril, ws_ref[g], 0.0).astype(BF16) for g in range(A_GROUPS)], tril


def _attn_probs(qm, kkd, sink, bias):
    s = lax.dot_general(qm, kkd, (((1,), (1,)), ((), ())), preferred_element_type=F32) * (HEAD_DIM ** -0.5) + bias
    m = jnp.maximum(jnp.max(s, axis=-1, keepdims=True), sink)
    e = jnp.exp(s - m)
    es = jnp.exp(sink - m)
    inv = 1.0 / (jnp.sum(e, axis=-1, keepdims=True) + es)
    return e * inv, es * inv


def _mixer_fwd(z, gv, ws, bst, sinks, ga, gb, *, name):
    T = z.shape[0]
    nb = T // CHUNK
    kvb = O_K // (2 * CHUNK)

    def body(z_ref, zp_ref, bias_ref, gv_ref, ws_ref, bst_ref, sk_ref, ga_ref, gb_ref, o_ref, p_ref):
        wsm, _ = _masked_ws(ws_ref)
        _, _, u, _, _, _, sv = _sgu_forward(z_ref, gv_ref[...], wsm, bst_ref[...])
        ya = u * sv
        o_ref[:, 0:D_A] = (ya * _rstd(ya) * ga_ref[...]).astype(BF16)

        mask = bias_ref[...]
        kk = jnp.concatenate([zp_ref[:, 0:CHUNK], z_ref[:, O_K:O_V]], axis=0)
        vv = jnp.concatenate([zp_ref[:, CHUNK:2 * CHUNK], z_ref[:, O_V:IN_COLS]], axis=0)
        lo = _lane_lt64((CHUNK, LANES))
        outs = []
        for kv in range(B_KV_HEADS):
            kkd = _dup_half(kk, kv).astype(BF16)
            vvd = _dup_half(vv, kv).astype(BF16)
            q = _stack_heads([z_ref[:, O_Q + (kv * PAIRS + pr) * LANES:O_Q + (kv * PAIRS + pr + 1) * LANES]
                              for pr in range(PAIRS)], lo).astype(BF16)
            p, _ = _attn_probs(q, kkd, _stack_sinks(sk_ref, kv), mask)
            p_ref[kv] = p
            outs += _unstack_heads(jnp.dot(p.astype(BF16), vvd, preferred_element_type=F32), lo)
        yb = jnp.concatenate(outs, axis=1)
        o_ref[:, D_A:D_A + D_B] = (yb * _rstd(yb) * gb_ref[...]).astype(BF16)

    full = lambda shape: pl.BlockSpec(shape, lambda i: (0,) * len(shape))
    pshape = (B_KV_HEADS, HEADS_PER_KV * CHUNK, 2 * CHUNK)
    return pl.pallas_call(
        body, name=name, grid=(nb,),
        in_specs=[pl.BlockSpec((CHUNK, IN_COLS), lambda i: (i, 0)),
                  pl.BlockSpec((CHUNK, 2 * CHUNK), lambda i: (jnp.maximum(i - 1, 0), kvb)),
                  pl.BlockSpec((None, HEADS_PER_KV * CHUNK, 2 * CHUNK), lambda i: (jnp.minimum(i, 1), 0, 0)),
                  full((1, D_A)), full((A_GROUPS, CHUNK, CHUNK)), full((CHUNK, A_GROUPS)), full((1, B_Q_HEADS)),
                  full((1, D_A)), full((1, D_B))],
        out_specs=[pl.BlockSpec((CHUNK, D_A + D_B), lambda i: (i, 0)),
                   pl.BlockSpec((None,) + pshape, lambda i: (i, 0, 0, 0))],
        out_shape=[jax.ShapeDtypeStruct((T, D_A + D_B), BF16), jax.ShapeDtypeStruct((nb,) + pshape, F32)],
        compiler_params=_params(("parallel",)),
    )(z, z, _attn_bias(), gv, ws, bst, sinks, ga, gb)


def _mixer_bwd(z, dyn, probs, gv, ws, bst, ga, gb, *, name):
    T = z.shape[0]
    nb = T // CHUNK
    kvb = O_K // (2 * CHUNK)
    NT = (((0,), (0,)), ((), ()))

    def body(z_ref, zp_ref, dy_ref, p_ref, gv_ref, ws_ref, bst_ref, ga_ref, gb_ref,
             dz_ref, dgv_ref, dws_ref, dbst_ref, dsk_ref, dga_ref, dgb_ref, carry_ref):
        step = pl.program_id(0)

        @pl.when(step == 0)
        def _():
            carry_ref[...] = jnp.zeros_like(carry_ref)
            dgv_ref[...] = jnp.zeros_like(dgv_ref)
            dws_ref[...] = jnp.zeros_like(dws_ref)
            dbst_ref[...] = jnp.zeros_like(dbst_ref)
            dsk_ref[...] = jnp.zeros_like(dsk_ref)
            dga_ref[...] = jnp.zeros_like(dga_ref)
            dgb_ref[...] = jnp.zeros_like(dgb_ref)

        wsm, tril = _masked_ws(ws_ref)
        gvv = gv_ref[...]
        zu, zv, u, v, rv, vn, sv = _sgu_forward(z_ref, gvv, wsm, bst_ref[...])
        ya = u * sv
        dya, dga_rows = _rms_bwd_math(ya, ga_ref[...], dy_ref[:, 0:D_A].astype(F32))
        dga_ref[...] += jnp.sum(dga_rows, axis=0, keepdims=True)
        du = dya * sv
        dsv = dya * u
        dvn_parts = []
        for g in range(A_GROUPS):
            sl = slice(g * CHUNK, (g + 1) * CHUNK)
            dsv_g = dsv[:, sl]
            dsv_gb = dsv_g.astype(BF16)
            dw = lax.dot_general(dsv_gb, vn[:, sl], (((1,), (1,)), ((), ())), preferred_element_type=F32)
            dws_ref[g] += jnp.where(tril, dw, 0.0)
            dbst_ref[:, g:g + 1] += jnp.sum(dsv_g, axis=1, keepdims=True)
            dvn_parts.append(lax.dot_general(wsm[g], dsv_gb, NT, preferred_element_type=F32))
        dvn = jnp.concatenate(dvn_parts, axis=1)
        dv, dgv_rows = _rms_bwd_math(v, gvv, dvn)
        dgv_ref[...] += jnp.sum(dgv_rows, axis=0, keepdims=True)
        dz_ref[:, 0:D_A] = (du * (_gelu_cdf(zu) + zu * jnp.exp(-0.5 * zu * zu) * _INV_SQRT2PI)).astype(BF16)
        dz_ref[:, D_A:2 * D_A] = (dv * (_gelu_cdf(zv) + zv * jnp.exp(-0.5 * zv * zv) * _INV_SQRT2PI)).astype(BF16)

        kk = jnp.concatenate([zp_ref[:, 0:CHUNK], z_ref[:, O_K:O_V]], axis=0)
        vv = jnp.concatenate([zp_ref[:, CHUNK:2 * CHUNK], z_ref[:, O_V:IN_COLS]], axis=0)
        lo = _lane_lt64((CHUNK, LANES))
        kkd = [_dup_half(kk, kv).astype(BF16) for kv in range(B_KV_HEADS)]
        vvd = [_dup_half(vv, kv).astype(BF16) for kv in range(B_KV_HEADS)]
        outs = []
        for kv in range(B_KV_HEADS):
            outs += _unstack_heads(jnp.dot(p_ref[kv].astype(BF16), vvd[kv], preferred_element_type=F32), lo)
        yb = jnp.concatenate(outs, axis=1)
        dyb, dgb_rows = _rms_bwd_math(yb, gb_ref[...], dy_ref[:, D_A:D_A + D_B].astype(F32))
        dgb_ref[...] += jnp.sum(dgb_rows, axis=0, keepdims=True)

        dkk, dvv = [], []
        for kv in range(B_KV_HEADS):
            do = _stack_heads([dyb[:, (kv * PAIRS + pr) * LANES:(kv * PAIRS + pr + 1) * LANES]
                               for pr in range(PAIRS)], lo).astype(BF16)
            q = _stack_heads([z_ref[:, O_Q + (kv * PAIRS + pr) * LANES:O_Q + (kv * PAIRS + pr + 1) * LANES]
                              for pr in range(PAIRS)], lo).astype(BF16)
            p = p_ref[kv]
            dvv.append(lax.dot_general(p.astype(BF16), do, NT, preferred_element_type=F32))
            dp = lax.dot_general(do, vvd[kv], (((1,), (1,)), ((), ())), preferred_element_type=F32)
            delta = jnp.sum(p * dp, axis=-1, keepdims=True)
            dsink = (jnp.sum(p, axis=-1, keepdims=True) - 1.0) * delta
            for g in range(HEADS_PER_KV):
                h = kv * HEADS_PER_KV + g
                dsk_ref[:, h:h + 1] += jnp.sum(dsink[g * CHUNK:(g + 1) * CHUNK], axis=0, keepdims=True)
            ds = (p * (dp - delta) * (HEAD_DIM ** -0.5)).astype(BF16)
            dq = _unstack_heads(jnp.dot(ds, kkd[kv], preferred_element_type=F32), lo)
            for pr in range(PAIRS):
                c0 = O_Q + (kv * PAIRS + pr) * LANES
                dz_ref[:, c0:c0 + LANES] = dq[pr].astype(BF16)
            dkk.append(lax.dot_general(ds, q, NT, preferred_element_type=F32))

        def fold(parts):
            tot = [t + pltpu.roll(t, HEAD_DIM, 1) for t in parts]
            return jnp.where(_lane_lt64(tot[0].shape), tot[0], tot[1])

        dk_all = fold(dkk)
        dv_all = fold(dvv)
        dz_ref[:, O_K:O_V] = (dk_all[CHUNK:] + carry_ref[:, 0:CHUNK]).astype(BF16)
        dz_ref[:, O_V:IN_COLS] = (dv_all[CHUNK:] + carry_ref[:, CHUNK:2 * CHUNK]).astype(BF16)
        carry_ref[:, 0:CHUNK] = dk_all[:CHUNK]
        carry_ref[:, CHUNK:2 * CHUNK] = dv_all[:CHUNK]

    full = lambda shape: pl.BlockSpec(shape, lambda s: (0,) * len(shape))
    rev = lambda s: nb - 1 - s
    return pl.pallas_call(
        body, name=name, grid=(nb,),
        in_specs=[pl.BlockSpec((CHUNK, IN_COLS), lambda s: (rev(s), 0)),
                  pl.BlockSpec((CHUNK, 2 * CHUNK), lambda s: (jnp.maximum(rev(s) - 1, 0), kvb)),
                  pl.BlockSpec((CHUNK, D_A + D_B), lambda s: (rev(s), 0)),
                  pl.BlockSpec((None, B_KV_HEADS, HEADS_PER_KV * CHUNK, 2 * CHUNK), lambda s: (rev(s), 0, 0, 0)),
                  full((1, D_A)), full((A_GROUPS, CHUNK, CHUNK)), full((CHUNK, A_GROUPS)),
                  full((1, D_A)), full((1, D_B))],
        out_specs=[pl.BlockSpec((CHUNK, IN_COLS), lambda s: (rev(s), 0)),
                   full((1, D_A)), full((A_GROUPS, CHUNK, CHUNK)), full((CHUNK, A_GROUPS)), full((1, B_Q_HEADS)),
                   full((1, D_A)), full((1, D_B))],
        out_shape=[jax.ShapeDtypeStruct((T, IN_COLS), BF16), jax.ShapeDtypeStruct((1, D_A), F32),
                   jax.ShapeDtypeStruct((A_GROUPS, CHUNK, CHUNK), F32), jax.ShapeDtypeStruct((CHUNK, A_GROUPS), F32),
                   jax.ShapeDtypeStruct((1, B_Q_HEADS), F32), jax.ShapeDtypeStruct((1, D_A), F32),
                   jax.ShapeDtypeStruct((1, D_B), F32)],
        scratch_shapes=[pltpu.VMEM((CHUNK, 2 * CHUNK), F32)],
        compiler_params=_params(("arbitrary",)),
    )(z, z, dyn, probs, gv, ws, bst, ga, gb)


def _xattn_probs(qh, kh):
    s = lax.dot_general(qh, kh, (((1,), (1,)), ((), ())), preferred_element_type=F32) * (X_HEAD_DIM ** -0.5)
    e = jnp.exp(s - jnp.max(s, axis=-1, keepdims=True))
    return e / jnp.sum(e, axis=-1, keepdims=True)


def _xattn_fwd(q, kvm, *, name, tm=512):
    T = q.shape[0]
    Mm = kvm.shape[0]
    tm = min(tm, T)

    def body(q_ref, kv_ref, o_ref):
        for h in range(X_HEADS):
            sl = slice(h * X_HEAD_DIM, (h + 1) * X_HEAD_DIM)
            kh = kv_ref[:, sl].astype(BF16)
            vh = kv_ref[:, D_MODEL + h * X_HEAD_DIM:D_MODEL + (h + 1) * X_HEAD_DIM].astype(BF16)
            p = _xattn_probs(q_ref[:, sl], kh)
            o_ref[:, sl] = jnp.dot(p.astype(BF16), vh, preferred_element_type=F32).astype(BF16)

    return pl.pallas_call(
        body, name=name, grid=(T // tm,),
        in_specs=[pl.BlockSpec((tm, D_MODEL), lambda i: (i, 0)), pl.BlockSpec((Mm, 2 * D_MODEL), lambda i: (0, 0))],
        out_specs=pl.BlockSpec((tm, D_MODEL), lambda i: (i, 0)),
        out_shape=jax.ShapeDtypeStruct((T, D_MODEL), BF16),
        compiler_params=_params(("parallel",)),
    )(q, kvm)


def _xattn_bwd(q, kvm, do, *, name, tm=512):
    T = q.shape[0]
    Mm = kvm.shape[0]
    tm = min(tm, T)
    NT = (((0,), (0,)), ((), ()))

    def body(q_ref, kv_ref, do_ref, dq_ref, dkv_ref):
        @pl.when(pl.program_id(0) == 0)
        def _():
            dkv_ref[...] = jnp.zeros_like(dkv_ref)

        for h in range(X_HEADS):
            sl = slice(h * X_HEAD_DIM, (h + 1) * X_HEAD_DIM)
            slv = slice(D_MODEL + h * X_HEAD_DIM, D_MODEL + (h + 1) * X_HEAD_DIM)
            kh = kv_ref[:, sl].astype(BF16)
            vh = kv_ref[:, slv].astype(BF16)
            qh = q_ref[:, sl]
            doh = do_ref[:, sl]
            p = _xattn_probs(qh, kh)
            dkv_ref[:, slv] += lax.dot_general(p.astype(BF16), doh, NT, preferred_element_type=F32)
            dp = lax.dot_general(doh, vh, (((1,), (1,)), ((), ())), preferred_element_type=F32)
            ds = (p * (dp - jnp.sum(p * dp, axis=-1, keepdims=True)) * (X_HEAD_DIM ** -0.5)).astype(BF16)
            dq_ref[:, sl] = jnp.dot(ds, kh, preferred_element_type=F32).astype(BF16)
            dkv_ref[:, sl] += lax.dot_general(ds, qh, NT, preferred_element_type=F32)

    row = pl.BlockSpec((tm, D_MODEL), lambda i: (i, 0))
    kvs = pl.BlockSpec((Mm, 2 * D_MODEL), lambda i: (0, 0))
    return pl.pallas_call(
        body, name=name, grid=(T // tm,),
        in_specs=[row, kvs, row], out_specs=[row, kvs],
        out_shape=[jax.ShapeDtypeStruct((T, D_MODEL), BF16), jax.ShapeDtypeStruct((Mm, 2 * D_MODEL), F32)],
        compiler_params=_params(("arbitrary",)),
    )(q, kvm, do)


def _swiglu_bwd_weights(tag, n, G, U, A, wd, dhb):
    T = n.shape[0]
    dG, dU = _swiglu_bwd_act(dhb, wd, G, U, name=f"{tag}_bwd_act", tm=1024)
    dwd = _matmul([(A, dhb)], M=D_FF, N=D_MODEL, K=T, tm=1408, tn=1024, tk=2048, a_t=True, out_dtype=BF16,
                  scale=0.5, name=f"{tag}_dwd")
    dwg = _matmul([(n, dG)], M=D_MODEL, N=D_FF, K=T, tm=512, tn=D_FF // 2, tk=2048, a_t=True, out_kind="s",
                  out_dtype=BF16, order="ji", name=f"{tag}_dwg")
    dwu = _matmul([(n, dU)], M=D_MODEL, N=D_FF, K=T, tm=1024, tn=D_FF // 2, tk=1024, a_t=True, out_kind="s",
                  out_dtype=BF16, order="ji", name=f"{tag}_dwu")
    return dG, dU, dwg, dwu, dwd


def _swiglu_bwd_input(tag, hin, g_norm, dG, dU, wg, wu, dh, dep):
    T = hin.shape[0]
    dn = _matmul([(dG, wg), (dU, wu)], M=T, N=D_MODEL, K=D_FF, tm=512, tn=D_MODEL // 2, tk=D_FF // 2, b_kind="t",
                 out_dtype=BF16, dep=dep, name=f"{tag}_dn")
    return _rms_bwd(hin, g_norm, dn, dh, name=f"{tag}_norm_bwd")


GROUP_FFN1 = ["w1_gate", "w1_up", "w1_down"]
GROUP_MID = ["w_in", "w_out", "w_xq", "w_xkv", "w_xo"]
GROUP_FFN2 = ["w2_gate", "w2_up", "w2_down"]


def _local_step(x, mem, tgt, W, comm):
    T = x.shape[0]
    Mm = mem.shape[0]
    mm = functools.partial(_matmul)

    W = {**W, **comm.gather_now("ffn1_up", ["w1_gate", "w1_up"])}
    tok = comm.gather_start("ffn1_down", ["w1_down"], after=W["w1_up"])
    tok = comm.gather_start("mid", GROUP_MID, after=tok)
    tok = comm.gather_start("ffn2", GROUP_FFN2, after=tok)
    n1 = _rms_fwd(x, W["g_ffn1"], dep=tok, name="f_norm1")
    G1, U1, A1 = _swiglu_up(n1, W["w1_gate"], W["w1_up"], name="f_ffn1_up")
    tok = comm.gather_landed("ffn1_down", after=A1)
    tok = comm.gather_landed("mid", after=tok)
    W = {**W, **comm.gather_ready("ffn1_down", after=tok)}
    h1 = mm([(A1, W["w1_down"])], M=T, N=D_MODEL, K=D_FF, tm=512, tn=D_MODEL // 2, tk=D_FF // 2, scale=0.5, res=x,
            name="f_ffn1_down")
    n2 = _rms_fwd(h1, W["g_mix"], name="f_norm2")
    W = {**W, **comm.gather_ready("mid", after=n2)}
    z = mm([(n2, W["w_in"])], M=T, N=IN_COLS, K=D_MODEL, tm=512, tn=IN_COLS // 2, tk=D_MODEL, name="f_w_in")
    bst = jnp.transpose(W["b_s"])
    yn, probs = _mixer_fwd(z, W["g_v"], W["w_s"], bst, W["sinks"], W["g_a_out"], W["g_b_out"], name="f_mixer")
    tok = comm.gather_landed("ffn2", after=yn)
    h2, n3 = mm([(yn, W["w_out"])], M=T, N=D_MODEL, K=D_MODEL, tm=512, tn=D_MODEL, tk=D_MODEL, res=h1,
                norm_g=W["g_x"], dep=tok, name="f_w_out")
    memn = _rms_fwd(mem, W["g_mem"], name="f_norm_mem")
    q3 = mm([(n3, W["w_xq"])], M=T, N=D_MODEL, K=D_MODEL, tm=1024, tn=D_MODEL, tk=D_MODEL, out_dtype=BF16,
            name="f_w_xq")
    kvm = mm([(memn, W["w_xkv"])], M=Mm, N=2 * D_MODEL, K=D_MODEL, tm=Mm, tn=1024, tk=D_MODEL, b_kind="n",
             name="f_w_xkv")
    o3 = _xattn_fwd(q3, kvm, name="f_xattn")
    h3, n4 = mm([(o3, W["w_xo"])], M=T, N=D_MODEL, K=D_MODEL, tm=512, tn=D_MODEL, tk=D_MODEL, res=h2,
                norm_g=W["g_ffn2"], name="f_w_xo")
    W = {**W, **comm.gather_ready("ffn2", after=n4)}
    G2, U2, A2 = _swiglu_up(n4, W["w2_gate"], W["w2_up"], name="f_ffn2_up")
    h4 = mm([(A2, W["w2_down"])], M=T, N=D_MODEL, K=D_FF, tm=512, tn=D_MODEL // 2, tk=D_FF // 2, scale=0.5, res=h3,
            name="f_ffn2_down")

    grads = {}
    dh4, dh4b, grads["g_final"], loss = _loss_head(h4, W["g_final"], tgt, name="loss_head")
    dG2, dU2, dwg, dwu, dwd = _swiglu_bwd_weights("b_ffn2", n4, G2, U2, A2, W["w2_down"], dh4b)
    tok = comm.reduce_pair_start("ffn2", {"w2_gate": dwg, "w2_up": dwu, "w2_down": dwd})
    dh3, dh3b, grads["g_ffn2"] = _swiglu_bwd_input("b_ffn2", h3, W["g_ffn2"], dG2, dU2, W["w2_gate"], W["w2_up"],
                                                   dh4, tok)
    tok = comm.reduce_pair_done("ffn2", after=dh3b)

    mid = {}
    do3 = mm([(dh3b, W["w_xo"])], M=T, N=D_MODEL, K=D_MODEL, tm=512, tn=D_MODEL, tk=D_MODEL, b_kind="t",
             out_dtype=BF16, dep=tok, name="b_do3")
    mid["w_xo"] = mm([(o3, dh3b)], M=D_MODEL, N=D_MODEL, K=T, tm=1024, tn=D_MODEL, tk=1024, a_t=True,
                       out_dtype=BF16, name="b_dw_xo")
    dq3, dkvm = _xattn_bwd(q3, kvm, do3, name="b_xattn")
    mid["w_xq"] = mm([(n3, dq3)], M=D_MODEL, N=D_MODEL, K=T, tm=1024, tn=D_MODEL, tk=1024, a_t=True,
                       out_dtype=BF16, name="b_dw_xq")
    dn3 = mm([(dq3, W["w_xq"])], M=T, N=D_MODEL, K=D_MODEL, tm=512, tn=D_MODEL, tk=D_MODEL, b_kind="t",
             out_dtype=BF16, name="b_dn3")
    dh2, dh2b, grads["g_x"] = _rms_bwd(h2, W["g_x"], dn3, dh3, name="b_norm3")
    dkvmb = dkvm.astype(BF16)
    mid["w_xkv"] = mm([(memn, dkvmb)], M=D_MODEL, N=2 * D_MODEL, K=Mm, tm=D_MODEL, tn=1024, tk=Mm, a_t=True,
                        out_kind="s", out_dtype=BF16, name="b_dw_xkv")
    dmemn = mm([(dkvmb, W["w_xkv"])], M=Mm, N=D_MODEL, K=2 * D_MODEL, tm=Mm, tn=D_MODEL, tk=1024, b_kind="t",
               name="b_dmemn")
    _, _, grads["g_mem"] = _rms_bwd(mem, W["g_mem"], dmemn, None, name="b_norm_mem")
    comm.reduce_finish("ffn2", after=dh2b)

    dyn = mm([(dh2b, W["w_out"])], M=T, N=D_MODEL, K=D_MODEL, tm=1024, tn=D_MODEL, tk=D_MODEL, b_kind="t",
             out_dtype=BF16, name="b_dyn")
    mid["w_out"] = mm([(yn, dh2b)], M=D_MODEL, N=D_MODEL, K=T, tm=1024, tn=D_MODEL, tk=1024, a_t=True,
                        out_dtype=BF16, name="b_dw_out")
    dz, grads["g_v"], grads["w_s"], dbst, grads["sinks"], grads["g_a_out"], grads["g_b_out"] = _mixer_bwd(
        z, dyn, probs, W["g_v"], W["w_s"], bst, W["g_a_out"], W["g_b_out"], name="b_mixer")
    grads["b_s"] = jnp.transpose(dbst)
    mid["w_in"] = mm([(n2, dz)], M=D_MODEL, N=IN_COLS, K=T, tm=1024, tn=IN_COLS, tk=1024, a_t=True,
                     out_dtype=BF16, name="b_dw_in")
    tok = comm.reduce_pair_start("mid", mid)
    dn2 = mm([(dz, W["w_in"])], M=T, N=D_MODEL, K=IN_COLS, tm=512, tn=D_MODEL, tk=IN_COLS, b_kind="t",
             out_dtype=BF16, dep=tok, name="b_dn2")
    tok = comm.reduce_pair_done("mid", after=dn2)
    dh1, dh1b, grads["g_mix"] = _rms_bwd(h1, W["g_mix"], dn2, dh2, dep=tok, name="b_norm2")

    dG1, dU1, dwg, dwu, dwd = _swiglu_bwd_weights("b_ffn1", n1, G1, U1, A1, W["w1_down"], dh1b)
    comm.reduce_finish("mid", after=dwu)
    tok = comm.reduce_start("ffn1", {"w1_gate": dwg, "w1_up": dwu, "w1_down": dwd})
    dx, _, grads["g_ffn1"] = _swiglu_bwd_input("b_ffn1", x, W["g_ffn1"], dG1, dU1, W["w1_gate"], W["w1_up"], dh1, tok)
    comm.reduce_finish("ffn1", after=dx)
    return loss, dx, grads


BIG = ["w1_gate", "w1_up", "w1_down", "w_in", "w_out", "w_xq", "w_xkv", "w_xo", "w2_gate", "w2_up", "w2_down"]
SMALL = ["g_ffn1", "g_mix", "g_v", "w_s", "b_s", "sinks", "g_a_out", "g_b_out", "g_x", "g_mem", "g_ffn2", "g_final"]
ALL_W = ["g_ffn1", "w1_gate", "w1_up", "w1_down", "g_mix", "w_in", "g_v", "w_s", "b_s", "sinks", "g_a_out",
         "g_b_out", "w_out", "g_x", "g_mem", "w_xq", "w_xkv", "w_xo", "g_ffn2", "w2_gate", "w2_up", "w2_down",
         "g_final"]
ANY = pl.BlockSpec(memory_space=pl.ANY)


def _place():
    x, y, c = lax.axis_index("x"), lax.axis_index("y"), lax.axis_index("c")
    chips = [(1 - x, y), (x, 1 - y), (1 - x, 1 - y)]
    return x, y, c, chips


COL_SHARDED = ("w1_gate", "w1_up", "w2_gate", "w2_up", "w_xkv")


def _gathered_shape(shape, by_cols):
    rows, cols = shape
    return (rows, N_CHIPS * cols) if by_cols else (N_CHIPS, rows, cols)


def _owner_rows(ref, shape, by_cols, slot, r0, rows):
    cols = shape[1]
    if by_cols:
        return ref.at[pl.ds(r0, rows), pl.ds(pl.multiple_of(slot * cols, LANES), cols)]
    return ref.at[slot, pl.ds(r0, rows), :]


def _allgather_weights(shards, by_cols, *, name):
    n = len(shards)

    def body(*refs):
        ins, outs = refs[:n], refs[n:2 * n]
        send, recv, loc = refs[2 * n:]
        x, y, c, chips = _place()
        me = 2 * x + y
        sib = (x, y, 1 - c)

        def half(w, slot, hc):
            h = shards[w].shape[0] // 2
            return _owner_rows(outs[w], shards[w].shape, by_cols[w], slot, hc * h, h)

        def copy(w, k, slot, hc, to, src=None):
            return pltpu.make_async_remote_copy(
                src_ref=half(w, slot, hc) if src is None else src, dst_ref=half(w, slot, hc),
                send_sem=send.at[6 * w + k], recv_sem=recv.at[6 * w + k], device_id=to, device_id_type=MESH)

        own = [pltpu.make_async_remote_copy(
            src_ref=ins[w], dst_ref=_owner_rows(outs[w], shards[w].shape, by_cols[w], me, 0, shards[w].shape[0]),
            send_sem=loc.at[w], recv_sem=loc.at[n + w], device_id=sib, device_id_type=MESH) for w in range(n)]
        for cp in own:
            cp.start()
        first = []
        for w in range(n):
            h = shards[w].shape[0] // 2
            for j, (tx, ty) in enumerate(chips):
                first.append(copy(w, j, me, c, (tx, ty, c), src=ins[w].at[pl.ds(c * h, h), :]))
                first[-1].start()
        passed = []
        for w in range(n):
            for j, (tx, ty) in enumerate(chips):
                slot = 2 * tx + ty
                copy(w, j, slot, c, (tx, ty, c)).wait_recv()
                passed.append(copy(w, 3 + j, slot, c, sib))
                passed[-1].start()
        for w in range(n):
            for j, (tx, ty) in enumerate(chips):
                copy(w, 3 + j, 2 * tx + ty, 1 - c, sib).wait_recv()
        for cp in first + passed:
            cp.wait_send()
        for cp in own:
            cp.wait()

    return pl.pallas_call(
        body, name=name, in_specs=[ANY] * n, out_specs=[ANY] * n,
        out_shape=[jax.ShapeDtypeStruct(_gathered_shape(s.shape, bc), s.dtype) for s, bc in zip(shards, by_cols)],
        scratch_shapes=[pltpu.SemaphoreType.DMA((6 * n,)), pltpu.SemaphoreType.DMA((6 * n,)),
                        pltpu.SemaphoreType.DMA((2 * n,))],
    )(*shards)


def _pair_exchange(grads, *, name):
    n = len(grads)

    def body(*refs):
        ins, outs = refs[:n], refs[n:2 * n]
        send, recv = refs[2 * n:]
        x, y, c, _ = _place()
        cps = []
        for w in range(n):
            h = grads[w].shape[1] // 2
            cps.append(pltpu.make_async_remote_copy(
                src_ref=ins[w].at[:, pl.ds((1 - c) * h, h), :], dst_ref=outs[w],
                send_sem=send.at[w], recv_sem=recv.at[w], device_id=(x, y, 1 - c), device_id_type=MESH))
            cps[-1].start()
        for cp in cps:
            cp.wait()

    return pl.pallas_call(
        body, name=name, in_specs=[ANY] * n, out_specs=[ANY] * n,
        out_shape=[jax.ShapeDtypeStruct((N_CHIPS, g.shape[1] // 2, g.shape[2]), g.dtype) for g in grads],
        scratch_shapes=[pltpu.SemaphoreType.DMA((n,)), pltpu.SemaphoreType.DMA((n,))],
    )(*grads)


def _pair_sum(g, got, *, name):
    S, R, C = g.shape
    h = R // 2
    tr = _row_block(h, 3 * C * 2, 16)
    nr = h // tr

    def body(a_ref, b_ref, o_ref):
        o_ref[...] = (a_ref[...].astype(F32) + b_ref[...].astype(F32)).astype(BF16)

    return pl.pallas_call(
        body, name=name, grid=(S, nr),
        in_specs=[pl.BlockSpec((None, tr, C), lambda s, r: (s, lax.axis_index("c") * nr + r, 0)),
                  pl.BlockSpec((None, tr, C), lambda s, r: (s, r, 0))],
        out_specs=pl.BlockSpec((None, tr, C), lambda s, r: (s, r, 0)),
        out_shape=jax.ShapeDtypeStruct((S, h, C), BF16),
        compiler_params=_params(("parallel", "parallel")),
    )(g, got)


def _chip_sum(part, got, *, name):
    S, h, C = part.shape
    tr = _row_block(h, 4 * C * 2 + C * 4, 16)
    nr = h // tr

    def body(own_ref, g0_ref, g1_ref, g2_ref, o_ref):
        acc = own_ref[...].astype(F32) + g0_ref[...].astype(F32)
        o_ref[...] = (acc + g1_ref[...].astype(F32)) + g2_ref[...].astype(F32)

    def piece(j):
        return pl.BlockSpec((None, tr, C), lambda r: (j, r, 0))

    return pl.pallas_call(
        body, name=name, grid=(nr,),
        in_specs=[pl.BlockSpec((None, tr, C), lambda r: (2 * lax.axis_index("x") + lax.axis_index("y"), r, 0)),
                  piece(0), piece(1), piece(2)],
        out_specs=pl.BlockSpec((tr, C), lambda r: (lax.axis_index("c") * nr + r, 0)),
        out_shape=jax.ShapeDtypeStruct((2 * h, C), F32),
        compiler_params=_params(("parallel",)),
    )(part, got, got, got)


def _pair_gather(totals, *, name):
    n = len(totals)

    def body(*refs):
        ins, outs = refs[:n], refs[n:2 * n]
        send, recv = refs[2 * n:]
        x, y, c, _ = _place()
        cps = []
        for w in range(n):
            h = totals[w].shape[0] // 2
            cps.append(pltpu.make_async_remote_copy(
                src_ref=ins[w].at[pl.ds(c * h, h), :], dst_ref=outs[w].at[pl.ds(c * h, h), :],
                send_sem=send.at[w], recv_sem=recv.at[w], device_id=(x, y, 1 - c), device_id_type=MESH))
            cps[-1].start()
        for w in range(n):
            h = totals[w].shape[0] // 2
            theirs = outs[w].at[pl.ds((1 - c) * h, h), :]
            pltpu.make_async_remote_copy(
                src_ref=theirs, dst_ref=theirs, send_sem=send.at[w], recv_sem=recv.at[w],
                device_id=(x, y, 1 - c), device_id_type=MESH).wait_recv()
        for cp in cps:
            cp.wait_send()

    return pl.pallas_call(
        body, name=name, in_specs=[ANY] * n, out_specs=[ANY] * n,
        out_shape=[jax.ShapeDtypeStruct(t.shape, t.dtype) for t in totals],
        input_output_aliases={w: w for w in range(n)},
        scratch_shapes=[pltpu.SemaphoreType.DMA((n,)), pltpu.SemaphoreType.DMA((n,))],
    )(*totals)


def _allreduce_small(v, *, name):
    R, C = v.shape
    ND = 8

    def body(v_ref, o_ref, all_ref, send, recv, loc):
        x, y, c, chips = _place()
        me, sib = (x, y, c), (x, y, 1 - c)

        def rows(px, py, pc):
            return all_ref.at[pl.ds((4 * px + 2 * py + pc) * R, R), :]

        def copy(k, block, to, src=None):
            return pltpu.make_async_remote_copy(
                src_ref=rows(*block) if src is None else src, dst_ref=rows(*block),
                send_sem=send.at[k], recv_sem=recv.at[k], device_id=to, device_id_type=MESH)

        mine = pltpu.make_async_copy(v_ref, rows(*me), loc)
        mine.start()
        first = [copy(0, me, sib, src=v_ref)]
        first += [copy(1 + j, me, (*chip, c), src=v_ref) for j, chip in enumerate(chips)]
        for cp in first:
            cp.start()
        passed = [copy(4 + j, (*chip, c), sib) for j, chip in enumerate(chips)]
        for j, chip in enumerate(chips):
            copy(1 + j, (*chip, c), me).wait_recv()
            passed[j].start()
        copy(0, sib, me).wait_recv()
        for j, chip in enumerate(chips):
            copy(4 + j, (*chip, 1 - c), me).wait_recv()
        for cp in first + passed:
            cp.wait_send()
        mine.wait()
        acc = all_ref[0:R, :]
        for d in range(1, ND):
            acc = acc + all_ref[d * R:(d + 1) * R, :]
        o_ref[...] = acc

    vm = pl.BlockSpec(memory_space=pltpu.VMEM)
    return pl.pallas_call(
        body, name=name, in_specs=[vm], out_specs=[vm, vm],
        out_shape=[jax.ShapeDtypeStruct((R, C), F32), jax.ShapeDtypeStruct((ND * R, C), F32)],
        scratch_shapes=[pltpu.SemaphoreType.DMA((7,)), pltpu.SemaphoreType.DMA((7,)), pltpu.SemaphoreType.DMA],
        compiler_params=pltpu.CompilerParams(vmem_limit_bytes=VMEM_LIMIT),
    )(v)[0]


HBM = pl.BlockSpec(memory_space=pltpu.HBM)
SEM = pl.BlockSpec(memory_space=pltpu.SEMAPHORE)
EFFECT = pltpu.SideEffectType.DATAFLOW_SIDE_EFFECTING


def _remote(src, dst, send, recv, k, to):
    return pltpu.make_async_remote_copy(src_ref=src, dst_ref=dst, send_sem=send.at[k], recv_sem=recv.at[k],
                                        device_id=to, device_id_type=MESH)


def _split_start(bufs, plan, ncopies, *, name, after=None):
    nb = len(bufs)
    extra = [] if after is None else [after]

    def body(*refs):
        pos = nb + len(extra)
        send, recv, token = refs[pos], refs[pos + 1], refs[-1]
        for k, (src, dst, to) in enumerate(plan(refs[:nb])):
            _remote(src, dst, send, recv, k, to).start()
        token[...] = jnp.zeros_like(token)

    outs = pl.pallas_call(
        body, name=name,
        out_shape=(pltpu.SemaphoreType.DMA((ncopies,)), pltpu.SemaphoreType.DMA((ncopies,)),
                   *[pltpu.HBM(b.shape, b.dtype) for b in bufs], jax.ShapeDtypeStruct((SUBLANES, LANES), F32)),
        in_specs=[HBM] * nb + [ANY] * len(extra),
        out_specs=(SEM, SEM, *[HBM] * nb, pl.BlockSpec(memory_space=pltpu.VMEM)),
        input_output_aliases={i: 2 + i for i in range(nb)},
        compiler_params=pltpu.CompilerParams(has_side_effects=EFFECT),
    )(*[pltpu.with_memory_space_constraint(b, pltpu.HBM) for b in bufs], *extra)
    return outs[0], outs[1], list(outs[2:2 + nb]), outs[-1]


def _split_wait(started, plan, after, *, name):
    send, recv, bufs, _ = started
    nb = len(bufs)

    def body(*refs):
        send_sem, recv_sem = refs[nb], refs[nb + 1]
        for k, (src, dst, to) in enumerate(plan(refs[:nb])):
            cp = _remote(src, dst, send_sem, recv_sem, k, to)
            cp.wait_send()
            cp.wait_recv()

    outs = pl.pallas_call(
        body, name=name,
        out_shape=tuple(pltpu.HBM(b.shape, b.dtype) for b in bufs),
        in_specs=[HBM] * nb + [SEM, SEM, ANY], out_specs=tuple([HBM] * nb),
        input_output_aliases={i: i for i in range(nb)},
        compiler_params=pltpu.CompilerParams(has_side_effects=EFFECT),
    )(*bufs, send, recv, after)
    return list(outs)


def _gather_chip_plan(shapes, by_cols):
    n = len(shapes)

    def plan(refs):
        srcs, lands = refs[:n], refs[n:]
        x, y, c, chips = _place()
        out = []
        for w in range(n):
            h = shapes[w][0] // 2
            mine = _owner_rows(lands[w], shapes[w], by_cols[w], 2 * x + y, c * h, h)
            for tx, ty in chips:
                out.append((srcs[w].at[pl.ds(c * h, h), :], mine, (tx, ty, c)))
        return out

    return plan


def _gather_pair_plan(shapes, by_cols):
    n = len(shapes)

    def plan(refs):
        srcs, lands = refs[:n], refs[n:]
        x, y, c, chips = _place()
        out = []
        for w in range(n):
            h = shapes[w][0] // 2
            for tx, ty in chips:
                half = _owner_rows(lands[w], shapes[w], by_cols[w], 2 * tx + ty, c * h, h)
                out.append((half, half, (x, y, 1 - c)))
            own = _owner_rows(lands[w], shapes[w], by_cols[w], 2 * x + y, 0, shapes[w][0])
            out.append((srcs[w], own, (x, y, 1 - c)))
        return out

    return plan


def _reduce_pair_plan(shapes):
    n = len(shapes)

    def plan(refs):
        local, lands = refs[:n], refs[n:]
        x, y, c, _ = _place()
        out = []
        for w in range(n):
            h = shapes[w][1] // 2
            out.append((local[w].at[:, pl.ds((1 - c) * h, h), :], lands[w], (x, y, 1 - c)))
        return out

    return plan


def _reduce_chip_plan(n):
    def plan(refs):
        parts, lands = refs[:n], refs[n:]
        x, y, c, chips = _place()
        return [(parts[w].at[2 * tx + ty], lands[w].at[j], (tx, ty, c))
                for w in range(n) for j, (tx, ty) in enumerate(chips)]

    return plan


def _as_operands(gathered):
    out = {}
    for n, g in gathered.items():
        if n in COL_SHARDED:
            out[n] = g
        elif n == "w_in":
            out[n] = jnp.transpose(g, (1, 0, 2)).reshape(D_MODEL, IN_COLS)
        else:
            out[n] = g.reshape(g.shape[0] * g.shape[1], g.shape[2])
    return out


def _by_owner(n, g):
    if n == "w_in":
        return jnp.transpose(g.reshape(D_MODEL, N_CHIPS, IN_COLS // N_CHIPS), (1, 0, 2))
    if g.ndim == 2:
        return g.reshape(N_CHIPS, g.shape[0] // N_CHIPS, g.shape[1])
    return g


class _Comm:
    def __init__(self, shards):
        self.shards = shards
        self.total = {}
        self._flight = {}

    def _layout(self, names):
        return [self.shards[n].shape for n in names], [n in COL_SHARDED for n in names]

    def gather_now(self, tag, names):
        _, by_cols = self._layout(names)
        got = _allgather_weights([self.shards[n] for n in names], by_cols, name=f"gather_{tag}")
        return _as_operands(dict(zip(names, got)))

    def gather_start(self, tag, names, after):
        shapes, by_cols = self._layout(names)
        srcs = [self.shards[n] for n in names]
        lands = [lax.empty(_gathered_shape(s.shape, bc), s.dtype) for s, bc in zip(srcs, by_cols)]
        started = _split_start(srcs + lands, _gather_chip_plan(shapes, by_cols), 3 * len(srcs),
                               after=after, name=f"gather_{tag}_chips_start")
        self._flight[tag] = (names, started)
        return started[3]

    def gather_landed(self, tag, after):
        names, started = self._flight[tag]
        shapes, by_cols = self._layout(names)
        bufs = _split_wait(started, _gather_chip_plan(shapes, by_cols), after, name=f"gather_{tag}_chips_wait")
        started = _split_start(bufs, _gather_pair_plan(shapes, by_cols), 4 * len(names),
                               name=f"gather_{tag}_pair_start")
        self._flight[tag] = (names, started)
        return started[3]

    def gather_ready(self, tag, after):
        names, started = self._flight.pop(tag)
        shapes, by_cols = self._layout(names)
        bufs = _split_wait(started, _gather_pair_plan(shapes, by_cols), after, name=f"gather_{tag}_pair_wait")
        return _as_operands(dict(zip(names, bufs[len(names):])))

    def reduce_start(self, tag, grads):
        names = list(grads)
        local = [_by_owner(n, grads[n]) for n in names]
        return self._chip_start(tag, names, local, _pair_exchange(local, name=f"pair_exchange_{tag}"))

    def reduce_pair_start(self, tag, grads):
        names = list(grads)
        local = [_by_owner(n, grads[n]) for n in names]
        lands = [lax.empty((N_CHIPS, g.shape[1] // 2, g.shape[2]), g.dtype) for g in local]
        started = _split_start(local + lands, _reduce_pair_plan([g.shape for g in local]), len(names),
                               name=f"pair_exchange_{tag}_start")
        self._flight[tag] = (names, started)
        return started[3]

    def reduce_pair_done(self, tag, after):
        names, started = self._flight.pop(tag)
        n = len(names)
        bufs = _split_wait(started, _reduce_pair_plan([b.shape for b in started[2][:n]]), after,
                           name=f"pair_exchange_{tag}_wait")
        return self._chip_start(tag, names, bufs[:n], bufs[n:])

    def _chip_start(self, tag, names, local, from_sib):
        parts = [_pair_sum(g, s, name=f"pair_sum_{n}") for n, g, s in zip(names, local, from_sib)]
        lands = [lax.empty((N_CHIPS - 1,) + p.shape[1:], p.dtype) for p in parts]
        self._flight[tag] = (names, _split_start(parts + lands, _reduce_chip_plan(len(names)), 3 * len(names),
                                                 name=f"chip_exchange_{tag}_start"))
        return self._flight[tag][1][3]

    def reduce_finish(self, tag, after):
        names, started = self._flight.pop(tag)
        n = len(names)
        bufs = _split_wait(started, _reduce_chip_plan(n), after, name=f"chip_exchange_{tag}_wait")
        totals = [_chip_sum(p, s, name=f"chip_sum_{nm}") for nm, p, s in zip(names, bufs[:n], bufs[n:])]
        self.total.update(zip(names, _pair_gather(totals, name=f"pair_gather_{tag}")))


def _adamw(w, g, m, v, *, name):
    R, C = w.shape
    tr = _row_block(R, 8 * C * 4, SUBLANES)

    def body(w_ref, g_ref, m_ref, v_ref, go_ref, d_ref, nm_ref, nv_ref):
        gg = g_ref[...]
        go_ref[...] = gg
        m_new = ADAM_B1 * m_ref[...] + (1.0 - ADAM_B1) * gg
        v_new = ADAM_B2 * v_ref[...] + (1.0 - ADAM_B2) * (gg * gg)
        m_hat = m_new / (1.0 - ADAM_B1 ** ADAM_STEP)
        v_hat = v_new / (1.0 - ADAM_B2 ** ADAM_STEP)
        d_ref[...] = -ADAM_LR * (m_hat / (jnp.sqrt(v_hat) + ADAM_EPS) + ADAM_WD * w_ref[...])
        nm_ref[...] = m_new
        nv_ref[...] = v_new

    blk = pl.BlockSpec((tr, C), lambda i: (i, 0))
    shp = jax.ShapeDtypeStruct((R, C), F32)
    return pl.pallas_call(
        body, name=name, grid=(R // tr,), in_specs=[blk] * 4, out_specs=[blk] * 4, out_shape=[shp] * 4,
        compiler_params=_params(("parallel",)),
    )(w, g, m, v)


def _to2d(a):
    flat = a.reshape(-1)
    pad = (-flat.shape[0]) % (SUBLANES * LANES)
    if pad:
        flat = jnp.pad(flat, (0, pad))
    return flat.reshape(-1, LANES)


def _small_rows(shape):
    return -(-math.prod(shape) // (SUBLANES * LANES)) * SUBLANES


def _pack_small(parts):
    rows = jnp.concatenate([_to2d(p) for p in parts], axis=0)
    pad = (-rows.shape[0]) % 256
    if pad:
        rows = jnp.concatenate([rows, jnp.zeros((pad, LANES), rows.dtype)], axis=0)
    return rows


def _unpack_small(rows, shapes):
    out, r = [], 0
    for shp in shapes:
        size = math.prod(shp)
        nrow = _small_rows(shp)
        out.append(rows[r:r + nrow].reshape(-1)[:size].reshape(shp))
        r += nrow
    return out


def kernel(x, mem, g_ffn1, w1_gate, w1_up, w1_down, g_mix, w_in, g_v, w_s, b_s, sinks, g_a_out, g_b_out, w_out, g_x, g_mem, w_xq, w_xkv, w_xo, g_ffn2, w2_gate, w2_up, w2_down, g_final, loss_target, m_g_ffn1, m_w1_gate, m_w1_up, m_w1_down, m_g_mix, m_w_in, m_g_v, m_w_s, m_b_s, m_sinks, m_g_a_out, m_g_b_out, m_w_out, m_g_x, m_g_mem, m_w_xq, m_w_xkv, m_w_xo, m_g_ffn2, m_w2_gate, m_w2_up, m_w2_down, m_g_final, v_g_ffn1, v_w1_gate, v_w1_up, v_w1_down, v_g_mix, v_w_in, v_g_v, v_w_s, v_b_s, v_sinks, v_g_a_out, v_g_b_out, v_w_out, v_g_x, v_g_mem, v_w_xq, v_w_xkv, v_w_xo, v_g_ffn2, v_w2_gate, v_w2_up, v_w2_down, v_g_final):
    args = dict(locals())
    Wp = {n: args[n] for n in ALL_W}
    Mp = {n: args["m_" + n] for n in ALL_W}
    Vp = {n: args["v_" + n] for n in ALL_W}

    comm = _Comm({n: Wp[n][0].astype(BF16) for n in BIG})
    W = {n: Wp[n] for n in SMALL}
    W["g_final"] = Wp["g_final"].reshape(1, D_MODEL)
    for n in ("w_s", "b_s"):
        W[n] = Wp[n][0]
    loss, dx, grads = _local_step(x[0], mem[0], loss_target[0], W, comm)
    big_grad = comm.total

    small_shapes = [Wp[n].shape for n in SMALL]
    packed = _pack_small([grads[n].reshape(Wp[n].shape) for n in SMALL] + [loss])
    summed = _allreduce_small(packed, name="allreduce_small")
    small_grad = dict(zip(SMALL, _unpack_small(summed, small_shapes)))
    nrows = sum(_small_rows(s) for s in small_shapes)
    loss_total = summed[nrows, 0]

    grad_out, delta, new_m, new_v = {}, {}, {}, {}
    for n in BIG:
        shp = Wp[n].shape
        g, d, nm, nv = _adamw(Wp[n][0], big_grad[n], Mp[n][0], Vp[n][0], name=f"adamw_{n}")
        grad_out[n], delta[n], new_m[n], new_v[n] = g.reshape(shp), d.reshape(shp), nm.reshape(shp), nv.reshape(shp)
    sw = _pack_small([Wp[n] for n in SMALL])
    sg = _pack_small([small_grad[n] for n in SMALL])
    sm = _pack_small([Mp[n] for n in SMALL])
    sv = _pack_small([Vp[n] for n in SMALL])
    _, d, nm, nv = _adamw(sw, sg, sm, sv, name="adamw_small")
    for n, dd, mm_, vv_ in zip(SMALL, _unpack_small(d, small_shapes), _unpack_small(nm, small_shapes),
                               _unpack_small(nv, small_shapes)):
        grad_out[n], delta[n], new_m[n], new_v[n] = small_grad[n], dd, mm_, vv_

    return (loss_total, dx[None], *[grad_out[n] for n in ALL_W], *[delta[n] for n in ALL_W],
            *[new_m[n] for n in ALL_W], *[new_v[n] for n in ALL_W])
```

```python
import functools
import math

import jax
import jax.numpy as jnp
from jax import lax
from jax.experimental import pallas as pl
from jax.experimental.pallas import tpu as pltpu

F32 = jnp.float32
BF16 = jnp.bfloat16
MESH = pl.DeviceIdType.MESH

D_MODEL = 2048
D_FF = 5632
D_A = 1024
D_B = 1024
CHUNK = 128
A_GROUPS = 8
HEAD_DIM = 64
B_Q_HEADS = 16
B_KV_HEADS = 2
X_HEADS = 4
X_HEAD_DIM = 512
IN_COLS = 3328
O_Q = 2 * D_A
O_K = O_Q + D_B
O_V = O_K + B_KV_HEADS * HEAD_DIM
N_CHIPS = 4
EPS = 1e-5
NEG = -1e30
ADAM_LR = 0.001
ADAM_B1 = 0.9
ADAM_B2 = 0.999
ADAM_EPS = 1e-08
ADAM_WD = 0.01
ADAM_STEP = 10

V7X_VMEM_BYTES = 64 * 1024 * 1024
VMEM_LIMIT = 56 * 1024 * 1024
LANES = 128
SUBLANES = 8


ANY = pl.BlockSpec(memory_space=pl.ANY)


def _params(sem, vmem=VMEM_LIMIT):
    return pltpu.CompilerParams(dimension_semantics=sem, vmem_limit_bytes=vmem)


def _matmul(pairs, *, M, N, K, tm, tn, tk, a_t=False, b_kind="n", out_kind="n", out_dtype=F32,
            scale=1.0, res=None, norm_g=None, order="ij", dep=None, name):
    tm, tn, tk = min(tm, M), min(tn, N), min(tk, K)
    assert M % tm == 0 and N % tn == 0 and K % tk == 0, (name, M, N, K, tm, tn, tk)
    nk = K // tk
    npairs = len(pairs)
    b_t = b_kind == "t"
    ns = N // N_CHIPS

    def ij(g0, g1):
        return (g0, g1) if order == "ij" else (g1, g0)

    def a_map(g0, g1, k):
        i, _ = ij(g0, g1)
        return (k, i) if a_t else (i, k)

    a_spec = pl.BlockSpec((tk, tm) if a_t else (tm, tk), a_map)

    if b_kind == "n":
        b_spec = pl.BlockSpec((tk, tn), lambda g0, g1, k: (k, ij(g0, g1)[1]))
    else:
        b_spec = pl.BlockSpec((tn, tk), lambda g0, g1, k: (ij(g0, g1)[1], k))

    if out_kind == "n":
        o_spec = pl.BlockSpec((tm, tn), lambda g0, g1, k: ij(g0, g1))
        o_shape = jax.ShapeDtypeStruct((M, N), out_dtype)
    else:
        assert tn % ns == 0
        o_spec = pl.BlockSpec((tn // ns, tm, ns), lambda g0, g1, k: (ij(g0, g1)[1], ij(g0, g1)[0], 0))
        o_shape = jax.ShapeDtypeStruct((N_CHIPS, M, ns), out_dtype)

    in_specs, args = [], []
    for a, b in pairs:
        in_specs += [a_spec, b_spec]
        args += [a, b]
    if res is not None:
        in_specs.append(pl.BlockSpec((tm, tn), lambda g0, g1, k: ij(g0, g1)))
        args.append(res)
    if norm_g is not None:
        assert tn == N and out_kind == "n"
        in_specs.append(pl.BlockSpec((1, N), lambda g0, g1, k: (0, 0)))
        args.append(norm_g)
    if dep is not None:
        in_specs.append(ANY)
        args.append(dep)

    dn = (((0,) if a_t else (1,), (1,) if b_t else (0,)), ((), ()))

    def body(*refs):
        pos = 2 * npairs
        res_ref = refs[pos] if res is not None else None
        pos += res is not None
        g_ref = refs[pos] if norm_g is not None else None
        pos += (norm_g is not None) + (dep is not None)
        o_ref = refs[pos]
        n_ref = refs[pos + 1] if norm_g is not None else None
        acc_ref = refs[-1] if nk > 1 else None
        part = None
        for p in range(npairs):
            d = lax.dot_general(refs[2 * p][...], refs[2 * p + 1][...], dn, preferred_element_type=F32)
            part = d if part is None else part + d

        def finish(acc):
            r = acc * scale if scale != 1.0 else acc
            if res_ref is not None:
                r = res_ref[...] + r
            if out_kind == "n":
                o_ref[...] = r.astype(out_dtype)
            else:
                for s in range(tn // ns):
                    o_ref[s] = r[:, s * ns:(s + 1) * ns].astype(out_dtype)
            if n_ref is not None:
                n_ref[...] = (r * _rstd(r) * g_ref[...]).astype(BF16)

        if nk == 1:
            finish(part)
        else:
            k = pl.program_id(2)

            @pl.when(k == 0)
            def _():
                acc_ref[...] = part

            @pl.when((k > 0) & (k < nk - 1))
            def _():
                acc_ref[...] += part

            @pl.when(k == nk - 1)
            def _():
                finish(acc_ref[...] + part)

    grid = (M // tm, N // tn, nk) if order == "ij" else (N // tn, M // tm, nk)
    out_specs, out_shape = o_spec, o_shape
    if norm_g is not None:
        out_specs = [o_spec, pl.BlockSpec((tm, tn), lambda g0, g1, k: ij(g0, g1))]
        out_shape = [o_shape, jax.ShapeDtypeStruct((M, N), BF16)]
    return pl.pallas_call(
        body, name=name, grid=grid, in_specs=in_specs, out_specs=out_specs, out_shape=out_shape,
        scratch_shapes=[pltpu.VMEM((tm, tn), F32)] if nk > 1 else [],
        compiler_params=_params(("parallel", "parallel", "arbitrary")),
    )(*args)


def _rstd(x):
    return lax.rsqrt(jnp.mean(x * x, axis=-1, keepdims=True) + EPS)


def _rms_bwd_math(x, g, dy):
    r = _rstd(x)
    gy = dy * g
    xr = x * r
    dx = r * (gy - xr * jnp.mean(gy * xr, axis=-1, keepdims=True))
    return dx, dy * xr


def _rms_fwd(h, g, *, name, tm=512, dep=None):
    T, Dm = h.shape
    tm = min(tm, T)

    def body(h_ref, g_ref, *rest):
        x = h_ref[...]
        rest[-1][...] = (x * _rstd(x) * g_ref[...]).astype(BF16)

    return pl.pallas_call(
        body, name=name, grid=(T // tm,),
        in_specs=[pl.BlockSpec((tm, Dm), lambda i: (i, 0)), pl.BlockSpec((1, Dm), lambda i: (0, 0))]
        + ([ANY] if dep is not None else []),
        out_specs=pl.BlockSpec((tm, Dm), lambda i: (i, 0)),
        out_shape=jax.ShapeDtypeStruct((T, Dm), BF16),
        compiler_params=_params(("parallel",)),
    )(h, g, *([dep] if dep is not None else []))


def _rms_bwd(h, g, dn, dres, *, name, tm=256, dep=None):
    T, Dm = h.shape
    tm = min(tm, T)
    has_res = dres is not None

    def body(*refs):
        h_ref, g_ref, dn_ref = refs[:3]
        pos = 3
        dres_ref = refs[pos] if has_res else None
        pos += has_res + (dep is not None)
        dh_ref, dhb_ref, dg_ref = refs[pos:pos + 3]
        dx, dgr = _rms_bwd_math(h_ref[...], g_ref[...], dn_ref[...].astype(F32))
        if has_res:
            dx = dres_ref[...] + dx
        dh_ref[...] = dx
        dhb_ref[...] = dx.astype(BF16)
        part = jnp.sum(dgr, axis=0, keepdims=True)

        @pl.when(pl.program_id(0) == 0)
        def _():
            dg_ref[...] = part

        @pl.when(pl.program_id(0) > 0)
        def _():
            dg_ref[...] += part

    row = pl.BlockSpec((tm, Dm), lambda i: (i, 0))
    vec = pl.BlockSpec((1, Dm), lambda i: (0, 0))
    args = [h, g, dn] + ([dres] if has_res else []) + ([dep] if dep is not None else [])
    return pl.pallas_call(
        body, name=name, grid=(T // tm,),
        in_specs=[row, vec, row] + ([row] if has_res else []) + ([ANY] if dep is not None else []),
        out_specs=[row, row, vec],
        out_shape=[jax.ShapeDtypeStruct((T, Dm), F32), jax.ShapeDtypeStruct((T, Dm), BF16),
                   jax.ShapeDtypeStruct((1, Dm), F32)],
        compiler_params=_params(("arbitrary",)),
    )(*args)


def _loss_head(h, g, tgt, *, name, tm=256):
    T, Dm = h.shape
    tm = min(tm, T)

    def body(h_ref, g_ref, t_ref, dh_ref, dhb_ref, dg_ref, loss_ref):
        x = h_ref[...]
        gv = g_ref[...]
        r = _rstd(x)
        diff = x * r * gv - t_ref[...]
        lpart = 0.5 * jnp.sum(jnp.mean(diff * diff, axis=-1, keepdims=True), axis=0, keepdims=True)
        dx, dgr = _rms_bwd_math(x, gv, diff * (1.0 / Dm))
        dh_ref[...] = dx
        dhb_ref[...] = dx.astype(BF16)
        part = jnp.sum(dgr, axis=0, keepdims=True)
        lrow = jnp.broadcast_to(lpart, (1, LANES))

        @pl.when(pl.program_id(0) == 0)
        def _():
            dg_ref[...] = part
            loss_ref[...] = lrow

        @pl.when(pl.program_id(0) > 0)
        def _():
            dg_ref[...] += part
            loss_ref[...] += lrow

    row = pl.BlockSpec((tm, Dm), lambda i: (i, 0))
    vec = pl.BlockSpec((1, Dm), lambda i: (0, 0))
    return pl.pallas_call(
        body, name=name, grid=(T // tm,),
        in_specs=[row, vec, row],
        out_specs=[row, row, vec, pl.BlockSpec((1, LANES), lambda i: (0, 0))],
        out_shape=[jax.ShapeDtypeStruct((T, Dm), F32), jax.ShapeDtypeStruct((T, Dm), BF16),
                   jax.ShapeDtypeStruct((1, Dm), F32), jax.ShapeDtypeStruct((1, LANES), F32)],
        compiler_params=_params(("arbitrary",)),
    )(h, g, tgt)


MXU_COLS = 256
FF_TILE = 2 * MXU_COLS


def _row_block(rows, row_bytes, align, budget=24 * 1024 * 1024):
    fits = [d for d in range(align, rows + 1, align) if rows % d == 0 and 2 * d * row_bytes <= budget]
    assert fits, (rows, row_bytes)
    return fits[-1]


def _swiglu_up(n, wg, wu, *, name, tm=1024, tn=FF_TILE):
    T, Dm = n.shape
    Fd = wg.shape[1]
    tm = min(tm, T)

    def body(n_ref, wg_ref, wu_ref, g_ref, u_ref, a_ref):
        x = n_ref[...]
        g = jnp.dot(x, wg_ref[...], preferred_element_type=F32)
        u = jnp.dot(x, wu_ref[...], preferred_element_type=F32)
        g_ref[...] = g.astype(BF16)
        u_ref[...] = u.astype(BF16)
        a_ref[...] = (g * jax.nn.sigmoid(g) * u).astype(BF16)

    wspec = pl.BlockSpec((Dm, tn), lambda j, i: (0, j))
    ospec = pl.BlockSpec((tm, tn), lambda j, i: (i, j))
    oshape = jax.ShapeDtypeStruct((T, Fd), BF16)
    return pl.pallas_call(
        body, name=name, grid=(Fd // tn, T // tm),
        in_specs=[pl.BlockSpec((tm, Dm), lambda j, i: (i, 0)), wspec, wspec],
        out_specs=[ospec, ospec, ospec], out_shape=[oshape, oshape, oshape],
        compiler_params=_params(("parallel", "parallel")),
    )(n, wg, wu)


def _swiglu_bwd_act(dhb, wd, G, U, *, name, tm=1024, tn=D_FF // N_CHIPS):
    T, Dm = dhb.shape
    Fd = wd.shape[0]
    tm, tn = min(tm, T), min(tn, Fd)

    def body(dh_ref, wd_ref, g_ref, u_ref, dg_ref, du_ref):
        da = 0.5 * lax.dot_general(dh_ref[...], wd_ref[...], (((1,), (1,)), ((), ())), preferred_element_type=F32)
        g = g_ref[...].astype(F32)
        u = u_ref[...].astype(F32)
        sg = jax.nn.sigmoid(g)
        dg_ref[...] = (da * u * (sg * (1.0 + g * (1.0 - sg)))).astype(BF16)
        du_ref[...] = (da * (g * sg)).astype(BF16)

    blk = pl.BlockSpec((tm, tn), lambda j, i: (i, j))
    oshape = jax.ShapeDtypeStruct((T, Fd), BF16)
    return pl.pallas_call(
        body, name=name, grid=(Fd // tn, T // tm),
        in_specs=[pl.BlockSpec((tm, Dm), lambda j, i: (i, 0)), pl.BlockSpec((tn, Dm), lambda j, i: (j, 0)), blk, blk],
        out_specs=[blk, blk], out_shape=[oshape, oshape],
        compiler_params=_params(("parallel", "parallel")),
    )(dhb, wd, G, U)


_INV_SQRT2 = 0.7071067811865476
_INV_SQRT2PI = 0.3989422804014327


def _erf(x):
    ax = jnp.abs(x)
    t = 1.0 / (1.0 + 0.3275911 * ax)
    poly = t * (0.254829592 + t * (-0.284496736 + t * (1.421413741 + t * (-1.453152027 + t * 1.061405429))))
    y = 1.0 - poly * jnp.exp(-ax * ax)
    return jnp.where(x < 0, -y, y)


def _gelu_cdf(x):
    return 0.5 * (1.0 + _erf(x * _INV_SQRT2))


def _lane_lt64(shape):
    return lax.broadcasted_iota(jnp.int32, shape, len(shape) - 1) < HEAD_DIM


def _dup_half(x, kv):
    rolled = pltpu.roll(x, HEAD_DIM, 1)
    lo = _lane_lt64(x.shape)
    return jnp.where(lo, x, rolled) if kv == 0 else jnp.where(lo, rolled, x)


HEADS_PER_KV = B_Q_HEADS // B_KV_HEADS
PAIRS = HEADS_PER_KV // 2


def _attn_bias():
    shape = (HEADS_PER_KV * CHUNK, 2 * CHUNK)
    qpos = (lax.broadcasted_iota(jnp.int32, shape, 0) & (CHUNK - 1)) + CHUNK
    kpos = lax.broadcasted_iota(jnp.int32, shape, 1)
    diff = qpos - kpos
    band = (diff >= 0) & (diff < CHUNK)
    return jnp.stack([jnp.where(band & (kpos >= CHUNK), 0.0, NEG), jnp.where(band, 0.0, NEG)]).astype(F32)


def _stack_heads(tiles, lo):
    parts = []
    for t in tiles:
        parts += [jnp.where(lo, t, 0.0), jnp.where(lo, 0.0, t)]
    return jnp.concatenate(parts, axis=0)


def _unstack_heads(s, lo):
    return [jnp.where(lo, s[2 * p * CHUNK:(2 * p + 1) * CHUNK], s[(2 * p + 1) * CHUNK:(2 * p + 2) * CHUNK])
            for p in range(PAIRS)]


def _stack_sinks(sk_ref, kv):
    return jnp.concatenate([jnp.broadcast_to(sk_ref[:, h:h + 1], (CHUNK, 1))
                            for h in range(kv * HEADS_PER_KV, (kv + 1) * HEADS_PER_KV)], axis=0)


def _sgu_forward(z_ref, gv, wsm, bst):
    zu = z_ref[:, 0:D_A]
    zv = z_ref[:, D_A:2 * D_A]
    u = zu * _gelu_cdf(zu)
    v = zv * _gelu_cdf(zv)
    rv = _rstd(v)
    vn = (v * rv * gv).astype(BF16)
    svs = []
    for g in range(A_GROUPS):
        sl = slice(g * CHUNK, (g + 1) * CHUNK)
        svs.append(jnp.dot(wsm[g], vn[:, sl], preferred_element_type=F32) + bst[:, g:g + 1])
    sv = jnp.concatenate(svs, axis=1)
    return zu, zv, u, v, rv, vn, sv


def _masked_ws(ws_ref):
    tril = lax.broadcasted_iota(jnp.int32, (CHUNK, CHUNK), 0) >= lax.broadcasted_iota(jnp.int32, (CHUNK, CHUNK), 1)
    return [jnp.where(tril, ws_ref[g], 0.0).astype(BF16) for g in range(A_GROUPS)], tril


def _attn_probs(qm, kkd, sink, bias):
    s = lax.dot_general(qm, kkd, (((1,), (1,)), ((), ())), preferred_element_type=F32) * (HEAD_DIM ** -0.5) + bias
    m = jnp.maximum(jnp.max(s, axis=-1, keepdims=True), sink)
    e = jnp.exp(s - m)
    es = jnp.exp(sink - m)
    inv = 1.0 / (jnp.sum(e, axis=-1, keepdims=True) + es)
    return e * inv, es * inv


def _mixer_fwd(z, gv, ws, bst, sinks, ga, gb, *, name):
    T = z.shape[0]
    nb = T // CHUNK
    kvb = O_K // (2 * CHUNK)

    def body(z_ref, zp_ref, bias_ref, gv_ref, ws_ref, bst_ref, sk_ref, ga_ref, gb_ref, o_ref, p_ref):
        wsm, _ = _masked_ws(ws_ref)
        _, _, u, _, _, _, sv = _sgu_forward(z_ref, gv_ref[...], wsm, bst_ref[...])
        ya = u * sv
        o_ref[:, 0:D_A] = (ya * _rstd(ya) * ga_ref[...]).astype(BF16)

        mask = bias_ref[...]
        kk = jnp.concatenate([zp_ref[:, 0:CHUNK], z_ref[:, O_K:O_V]], axis=0)
        vv = jnp.concatenate([zp_ref[:, CHUNK:2 * CHUNK], z_ref[:, O_V:IN_COLS]], axis=0)
        lo = _lane_lt64((CHUNK, LANES))
        outs = []
        for kv in range(B_KV_HEADS):
            kkd = _dup_half(kk, kv).astype(BF16)
            vvd = _dup_half(vv, kv).astype(BF16)
            q = _stack_heads([z_ref[:, O_Q + (kv * PAIRS + pr) * LANES:O_Q + (kv * PAIRS + pr + 1) * LANES]
                              for pr in range(PAIRS)], lo).astype(BF16)
            p, _ = _attn_probs(q, kkd, _stack_sinks(sk_ref, kv), mask)
            p_ref[kv] = p
            outs += _unstack_heads(jnp.dot(p.astype(BF16), vvd, preferred_element_type=F32), lo)
        yb = jnp.concatenate(outs, axis=1)
        o_ref[:, D_A:D_A + D_B] = (yb * _rstd(yb) * gb_ref[...]).astype(BF16)

    full = lambda shape: pl.BlockSpec(shape, lambda i: (0,) * len(shape))
    pshape = (B_KV_HEADS, HEADS_PER_KV * CHUNK, 2 * CHUNK)
    return pl.pallas_call(
        body, name=name, grid=(nb,),
        in_specs=[pl.BlockSpec((CHUNK, IN_COLS), lambda i: (i, 0)),
                  pl.BlockSpec((CHUNK, 2 * CHUNK), lambda i: (jnp.maximum(i - 1, 0), kvb)),
                  pl.BlockSpec((None, HEADS_PER_KV * CHUNK, 2 * CHUNK), lambda i: (jnp.minimum(i, 1), 0, 0)),
                  full((1, D_A)), full((A_GROUPS, CHUNK, CHUNK)), full((CHUNK, A_GROUPS)), full((1, B_Q_HEADS)),
                  full((1, D_A)), full((1, D_B))],
        out_specs=[pl.BlockSpec((CHUNK, D_A + D_B), lambda i: (i, 0)),
                   pl.BlockSpec((None,) + pshape, lambda i: (i, 0, 0, 0))],
        out_shape=[jax.ShapeDtypeStruct((T, D_A + D_B), BF16), jax.ShapeDtypeStruct((nb,) + pshape, F32)],
        compiler_params=_params(("parallel",)),
    )(z, z, _attn_bias(), gv, ws, bst, sinks, ga, gb)


def _mixer_bwd(z, dyn, probs, gv, ws, bst, ga, gb, *, name):
    T = z.shape[0]
    nb = T // CHUNK
    kvb = O_K // (2 * CHUNK)
    NT = (((0,), (0,)), ((), ()))

    def body(z_ref, zp_ref, dy_ref, p_ref, gv_ref, ws_ref, bst_ref, ga_ref, gb_ref,
             dz_ref, dgv_ref, dws_ref, dbst_ref, dsk_ref, dga_ref, dgb_ref, carry_ref):
        step = pl.program_id(0)

        @pl.when(step == 0)
        def _():
            carry_ref[...] = jnp.zeros_like(carry_ref)
            dgv_ref[...] = jnp.zeros_like(dgv_ref)
            dws_ref[...] = jnp.zeros_like(dws_ref)
            dbst_ref[...] = jnp.zeros_like(dbst_ref)
            dsk_ref[...] = jnp.zeros_like(dsk_ref)
            dga_ref[...] = jnp.zeros_like(dga_ref)
            dgb_ref[...] = jnp.zeros_like(dgb_ref)

        wsm, tril = _masked_ws(ws_ref)
        gvv = gv_ref[...]
        zu, zv, u, v, rv, vn, sv = _sgu_forward(z_ref, gvv, wsm, bst_ref[...])
        ya = u * sv
        dya, dga_rows = _rms_bwd_math(ya, ga_ref[...], dy_ref[:, 0:D_A].astype(F32))
        dga_ref[...] += jnp.sum(dga_rows, axis=0, keepdims=True)
        du = dya * sv
        dsv = dya * u
        dvn_parts = []
        for g in range(A_GROUPS):
            sl = slice(g * CHUNK, (g + 1) * CHUNK)
            dsv_g = dsv[:, sl]
            dsv_gb = dsv_g.astype(BF16)
            dw = lax.dot_general(dsv_gb, vn[:, sl], (((1,), (1,)), ((), ())), preferred_element_type=F32)
            dws_ref[g] += jnp.where(tril, dw, 0.0)
            dbst_ref[:, g:g + 1] += jnp.sum(dsv_g, axis=1, keepdims=True)
            dvn_parts.append(lax.dot_general(wsm[g], dsv_gb, NT, preferred_element_type=F32))
        dvn = jnp.concatenate(dvn_parts, axis=1)
        dv, dgv_rows = _rms_bwd_math(v, gvv, dvn)
        dgv_ref[...] += jnp.sum(dgv_rows, axis=0, keepdims=True)
        dz_ref[:, 0:D_A] = (du * (_gelu_cdf(zu) + zu * jnp.exp(-0.5 * zu * zu) * _INV_SQRT2PI)).astype(BF16)
        dz_ref[:, D_A:2 * D_A] = (dv * (_gelu_cdf(zv) + zv * jnp.exp(-0.5 * zv * zv) * _INV_SQRT2PI)).astype(BF16)

        kk = jnp.concatenate([zp_ref[:, 0:CHUNK], z_ref[:, O_K:O_V]], axis=0)
        vv = jnp.concatenate([zp_ref[:, CHUNK:2 * CHUNK], z_ref[:, O_V:IN_COLS]], axis=0)
        lo = _lane_lt64((CHUNK, LANES))
        kkd = [_dup_half(kk, kv).astype(BF16) for kv in range(B_KV_HEADS)]
        vvd = [_dup_half(vv, kv).astype(BF16) for kv in range(B_KV_HEADS)]
        outs = []
        for kv in range(B_KV_HEADS):
            outs += _unstack_heads(jnp.dot(p_ref[kv].astype(BF16), vvd[kv], preferred_element_type=F32), lo)
        yb = jnp.concatenate(outs, axis=1)
        dyb, dgb_rows = _rms_bwd_math(yb, gb_ref[...], dy_ref[:, D_A:D_A + D_B].astype(F32))
        dgb_ref[...] += jnp.sum(dgb_rows, axis=0, keepdims=True)

        dkk, dvv = [], []
        for kv in range(B_KV_HEADS):
            do = _stack_heads([dyb[:, (kv * PAIRS + pr) * LANES:(kv * PAIRS + pr + 1) * LANES]
                               for pr in range(PAIRS)], lo).astype(BF16)
            q = _stack_heads([z_ref[:, O_Q + (kv * PAIRS + pr) * LANES:O_Q + (kv * PAIRS + pr + 1) * LANES]
                              for pr in range(PAIRS)], lo).astype(BF16)
            p = p_ref[kv]
            dvv.append(lax.dot_general(p.astype(BF16), do, NT, preferred_element_type=F32))
            dp = lax.dot_general(do, vvd[kv], (((1,), (1,)), ((), ())), preferred_element_type=F32)
            delta = jnp.sum(p * dp, axis=-1, keepdims=True)
            dsink = (jnp.sum(p, axis=-1, keepdims=True) - 1.0) * delta
            for g in range(HEADS_PER_KV):
                h = kv * HEADS_PER_KV + g
                dsk_ref[:, h:h + 1] += jnp.sum(dsink[g * CHUNK:(g + 1) * CHUNK], axis=0, keepdims=True)
            ds = (p * (dp - delta) * (HEAD_DIM ** -0.5)).astype(BF16)
            dq = _unstack_heads(jnp.dot(ds, kkd[kv], preferred_element_type=F32), lo)
            for pr in range(PAIRS):
                c0 = O_Q + (kv * PAIRS + pr) * LANES
                dz_ref[:, c0:c0 + LANES] = dq[pr].astype(BF16)
            dkk.append(lax.dot_general(ds, q, NT, preferred_element_type=F32))

        def fold(parts):
            tot = [t + pltpu.roll(t, HEAD_DIM, 1) for t in parts]
            return jnp.where(_lane_lt64(tot[0].shape), tot[0], tot[1])

        dk_all = fold(dkk)
        dv_all = fold(dvv)
        dz_ref[:, O_K:O_V] = (dk_all[CHUNK:] + carry_ref[:, 0:CHUNK]).astype(BF16)
        dz_ref[:, O_V:IN_COLS] = (dv_all[CHUNK:] + carry_ref[:, CHUNK:2 * CHUNK]).astype(BF16)
        carry_ref[:, 0:CHUNK] = dk_all[:CHUNK]
        carry_ref[:, CHUNK:2 * CHUNK] = dv_all[:CHUNK]

    full = lambda shape: pl.BlockSpec(shape, lambda s: (0,) * len(shape))
    rev = lambda s: nb - 1 - s
    return pl.pallas_call(
        body, name=name, grid=(nb,),
        in_specs=[pl.BlockSpec((CHUNK, IN_COLS), lambda s: (rev(s), 0)),
                  pl.BlockSpec((CHUNK, 2 * CHUNK), lambda s: (jnp.maximum(rev(s) - 1, 0), kvb)),
                  pl.BlockSpec((CHUNK, D_A + D_B), lambda s: (rev(s), 0)),
                  pl.BlockSpec((None, B_KV_HEADS, HEADS_PER_KV * CHUNK, 2 * CHUNK), lambda s: (rev(s), 0, 0, 0)),
                  full((1, D_A)), full((A_GROUPS, CHUNK, CHUNK)), full((CHUNK, A_GROUPS)),
                  full((1, D_A)), full((1, D_B))],
        out_specs=[pl.BlockSpec((CHUNK, IN_COLS), lambda s: (rev(s), 0)),
                   full((1, D_A)), full((A_GROUPS, CHUNK, CHUNK)), full((CHUNK, A_GROUPS)), full((1, B_Q_HEADS)),
                   full((1, D_A)), full((1, D_B))],
        out_shape=[jax.ShapeDtypeStruct((T, IN_COLS), BF16), jax.ShapeDtypeStruct((1, D_A), F32),
                   jax.ShapeDtypeStruct((A_GROUPS, CHUNK, CHUNK), F32), jax.ShapeDtypeStruct((CHUNK, A_GROUPS), F32),
                   jax.ShapeDtypeStruct((1, B_Q_HEADS), F32), jax.ShapeDtypeStruct((1, D_A), F32),
                   jax.ShapeDtypeStruct((1, D_B), F32)],
        scratch_shapes=[pltpu.VMEM((CHUNK, 2 * CHUNK), F32)],
        compiler_params=_params(("arbitrary",)),
    )(z, z, dyn, probs, gv, ws, bst, ga, gb)


def _xattn_probs(qh, kh):
    s = lax.dot_general(qh, kh, (((1,), (1,)), ((), ())), preferred_element_type=F32) * (X_HEAD_DIM ** -0.5)
    e = jnp.exp(s - jnp.max(s, axis=-1, keepdims=True))
    return e / jnp.sum(e, axis=-1, keepdims=True)


def _xattn_fwd(q, kvm, *, name, tm=512):
    T = q.shape[0]
    Mm = kvm.shape[0]
    tm = min(tm, T)

    def body(q_ref, kv_ref, o_ref, p_ref):
        for h in range(X_HEADS):
            sl = slice(h * X_HEAD_DIM, (h + 1) * X_HEAD_DIM)
            kh = kv_ref[:, sl].astype(BF16)
            vh = kv_ref[:, D_MODEL + h * X_HEAD_DIM:D_MODEL + (h + 1) * X_HEAD_DIM].astype(BF16)
            p = _xattn_probs(q_ref[:, sl], kh)
            p_ref[:, h * Mm:(h + 1) * Mm] = p
            o_ref[:, sl] = jnp.dot(p.astype(BF16), vh, preferred_element_type=F32).astype(BF16)

    return pl.pallas_call(
        body, name=name, grid=(T // tm,),
        in_specs=[pl.BlockSpec((tm, D_MODEL), lambda i: (i, 0)), pl.BlockSpec((Mm, 2 * D_MODEL), lambda i: (0, 0))],
        out_specs=[pl.BlockSpec((tm, D_MODEL), lambda i: (i, 0)), pl.BlockSpec((tm, X_HEADS * Mm), lambda i: (i, 0))],
        out_shape=[jax.ShapeDtypeStruct((T, D_MODEL), BF16), jax.ShapeDtypeStruct((T, X_HEADS * Mm), F32)],
        compiler_params=_params(("parallel",)),
    )(q, kvm)


def _xattn_bwd(q, kvm, probs, do, *, name, tm=512):
    T = q.shape[0]
    Mm = kvm.shape[0]
    tm = min(tm, T)
    NT = (((0,), (0,)), ((), ()))

    def body(q_ref, kv_ref, p_ref, do_ref, dq_ref, dkv_ref):
        @pl.when(pl.program_id(0) == 0)
        def _():
            dkv_ref[...] = jnp.zeros_like(dkv_ref)

        for h in range(X_HEADS):
            sl = slice(h * X_HEAD_DIM, (h + 1) * X_HEAD_DIM)
            slv = slice(D_MODEL + h * X_HEAD_DIM, D_MODEL + (h + 1) * X_HEAD_DIM)
            kh = kv_ref[:, sl].astype(BF16)
            vh = kv_ref[:, slv].astype(BF16)
            qh = q_ref[:, sl]
            doh = do_ref[:, sl]
            p = p_ref[:, h * Mm:(h + 1) * Mm]
            dkv_ref[:, slv] += lax.dot_general(p.astype(BF16), doh, NT, preferred_element_type=F32)
            dp = lax.dot_general(doh, vh, (((1,), (1,)), ((), ())), preferred_element_type=F32)
            ds = (p * (dp - jnp.sum(p * dp, axis=-1, keepdims=True)) * (X_HEAD_DIM ** -0.5)).astype(BF16)
            dq_ref[:, sl] = jnp.dot(ds, kh, preferred_element_type=F32).astype(BF16)
            dkv_ref[:, sl] += lax.dot_general(ds, qh, NT, preferred_element_type=F32)

    row = pl.BlockSpec((tm, D_MODEL), lambda i: (i, 0))
    kvs = pl.BlockSpec((Mm, 2 * D_MODEL), lambda i: (0, 0))
    return pl.pallas_call(
        body, name=name, grid=(T // tm,),
        in_specs=[row, kvs, pl.BlockSpec((tm, X_HEADS * Mm), lambda i: (i, 0)), row], out_specs=[row, kvs],
        out_shape=[jax.ShapeDtypeStruct((T, D_MODEL), BF16), jax.ShapeDtypeStruct((Mm, 2 * D_MODEL), F32)],
        compiler_params=_params(("arbitrary",)),
    )(q, kvm, probs, do)


def _swiglu_bwd_weights(tag, n, G, U, A, wd, dhb):
    T = n.shape[0]
    dG, dU = _swiglu_bwd_act(dhb, wd, G, U, name=f"{tag}_bwd_act", tm=1024)
    dwd = _matmul([(A, dhb)], M=D_FF, N=D_MODEL, K=T, tm=1408, tn=1024, tk=2048, a_t=True, out_dtype=BF16,
                  scale=0.5, name=f"{tag}_dwd")
    dwg = _matmul([(n, dG)], M=D_MODEL, N=D_FF, K=T, tm=512, tn=D_FF // 2, tk=2048, a_t=True, out_kind="s",
                  out_dtype=BF16, order="ji", name=f"{tag}_dwg")
    dwu = _matmul([(n, dU)], M=D_MODEL, N=D_FF, K=T, tm=512, tn=D_FF // 2, tk=2048, a_t=True, out_kind="s",
                  out_dtype=BF16, order="ji", name=f"{tag}_dwu")
    return dG, dU, dwg, dwu, dwd


def _swiglu_bwd_input(tag, hin, g_norm, dG, dU, wg, wu, dh, dep):
    T = hin.shape[0]
    dn = _matmul([(dG, wg), (dU, wu)], M=T, N=D_MODEL, K=D_FF, tm=512, tn=D_MODEL // 2, tk=D_FF // 2, b_kind="t",
                 out_dtype=BF16, dep=dep, name=f"{tag}_dn")
    return _rms_bwd(hin, g_norm, dn, dh, name=f"{tag}_norm_bwd")


GROUP_FFN1 = ["w1_gate", "w1_up", "w1_down"]
GROUP_MID = ["w_in", "w_out", "w_xq", "w_xkv", "w_xo"]
GROUP_FFN2 = ["w2_gate", "w2_up", "w2_down"]


def _local_step(x, mem, tgt, W, comm):
    T = x.shape[0]
    Mm = mem.shape[0]
    mm = functools.partial(_matmul)

    W = {**W, **comm.gather_now("ffn1_up", ["w1_gate", "w1_up"])}
    tok = comm.gather_start("ffn1_down", ["w1_down"], after=W["w1_up"])
    tok = comm.gather_start("mid", GROUP_MID, after=tok)
    tok = comm.gather_start("ffn2", GROUP_FFN2, after=tok)
    n1 = _rms_fwd(x, W["g_ffn1"], dep=tok, name="f_norm1")
    G1, U1, A1 = _swiglu_up(n1, W["w1_gate"], W["w1_up"], name="f_ffn1_up")
    tok = comm.gather_landed("ffn1_down", after=A1)
    tok = comm.gather_landed("mid", after=tok)
    W = {**W, **comm.gather_ready("ffn1_down", after=tok)}
    h1 = mm([(A1, W["w1_down"])], M=T, N=D_MODEL, K=D_FF, tm=512, tn=D_MODEL // 2, tk=D_FF, scale=0.5, res=x,
            order="ji", name="f_ffn1_down")
    n2 = _rms_fwd(h1, W["g_mix"], name="f_norm2")
    W = {**W, **comm.gather_ready("mid", after=n2)}
    z = mm([(n2, W["w_in"])], M=T, N=IN_COLS, K=D_MODEL, tm=512, tn=IN_COLS // 2, tk=D_MODEL, name="f_w_in")
    bst = jnp.transpose(W["b_s"])
    yn, probs = _mixer_fwd(z, W["g_v"], W["w_s"], bst, W["sinks"], W["g_a_out"], W["g_b_out"], name="f_mixer")
    tok = comm.gather_landed("ffn2", after=yn)
    h2, n3 = mm([(yn, W["w_out"])], M=T, N=D_MODEL, K=D_MODEL, tm=512, tn=D_MODEL, tk=D_MODEL, res=h1,
                norm_g=W["g_x"], dep=tok, name="f_w_out")
    memn = _rms_fwd(mem, W["g_mem"], name="f_norm_mem")
    q3 = mm([(n3, W["w_xq"])], M=T, N=D_MODEL, K=D_MODEL, tm=1024, tn=D_MODEL, tk=D_MODEL, out_dtype=BF16,
            name="f_w_xq")
    kvm = mm([(memn, W["w_xkv"])], M=Mm, N=2 * D_MODEL, K=D_MODEL, tm=Mm, tn=1024, tk=D_MODEL, b_kind="n",
             name="f_w_xkv")
    o3, xprobs = _xattn_fwd(q3, kvm, name="f_xattn")
    h3, n4 = mm([(o3, W["w_xo"])], M=T, N=D_MODEL, K=D_MODEL, tm=512, tn=D_MODEL, tk=D_MODEL, res=h2,
                norm_g=W["g_ffn2"], name="f_w_xo")
    W = {**W, **comm.gather_ready("ffn2", after=n4)}
    G2, U2, A2 = _swiglu_up(n4, W["w2_gate"], W["w2_up"], name="f_ffn2_up")
    h4 = mm([(A2, W["w2_down"])], M=T, N=D_MODEL, K=D_FF, tm=512, tn=D_MODEL // 2, tk=D_FF // 2, scale=0.5, res=h3,
            name="f_ffn2_down")

    grads = {}
    dh4, dh4b, grads["g_final"], loss = _loss_head(h4, W["g_final"], tgt, name="loss_head")
    dG2, dU2, dwg, dwu, dwd = _swiglu_bwd_weights("b_ffn2", n4, G2, U2, A2, W["w2_down"], dh4b)
    tok = comm.reduce_pair_start("ffn2", {"w2_gate": dwg, "w2_up": dwu, "w2_down": dwd})
    dh3, dh3b, grads["g_ffn2"] = _swiglu_bwd_input("b_ffn2", h3, W["g_ffn2"], dG2, dU2, W["w2_gate"], W["w2_up"],
                                                   dh4, tok)
    tok = comm.reduce_pair_done("ffn2", after=dh3b)

    mid = {}
    do3 = mm([(dh3b, W["w_xo"])], M=T, N=D_MODEL, K=D_MODEL, tm=512, tn=D_MODEL, tk=D_MODEL, b_kind="t",
             out_dtype=BF16, dep=tok, name="b_do3")
    mid["w_xo"] = mm([(o3, dh3b)], M=D_MODEL, N=D_MODEL, K=T, tm=1024, tn=D_MODEL, tk=1024, a_t=True,
                       out_dtype=BF16, name="b_dw_xo")
    dq3, dkvm = _xattn_bwd(q3, kvm, xprobs, do3, name="b_xattn")
    mid["w_xq"] = mm([(n3, dq3)], M=D_MODEL, N=D_MODEL, K=T, tm=1024, tn=D_MODEL, tk=1024, a_t=True,
                       out_dtype=BF16, name="b_dw_xq")
    dn3 = mm([(dq3, W["w_xq"])], M=T, N=D_MODEL, K=D_MODEL, tm=512, tn=D_MODEL, tk=D_MODEL, b_kind="t",
             out_dtype=BF16, name="b_dn3")
    dh2, dh2b, grads["g_x"] = _rms_bwd(h2, W["g_x"], dn3, dh3, name="b_norm3")
    dkvmb = dkvm.astype(BF16)
    mid["w_xkv"] = mm([(memn, dkvmb)], M=D_MODEL, N=2 * D_MODEL, K=Mm, tm=D_MODEL, tn=1024, tk=Mm, a_t=True,
                        out_kind="s", out_dtype=BF16, name="b_dw_xkv")
    dmemn = mm([(dkvmb, W["w_xkv"])], M=Mm, N=D_MODEL, K=2 * D_MODEL, tm=Mm, tn=D_MODEL, tk=1024, b_kind="t",
               name="b_dmemn")
    _, _, grads["g_mem"] = _rms_bwd(mem, W["g_mem"], dmemn, None, name="b_norm_mem")
    comm.reduce_finish("ffn2", after=dh2b)

    dyn = mm([(dh2b, W["w_out"])], M=T, N=D_MODEL, K=D_MODEL, tm=1024, tn=D_MODEL, tk=D_MODEL, b_kind="t",
             out_dtype=BF16, name="b_dyn")
    mid["w_out"] = mm([(yn, dh2b)], M=D_MODEL, N=D_MODEL, K=T, tm=1024, tn=D_MODEL, tk=1024, a_t=True,
                        out_dtype=BF16, name="b_dw_out")
    dz, grads["g_v"], grads["w_s"], dbst, grads["sinks"], grads["g_a_out"], grads["g_b_out"] = _mixer_bwd(
        z, dyn, probs, W["g_v"], W["w_s"], bst, W["g_a_out"], W["g_b_out"], name="b_mixer")
    grads["b_s"] = jnp.transpose(dbst)
    mid["w_in"] = mm([(n2, dz)], M=D_MODEL, N=IN_COLS, K=T, tm=1024, tn=IN_COLS, tk=1024, a_t=True,
                     out_dtype=BF16, name="b_dw_in")
    tok = comm.reduce_pair_start("mid", mid)
    dn2 = mm([(dz, W["w_in"])], M=T, N=D_MODEL, K=IN_COLS, tm=512, tn=D_MODEL, tk=IN_COLS, b_kind="t",
             out_dtype=BF16, dep=tok, name="b_dn2")
    tok = comm.reduce_pair_done("mid", after=dn2)
    dh1, dh1b, grads["g_mix"] = _rms_bwd(h1, W["g_mix"], dn2, dh2, dep=tok, name="b_norm2")

    dG1, dU1, dwg, dwu, dwd = _swiglu_bwd_weights("b_ffn1", n1, G1, U1, A1, W["w1_down"], dh1b)
    comm.reduce_finish("mid", after=dwu)
    tok = comm.reduce_start("ffn1", {"w1_gate": dwg, "w1_up": dwu, "w1_down": dwd})
    dx, _, grads["g_ffn1"] = _swiglu_bwd_input("b_ffn1", x, W["g_ffn1"], dG1, dU1, W["w1_gate"], W["w1_up"], dh1, tok)
    comm.reduce_finish("ffn1", after=dx)
    return loss, dx, grads


BIG = ["w1_gate", "w1_up", "w1_down", "w_in", "w_out", "w_xq", "w_xkv", "w_xo", "w2_gate", "w2_up", "w2_down"]
SMALL = ["g_ffn1", "g_mix", "g_v", "w_s", "b_s", "sinks", "g_a_out", "g_b_out", "g_x", "g_mem", "g_ffn2", "g_final"]
ALL_W = ["g_ffn1", "w1_gate", "w1_up", "w1_down", "g_mix", "w_in", "g_v", "w_s", "b_s", "sinks", "g_a_out",
         "g_b_out", "w_out", "g_x", "g_mem", "w_xq", "w_xkv", "w_xo", "g_ffn2", "w2_gate", "w2_up", "w2_down",
         "g_final"]
ANY = pl.BlockSpec(memory_space=pl.ANY)


def _place():
    x, y, c = lax.axis_index("x"), lax.axis_index("y"), lax.axis_index("c")
    chips = [(1 - x, y), (x, 1 - y), (1 - x, 1 - y)]
    return x, y, c, chips


COL_SHARDED = ("w1_gate", "w1_up", "w2_gate", "w2_up", "w_xkv")


def _gathered_shape(shape, by_cols):
    rows, cols = shape
    return (rows, N_CHIPS * cols) if by_cols else (N_CHIPS, rows, cols)


def _owner_rows(ref, shape, by_cols, slot, r0, rows):
    cols = shape[1]
    if by_cols:
        return ref.at[pl.ds(r0, rows), pl.ds(pl.multiple_of(slot * cols, LANES), cols)]
    return ref.at[slot, pl.ds(r0, rows), :]


def _allgather_weights(shards, by_cols, *, name):
    n = len(shards)

    def body(*refs):
        ins, outs = refs[:n], refs[n:2 * n]
        send, recv, loc = refs[2 * n:]
        x, y, c, chips = _place()
        me = 2 * x + y
        sib = (x, y, 1 - c)

        def half(w, slot, hc):
            h = shards[w].shape[0] // 2
            return _owner_rows(outs[w], shards[w].shape, by_cols[w], slot, hc * h, h)

        def copy(w, k, slot, hc, to, src=None):
            return pltpu.make_async_remote_copy(
                src_ref=half(w, slot, hc) if src is None else src, dst_ref=half(w, slot, hc),
                send_sem=send.at[6 * w + k], recv_sem=recv.at[6 * w + k], device_id=to, device_id_type=MESH)

        own = [pltpu.make_async_remote_copy(
            src_ref=ins[w], dst_ref=_owner_rows(outs[w], shards[w].shape, by_cols[w], me, 0, shards[w].shape[0]),
            send_sem=loc.at[w], recv_sem=loc.at[n + w], device_id=sib, device_id_type=MESH) for w in range(n)]
        for cp in own:
            cp.start()
        first = []
        for w in range(n):
            h = shards[w].shape[0] // 2
            for j, (tx, ty) in enumerate(chips):
                first.append(copy(w, j, me, c, (tx, ty, c), src=ins[w].at[pl.ds(c * h, h), :]))
                first[-1].start()
        passed = []
        for w in range(n):
            for j, (tx, ty) in enumerate(chips):
                slot = 2 * tx + ty
                copy(w, j, slot, c, (tx, ty, c)).wait_recv()
                passed.append(copy(w, 3 + j, slot, c, sib))
                passed[-1].start()
        for w in range(n):
            for j, (tx, ty) in enumerate(chips):
                copy(w, 3 + j, 2 * tx + ty, 1 - c, sib).wait_recv()
        for cp in first + passed:
            cp.wait_send()
        for cp in own:
            cp.wait()

    return pl.pallas_call(
        body, name=name, in_specs=[ANY] * n, out_specs=[ANY] * n,
        out_shape=[jax.ShapeDtypeStruct(_gathered_shape(s.shape, bc), s.dtype) for s, bc in zip(shards, by_cols)],
        scratch_shapes=[pltpu.SemaphoreType.DMA((6 * n,)), pltpu.SemaphoreType.DMA((6 * n,)),
                        pltpu.SemaphoreType.DMA((2 * n,))],
    )(*shards)


def _pair_exchange(grads, *, name):
    n = len(grads)

    def body(*refs):
        ins, outs = refs[:n], refs[n:2 * n]
        send, recv = refs[2 * n:]
        x, y, c, _ = _place()
        cps = []
        for w in range(n):
            h = grads[w].shape[1] // 2
            cps.append(pltpu.make_async_remote_copy(
                src_ref=ins[w].at[:, pl.ds((1 - c) * h, h), :], dst_ref=outs[w],
                send_sem=send.at[w], recv_sem=recv.at[w], device_id=(x, y, 1 - c), device_id_type=MESH))
            cps[-1].start()
        for cp in cps:
            cp.wait()

    return pl.pallas_call(
        body, name=name, in_specs=[ANY] * n, out_specs=[ANY] * n,
        out_shape=[jax.ShapeDtypeStruct((N_CHIPS, g.shape[1] // 2, g.shape[2]), g.dtype) for g in grads],
        scratch_shapes=[pltpu.SemaphoreType.DMA((n,)), pltpu.SemaphoreType.DMA((n,))],
    )(*grads)


def _pair_sum(g, got, *, name):
    S, R, C = g.shape
    h = R // 2
    tr = _row_block(h, 3 * C * 2, 16)
    nr = h // tr

    def body(a_ref, b_ref, o_ref):
        o_ref[...] = (a_ref[...].astype(F32) + b_ref[...].astype(F32)).astype(BF16)

    return pl.pallas_call(
        body, name=name, grid=(S, nr),
        in_specs=[pl.BlockSpec((None, tr, C), lambda s, r: (s, lax.axis_index("c") * nr + r, 0)),
                  pl.BlockSpec((None, tr, C), lambda s, r: (s, r, 0))],
        out_specs=pl.BlockSpec((None, tr, C), lambda s, r: (s, r, 0)),
        out_shape=jax.ShapeDtypeStruct((S, h, C), BF16),
        compiler_params=_params(("parallel", "parallel")),
    )(g, got)


def _chip_sum(part, got, *, name):
    S, h, C = part.shape
    tr = _row_block(h, 4 * C * 2 + C * 4, 16)
    nr = h // tr

    def body(own_ref, g0_ref, g1_ref, g2_ref, o_ref):
        acc = own_ref[...].astype(F32) + g0_ref[...].astype(F32)
        o_ref[...] = (acc + g1_ref[...].astype(F32)) + g2_ref[...].astype(F32)

    def piece(j):
        return pl.BlockSpec((None, tr, C), lambda r: (j, r, 0))

    return pl.pallas_call(
        body, name=name, grid=(nr,),
        in_specs=[pl.BlockSpec((None, tr, C), lambda r: (2 * lax.axis_index("x") + lax.axis_index("y"), r, 0)),
                  piece(0), piece(1), piece(2)],
        out_specs=pl.BlockSpec((tr, C), lambda r: (lax.axis_index("c") * nr + r, 0)),
        out_shape=jax.ShapeDtypeStruct((2 * h, C), F32),
        compiler_params=_params(("parallel",)),
    )(part, got, got, got)


def _pair_gather(totals, *, name):
    n = len(totals)

    def body(*refs):
        ins, outs = refs[:n], refs[n:2 * n]
        send, recv = refs[2 * n:]
        x, y, c, _ = _place()
        cps = []
        for w in range(n):
            h = totals[w].shape[0] // 2
            cps.append(pltpu.make_async_remote_copy(
                src_ref=ins[w].at[pl.ds(c * h, h), :], dst_ref=outs[w].at[pl.ds(c * h, h), :],
                send_sem=send.at[w], recv_sem=recv.at[w], device_id=(x, y, 1 - c), device_id_type=MESH))
            cps[-1].start()
        for w in range(n):
            h = totals[w].shape[0] // 2
            theirs = outs[w].at[pl.ds((1 - c) * h, h), :]
            pltpu.make_async_remote_copy(
                src_ref=theirs, dst_ref=theirs, send_sem=send.at[w], recv_sem=recv.at[w],
                device_id=(x, y, 1 - c), device_id_type=MESH).wait_recv()
        for cp in cps:
            cp.wait_send()

    return pl.pallas_call(
        body, name=name, in_specs=[ANY] * n, out_specs=[ANY] * n,
        out_shape=[jax.ShapeDtypeStruct(t.shape, t.dtype) for t in totals],
        input_output_aliases={w: w for w in range(n)},
        scratch_shapes=[pltpu.SemaphoreType.DMA((n,)), pltpu.SemaphoreType.DMA((n,))],
    )(*totals)


def _allreduce_small(v, *, name):
    R, C = v.shape
    ND = 8

    def body(v_ref, o_ref, all_ref, send, recv, loc):
        x, y, c, chips = _place()
        me, sib = (x, y, c), (x, y, 1 - c)

        def rows(px, py, pc):
            return all_ref.at[pl.ds((4 * px + 2 * py + pc) * R, R), :]

        def copy(k, block, to, src=None):
            return pltpu.make_async_remote_copy(
                src_ref=rows(*block) if src is None else src, dst_ref=rows(*block),
                send_sem=send.at[k], recv_sem=recv.at[k], device_id=to, device_id_type=MESH)

        mine = pltpu.make_async_copy(v_ref, rows(*me), loc)
        mine.start()
        first = [copy(0, me, sib, src=v_ref)]
        first += [copy(1 + j, me, (*chip, c), src=v_ref) for j, chip in enumerate(chips)]
        for cp in first:
            cp.start()
        passed = [copy(4 + j, (*chip, c), sib) for j, chip in enumerate(chips)]
        for j, chip in enumerate(chips):
            copy(1 + j, (*chip, c), me).wait_recv()
            passed[j].start()
        copy(0, sib, me).wait_recv()
        for j, chip in enumerate(chips):
            copy(4 + j, (*chip, 1 - c), me).wait_recv()
        for cp in first + passed:
            cp.wait_send()
        mine.wait()
        acc = all_ref[0:R, :]
        for d in range(1, ND):
            acc = acc + all_ref[d * R:(d + 1) * R, :]
        o_ref[...] = acc

    vm = pl.BlockSpec(memory_space=pltpu.VMEM)
    return pl.pallas_call(
        body, name=name, in_specs=[vm], out_specs=[vm, vm],
        out_shape=[jax.ShapeDtypeStruct((R, C), F32), jax.ShapeDtypeStruct((ND * R, C), F32)],
        scratch_shapes=[pltpu.SemaphoreType.DMA((7,)), pltpu.SemaphoreType.DMA((7,)), pltpu.SemaphoreType.DMA],
        compiler_params=pltpu.CompilerParams(vmem_limit_bytes=VMEM_LIMIT),
    )(v)[0]


HBM = pl.BlockSpec(memory_space=pltpu.HBM)
SEM = pl.BlockSpec(memory_space=pltpu.SEMAPHORE)
EFFECT = pltpu.SideEffectType.DATAFLOW_SIDE_EFFECTING


def _remote(src, dst, send, recv, k, to):
    return pltpu.make_async_remote_copy(src_ref=src, dst_ref=dst, send_sem=send.at[k], recv_sem=recv.at[k],
                                        device_id=to, device_id_type=MESH)


def _split_start(bufs, plan, ncopies, *, name, after=None):
    nb = len(bufs)
    extra = [] if after is None else [after]

    def body(*refs):
        pos = nb + len(extra)
        send, recv, token = refs[pos], refs[pos + 1], refs[-1]
        for k, (src, dst, to) in enumerate(plan(refs[:nb])):
            _remote(src, dst, send, recv, k, to).start()
        token[...] = jnp.zeros_like(token)

    outs = pl.pallas_call(
        body, name=name,
        out_shape=(pltpu.SemaphoreType.DMA((ncopies,)), pltpu.SemaphoreType.DMA((ncopies,)),
                   *[pltpu.HBM(b.shape, b.dtype) for b in bufs], jax.ShapeDtypeStruct((SUBLANES, LANES), F32)),
        in_specs=[HBM] * nb + [ANY] * len(extra),
        out_specs=(SEM, SEM, *[HBM] * nb, pl.BlockSpec(memory_space=pltpu.VMEM)),
        input_output_aliases={i: 2 + i for i in range(nb)},
        compiler_params=pltpu.CompilerParams(has_side_effects=EFFECT),
    )(*[pltpu.with_memory_space_constraint(b, pltpu.HBM) for b in bufs], *extra)
    return outs[0], outs[1], list(outs[2:2 + nb]), outs[-1]


def _split_wait(started, plan, after, *, name):
    send, recv, bufs, _ = started
    nb = len(bufs)

    def body(*refs):
        send_sem, recv_sem = refs[nb], refs[nb + 1]
        for k, (src, dst, to) in enumerate(plan(refs[:nb])):
            cp = _remote(src, dst, send_sem, recv_sem, k, to)
            cp.wait_send()
            cp.wait_recv()

    outs = pl.pallas_call(
        body, name=name,
        out_shape=tuple(pltpu.HBM(b.shape, b.dtype) for b in bufs),
        in_specs=[HBM] * nb + [SEM, SEM, ANY], out_specs=tuple([HBM] * nb),
        input_output_aliases={i: i for i in range(nb)},
        compiler_params=pltpu.CompilerParams(has_side_effects=EFFECT),
    )(*bufs, send, recv, after)
    return list(outs)


def _gather_chip_plan(shapes, by_cols):
    n = len(shapes)

    def plan(refs):
        srcs, lands = refs[:n], refs[n:]
        x, y, c, chips = _place()
        out = []
        for w in range(n):
            h = shapes[w][0] // 2
            mine = _owner_rows(lands[w], shapes[w], by_cols[w], 2 * x + y, c * h, h)
            for tx, ty in chips:
                out.append((srcs[w].at[pl.ds(c * h, h), :], mine, (tx, ty, c)))
        return out

    return plan


def _gather_pair_plan(shapes, by_cols):
    n = len(shapes)

    def plan(refs):
        srcs, lands = refs[:n], refs[n:]
        x, y, c, chips = _place()
        out = []
        for w in range(n):
            h = shapes[w][0] // 2
            for tx, ty in chips:
                half = _owner_rows(lands[w], shapes[w], by_cols[w], 2 * tx + ty, c * h, h)
                out.append((half, half, (x, y, 1 - c)))
            own = _owner_rows(lands[w], shapes[w], by_cols[w], 2 * x + y, 0, shapes[w][0])
            out.append((srcs[w], own, (x, y, 1 - c)))
        return out

    return plan


def _reduce_pair_plan(shapes):
    n = len(shapes)

    def plan(refs):
        local, lands = refs[:n], refs[n:]
        x, y, c, _ = _place()
        out = []
        for w in range(n):
            h = shapes[w][1] // 2
            out.append((local[w].at[:, pl.ds((1 - c) * h, h), :], lands[w], (x, y, 1 - c)))
        return out

    return plan


def _reduce_chip_plan(n):
    def plan(refs):
        parts, lands = refs[:n], refs[n:]
        x, y, c, chips = _place()
        return [(parts[w].at[2 * tx + ty], lands[w].at[j], (tx, ty, c))
                for w in range(n) for j, (tx, ty) in enumerate(chips)]

    return plan


def _as_operands(gathered):
    out = {}
    for n, g in gathered.items():
        if n in COL_SHARDED:
            out[n] = g
        elif n == "w_in":
            out[n] = jnp.transpose(g, (1, 0, 2)).reshape(D_MODEL, IN_COLS)
        else:
            out[n] = g.reshape(g.shape[0] * g.shape[1], g.shape[2])
    return out


def _by_owner(n, g):
    if n == "w_in":
        return jnp.transpose(g.reshape(D_MODEL, N_CHIPS, IN_COLS // N_CHIPS), (1, 0, 2))
    if g.ndim == 2:
        return g.reshape(N_CHIPS, g.shape[0] // N_CHIPS, g.shape[1])
    return g


class _Comm:
    def __init__(self, shards):
        self.shards = shards
        self.total = {}
        self._flight = {}

    def _layout(self, names):
        return [self.shards[n].shape for n in names], [n in COL_SHARDED for n in names]

    def gather_now(self, tag, names):
        _, by_cols = self._layout(names)
        got = _allgather_weights([self.shards[n] for n in names], by_cols, name=f"gather_{tag}")
        return _as_operands(dict(zip(names, got)))

    def gather_start(self, tag, names, after):
        shapes, by_cols = self._layout(names)
        srcs = [self.shards[n] for n in names]
        lands = [lax.empty(_gathered_shape(s.shape, bc), s.dtype) for s, bc in zip(srcs, by_cols)]
        started = _split_start(srcs + lands, _gather_chip_plan(shapes, by_cols), 3 * len(srcs),
                               after=after, name=f"gather_{tag}_chips_start")
        self._flight[tag] = (names, started)
        return started[3]

    def gather_landed(self, tag, after):
        names, started = self._flight[tag]
        shapes, by_cols = self._layout(names)
        bufs = _split_wait(started, _gather_chip_plan(shapes, by_cols), after, name=f"gather_{tag}_chips_wait")
        started = _split_start(bufs, _gather_pair_plan(shapes, by_cols), 4 * len(names),
                               name=f"gather_{tag}_pair_start")
        self._flight[tag] = (names, started)
        return started[3]

    def gather_ready(self, tag, after):
        names, started = self._flight.pop(tag)
        shapes, by_cols = self._layout(names)
        bufs = _split_wait(started, _gather_pair_plan(shapes, by_cols), after, name=f"gather_{tag}_pair_wait")
        return _as_operands(dict(zip(names, bufs[len(names):])))

    def reduce_start(self, tag, grads):
        names = list(grads)
        local = [_by_owner(n, grads[n]) for n in names]
        return self._chip_start(tag, names, local, _pair_exchange(local, name=f"pair_exchange_{tag}"))

    def reduce_pair_start(self, tag, grads):
        names = list(grads)
        local = [_by_owner(n, grads[n]) for n in names]
        lands = [lax.empty((N_CHIPS, g.shape[1] // 2, g.shape[2]), g.dtype) for g in local]
        started = _split_start(local + lands, _reduce_pair_plan([g.shape for g in local]), len(names),
                               name=f"pair_exchange_{tag}_start")
        self._flight[tag] = (names, started)
        return started[3]

    def reduce_pair_done(self, tag, after):
        names, started = self._flight.pop(tag)
        n = len(names)
        bufs = _split_wait(started, _reduce_pair_plan([b.shape for b in started[2][:n]]), after,
                           name=f"pair_exchange_{tag}_wait")
        return self._chip_start(tag, names, bufs[:n], bufs[n:])

    def _chip_start(self, tag, names, local, from_sib):
        parts = [_pair_sum(g, s, name=f"pair_sum_{n}") for n, g, s in zip(names, local, from_sib)]
        lands = [lax.empty((N_CHIPS - 1,) + p.shape[1:], p.dtype) for p in parts]
        self._flight[tag] = (names, _split_start(parts + lands, _reduce_chip_plan(len(names)), 3 * len(names),
                                                 name=f"chip_exchange_{tag}_start"))
        return self._flight[tag][1][3]

    def reduce_finish(self, tag, after):
        names, started = self._flight.pop(tag)
        n = len(names)
        bufs = _split_wait(started, _reduce_chip_plan(n), after, name=f"chip_exchange_{tag}_wait")
        totals = [_chip_sum(p, s, name=f"chip_sum_{nm}") for nm, p, s in zip(names, bufs[:n], bufs[n:])]
        self.total.update(zip(names, _pair_gather(totals, name=f"pair_gather_{tag}")))


def _adamw(w, g, m, v, *, name):
    R, C = w.shape
    tr = _row_block(R, 8 * C * 4, SUBLANES)

    def body(w_ref, g_ref, m_ref, v_ref, go_ref, d_ref, nm_ref, nv_ref):
        gg = g_ref[...]
        go_ref[...] = gg
        m_new = ADAM_B1 * m_ref[...] + (1.0 - ADAM_B1) * gg
        v_new = ADAM_B2 * v_ref[...] + (1.0 - ADAM_B2) * (gg * gg)
        m_hat = m_new / (1.0 - ADAM_B1 ** ADAM_STEP)
        v_hat = v_new / (1.0 - ADAM_B2 ** ADAM_STEP)
        d_ref[...] = -ADAM_LR * (m_hat / (jnp.sqrt(v_hat) + ADAM_EPS) + ADAM_WD * w_ref[...])
        nm_ref[...] = m_new
        nv_ref[...] = v_new

    blk = pl.BlockSpec((tr, C), lambda i: (i, 0))
    shp = jax.ShapeDtypeStruct((R, C), F32)
    return pl.pallas_call(
        body, name=name, grid=(R // tr,), in_specs=[blk] * 4, out_specs=[blk] * 4, out_shape=[shp] * 4,
        compiler_params=_params(("parallel",)),
    )(w, g, m, v)


def _to2d(a):
    flat = a.reshape(-1)
    pad = (-flat.shape[0]) % (SUBLANES * LANES)
    if pad:
        flat = jnp.pad(flat, (0, pad))
    return flat.reshape(-1, LANES)


def _small_rows(shape):
    return -(-math.prod(shape) // (SUBLANES * LANES)) * SUBLANES


def _pack_small(parts):
    rows = jnp.concatenate([_to2d(p) for p in parts], axis=0)
    pad = (-rows.shape[0]) % 256
    if pad:
        rows = jnp.concatenate([rows, jnp.zeros((pad, LANES), rows.dtype)], axis=0)
    return rows


def _unpack_small(rows, shapes):
    out, r = [], 0
    for shp in shapes:
        size = math.prod(shp)
        nrow = _small_rows(shp)
        out.append(rows[r:r + nrow].reshape(-1)[:size].reshape(shp))
        r += nrow
    return out


def kernel(x, mem, g_ffn1, w1_gate, w1_up, w1_down, g_mix, w_in, g_v, w_s, b_s, sinks, g_a_out, g_b_out, w_out, g_x, g_mem, w_xq, w_xkv, w_xo, g_ffn2, w2_gate, w2_up, w2_down, g_final, loss_target, m_g_ffn1, m_w1_gate, m_w1_up, m_w1_down, m_g_mix, m_w_in, m_g_v, m_w_s, m_b_s, m_sinks, m_g_a_out, m_g_b_out, m_w_out, m_g_x, m_g_mem, m_w_xq, m_w_xkv, m_w_xo, m_g_ffn2, m_w2_gate, m_w2_up, m_w2_down, m_g_final, v_g_ffn1, v_w1_gate, v_w1_up, v_w1_down, v_g_mix, v_w_in, v_g_v, v_w_s, v_b_s, v_sinks, v_g_a_out, v_g_b_out, v_w_out, v_g_x, v_g_mem, v_w_xq, v_w_xkv, v_w_xo, v_g_ffn2, v_w2_gate, v_w2_up, v_w2_down, v_g_final):
    args = dict(locals())
    Wp = {n: args[n] for n in ALL_W}
    Mp = {n: args["m_" + n] for n in ALL_W}
    Vp = {n: args["v_" + n] for n in ALL_W}

    comm = _Comm({n: Wp[n][0].astype(BF16) for n in BIG})
    W = {n: Wp[n] for n in SMALL}
    W["g_final"] = Wp["g_final"].reshape(1, D_MODEL)
    for n in ("w_s", "b_s"):
        W[n] = Wp[n][0]
    loss, dx, grads = _local_step(x[0], mem[0], loss_target[0], W, comm)
    big_grad = comm.total

    small_shapes = [Wp[n].shape for n in SMALL]
    packed = _pack_small([grads[n].reshape(Wp[n].shape) for n in SMALL] + [loss])
    summed = _allreduce_small(packed, name="allreduce_small")
    small_grad = dict(zip(SMALL, _unpack_small(summed, small_shapes)))
    nrows = sum(_small_rows(s) for s in small_shapes)
    loss_total = summed[nrows, 0]

    grad_out, delta, new_m, new_v = {}, {}, {}, {}
    for n in BIG:
        shp = Wp[n].shape
        g, d, nm, nv = _adamw(Wp[n][0], big_grad[n], Mp[n][0], Vp[n][0], name=f"adamw_{n}")
        grad_out[n], delta[n], new_m[n], new_v[n] = g.reshape(shp), d.reshape(shp), nm.reshape(shp), nv.reshape(shp)
    sw = _pack_small([Wp[n] for n in SMALL])
    sg = _pack_small([small_grad[n] for n in SMALL])
    sm = _pack_small([Mp[n] for n in SMALL])
    sv = _pack_small([Vp[n] for n in SMALL])
    _, d, nm, nv = _adamw(sw, sg, sm, sv, name="adamw_small")
    for n, dd, mm_, vv_ in zip(SMALL, _unpack_small(d, small_shapes), _unpack_small(nm, small_shapes),
                               _unpack_small(nv, small_shapes)):
        grad_out[n], delta[n], new_m[n], new_v[n] = small_grad[n], dd, mm_, vv_

    return (loss_total, dx[None], *[grad_out[n] for n in ALL_W], *[delta[n] for n in ALL_W],
            *[new_m[n] for n in ALL_W], *[new_v[n] for n in ALL_W])
```

```python
import functools
import math

import jax
import jax.numpy as jnp
from jax import lax
from jax.experimental import pallas as pl
from jax.experimental.pallas import tpu as pltpu

F32 = jnp.float32
BF16 = jnp.bfloat16
MESH = pl.DeviceIdType.MESH

D_MODEL = 2048
D_FF = 5632
D_A = 1024
D_B = 1024
CHUNK = 128
A_GROUPS = 8
HEAD_DIM = 64
B_Q_HEADS = 16
B_KV_HEADS = 2
X_HEADS = 4
X_HEAD_DIM = 512
IN_COLS = 3328
O_Q = 2 * D_A
O_K = O_Q + D_B
O_V = O_K + B_KV_HEADS * HEAD_DIM
N_CHIPS = 4
EPS = 1e-5
NEG = -1e30
ADAM_LR = 0.001
ADAM_B1 = 0.9
ADAM_B2 = 0.999
ADAM_EPS = 1e-08
ADAM_WD = 0.01
ADAM_STEP = 10

V7X_VMEM_BYTES = 64 * 1024 * 1024
VMEM_LIMIT = 56 * 1024 * 1024
LANES = 128
SUBLANES = 8


ANY = pl.BlockSpec(memory_space=pl.ANY)


def _params(sem, vmem=VMEM_LIMIT):
    return pltpu.CompilerParams(dimension_semantics=sem, vmem_limit_bytes=vmem)


def _matmul(pairs, *, M, N, K, tm, tn, tk, a_t=False, b_kind="n", out_kind="n", out_dtype=F32,
            scale=1.0, res=None, norm_g=None, order="ij", dep=None, name):
    tm, tn, tk = min(tm, M), min(tn, N), min(tk, K)
    assert M % tm == 0 and N % tn == 0 and K % tk == 0, (name, M, N, K, tm, tn, tk)
    nk = K // tk
    npairs = len(pairs)
    b_t = b_kind == "t"
    ns = N // N_CHIPS

    def ij(g0, g1):
        return (g0, g1) if order == "ij" else (g1, g0)

    def a_map(g0, g1, k):
        i, _ = ij(g0, g1)
        return (k, i) if a_t else (i, k)

    a_spec = pl.BlockSpec((tk, tm) if a_t else (tm, tk), a_map)

    if b_kind == "n":
        b_spec = pl.BlockSpec((tk, tn), lambda g0, g1, k: (k, ij(g0, g1)[1]))
    else:
        b_spec = pl.BlockSpec((tn, tk), lambda g0, g1, k: (ij(g0, g1)[1], k))

    if out_kind == "n":
        o_spec = pl.BlockSpec((tm, tn), lambda g0, g1, k: ij(g0, g1))
        o_shape = jax.ShapeDtypeStruct((M, N), out_dtype)
    else:
        assert tn % ns == 0
        o_spec = pl.BlockSpec((tn // ns, tm, ns), lambda g0, g1, k: (ij(g0, g1)[1], ij(g0, g1)[0], 0))
        o_shape = jax.ShapeDtypeStruct((N_CHIPS, M, ns), out_dtype)

    in_specs, args = [], []
    for a, b in pairs:
        in_specs += [a_spec, b_spec]
        args += [a, b]
    if res is not None:
        in_specs.append(pl.BlockSpec((tm, tn), lambda g0, g1, k: ij(g0, g1)))
        args.append(res)
    if norm_g is not None:
        assert tn == N and out_kind == "n"
        in_specs.append(pl.BlockSpec((1, N), lambda g0, g1, k: (0, 0)))
        args.append(norm_g)
    if dep is not None:
        in_specs.append(ANY)
        args.append(dep)

    dn = (((0,) if a_t else (1,), (1,) if b_t else (0,)), ((), ()))

    def body(*refs):
        pos = 2 * npairs
        res_ref = refs[pos] if res is not None else None
        pos += res is not None
        g_ref = refs[pos] if norm_g is not None else None
        pos += (norm_g is not None) + (dep is not None)
        o_ref = refs[pos]
        n_ref = refs[pos + 1] if norm_g is not None else None
        acc_ref = refs[-1] if nk > 1 else None
        part = None
        for p in range(npairs):
            d = lax.dot_general(refs[2 * p][...], refs[2 * p + 1][...], dn, preferred_element_type=F32)
            part = d if part is None else part + d

        def finish(acc):
            r = acc * scale if scale != 1.0 else acc
            if res_ref is not None:
                r = res_ref[...] + r
            if out_kind == "n":
                o_ref[...] = r.astype(out_dtype)
            else:
                for s in range(tn // ns):
                    o_ref[s] = r[:, s * ns:(s + 1) * ns].astype(out_dtype)
            if n_ref is not None:
                n_ref[...] = (r * _rstd(r) * g_ref[...]).astype(BF16)

        if nk == 1:
            finish(part)
        else:
            k = pl.program_id(2)

            @pl.when(k == 0)
            def _():
                acc_ref[...] = part

            @pl.when((k > 0) & (k < nk - 1))
            def _():
                acc_ref[...] += part

            @pl.when(k == nk - 1)
            def _():
                finish(acc_ref[...] + part)

    grid = (M // tm, N // tn, nk) if order == "ij" else (N // tn, M // tm, nk)
    out_specs, out_shape = o_spec, o_shape
    if norm_g is not None:
        out_specs = [o_spec, pl.BlockSpec((tm, tn), lambda g0, g1, k: ij(g0, g1))]
        out_shape = [o_shape, jax.ShapeDtypeStruct((M, N), BF16)]
    return pl.pallas_call(
        body, name=name, grid=grid, in_specs=in_specs, out_specs=out_specs, out_shape=out_shape,
        scratch_shapes=[pltpu.VMEM((tm, tn), F32)] if nk > 1 else [],
        compiler_params=_params(("parallel", "parallel", "arbitrary")),
    )(*args)


def _rstd(x):
    return lax.rsqrt(jnp.mean(x * x, axis=-1, keepdims=True) + EPS)


def _rms_bwd_math(x, g, dy):
    r = _rstd(x)
    gy = dy * g
    xr = x * r
    dx = r * (gy - xr * jnp.mean(gy * xr, axis=-1, keepdims=True))
    return dx, dy * xr


def _rms_fwd(h, g, *, name, tm=512, dep=None):
    T, Dm = h.shape
    tm = min(tm, T)

    def body(h_ref, g_ref, *rest):
        x = h_ref[...]
        rest[-1][...] = (x * _rstd(x) * g_ref[...]).astype(BF16)

    return pl.pallas_call(
        body, name=name, grid=(T // tm,),
        in_specs=[pl.BlockSpec((tm, Dm), lambda i: (i, 0)), pl.BlockSpec((1, Dm), lambda i: (0, 0))]
        + ([ANY] if dep is not None else []),
        out_specs=pl.BlockSpec((tm, Dm), lambda i: (i, 0)),
        out_shape=jax.ShapeDtypeStruct((T, Dm), BF16),
        compiler_params=_params(("parallel",)),
    )(h, g, *([dep] if dep is not None else []))


def _rms_bwd(h, g, dn, dres, *, name, tm=256, dep=None):
    T, Dm = h.shape
    tm = min(tm, T)
    has_res = dres is not None

    def body(*refs):
        h_ref, g_ref, dn_ref = refs[:3]
        pos = 3
        dres_ref = refs[pos] if has_res else None
        pos += has_res + (dep is not None)
        dh_ref, dhb_ref, dg_ref = refs[pos:pos + 3]
        dx, dgr = _rms_bwd_math(h_ref[...], g_ref[...], dn_ref[...].astype(F32))
        if has_res:
            dx = dres_ref[...] + dx
        dh_ref[...] = dx
        dhb_ref[...] = dx.astype(BF16)
        part = jnp.sum(dgr, axis=0, keepdims=True)

        @pl.when(pl.program_id(0) == 0)
        def _():
            dg_ref[...] = part

        @pl.when(pl.program_id(0) > 0)
        def _():
            dg_ref[...] += part

    row = pl.BlockSpec((tm, Dm), lambda i: (i, 0))
    vec = pl.BlockSpec((1, Dm), lambda i: (0, 0))
    args = [h, g, dn] + ([dres] if has_res else []) + ([dep] if dep is not None else [])
    return pl.pallas_call(
        body, name=name, grid=(T // tm,),
        in_specs=[row, vec, row] + ([row] if has_res else []) + ([ANY] if dep is not None else []),
        out_specs=[row, row, vec],
        out_shape=[jax.ShapeDtypeStruct((T, Dm), F32), jax.ShapeDtypeStruct((T, Dm), BF16),
                   jax.ShapeDtypeStruct((1, Dm), F32)],
        compiler_params=_params(("arbitrary",)),
    )(*args)


def _loss_head(h, g, tgt, *, name, tm=256):
    T, Dm = h.shape
    tm = min(tm, T)

    def body(h_ref, g_ref, t_ref, dh_ref, dhb_ref, dg_ref, loss_ref):
        x = h_ref[...]
        gv = g_ref[...]
        r = _rstd(x)
        diff = x * r * gv - t_ref[...]
        lpart = 0.5 * jnp.sum(jnp.mean(diff * diff, axis=-1, keepdims=True), axis=0, keepdims=True)
        dx, dgr = _rms_bwd_math(x, gv, diff * (1.0 / Dm))
        dh_ref[...] = dx
        dhb_ref[...] = dx.astype(BF16)
        part = jnp.sum(dgr, axis=0, keepdims=True)
        lrow = jnp.broadcast_to(lpart, (1, LANES))

        @pl.when(pl.program_id(0) == 0)
        def _():
            dg_ref[...] = part
            loss_ref[...] = lrow

        @pl.when(pl.program_id(0) > 0)
        def _():
            dg_ref[...] += part
            loss_ref[...] += lrow

    row = pl.BlockSpec((tm, Dm), lambda i: (i, 0))
    vec = pl.BlockSpec((1, Dm), lambda i: (0, 0))
    return pl.pallas_call(
        body, name=name, grid=(T // tm,),
        in_specs=[row, vec, row],
        out_specs=[row, row, vec, pl.BlockSpec((1, LANES), lambda i: (0, 0))],
        out_shape=[jax.ShapeDtypeStruct((T, Dm), F32), jax.ShapeDtypeStruct((T, Dm), BF16),
                   jax.ShapeDtypeStruct((1, Dm), F32), jax.ShapeDtypeStruct((1, LANES), F32)],
        compiler_params=_params(("arbitrary",)),
    )(h, g, tgt)


MXU_COLS = 256
FF_TILE = 2 * MXU_COLS


def _row_block(rows, row_bytes, align, budget=24 * 1024 * 1024):
    fits = [d for d in range(align, rows + 1, align) if rows % d == 0 and 2 * d * row_bytes <= budget]
    assert fits, (rows, row_bytes)
    return fits[-1]


def _swiglu_up(n, wg, wu, *, name, tm=1024, tn=FF_TILE):
    T, Dm = n.shape
    Fd = wg.shape[1]
    tm = min(tm, T)

    def body(n_ref, wg_ref, wu_ref, g_ref, u_ref, a_ref):
        x = n_ref[...]
        g = jnp.dot(x, wg_ref[...], preferred_element_type=F32)
        u = jnp.dot(x, wu_ref[...], preferred_element_type=F32)
        g_ref[...] = g.astype(BF16)
        u_ref[...] = u.astype(BF16)
        a_ref[...] = (g * jax.nn.sigmoid(g) * u).astype(BF16)

    wspec = pl.BlockSpec((Dm, tn), lambda j, i: (0, j))
    ospec = pl.BlockSpec((tm, tn), lambda j, i: (i, j))
    oshape = jax.ShapeDtypeStruct((T, Fd), BF16)
    return pl.pallas_call(
        body, name=name, grid=(Fd // tn, T // tm),
        in_specs=[pl.BlockSpec((tm, Dm), lambda j, i: (i, 0)), wspec, wspec],
        out_specs=[ospec, ospec, ospec], out_shape=[oshape, oshape, oshape],
        compiler_params=_params(("parallel", "parallel")),
    )(n, wg, wu)


def _swiglu_bwd_act(dhb, wd, G, U, *, name, tm=1024, tn=D_FF // N_CHIPS):
    T, Dm = dhb.shape
    Fd = wd.shape[0]
    tm, tn = min(tm, T), min(tn, Fd)

    def body(dh_ref, wd_ref, g_ref, u_ref, dg_ref, du_ref):
        da = 0.5 * lax.dot_general(dh_ref[...], wd_ref[...], (((1,), (1,)), ((), ())), preferred_element_type=F32)
        g = g_ref[...].astype(F32)
        u = u_ref[...].astype(F32)
        sg = jax.nn.sigmoid(g)
        dg_ref[...] = (da * u * (sg * (1.0 + g * (1.0 - sg)))).astype(BF16)
        du_ref[...] = (da * (g * sg)).astype(BF16)

    blk = pl.BlockSpec((tm, tn), lambda j, i: (i, j))
    oshape = jax.ShapeDtypeStruct((T, Fd), BF16)
    return pl.pallas_call(
        body, name=name, grid=(Fd // tn, T // tm),
        in_specs=[pl.BlockSpec((tm, Dm), lambda j, i: (i, 0)), pl.BlockSpec((tn, Dm), lambda j, i: (j, 0)), blk, blk],
        out_specs=[blk, blk], out_shape=[oshape, oshape],
        compiler_params=_params(("parallel", "parallel")),
    )(dhb, wd, G, U)


_INV_SQRT2 = 0.7071067811865476
_INV_SQRT2PI = 0.3989422804014327


def _erf(x):
    ax = jnp.abs(x)
    t = 1.0 / (1.0 + 0.3275911 * ax)
    poly = t * (0.254829592 + t * (-0.284496736 + t * (1.421413741 + t * (-1.453152027 + t * 1.061405429))))
    y = 1.0 - poly * jnp.exp(-ax * ax)
    return jnp.where(x < 0, -y, y)


def _gelu_cdf(x):
    return 0.5 * (1.0 + _erf(x * _INV_SQRT2))


def _lane_lt64(shape):
    return lax.broadcasted_iota(jnp.int32, shape, len(shape) - 1) < HEAD_DIM


def _dup_half(x, kv):
    rolled = pltpu.roll(x, HEAD_DIM, 1)
    lo = _lane_lt64(x.shape)
    return jnp.where(lo, x, rolled) if kv == 0 else jnp.where(lo, rolled, x)


HEADS_PER_KV = B_Q_HEADS // B_KV_HEADS
PAIRS = HEADS_PER_KV // 2


def _attn_bias():
    shape = (HEADS_PER_KV * CHUNK, 2 * CHUNK)
    qpos = (lax.broadcasted_iota(jnp.int32, shape, 0) & (CHUNK - 1)) + CHUNK
    kpos = lax.broadcasted_iota(jnp.int32, shape, 1)
    diff = qpos - kpos
    band = (diff >= 0) & (diff < CHUNK)
    return jnp.stack([jnp.where(band & (kpos >= CHUNK), 0.0, NEG), jnp.where(band, 0.0, NEG)]).astype(F32)


def _stack_heads(tiles, lo):
    parts = []
    for t in tiles:
        parts += [jnp.where(lo, t, 0.0), jnp.where(lo, 0.0, t)]
    return jnp.concatenate(parts, axis=0)


def _unstack_heads(s, lo):
    return [jnp.where(lo, s[2 * p * CHUNK:(2 * p + 1) * CHUNK], s[(2 * p + 1) * CHUNK:(2 * p + 2) * CHUNK])
            for p in range(PAIRS)]


def _stack_sinks(sk_ref, kv):
    return jnp.concatenate([jnp.broadcast_to(sk_ref[:, h:h + 1], (CHUNK, 1))
                            for h in range(kv * HEADS_PER_KV, (kv + 1) * HEADS_PER_KV)], axis=0)


def _sgu_forward(z_ref, gv, wsm, bst):
    zu = z_ref[:, 0:D_A]
    zv = z_ref[:, D_A:2 * D_A]
    u = zu * _gelu_cdf(zu)
    v = zv * _gelu_cdf(zv)
    rv = _rstd(v)
    vn = (v * rv * gv).astype(BF16)
    svs = []
    for g in range(A_GROUPS):
        sl = slice(g * CHUNK, (g + 1) * CHUNK)
        svs.append(jnp.dot(wsm[g], vn[:, sl], preferred_element_type=F32) + bst[:, g:g + 1])
    sv = jnp.concatenate(svs, axis=1)
    return zu, zv, u, v, rv, vn, sv


def _masked_ws(ws_ref):
    tril = lax.broadcasted_iota(jnp.int32, (CHUNK, CHUNK), 0) >= lax.broadcasted_iota(jnp.int32, (CHUNK, CHUNK), 1)
    return [jnp.where(tril, ws_ref[g], 0.0).astype(BF16) for g in range(A_GROUPS)], tril


def _attn_probs(qm, kkd, sink, bias):
    s = lax.dot_general(qm, kkd, (((1,), (1,)), ((), ())), preferred_element_type=F32) * (HEAD_DIM ** -0.5) + bias
    m = jnp.maximum(jnp.max(s, axis=-1, keepdims=True), sink)
    e = jnp.exp(s - m)
    es = jnp.exp(sink - m)
    inv = 1.0 / (jnp.sum(e, axis=-1, keepdims=True) + es)
    return e * inv, es * inv


def _mixer_fwd(z, gv, ws, bst, sinks, ga, gb, *, name):
    T = z.shape[0]
    nb = T // CHUNK
    kvb = O_K // (2 * CHUNK)

    def body(z_ref, zp_ref, bias_ref, gv_ref, ws_ref, bst_ref, sk_ref, ga_ref, gb_ref, o_ref, p_ref):
        wsm, _ = _masked_ws(ws_ref)
        _, _, u, _, _, _, sv = _sgu_forward(z_ref, gv_ref[...], wsm, bst_ref[...])
        ya = u * sv
        o_ref[:, 0:D_A] = (ya * _rstd(ya) * ga_ref[...]).astype(BF16)

        mask = bias_ref[...]
        kk = jnp.concatenate([zp_ref[:, 0:CHUNK], z_ref[:, O_K:O_V]], axis=0)
        vv = jnp.concatenate([zp_ref[:, CHUNK:2 * CHUNK], z_ref[:, O_V:IN_COLS]], axis=0)
        lo = _lane_lt64((CHUNK, LANES))
        outs = []
        for kv in range(B_KV_HEADS):
            kkd = _dup_half(kk, kv).astype(BF16)
            vvd = _dup_half(vv, kv).astype(BF16)
            q = _stack_heads([z_ref[:, O_Q + (kv * PAIRS + pr) * LANES:O_Q + (kv * PAIRS + pr + 1) * LANES]
                              for pr in range(PAIRS)], lo).astype(BF16)
            p, _ = _attn_probs(q, kkd, _stack_sinks(sk_ref, kv), mask)
            p_ref[kv] = p
            outs += _unstack_heads(jnp.dot(p.astype(BF16), vvd, preferred_element_type=F32), lo)
        yb = jnp.concatenate(outs, axis=1)
        o_ref[:, D_A:D_A + D_B] = (yb * _rstd(yb) * gb_ref[...]).astype(BF16)

    full = lambda shape: pl.BlockSpec(shape, lambda i: (0,) * len(shape))
    pshape = (B_KV_HEADS, HEADS_PER_KV * CHUNK, 2 * CHUNK)
    return pl.pallas_call(
        body, name=name, grid=(nb,),
        in_specs=[pl.BlockSpec((CHUNK, IN_COLS), lambda i: (i, 0)),
                  pl.BlockSpec((CHUNK, 2 * CHUNK), lambda i: (jnp.maximum(i - 1, 0), kvb)),
                  pl.BlockSpec((None, HEADS_PER_KV * CHUNK, 2 * CHUNK), lambda i: (jnp.minimum(i, 1), 0, 0)),
                  full((1, D_A)), full((A_GROUPS, CHUNK, CHUNK)), full((CHUNK, A_GROUPS)), full((1, B_Q_HEADS)),
                  full((1, D_A)), full((1, D_B))],
        out_specs=[pl.BlockSpec((CHUNK, D_A + D_B), lambda i: (i, 0)),
                   pl.BlockSpec((None,) + pshape, lambda i: (i, 0, 0, 0))],
        out_shape=[jax.ShapeDtypeStruct((T, D_A + D_B), BF16), jax.ShapeDtypeStruct((nb,) + pshape, F32)],
        compiler_params=_params(("parallel",)),
    )(z, z, _attn_bias(), gv, ws, bst, sinks, ga, gb)


def _mixer_bwd(z, dyn, probs, gv, ws, bst, ga, gb, *, name):
    T = z.shape[0]
    nb = T // CHUNK
    kvb = O_K // (2 * CHUNK)
    NT = (((0,), (0,)), ((), ()))

    def body(z_ref, zp_ref, dy_ref, p_ref, gv_ref, ws_ref, bst_ref, ga_ref, gb_ref,
             dz_ref, dgv_ref, dws_ref, dbst_ref, dsk_ref, dga_ref, dgb_ref, carry_ref):
        step = pl.program_id(0)

        @pl.when(step == 0)
        def _():
            carry_ref[...] = jnp.zeros_like(carry_ref)
            dgv_ref[...] = jnp.zeros_like(dgv_ref)
            dws_ref[...] = jnp.zeros_like(dws_ref)
            dbst_ref[...] = jnp.zeros_like(dbst_ref)
            dsk_ref[...] = jnp.zeros_like(dsk_ref)
            dga_ref[...] = jnp.zeros_like(dga_ref)
            dgb_ref[...] = jnp.zeros_like(dgb_ref)

        wsm, tril = _masked_ws(ws_ref)
        gvv = gv_ref[...]
        zu, zv, u, v, rv, vn, sv = _sgu_forward(z_ref, gvv, wsm, bst_ref[...])
        ya = u * sv
        dya, dga_rows = _rms_bwd_math(ya, ga_ref[...], dy_ref[:, 0:D_A].astype(F32))
        dga_ref[...] += jnp.sum(dga_rows, axis=0, keepdims=True)
        du = dya * sv
        dsv = dya * u
        dvn_parts = []
        for g in range(A_GROUPS):
            sl = slice(g * CHUNK, (g + 1) * CHUNK)
            dsv_g = dsv[:, sl]
            dsv_gb = dsv_g.astype(BF16)
            dw = lax.dot_general(dsv_gb, vn[:, sl], (((1,), (1,)), ((), ())), preferred_element_type=F32)
            dws_ref[g] += jnp.where(tril, dw, 0.0)
            dbst_ref[:, g:g + 1] += jnp.sum(dsv_g, axis=1, keepdims=True)
            dvn_parts.append(lax.dot_general(wsm[g], dsv_gb, NT, preferred_element_type=F32))
        dvn = jnp.concatenate(dvn_parts, axis=1)
        dv, dgv_rows = _rms_bwd_math(v, gvv, dvn)
        dgv_ref[...] += jnp.sum(dgv_rows, axis=0, keepdims=True)
        dz_ref[:, 0:D_A] = (du * (_gelu_cdf(zu) + zu * jnp.exp(-0.5 * zu * zu) * _INV_SQRT2PI)).astype(BF16)
        dz_ref[:, D_A:2 * D_A] = (dv * (_gelu_cdf(zv) + zv * jnp.exp(-0.5 * zv * zv) * _INV_SQRT2PI)).astype(BF16)

        kk = jnp.concatenate([zp_ref[:, 0:CHUNK], z_ref[:, O_K:O_V]], axis=0)
        vv = jnp.concatenate([zp_ref[:, CHUNK:2 * CHUNK], z_ref[:, O_V:IN_COLS]], axis=0)
        lo = _lane_lt64((CHUNK, LANES))
        kkd = [_dup_half(kk, kv).astype(BF16) for kv in range(B_KV_HEADS)]
        vvd = [_dup_half(vv, kv).astype(BF16) for kv in range(B_KV_HEADS)]
        outs = []
        for kv in range(B_KV_HEADS):
            outs += _unstack_heads(jnp.dot(p_ref[kv].astype(BF16), vvd[kv], preferred_element_type=F32), lo)
        yb = jnp.concatenate(outs, axis=1)
        dyb, dgb_rows = _rms_bwd_math(yb, gb_ref[...], dy_ref[:, D_A:D_A + D_B].astype(F32))
        dgb_ref[...] += jnp.sum(dgb_rows, axis=0, keepdims=True)

        dkk, dvv = [], []
        for kv in range(B_KV_HEADS):
            do = _stack_heads([dyb[:, (kv * PAIRS + pr) * LANES:(kv * PAIRS + pr + 1) * LANES]
                               for pr in range(PAIRS)], lo).astype(BF16)
            q = _stack_heads([z_ref[:, O_Q + (kv * PAIRS + pr) * LANES:O_Q + (kv * PAIRS + pr + 1) * LANES]
                              for pr in range(PAIRS)], lo).astype(BF16)
            p = p_ref[kv]
            dvv.append(lax.dot_general(p.astype(BF16), do, NT, preferred_element_type=F32))
            dp = lax.dot_general(do, vvd[kv], (((1,), (1,)), ((), ())), preferred_element_type=F32)
            delta = jnp.sum(p * dp, axis=-1, keepdims=True)
            dsink = (jnp.sum(p, axis=-1, keepdims=True) - 1.0) * delta
            for g in range(HEADS_PER_KV):
                h = kv * HEADS_PER_KV + g
                dsk_ref[:, h:h + 1] += jnp.sum(dsink[g * CHUNK:(g + 1) * CHUNK], axis=0, keepdims=True)
            ds = (p * (dp - delta) * (HEAD_DIM ** -0.5)).astype(BF16)
            dq = _unstack_heads(jnp.dot(ds, kkd[kv], preferred_element_type=F32), lo)
            for pr in range(PAIRS):
                c0 = O_Q + (kv * PAIRS + pr) * LANES
                dz_ref[:, c0:c0 + LANES] = dq[pr].astype(BF16)
            dkk.append(lax.dot_general(ds, q, NT, preferred_element_type=F32))

        def fold(parts):
            tot = [t + pltpu.roll(t, HEAD_DIM, 1) for t in parts]
            return jnp.where(_lane_lt64(tot[0].shape), tot[0], tot[1])

        dk_all = fold(dkk)
        dv_all = fold(dvv)
        dz_ref[:, O_K:O_V] = (dk_all[CHUNK:] + carry_ref[:, 0:CHUNK]).astype(BF16)
        dz_ref[:, O_V:IN_COLS] = (dv_all[CHUNK:] + carry_ref[:, CHUNK:2 * CHUNK]).astype(BF16)
        carry_ref[:, 0:CHUNK] = dk_all[:CHUNK]
        carry_ref[:, CHUNK:2 * CHUNK] = dv_all[:CHUNK]

    full = lambda shape: pl.BlockSpec(shape, lambda s: (0,) * len(shape))
    rev = lambda s: nb - 1 - s
    return pl.pallas_call(
        body, name=name, grid=(nb,),
        in_specs=[pl.BlockSpec((CHUNK, IN_COLS), lambda s: (rev(s), 0)),
                  pl.BlockSpec((CHUNK, 2 * CHUNK), lambda s: (jnp.maximum(rev(s) - 1, 0), kvb)),
                  pl.BlockSpec((CHUNK, D_A + D_B), lambda s: (rev(s), 0)),
                  pl.BlockSpec((None, B_KV_HEADS, HEADS_PER_KV * CHUNK, 2 * CHUNK), lambda s: (rev(s), 0, 0, 0)),
                  full((1, D_A)), full((A_GROUPS, CHUNK, CHUNK)), full((CHUNK, A_GROUPS)),
                  full((1, D_A)), full((1, D_B))],
        out_specs=[pl.BlockSpec((CHUNK, IN_COLS), lambda s: (rev(s), 0)),
                   full((1, D_A)), full((A_GROUPS, CHUNK, CHUNK)), full((CHUNK, A_GROUPS)), full((1, B_Q_HEADS)),
                   full((1, D_A)), full((1, D_B))],
        out_shape=[jax.ShapeDtypeStruct((T, IN_COLS), BF16), jax.ShapeDtypeStruct((1, D_A), F32),
                   jax.ShapeDtypeStruct((A_GROUPS, CHUNK, CHUNK), F32), jax.ShapeDtypeStruct((CHUNK, A_GROUPS), F32),
                   jax.ShapeDtypeStruct((1, B_Q_HEADS), F32), jax.ShapeDtypeStruct((1, D_A), F32),
                   jax.ShapeDtypeStruct((1, D_B), F32)],
        scratch_shapes=[pltpu.VMEM((CHUNK, 2 * CHUNK), F32)],
        compiler_params=_params(("arbitrary",)),
    )(z, z, dyn, probs, gv, ws, bst, ga, gb)


def _xattn_probs(qh, kh):
    s = lax.dot_general(qh, kh, (((1,), (1,)), ((), ())), preferred_element_type=F32) * (X_HEAD_DIM ** -0.5)
    e = jnp.exp(s - jnp.max(s, axis=-1, keepdims=True))
    return e / jnp.sum(e, axis=-1, keepdims=True)


def _xattn_fwd(q, kvm, *, name, tm=512):
    T = q.shape[0]
    Mm = kvm.shape[0]
    tm = min(tm, T)

    def body(q_ref, kv_ref, o_ref, p_ref):
        for h in range(X_HEADS):
            sl = slice(h * X_HEAD_DIM, (h + 1) * X_HEAD_DIM)
            kh = kv_ref[:, sl].astype(BF16)
            vh = kv_ref[:, D_MODEL + h * X_HEAD_DIM:D_MODEL + (h + 1) * X_HEAD_DIM].astype(BF16)
            p = _xattn_probs(q_ref[:, sl], kh)
            p_ref[:, h * Mm:(h + 1) * Mm] = p
            o_ref[:, sl] = jnp.dot(p.astype(BF16), vh, preferred_element_type=F32).astype(BF16)

    return pl.pallas_call(
        body, name=name, grid=(T // tm,),
        in_specs=[pl.BlockSpec((tm, D_MODEL), lambda i: (i, 0)), pl.BlockSpec((Mm, 2 * D_MODEL), lambda i: (0, 0))],
        out_specs=[pl.BlockSpec((tm, D_MODEL), lambda i: (i, 0)), pl.BlockSpec((tm, X_HEADS * Mm), lambda i: (i, 0))],
        out_shape=[jax.ShapeDtypeStruct((T, D_MODEL), BF16), jax.ShapeDtypeStruct((T, X_HEADS * Mm), F32)],
        compiler_params=_params(("parallel",)),
    )(q, kvm)


def _xattn_bwd(q, kvm, probs, do, *, name, tm=512):
    T = q.shape[0]
    Mm = kvm.shape[0]
    tm = min(tm, T)
    NT = (((0,), (0,)), ((), ()))

    def body(q_ref, kv_ref, p_ref, do_ref, dq_ref, dkv_ref):
        @pl.when(pl.program_id(0) == 0)
        def _():
            dkv_ref[...] = jnp.zeros_like(dkv_ref)

        for h in range(X_HEADS):
            sl = slice(h * X_HEAD_DIM, (h + 1) * X_HEAD_DIM)
            slv = slice(D_MODEL + h * X_HEAD_DIM, D_MODEL + (h + 1) * X_HEAD_DIM)
            kh = kv_ref[:, sl].astype(BF16)
            vh = kv_ref[:, slv].astype(BF16)
            qh = q_ref[:, sl]
            doh = do_ref[:, sl]
            p = p_ref[:, h * Mm:(h + 1) * Mm]
            dkv_ref[:, slv] += lax.dot_general(p.astype(BF16), doh, NT, preferred_element_type=F32)
            dp = lax.dot_general(doh, vh, (((1,), (1,)), ((), ())), preferred_element_type=F32)
            ds = (p * (dp - jnp.sum(p * dp, axis=-1, keepdims=True)) * (X_HEAD_DIM ** -0.5)).astype(BF16)
            dq_ref[:, sl] = jnp.dot(ds, kh, preferred_element_type=F32).astype(BF16)
            dkv_ref[:, sl] += lax.dot_general(ds, qh, NT, preferred_element_type=F32)

    row = pl.BlockSpec((tm, D_MODEL), lambda i: (i, 0))
    kvs = pl.BlockSpec((Mm, 2 * D_MODEL), lambda i: (0, 0))
    return pl.pallas_call(
        body, name=name, grid=(T // tm,),
        in_specs=[row, kvs, pl.BlockSpec((tm, X_HEADS * Mm), lambda i: (i, 0)), row], out_specs=[row, kvs],
        out_shape=[jax.ShapeDtypeStruct((T, D_MODEL), BF16), jax.ShapeDtypeStruct((Mm, 2 * D_MODEL), F32)],
        compiler_params=_params(("arbitrary",)),
    )(q, kvm, probs, do)


def _swiglu_bwd_weights(tag, n, G, U, A, wd, dhb):
    T = n.shape[0]
    dG, dU = _swiglu_bwd_act(dhb, wd, G, U, name=f"{tag}_bwd_act", tm=1024)
    dwd = _matmul([(A, dhb)], M=D_FF, N=D_MODEL, K=T, tm=1408, tn=1024, tk=2048, a_t=True, out_dtype=BF16,
                  scale=0.5, name=f"{tag}_dwd")
    dwg = _matmul([(n, dG)], M=D_MODEL, N=D_FF, K=T, tm=512, tn=D_FF // 2, tk=2048, a_t=True, out_kind="s",
                  out_dtype=BF16, order="ji", name=f"{tag}_dwg")
    dwu = _matmul([(n, dU)], M=D_MODEL, N=D_FF, K=T, tm=512, tn=D_FF // 2, tk=2048, a_t=True, out_kind="s",
                  out_dtype=BF16, order="ji", name=f"{tag}_dwu")
    return dG, dU, dwg, dwu, dwd


def _swiglu_bwd_input(tag, hin, g_norm, dG, dU, wg, wu, dh, dep):
    T = hin.shape[0]
    alt = tag == "b_ffn1"
    dn = _matmul([(dG, wg), (dU, wu)], M=T, N=D_MODEL, K=D_FF, tm=512, tn=D_MODEL // (4 if alt else 2),
                 tk=D_FF // (1 if alt else 2), b_kind="t", out_dtype=BF16, dep=dep, name=f"{tag}_dn")
    return _rms_bwd(hin, g_norm, dn, dh, name=f"{tag}_norm_bwd")


GROUP_FFN1 = ["w1_gate", "w1_up", "w1_down"]
GROUP_MID = ["w_in", "w_out", "w_xq", "w_xkv", "w_xo"]
GROUP_FFN2 = ["w2_gate", "w2_up", "w2_down"]


def _local_step(x, mem, tgt, W, comm):
    T = x.shape[0]
    Mm = mem.shape[0]
    mm = functools.partial(_matmul)

    W = {**W, **comm.gather_now("ffn1_up", ["w1_gate", "w1_up"])}
    tok = comm.gather_start("ffn1_down", ["w1_down"], after=W["w1_up"])
    tok = comm.gather_start("mid", GROUP_MID, after=tok)
    tok = comm.gather_start("ffn2", GROUP_FFN2, after=tok)
    n1 = _rms_fwd(x, W["g_ffn1"], dep=tok, name="f_norm1")
    G1, U1, A1 = _swiglu_up(n1, W["w1_gate"], W["w1_up"], name="f_ffn1_up")
    tok = comm.gather_landed("ffn1_down", after=A1)
    tok = comm.gather_landed("mid", after=tok)
    W = {**W, **comm.gather_ready("ffn1_down", after=tok)}
    h1 = mm([(A1, W["w1_down"])], M=T, N=D_MODEL, K=D_FF, tm=512, tn=D_MODEL // 2, tk=D_FF, scale=0.5, res=x,
            order="ji", name="f_ffn1_down")
    n2 = _rms_fwd(h1, W["g_mix"], name="f_norm2")
    W = {**W, **comm.gather_ready("mid", after=n2)}
    z = mm([(n2, W["w_in"])], M=T, N=IN_COLS, K=D_MODEL, tm=512, tn=IN_COLS // 2, tk=D_MODEL, name="f_w_in")
    bst = jnp.transpose(W["b_s"])
    yn, probs = _mixer_fwd(z, W["g_v"], W["w_s"], bst, W["sinks"], W["g_a_out"], W["g_b_out"], name="f_mixer")
    tok = comm.gather_landed("ffn2", after=yn)
    h2, n3 = mm([(yn, W["w_out"])], M=T, N=D_MODEL, K=D_MODEL, tm=512, tn=D_MODEL, tk=D_MODEL, res=h1,
                norm_g=W["g_x"], dep=tok, name="f_w_out")
    memn = _rms_fwd(mem, W["g_mem"], name="f_norm_mem")
    q3 = mm([(n3, W["w_xq"])], M=T, N=D_MODEL, K=D_MODEL, tm=1024, tn=D_MODEL, tk=D_MODEL, out_dtype=BF16,
            name="f_w_xq")
    kvm = mm([(memn, W["w_xkv"])], M=Mm, N=2 * D_MODEL, K=D_MODEL, tm=Mm, tn=1024, tk=D_MODEL, b_kind="n",
             name="f_w_xkv")
    o3, xprobs = _xattn_fwd(q3, kvm, name="f_xattn")
    h3, n4 = mm([(o3, W["w_xo"])], M=T, N=D_MODEL, K=D_MODEL, tm=512, tn=D_MODEL, tk=D_MODEL, res=h2,
                norm_g=W["g_ffn2"], name="f_w_xo")
    W = {**W, **comm.gather_ready("ffn2", after=n4)}
    G2, U2, A2 = _swiglu_up(n4, W["w2_gate"], W["w2_up"], name="f_ffn2_up")
    h4 = mm([(A2, W["w2_down"])], M=T, N=D_MODEL, K=D_FF, tm=512, tn=D_MODEL // 2, tk=D_FF, scale=0.5, res=h3,
            order="ji", name="f_ffn2_down")

    grads = {}
    dh4, dh4b, grads["g_final"], loss = _loss_head(h4, W["g_final"], tgt, name="loss_head")
    dG2, dU2, dwg, dwu, dwd = _swiglu_bwd_weights("b_ffn2", n4, G2, U2, A2, W["w2_down"], dh4b)
    tok = comm.reduce_pair_start("ffn2", {"w2_gate": dwg, "w2_up": dwu, "w2_down": dwd})
    dh3, dh3b, grads["g_ffn2"] = _swiglu_bwd_input("b_ffn2", h3, W["g_ffn2"], dG2, dU2, W["w2_gate"], W["w2_up"],
                                                   dh4, tok)
    tok = comm.reduce_pair_done("ffn2", after=dh3b)

    mid = {}
    do3 = mm([(dh3b, W["w_xo"])], M=T, N=D_MODEL, K=D_MODEL, tm=512, tn=D_MODEL, tk=D_MODEL, b_kind="t",
             out_dtype=BF16, dep=tok, name="b_do3")
    mid["w_xo"] = mm([(o3, dh3b)], M=D_MODEL, N=D_MODEL, K=T, tm=1024, tn=D_MODEL // 2, tk=4096, a_t=True,
                       out_dtype=BF16, name="b_dw_xo")
    dq3, dkvm = _xattn_bwd(q3, kvm, xprobs, do3, name="b_xattn")
    mid["w_xq"] = mm([(n3, dq3)], M=D_MODEL, N=D_MODEL, K=T, tm=1024, tn=D_MODEL, tk=1024, a_t=True,
                       out_dtype=BF16, name="b_dw_xq")
    dn3 = mm([(dq3, W["w_xq"])], M=T, N=D_MODEL, K=D_MODEL, tm=512, tn=D_MODEL, tk=D_MODEL, b_kind="t",
             out_dtype=BF16, name="b_dn3")
    dh2, dh2b, grads["g_x"] = _rms_bwd(h2, W["g_x"], dn3, dh3, name="b_norm3")
    dkvmb = dkvm.astype(BF16)
    mid["w_xkv"] = mm([(memn, dkvmb)], M=D_MODEL, N=2 * D_MODEL, K=Mm, tm=D_MODEL, tn=1024, tk=Mm, a_t=True,
                        out_kind="s", out_dtype=BF16, name="b_dw_xkv")
    dmemn = mm([(dkvmb, W["w_xkv"])], M=Mm, N=D_MODEL, K=2 * D_MODEL, tm=Mm, tn=D_MODEL, tk=1024, b_kind="t",
               name="b_dmemn")
    _, _, grads["g_mem"] = _rms_bwd(mem, W["g_mem"], dmemn, None, name="b_norm_mem")
    comm.reduce_finish("ffn2", after=dh2b)

    dyn = mm([(dh2b, W["w_out"])], M=T, N=D_MODEL, K=D_MODEL, tm=1024, tn=D_MODEL, tk=D_MODEL, b_kind="t",
             out_dtype=BF16, name="b_dyn")
    mid["w_out"] = mm([(yn, dh2b)], M=D_MODEL, N=D_MODEL, K=T, tm=1024, tn=D_MODEL, tk=1024, a_t=True,
                        out_dtype=BF16, name="b_dw_out")
    dz, grads["g_v"], grads["w_s"], dbst, grads["sinks"], grads["g_a_out"], grads["g_b_out"] = _mixer_bwd(
        z, dyn, probs, W["g_v"], W["w_s"], bst, W["g_a_out"], W["g_b_out"], name="b_mixer")
    grads["b_s"] = jnp.transpose(dbst)
    mid["w_in"] = mm([(n2, dz)], M=D_MODEL, N=IN_COLS, K=T, tm=1024, tn=IN_COLS, tk=1024, a_t=True,
                     out_dtype=BF16, name="b_dw_in")
    tok = comm.reduce_pair_start("mid", mid)
    dn2 = mm([(dz, W["w_in"])], M=T, N=D_MODEL, K=IN_COLS, tm=512, tn=D_MODEL, tk=IN_COLS, b_kind="t",
             out_dtype=BF16, dep=tok, name="b_dn2")
    tok = comm.reduce_pair_done("mid", after=dn2)
    dh1, dh1b, grads["g_mix"] = _rms_bwd(h1, W["g_mix"], dn2, dh2, dep=tok, name="b_norm2")

    dG1, dU1, dwg, dwu, dwd = _swiglu_bwd_weights("b_ffn1", n1, G1, U1, A1, W["w1_down"], dh1b)
    comm.reduce_finish("mid", after=dwu)
    tok = comm.reduce_start("ffn1", {"w1_gate": dwg, "w1_up": dwu, "w1_down": dwd})
    dx, _, grads["g_ffn1"] = _swiglu_bwd_input("b_ffn1", x, W["g_ffn1"], dG1, dU1, W["w1_gate"], W["w1_up"], dh1, tok)
    comm.reduce_finish("ffn1", after=dx)
    return loss, dx, grads


BIG = ["w1_gate", "w1_up", "w1_down", "w_in", "w_out", "w_xq", "w_xkv", "w_xo", "w2_gate", "w2_up", "w2_down"]
SMALL = ["g_ffn1", "g_mix", "g_v", "w_s", "b_s", "sinks", "g_a_out", "g_b_out", "g_x", "g_mem", "g_ffn2", "g_final"]
ALL_W = ["g_ffn1", "w1_gate", "w1_up", "w1_down", "g_mix", "w_in", "g_v", "w_s", "b_s", "sinks", "g_a_out",
         "g_b_out", "w_out", "g_x", "g_mem", "w_xq", "w_xkv", "w_xo", "g_ffn2", "w2_gate", "w2_up", "w2_down",
         "g_final"]
ANY = pl.BlockSpec(memory_space=pl.ANY)


def _place():
    x, y, c = lax.axis_index("x"), lax.axis_index("y"), lax.axis_index("c")
    chips = [(1 - x, y), (x, 1 - y), (1 - x, 1 - y)]
    return x, y, c, chips


COL_SHARDED = ("w1_gate", "w1_up", "w2_gate", "w2_up", "w_xkv")


def _gathered_shape(shape, by_cols):
    rows, cols = shape
    return (rows, N_CHIPS * cols) if by_cols else (N_CHIPS, rows, cols)


def _owner_rows(ref, shape, by_cols, slot, r0, rows):
    cols = shape[1]
    if by_cols:
        return ref.at[pl.ds(r0, rows), pl.ds(pl.multiple_of(slot * cols, LANES), cols)]
    return ref.at[slot, pl.ds(r0, rows), :]


def _allgather_weights(shards, by_cols, *, name):
    n = len(shards)

    def body(*refs):
        ins, outs = refs[:n], refs[n:2 * n]
        send, recv, loc = refs[2 * n:]
        x, y, c, chips = _place()
        me = 2 * x + y
        sib = (x, y, 1 - c)

        def half(w, slot, hc):
            h = shards[w].shape[0] // 2
            return _owner_rows(outs[w], shards[w].shape, by_cols[w], slot, hc * h, h)

        def copy(w, k, slot, hc, to, src=None):
            return pltpu.make_async_remote_copy(
                src_ref=half(w, slot, hc) if src is None else src, dst_ref=half(w, slot, hc),
                send_sem=send.at[6 * w + k], recv_sem=recv.at[6 * w + k], device_id=to, device_id_type=MESH)

        own = [pltpu.make_async_remote_copy(
            src_ref=ins[w], dst_ref=_owner_rows(outs[w], shards[w].shape, by_cols[w], me, 0, shards[w].shape[0]),
            send_sem=loc.at[w], recv_sem=loc.at[n + w], device_id=sib, device_id_type=MESH) for w in range(n)]
        for cp in own:
            cp.start()
        first = []
        for w in range(n):
            h = shards[w].shape[0] // 2
            for j, (tx, ty) in enumerate(chips):
                first.append(copy(w, j, me, c, (tx, ty, c), src=ins[w].at[pl.ds(c * h, h), :]))
                first[-1].start()
        passed = []
        for w in range(n):
            for j, (tx, ty) in enumerate(chips):
                slot = 2 * tx + ty
                copy(w, j, slot, c, (tx, ty, c)).wait_recv()
                passed.append(copy(w, 3 + j, slot, c, sib))
                passed[-1].start()
        for w in range(n):
            for j, (tx, ty) in enumerate(chips):
                copy(w, 3 + j, 2 * tx + ty, 1 - c, sib).wait_recv()
        for cp in first + passed:
            cp.wait_send()
        for cp in own:
            cp.wait()

    return pl.pallas_call(
        body, name=name, in_specs=[ANY] * n, out_specs=[ANY] * n,
        out_shape=[jax.ShapeDtypeStruct(_gathered_shape(s.shape, bc), s.dtype) for s, bc in zip(shards, by_cols)],
        scratch_shapes=[pltpu.SemaphoreType.DMA((6 * n,)), pltpu.SemaphoreType.DMA((6 * n,)),
                        pltpu.SemaphoreType.DMA((2 * n,))],
    )(*shards)


def _pair_exchange(grads, *, name):
    n = len(grads)

    def body(*refs):
        ins, outs = refs[:n], refs[n:2 * n]
        send, recv = refs[2 * n:]
        x, y, c, _ = _place()
        cps = []
        for w in range(n):
            h = grads[w].shape[1] // 2
            cps.append(pltpu.make_async_remote_copy(
                src_ref=ins[w].at[:, pl.ds((1 - c) * h, h), :], dst_ref=outs[w],
                send_sem=send.at[w], recv_sem=recv.at[w], device_id=(x, y, 1 - c), device_id_type=MESH))
            cps[-1].start()
        for cp in cps:
            cp.wait()

    return pl.pallas_call(
        body, name=name, in_specs=[ANY] * n, out_specs=[ANY] * n,
        out_shape=[jax.ShapeDtypeStruct((N_CHIPS, g.shape[1] // 2, g.shape[2]), g.dtype) for g in grads],
        scratch_shapes=[pltpu.SemaphoreType.DMA((n,)), pltpu.SemaphoreType.DMA((n,))],
    )(*grads)


def _pair_sum(g, got, *, name):
    S, R, C = g.shape
    h = R // 2
    tr = _row_block(h, 3 * C * 2, 16)
    nr = h // tr

    def body(a_ref, b_ref, o_ref):
        o_ref[...] = (a_ref[...].astype(F32) + b_ref[...].astype(F32)).astype(BF16)

    return pl.pallas_call(
        body, name=name, grid=(S, nr),
        in_specs=[pl.BlockSpec((None, tr, C), lambda s, r: (s, lax.axis_index("c") * nr + r, 0)),
                  pl.BlockSpec((None, tr, C), lambda s, r: (s, r, 0))],
        out_specs=pl.BlockSpec((None, tr, C), lambda s, r: (s, r, 0)),
        out_shape=jax.ShapeDtypeStruct((S, h, C), BF16),
        compiler_params=_params(("parallel", "parallel")),
    )(g, got)


def _chip_sum(part, got, *, name):
    S, h, C = part.shape
    tr = _row_block(h, 4 * C * 2 + C * 4, 16)
    nr = h // tr

    def body(own_ref, g0_ref, g1_ref, g2_ref, o_ref):
        acc = own_ref[...].astype(F32) + g0_ref[...].astype(F32)
        o_ref[...] = (acc + g1_ref[...].astype(F32)) + g2_ref[...].astype(F32)

    def piece(j):
        return pl.BlockSpec((None, tr, C), lambda r: (j, r, 0))

    return pl.pallas_call(
        body, name=name, grid=(nr,),
        in_specs=[pl.BlockSpec((None, tr, C), lambda r: (2 * lax.axis_index("x") + lax.axis_index("y"), r, 0)),
                  piece(0), piece(1), piece(2)],
        out_specs=pl.BlockSpec((tr, C), lambda r: (lax.axis_index("c") * nr + r, 0)),
        out_shape=jax.ShapeDtypeStruct((2 * h, C), F32),
        compiler_params=_params(("parallel",)),
    )(part, got, got, got)


def _pair_gather(totals, *, name):
    n = len(totals)

    def body(*refs):
        ins, outs = refs[:n], refs[n:2 * n]
        send, recv = refs[2 * n:]
        x, y, c, _ = _place()
        cps = []
        for w in range(n):
            h = totals[w].shape[0] // 2
            cps.append(pltpu.make_async_remote_copy(
                src_ref=ins[w].at[pl.ds(c * h, h), :], dst_ref=outs[w].at[pl.ds(c * h, h), :],
                send_sem=send.at[w], recv_sem=recv.at[w], device_id=(x, y, 1 - c), device_id_type=MESH))
            cps[-1].start()
        for w in range(n):
            h = totals[w].shape[0] // 2
            theirs = outs[w].at[pl.ds((1 - c) * h, h), :]
            pltpu.make_async_remote_copy(
                src_ref=theirs, dst_ref=theirs, send_sem=send.at[w], recv_sem=recv.at[w],
                device_id=(x, y, 1 - c), device_id_type=MESH).wait_recv()
        for cp in cps:
            cp.wait_send()

    return pl.pallas_call(
        body, name=name, in_specs=[ANY] * n, out_specs=[ANY] * n,
        out_shape=[jax.ShapeDtypeStruct(t.shape, t.dtype) for t in totals],
        input_output_aliases={w: w for w in range(n)},
        scratch_shapes=[pltpu.SemaphoreType.DMA((n,)), pltpu.SemaphoreType.DMA((n,))],
    )(*totals)


def _allreduce_small(v, *, name):
    R, C = v.shape
    ND = 8

    def body(v_ref, o_ref, all_ref, send, recv, loc):
        x, y, c, chips = _place()
        me, sib = (x, y, c), (x, y, 1 - c)

        def rows(px, py, pc):
            return all_ref.at[pl.ds((4 * px + 2 * py + pc) * R, R), :]

        def copy(k, block, to, src=None):
            return pltpu.make_async_remote_copy(
                src_ref=rows(*block) if src is None else src, dst_ref=rows(*block),
                send_sem=send.at[k], recv_sem=recv.at[k], device_id=to, device_id_type=MESH)

        mine = pltpu.make_async_copy(v_ref, rows(*me), loc)
        mine.start()
        first = [copy(0, me, sib, src=v_ref)]
        first += [copy(1 + j, me, (*chip, c), src=v_ref) for j, chip in enumerate(chips)]
        for cp in first:
            cp.start()
        passed = [copy(4 + j, (*chip, c), sib) for j, chip in enumerate(chips)]
        for j, chip in enumerate(chips):
            copy(1 + j, (*chip, c), me).wait_recv()
            passed[j].start()
        copy(0, sib, me).wait_recv()
        for j, chip in enumerate(chips):
            copy(4 + j, (*chip, 1 - c), me).wait_recv()
        for cp in first + passed:
            cp.wait_send()
        mine.wait()
        acc = all_ref[0:R, :]
        for d in range(1, ND):
            acc = acc + all_ref[d * R:(d + 1) * R, :]
        o_ref[...] = acc

    vm = pl.BlockSpec(memory_space=pltpu.VMEM)
    return pl.pallas_call(
        body, name=name, in_specs=[vm], out_specs=[vm, vm],
        out_shape=[jax.ShapeDtypeStruct((R, C), F32), jax.ShapeDtypeStruct((ND * R, C), F32)],
        scratch_shapes=[pltpu.SemaphoreType.DMA((7,)), pltpu.SemaphoreType.DMA((7,)), pltpu.SemaphoreType.DMA],
        compiler_params=pltpu.CompilerParams(vmem_limit_bytes=VMEM_LIMIT),
    )(v)[0]


HBM = pl.BlockSpec(memory_space=pltpu.HBM)
SEM = pl.BlockSpec(memory_space=pltpu.SEMAPHORE)
EFFECT = pltpu.SideEffectType.DATAFLOW_SIDE_EFFECTING


def _remote(src, dst, send, recv, k, to):
    return pltpu.make_async_remote_copy(src_ref=src, dst_ref=dst, send_sem=send.at[k], recv_sem=recv.at[k],
                                        device_id=to, device_id_type=MESH)


def _split_start(bufs, plan, ncopies, *, name, after=None):
    nb = len(bufs)
    extra = [] if after is None else [after]

    def body(*refs):
        pos = nb + len(extra)
        send, recv, token = refs[pos], refs[pos + 1], refs[-1]
        for k, (src, dst, to) in enumerate(plan(refs[:nb])):
            _remote(src, dst, send, recv, k, to).start()
        token[...] = jnp.zeros_like(token)

    outs = pl.pallas_call(
        body, name=name,
        out_shape=(pltpu.SemaphoreType.DMA((ncopies,)), pltpu.SemaphoreType.DMA((ncopies,)),
                   *[pltpu.HBM(b.shape, b.dtype) for b in bufs], jax.ShapeDtypeStruct((SUBLANES, LANES), F32)),
        in_specs=[HBM] * nb + [ANY] * len(extra),
        out_specs=(SEM, SEM, *[HBM] * nb, pl.BlockSpec(memory_space=pltpu.VMEM)),
        input_output_aliases={i: 2 + i for i in range(nb)},
        compiler_params=pltpu.CompilerParams(has_side_effects=EFFECT),
    )(*[pltpu.with_memory_space_constraint(b, pltpu.HBM) for b in bufs], *extra)
    return outs[0], outs[1], list(outs[2:2 + nb]), outs[-1]


def _split_wait(started, plan, after, *, name):
    send, recv, bufs, _ = started
    nb = len(bufs)

    def body(*refs):
        send_sem, recv_sem = refs[nb], refs[nb + 1]
        for k, (src, dst, to) in enumerate(plan(refs[:nb])):
            cp = _remote(src, dst, send_sem, recv_sem, k, to)
            cp.wait_send()
            cp.wait_recv()

    outs = pl.pallas_call(
        body, name=name,
        out_shape=tuple(pltpu.HBM(b.shape, b.dtype) for b in bufs),
        in_specs=[HBM] * nb + [SEM, SEM, ANY], out_specs=tuple([HBM] * nb),
        input_output_aliases={i: i for i in range(nb)},
        compiler_params=pltpu.CompilerParams(has_side_effects=EFFECT),
    )(*bufs, send, recv, after)
    return list(outs)


def _gather_chip_plan(shapes, by_cols):
    n = len(shapes)

    def plan(refs):
        srcs, lands = refs[:n], refs[n:]
        x, y, c, chips = _place()
        out = []
        for w in range(n):
            h = shapes[w][0] // 2
            mine = _owner_rows(lands[w], shapes[w], by_cols[w], 2 * x + y, c * h, h)
            for tx, ty in chips:
                out.append((srcs[w].at[pl.ds(c * h, h), :], mine, (tx, ty, c)))
        return out

    return plan


def _gather_pair_plan(shapes, by_cols):
    n = len(shapes)

    def plan(refs):
        srcs, lands = refs[:n], refs[n:]
        x, y, c, chips = _place()
        out = []
        for w in range(n):
            h = shapes[w][0] // 2
            for tx, ty in chips:
                half = _owner_rows(lands[w], shapes[w], by_cols[w], 2 * tx + ty, c * h, h)
                out.append((half, half, (x, y, 1 - c)))
            own = _owner_rows(lands[w], shapes[w], by_cols[w], 2 * x + y, 0, shapes[w][0])
            out.append((srcs[w], own, (x, y, 1 - c)))
        return out

    return plan


def _reduce_pair_plan(shapes):
    n = len(shapes)

    def plan(refs):
        local, lands = refs[:n], refs[n:]
        x, y, c, _ = _place()
        out = []
        for w in range(n):
            h = shapes[w][1] // 2
            out.append((local[w].at[:, pl.ds((1 - c) * h, h), :], lands[w], (x, y, 1 - c)))
        return out

    return plan


def _reduce_chip_plan(n):
    def plan(refs):
        parts, lands = refs[:n], refs[n:]
        x, y, c, chips = _place()
        return [(parts[w].at[2 * tx + ty], lands[w].at[j], (tx, ty, c))
                for w in range(n) for j, (tx, ty) in enumerate(chips)]

    return plan


def _as_operands(gathered):
    out = {}
    for n, g in gathered.items():
        if n in COL_SHARDED:
            out[n] = g
        elif n == "w_in":
            out[n] = jnp.transpose(g, (1, 0, 2)).reshape(D_MODEL, IN_COLS)
        else:
            out[n] = g.reshape(g.shape[0] * g.shape[1], g.shape[2])
    return out


def _by_owner(n, g):
    if n == "w_in":
        return jnp.transpose(g.reshape(D_MODEL, N_CHIPS, IN_COLS // N_CHIPS), (1, 0, 2))
    if g.ndim == 2:
        return g.reshape(N_CHIPS, g.shape[0] // N_CHIPS, g.shape[1])
    return g


class _Comm:
    def __init__(self, shards):
        self.shards = shards
        self.total = {}
        self._flight = {}

    def _layout(self, names):
        return [self.shards[n].shape for n in names], [n in COL_SHARDED for n in names]

    def gather_now(self, tag, names):
        _, by_cols = self._layout(names)
        got = _allgather_weights([self.shards[n] for n in names], by_cols, name=f"gather_{tag}")
        return _as_operands(dict(zip(names, got)))

    def gather_start(self, tag, names, after):
        shapes, by_cols = self._layout(names)
        srcs = [self.shards[n] for n in names]
        lands = [lax.empty(_gathered_shape(s.shape, bc), s.dtype) for s, bc in zip(srcs, by_cols)]
        started = _split_start(srcs + lands, _gather_chip_plan(shapes, by_cols), 3 * len(srcs),
                               after=after, name=f"gather_{tag}_chips_start")
        self._flight[tag] = (names, started)
        return started[3]

    def gather_landed(self, tag, after):
        names, started = self._flight[tag]
        shapes, by_cols = self._layout(names)
        bufs = _split_wait(started, _gather_chip_plan(shapes, by_cols), after, name=f"gather_{tag}_chips_wait")
        started = _split_start(bufs, _gather_pair_plan(shapes, by_cols), 4 * len(names),
                               name=f"gather_{tag}_pair_start")
        self._flight[tag] = (names, started)
        return started[3]

    def gather_ready(self, tag, after):
        names, started = self._flight.pop(tag)
        shapes, by_cols = self._layout(names)
        bufs = _split_wait(started, _gather_pair_plan(shapes, by_cols), after, name=f"gather_{tag}_pair_wait")
        return _as_operands(dict(zip(names, bufs[len(names):])))

    def reduce_start(self, tag, grads):
        names = list(grads)
        local = [_by_owner(n, grads[n]) for n in names]
        return self._chip_start(tag, names, local, _pair_exchange(local, name=f"pair_exchange_{tag}"))

    def reduce_pair_start(self, tag, grads):
        names = list(grads)
        local = [_by_owner(n, grads[n]) for n in names]
        lands = [lax.empty((N_CHIPS, g.shape[1] // 2, g.shape[2]), g.dtype) for g in local]
        started = _split_start(local + lands, _reduce_pair_plan([g.shape for g in local]), len(names),
                               name=f"pair_exchange_{tag}_start")
        self._flight[tag] = (names, started)
        return started[3]

    def reduce_pair_done(self, tag, after):
        names, started = self._flight.pop(tag)
        n = len(names)
        bufs = _split_wait(started, _reduce_pair_plan([b.shape for b in started[2][:n]]), after,
                           name=f"pair_exchange_{tag}_wait")
        return self._chip_start(tag, names, bufs[:n], bufs[n:])

    def _chip_start(self, tag, names, local, from_sib):
        parts = [_pair_sum(g, s, name=f"pair_sum_{n}") for n, g, s in zip(names, local, from_sib)]
        lands = [lax.empty((N_CHIPS - 1,) + p.shape[1:], p.dtype) for p in parts]
        self._flight[tag] = (names, _split_start(parts + lands, _reduce_chip_plan(len(names)), 3 * len(names),
                                                 name=f"chip_exchange_{tag}_start"))
        return self._flight[tag][1][3]

    def reduce_finish(self, tag, after):
        names, started = self._flight.pop(tag)
        n = len(names)
        bufs = _split_wait(started, _reduce_chip_plan(n), after, name=f"chip_exchange_{tag}_wait")
        totals = [_chip_sum(p, s, name=f"chip_sum_{nm}") for nm, p, s in zip(names, bufs[:n], bufs[n:])]
        self.total.update(zip(names, _pair_gather(totals, name=f"pair_gather_{tag}")))


def _adamw(w, g, m, v, *, name):
    R, C = w.shape
    tr = _row_block(R, 8 * C * 4, SUBLANES)

    def body(w_ref, g_ref, m_ref, v_ref, go_ref, d_ref, nm_ref, nv_ref):
        gg = g_ref[...]
        go_ref[...] = gg
        m_new = ADAM_B1 * m_ref[...] + (1.0 - ADAM_B1) * gg
        v_new = ADAM_B2 * v_ref[...] + (1.0 - ADAM_B2) * (gg * gg)
        m_hat = m_new / (1.0 - ADAM_B1 ** ADAM_STEP)
        v_hat = v_new / (1.0 - ADAM_B2 ** ADAM_STEP)
        d_ref[...] = -ADAM_LR * (m_hat / (jnp.sqrt(v_hat) + ADAM_EPS) + ADAM_WD * w_ref[...])
        nm_ref[...] = m_new
        nv_ref[...] = v_new

    blk = pl.BlockSpec((tr, C), lambda i: (i, 0))
    shp = jax.ShapeDtypeStruct((R, C), F32)
    return pl.pallas_call(
        body, name=name, grid=(R // tr,), in_specs=[blk] * 4, out_specs=[blk] * 4, out_shape=[shp] * 4,
        compiler_params=_params(("parallel",)),
    )(w, g, m, v)


def _to2d(a):
    flat = a.reshape(-1)
    pad = (-flat.shape[0]) % (SUBLANES * LANES)
    if pad:
        flat = jnp.pad(flat, (0, pad))
    return flat.reshape(-1, LANES)


def _small_rows(shape):
    return -(-math.prod(shape) // (SUBLANES * LANES)) * SUBLANES


def _pack_small(parts):
    rows = jnp.concatenate([_to2d(p) for p in parts], axis=0)
    pad = (-rows.shape[0]) % 256
    if pad:
        rows = jnp.concatenate([rows, jnp.zeros((pad, LANES), rows.dtype)], axis=0)
    return rows


def _unpack_small(rows, shapes):
    out, r = [], 0
    for shp in shapes:
        size = math.prod(shp)
        nrow = _small_rows(shp)
        out.append(rows[r:r + nrow].reshape(-1)[:size].reshape(shp))
        r += nrow
    return out


def kernel(x, mem, g_ffn1, w1_gate, w1_up, w1_down, g_mix, w_in, g_v, w_s, b_s, sinks, g_a_out, g_b_out, w_out, g_x, g_mem, w_xq, w_xkv, w_xo, g_ffn2, w2_gate, w2_up, w2_down, g_final, loss_target, m_g_ffn1, m_w1_gate, m_w1_up, m_w1_down, m_g_mix, m_w_in, m_g_v, m_w_s, m_b_s, m_sinks, m_g_a_out, m_g_b_out, m_w_out, m_g_x, m_g_mem, m_w_xq, m_w_xkv, m_w_xo, m_g_ffn2, m_w2_gate, m_w2_up, m_w2_down, m_g_final, v_g_ffn1, v_w1_gate, v_w1_up, v_w1_down, v_g_mix, v_w_in, v_g_v, v_w_s, v_b_s, v_sinks, v_g_a_out, v_g_b_out, v_w_out, v_g_x, v_g_mem, v_w_xq, v_w_xkv, v_w_xo, v_g_ffn2, v_w2_gate, v_w2_up, v_w2_down, v_g_final):
    args = dict(locals())
    Wp = {n: args[n] for n in ALL_W}
    Mp = {n: args["m_" + n] for n in ALL_W}
    Vp = {n: args["v_" + n] for n in ALL_W}

    comm = _Comm({n: Wp[n][0].astype(BF16) for n in BIG})
    W = {n: Wp[n] for n in SMALL}
    W["g_final"] = Wp["g_final"].reshape(1, D_MODEL)
    for n in ("w_s", "b_s"):
        W[n] = Wp[n][0]
    loss, dx, grads = _local_step(x[0], mem[0], loss_target[0], W, comm)
    big_grad = comm.total

    small_shapes = [Wp[n].shape for n in SMALL]
    packed = _pack_small([grads[n].reshape(Wp[n].shape) for n in SMALL] + [loss])
    summed = _allreduce_small(packed, name="allreduce_small")
    small_grad = dict(zip(SMALL, _unpack_small(summed, small_shapes)))
    nrows = sum(_small_rows(s) for s in small_shapes)
    loss_total = summed[nrows, 0]

    grad_out, delta, new_m, new_v = {}, {}, {}, {}
    for n in BIG:
        shp = Wp[n].shape
        g, d, nm, nv = _adamw(Wp[n][0], big_grad[n], Mp[n][0], Vp[n][0], name=f"adamw_{n}")
        grad_out[n], delta[n], new_m[n], new_v[n] = g.reshape(shp), d.reshape(shp), nm.reshape(shp), nv.reshape(shp)
    sw = _pack_small([Wp[n] for n in SMALL])
    sg = _pack_small([small_grad[n] for n in SMALL])
    sm = _pack_small([Mp[n] for n in SMALL])
    sv = _pack_small([Vp[n] for n in SMALL])
    _, d, nm, nv = _adamw(sw, sg, sm, sv, name="adamw_small")
    for n, dd, mm_, vv_ in zip(SMALL, _unpack_small(d, small_shapes), _unpack_small(nm, small_shapes),
                               _unpack_small(nv, small_shapes)):
        grad_out[n], delta[n], new_m[n], new_v[n] = small_grad[n], dd, mm_, vv_

    return (loss_total, dx[None], *[grad_out[n] for n in ALL_W], *[delta[n] for n in ALL_W],
            *[new_m[n] for n in ALL_W], *[new_v[n] for n in ALL_W])
```

```python
import functools
import math

import jax
import jax.numpy as jnp
from jax import lax
from jax.experimental import pallas as pl
from jax.experimental.pallas import tpu as pltpu

F32 = jnp.float32
BF16 = jnp.bfloat16
MESH = pl.DeviceIdType.MESH

D_MODEL = 2048
D_FF = 5632
D_A = 1024
D_B = 1024
CHUNK = 128
A_GROUPS = 8
HEAD_DIM = 64
B_Q_HEADS = 16
B_KV_HEADS = 2
X_HEADS = 4
X_HEAD_DIM = 512
IN_COLS = 3328
O_Q = 2 * D_A
O_K = O_Q + D_B
O_V = O_K + B_KV_HEADS * HEAD_DIM
N_CHIPS = 4
EPS = 1e-5
NEG = -1e30
ADAM_LR = 0.001
ADAM_B1 = 0.9
ADAM_B2 = 0.999
ADAM_EPS = 1e-08
ADAM_WD = 0.01
ADAM_STEP = 10

V7X_VMEM_BYTES = 64 * 1024 * 1024
VMEM_LIMIT = 56 * 1024 * 1024
LANES = 128
SUBLANES = 8


ANY = pl.BlockSpec(memory_space=pl.ANY)


def _params(sem, vmem=VMEM_LIMIT):
    return pltpu.CompilerParams(dimension_semantics=sem, vmem_limit_bytes=vmem)


def _matmul(pairs, *, M, N, K, tm, tn, tk, a_t=False, b_kind="n", out_kind="n", out_dtype=F32,
            scale=1.0, res=None, norm_g=None, order="ij", dep=None, name):
    tm, tn, tk = min(tm, M), min(tn, N), min(tk, K)
    assert M % tm == 0 and N % tn == 0 and K % tk == 0, (name, M, N, K, tm, tn, tk)
    nk = K // tk
    npairs = len(pairs)
    b_t = b_kind == "t"
    ns = N // N_CHIPS

    def ij(g0, g1):
        return (g0, g1) if order == "ij" else (g1, g0)

    def a_map(g0, g1, k):
        i, _ = ij(g0, g1)
        return (k, i) if a_t else (i, k)

    a_spec = pl.BlockSpec((tk, tm) if a_t else (tm, tk), a_map)

    if b_kind == "n":
        b_spec = pl.BlockSpec((tk, tn), lambda g0, g1, k: (k, ij(g0, g1)[1]))
    else:
        b_spec = pl.BlockSpec((tn, tk), lambda g0, g1, k: (ij(g0, g1)[1], k))

    if out_kind == "n":
        o_spec = pl.BlockSpec((tm, tn), lambda g0, g1, k: ij(g0, g1))
        o_shape = jax.ShapeDtypeStruct((M, N), out_dtype)
    else:
        assert tn % ns == 0
        o_spec = pl.BlockSpec((tn // ns, tm, ns), lambda g0, g1, k: (ij(g0, g1)[1], ij(g0, g1)[0], 0))
        o_shape = jax.ShapeDtypeStruct((N_CHIPS, M, ns), out_dtype)

    in_specs, args = [], []
    for a, b in pairs:
        in_specs += [a_spec, b_spec]
        args += [a, b]
    if res is not None:
        in_specs.append(pl.BlockSpec((tm, tn), lambda g0, g1, k: ij(g0, g1)))
        args.append(res)
    if norm_g is not None:
        assert tn == N and out_kind == "n"
        in_specs.append(pl.BlockSpec((1, N), lambda g0, g1, k: (0, 0)))
        args.append(norm_g)
    if dep is not None:
        in_specs.append(ANY)
        args.append(dep)

    dn = (((0,) if a_t else (1,), (1,) if b_t else (0,)), ((), ()))

    def body(*refs):
        pos = 2 * npairs
        res_ref = refs[pos] if res is not None else None
        pos += res is not None
        g_ref = refs[pos] if norm_g is not None else None
        pos += (norm_g is not None) + (dep is not None)
        o_ref = refs[pos]
        n_ref = refs[pos + 1] if norm_g is not None else None
        acc_ref = refs[-1] if nk > 1 else None
        part = None
        for p in range(npairs):
            d = lax.dot_general(refs[2 * p][...], refs[2 * p + 1][...], dn, preferred_element_type=F32)
            part = d if part is None else part + d

        def finish(acc):
            r = acc * scale if scale != 1.0 else acc
            if res_ref is not None:
                r = res_ref[...] + r
            if out_kind == "n":
                o_ref[...] = r.astype(out_dtype)
            else:
                for s in range(tn // ns):
                    o_ref[s] = r[:, s * ns:(s + 1) * ns].astype(out_dtype)
            if n_ref is not None:
                n_ref[...] = (r * _rstd(r) * g_ref[...]).astype(BF16)

        if nk == 1:
            finish(part)
        else:
            k = pl.program_id(2)

            @pl.when(k == 0)
            def _():
                acc_ref[...] = part

            @pl.when((k > 0) & (k < nk - 1))
            def _():
                acc_ref[...] += part

            @pl.when(k == nk - 1)
            def _():
                finish(acc_ref[...] + part)

    grid = (M // tm, N // tn, nk) if order == "ij" else (N // tn, M // tm, nk)
    out_specs, out_shape = o_spec, o_shape
    if norm_g is not None:
        out_specs = [o_spec, pl.BlockSpec((tm, tn), lambda g0, g1, k: ij(g0, g1))]
        out_shape = [o_shape, jax.ShapeDtypeStruct((M, N), BF16)]
    return pl.pallas_call(
        body, name=name, grid=grid, in_specs=in_specs, out_specs=out_specs, out_shape=out_shape,
        scratch_shapes=[pltpu.VMEM((tm, tn), F32)] if nk > 1 else [],
        compiler_params=_params(("parallel", "parallel", "arbitrary")),
    )(*args)


def _rstd(x):
    return lax.rsqrt(jnp.mean(x * x, axis=-1, keepdims=True) + EPS)


def _rms_bwd_math(x, g, dy):
    r = _rstd(x)
    gy = dy * g
    xr = x * r
    dx = r * (gy - xr * jnp.mean(gy * xr, axis=-1, keepdims=True))
    return dx, dy * xr


def _rms_fwd(h, g, *, name, tm=512, dep=None):
    T, Dm = h.shape
    tm = min(tm, T)

    def body(h_ref, g_ref, *rest):
        x = h_ref[...]
        rest[-1][...] = (x * _rstd(x) * g_ref[...]).astype(BF16)

    return pl.pallas_call(
        body, name=name, grid=(T // tm,),
        in_specs=[pl.BlockSpec((tm, Dm), lambda i: (i, 0)), pl.BlockSpec((1, Dm), lambda i: (0, 0))]
        + ([ANY] if dep is not None else []),
        out_specs=pl.BlockSpec((tm, Dm), lambda i: (i, 0)),
        out_shape=jax.ShapeDtypeStruct((T, Dm), BF16),
        compiler_params=_params(("parallel",)),
    )(h, g, *([dep] if dep is not None else []))


def _rms_bwd(h, g, dn, dres, *, name, tm=256, dep=None):
    T, Dm = h.shape
    tm = min(tm, T)
    has_res = dres is not None

    def body(*refs):
        h_ref, g_ref, dn_ref = refs[:3]
        pos = 3
        dres_ref = refs[pos] if has_res else None
        pos += has_res + (dep is not None)
        dh_ref, dhb_ref, dg_ref = refs[pos:pos + 3]
        dx, dgr = _rms_bwd_math(h_ref[...], g_ref[...], dn_ref[...].astype(F32))
        if has_res:
            dx = dres_ref[...] + dx
        dh_ref[...] = dx
        dhb_ref[...] = dx.astype(BF16)
        part = jnp.sum(dgr, axis=0, keepdims=True)

        @pl.when(pl.program_id(0) == 0)
        def _():
            dg_ref[...] = part

        @pl.when(pl.program_id(0) > 0)
        def _():
            dg_ref[...] += part

    row = pl.BlockSpec((tm, Dm), lambda i: (i, 0))
    vec = pl.BlockSpec((1, Dm), lambda i: (0, 0))
    args = [h, g, dn] + ([dres] if has_res else []) + ([dep] if dep is not None else [])
    return pl.pallas_call(
        body, name=name, grid=(T // tm,),
        in_specs=[row, vec, row] + ([row] if has_res else []) + ([ANY] if dep is not None else []),
        out_specs=[row, row, vec],
        out_shape=[jax.ShapeDtypeStruct((T, Dm), F32), jax.ShapeDtypeStruct((T, Dm), BF16),
                   jax.ShapeDtypeStruct((1, Dm), F32)],
        compiler_params=_params(("arbitrary",)),
    )(*args)


def _loss_head(h, g, tgt, *, name, tm=256):
    T, Dm = h.shape
    tm = min(tm, T)

    def body(h_ref, g_ref, t_ref, dh_ref, dhb_ref, dg_ref, loss_ref):
        x = h_ref[...]
        gv = g_ref[...]
        r = _rstd(x)
        diff = x * r * gv - t_ref[...]
        lpart = 0.5 * jnp.sum(jnp.mean(diff * diff, axis=-1, keepdims=True), axis=0, keepdims=True)
        dx, dgr = _rms_bwd_math(x, gv, diff * (1.0 / Dm))
        dh_ref[...] = dx
        dhb_ref[...] = dx.astype(BF16)
        part = jnp.sum(dgr, axis=0, keepdims=True)
        lrow = jnp.broadcast_to(lpart, (1, LANES))

        @pl.when(pl.program_id(0) == 0)
        def _():
            dg_ref[...] = part
            loss_ref[...] = lrow

        @pl.when(pl.program_id(0) > 0)
        def _():
            dg_ref[...] += part
            loss_ref[...] += lrow

    row = pl.BlockSpec((tm, Dm), lambda i: (i, 0))
    vec = pl.BlockSpec((1, Dm), lambda i: (0, 0))
    return pl.pallas_call(
        body, name=name, grid=(T // tm,),
        in_specs=[row, vec, row],
        out_specs=[row, row, vec, pl.BlockSpec((1, LANES), lambda i: (0, 0))],
        out_shape=[jax.ShapeDtypeStruct((T, Dm), F32), jax.ShapeDtypeStruct((T, Dm), BF16),
                   jax.ShapeDtypeStruct((1, Dm), F32), jax.ShapeDtypeStruct((1, LANES), F32)],
        compiler_params=_params(("arbitrary",)),
    )(h, g, tgt)


MXU_COLS = 256
FF_TILE = 2 * MXU_COLS


def _row_block(rows, row_bytes, align, budget=24 * 1024 * 1024):
    fits = [d for d in range(align, rows + 1, align) if rows % d == 0 and 2 * d * row_bytes <= budget]
    assert fits, (rows, row_bytes)
    return fits[-1]


def _swiglu_up(n, wg, wu, *, name, tm=1024, tn=FF_TILE):
    T, Dm = n.shape
    Fd = wg.shape[1]
    tm = min(tm, T)

    def body(n_ref, wg_ref, wu_ref, g_ref, u_ref, a_ref):
        x = n_ref[...]
        g = jnp.dot(x, wg_ref[...], preferred_element_type=F32)
        u = jnp.dot(x, wu_ref[...], preferred_element_type=F32)
        g_ref[...] = g.astype(BF16)
        u_ref[...] = u.astype(BF16)
        a_ref[...] = (g * jax.nn.sigmoid(g) * u).astype(BF16)

    wspec = pl.BlockSpec((Dm, tn), lambda j, i: (0, j))
    ospec = pl.BlockSpec((tm, tn), lambda j, i: (i, j))
    oshape = jax.ShapeDtypeStruct((T, Fd), BF16)
    return pl.pallas_call(
        body, name=name, grid=(Fd // tn, T // tm),
        in_specs=[pl.BlockSpec((tm, Dm), lambda j, i: (i, 0)), wspec, wspec],
        out_specs=[ospec, ospec, ospec], out_shape=[oshape, oshape, oshape],
        compiler_params=_params(("parallel", "parallel")),
    )(n, wg, wu)


def _swiglu_bwd_act(dhb, wd, G, U, *, name, tm=1024, tn=D_FF // N_CHIPS):
    T, Dm = dhb.shape
    Fd = wd.shape[0]
    tm, tn = min(tm, T), min(tn, Fd)

    def body(dh_ref, wd_ref, g_ref, u_ref, dg_ref, du_ref):
        da = 0.5 * lax.dot_general(dh_ref[...], wd_ref[...], (((1,), (1,)), ((), ())), preferred_element_type=F32)
        g = g_ref[...].astype(F32)
        u = u_ref[...].astype(F32)
        sg = jax.nn.sigmoid(g)
        dg_ref[...] = (da * u * (sg * (1.0 + g * (1.0 - sg)))).astype(BF16)
        du_ref[...] = (da * (g * sg)).astype(BF16)

    blk = pl.BlockSpec((tm, tn), lambda j, i: (i, j))
    oshape = jax.ShapeDtypeStruct((T, Fd), BF16)
    return pl.pallas_call(
        body, name=name, grid=(Fd // tn, T // tm),
        in_specs=[pl.BlockSpec((tm, Dm), lambda j, i: (i, 0)), pl.BlockSpec((tn, Dm), lambda j, i: (j, 0)), blk, blk],
        out_specs=[blk, blk], out_shape=[oshape, oshape],
        compiler_params=_params(("parallel", "parallel")),
    )(dhb, wd, G, U)


_INV_SQRT2 = 0.7071067811865476
_INV_SQRT2PI = 0.3989422804014327


def _erf(x):
    ax = jnp.abs(x)
    t = 1.0 / (1.0 + 0.3275911 * ax)
    poly = t * (0.254829592 + t * (-0.284496736 + t * (1.421413741 + t * (-1.453152027 + t * 1.061405429))))
    y = 1.0 - poly * jnp.exp(-ax * ax)
    return jnp.where(x < 0, -y, y)


def _gelu_cdf(x):
    return 0.5 * (1.0 + _erf(x * _INV_SQRT2))


def _lane_lt64(shape):
    return lax.broadcasted_iota(jnp.int32, shape, len(shape) - 1) < HEAD_DIM


def _dup_half(x, kv):
    rolled = pltpu.roll(x, HEAD_DIM, 1)
    lo = _lane_lt64(x.shape)
    return jnp.where(lo, x, rolled) if kv == 0 else jnp.where(lo, rolled, x)


HEADS_PER_KV = B_Q_HEADS // B_KV_HEADS
PAIRS = HEADS_PER_KV // 2


def _attn_bias():
    shape = (2 * CHUNK, HEADS_PER_KV * CHUNK)
    qpos = (lax.broadcasted_iota(jnp.int32, shape, 1) & (CHUNK - 1)) + CHUNK
    kpos = lax.broadcasted_iota(jnp.int32, shape, 0)
    diff = qpos - kpos
    band = (diff >= 0) & (diff < CHUNK)
    return jnp.stack([jnp.where(band & (kpos >= CHUNK), 0.0, NEG), jnp.where(band, 0.0, NEG)]).astype(F32)


def _stack_heads(tiles, lo):
    parts = []
    for t in tiles:
        parts += [jnp.where(lo, t, 0.0), jnp.where(lo, 0.0, t)]
    return jnp.concatenate(parts, axis=0)


def _unstack_heads(s, lo):
    return [jnp.where(lo, s[2 * p * CHUNK:(2 * p + 1) * CHUNK], s[(2 * p + 1) * CHUNK:(2 * p + 2) * CHUNK])
            for p in range(PAIRS)]


def _stack_sinks(sk_ref, kv):
    return jnp.concatenate([jnp.broadcast_to(sk_ref[:, h:h + 1], (1, CHUNK))
                            for h in range(kv * HEADS_PER_KV, (kv + 1) * HEADS_PER_KV)], axis=1)


def _sgu_forward(z_ref, gv, wsm, bst):
    zu = z_ref[:, 0:D_A]
    zv = z_ref[:, D_A:2 * D_A]
    u = zu * _gelu_cdf(zu)
    v = zv * _gelu_cdf(zv)
    rv = _rstd(v)
    vn = (v * rv * gv).astype(BF16)
    svs = []
    for g in range(A_GROUPS):
        sl = slice(g * CHUNK, (g + 1) * CHUNK)
        svs.append(jnp.dot(wsm[g], vn[:, sl], preferred_element_type=F32) + bst[:, g:g + 1])
    sv = jnp.concatenate(svs, axis=1)
    return zu, zv, u, v, rv, vn, sv


def _masked_ws(ws_ref):
    tril = lax.broadcasted_iota(jnp.int32, (CHUNK, CHUNK), 0) >= lax.broadcasted_iota(jnp.int32, (CHUNK, CHUNK), 1)
    return [jnp.where(tril, ws_ref[g], 0.0).astype(BF16) for g in range(A_GROUPS)], tril


def _attn_probs(qm, kkd, sink, bias):
    s = lax.dot_general(kkd, qm, (((1,), (1,)), ((), ())), preferred_element_type=F32) * (HEAD_DIM ** -0.5) + bias
    m = jnp.maximum(jnp.max(s, axis=0, keepdims=True), sink)
    e = jnp.exp(s - m)
    inv = 1.0 / (jnp.sum(e, axis=0, keepdims=True) + jnp.exp(sink - m))
    return e * inv


def _mixer_fwd(z, gv, ws, bst, sinks, ga, gb, *, name):
    T = z.shape[0]
    nb = T // CHUNK
    kvb = O_K // (2 * CHUNK)

    def body(z_ref, zp_ref, bias_ref, gv_ref, ws_ref, bst_ref, sk_ref, ga_ref, gb_ref, o_ref, p_ref):
        wsm, _ = _masked_ws(ws_ref)
        _, _, u, _, _, _, sv = _sgu_forward(z_ref, gv_ref[...], wsm, bst_ref[...])
        ya = u * sv
        o_ref[:, 0:D_A] = (ya * _rstd(ya) * ga_ref[...]).astype(BF16)

        mask = bias_ref[...]
        kk = jnp.concatenate([zp_ref[:, 0:CHUNK], z_ref[:, O_K:O_V]], axis=0)
        vv = jnp.concatenate([zp_ref[:, CHUNK:2 * CHUNK], z_ref[:, O_V:IN_COLS]], axis=0)
        lo = _lane_lt64((CHUNK, LANES))
        outs = []
        for kv in range(B_KV_HEADS):
            kkd = _dup_half(kk, kv).astype(BF16)
            vvd = _dup_half(vv, kv).astype(BF16)
            q = _stack_heads([z_ref[:, O_Q + (kv * PAIRS + pr) * LANES:O_Q + (kv * PAIRS + pr + 1) * LANES]
                              for pr in range(PAIRS)], lo).astype(BF16)
            p = _attn_probs(q, kkd, _stack_sinks(sk_ref, kv), mask)
            p_ref[kv] = p
            out = lax.dot_general(p.astype(BF16), vvd, (((0,), (0,)), ((), ())), preferred_element_type=F32)
            outs += _unstack_heads(out, lo)
        yb = jnp.concatenate(outs, axis=1)
        o_ref[:, D_A:D_A + D_B] = (yb * _rstd(yb) * gb_ref[...]).astype(BF16)

    full = lambda shape: pl.BlockSpec(shape, lambda i: (0,) * len(shape))
    pshape = (B_KV_HEADS, 2 * CHUNK, HEADS_PER_KV * CHUNK)
    return pl.pallas_call(
        body, name=name, grid=(nb,),
        in_specs=[pl.BlockSpec((CHUNK, IN_COLS), lambda i: (i, 0)),
                  pl.BlockSpec((CHUNK, 2 * CHUNK), lambda i: (jnp.maximum(i - 1, 0), kvb)),
                  pl.BlockSpec((None, 2 * CHUNK, HEADS_PER_KV * CHUNK), lambda i: (jnp.minimum(i, 1), 0, 0)),
                  full((1, D_A)), full((A_GROUPS, CHUNK, CHUNK)), full((CHUNK, A_GROUPS)), full((1, B_Q_HEADS)),
                  full((1, D_A)), full((1, D_B))],
        out_specs=[pl.BlockSpec((CHUNK, D_A + D_B), lambda i: (i, 0)),
                   pl.BlockSpec((None,) + pshape, lambda i: (i, 0, 0, 0))],
        out_shape=[jax.ShapeDtypeStruct((T, D_A + D_B), BF16), jax.ShapeDtypeStruct((nb,) + pshape, F32)],
        compiler_params=_params(("parallel",)),
    )(z, z, _attn_bias(), gv, ws, bst, sinks, ga, gb)


def _mixer_bwd(z, dyn, probs, gv, ws, bst, ga, gb, *, name):
    T = z.shape[0]
    nb = T // CHUNK
    kvb = O_K // (2 * CHUNK)
    NT = (((0,), (0,)), ((), ()))

    def body(z_ref, zp_ref, dy_ref, p_ref, gv_ref, ws_ref, bst_ref, ga_ref, gb_ref,
             dz_ref, dgv_ref, dws_ref, dbst_ref, dsk_ref, dga_ref, dgb_ref, carry_ref):
        step = pl.program_id(0)

        @pl.when(step == 0)
        def _():
            carry_ref[...] = jnp.zeros_like(carry_ref)
            dgv_ref[...] = jnp.zeros_like(dgv_ref)
            dws_ref[...] = jnp.zeros_like(dws_ref)
            dbst_ref[...] = jnp.zeros_like(dbst_ref)
            dsk_ref[...] = jnp.zeros_like(dsk_ref)
            dga_ref[...] = jnp.zeros_like(dga_ref)
            dgb_ref[...] = jnp.zeros_like(dgb_ref)

        wsm, tril = _masked_ws(ws_ref)
        gvv = gv_ref[...]
        zu, zv, u, v, rv, vn, sv = _sgu_forward(z_ref, gvv, wsm, bst_ref[...])
        ya = u * sv
        dya, dga_rows = _rms_bwd_math(ya, ga_ref[...], dy_ref[:, 0:D_A].astype(F32))
        dga_ref[...] += jnp.sum(dga_rows, axis=0, keepdims=True)
        du = dya * sv
        dsv = dya * u
        dvn_parts = []
        for g in range(A_GROUPS):
            sl = slice(g * CHUNK, (g + 1) * CHUNK)
            dsv_g = dsv[:, sl]
            dsv_gb = dsv_g.astype(BF16)
            dw = lax.dot_general(dsv_gb, vn[:, sl], (((1,), (1,)), ((), ())), preferred_element_type=F32)
            dws_ref[g] += jnp.where(tril, dw, 0.0)
            dbst_ref[:, g:g + 1] += jnp.sum(dsv_g, axis=1, keepdims=True)
            dvn_parts.append(lax.dot_general(wsm[g], dsv_gb, NT, preferred_element_type=F32))
        dvn = jnp.concatenate(dvn_parts, axis=1)
        dv, dgv_rows = _rms_bwd_math(v, gvv, dvn)
        dgv_ref[...] += jnp.sum(dgv_rows, axis=0, keepdims=True)
        dz_ref[:, 0:D_A] = (du * (_gelu_cdf(zu) + zu * jnp.exp(-0.5 * zu * zu) * _INV_SQRT2PI)).astype(BF16)
        dz_ref[:, D_A:2 * D_A] = (dv * (_gelu_cdf(zv) + zv * jnp.exp(-0.5 * zv * zv) * _INV_SQRT2PI)).astype(BF16)

        kk = jnp.concatenate([zp_ref[:, 0:CHUNK], z_ref[:, O_K:O_V]], axis=0)
        vv = jnp.concatenate([zp_ref[:, CHUNK:2 * CHUNK], z_ref[:, O_V:IN_COLS]], axis=0)
        lo = _lane_lt64((CHUNK, LANES))
        kkd = [_dup_half(kk, kv).astype(BF16) for kv in range(B_KV_HEADS)]
        vvd = [_dup_half(vv, kv).astype(BF16) for kv in range(B_KV_HEADS)]
        outs = []
        for kv in range(B_KV_HEADS):
            outs += _unstack_heads(lax.dot_general(p_ref[kv].astype(BF16), vvd[kv], NT, preferred_element_type=F32), lo)
        yb = jnp.concatenate(outs, axis=1)
        dyb, dgb_rows = _rms_bwd_math(yb, gb_ref[...], dy_ref[:, D_A:D_A + D_B].astype(F32))
        dgb_ref[...] += jnp.sum(dgb_rows, axis=0, keepdims=True)

        dkk, dvv = [], []
        for kv in range(B_KV_HEADS):
            do = _stack_heads([dyb[:, (kv * PAIRS + pr) * LANES:(kv * PAIRS + pr + 1) * LANES]
                               for pr in range(PAIRS)], lo).astype(BF16)
            q = _stack_heads([z_ref[:, O_Q + (kv * PAIRS + pr) * LANES:O_Q + (kv * PAIRS + pr + 1) * LANES]
                              for pr in range(PAIRS)], lo).astype(BF16)
            p = p_ref[kv]
            dvv.append(jnp.dot(p.astype(BF16), do, preferred_element_type=F32))
            dp = lax.dot_general(vvd[kv], do, (((1,), (1,)), ((), ())), preferred_element_type=F32)
            delta = jnp.sum(p * dp, axis=0, keepdims=True)
            dsink = (jnp.sum(p, axis=0, keepdims=True) - 1.0) * delta
            for g in range(HEADS_PER_KV):
                h = kv * HEADS_PER_KV + g
                dsk_ref[:, h:h + 1] += jnp.sum(dsink[:, g * CHUNK:(g + 1) * CHUNK], axis=1, keepdims=True)
            ds = (p * (dp - delta) * (HEAD_DIM ** -0.5)).astype(BF16)
            dq = _unstack_heads(lax.dot_general(ds, kkd[kv], NT, preferred_element_type=F32), lo)
            for pr in range(PAIRS):
                c0 = O_Q + (kv * PAIRS + pr) * LANES
                dz_ref[:, c0:c0 + LANES] = dq[pr].astype(BF16)
            dkk.append(jnp.dot(ds, q, preferred_element_type=F32))

        def fold(parts):
            tot = [t + pltpu.roll(t, HEAD_DIM, 1) for t in parts]
            return jnp.where(_lane_lt64(tot[0].shape), tot[0], tot[1])

        dk_all = fold(dkk)
        dv_all = fold(dvv)
        dz_ref[:, O_K:O_V] = (dk_all[CHUNK:] + carry_ref[:, 0:CHUNK]).astype(BF16)
        dz_ref[:, O_V:IN_COLS] = (dv_all[CHUNK:] + carry_ref[:, CHUNK:2 * CHUNK]).astype(BF16)
        carry_ref[:, 0:CHUNK] = dk_all[:CHUNK]
        carry_ref[:, CHUNK:2 * CHUNK] = dv_all[:CHUNK]

    full = lambda shape: pl.BlockSpec(shape, lambda s: (0,) * len(shape))
    rev = lambda s: nb - 1 - s
    return pl.pallas_call(
        body, name=name, grid=(nb,),
        in_specs=[pl.BlockSpec((CHUNK, IN_COLS), lambda s: (rev(s), 0)),
                  pl.BlockSpec((CHUNK, 2 * CHUNK), lambda s: (jnp.maximum(rev(s) - 1, 0), kvb)),
                  pl.BlockSpec((CHUNK, D_A + D_B), lambda s: (rev(s), 0)),
                  pl.BlockSpec((None, B_KV_HEADS, 2 * CHUNK, HEADS_PER_KV * CHUNK), lambda s: (rev(s), 0, 0, 0)),
                  full((1, D_A)), full((A_GROUPS, CHUNK, CHUNK)), full((CHUNK, A_GROUPS)),
                  full((1, D_A)), full((1, D_B))],
        out_specs=[pl.BlockSpec((CHUNK, IN_COLS), lambda s: (rev(s), 0)),
                   full((1, D_A)), full((A_GROUPS, CHUNK, CHUNK)), full((CHUNK, A_GROUPS)), full((1, B_Q_HEADS)),
                   full((1, D_A)), full((1, D_B))],
        out_shape=[jax.ShapeDtypeStruct((T, IN_COLS), BF16), jax.ShapeDtypeStruct((1, D_A), F32),
                   jax.ShapeDtypeStruct((A_GROUPS, CHUNK, CHUNK), F32), jax.ShapeDtypeStruct((CHUNK, A_GROUPS), F32),
                   jax.ShapeDtypeStruct((1, B_Q_HEADS), F32), jax.ShapeDtypeStruct((1, D_A), F32),
                   jax.ShapeDtypeStruct((1, D_B), F32)],
        scratch_shapes=[pltpu.VMEM((CHUNK, 2 * CHUNK), F32)],
        compiler_params=_params(("arbitrary",)),
    )(z, z, dyn, probs, gv, ws, bst, ga, gb)


def _xattn_probs(qh, kh):
    s = lax.dot_general(qh, kh, (((1,), (1,)), ((), ())), preferred_element_type=F32) * (X_HEAD_DIM ** -0.5)
    e = jnp.exp(s - jnp.max(s, axis=-1, keepdims=True))
    return e / jnp.sum(e, axis=-1, keepdims=True)


def _xattn_fwd(q, kvm, *, name, tm=512):
    T = q.shape[0]
    Mm = kvm.shape[0]
    tm = min(tm, T)

    def body(q_ref, kv_ref, o_ref, p_ref):
        for h in range(X_HEADS):
            sl = slice(h * X_HEAD_DIM, (h + 1) * X_HEAD_DIM)
            kh = kv_ref[:, sl].astype(BF16)
            vh = kv_ref[:, D_MODEL + h * X_HEAD_DIM:D_MODEL + (h + 1) * X_HEAD_DIM].astype(BF16)
            p = _xattn_probs(q_ref[:, sl], kh)
            p_ref[:, h * Mm:(h + 1) * Mm] = p
            o_ref[:, sl] = jnp.dot(p.astype(BF16), vh, preferred_element_type=F32).astype(BF16)

    return pl.pallas_call(
        body, name=name, grid=(T // tm,),
        in_specs=[pl.BlockSpec((tm, D_MODEL), lambda i: (i, 0)), pl.BlockSpec((Mm, 2 * D_MODEL), lambda i: (0, 0))],
        out_specs=[pl.BlockSpec((tm, D_MODEL), lambda i: (i, 0)), pl.BlockSpec((tm, X_HEADS * Mm), lambda i: (i, 0))],
        out_shape=[jax.ShapeDtypeStruct((T, D_MODEL), BF16), jax.ShapeDtypeStruct((T, X_HEADS * Mm), F32)],
        compiler_params=_params(("parallel",)),
    )(q, kvm)


def _xattn_bwd(q, kvm, probs, do, *, name, tm=512):
    T = q.shape[0]
    Mm = kvm.shape[0]
    tm = min(tm, T)
    NT = (((0,), (0,)), ((), ()))

    def body(q_ref, kv_ref, p_ref, do_ref, dq_ref, dkv_ref):
        @pl.when(pl.program_id(0) == 0)
        def _():
            dkv_ref[...] = jnp.zeros_like(dkv_ref)

        for h in range(X_HEADS):
            sl = slice(h * X_HEAD_DIM, (h + 1) * X_HEAD_DIM)
            slv = slice(D_MODEL + h * X_HEAD_DIM, D_MODEL + (h + 1) * X_HEAD_DIM)
            kh = kv_ref[:, sl].astype(BF16)
            vh = kv_ref[:, slv].astype(BF16)
            qh = q_ref[:, sl]
            doh = do_ref[:, sl]
            p = p_ref[:, h * Mm:(h + 1) * Mm]
            dkv_ref[:, slv] += lax.dot_general(p.astype(BF16), doh, NT, preferred_element_type=F32)
            dp = lax.dot_general(doh, vh, (((1,), (1,)), ((), ())), preferred_element_type=F32)
            ds = (p * (dp - jnp.sum(p * dp, axis=-1, keepdims=True)) * (X_HEAD_DIM ** -0.5)).astype(BF16)
            dq_ref[:, sl] = jnp.dot(ds, kh, preferred_element_type=F32).astype(BF16)
            dkv_ref[:, sl] += lax.dot_general(ds, qh, NT, preferred_element_type=F32)

    row = pl.BlockSpec((tm, D_MODEL), lambda i: (i, 0))
    kvs = pl.BlockSpec((Mm, 2 * D_MODEL), lambda i: (0, 0))
    return pl.pallas_call(
        body, name=name, grid=(T // tm,),
        in_specs=[row, kvs, pl.BlockSpec((tm, X_HEADS * Mm), lambda i: (i, 0)), row], out_specs=[row, kvs],
        out_shape=[jax.ShapeDtypeStruct((T, D_MODEL), BF16), jax.ShapeDtypeStruct((Mm, 2 * D_MODEL), F32)],
        compiler_params=_params(("arbitrary",)),
    )(q, kvm, probs, do)


def _swiglu_bwd_weights(tag, n, G, U, A, wd, dhb):
    T = n.shape[0]
    dG, dU = _swiglu_bwd_act(dhb, wd, G, U, name=f"{tag}_bwd_act", tm=1024)
    dwd = _matmul([(A, dhb)], M=D_FF, N=D_MODEL, K=T, tm=1408, tn=1024, tk=2048, a_t=True, out_dtype=BF16,
                  scale=0.5, name=f"{tag}_dwd")
    dwg = _matmul([(n, dG)], M=D_MODEL, N=D_FF, K=T, tm=512, tn=D_FF // 2, tk=2048, a_t=True, out_kind="s",
                  out_dtype=BF16, order="ji", name=f"{tag}_dwg")
    dwu = _matmul([(n, dU)], M=D_MODEL, N=D_FF, K=T, tm=512, tn=D_FF // 2, tk=2048, a_t=True, out_kind="s",
                  out_dtype=BF16, order="ji", name=f"{tag}_dwu")
    return dG, dU, dwg, dwu, dwd


def _swiglu_bwd_input(tag, hin, g_norm, dG, dU, wg, wu, dh, dep):
    T = hin.shape[0]
    dn = _matmul([(dG, wg), (dU, wu)], M=T, N=D_MODEL, K=D_FF, tm=512, tn=D_MODEL // 2, tk=D_FF // 2, b_kind="t",
                 out_dtype=BF16, dep=dep, name=f"{tag}_dn")
    return _rms_bwd(hin, g_norm, dn, dh, name=f"{tag}_norm_bwd")


GROUP_FFN1 = ["w1_gate", "w1_up", "w1_down"]
GROUP_MID = ["w_in", "w_out", "w_xq", "w_xkv", "w_xo"]
GROUP_FFN2 = ["w2_gate", "w2_up", "w2_down"]


def _local_step(x, mem, tgt, W, comm):
    T = x.shape[0]
    Mm = mem.shape[0]
    mm = functools.partial(_matmul)

    W = {**W, **comm.gather_now("ffn1_up", ["w1_gate", "w1_up"])}
    tok = comm.gather_start("ffn1_down", ["w1_down"], after=W["w1_up"])
    tok = comm.gather_start("mid", GROUP_MID, after=tok)
    tok = comm.gather_start("ffn2", GROUP_FFN2, after=tok)
    n1 = _rms_fwd(x, W["g_ffn1"], dep=tok, name="f_norm1")
    G1, U1, A1 = _swiglu_up(n1, W["w1_gate"], W["w1_up"], name="f_ffn1_up")
    tok = comm.gather_landed("ffn1_down", after=A1)
    tok = comm.gather_landed("mid", after=tok)
    W = {**W, **comm.gather_ready("ffn1_down", after=tok)}
    h1 = mm([(A1, W["w1_down"])], M=T, N=D_MODEL, K=D_FF, tm=512, tn=D_MODEL // 2, tk=D_FF, scale=0.5, res=x,
            order="ji", name="f_ffn1_down")
    n2 = _rms_fwd(h1, W["g_mix"], name="f_norm2")
    W = {**W, **comm.gather_ready("mid", after=n2)}
    z = mm([(n2, W["w_in"])], M=T, N=IN_COLS, K=D_MODEL, tm=512, tn=IN_COLS // 2, tk=D_MODEL, name="f_w_in")
    bst = jnp.transpose(W["b_s"])
    yn, probs = _mixer_fwd(z, W["g_v"], W["w_s"], bst, W["sinks"], W["g_a_out"], W["g_b_out"], name="f_mixer")
    tok = comm.gather_landed("ffn2", after=yn)
    h2, n3 = mm([(yn, W["w_out"])], M=T, N=D_MODEL, K=D_MODEL, tm=512, tn=D_MODEL, tk=D_MODEL, res=h1,
                norm_g=W["g_x"], dep=tok, name="f_w_out")
    memn = _rms_fwd(mem, W["g_mem"], name="f_norm_mem")
    q3 = mm([(n3, W["w_xq"])], M=T, N=D_MODEL, K=D_MODEL, tm=1024, tn=D_MODEL, tk=D_MODEL, out_dtype=BF16,
            name="f_w_xq")
    kvm = mm([(memn, W["w_xkv"])], M=Mm, N=2 * D_MODEL, K=D_MODEL, tm=Mm, tn=1024, tk=D_MODEL, b_kind="n",
             name="f_w_xkv")
    o3, xprobs = _xattn_fwd(q3, kvm, name="f_xattn")
    h3, n4 = mm([(o3, W["w_xo"])], M=T, N=D_MODEL, K=D_MODEL, tm=512, tn=D_MODEL, tk=D_MODEL, res=h2,
                norm_g=W["g_ffn2"], name="f_w_xo")
    W = {**W, **comm.gather_ready("ffn2", after=n4)}
    G2, U2, A2 = _swiglu_up(n4, W["w2_gate"], W["w2_up"], name="f_ffn2_up")
    h4 = mm([(A2, W["w2_down"])], M=T, N=D_MODEL, K=D_FF, tm=512, tn=D_MODEL // 2, tk=D_FF, scale=0.5, res=h3,
            order="ji", name="f_ffn2_down")

    grads = {}
    dh4, dh4b, grads["g_final"], loss = _loss_head(h4, W["g_final"], tgt, name="loss_head")
    dG2, dU2, dwg, dwu, dwd = _swiglu_bwd_weights("b_ffn2", n4, G2, U2, A2, W["w2_down"], dh4b)
    tok = comm.reduce_pair_start("ffn2", {"w2_gate": dwg, "w2_up": dwu, "w2_down": dwd})
    dh3, dh3b, grads["g_ffn2"] = _swiglu_bwd_input("b_ffn2", h3, W["g_ffn2"], dG2, dU2, W["w2_gate"], W["w2_up"],
                                                   dh4, tok)
    tok = comm.reduce_pair_done("ffn2", after=dh3b)

    mid = {}
    do3 = mm([(dh3b, W["w_xo"])], M=T, N=D_MODEL, K=D_MODEL, tm=512, tn=D_MODEL, tk=D_MODEL, b_kind="t",
             out_dtype=BF16, dep=tok, name="b_do3")
    mid["w_xo"] = mm([(o3, dh3b)], M=D_MODEL, N=D_MODEL, K=T, tm=1024, tn=D_MODEL // 2, tk=4096, a_t=True,
                       out_dtype=BF16, name="b_dw_xo")
    dq3, dkvm = _xattn_bwd(q3, kvm, xprobs, do3, name="b_xattn")
    mid["w_xq"] = mm([(n3, dq3)], M=D_MODEL, N=D_MODEL, K=T, tm=1024, tn=D_MODEL // 2, tk=4096, a_t=True,
                       out_dtype=BF16, name="b_dw_xq")
    dn3 = mm([(dq3, W["w_xq"])], M=T, N=D_MODEL, K=D_MODEL, tm=512, tn=D_MODEL, tk=D_MODEL, b_kind="t",
             out_dtype=BF16, name="b_dn3")
    dh2, dh2b, grads["g_x"] = _rms_bwd(h2, W["g_x"], dn3, dh3, name="b_norm3")
    dkvmb = dkvm.astype(BF16)
    mid["w_xkv"] = mm([(memn, dkvmb)], M=D_MODEL, N=2 * D_MODEL, K=Mm, tm=D_MODEL, tn=1024, tk=Mm, a_t=True,
                        out_kind="s", out_dtype=BF16, name="b_dw_xkv")
    dmemn = mm([(dkvmb, W["w_xkv"])], M=Mm, N=D_MODEL, K=2 * D_MODEL, tm=Mm, tn=D_MODEL, tk=1024, b_kind="t",
               name="b_dmemn")
    _, _, grads["g_mem"] = _rms_bwd(mem, W["g_mem"], dmemn, None, name="b_norm_mem")
    comm.reduce_finish("ffn2", after=dh2b)

    dyn = mm([(dh2b, W["w_out"])], M=T, N=D_MODEL, K=D_MODEL, tm=1024, tn=D_MODEL, tk=D_MODEL, b_kind="t",
             out_dtype=BF16, name="b_dyn")
    mid["w_out"] = mm([(yn, dh2b)], M=D_MODEL, N=D_MODEL, K=T, tm=1024, tn=D_MODEL // 2, tk=4096, a_t=True,
                        out_dtype=BF16, name="b_dw_out")
    dz, grads["g_v"], grads["w_s"], dbst, grads["sinks"], grads["g_a_out"], grads["g_b_out"] = _mixer_bwd(
        z, dyn, probs, W["g_v"], W["w_s"], bst, W["g_a_out"], W["g_b_out"], name="b_mixer")
    grads["b_s"] = jnp.transpose(dbst)
    mid["w_in"] = mm([(n2, dz)], M=D_MODEL, N=IN_COLS, K=T, tm=1024, tn=IN_COLS, tk=1024, a_t=True,
                     out_dtype=BF16, name="b_dw_in")
    tok = comm.reduce_pair_start("mid", mid)
    dn2 = mm([(dz, W["w_in"])], M=T, N=D_MODEL, K=IN_COLS, tm=512, tn=D_MODEL, tk=IN_COLS, b_kind="t",
             out_dtype=BF16, dep=tok, name="b_dn2")
    tok = comm.reduce_pair_done("mid", after=dn2)
    dh1, dh1b, grads["g_mix"] = _rms_bwd(h1, W["g_mix"], dn2, dh2, dep=tok, name="b_norm2")

    dG1, dU1, dwg, dwu, dwd = _swiglu_bwd_weights("b_ffn1", n1, G1, U1, A1, W["w1_down"], dh1b)
    comm.reduce_finish("mid", after=dwu)
    tok = comm.reduce_start("ffn1", {"w1_gate": dwg, "w1_up": dwu, "w1_down": dwd})
    dx, _, grads["g_ffn1"] = _swiglu_bwd_input("b_ffn1", x, W["g_ffn1"], dG1, dU1, W["w1_gate"], W["w1_up"], dh1, tok)
    comm.reduce_finish("ffn1", after=dx)
    return loss, dx, grads


BIG = ["w1_gate", "w1_up", "w1_down", "w_in", "w_out", "w_xq", "w_xkv", "w_xo", "w2_gate", "w2_up", "w2_down"]
SMALL = ["g_ffn1", "g_mix", "g_v", "w_s", "b_s", "sinks", "g_a_out", "g_b_out", "g_x", "g_mem", "g_ffn2", "g_final"]
ALL_W = ["g_ffn1", "w1_gate", "w1_up", "w1_down", "g_mix", "w_in", "g_v", "w_s", "b_s", "sinks", "g_a_out",
         "g_b_out", "w_out", "g_x", "g_mem", "w_xq", "w_xkv", "w_xo", "g_ffn2", "w2_gate", "w2_up", "w2_down",
         "g_final"]
ANY = pl.BlockSpec(memory_space=pl.ANY)


def _place():
    x, y, c = lax.axis_index("x"), lax.axis_index("y"), lax.axis_index("c")
    chips = [(1 - x, y), (x, 1 - y), (1 - x, 1 - y)]
    return x, y, c, chips


COL_SHARDED = ("w1_gate", "w1_up", "w2_gate", "w2_up", "w_xkv")


def _gathered_shape(shape, by_cols):
    rows, cols = shape
    return (rows, N_CHIPS * cols) if by_cols else (N_CHIPS, rows, cols)


def _owner_rows(ref, shape, by_cols, slot, r0, rows):
    cols = shape[1]
    if by_cols:
        return ref.at[pl.ds(r0, rows), pl.ds(pl.multiple_of(slot * cols, LANES), cols)]
    return ref.at[slot, pl.ds(r0, rows), :]


def _allgather_weights(shards, by_cols, *, name):
    n = len(shards)

    def body(*refs):
        ins, outs = refs[:n], refs[n:2 * n]
        send, recv, loc = refs[2 * n:]
        x, y, c, chips = _place()
        me = 2 * x + y
        sib = (x, y, 1 - c)

        def half(w, slot, hc):
            h = shards[w].shape[0] // 2
            return _owner_rows(outs[w], shards[w].shape, by_cols[w], slot, hc * h, h)

        def copy(w, k, slot, hc, to, src=None):
            return pltpu.make_async_remote_copy(
                src_ref=half(w, slot, hc) if src is None else src, dst_ref=half(w, slot, hc),
                send_sem=send.at[6 * w + k], recv_sem=recv.at[6 * w + k], device_id=to, device_id_type=MESH)

        own = [pltpu.make_async_remote_copy(
            src_ref=ins[w], dst_ref=_owner_rows(outs[w], shards[w].shape, by_cols[w], me, 0, shards[w].shape[0]),
            send_sem=loc.at[w], recv_sem=loc.at[n + w], device_id=sib, device_id_type=MESH) for w in range(n)]
        for cp in own:
            cp.start()
        first = []
        for w in range(n):
            h = shards[w].shape[0] // 2
            for j, (tx, ty) in enumerate(chips):
                first.append(copy(w, j, me, c, (tx, ty, c), src=ins[w].at[pl.ds(c * h, h), :]))
                first[-1].start()
        passed = []
        for w in range(n):
            for j, (tx, ty) in enumerate(chips):
                slot = 2 * tx + ty
                copy(w, j, slot, c, (tx, ty, c)).wait_recv()
                passed.append(copy(w, 3 + j, slot, c, sib))
                passed[-1].start()
        for w in range(n):
            for j, (tx, ty) in enumerate(chips):
                copy(w, 3 + j, 2 * tx + ty, 1 - c, sib).wait_recv()
        for cp in first + passed:
            cp.wait_send()
        for cp in own:
            cp.wait()

    return pl.pallas_call(
        body, name=name, in_specs=[ANY] * n, out_specs=[ANY] * n,
        out_shape=[jax.ShapeDtypeStruct(_gathered_shape(s.shape, bc), s.dtype) for s, bc in zip(shards, by_cols)],
        scratch_shapes=[pltpu.SemaphoreType.DMA((6 * n,)), pltpu.SemaphoreType.DMA((6 * n,)),
                        pltpu.SemaphoreType.DMA((2 * n,))],
    )(*shards)


def _pair_exchange(grads, *, name):
    n = len(grads)

    def body(*refs):
        ins, outs = refs[:n], refs[n:2 * n]
        send, recv = refs[2 * n:]
        x, y, c, _ = _place()
        cps = []
        for w in range(n):
            h = grads[w].shape[1] // 2
            cps.append(pltpu.make_async_remote_copy(
                src_ref=ins[w].at[:, pl.ds((1 - c) * h, h), :], dst_ref=outs[w],
                send_sem=send.at[w], recv_sem=recv.at[w], device_id=(x, y, 1 - c), device_id_type=MESH))
            cps[-1].start()
        for cp in cps:
            cp.wait()

    return pl.pallas_call(
        body, name=name, in_specs=[ANY] * n, out_specs=[ANY] * n,
        out_shape=[jax.ShapeDtypeStruct((N_CHIPS, g.shape[1] // 2, g.shape[2]), g.dtype) for g in grads],
        scratch_shapes=[pltpu.SemaphoreType.DMA((n,)), pltpu.SemaphoreType.DMA((n,))],
    )(*grads)


def _pair_sum(g, got, *, name):
    S, R, C = g.shape
    h = R // 2
    tr = _row_block(h, 3 * C * 2, 16)
    nr = h // tr

    def body(a_ref, b_ref, o_ref):
        o_ref[...] = (a_ref[...].astype(F32) + b_ref[...].astype(F32)).astype(BF16)

    return pl.pallas_call(
        body, name=name, grid=(S, nr),
        in_specs=[pl.BlockSpec((None, tr, C), lambda s, r: (s, lax.axis_index("c") * nr + r, 0)),
                  pl.BlockSpec((None, tr, C), lambda s, r: (s, r, 0))],
        out_specs=pl.BlockSpec((None, tr, C), lambda s, r: (s, r, 0)),
        out_shape=jax.ShapeDtypeStruct((S, h, C), BF16),
        compiler_params=_params(("parallel", "parallel")),
    )(g, got)


def _chip_sum(part, got, *, name):
    S, h, C = part.shape
    tr = _row_block(h, 4 * C * 2 + C * 4, 16)
    nr = h // tr

    def body(own_ref, g0_ref, g1_ref, g2_ref, o_ref):
        acc = own_ref[...].astype(F32) + g0_ref[...].astype(F32)
        o_ref[...] = (acc + g1_ref[...].astype(F32)) + g2_ref[...].astype(F32)

    def piece(j):
        return pl.BlockSpec((None, tr, C), lambda r: (j, r, 0))

    return pl.pallas_call(
        body, name=name, grid=(nr,),
        in_specs=[pl.BlockSpec((None, tr, C), lambda r: (2 * lax.axis_index("x") + lax.axis_index("y"), r, 0)),
                  piece(0), piece(1), piece(2)],
        out_specs=pl.BlockSpec((tr, C), lambda r: (lax.axis_index("c") * nr + r, 0)),
        out_shape=jax.ShapeDtypeStruct((2 * h, C), F32),
        compiler_params=_params(("parallel",)),
    )(part, got, got, got)


def _pair_gather(totals, *, name):
    n = len(totals)

    def body(*refs):
        ins, outs = refs[:n], refs[n:2 * n]
        send, recv = refs[2 * n:]
        x, y, c, _ = _place()
        cps = []
        for w in range(n):
            h = totals[w].shape[0] // 2
            cps.append(pltpu.make_async_remote_copy(
                src_ref=ins[w].at[pl.ds(c * h, h), :], dst_ref=outs[w].at[pl.ds(c * h, h), :],
                send_sem=send.at[w], recv_sem=recv.at[w], device_id=(x, y, 1 - c), device_id_type=MESH))
            cps[-1].start()
        for w in range(n):
            h = totals[w].shape[0] // 2
            theirs = outs[w].at[pl.ds((1 - c) * h, h), :]
            pltpu.make_async_remote_copy(
                src_ref=theirs, dst_ref=theirs, send_sem=send.at[w], recv_sem=recv.at[w],
                device_id=(x, y, 1 - c), device_id_type=MESH).wait_recv()
        for cp in cps:
            cp.wait_send()

    return pl.pallas_call(
        body, name=name, in_specs=[ANY] * n, out_specs=[ANY] * n,
        out_shape=[jax.ShapeDtypeStruct(t.shape, t.dtype) for t in totals],
        input_output_aliases={w: w for w in range(n)},
        scratch_shapes=[pltpu.SemaphoreType.DMA((n,)), pltpu.SemaphoreType.DMA((n,))],
    )(*totals)


def _allreduce_small(v, *, name):
    R, C = v.shape
    ND = 8

    def body(v_ref, o_ref, all_ref, send, recv, loc):
        x, y, c, chips = _place()
        me, sib = (x, y, c), (x, y, 1 - c)

        def rows(px, py, pc):
            return all_ref.at[pl.ds((4 * px + 2 * py + pc) * R, R), :]

        def copy(k, block, to, src=None):
            return pltpu.make_async_remote_copy(
                src_ref=rows(*block) if src is None else src, dst_ref=rows(*block),
                send_sem=send.at[k], recv_sem=recv.at[k], device_id=to, device_id_type=MESH)

        mine = pltpu.make_async_copy(v_ref, rows(*me), loc)
        mine.start()
        first = [copy(0, me, sib, src=v_ref)]
        first += [copy(1 + j, me, (*chip, c), src=v_ref) for j, chip in enumerate(chips)]
        for cp in first:
            cp.start()
        passed = [copy(4 + j, (*chip, c), sib) for j, chip in enumerate(chips)]
        for j, chip in enumerate(chips):
            copy(1 + j, (*chip, c), me).wait_recv()
            passed[j].start()
        copy(0, sib, me).wait_recv()
        for j, chip in enumerate(chips):
            copy(4 + j, (*chip, 1 - c), me).wait_recv()
        for cp in first + passed:
            cp.wait_send()
        mine.wait()
        acc = all_ref[0:R, :]
        for d in range(1, ND):
            acc = acc + all_ref[d * R:(d + 1) * R, :]
        o_ref[...] = acc

    vm = pl.BlockSpec(memory_space=pltpu.VMEM)
    return pl.pallas_call(
        body, name=name, in_specs=[vm], out_specs=[vm, vm],
        out_shape=[jax.ShapeDtypeStruct((R, C), F32), jax.ShapeDtypeStruct((ND * R, C), F32)],
        scratch_shapes=[pltpu.SemaphoreType.DMA((7,)), pltpu.SemaphoreType.DMA((7,)), pltpu.SemaphoreType.DMA],
        compiler_params=pltpu.CompilerParams(vmem_limit_bytes=VMEM_LIMIT),
    )(v)[0]


HBM = pl.BlockSpec(memory_space=pltpu.HBM)
SEM = pl.BlockSpec(memory_space=pltpu.SEMAPHORE)
EFFECT = pltpu.SideEffectType.DATAFLOW_SIDE_EFFECTING


def _remote(src, dst, send, recv, k, to):
    return pltpu.make_async_remote_copy(src_ref=src, dst_ref=dst, send_sem=send.at[k], recv_sem=recv.at[k],
                                        device_id=to, device_id_type=MESH)


def _split_start(bufs, plan, ncopies, *, name, after=None):
    nb = len(bufs)
    extra = [] if after is None else [after]

    def body(*refs):
        pos = nb + len(extra)
        send, recv, token = refs[pos], refs[pos + 1], refs[-1]
        for k, (src, dst, to) in enumerate(plan(refs[:nb])):
            _remote(src, dst, send, recv, k, to).start()
        token[...] = jnp.zeros_like(token)

    outs = pl.pallas_call(
        body, name=name,
        out_shape=(pltpu.SemaphoreType.DMA((ncopies,)), pltpu.SemaphoreType.DMA((ncopies,)),
                   *[pltpu.HBM(b.shape, b.dtype) for b in bufs], jax.ShapeDtypeStruct((SUBLANES, LANES), F32)),
        in_specs=[HBM] * nb + [ANY] * len(extra),
        out_specs=(SEM, SEM, *[HBM] * nb, pl.BlockSpec(memory_space=pltpu.VMEM)),
        input_output_aliases={i: 2 + i for i in range(nb)},
        compiler_params=pltpu.CompilerParams(has_side_effects=EFFECT),
    )(*[pltpu.with_memory_space_constraint(b, pltpu.HBM) for b in bufs], *extra)
    return outs[0], outs[1], list(outs[2:2 + nb]), outs[-1]


def _split_wait(started, plan, after, *, name):
    send, recv, bufs, _ = started
    nb = len(bufs)

    def body(*refs):
        send_sem, recv_sem = refs[nb], refs[nb + 1]
        for k, (src, dst, to) in enumerate(plan(refs[:nb])):
            cp = _remote(src, dst, send_sem, recv_sem, k, to)
            cp.wait_send()
            cp.wait_recv()

    outs = pl.pallas_call(
        body, name=name,
        out_shape=tuple(pltpu.HBM(b.shape, b.dtype) for b in bufs),
        in_specs=[HBM] * nb + [SEM, SEM, ANY], out_specs=tuple([HBM] * nb),
        input_output_aliases={i: i for i in range(nb)},
        compiler_params=pltpu.CompilerParams(has_side_effects=EFFECT),
    )(*bufs, send, recv, after)
    return list(outs)


def _gather_chip_plan(shapes, by_cols):
    n = len(shapes)

    def plan(refs):
        srcs, lands = refs[:n], refs[n:]
        x, y, c, chips = _place()
        out = []
        for w in range(n):
            h = shapes[w][0] // 2
            mine = _owner_rows(lands[w], shapes[w], by_cols[w], 2 * x + y, c * h, h)
            for tx, ty in chips:
                out.append((srcs[w].at[pl.ds(c * h, h), :], mine, (tx, ty, c)))
        return out

    return plan


def _gather_pair_plan(shapes, by_cols):
    n = len(shapes)

    def plan(refs):
        srcs, lands = refs[:n], refs[n:]
        x, y, c, chips = _place()
        out = []
        for w in range(n):
            h = shapes[w][0] // 2
            for tx, ty in chips:
                half = _owner_rows(lands[w], shapes[w], by_cols[w], 2 * tx + ty, c * h, h)
                out.append((half, half, (x, y, 1 - c)))
            own = _owner_rows(lands[w], shapes[w], by_cols[w], 2 * x + y, 0, shapes[w][0])
            out.append((srcs[w], own, (x, y, 1 - c)))
        return out

    return plan


def _reduce_pair_plan(shapes):
    n = len(shapes)

    def plan(refs):
        local, lands = refs[:n], refs[n:]
        x, y, c, _ = _place()
        out = []
        for w in range(n):
            h = shapes[w][1] // 2
            out.append((local[w].at[:, pl.ds((1 - c) * h, h), :], lands[w], (x, y, 1 - c)))
        return out

    return plan


def _reduce_chip_plan(n):
    def plan(refs):
        parts, lands = refs[:n], refs[n:]
        x, y, c, chips = _place()
        return [(parts[w].at[2 * tx + ty], lands[w].at[j], (tx, ty, c))
                for w in range(n) for j, (tx, ty) in enumerate(chips)]

    return plan


def _as_operands(gathered):
    out = {}
    for n, g in gathered.items():
        if n in COL_SHARDED:
            out[n] = g
        elif n == "w_in":
            out[n] = jnp.transpose(g, (1, 0, 2)).reshape(D_MODEL, IN_COLS)
        else:
            out[n] = g.reshape(g.shape[0] * g.shape[1], g.shape[2])
    return out


def _by_owner(n, g):
    if n == "w_in":
        return jnp.transpose(g.reshape(D_MODEL, N_CHIPS, IN_COLS // N_CHIPS), (1, 0, 2))
    if g.ndim == 2:
        return g.reshape(N_CHIPS, g.shape[0] // N_CHIPS, g.shape[1])
    return g


class _Comm:
    def __init__(self, shards):
        self.shards = shards
        self.total = {}
        self._flight = {}

    def _layout(self, names):
        return [self.shards[n].shape for n in names], [n in COL_SHARDED for n in names]

    def gather_now(self, tag, names):
        _, by_cols = self._layout(names)
        got = _allgather_weights([self.shards[n] for n in names], by_cols, name=f"gather_{tag}")
        return _as_operands(dict(zip(names, got)))

    def gather_start(self, tag, names, after):
        shapes, by_cols = self._layout(names)
        srcs = [self.shards[n] for n in names]
        lands = [lax.empty(_gathered_shape(s.shape, bc), s.dtype) for s, bc in zip(srcs, by_cols)]
        started = _split_start(srcs + lands, _gather_chip_plan(shapes, by_cols), 3 * len(srcs),
                               after=after, name=f"gather_{tag}_chips_start")
        self._flight[tag] = (names, started)
        return started[3]

    def gather_landed(self, tag, after):
        names, started = self._flight[tag]
        shapes, by_cols = self._layout(names)
        bufs = _split_wait(started, _gather_chip_plan(shapes, by_cols), after, name=f"gather_{tag}_chips_wait")
        started = _split_start(bufs, _gather_pair_plan(shapes, by_cols), 4 * len(names),
                               name=f"gather_{tag}_pair_start")
        self._flight[tag] = (names, started)
        return started[3]

    def gather_ready(self, tag, after):
        names, started = self._flight.pop(tag)
        shapes, by_cols = self._layout(names)
        bufs = _split_wait(started, _gather_pair_plan(shapes, by_cols), after, name=f"gather_{tag}_pair_wait")
        return _as_operands(dict(zip(names, bufs[len(names):])))

    def reduce_start(self, tag, grads):
        names = list(grads)
        local = [_by_owner(n, grads[n]) for n in names]
        return self._chip_start(tag, names, local, _pair_exchange(local, name=f"pair_exchange_{tag}"))

    def reduce_pair_start(self, tag, grads):
        names = list(grads)
        local = [_by_owner(n, grads[n]) for n in names]
        lands = [lax.empty((N_CHIPS, g.shape[1] // 2, g.shape[2]), g.dtype) for g in local]
        started = _split_start(local + lands, _reduce_pair_plan([g.shape for g in local]), len(names),
                               name=f"pair_exchange_{tag}_start")
        self._flight[tag] = (names, started)
        return started[3]

    def reduce_pair_done(self, tag, after):
        names, started = self._flight.pop(tag)
        n = len(names)
        bufs = _split_wait(started, _reduce_pair_plan([b.shape for b in started[2][:n]]), after,
                           name=f"pair_exchange_{tag}_wait")
        return self._chip_start(tag, names, bufs[:n], bufs[n:])

    def _chip_start(self, tag, names, local, from_sib):
        parts = [_pair_sum(g, s, name=f"pair_sum_{n}") for n, g, s in zip(names, local, from_sib)]
        lands = [lax.empty((N_CHIPS - 1,) + p.shape[1:], p.dtype) for p in parts]
        self._flight[tag] = (names, _split_start(parts + lands, _reduce_chip_plan(len(names)), 3 * len(names),
                                                 name=f"chip_exchange_{tag}_start"))
        return self._flight[tag][1][3]

    def reduce_finish(self, tag, after):
        names, started = self._flight.pop(tag)
        n = len(names)
        bufs = _split_wait(started, _reduce_chip_plan(n), after, name=f"chip_exchange_{tag}_wait")
        totals = [_chip_sum(p, s, name=f"chip_sum_{nm}") for nm, p, s in zip(names, bufs[:n], bufs[n:])]
        self.total.update(zip(names, _pair_gather(totals, name=f"pair_gather_{tag}")))


def _adamw(w, g, m, v, *, name):
    R, C = w.shape
    tr = _row_block(R, 8 * C * 4, SUBLANES)

    def body(w_ref, g_ref, m_ref, v_ref, go_ref, d_ref, nm_ref, nv_ref):
        gg = g_ref[...]
        go_ref[...] = gg
        m_new = ADAM_B1 * m_ref[...] + (1.0 - ADAM_B1) * gg
        v_new = ADAM_B2 * v_ref[...] + (1.0 - ADAM_B2) * (gg * gg)
        m_hat = m_new / (1.0 - ADAM_B1 ** ADAM_STEP)
        v_hat = v_new / (1.0 - ADAM_B2 ** ADAM_STEP)
        d_ref[...] = -ADAM_LR * (m_hat / (jnp.sqrt(v_hat) + ADAM_EPS) + ADAM_WD * w_ref[...])
        nm_ref[...] = m_new
        nv_ref[...] = v_new

    blk = pl.BlockSpec((tr, C), lambda i: (i, 0))
    shp = jax.ShapeDtypeStruct((R, C), F32)
    return pl.pallas_call(
        body, name=name, grid=(R // tr,), in_specs=[blk] * 4, out_specs=[blk] * 4, out_shape=[shp] * 4,
        compiler_params=_params(("parallel",)),
    )(w, g, m, v)


def _to2d(a):
    flat = a.reshape(-1)
    pad = (-flat.shape[0]) % (SUBLANES * LANES)
    if pad:
        flat = jnp.pad(flat, (0, pad))
    return flat.reshape(-1, LANES)


def _small_rows(shape):
    return -(-math.prod(shape) // (SUBLANES * LANES)) * SUBLANES


def _pack_small(parts):
    rows = jnp.concatenate([_to2d(p) for p in parts], axis=0)
    pad = (-rows.shape[0]) % 256
    if pad:
        rows = jnp.concatenate([rows, jnp.zeros((pad, LANES), rows.dtype)], axis=0)
    return rows


def _unpack_small(rows, shapes):
    out, r = [], 0
    for shp in shapes:
        size = math.prod(shp)
        nrow = _small_rows(shp)
        out.append(rows[r:r + nrow].reshape(-1)[:size].reshape(shp))
        r += nrow
    return out


def kernel(x, mem, g_ffn1, w1_gate, w1_up, w1_down, g_mix, w_in, g_v, w_s, b_s, sinks, g_a_out, g_b_out, w_out, g_x, g_mem, w_xq, w_xkv, w_xo, g_ffn2, w2_gate, w2_up, w2_down, g_final, loss_target, m_g_ffn1, m_w1_gate, m_w1_up, m_w1_down, m_g_mix, m_w_in, m_g_v, m_w_s, m_b_s, m_sinks, m_g_a_out, m_g_b_out, m_w_out, m_g_x, m_g_mem, m_w_xq, m_w_xkv, m_w_xo, m_g_ffn2, m_w2_gate, m_w2_up, m_w2_down, m_g_final, v_g_ffn1, v_w1_gate, v_w1_up, v_w1_down, v_g_mix, v_w_in, v_g_v, v_w_s, v_b_s, v_sinks, v_g_a_out, v_g_b_out, v_w_out, v_g_x, v_g_mem, v_w_xq, v_w_xkv, v_w_xo, v_g_ffn2, v_w2_gate, v_w2_up, v_w2_down, v_g_final):
    args = dict(locals())
    Wp = {n: args[n] for n in ALL_W}
    Mp = {n: args["m_" + n] for n in ALL_W}
    Vp = {n: args["v_" + n] for n in ALL_W}

    comm = _Comm({n: Wp[n][0].astype(BF16) for n in BIG})
    W = {n: Wp[n] for n in SMALL}
    W["g_final"] = Wp["g_final"].reshape(1, D_MODEL)
    for n in ("w_s", "b_s"):
        W[n] = Wp[n][0]
    loss, dx, grads = _local_step(x[0], mem[0], loss_target[0], W, comm)
    big_grad = comm.total

    small_shapes = [Wp[n].shape for n in SMALL]
    packed = _pack_small([grads[n].reshape(Wp[n].shape) for n in SMALL] + [loss])
    summed = _allreduce_small(packed, name="allreduce_small")
    small_grad = dict(zip(SMALL, _unpack_small(summed, small_shapes)))
    nrows = sum(_small_rows(s) for s in small_shapes)
    loss_total = summed[nrows, 0]

    grad_out, delta, new_m, new_v = {}, {}, {}, {}
    for n in BIG:
        shp = Wp[n].shape
        g, d, nm, nv = _adamw(Wp[n][0], big_grad[n], Mp[n][0], Vp[n][0], name=f"adamw_{n}")
        grad_out[n], delta[n], new_m[n], new_v[n] = g.reshape(shp), d.reshape(shp), nm.reshape(shp), nv.reshape(shp)
    sw = _pack_small([Wp[n] for n in SMALL])
    sg = _pack_small([small_grad[n] for n in SMALL])
    sm = _pack_small([Mp[n] for n in SMALL])
    sv = _pack_small([Vp[n] for n in SMALL])
    _, d, nm, nv = _adamw(sw, sg, sm, sv, name="adamw_small")
    for n, dd, mm_, vv_ in zip(SMALL, _unpack_small(d, small_shapes), _unpack_small(nm, small_shapes),
                               _unpack_small(nv, small_shapes)):
        grad_out[n], delta[n], new_m[n], new_v[n] = small_grad[n], dd, mm_, vv_

    return (loss_total, dx[None], *[grad_out[n] for n in ALL_W], *[delta[n] for n in ALL_W],
            *[new_m[n] for n in ALL_W], *[new_v[n] for n in ALL_W])
```

```python
import functools
import math

import jax
import jax.numpy as jnp
from jax import lax
from jax.experimental import pallas as pl
from jax.experimental.pallas import tpu as pltpu

F32 = jnp.float32
BF16 = jnp.bfloat16
MESH = pl.DeviceIdType.MESH

D_MODEL = 2048
D_FF = 5632
D_A = 1024
D_B = 1024
CHUNK = 128
A_GROUPS = 8
HEAD_DIM = 64
B_Q_HEADS = 16
B_KV_HEADS = 2
X_HEADS = 4
X_HEAD_DIM = 512
IN_COLS = 3328
O_Q = 2 * D_A
O_K = O_Q + D_B
O_V = O_K + B_KV_HEADS * HEAD_DIM
N_CHIPS = 4
EPS = 1e-5
NEG = -1e30
ADAM_LR = 0.001
ADAM_B1 = 0.9
ADAM_B2 = 0.999
ADAM_EPS = 1e-08
ADAM_WD = 0.01
ADAM_STEP = 10

V7X_VMEM_BYTES = 64 * 1024 * 1024
VMEM_LIMIT = 56 * 1024 * 1024
LANES = 128
SUBLANES = 8


ANY = pl.BlockSpec(memory_space=pl.ANY)


def _params(sem, vmem=VMEM_LIMIT):
    return pltpu.CompilerParams(dimension_semantics=sem, vmem_limit_bytes=vmem)


def _matmul(pairs, *, M, N, K, tm, tn, tk, a_t=False, b_kind="n", out_kind="n", out_dtype=F32,
            scale=1.0, res=None, norm_g=None, order="ij", dep=None, name):
    tm, tn, tk = min(tm, M), min(tn, N), min(tk, K)
    assert M % tm == 0 and N % tn == 0 and K % tk == 0, (name, M, N, K, tm, tn, tk)
    nk = K // tk
    npairs = len(pairs)
    b_t = b_kind == "t"
    ns = N // N_CHIPS

    def ij(g0, g1):
        return (g0, g1) if order == "ij" else (g1, g0)

    def a_map(g0, g1, k):
        i, _ = ij(g0, g1)
        return (k, i) if a_t else (i, k)

    a_spec = pl.BlockSpec((tk, tm) if a_t else (tm, tk), a_map)

    if b_kind == "n":
        b_spec = pl.BlockSpec((tk, tn), lambda g0, g1, k: (k, ij(g0, g1)[1]))
    else:
        b_spec = pl.BlockSpec((tn, tk), lambda g0, g1, k: (ij(g0, g1)[1], k))

    if out_kind == "n":
        o_spec = pl.BlockSpec((tm, tn), lambda g0, g1, k: ij(g0, g1))
        o_shape = jax.ShapeDtypeStruct((M, N), out_dtype)
    else:
        assert tn % ns == 0
        o_spec = pl.BlockSpec((tn // ns, tm, ns), lambda g0, g1, k: (ij(g0, g1)[1], ij(g0, g1)[0], 0))
        o_shape = jax.ShapeDtypeStruct((N_CHIPS, M, ns), out_dtype)

    in_specs, args = [], []
    for a, b in pairs:
        in_specs += [a_spec, b_spec]
        args += [a, b]
    if res is not None:
        in_specs.append(pl.BlockSpec((tm, tn), lambda g0, g1, k: ij(g0, g1)))
        args.append(res)
    if norm_g is not None:
        assert tn == N and out_kind == "n"
        in_specs.append(pl.BlockSpec((1, N), lambda g0, g1, k: (0, 0)))
        args.append(norm_g)
    if dep is not None:
        in_specs.append(ANY)
        args.append(dep)

    dn = (((0,) if a_t else (1,), (1,) if b_t else (0,)), ((), ()))

    def body(*refs):
        pos = 2 * npairs
        res_ref = refs[pos] if res is not None else None
        pos += res is not None
        g_ref = refs[pos] if norm_g is not None else None
        pos += (norm_g is not None) + (dep is not None)
        o_ref = refs[pos]
        n_ref = refs[pos + 1] if norm_g is not None else None
        acc_ref = refs[-1] if nk > 1 else None
        part = None
        for p in range(npairs):
            d = lax.dot_general(refs[2 * p][...], refs[2 * p + 1][...], dn, preferred_element_type=F32)
            part = d if part is None else part + d

        def finish(acc):
            r = acc * scale if scale != 1.0 else acc
            if res_ref is not None:
                r = res_ref[...] + r
            if out_kind == "n":
                o_ref[...] = r.astype(out_dtype)
            else:
                for s in range(tn // ns):
                    o_ref[s] = r[:, s * ns:(s + 1) * ns].astype(out_dtype)
            if n_ref is not None:
                n_ref[...] = (r * _rstd(r) * g_ref[...]).astype(BF16)

        if nk == 1:
            finish(part)
        else:
            k = pl.program_id(2)

            @pl.when(k == 0)
            def _():
                acc_ref[...] = part

            @pl.when((k > 0) & (k < nk - 1))
            def _():
                acc_ref[...] += part

            @pl.when(k == nk - 1)
            def _():
                finish(acc_ref[...] + part)

    grid = (M // tm, N // tn, nk) if order == "ij" else (N // tn, M // tm, nk)
    out_specs, out_shape = o_spec, o_shape
    if norm_g is not None:
        out_specs = [o_spec, pl.BlockSpec((tm, tn), lambda g0, g1, k: ij(g0, g1))]
        out_shape = [o_shape, jax.ShapeDtypeStruct((M, N), BF16)]
    return pl.pallas_call(
        body, name=name, grid=grid, in_specs=in_specs, out_specs=out_specs, out_shape=out_shape,
        scratch_shapes=[pltpu.VMEM((tm, tn), F32)] if nk > 1 else [],
        compiler_params=_params(("parallel", "parallel", "arbitrary")),
    )(*args)


def _rstd(x):
    return lax.rsqrt(jnp.mean(x * x, axis=-1, keepdims=True) + EPS)


def _rms_bwd_math(x, g, dy):
    r = _rstd(x)
    gy = dy * g
    xr = x * r
    dx = r * (gy - xr * jnp.mean(gy * xr, axis=-1, keepdims=True))
    return dx, dy * xr


def _rms_fwd(h, g, *, name, tm=512, dep=None):
    T, Dm = h.shape
    tm = min(tm, T)

    def body(h_ref, g_ref, *rest):
        x = h_ref[...]
        rest[-1][...] = (x * _rstd(x) * g_ref[...]).astype(BF16)

    return pl.pallas_call(
        body, name=name, grid=(T // tm,),
        in_specs=[pl.BlockSpec((tm, Dm), lambda i: (i, 0)), pl.BlockSpec((1, Dm), lambda i: (0, 0))]
        + ([ANY] if dep is not None else []),
        out_specs=pl.BlockSpec((tm, Dm), lambda i: (i, 0)),
        out_shape=jax.ShapeDtypeStruct((T, Dm), BF16),
        compiler_params=_params(("parallel",)),
    )(h, g, *([dep] if dep is not None else []))


def _rms_bwd(h, g, dn, dres, *, name, tm=256, dep=None):
    T, Dm = h.shape
    tm = min(tm, T)
    has_res = dres is not None

    def body(*refs):
        h_ref, g_ref, dn_ref = refs[:3]
        pos = 3
        dres_ref = refs[pos] if has_res else None
        pos += has_res + (dep is not None)
        dh_ref, dhb_ref, dg_ref = refs[pos:pos + 3]
        dx, dgr = _rms_bwd_math(h_ref[...], g_ref[...], dn_ref[...].astype(F32))
        if has_res:
            dx = dres_ref[...] + dx
        dh_ref[...] = dx
        dhb_ref[...] = dx.astype(BF16)
        part = jnp.sum(dgr, axis=0, keepdims=True)

        @pl.when(pl.program_id(0) == 0)
        def _():
            dg_ref[...] = part

        @pl.when(pl.program_id(0) > 0)
        def _():
            dg_ref[...] += part

    row = pl.BlockSpec((tm, Dm), lambda i: (i, 0))
    vec = pl.BlockSpec((1, Dm), lambda i: (0, 0))
    args = [h, g, dn] + ([dres] if has_res else []) + ([dep] if dep is not None else [])
    return pl.pallas_call(
        body, name=name, grid=(T // tm,),
        in_specs=[row, vec, row] + ([row] if has_res else []) + ([ANY] if dep is not None else []),
        out_specs=[row, row, vec],
        out_shape=[jax.ShapeDtypeStruct((T, Dm), F32), jax.ShapeDtypeStruct((T, Dm), BF16),
                   jax.ShapeDtypeStruct((1, Dm), F32)],
        compiler_params=_params(("arbitrary",)),
    )(*args)


def _loss_head(h, g, tgt, *, name, tm=256):
    T, Dm = h.shape
    tm = min(tm, T)

    def body(h_ref, g_ref, t_ref, dh_ref, dhb_ref, dg_ref, loss_ref):
        x = h_ref[...]
        gv = g_ref[...]
        r = _rstd(x)
        diff = x * r * gv - t_ref[...]
        lpart = 0.5 * jnp.sum(jnp.mean(diff * diff, axis=-1, keepdims=True), axis=0, keepdims=True)
        dx, dgr = _rms_bwd_math(x, gv, diff * (1.0 / Dm))
        dh_ref[...] = dx
        dhb_ref[...] = dx.astype(BF16)
        part = jnp.sum(dgr, axis=0, keepdims=True)
        lrow = jnp.broadcast_to(lpart, (1, LANES))

        @pl.when(pl.program_id(0) == 0)
        def _():
            dg_ref[...] = part
            loss_ref[...] = lrow

        @pl.when(pl.program_id(0) > 0)
        def _():
            dg_ref[...] += part
            loss_ref[...] += lrow

    row = pl.BlockSpec((tm, Dm), lambda i: (i, 0))
    vec = pl.BlockSpec((1, Dm), lambda i: (0, 0))
    return pl.pallas_call(
        body, name=name, grid=(T // tm,),
        in_specs=[row, vec, row],
        out_specs=[row, row, vec, pl.BlockSpec((1, LANES), lambda i: (0, 0))],
        out_shape=[jax.ShapeDtypeStruct((T, Dm), F32), jax.ShapeDtypeStruct((T, Dm), BF16),
                   jax.ShapeDtypeStruct((1, Dm), F32), jax.ShapeDtypeStruct((1, LANES), F32)],
        compiler_params=_params(("arbitrary",)),
    )(h, g, tgt)


MXU_COLS = 256
FF_TILE = 2 * MXU_COLS


def _row_block(rows, row_bytes, align, budget=24 * 1024 * 1024):
    fits = [d for d in range(align, rows + 1, align) if rows % d == 0 and 2 * d * row_bytes <= budget]
    assert fits, (rows, row_bytes)
    return fits[-1]


def _swiglu_up(n, wg, wu, *, name, tm=1024, tn=FF_TILE):
    T, Dm = n.shape
    Fd = wg.shape[1]
    tm = min(tm, T)

    def body(n_ref, wg_ref, wu_ref, g_ref, u_ref, a_ref):
        x = n_ref[...]
        g = jnp.dot(x, wg_ref[...], preferred_element_type=F32)
        u = jnp.dot(x, wu_ref[...], preferred_element_type=F32)
        g_ref[...] = g.astype(BF16)
        u_ref[...] = u.astype(BF16)
        a_ref[...] = (g * jax.nn.sigmoid(g) * u).astype(BF16)

    wspec = pl.BlockSpec((Dm, tn), lambda j, i: (0, j))
    ospec = pl.BlockSpec((tm, tn), lambda j, i: (i, j))
    oshape = jax.ShapeDtypeStruct((T, Fd), BF16)
    return pl.pallas_call(
        body, name=name, grid=(Fd // tn, T // tm),
        in_specs=[pl.BlockSpec((tm, Dm), lambda j, i: (i, 0)), wspec, wspec],
        out_specs=[ospec, ospec, ospec], out_shape=[oshape, oshape, oshape],
        compiler_params=_params(("parallel", "parallel")),
    )(n, wg, wu)


def _swiglu_bwd_act(dhb, wd, G, U, *, name, tm=1024, tn=D_FF // N_CHIPS):
    T, Dm = dhb.shape
    Fd = wd.shape[0]
    tm, tn = min(tm, T), min(tn, Fd)

    def body(dh_ref, wd_ref, g_ref, u_ref, dg_ref, du_ref):
        da = 0.5 * lax.dot_general(dh_ref[...], wd_ref[...], (((1,), (1,)), ((), ())), preferred_element_type=F32)
        g = g_ref[...].astype(F32)
        u = u_ref[...].astype(F32)
        sg = jax.nn.sigmoid(g)
        dg_ref[...] = (da * u * (sg * (1.0 + g * (1.0 - sg)))).astype(BF16)
        du_ref[...] = (da * (g * sg)).astype(BF16)

    blk = pl.BlockSpec((tm, tn), lambda j, i: (i, j))
    oshape = jax.ShapeDtypeStruct((T, Fd), BF16)
    return pl.pallas_call(
        body, name=name, grid=(Fd // tn, T // tm),
        in_specs=[pl.BlockSpec((tm, Dm), lambda j, i: (i, 0)), pl.BlockSpec((tn, Dm), lambda j, i: (j, 0)), blk, blk],
        out_specs=[blk, blk], out_shape=[oshape, oshape],
        compiler_params=_params(("parallel", "parallel")),
    )(dhb, wd, G, U)


_INV_SQRT2 = 0.7071067811865476
_INV_SQRT2PI = 0.3989422804014327


def _erf(x):
    ax = jnp.abs(x)
    t = 1.0 / (1.0 + 0.3275911 * ax)
    poly = t * (0.254829592 + t * (-0.284496736 + t * (1.421413741 + t * (-1.453152027 + t * 1.061405429))))
    y = 1.0 - poly * jnp.exp(-ax * ax)
    return jnp.where(x < 0, -y, y)


def _gelu_cdf(x):
    return 0.5 * (1.0 + _erf(x * _INV_SQRT2))


def _lane_lt64(shape):
    return lax.broadcasted_iota(jnp.int32, shape, len(shape) - 1) < HEAD_DIM


def _dup_half(x, kv):
    rolled = pltpu.roll(x, HEAD_DIM, 1)
    lo = _lane_lt64(x.shape)
    return jnp.where(lo, x, rolled) if kv == 0 else jnp.where(lo, rolled, x)


HEADS_PER_KV = B_Q_HEADS // B_KV_HEADS
PAIRS = HEADS_PER_KV // 2


def _attn_bias():
    shape = (2 * CHUNK, HEADS_PER_KV * CHUNK)
    qpos = (lax.broadcasted_iota(jnp.int32, shape, 1) & (CHUNK - 1)) + CHUNK
    kpos = lax.broadcasted_iota(jnp.int32, shape, 0)
    diff = qpos - kpos
    band = (diff >= 0) & (diff < CHUNK)
    return jnp.stack([jnp.where(band & (kpos >= CHUNK), 0.0, NEG), jnp.where(band, 0.0, NEG)]).astype(F32)


def _stack_heads(tiles, lo):
    parts = []
    for t in tiles:
        parts += [jnp.where(lo, t, 0.0), jnp.where(lo, 0.0, t)]
    return jnp.concatenate(parts, axis=0)


def _unstack_heads(s, lo):
    return [jnp.where(lo, s[2 * p * CHUNK:(2 * p + 1) * CHUNK], s[(2 * p + 1) * CHUNK:(2 * p + 2) * CHUNK])
            for p in range(PAIRS)]


def _stack_sinks(sk_ref, kv):
    return jnp.concatenate([jnp.broadcast_to(sk_ref[:, h:h + 1], (1, CHUNK))
                            for h in range(kv * HEADS_PER_KV, (kv + 1) * HEADS_PER_KV)], axis=1)


def _sgu_forward(z_ref, gv, wsm, bst):
    zu = z_ref[:, 0:D_A]
    zv = z_ref[:, D_A:2 * D_A]
    cu = _gelu_cdf(zu)
    cv = _gelu_cdf(zv)
    u = zu * cu
    v = zv * cv
    rv = _rstd(v)
    vn = (v * rv * gv).astype(BF16)
    svs = []
    for g in range(A_GROUPS):
        sl = slice(g * CHUNK, (g + 1) * CHUNK)
        svs.append(jnp.dot(wsm[g], vn[:, sl], preferred_element_type=F32) + bst[:, g:g + 1])
    sv = jnp.concatenate(svs, axis=1)
    return (zu, zv, cu, cv), u, v, rv, vn, sv


def _masked_ws(ws_ref):
    tril = lax.broadcasted_iota(jnp.int32, (CHUNK, CHUNK), 0) >= lax.broadcasted_iota(jnp.int32, (CHUNK, CHUNK), 1)
    return [jnp.where(tril, ws_ref[g], 0.0).astype(BF16) for g in range(A_GROUPS)], tril


def _attn_probs(qm, kkd, sink, bias):
    s = lax.dot_general(kkd, qm, (((1,), (1,)), ((), ())), preferred_element_type=F32) * (HEAD_DIM ** -0.5) + bias
    m = jnp.maximum(jnp.max(s, axis=0, keepdims=True), sink)
    e = jnp.exp(s - m)
    inv = 1.0 / (jnp.sum(e, axis=0, keepdims=True) + jnp.exp(sink - m))
    return e * inv


def _mixer_fwd(z, gv, ws, bst, sinks, ga, gb, *, name):
    T = z.shape[0]
    nb = T // CHUNK
    kvb = O_K // (2 * CHUNK)

    def body(z_ref, zp_ref, bias_ref, gv_ref, ws_ref, bst_ref, sk_ref, ga_ref, gb_ref, o_ref, p_ref):
        wsm, _ = _masked_ws(ws_ref)
        _, u, _, _, _, sv = _sgu_forward(z_ref, gv_ref[...], wsm, bst_ref[...])
        ya = u * sv
        o_ref[:, 0:D_A] = (ya * _rstd(ya) * ga_ref[...]).astype(BF16)

        mask = bias_ref[...]
        kk = jnp.concatenate([zp_ref[:, 0:CHUNK], z_ref[:, O_K:O_V]], axis=0)
        vv = jnp.concatenate([zp_ref[:, CHUNK:2 * CHUNK], z_ref[:, O_V:IN_COLS]], axis=0)
        lo = _lane_lt64((CHUNK, LANES))
        outs = []
        for kv in range(B_KV_HEADS):
            kkd = _dup_half(kk, kv).astype(BF16)
            vvd = _dup_half(vv, kv).astype(BF16)
            q = _stack_heads([z_ref[:, O_Q + (kv * PAIRS + pr) * LANES:O_Q + (kv * PAIRS + pr + 1) * LANES]
                              for pr in range(PAIRS)], lo).astype(BF16)
            p = _attn_probs(q, kkd, _stack_sinks(sk_ref, kv), mask)
            p_ref[kv] = p
            out = lax.dot_general(p.astype(BF16), vvd, (((0,), (0,)), ((), ())), preferred_element_type=F32)
            outs += _unstack_heads(out, lo)
        yb = jnp.concatenate(outs, axis=1)
        o_ref[:, D_A:D_A + D_B] = (yb * _rstd(yb) * gb_ref[...]).astype(BF16)

    full = lambda shape: pl.BlockSpec(shape, lambda i: (0,) * len(shape))
    pshape = (B_KV_HEADS, 2 * CHUNK, HEADS_PER_KV * CHUNK)
    return pl.pallas_call(
        body, name=name, grid=(nb,),
        in_specs=[pl.BlockSpec((CHUNK, IN_COLS), lambda i: (i, 0)),
                  pl.BlockSpec((CHUNK, 2 * CHUNK), lambda i: (jnp.maximum(i - 1, 0), kvb)),
                  pl.BlockSpec((None, 2 * CHUNK, HEADS_PER_KV * CHUNK), lambda i: (jnp.minimum(i, 1), 0, 0)),
                  full((1, D_A)), full((A_GROUPS, CHUNK, CHUNK)), full((CHUNK, A_GROUPS)), full((1, B_Q_HEADS)),
                  full((1, D_A)), full((1, D_B))],
        out_specs=[pl.BlockSpec((CHUNK, D_A + D_B), lambda i: (i, 0)),
                   pl.BlockSpec((None,) + pshape, lambda i: (i, 0, 0, 0))],
        out_shape=[jax.ShapeDtypeStruct((T, D_A + D_B), BF16), jax.ShapeDtypeStruct((nb,) + pshape, F32)],
        compiler_params=_params(("parallel",)),
    )(z, z, _attn_bias(), gv, ws, bst, sinks, ga, gb)


def _mixer_bwd(z, dyn, probs, gv, ws, bst, ga, gb, *, name):
    T = z.shape[0]
    nb = T // CHUNK
    kvb = O_K // (2 * CHUNK)
    NT = (((0,), (0,)), ((), ()))

    def body(z_ref, zp_ref, dy_ref, p_ref, gv_ref, ws_ref, bst_ref, ga_ref, gb_ref,
             dz_ref, dgv_ref, dws_ref, dbst_ref, dsk_ref, dga_ref, dgb_ref, carry_ref):
        step = pl.program_id(0)

        @pl.when(step == 0)
        def _():
            carry_ref[...] = jnp.zeros_like(carry_ref)
            dgv_ref[...] = jnp.zeros_like(dgv_ref)
            dws_ref[...] = jnp.zeros_like(dws_ref)
            dbst_ref[...] = jnp.zeros_like(dbst_ref)
            dsk_ref[...] = jnp.zeros_like(dsk_ref)
            dga_ref[...] = jnp.zeros_like(dga_ref)
            dgb_ref[...] = jnp.zeros_like(dgb_ref)

        wsm, tril = _masked_ws(ws_ref)
        gvv = gv_ref[...]
        (zu, zv, cu, cv), u, v, rv, vn, sv = _sgu_forward(z_ref, gvv, wsm, bst_ref[...])
        ya = u * sv
        dya, dga_rows = _rms_bwd_math(ya, ga_ref[...], dy_ref[:, 0:D_A].astype(F32))
        dga_ref[...] += jnp.sum(dga_rows, axis=0, keepdims=True)
        du = dya * sv
        dsv = dya * u
        dvn_parts = []
        for g in range(A_GROUPS):
            sl = slice(g * CHUNK, (g + 1) * CHUNK)
            dsv_g = dsv[:, sl]
            dsv_gb = dsv_g.astype(BF16)
            dw = lax.dot_general(dsv_gb, vn[:, sl], (((1,), (1,)), ((), ())), preferred_element_type=F32)
            dws_ref[g] += jnp.where(tril, dw, 0.0)
            dbst_ref[:, g:g + 1] += jnp.sum(dsv_g, axis=1, keepdims=True)
            dvn_parts.append(lax.dot_general(wsm[g], dsv_gb, NT, preferred_element_type=F32))
        dvn = jnp.concatenate(dvn_parts, axis=1)
        dv, dgv_rows = _rms_bwd_math(v, gvv, dvn)
        dgv_ref[...] += jnp.sum(dgv_rows, axis=0, keepdims=True)
        dz_ref[:, 0:D_A] = (du * (cu + zu * jnp.exp(-0.5 * zu * zu) * _INV_SQRT2PI)).astype(BF16)
        dz_ref[:, D_A:2 * D_A] = (dv * (cv + zv * jnp.exp(-0.5 * zv * zv) * _INV_SQRT2PI)).astype(BF16)

        kk = jnp.concatenate([zp_ref[:, 0:CHUNK], z_ref[:, O_K:O_V]], axis=0)
        vv = jnp.concatenate([zp_ref[:, CHUNK:2 * CHUNK], z_ref[:, O_V:IN_COLS]], axis=0)
        lo = _lane_lt64((CHUNK, LANES))
        kkd = [_dup_half(kk, kv).astype(BF16) for kv in range(B_KV_HEADS)]
        vvd = [_dup_half(vv, kv).astype(BF16) for kv in range(B_KV_HEADS)]
        outs = []
        for kv in range(B_KV_HEADS):
            outs += _unstack_heads(lax.dot_general(p_ref[kv].astype(BF16), vvd[kv], NT, preferred_element_type=F32), lo)
        yb = jnp.concatenate(outs, axis=1)
        dyb, dgb_rows = _rms_bwd_math(yb, gb_ref[...], dy_ref[:, D_A:D_A + D_B].astype(F32))
        dgb_ref[...] += jnp.sum(dgb_rows, axis=0, keepdims=True)

        dkk, dvv = [], []
        for kv in range(B_KV_HEADS):
            do = _stack_heads([dyb[:, (kv * PAIRS + pr) * LANES:(kv * PAIRS + pr + 1) * LANES]
                               for pr in range(PAIRS)], lo).astype(BF16)
            q = _stack_heads([z_ref[:, O_Q + (kv * PAIRS + pr) * LANES:O_Q + (kv * PAIRS + pr + 1) * LANES]
                              for pr in range(PAIRS)], lo).astype(BF16)
            p = p_ref[kv]
            dvv.append(jnp.dot(p.astype(BF16), do, preferred_element_type=F32))
            dp = lax.dot_general(vvd[kv], do, (((1,), (1,)), ((), ())), preferred_element_type=F32)
            delta = jnp.sum(p * dp, axis=0, keepdims=True)
            dsink = (jnp.sum(p, axis=0, keepdims=True) - 1.0) * delta
            for g in range(HEADS_PER_KV):
                h = kv * HEADS_PER_KV + g
                dsk_ref[:, h:h + 1] += jnp.sum(dsink[:, g * CHUNK:(g + 1) * CHUNK], axis=1, keepdims=True)
            ds = (p * (dp - delta) * (HEAD_DIM ** -0.5)).astype(BF16)
            dq = _unstack_heads(lax.dot_general(ds, kkd[kv], NT, preferred_element_type=F32), lo)
            for pr in range(PAIRS):
                c0 = O_Q + (kv * PAIRS + pr) * LANES
                dz_ref[:, c0:c0 + LANES] = dq[pr].astype(BF16)
            dkk.append(jnp.dot(ds, q, preferred_element_type=F32))

        def fold(parts):
            tot = [t + pltpu.roll(t, HEAD_DIM, 1) for t in parts]
            return jnp.where(_lane_lt64(tot[0].shape), tot[0], tot[1])

        dk_all = fold(dkk)
        dv_all = fold(dvv)
        dz_ref[:, O_K:O_V] = (dk_all[CHUNK:] + carry_ref[:, 0:CHUNK]).astype(BF16)
        dz_ref[:, O_V:IN_COLS] = (dv_all[CHUNK:] + carry_ref[:, CHUNK:2 * CHUNK]).astype(BF16)
        carry_ref[:, 0:CHUNK] = dk_all[:CHUNK]
        carry_ref[:, CHUNK:2 * CHUNK] = dv_all[:CHUNK]

    full = lambda shape: pl.BlockSpec(shape, lambda s: (0,) * len(shape))
    rev = lambda s: nb - 1 - s
    return pl.pallas_call(
        body, name=name, grid=(nb,),
        in_specs=[pl.BlockSpec((CHUNK, IN_COLS), lambda s: (rev(s), 0)),
                  pl.BlockSpec((CHUNK, 2 * CHUNK), lambda s: (jnp.maximum(rev(s) - 1, 0), kvb)),
                  pl.BlockSpec((CHUNK, D_A + D_B), lambda s: (rev(s), 0)),
                  pl.BlockSpec((None, B_KV_HEADS, 2 * CHUNK, HEADS_PER_KV * CHUNK), lambda s: (rev(s), 0, 0, 0)),
                  full((1, D_A)), full((A_GROUPS, CHUNK, CHUNK)), full((CHUNK, A_GROUPS)),
                  full((1, D_A)), full((1, D_B))],
        out_specs=[pl.BlockSpec((CHUNK, IN_COLS), lambda s: (rev(s), 0)),
                   full((1, D_A)), full((A_GROUPS, CHUNK, CHUNK)), full((CHUNK, A_GROUPS)), full((1, B_Q_HEADS)),
                   full((1, D_A)), full((1, D_B))],
        out_shape=[jax.ShapeDtypeStruct((T, IN_COLS), BF16), jax.ShapeDtypeStruct((1, D_A), F32),
                   jax.ShapeDtypeStruct((A_GROUPS, CHUNK, CHUNK), F32), jax.ShapeDtypeStruct((CHUNK, A_GROUPS), F32),
                   jax.ShapeDtypeStruct((1, B_Q_HEADS), F32), jax.ShapeDtypeStruct((1, D_A), F32),
                   jax.ShapeDtypeStruct((1, D_B), F32)],
        scratch_shapes=[pltpu.VMEM((CHUNK, 2 * CHUNK), F32)],
        compiler_params=_params(("arbitrary",)),
    )(z, z, dyn, probs, gv, ws, bst, ga, gb)


def _xattn_probs(qh, kh):
    s = lax.dot_general(kh, qh, (((1,), (1,)), ((), ())), preferred_element_type=F32) * (X_HEAD_DIM ** -0.5)
    e = jnp.exp(s - jnp.max(s, axis=0, keepdims=True))
    return e / jnp.sum(e, axis=0, keepdims=True)


def _xattn_fwd(q, kvm, *, name, tm=512):
    T = q.shape[0]
    Mm = kvm.shape[0]
    tm = min(tm, T)

    def body(q_ref, kv_ref, o_ref, p_ref):
        for h in range(X_HEADS):
            sl = slice(h * X_HEAD_DIM, (h + 1) * X_HEAD_DIM)
            kh = kv_ref[:, sl].astype(BF16)
            vh = kv_ref[:, D_MODEL + h * X_HEAD_DIM:D_MODEL + (h + 1) * X_HEAD_DIM].astype(BF16)
            p = _xattn_probs(q_ref[:, sl], kh)
            p_ref[h * Mm:(h + 1) * Mm, :] = p
            o_ref[:, sl] = lax.dot_general(p.astype(BF16), vh, (((0,), (0,)), ((), ())),
                                           preferred_element_type=F32).astype(BF16)

    return pl.pallas_call(
        body, name=name, grid=(T // tm,),
        in_specs=[pl.BlockSpec((tm, D_MODEL), lambda i: (i, 0)), pl.BlockSpec((Mm, 2 * D_MODEL), lambda i: (0, 0))],
        out_specs=[pl.BlockSpec((tm, D_MODEL), lambda i: (i, 0)), pl.BlockSpec((X_HEADS * Mm, tm), lambda i: (0, i))],
        out_shape=[jax.ShapeDtypeStruct((T, D_MODEL), BF16), jax.ShapeDtypeStruct((X_HEADS * Mm, T), F32)],
        compiler_params=_params(("parallel",)),
    )(q, kvm)


def _xattn_bwd(q, kvm, probs, do, *, name, tm=512):
    T = q.shape[0]
    Mm = kvm.shape[0]
    tm = min(tm, T)
    NT = (((0,), (0,)), ((), ()))

    def body(q_ref, kv_ref, p_ref, do_ref, dq_ref, dkv_ref):
        @pl.when(pl.program_id(0) == 0)
        def _():
            dkv_ref[...] = jnp.zeros_like(dkv_ref)

        for h in range(X_HEADS):
            sl = slice(h * X_HEAD_DIM, (h + 1) * X_HEAD_DIM)
            slv = slice(D_MODEL + h * X_HEAD_DIM, D_MODEL + (h + 1) * X_HEAD_DIM)
            kh = kv_ref[:, sl].astype(BF16)
            vh = kv_ref[:, slv].astype(BF16)
            qh = q_ref[:, sl]
            doh = do_ref[:, sl]
            p = p_ref[h * Mm:(h + 1) * Mm, :]
            dkv_ref[:, slv] += jnp.dot(p.astype(BF16), doh, preferred_element_type=F32)
            dp = lax.dot_general(vh, doh, (((1,), (1,)), ((), ())), preferred_element_type=F32)
            ds = (p * (dp - jnp.sum(p * dp, axis=0, keepdims=True)) * (X_HEAD_DIM ** -0.5)).astype(BF16)
            dq_ref[:, sl] = lax.dot_general(ds, kh, NT, preferred_element_type=F32).astype(BF16)
            dkv_ref[:, sl] += jnp.dot(ds, qh, preferred_element_type=F32)

    row = pl.BlockSpec((tm, D_MODEL), lambda i: (i, 0))
    kvs = pl.BlockSpec((Mm, 2 * D_MODEL), lambda i: (0, 0))
    return pl.pallas_call(
        body, name=name, grid=(T // tm,),
        in_specs=[row, kvs, pl.BlockSpec((X_HEADS * Mm, tm), lambda i: (0, i)), row], out_specs=[row, kvs],
        out_shape=[jax.ShapeDtypeStruct((T, D_MODEL), BF16), jax.ShapeDtypeStruct((Mm, 2 * D_MODEL), F32)],
        compiler_params=_params(("arbitrary",)),
    )(q, kvm, probs, do)


def _swiglu_bwd_weights(tag, n, G, U, A, wd, dhb):
    T = n.shape[0]
    dG, dU = _swiglu_bwd_act(dhb, wd, G, U, name=f"{tag}_bwd_act", tm=1024)
    dwd = _matmul([(A, dhb)], M=D_FF, N=D_MODEL, K=T, tm=1408, tn=1024, tk=2048, a_t=True, out_dtype=BF16,
                  scale=0.5, name=f"{tag}_dwd")
    dwg = _matmul([(n, dG)], M=D_MODEL, N=D_FF, K=T, tm=512, tn=D_FF // 2, tk=2048, a_t=True, out_kind="s",
                  out_dtype=BF16, order="ji", name=f"{tag}_dwg")
    dwu = _matmul([(n, dU)], M=D_MODEL, N=D_FF, K=T, tm=512, tn=D_FF // 2, tk=2048, a_t=True, out_kind="s",
                  out_dtype=BF16, order="ji", name=f"{tag}_dwu")
    return dG, dU, dwg, dwu, dwd


def _swiglu_bwd_input(tag, hin, g_norm, dG, dU, wg, wu, dh, dep):
    T = hin.shape[0]
    dn = _matmul([(dG, wg), (dU, wu)], M=T, N=D_MODEL, K=D_FF, tm=512, tn=D_MODEL // 2, tk=D_FF // 2, b_kind="t",
                 out_dtype=BF16, dep=dep, name=f"{tag}_dn")
    return _rms_bwd(hin, g_norm, dn, dh, name=f"{tag}_norm_bwd")


GROUP_FFN1 = ["w1_gate", "w1_up", "w1_down"]
GROUP_MID = ["w_in", "w_out", "w_xq", "w_xkv", "w_xo"]
GROUP_FFN2 = ["w2_gate", "w2_up", "w2_down"]


def _local_step(x, mem, tgt, W, comm):
    T = x.shape[0]
    Mm = mem.shape[0]
    mm = functools.partial(_matmul)

    W = {**W, **comm.gather_now("ffn1_up", ["w1_gate", "w1_up"])}
    tok = comm.gather_start("ffn1_down", ["w1_down"], after=W["w1_up"])
    tok = comm.gather_start("mid", GROUP_MID, after=tok)
    tok = comm.gather_start("ffn2", GROUP_FFN2, after=tok)
    n1 = _rms_fwd(x, W["g_ffn1"], dep=tok, name="f_norm1")
    G1, U1, A1 = _swiglu_up(n1, W["w1_gate"], W["w1_up"], name="f_ffn1_up")
    tok = comm.gather_landed("ffn1_down", after=A1)
    tok = comm.gather_landed("mid", after=tok)
    W = {**W, **comm.gather_ready("ffn1_down", after=tok)}
    h1 = mm([(A1, W["w1_down"])], M=T, N=D_MODEL, K=D_FF, tm=512, tn=D_MODEL // 2, tk=D_FF, scale=0.5, res=x,
            order="ji", name="f_ffn1_down")
    n2 = _rms_fwd(h1, W["g_mix"], name="f_norm2")
    W = {**W, **comm.gather_ready("mid", after=n2)}
    z = mm([(n2, W["w_in"])], M=T, N=IN_COLS, K=D_MODEL, tm=512, tn=IN_COLS // 2, tk=D_MODEL, name="f_w_in")
    bst = jnp.transpose(W["b_s"])
    yn, probs = _mixer_fwd(z, W["g_v"], W["w_s"], bst, W["sinks"], W["g_a_out"], W["g_b_out"], name="f_mixer")
    tok = comm.gather_landed("ffn2", after=yn)
    h2, n3 = mm([(yn, W["w_out"])], M=T, N=D_MODEL, K=D_MODEL, tm=512, tn=D_MODEL, tk=D_MODEL, res=h1,
                norm_g=W["g_x"], dep=tok, name="f_w_out")
    memn = _rms_fwd(mem, W["g_mem"], name="f_norm_mem")
    q3 = mm([(n3, W["w_xq"])], M=T, N=D_MODEL, K=D_MODEL, tm=1024, tn=D_MODEL, tk=D_MODEL, out_dtype=BF16,
            name="f_w_xq")
    kvm = mm([(memn, W["w_xkv"])], M=Mm, N=2 * D_MODEL, K=D_MODEL, tm=Mm, tn=1024, tk=D_MODEL, b_kind="n",
             name="f_w_xkv")
    o3, xprobs = _xattn_fwd(q3, kvm, name="f_xattn")
    h3, n4 = mm([(o3, W["w_xo"])], M=T, N=D_MODEL, K=D_MODEL, tm=512, tn=D_MODEL, tk=D_MODEL, res=h2,
                norm_g=W["g_ffn2"], name="f_w_xo")
    W = {**W, **comm.gather_ready("ffn2", after=n4)}
    G2, U2, A2 = _swiglu_up(n4, W["w2_gate"], W["w2_up"], name="f_ffn2_up")
    h4 = mm([(A2, W["w2_down"])], M=T, N=D_MODEL, K=D_FF, tm=512, tn=D_MODEL // 2, tk=D_FF, scale=0.5, res=h3,
            order="ji", name="f_ffn2_down")

    grads = {}
    dh4, dh4b, grads["g_final"], loss = _loss_head(h4, W["g_final"], tgt, name="loss_head")
    dG2, dU2, dwg, dwu, dwd = _swiglu_bwd_weights("b_ffn2", n4, G2, U2, A2, W["w2_down"], dh4b)
    tok = comm.reduce_pair_start("ffn2", {"w2_gate": dwg, "w2_up": dwu, "w2_down": dwd})
    dh3, dh3b, grads["g_ffn2"] = _swiglu_bwd_input("b_ffn2", h3, W["g_ffn2"], dG2, dU2, W["w2_gate"], W["w2_up"],
                                                   dh4, tok)
    tok = comm.reduce_pair_done("ffn2", after=dh3b)

    mid = {}
    do3 = mm([(dh3b, W["w_xo"])], M=T, N=D_MODEL, K=D_MODEL, tm=512, tn=D_MODEL, tk=D_MODEL, b_kind="t",
             out_dtype=BF16, dep=tok, name="b_do3")
    mid["w_xo"] = mm([(o3, dh3b)], M=D_MODEL, N=D_MODEL, K=T, tm=1024, tn=D_MODEL // 2, tk=4096, a_t=True,
                       out_dtype=BF16, name="b_dw_xo")
    dq3, dkvm = _xattn_bwd(q3, kvm, xprobs, do3, name="b_xattn")
    mid["w_xq"] = mm([(n3, dq3)], M=D_MODEL, N=D_MODEL, K=T, tm=1024, tn=D_MODEL // 2, tk=4096, a_t=True,
                       out_dtype=BF16, name="b_dw_xq")
    dn3 = mm([(dq3, W["w_xq"])], M=T, N=D_MODEL, K=D_MODEL, tm=512, tn=D_MODEL, tk=D_MODEL, b_kind="t",
             out_dtype=BF16, name="b_dn3")
    dh2, dh2b, grads["g_x"] = _rms_bwd(h2, W["g_x"], dn3, dh3, name="b_norm3")
    dkvmb = dkvm.astype(BF16)
    mid["w_xkv"] = mm([(memn, dkvmb)], M=D_MODEL, N=2 * D_MODEL, K=Mm, tm=D_MODEL, tn=1024, tk=Mm, a_t=True,
                        out_kind="s", out_dtype=BF16, name="b_dw_xkv")
    dmemn = mm([(dkvmb, W["w_xkv"])], M=Mm, N=D_MODEL, K=2 * D_MODEL, tm=Mm, tn=D_MODEL, tk=1024, b_kind="t",
               name="b_dmemn")
    _, _, grads["g_mem"] = _rms_bwd(mem, W["g_mem"], dmemn, None, name="b_norm_mem")
    comm.reduce_finish("ffn2", after=dh2b)

    dyn = mm([(dh2b, W["w_out"])], M=T, N=D_MODEL, K=D_MODEL, tm=1024, tn=D_MODEL, tk=D_MODEL, b_kind="t",
             out_dtype=BF16, name="b_dyn")
    mid["w_out"] = mm([(yn, dh2b)], M=D_MODEL, N=D_MODEL, K=T, tm=1024, tn=D_MODEL // 2, tk=4096, a_t=True,
                        out_dtype=BF16, name="b_dw_out")
    dz, grads["g_v"], grads["w_s"], dbst, grads["sinks"], grads["g_a_out"], grads["g_b_out"] = _mixer_bwd(
        z, dyn, probs, W["g_v"], W["w_s"], bst, W["g_a_out"], W["g_b_out"], name="b_mixer")
    grads["b_s"] = jnp.transpose(dbst)
    mid["w_in"] = mm([(n2, dz)], M=D_MODEL, N=IN_COLS, K=T, tm=1024, tn=IN_COLS, tk=1024, a_t=True,
                     out_dtype=BF16, name="b_dw_in")
    tok = comm.reduce_pair_start("mid", mid)
    dn2 = mm([(dz, W["w_in"])], M=T, N=D_MODEL, K=IN_COLS, tm=512, tn=D_MODEL, tk=IN_COLS, b_kind="t",
             out_dtype=BF16, dep=tok, name="b_dn2")
    tok = comm.reduce_pair_done("mid", after=dn2)
    dh1, dh1b, grads["g_mix"] = _rms_bwd(h1, W["g_mix"], dn2, dh2, dep=tok, name="b_norm2")

    dG1, dU1, dwg, dwu, dwd = _swiglu_bwd_weights("b_ffn1", n1, G1, U1, A1, W["w1_down"], dh1b)
    comm.reduce_finish("mid", after=dwu)
    tok = comm.reduce_start("ffn1", {"w1_gate": dwg, "w1_up": dwu, "w1_down": dwd})
    dx, _, grads["g_ffn1"] = _swiglu_bwd_input("b_ffn1", x, W["g_ffn1"], dG1, dU1, W["w1_gate"], W["w1_up"], dh1, tok)
    comm.reduce_finish("ffn1", after=dx)
    return loss, dx, grads


BIG = ["w1_gate", "w1_up", "w1_down", "w_in", "w_out", "w_xq", "w_xkv", "w_xo", "w2_gate", "w2_up", "w2_down"]
SMALL = ["g_ffn1", "g_mix", "g_v", "w_s", "b_s", "sinks", "g_a_out", "g_b_out", "g_x", "g_mem", "g_ffn2", "g_final"]
ALL_W = ["g_ffn1", "w1_gate", "w1_up", "w1_down", "g_mix", "w_in", "g_v", "w_s", "b_s", "sinks", "g_a_out",
         "g_b_out", "w_out", "g_x", "g_mem", "w_xq", "w_xkv", "w_xo", "g_ffn2", "w2_gate", "w2_up", "w2_down",
         "g_final"]
ANY = pl.BlockSpec(memory_space=pl.ANY)


def _place():
    x, y, c = lax.axis_index("x"), lax.axis_index("y"), lax.axis_index("c")
    chips = [(1 - x, y), (x, 1 - y), (1 - x, 1 - y)]
    return x, y, c, chips


COL_SHARDED = ("w1_gate", "w1_up", "w2_gate", "w2_up", "w_xkv")


def _gathered_shape(shape, by_cols):
    rows, cols = shape
    return (rows, N_CHIPS * cols) if by_cols else (N_CHIPS, rows, cols)


def _owner_rows(ref, shape, by_cols, slot, r0, rows):
    cols = shape[1]
    if by_cols:
        return ref.at[pl.ds(r0, rows), pl.ds(pl.multiple_of(slot * cols, LANES), cols)]
    return ref.at[slot, pl.ds(r0, rows), :]


def _allgather_weights(shards, by_cols, *, name):
    n = len(shards)

    def body(*refs):
        ins, outs = refs[:n], refs[n:2 * n]
        send, recv, loc = refs[2 * n:]
        x, y, c, chips = _place()
        me = 2 * x + y
        sib = (x, y, 1 - c)

        def half(w, slot, hc):
            h = shards[w].shape[0] // 2
            return _owner_rows(outs[w], shards[w].shape, by_cols[w], slot, hc * h, h)

        def copy(w, k, slot, hc, to, src=None):
            return pltpu.make_async_remote_copy(
                src_ref=half(w, slot, hc) if src is None else src, dst_ref=half(w, slot, hc),
                send_sem=send.at[6 * w + k], recv_sem=recv.at[6 * w + k], device_id=to, device_id_type=MESH)

        own = [pltpu.make_async_remote_copy(
            src_ref=ins[w], dst_ref=_owner_rows(outs[w], shards[w].shape, by_cols[w], me, 0, shards[w].shape[0]),
            send_sem=loc.at[w], recv_sem=loc.at[n + w], device_id=sib, device_id_type=MESH) for w in range(n)]
        for cp in own:
            cp.start()
        first = []
        for w in range(n):
            h = shards[w].shape[0] // 2
            for j, (tx, ty) in enumerate(chips):
                first.append(copy(w, j, me, c, (tx, ty, c), src=ins[w].at[pl.ds(c * h, h), :]))
                first[-1].start()
        passed = []
        for w in range(n):
            for j, (tx, ty) in enumerate(chips):
                slot = 2 * tx + ty
                copy(w, j, slot, c, (tx, ty, c)).wait_recv()
                passed.append(copy(w, 3 + j, slot, c, sib))
                passed[-1].start()
        for w in range(n):
            for j, (tx, ty) in enumerate(chips):
                copy(w, 3 + j, 2 * tx + ty, 1 - c, sib).wait_recv()
        for cp in first + passed:
            cp.wait_send()
        for cp in own:
            cp.wait()

    return pl.pallas_call(
        body, name=name, in_specs=[ANY] * n, out_specs=[ANY] * n,
        out_shape=[jax.ShapeDtypeStruct(_gathered_shape(s.shape, bc), s.dtype) for s, bc in zip(shards, by_cols)],
        scratch_shapes=[pltpu.SemaphoreType.DMA((6 * n,)), pltpu.SemaphoreType.DMA((6 * n,)),
                        pltpu.SemaphoreType.DMA((2 * n,))],
    )(*shards)


def _pair_exchange(grads, *, name):
    n = len(grads)

    def body(*refs):
        ins, outs = refs[:n], refs[n:2 * n]
        send, recv = refs[2 * n:]
        x, y, c, _ = _place()
        cps = []
        for w in range(n):
            h = grads[w].shape[1] // 2
            cps.append(pltpu.make_async_remote_copy(
                src_ref=ins[w].at[:, pl.ds((1 - c) * h, h), :], dst_ref=outs[w],
                send_sem=send.at[w], recv_sem=recv.at[w], device_id=(x, y, 1 - c), device_id_type=MESH))
            cps[-1].start()
        for cp in cps:
            cp.wait()

    return pl.pallas_call(
        body, name=name, in_specs=[ANY] * n, out_specs=[ANY] * n,
        out_shape=[jax.ShapeDtypeStruct((N_CHIPS, g.shape[1] // 2, g.shape[2]), g.dtype) for g in grads],
        scratch_shapes=[pltpu.SemaphoreType.DMA((n,)), pltpu.SemaphoreType.DMA((n,))],
    )(*grads)


def _pair_sum(g, got, *, name):
    S, R, C = g.shape
    h = R // 2
    tr = _row_block(h, 3 * C * 2, 16)
    nr = h // tr

    def body(a_ref, b_ref, o_ref):
        o_ref[...] = (a_ref[...].astype(F32) + b_ref[...].astype(F32)).astype(BF16)

    return pl.pallas_call(
        body, name=name, grid=(S, nr),
        in_specs=[pl.BlockSpec((None, tr, C), lambda s, r: (s, lax.axis_index("c") * nr + r, 0)),
                  pl.BlockSpec((None, tr, C), lambda s, r: (s, r, 0))],
        out_specs=pl.BlockSpec((None, tr, C), lambda s, r: (s, r, 0)),
        out_shape=jax.ShapeDtypeStruct((S, h, C), BF16),
        compiler_params=_params(("parallel", "parallel")),
    )(g, got)


def _chip_sum(part, got, *, name):
    S, h, C = part.shape
    tr = _row_block(h, 4 * C * 2 + C * 4, 16)
    nr = h // tr

    def body(own_ref, g0_ref, g1_ref, g2_ref, o_ref):
        acc = own_ref[...].astype(F32) + g0_ref[...].astype(F32)
        o_ref[...] = (acc + g1_ref[...].astype(F32)) + g2_ref[...].astype(F32)

    def piece(j):
        return pl.BlockSpec((None, tr, C), lambda r: (j, r, 0))

    return pl.pallas_call(
        body, name=name, grid=(nr,),
        in_specs=[pl.BlockSpec((None, tr, C), lambda r: (2 * lax.axis_index("x") + lax.axis_index("y"), r, 0)),
                  piece(0), piece(1), piece(2)],
        out_specs=pl.BlockSpec((tr, C), lambda r: (lax.axis_index("c") * nr + r, 0)),
        out_shape=jax.ShapeDtypeStruct((2 * h, C), F32),
        compiler_params=_params(("parallel",)),
    )(part, got, got, got)


def _pair_gather(totals, *, name):
    n = len(totals)

    def body(*refs):
        ins, outs = refs[:n], refs[n:2 * n]
        send, recv = refs[2 * n:]
        x, y, c, _ = _place()
        cps = []
        for w in range(n):
            h = totals[w].shape[0] // 2
            cps.append(pltpu.make_async_remote_copy(
                src_ref=ins[w].at[pl.ds(c * h, h), :], dst_ref=outs[w].at[pl.ds(c * h, h), :],
                send_sem=send.at[w], recv_sem=recv.at[w], device_id=(x, y, 1 - c), device_id_type=MESH))
            cps[-1].start()
        for w in range(n):
            h = totals[w].shape[0] // 2
            theirs = outs[w].at[pl.ds((1 - c) * h, h), :]
            pltpu.make_async_remote_copy(
                src_ref=theirs, dst_ref=theirs, send_sem=send.at[w], recv_sem=recv.at[w],
                device_id=(x, y, 1 - c), device_id_type=MESH).wait_recv()
        for cp in cps:
            cp.wait_send()

    return pl.pallas_call(
        body, name=name, in_specs=[ANY] * n, out_specs=[ANY] * n,
        out_shape=[jax.ShapeDtypeStruct(t.shape, t.dtype) for t in totals],
        input_output_aliases={w: w for w in range(n)},
        scratch_shapes=[pltpu.SemaphoreType.DMA((n,)), pltpu.SemaphoreType.DMA((n,))],
    )(*totals)


def _allreduce_small(v, *, name):
    R, C = v.shape
    ND = 8

    def body(v_ref, o_ref, all_ref, send, recv, loc):
        x, y, c, chips = _place()
        me, sib = (x, y, c), (x, y, 1 - c)

        def rows(px, py, pc):
            return all_ref.at[pl.ds((4 * px + 2 * py + pc) * R, R), :]

        def copy(k, block, to, src=None):
            return pltpu.make_async_remote_copy(
                src_ref=rows(*block) if src is None else src, dst_ref=rows(*block),
                send_sem=send.at[k], recv_sem=recv.at[k], device_id=to, device_id_type=MESH)

        mine = pltpu.make_async_copy(v_ref, rows(*me), loc)
        mine.start()
        first = [copy(0, me, sib, src=v_ref)]
        first += [copy(1 + j, me, (*chip, c), src=v_ref) for j, chip in enumerate(chips)]
        for cp in first:
            cp.start()
        passed = [copy(4 + j, (*chip, c), sib) for j, chip in enumerate(chips)]
        for j, chip in enumerate(chips):
            copy(1 + j, (*chip, c), me).wait_recv()
            passed[j].start()
        copy(0, sib, me).wait_recv()
        for j, chip in enumerate(chips):
            copy(4 + j, (*chip, 1 - c), me).wait_recv()
        for cp in first + passed:
            cp.wait_send()
        mine.wait()
        acc = all_ref[0:R, :]
        for d in range(1, ND):
            acc = acc + all_ref[d * R:(d + 1) * R, :]
        o_ref[...] = acc

    vm = pl.BlockSpec(memory_space=pltpu.VMEM)
    return pl.pallas_call(
        body, name=name, in_specs=[vm], out_specs=[vm, vm],
        out_shape=[jax.ShapeDtypeStruct((R, C), F32), jax.ShapeDtypeStruct((ND * R, C), F32)],
        scratch_shapes=[pltpu.SemaphoreType.DMA((7,)), pltpu.SemaphoreType.DMA((7,)), pltpu.SemaphoreType.DMA],
        compiler_params=pltpu.CompilerParams(vmem_limit_bytes=VMEM_LIMIT),
    )(v)[0]


HBM = pl.BlockSpec(memory_space=pltpu.HBM)
SEM = pl.BlockSpec(memory_space=pltpu.SEMAPHORE)
EFFECT = pltpu.SideEffectType.DATAFLOW_SIDE_EFFECTING


def _remote(src, dst, send, recv, k, to):
    return pltpu.make_async_remote_copy(src_ref=src, dst_ref=dst, send_sem=send.at[k], recv_sem=recv.at[k],
                                        device_id=to, device_id_type=MESH)


def _split_start(bufs, plan, ncopies, *, name, after=None):
    nb = len(bufs)
    extra = [] if after is None else [after]

    def body(*refs):
        pos = nb + len(extra)
        send, recv, token = refs[pos], refs[pos + 1], refs[-1]
        for k, (src, dst, to) in enumerate(plan(refs[:nb])):
            _remote(src, dst, send, recv, k, to).start()
        token[...] = jnp.zeros_like(token)

    outs = pl.pallas_call(
        body, name=name,
        out_shape=(pltpu.SemaphoreType.DMA((ncopies,)), pltpu.SemaphoreType.DMA((ncopies,)),
                   *[pltpu.HBM(b.shape, b.dtype) for b in bufs], jax.ShapeDtypeStruct((SUBLANES, LANES), F32)),
        in_specs=[HBM] * nb + [ANY] * len(extra),
        out_specs=(SEM, SEM, *[HBM] * nb, pl.BlockSpec(memory_space=pltpu.VMEM)),
        input_output_aliases={i: 2 + i for i in range(nb)},
        compiler_params=pltpu.CompilerParams(has_side_effects=EFFECT),
    )(*[pltpu.with_memory_space_constraint(b, pltpu.HBM) for b in bufs], *extra)
    return outs[0], outs[1], list(outs[2:2 + nb]), outs[-1]


def _split_wait(started, plan, after, *, name):
    send, recv, bufs, _ = started
    nb = len(bufs)

    def body(*refs):
        send_sem, recv_sem = refs[nb], refs[nb + 1]
        for k, (src, dst, to) in enumerate(plan(refs[:nb])):
            cp = _remote(src, dst, send_sem, recv_sem, k, to)
            cp.wait_send()
            cp.wait_recv()

    outs = pl.pallas_call(
        body, name=name,
        out_shape=tuple(pltpu.HBM(b.shape, b.dtype) for b in bufs),
        in_specs=[HBM] * nb + [SEM, SEM, ANY], out_specs=tuple([HBM] * nb),
        input_output_aliases={i: i for i in range(nb)},
        compiler_params=pltpu.CompilerParams(has_side_effects=EFFECT),
    )(*bufs, send, recv, after)
    return list(outs)


def _gather_chip_plan(shapes, by_cols):
    n = len(shapes)

    def plan(refs):
        srcs, lands = refs[:n], refs[n:]
        x, y, c, chips = _place()
        out = []
        for w in range(n):
            h = shapes[w][0] // 2
            mine = _owner_rows(lands[w], shapes[w], by_cols[w], 2 * x + y, c * h, h)
            for tx, ty in chips:
                out.append((srcs[w].at[pl.ds(c * h, h), :], mine, (tx, ty, c)))
        return out

    return plan


def _gather_pair_plan(shapes, by_cols):
    n = len(shapes)

    def plan(refs):
        srcs, lands = refs[:n], refs[n:]
        x, y, c, chips = _place()
        out = []
        for w in range(n):
            h = shapes[w][0] // 2
            for tx, ty in chips:
                half = _owner_rows(lands[w], shapes[w], by_cols[w], 2 * tx + ty, c * h, h)
                out.append((half, half, (x, y, 1 - c)))
            own = _owner_rows(lands[w], shapes[w], by_cols[w], 2 * x + y, 0, shapes[w][0])
            out.append((srcs[w], own, (x, y, 1 - c)))
        return out

    return plan


def _reduce_pair_plan(shapes):
    n = len(shapes)

    def plan(refs):
        local, lands = refs[:n], refs[n:]
        x, y, c, _ = _place()
        out = []
        for w in range(n):
            h = shapes[w][1] // 2
            out.append((local[w].at[:, pl.ds((1 - c) * h, h), :], lands[w], (x, y, 1 - c)))
        return out

    return plan


def _reduce_chip_plan(n):
    def plan(refs):
        parts, lands = refs[:n], refs[n:]
        x, y, c, chips = _place()
        return [(parts[w].at[2 * tx + ty], lands[w].at[j], (tx, ty, c))
                for w in range(n) for j, (tx, ty) in enumerate(chips)]

    return plan


def _as_operands(gathered):
    out = {}
    for n, g in gathered.items():
        if n in COL_SHARDED:
            out[n] = g
        elif n == "w_in":
            out[n] = jnp.transpose(g, (1, 0, 2)).reshape(D_MODEL, IN_COLS)
        else:
            out[n] = g.reshape(g.shape[0] * g.shape[1], g.shape[2])
    return out


def _by_owner(n, g):
    if n == "w_in":
        return jnp.transpose(g.reshape(D_MODEL, N_CHIPS, IN_COLS // N_CHIPS), (1, 0, 2))
    if g.ndim == 2:
        return g.reshape(N_CHIPS, g.shape[0] // N_CHIPS, g.shape[1])
    return g


class _Comm:
    def __init__(self, shards):
        self.shards = shards
        self.total = {}
        self._flight = {}

    def _layout(self, names):
        return [self.shards[n].shape for n in names], [n in COL_SHARDED for n in names]

    def gather_now(self, tag, names):
        _, by_cols = self._layout(names)
        got = _allgather_weights([self.shards[n] for n in names], by_cols, name=f"gather_{tag}")
        return _as_operands(dict(zip(names, got)))

    def gather_start(self, tag, names, after):
        shapes, by_cols = self._layout(names)
        srcs = [self.shards[n] for n in names]
        lands = [lax.empty(_gathered_shape(s.shape, bc), s.dtype) for s, bc in zip(srcs, by_cols)]
        started = _split_start(srcs + lands, _gather_chip_plan(shapes, by_cols), 3 * len(srcs),
                               after=after, name=f"gather_{tag}_chips_start")
        self._flight[tag] = (names, started)
        return started[3]

    def gather_landed(self, tag, after):
        names, started = self._flight[tag]
        shapes, by_cols = self._layout(names)
        bufs = _split_wait(started, _gather_chip_plan(shapes, by_cols), after, name=f"gather_{tag}_chips_wait")
        started = _split_start(bufs, _gather_pair_plan(shapes, by_cols), 4 * len(names),
                               name=f"gather_{tag}_pair_start")
        self._flight[tag] = (names, started)
        return started[3]

    def gather_ready(self, tag, after):
        names, started = self._flight.pop(tag)
        shapes, by_cols = self._layout(names)
        bufs = _split_wait(started, _gather_pair_plan(shapes, by_cols), after, name=f"gather_{tag}_pair_wait")
        return _as_operands(dict(zip(names, bufs[len(names):])))

    def reduce_start(self, tag, grads):
        names = list(grads)
        local = [_by_owner(n, grads[n]) for n in names]
        return self._chip_start(tag, names, local, _pair_exchange(local, name=f"pair_exchange_{tag}"))

    def reduce_pair_start(self, tag, grads):
        names = list(grads)
        local = [_by_owner(n, grads[n]) for n in names]
        lands = [lax.empty((N_CHIPS, g.shape[1] // 2, g.shape[2]), g.dtype) for g in local]
        started = _split_start(local + lands, _reduce_pair_plan([g.shape for g in local]), len(names),
                               name=f"pair_exchange_{tag}_start")
        self._flight[tag] = (names, started)
        return started[3]

    def reduce_pair_done(self, tag, after):
        names, started = self._flight.pop(tag)
        n = len(names)
        bufs = _split_wait(started, _reduce_pair_plan([b.shape for b in started[2][:n]]), after,
                           name=f"pair_exchange_{tag}_wait")
        return self._chip_start(tag, names, bufs[:n], bufs[n:])

    def _chip_start(self, tag, names, local, from_sib):
        parts = [_pair_sum(g, s, name=f"pair_sum_{n}") for n, g, s in zip(names, local, from_sib)]
        lands = [lax.empty((N_CHIPS - 1,) + p.shape[1:], p.dtype) for p in parts]
        self._flight[tag] = (names, _split_start(parts + lands, _reduce_chip_plan(len(names)), 3 * len(names),
                                                 name=f"chip_exchange_{tag}_start"))
        return self._flight[tag][1][3]

    def reduce_finish(self, tag, after):
        names, started = self._flight.pop(tag)
        n = len(names)
        bufs = _split_wait(started, _reduce_chip_plan(n), after, name=f"chip_exchange_{tag}_wait")
        totals = [_chip_sum(p, s, name=f"chip_sum_{nm}") for nm, p, s in zip(names, bufs[:n], bufs[n:])]
        self.total.update(zip(names, _pair_gather(totals, name=f"pair_gather_{tag}")))


def _adamw(w, g, m, v, *, name):
    R, C = w.shape
    tr = _row_block(R, 8 * C * 4, SUBLANES)

    def body(w_ref, g_ref, m_ref, v_ref, go_ref, d_ref, nm_ref, nv_ref):
        gg = g_ref[...]
        go_ref[...] = gg
        m_new = ADAM_B1 * m_ref[...] + (1.0 - ADAM_B1) * gg
        v_new = ADAM_B2 * v_ref[...] + (1.0 - ADAM_B2) * (gg * gg)
        m_hat = m_new / (1.0 - ADAM_B1 ** ADAM_STEP)
        v_hat = v_new / (1.0 - ADAM_B2 ** ADAM_STEP)
        d_ref[...] = -ADAM_LR * (m_hat / (jnp.sqrt(v_hat) + ADAM_EPS) + ADAM_WD * w_ref[...])
        nm_ref[...] = m_new
        nv_ref[...] = v_new

    blk = pl.BlockSpec((tr, C), lambda i: (i, 0))
    shp = jax.ShapeDtypeStruct((R, C), F32)
    return pl.pallas_call(
        body, name=name, grid=(R // tr,), in_specs=[blk] * 4, out_specs=[blk] * 4, out_shape=[shp] * 4,
        compiler_params=_params(("parallel",)),
    )(w, g, m, v)


def _to2d(a):
    flat = a.reshape(-1)
    pad = (-flat.shape[0]) % (SUBLANES * LANES)
    if pad:
        flat = jnp.pad(flat, (0, pad))
    return flat.reshape(-1, LANES)


def _small_rows(shape):
    return -(-math.prod(shape) // (SUBLANES * LANES)) * SUBLANES


def _pack_small(parts):
    rows = jnp.concatenate([_to2d(p) for p in parts], axis=0)
    pad = (-rows.shape[0]) % 256
    if pad:
        rows = jnp.concatenate([rows, jnp.zeros((pad, LANES), rows.dtype)], axis=0)
    return rows


def _unpack_small(rows, shapes):
    out, r = [], 0
    for shp in shapes:
        size = math.prod(shp)
        nrow = _small_rows(shp)
        out.append(rows[r:r + nrow].reshape(-1)[:size].reshape(shp))
        r += nrow
    return out


def kernel(x, mem, g_ffn1, w1_gate, w1_up, w1_down, g_mix, w_in, g_v, w_s, b_s, sinks, g_a_out, g_b_out, w_out, g_x, g_mem, w_xq, w_xkv, w_xo, g_ffn2, w2_gate, w2_up, w2_down, g_final, loss_target, m_g_ffn1, m_w1_gate, m_w1_up, m_w1_down, m_g_mix, m_w_in, m_g_v, m_w_s, m_b_s, m_sinks, m_g_a_out, m_g_b_out, m_w_out, m_g_x, m_g_mem, m_w_xq, m_w_xkv, m_w_xo, m_g_ffn2, m_w2_gate, m_w2_up, m_w2_down, m_g_final, v_g_ffn1, v_w1_gate, v_w1_up, v_w1_down, v_g_mix, v_w_in, v_g_v, v_w_s, v_b_s, v_sinks, v_g_a_out, v_g_b_out, v_w_out, v_g_x, v_g_mem, v_w_xq, v_w_xkv, v_w_xo, v_g_ffn2, v_w2_gate, v_w2_up, v_w2_down, v_g_final):
    args = dict(locals())
    Wp = {n: args[n] for n in ALL_W}
    Mp = {n: args["m_" + n] for n in ALL_W}
    Vp = {n: args["v_" + n] for n in ALL_W}

    comm = _Comm({n: Wp[n][0].astype(BF16) for n in BIG})
    W = {n: Wp[n] for n in SMALL}
    W["g_final"] = Wp["g_final"].reshape(1, D_MODEL)
    for n in ("w_s", "b_s"):
        W[n] = Wp[n][0]
    loss, dx, grads = _local_step(x[0], mem[0], loss_target[0], W, comm)
    big_grad = comm.total

    small_shapes = [Wp[n].shape for n in SMALL]
    packed = _pack_small([grads[n].reshape(Wp[n].shape) for n in SMALL] + [loss])
    summed = _allreduce_small(packed, name="allreduce_small")
    small_grad = dict(zip(SMALL, _unpack_small(summed, small_shapes)))
    nrows = sum(_small_rows(s) for s in small_shapes)
    loss_total = summed[nrows, 0]

    grad_out, delta, new_m, new_v = {}, {}, {}, {}
    for n in BIG:
        shp = Wp[n].shape
        g, d, nm, nv = _adamw(Wp[n][0], big_grad[n], Mp[n][0], Vp[n][0], name=f"adamw_{n}")
        grad_out[n], delta[n], new_m[n], new_v[n] = g.reshape(shp), d.reshape(shp), nm.reshape(shp), nv.reshape(shp)
    sw = _pack_small([Wp[n] for n in SMALL])
    sg = _pack_small([small_grad[n] for n in SMALL])
    sm = _pack_small([Mp[n] for n in SMALL])
    sv = _pack_small([Vp[n] for n in SMALL])
    _, d, nm, nv = _adamw(sw, sg, sm, sv, name="adamw_small")
    for n, dd, mm_, vv_ in zip(SMALL, _unpack_small(d, small_shapes), _unpack_small(nm, small_shapes),
                               _unpack_small(nv, small_shapes)):
        grad_out[n], delta[n], new_m[n], new_v[n] = small_grad[n], dd, mm_, vv_

    return (loss_total, dx[None], *[grad_out[n] for n in ALL_W], *[delta[n] for n in ALL_W],
            *[new_m[n] for n in ALL_W], *[new_v[n] for n in ALL_W])
```

```python
import functools
import math

import jax
import jax.numpy as jnp
from jax import lax
from jax.experimental import pallas as pl
from jax.experimental.pallas import tpu as pltpu

F32 = jnp.float32
BF16 = jnp.bfloat16
MESH = pl.DeviceIdType.MESH

D_MODEL = 2048
D_FF = 5632
D_A = 1024
D_B = 1024
CHUNK = 128
A_GROUPS = 8
HEAD_DIM = 64
B_Q_HEADS = 16
B_KV_HEADS = 2
X_HEADS = 4
X_HEAD_DIM = 512
IN_COLS = 3328
O_Q = 2 * D_A
O_K = O_Q + D_B
O_V = O_K + B_KV_HEADS * HEAD_DIM
N_CHIPS = 4
EPS = 1e-5
NEG = -1e30
ADAM_LR = 0.001
ADAM_B1 = 0.9
ADAM_B2 = 0.999
ADAM_EPS = 1e-08
ADAM_WD = 0.01
ADAM_STEP = 10

V7X_VMEM_BYTES = 64 * 1024 * 1024
VMEM_LIMIT = 56 * 1024 * 1024
LANES = 128
SUBLANES = 8


ANY = pl.BlockSpec(memory_space=pl.ANY)


def _params(sem, vmem=VMEM_LIMIT):
    return pltpu.CompilerParams(dimension_semantics=sem, vmem_limit_bytes=vmem)


def _matmul(pairs, *, M, N, K, tm, tn, tk, a_t=False, b_kind="n", out_kind="n", out_dtype=F32,
            scale=1.0, res=None, norm_g=None, order="ij", dep=None, name):
    tm, tn, tk = min(tm, M), min(tn, N), min(tk, K)
    assert M % tm == 0 and N % tn == 0 and K % tk == 0, (name, M, N, K, tm, tn, tk)
    nk = K // tk
    npairs = len(pairs)
    b_t = b_kind == "t"
    ns = N // N_CHIPS

    def ij(g0, g1):
        return (g0, g1) if order == "ij" else (g1, g0)

    def a_map(g0, g1, k):
        i, _ = ij(g0, g1)
        return (k, i) if a_t else (i, k)

    a_spec = pl.BlockSpec((tk, tm) if a_t else (tm, tk), a_map)

    if b_kind == "n":
        b_spec = pl.BlockSpec((tk, tn), lambda g0, g1, k: (k, ij(g0, g1)[1]))
    else:
        b_spec = pl.BlockSpec((tn, tk), lambda g0, g1, k: (ij(g0, g1)[1], k))

    if out_kind == "n":
        o_spec = pl.BlockSpec((tm, tn), lambda g0, g1, k: ij(g0, g1))
        o_shape = jax.ShapeDtypeStruct((M, N), out_dtype)
    else:
        assert tn % ns == 0
        o_spec = pl.BlockSpec((tn // ns, tm, ns), lambda g0, g1, k: (ij(g0, g1)[1], ij(g0, g1)[0], 0))
        o_shape = jax.ShapeDtypeStruct((N_CHIPS, M, ns), out_dtype)

    in_specs, args = [], []
    for a, b in pairs:
        in_specs += [a_spec, b_spec]
        args += [a, b]
    if res is not None:
        in_specs.append(pl.BlockSpec((tm, tn), lambda g0, g1, k: ij(g0, g1)))
        args.append(res)
    if norm_g is not None:
        assert tn == N and out_kind == "n"
        in_specs.append(pl.BlockSpec((1, N), lambda g0, g1, k: (0, 0)))
        args.append(norm_g)
    if dep is not None:
        in_specs.append(ANY)
        args.append(dep)

    dn = (((0,) if a_t else (1,), (1,) if b_t else (0,)), ((), ()))

    def body(*refs):
        pos = 2 * npairs
        res_ref = refs[pos] if res is not None else None
        pos += res is not None
        g_ref = refs[pos] if norm_g is not None else None
        pos += (norm_g is not None) + (dep is not None)
        o_ref = refs[pos]
        n_ref = refs[pos + 1] if norm_g is not None else None
        acc_ref = refs[-1] if nk > 1 else None
        part = None
        for p in range(npairs):
            d = lax.dot_general(refs[2 * p][...], refs[2 * p + 1][...], dn, preferred_element_type=F32)
            part = d if part is None else part + d

        def finish(acc):
            r = acc * scale if scale != 1.0 else acc
            if res_ref is not None:
                r = res_ref[...] + r
            if out_kind == "n":
                o_ref[...] = r.astype(out_dtype)
            else:
                for s in range(tn // ns):
                    o_ref[s] = r[:, s * ns:(s + 1) * ns].astype(out_dtype)
            if n_ref is not None:
                n_ref[...] = (r * _rstd(r) * g_ref[...]).astype(BF16)

        if nk == 1:
            finish(part)
        else:
            k = pl.program_id(2)

            @pl.when(k == 0)
            def _():
                acc_ref[...] = part

            @pl.when((k > 0) & (k < nk - 1))
            def _():
                acc_ref[...] += part

            @pl.when(k == nk - 1)
            def _():
                finish(acc_ref[...] + part)

    grid = (M // tm, N // tn, nk) if order == "ij" else (N // tn, M // tm, nk)
    out_specs, out_shape = o_spec, o_shape
    if norm_g is not None:
        out_specs = [o_spec, pl.BlockSpec((tm, tn), lambda g0, g1, k: ij(g0, g1))]
        out_shape = [o_shape, jax.ShapeDtypeStruct((M, N), BF16)]
    return pl.pallas_call(
        body, name=name, grid=grid, in_specs=in_specs, out_specs=out_specs, out_shape=out_shape,
        scratch_shapes=[pltpu.VMEM((tm, tn), F32)] if nk > 1 else [],
        compiler_params=_params(("parallel", "parallel", "arbitrary")),
    )(*args)


def _rstd(x):
    return lax.rsqrt(jnp.mean(x * x, axis=-1, keepdims=True) + EPS)


def _rms_bwd_math(x, g, dy):
    r = _rstd(x)
    gy = dy * g
    xr = x * r
    dx = r * (gy - xr * jnp.mean(gy * xr, axis=-1, keepdims=True))
    return dx, dy * xr


def _rms_fwd(h, g, *, name, tm=512, dep=None):
    T, Dm = h.shape
    tm = min(tm, T)

    def body(h_ref, g_ref, *rest):
        x = h_ref[...]
        rest[-1][...] = (x * _rstd(x) * g_ref[...]).astype(BF16)

    return pl.pallas_call(
        body, name=name, grid=(T // tm,),
        in_specs=[pl.BlockSpec((tm, Dm), lambda i: (i, 0)), pl.BlockSpec((1, Dm), lambda i: (0, 0))]
        + ([ANY] if dep is not None else []),
        out_specs=pl.BlockSpec((tm, Dm), lambda i: (i, 0)),
        out_shape=jax.ShapeDtypeStruct((T, Dm), BF16),
        compiler_params=_params(("parallel",)),
    )(h, g, *([dep] if dep is not None else []))


def _rms_bwd(h, g, dn, dres, *, name, tm=256, dep=None):
    T, Dm = h.shape
    tm = min(tm, T)
    has_res = dres is not None

    def body(*refs):
        h_ref, g_ref, dn_ref = refs[:3]
        pos = 3
        dres_ref = refs[pos] if has_res else None
        pos += has_res + (dep is not None)
        dh_ref, dhb_ref, dg_ref = refs[pos:pos + 3]
        dx, dgr = _rms_bwd_math(h_ref[...], g_ref[...], dn_ref[...].astype(F32))
        if has_res:
            dx = dres_ref[...] + dx
        dh_ref[...] = dx
        dhb_ref[...] = dx.astype(BF16)
        part = jnp.sum(dgr, axis=0, keepdims=True)

        @pl.when(pl.program_id(0) == 0)
        def _():
            dg_ref[...] = part

        @pl.when(pl.program_id(0) > 0)
        def _():
            dg_ref[...] += part

    row = pl.BlockSpec((tm, Dm), lambda i: (i, 0))
    vec = pl.BlockSpec((1, Dm), lambda i: (0, 0))
    args = [h, g, dn] + ([dres] if has_res else []) + ([dep] if dep is not None else [])
    return pl.pallas_call(
        body, name=name, grid=(T // tm,),
        in_specs=[row, vec, row] + ([row] if has_res else []) + ([ANY] if dep is not None else []),
        out_specs=[row, row, vec],
        out_shape=[jax.ShapeDtypeStruct((T, Dm), F32), jax.ShapeDtypeStruct((T, Dm), BF16),
                   jax.ShapeDtypeStruct((1, Dm), F32)],
        compiler_params=_params(("arbitrary",)),
    )(*args)


def _loss_head(h, g, tgt, *, name, tm=256):
    T, Dm = h.shape
    tm = min(tm, T)

    def body(h_ref, g_ref, t_ref, dh_ref, dhb_ref, dg_ref, loss_ref):
        x = h_ref[...]
        gv = g_ref[...]
        r = _rstd(x)
        diff = x * r * gv - t_ref[...]
        lpart = 0.5 * jnp.sum(jnp.mean(diff * diff, axis=-1, keepdims=True), axis=0, keepdims=True)
        dx, dgr = _rms_bwd_math(x, gv, diff * (1.0 / Dm))
        dh_ref[...] = dx
        dhb_ref[...] = dx.astype(BF16)
        part = jnp.sum(dgr, axis=0, keepdims=True)
        lrow = jnp.broadcast_to(lpart, (1, LANES))

        @pl.when(pl.program_id(0) == 0)
        def _():
            dg_ref[...] = part
            loss_ref[...] = lrow

        @pl.when(pl.program_id(0) > 0)
        def _():
            dg_ref[...] += part
            loss_ref[...] += lrow

    row = pl.BlockSpec((tm, Dm), lambda i: (i, 0))
    vec = pl.BlockSpec((1, Dm), lambda i: (0, 0))
    return pl.pallas_call(
        body, name=name, grid=(T // tm,),
        in_specs=[row, vec, row],
        out_specs=[row, row, vec, pl.BlockSpec((1, LANES), lambda i: (0, 0))],
        out_shape=[jax.ShapeDtypeStruct((T, Dm), F32), jax.ShapeDtypeStruct((T, Dm), BF16),
                   jax.ShapeDtypeStruct((1, Dm), F32), jax.ShapeDtypeStruct((1, LANES), F32)],
        compiler_params=_params(("arbitrary",)),
    )(h, g, tgt)


MXU_COLS = 256
FF_TILE = 2 * MXU_COLS


def _row_block(rows, row_bytes, align, budget=24 * 1024 * 1024):
    fits = [d for d in range(align, rows + 1, align) if rows % d == 0 and 2 * d * row_bytes <= budget]
    assert fits, (rows, row_bytes)
    return fits[-1]


def _swiglu_up(n, wg, wu, *, name, tm=1024, tn=FF_TILE):
    T, Dm = n.shape
    Fd = wg.shape[1]
    tm = min(tm, T)

    def body(n_ref, wg_ref, wu_ref, pg_ref, pu_ref, a_ref):
        x = n_ref[...]
        g = jnp.dot(x, wg_ref[...], preferred_element_type=F32)
        u = jnp.dot(x, wu_ref[...], preferred_element_type=F32)
        sg = jax.nn.sigmoid(g)
        silu = g * sg
        pg_ref[...] = ((sg + silu * (1.0 - sg)) * u).astype(BF16)
        pu_ref[...] = silu.astype(BF16)
        a_ref[...] = (silu * u).astype(BF16)

    wspec = pl.BlockSpec((Dm, tn), lambda j, i: (0, j))
    ospec = pl.BlockSpec((tm, tn), lambda j, i: (i, j))
    oshape = jax.ShapeDtypeStruct((T, Fd), BF16)
    return pl.pallas_call(
        body, name=name, grid=(Fd // tn, T // tm),
        in_specs=[pl.BlockSpec((tm, Dm), lambda j, i: (i, 0)), wspec, wspec],
        out_specs=[ospec, ospec, ospec], out_shape=[oshape, oshape, oshape],
        compiler_params=_params(("parallel", "parallel")),
    )(n, wg, wu)


def _swiglu_bwd_act(dhb, wd, PG, PU, *, name, tm=1024, tn=D_FF // N_CHIPS):
    T, Dm = dhb.shape
    Fd = wd.shape[0]
    tm, tn = min(tm, T), min(tn, Fd)

    def body(dh_ref, wd_ref, pg_ref, pu_ref, dg_ref, du_ref):
        da = 0.5 * lax.dot_general(dh_ref[...], wd_ref[...], (((1,), (1,)), ((), ())), preferred_element_type=F32)
        dg_ref[...] = (da * pg_ref[...].astype(F32)).astype(BF16)
        du_ref[...] = (da * pu_ref[...].astype(F32)).astype(BF16)

    blk = pl.BlockSpec((tm, tn), lambda j, i: (i, j))
    oshape = jax.ShapeDtypeStruct((T, Fd), BF16)
    return pl.pallas_call(
        body, name=name, grid=(Fd // tn, T // tm),
        in_specs=[pl.BlockSpec((tm, Dm), lambda j, i: (i, 0)), pl.BlockSpec((tn, Dm), lambda j, i: (j, 0)), blk, blk],
        out_specs=[blk, blk], out_shape=[oshape, oshape],
        compiler_params=_params(("parallel", "parallel")),
    )(dhb, wd, PG, PU)


_INV_SQRT2 = 0.7071067811865476
_INV_SQRT2PI = 0.3989422804014327


def _erf(x):
    ax = jnp.abs(x)
    t = 1.0 / (1.0 + 0.3275911 * ax)
    poly = t * (0.254829592 + t * (-0.284496736 + t * (1.421413741 + t * (-1.453152027 + t * 1.061405429))))
    y = 1.0 - poly * jnp.exp(-ax * ax)
    return jnp.where(x < 0, -y, y)


def _gelu_cdf(x):
    return 0.5 * (1.0 + _erf(x * _INV_SQRT2))


def _lane_lt64(shape):
    return lax.broadcasted_iota(jnp.int32, shape, len(shape) - 1) < HEAD_DIM


def _dup_half(x, kv):
    rolled = pltpu.roll(x, HEAD_DIM, 1)
    lo = _lane_lt64(x.shape)
    return jnp.where(lo, x, rolled) if kv == 0 else jnp.where(lo, rolled, x)


HEADS_PER_KV = B_Q_HEADS // B_KV_HEADS
PAIRS = HEADS_PER_KV // 2


def _attn_bias():
    shape = (2 * CHUNK, HEADS_PER_KV * CHUNK)
    qpos = (lax.broadcasted_iota(jnp.int32, shape, 1) & (CHUNK - 1)) + CHUNK
    kpos = lax.broadcasted_iota(jnp.int32, shape, 0)
    diff = qpos - kpos
    band = (diff >= 0) & (diff < CHUNK)
    return jnp.stack([jnp.where(band & (kpos >= CHUNK), 0.0, NEG), jnp.where(band, 0.0, NEG)]).astype(F32)


def _stack_heads(tiles, lo):
    parts = []
    for t in tiles:
        parts += [jnp.where(lo, t, 0.0), jnp.where(lo, 0.0, t)]
    return jnp.concatenate(parts, axis=0)


def _unstack_heads(s, lo):
    return [jnp.where(lo, s[2 * p * CHUNK:(2 * p + 1) * CHUNK], s[(2 * p + 1) * CHUNK:(2 * p + 2) * CHUNK])
            for p in range(PAIRS)]


def _stack_sinks(sk_ref, kv):
    return jnp.concatenate([jnp.broadcast_to(sk_ref[:, h:h + 1], (1, CHUNK))
                            for h in range(kv * HEADS_PER_KV, (kv + 1) * HEADS_PER_KV)], axis=1)


def _sgu_forward(z_ref, gv, wsm, bst):
    zu = z_ref[:, 0:D_A]
    zv = z_ref[:, D_A:2 * D_A]
    cu = _gelu_cdf(zu)
    cv = _gelu_cdf(zv)
    u = zu * cu
    v = zv * cv
    rv = _rstd(v)
    vn = (v * rv * gv).astype(BF16)
    svs = []
    for g in range(A_GROUPS):
        sl = slice(g * CHUNK, (g + 1) * CHUNK)
        svs.append(jnp.dot(wsm[g], vn[:, sl], preferred_element_type=F32) + bst[:, g:g + 1])
    sv = jnp.concatenate(svs, axis=1)
    return (zu, zv, cu, cv), u, v, rv, vn, sv


def _masked_ws(ws_ref):
    tril = lax.broadcasted_iota(jnp.int32, (CHUNK, CHUNK), 0) >= lax.broadcasted_iota(jnp.int32, (CHUNK, CHUNK), 1)
    return [jnp.where(tril, ws_ref[g], 0.0).astype(BF16) for g in range(A_GROUPS)], tril


def _attn_probs(qm, kkd, sink, bias):
    s = lax.dot_general(kkd, qm, (((1,), (1,)), ((), ())), preferred_element_type=F32) * (HEAD_DIM ** -0.5) + bias
    m = jnp.maximum(jnp.max(s, axis=0, keepdims=True), sink)
    e = jnp.exp(s - m)
    inv = 1.0 / (jnp.sum(e, axis=0, keepdims=True) + jnp.exp(sink - m))
    return e * inv


def _mixer_fwd(z, gv, ws, bst, sinks, ga, gb, *, name):
    T = z.shape[0]
    nb = T // CHUNK
    kvb = O_K // (2 * CHUNK)

    def body(z_ref, zp_ref, bias_ref, gv_ref, ws_ref, bst_ref, sk_ref, ga_ref, gb_ref, o_ref, p_ref):
        wsm, _ = _masked_ws(ws_ref)
        _, u, _, _, _, sv = _sgu_forward(z_ref, gv_ref[...], wsm, bst_ref[...])
        ya = u * sv
        o_ref[:, 0:D_A] = (ya * _rstd(ya) * ga_ref[...]).astype(BF16)

        mask = bias_ref[...]
        kk = jnp.concatenate([zp_ref[:, 0:CHUNK], z_ref[:, O_K:O_V]], axis=0)
        vv = jnp.concatenate([zp_ref[:, CHUNK:2 * CHUNK], z_ref[:, O_V:IN_COLS]], axis=0)
        lo = _lane_lt64((CHUNK, LANES))
        outs = []
        for kv in range(B_KV_HEADS):
            kkd = _dup_half(kk, kv).astype(BF16)
            vvd = _dup_half(vv, kv).astype(BF16)
            q = _stack_heads([z_ref[:, O_Q + (kv * PAIRS + pr) * LANES:O_Q + (kv * PAIRS + pr + 1) * LANES]
                              for pr in range(PAIRS)], lo).astype(BF16)
            p = _attn_probs(q, kkd, _stack_sinks(sk_ref, kv), mask)
            p_ref[kv] = p
            out = lax.dot_general(p.astype(BF16), vvd, (((0,), (0,)), ((), ())), preferred_element_type=F32)
            outs += _unstack_heads(out, lo)
        yb = jnp.concatenate(outs, axis=1)
        o_ref[:, D_A:D_A + D_B] = (yb * _rstd(yb) * gb_ref[...]).astype(BF16)

    full = lambda shape: pl.BlockSpec(shape, lambda i: (0,) * len(shape))
    pshape = (B_KV_HEADS, 2 * CHUNK, HEADS_PER_KV * CHUNK)
    return pl.pallas_call(
        body, name=name, grid=(nb,),
        in_specs=[pl.BlockSpec((CHUNK, IN_COLS), lambda i: (i, 0)),
                  pl.BlockSpec((CHUNK, 2 * CHUNK), lambda i: (jnp.maximum(i - 1, 0), kvb)),
                  pl.BlockSpec((None, 2 * CHUNK, HEADS_PER_KV * CHUNK), lambda i: (jnp.minimum(i, 1), 0, 0)),
                  full((1, D_A)), full((A_GROUPS, CHUNK, CHUNK)), full((CHUNK, A_GROUPS)), full((1, B_Q_HEADS)),
                  full((1, D_A)), full((1, D_B))],
        out_specs=[pl.BlockSpec((CHUNK, D_A + D_B), lambda i: (i, 0)),
                   pl.BlockSpec((None,) + pshape, lambda i: (i, 0, 0, 0))],
        out_shape=[jax.ShapeDtypeStruct((T, D_A + D_B), BF16), jax.ShapeDtypeStruct((nb,) + pshape, F32)],
        compiler_params=_params(("parallel",)),
    )(z, z, _attn_bias(), gv, ws, bst, sinks, ga, gb)


def _mixer_bwd(z, dyn, probs, gv, ws, bst, ga, gb, *, name):
    T = z.shape[0]
    nb = T // CHUNK
    kvb = O_K // (2 * CHUNK)
    NT = (((0,), (0,)), ((), ()))

    def body(z_ref, zp_ref, dy_ref, p_ref, gv_ref, ws_ref, bst_ref, ga_ref, gb_ref,
             dz_ref, dgv_ref, dws_ref, dbst_ref, dsk_ref, dga_ref, dgb_ref, carry_ref):
        step = pl.program_id(0)

        @pl.when(step == 0)
        def _():
            carry_ref[...] = jnp.zeros_like(carry_ref)
            dgv_ref[...] = jnp.zeros_like(dgv_ref)
            dws_ref[...] = jnp.zeros_like(dws_ref)
            dbst_ref[...] = jnp.zeros_like(dbst_ref)
            dsk_ref[...] = jnp.zeros_like(dsk_ref)
            dga_ref[...] = jnp.zeros_like(dga_ref)
            dgb_ref[...] = jnp.zeros_like(dgb_ref)

        wsm, tril = _masked_ws(ws_ref)
        gvv = gv_ref[...]
        (zu, zv, cu, cv), u, v, rv, vn, sv = _sgu_forward(z_ref, gvv, wsm, bst_ref[...])
        ya = u * sv
        dya, dga_rows = _rms_bwd_math(ya, ga_ref[...], dy_ref[:, 0:D_A].astype(F32))
        dga_ref[...] += jnp.sum(dga_rows, axis=0, keepdims=True)
        du = dya * sv
        dsv = dya * u
        dvn_parts = []
        for g in range(A_GROUPS):
            sl = slice(g * CHUNK, (g + 1) * CHUNK)
            dsv_g = dsv[:, sl]
            dsv_gb = dsv_g.astype(BF16)
            dw = lax.dot_general(dsv_gb, vn[:, sl], (((1,), (1,)), ((), ())), preferred_element_type=F32)
            dws_ref[g] += jnp.where(tril, dw, 0.0)
            dbst_ref[:, g:g + 1] += jnp.sum(dsv_g, axis=1, keepdims=True)
            dvn_parts.append(lax.dot_general(wsm[g], dsv_gb, NT, preferred_element_type=F32))
        dvn = jnp.concatenate(dvn_parts, axis=1)
        dv, dgv_rows = _rms_bwd_math(v, gvv, dvn)
        dgv_ref[...] += jnp.sum(dgv_rows, axis=0, keepdims=True)
        dz_ref[:, 0:D_A] = (du * (cu + zu * jnp.exp(-0.5 * zu * zu) * _INV_SQRT2PI)).astype(BF16)
        dz_ref[:, D_A:2 * D_A] = (dv * (cv + zv * jnp.exp(-0.5 * zv * zv) * _INV_SQRT2PI)).astype(BF16)

        kk = jnp.concatenate([zp_ref[:, 0:CHUNK], z_ref[:, O_K:O_V]], axis=0)
        vv = jnp.concatenate([zp_ref[:, CHUNK:2 * CHUNK], z_ref[:, O_V:IN_COLS]], axis=0)
        lo = _lane_lt64((CHUNK, LANES))
        kkd = [_dup_half(kk, kv).astype(BF16) for kv in range(B_KV_HEADS)]
        vvd = [_dup_half(vv, kv).astype(BF16) for kv in range(B_KV_HEADS)]
        outs = []
        for kv in range(B_KV_HEADS):
            outs += _unstack_heads(lax.dot_general(p_ref[kv].astype(BF16), vvd[kv], NT, preferred_element_type=F32), lo)
        yb = jnp.concatenate(outs, axis=1)
        dyb, dgb_rows = _rms_bwd_math(yb, gb_ref[...], dy_ref[:, D_A:D_A + D_B].astype(F32))
        dgb_ref[...] += jnp.sum(dgb_rows, axis=0, keepdims=True)

        dkk, dvv = [], []
        for kv in range(B_KV_HEADS):
            do = _stack_heads([dyb[:, (kv * PAIRS + pr) * LANES:(kv * PAIRS + pr + 1) * LANES]
                               for pr in range(PAIRS)], lo).astype(BF16)
            q = _stack_heads([z_ref[:, O_Q + (kv * PAIRS + pr) * LANES:O_Q + (kv * PAIRS + pr + 1) * LANES]
                              for pr in range(PAIRS)], lo).astype(BF16)
            p = p_ref[kv]
            dvv.append(jnp.dot(p.astype(BF16), do, preferred_element_type=F32))
            dp = lax.dot_general(vvd[kv], do, (((1,), (1,)), ((), ())), preferred_element_type=F32)
            delta = jnp.sum(p * dp, axis=0, keepdims=True)
            dsink = (jnp.sum(p, axis=0, keepdims=True) - 1.0) * delta
            for g in range(HEADS_PER_KV):
                h = kv * HEADS_PER_KV + g
                dsk_ref[:, h:h + 1] += jnp.sum(dsink[:, g * CHUNK:(g + 1) * CHUNK], axis=1, keepdims=True)
            ds = (p * (dp - delta) * (HEAD_DIM ** -0.5)).astype(BF16)
            dq = _unstack_heads(lax.dot_general(ds, kkd[kv], NT, preferred_element_type=F32), lo)
            for pr in range(PAIRS):
                c0 = O_Q + (kv * PAIRS + pr) * LANES
                dz_ref[:, c0:c0 + LANES] = dq[pr].astype(BF16)
            dkk.append(jnp.dot(ds, q, preferred_element_type=F32))

        def fold(parts):
            tot = [t + pltpu.roll(t, HEAD_DIM, 1) for t in parts]
            return jnp.where(_lane_lt64(tot[0].shape), tot[0], tot[1])

        dk_all = fold(dkk)
        dv_all = fold(dvv)
        dz_ref[:, O_K:O_V] = (dk_all[CHUNK:] + carry_ref[:, 0:CHUNK]).astype(BF16)
        dz_ref[:, O_V:IN_COLS] = (dv_all[CHUNK:] + carry_ref[:, CHUNK:2 * CHUNK]).astype(BF16)
        carry_ref[:, 0:CHUNK] = dk_all[:CHUNK]
        carry_ref[:, CHUNK:2 * CHUNK] = dv_all[:CHUNK]

    full = lambda shape: pl.BlockSpec(shape, lambda s: (0,) * len(shape))
    rev = lambda s: nb - 1 - s
    return pl.pallas_call(
        body, name=name, grid=(nb,),
        in_specs=[pl.BlockSpec((CHUNK, IN_COLS), lambda s: (rev(s), 0)),
                  pl.BlockSpec((CHUNK, 2 * CHUNK), lambda s: (jnp.maximum(rev(s) - 1, 0), kvb)),
                  pl.BlockSpec((CHUNK, D_A + D_B), lambda s: (rev(s), 0)),
                  pl.BlockSpec((None, B_KV_HEADS, 2 * CHUNK, HEADS_PER_KV * CHUNK), lambda s: (rev(s), 0, 0, 0)),
                  full((1, D_A)), full((A_GROUPS, CHUNK, CHUNK)), full((CHUNK, A_GROUPS)),
                  full((1, D_A)), full((1, D_B))],
        out_specs=[pl.BlockSpec((CHUNK, IN_COLS), lambda s: (rev(s), 0)),
                   full((1, D_A)), full((A_GROUPS, CHUNK, CHUNK)), full((CHUNK, A_GROUPS)), full((1, B_Q_HEADS)),
                   full((1, D_A)), full((1, D_B))],
        out_shape=[jax.ShapeDtypeStruct((T, IN_COLS), BF16), jax.ShapeDtypeStruct((1, D_A), F32),
                   jax.ShapeDtypeStruct((A_GROUPS, CHUNK, CHUNK), F32), jax.ShapeDtypeStruct((CHUNK, A_GROUPS), F32),
                   jax.ShapeDtypeStruct((1, B_Q_HEADS), F32), jax.ShapeDtypeStruct((1, D_A), F32),
                   jax.ShapeDtypeStruct((1, D_B), F32)],
        scratch_shapes=[pltpu.VMEM((CHUNK, 2 * CHUNK), F32)],
        compiler_params=_params(("arbitrary",)),
    )(z, z, dyn, probs, gv, ws, bst, ga, gb)


def _xattn_probs(qh, kh):
    s = lax.dot_general(kh, qh, (((1,), (1,)), ((), ())), preferred_element_type=F32) * (X_HEAD_DIM ** -0.5)
    e = jnp.exp(s - jnp.max(s, axis=0, keepdims=True))
    return e / jnp.sum(e, axis=0, keepdims=True)


def _xattn_fwd(q, kvm, *, name, tm=512):
    T = q.shape[0]
    Mm = kvm.shape[0]
    tm = min(tm, T)

    def body(q_ref, kv_ref, o_ref, p_ref):
        for h in range(X_HEADS):
            sl = slice(h * X_HEAD_DIM, (h + 1) * X_HEAD_DIM)
            kh = kv_ref[:, sl].astype(BF16)
            vh = kv_ref[:, D_MODEL + h * X_HEAD_DIM:D_MODEL + (h + 1) * X_HEAD_DIM].astype(BF16)
            p = _xattn_probs(q_ref[:, sl], kh)
            p_ref[h * Mm:(h + 1) * Mm, :] = p
            o_ref[:, sl] = lax.dot_general(p.astype(BF16), vh, (((0,), (0,)), ((), ())),
                                           preferred_element_type=F32).astype(BF16)

    return pl.pallas_call(
        body, name=name, grid=(T // tm,),
        in_specs=[pl.BlockSpec((tm, D_MODEL), lambda i: (i, 0)), pl.BlockSpec((Mm, 2 * D_MODEL), lambda i: (0, 0))],
        out_specs=[pl.BlockSpec((tm, D_MODEL), lambda i: (i, 0)), pl.BlockSpec((X_HEADS * Mm, tm), lambda i: (0, i))],
        out_shape=[jax.ShapeDtypeStruct((T, D_MODEL), BF16), jax.ShapeDtypeStruct((X_HEADS * Mm, T), F32)],
        compiler_params=_params(("parallel",)),
    )(q, kvm)


def _xattn_bwd(q, kvm, probs, do, *, name, tm=512):
    T = q.shape[0]
    Mm = kvm.shape[0]
    tm = min(tm, T)
    NT = (((0,), (0,)), ((), ()))

    def body(q_ref, kv_ref, p_ref, do_ref, dq_ref, dkv_ref):
        @pl.when(pl.program_id(0) == 0)
        def _():
            dkv_ref[...] = jnp.zeros_like(dkv_ref)

        for h in range(X_HEADS):
            sl = slice(h * X_HEAD_DIM, (h + 1) * X_HEAD_DIM)
            slv = slice(D_MODEL + h * X_HEAD_DIM, D_MODEL + (h + 1) * X_HEAD_DIM)
            kh = kv_ref[:, sl].astype(BF16)
            vh = kv_ref[:, slv].astype(BF16)
            qh = q_ref[:, sl]
            doh = do_ref[:, sl]
            p = p_ref[h * Mm:(h + 1) * Mm, :]
            dkv_ref[:, slv] += jnp.dot(p.astype(BF16), doh, preferred_element_type=F32)
            dp = lax.dot_general(vh, doh, (((1,), (1,)), ((), ())), preferred_element_type=F32)
            ds = (p * (dp - jnp.sum(p * dp, axis=0, keepdims=True)) * (X_HEAD_DIM ** -0.5)).astype(BF16)
            dq_ref[:, sl] = lax.dot_general(ds, kh, NT, preferred_element_type=F32).astype(BF16)
            dkv_ref[:, sl] += jnp.dot(ds, qh, preferred_element_type=F32)

    row = pl.BlockSpec((tm, D_MODEL), lambda i: (i, 0))
    kvs = pl.BlockSpec((Mm, 2 * D_MODEL), lambda i: (0, 0))
    return pl.pallas_call(
        body, name=name, grid=(T // tm,),
        in_specs=[row, kvs, pl.BlockSpec((X_HEADS * Mm, tm), lambda i: (0, i)), row], out_specs=[row, kvs],
        out_shape=[jax.ShapeDtypeStruct((T, D_MODEL), BF16), jax.ShapeDtypeStruct((Mm, 2 * D_MODEL), F32)],
        compiler_params=_params(("arbitrary",)),
    )(q, kvm, probs, do)


def _swiglu_bwd_weights(tag, n, PG, PU, A, wd, dhb):
    T = n.shape[0]
    dG, dU = _swiglu_bwd_act(dhb, wd, PG, PU, name=f"{tag}_bwd_act", tm=1024)
    dwd = _matmul([(A, dhb)], M=D_FF, N=D_MODEL, K=T, tm=1408, tn=1024, tk=2048, a_t=True, out_dtype=BF16,
                  scale=0.5, name=f"{tag}_dwd")
    dwg = _matmul([(n, dG)], M=D_MODEL, N=D_FF, K=T, tm=512, tn=D_FF // 2, tk=2048, a_t=True, out_kind="s",
                  out_dtype=BF16, order="ji", name=f"{tag}_dwg")
    dwu = _matmul([(n, dU)], M=D_MODEL, N=D_FF, K=T, tm=512, tn=D_FF // 2, tk=2048, a_t=True, out_kind="s",
                  out_dtype=BF16, order="ji", name=f"{tag}_dwu")
    return dG, dU, dwg, dwu, dwd


def _swiglu_bwd_input(tag, hin, g_norm, dG, dU, wg, wu, dh, dep):
    T = hin.shape[0]
    dn = _matmul([(dG, wg), (dU, wu)], M=T, N=D_MODEL, K=D_FF, tm=512, tn=D_MODEL // 2, tk=D_FF // 2, b_kind="t",
                 out_dtype=BF16, dep=dep, name=f"{tag}_dn")
    return _rms_bwd(hin, g_norm, dn, dh, name=f"{tag}_norm_bwd")


GROUP_FFN1 = ["w1_gate", "w1_up", "w1_down"]
GROUP_MID = ["w_in", "w_out", "w_xq", "w_xkv", "w_xo"]
GROUP_FFN2 = ["w2_gate", "w2_up", "w2_down"]


def _local_step(x, mem, tgt, W, comm):
    T = x.shape[0]
    Mm = mem.shape[0]
    mm = functools.partial(_matmul)

    W = {**W, **comm.gather_now("ffn1_up", ["w1_gate", "w1_up"])}
    tok = comm.gather_start("ffn1_down", ["w1_down"], after=W["w1_up"])
    tok = comm.gather_start("mid", GROUP_MID, after=tok)
    tok = comm.gather_start("ffn2", GROUP_FFN2, after=tok)
    n1 = _rms_fwd(x, W["g_ffn1"], dep=tok, name="f_norm1")
    PG1, PU1, A1 = _swiglu_up(n1, W["w1_gate"], W["w1_up"], name="f_ffn1_up")
    tok = comm.gather_landed("ffn1_down", after=A1)
    tok = comm.gather_landed("mid", after=tok)
    W = {**W, **comm.gather_ready("ffn1_down", after=tok)}
    h1 = mm([(A1, W["w1_down"])], M=T, N=D_MODEL, K=D_FF, tm=512, tn=D_MODEL // 2, tk=D_FF, scale=0.5, res=x,
            order="ji", name="f_ffn1_down")
    n2 = _rms_fwd(h1, W["g_mix"], name="f_norm2")
    W = {**W, **comm.gather_ready("mid", after=n2)}
    z = mm([(n2, W["w_in"])], M=T, N=IN_COLS, K=D_MODEL, tm=512, tn=IN_COLS // 2, tk=D_MODEL, name="f_w_in")
    bst = jnp.transpose(W["b_s"])
    yn, probs = _mixer_fwd(z, W["g_v"], W["w_s"], bst, W["sinks"], W["g_a_out"], W["g_b_out"], name="f_mixer")
    tok = comm.gather_landed("ffn2", after=yn)
    h2, n3 = mm([(yn, W["w_out"])], M=T, N=D_MODEL, K=D_MODEL, tm=512, tn=D_MODEL, tk=D_MODEL, res=h1,
                norm_g=W["g_x"], dep=tok, name="f_w_out")
    memn = _rms_fwd(mem, W["g_mem"], name="f_norm_mem")
    q3 = mm([(n3, W["w_xq"])], M=T, N=D_MODEL, K=D_MODEL, tm=1024, tn=D_MODEL, tk=D_MODEL, out_dtype=BF16,
            name="f_w_xq")
    kvm = mm([(memn, W["w_xkv"])], M=Mm, N=2 * D_MODEL, K=D_MODEL, tm=Mm, tn=1024, tk=D_MODEL, b_kind="n",
             name="f_w_xkv")
    o3, xprobs = _xattn_fwd(q3, kvm, name="f_xattn")
    h3, n4 = mm([(o3, W["w_xo"])], M=T, N=D_MODEL, K=D_MODEL, tm=512, tn=D_MODEL, tk=D_MODEL, res=h2,
                norm_g=W["g_ffn2"], name="f_w_xo")
    W = {**W, **comm.gather_ready("ffn2", after=n4)}
    PG2, PU2, A2 = _swiglu_up(n4, W["w2_gate"], W["w2_up"], name="f_ffn2_up")
    h4 = mm([(A2, W["w2_down"])], M=T, N=D_MODEL, K=D_FF, tm=512, tn=D_MODEL // 2, tk=D_FF, scale=0.5, res=h3,
            order="ji", name="f_ffn2_down")

    grads = {}
    dh4, dh4b, grads["g_final"], loss = _loss_head(h4, W["g_final"], tgt, name="loss_head")
    dG2, dU2, dwg, dwu, dwd = _swiglu_bwd_weights("b_ffn2", n4, PG2, PU2, A2, W["w2_down"], dh4b)
    tok = comm.reduce_pair_start("ffn2", {"w2_gate": dwg, "w2_up": dwu, "w2_down": dwd})
    dh3, dh3b, grads["g_ffn2"] = _swiglu_bwd_input("b_ffn2", h3, W["g_ffn2"], dG2, dU2, W["w2_gate"], W["w2_up"],
                                                   dh4, tok)
    tok = comm.reduce_pair_done("ffn2", after=dh3b)

    mid = {}
    do3 = mm([(dh3b, W["w_xo"])], M=T, N=D_MODEL, K=D_MODEL, tm=512, tn=D_MODEL, tk=D_MODEL, b_kind="t",
             out_dtype=BF16, dep=tok, name="b_do3")
    mid["w_xo"] = mm([(o3, dh3b)], M=D_MODEL, N=D_MODEL, K=T, tm=1024, tn=D_MODEL // 2, tk=4096, a_t=True,
                       out_dtype=BF16, name="b_dw_xo")
    dq3, dkvm = _xattn_bwd(q3, kvm, xprobs, do3, name="b_xattn")
    mid["w_xq"] = mm([(n3, dq3)], M=D_MODEL, N=D_MODEL, K=T, tm=1024, tn=D_MODEL // 2, tk=4096, a_t=True,
                       out_dtype=BF16, name="b_dw_xq")
    dn3 = mm([(dq3, W["w_xq"])], M=T, N=D_MODEL, K=D_MODEL, tm=512, tn=D_MODEL, tk=D_MODEL, b_kind="t",
             out_dtype=BF16, name="b_dn3")
    dh2, dh2b, grads["g_x"] = _rms_bwd(h2, W["g_x"], dn3, dh3, name="b_norm3")
    dkvmb = dkvm.astype(BF16)
    mid["w_xkv"] = mm([(memn, dkvmb)], M=D_MODEL, N=2 * D_MODEL, K=Mm, tm=D_MODEL, tn=1024, tk=Mm, a_t=True,
                        out_kind="s", out_dtype=BF16, name="b_dw_xkv")
    dmemn = mm([(dkvmb, W["w_xkv"])], M=Mm, N=D_MODEL, K=2 * D_MODEL, tm=Mm, tn=D_MODEL, tk=1024, b_kind="t",
               name="b_dmemn")
    _, _, grads["g_mem"] = _rms_bwd(mem, W["g_mem"], dmemn, None, name="b_norm_mem")
    comm.reduce_finish("ffn2", after=dh2b)

    dyn = mm([(dh2b, W["w_out"])], M=T, N=D_MODEL, K=D_MODEL, tm=1024, tn=D_MODEL, tk=D_MODEL, b_kind="t",
             out_dtype=BF16, name="b_dyn")
    mid["w_out"] = mm([(yn, dh2b)], M=D_MODEL, N=D_MODEL, K=T, tm=1024, tn=D_MODEL // 2, tk=4096, a_t=True,
                        out_dtype=BF16, name="b_dw_out")
    dz, grads["g_v"], grads["w_s"], dbst, grads["sinks"], grads["g_a_out"], grads["g_b_out"] = _mixer_bwd(
        z, dyn, probs, W["g_v"], W["w_s"], bst, W["g_a_out"], W["g_b_out"], name="b_mixer")
    grads["b_s"] = jnp.transpose(dbst)
    mid["w_in"] = mm([(n2, dz)], M=D_MODEL, N=IN_COLS, K=T, tm=1024, tn=IN_COLS, tk=1024, a_t=True,
                     out_dtype=BF16, name="b_dw_in")
    tok = comm.reduce_pair_start("mid", mid)
    dn2 = mm([(dz, W["w_in"])], M=T, N=D_MODEL, K=IN_COLS, tm=512, tn=D_MODEL, tk=IN_COLS, b_kind="t",
             out_dtype=BF16, dep=tok, name="b_dn2")
    tok = comm.reduce_pair_done("mid", after=dn2)
    dh1, dh1b, grads["g_mix"] = _rms_bwd(h1, W["g_mix"], dn2, dh2, dep=tok, name="b_norm2")

    dG1, dU1, dwg, dwu, dwd = _swiglu_bwd_weights("b_ffn1", n1, PG1, PU1, A1, W["w1_down"], dh1b)
    comm.reduce_finish("mid", after=dwu)
    tok = comm.reduce_start("ffn1", {"w1_gate": dwg, "w1_up": dwu, "w1_down": dwd})
    dx, _, grads["g_ffn1"] = _swiglu_bwd_input("b_ffn1", x, W["g_ffn1"], dG1, dU1, W["w1_gate"], W["w1_up"], dh1, tok)
    comm.reduce_finish("ffn1", after=dx)
    return loss, dx, grads


BIG = ["w1_gate", "w1_up", "w1_down", "w_in", "w_out", "w_xq", "w_xkv", "w_xo", "w2_gate", "w2_up", "w2_down"]
SMALL = ["g_ffn1", "g_mix", "g_v", "w_s", "b_s", "sinks", "g_a_out", "g_b_out", "g_x", "g_mem", "g_ffn2", "g_final"]
ALL_W = ["g_ffn1", "w1_gate", "w1_up", "w1_down", "g_mix", "w_in", "g_v", "w_s", "b_s", "sinks", "g_a_out",
         "g_b_out", "w_out", "g_x", "g_mem", "w_xq", "w_xkv", "w_xo", "g_ffn2", "w2_gate", "w2_up", "w2_down",
         "g_final"]
ANY = pl.BlockSpec(memory_space=pl.ANY)


def _place():
    x, y, c = lax.axis_index("x"), lax.axis_index("y"), lax.axis_index("c")
    chips = [(1 - x, y), (x, 1 - y), (1 - x, 1 - y)]
    return x, y, c, chips


COL_SHARDED = ("w1_gate", "w1_up", "w2_gate", "w2_up", "w_xkv")


def _gathered_shape(shape, by_cols):
    rows, cols = shape
    return (rows, N_CHIPS * cols) if by_cols else (N_CHIPS, rows, cols)


def _owner_rows(ref, shape, by_cols, slot, r0, rows):
    cols = shape[1]
    if by_cols:
        return ref.at[pl.ds(r0, rows), pl.ds(pl.multiple_of(slot * cols, LANES), cols)]
    return ref.at[slot, pl.ds(r0, rows), :]


def _allgather_weights(shards, by_cols, *, name):
    n = len(shards)

    def body(*refs):
        ins, outs = refs[:n], refs[n:2 * n]
        send, recv, loc = refs[2 * n:]
        x, y, c, chips = _place()
        me = 2 * x + y
        sib = (x, y, 1 - c)

        def half(w, slot, hc):
            h = shards[w].shape[0] // 2
            return _owner_rows(outs[w], shards[w].shape, by_cols[w], slot, hc * h, h)

        def copy(w, k, slot, hc, to, src=None):
            return pltpu.make_async_remote_copy(
                src_ref=half(w, slot, hc) if src is None else src, dst_ref=half(w, slot, hc),
                send_sem=send.at[6 * w + k], recv_sem=recv.at[6 * w + k], device_id=to, device_id_type=MESH)

        own = [pltpu.make_async_remote_copy(
            src_ref=ins[w], dst_ref=_owner_rows(outs[w], shards[w].shape, by_cols[w], me, 0, shards[w].shape[0]),
            send_sem=loc.at[w], recv_sem=loc.at[n + w], device_id=sib, device_id_type=MESH) for w in range(n)]
        for cp in own:
            cp.start()
        first = []
        for w in range(n):
            h = shards[w].shape[0] // 2
            for j, (tx, ty) in enumerate(chips):
                first.append(copy(w, j, me, c, (tx, ty, c), src=ins[w].at[pl.ds(c * h, h), :]))
                first[-1].start()
        passed = []
        for w in range(n):
            for j, (tx, ty) in enumerate(chips):
                slot = 2 * tx + ty
                copy(w, j, slot, c, (tx, ty, c)).wait_recv()
                passed.append(copy(w, 3 + j, slot, c, sib))
                passed[-1].start()
        for w in range(n):
            for j, (tx, ty) in enumerate(chips):
                copy(w, 3 + j, 2 * tx + ty, 1 - c, sib).wait_recv()
        for cp in first + passed:
            cp.wait_send()
        for cp in own:
            cp.wait()

    return pl.pallas_call(
        body, name=name, in_specs=[ANY] * n, out_specs=[ANY] * n,
        out_shape=[jax.ShapeDtypeStruct(_gathered_shape(s.shape, bc), s.dtype) for s, bc in zip(shards, by_cols)],
        scratch_shapes=[pltpu.SemaphoreType.DMA((6 * n,)), pltpu.SemaphoreType.DMA((6 * n,)),
                        pltpu.SemaphoreType.DMA((2 * n,))],
    )(*shards)


def _pair_exchange(grads, *, name):
    n = len(grads)

    def body(*refs):
        ins, outs = refs[:n], refs[n:2 * n]
        send, recv = refs[2 * n:]
        x, y, c, _ = _place()
        cps = []
        for w in range(n):
            h = grads[w].shape[1] // 2
            cps.append(pltpu.make_async_remote_copy(
                src_ref=ins[w].at[:, pl.ds((1 - c) * h, h), :], dst_ref=outs[w],
                send_sem=send.at[w], recv_sem=recv.at[w], device_id=(x, y, 1 - c), device_id_type=MESH))
            cps[-1].start()
        for cp in cps:
            cp.wait()

    return pl.pallas_call(
        body, name=name, in_specs=[ANY] * n, out_specs=[ANY] * n,
        out_shape=[jax.ShapeDtypeStruct((N_CHIPS, g.shape[1] // 2, g.shape[2]), g.dtype) for g in grads],
        scratch_shapes=[pltpu.SemaphoreType.DMA((n,)), pltpu.SemaphoreType.DMA((n,))],
    )(*grads)


def _pair_sum(g, got, *, name):
    S, R, C = g.shape
    h = R // 2
    tr = _row_block(h, 3 * C * 2, 16)
    nr = h // tr

    def body(a_ref, b_ref, o_ref):
        o_ref[...] = (a_ref[...].astype(F32) + b_ref[...].astype(F32)).astype(BF16)

    return pl.pallas_call(
        body, name=name, grid=(S, nr),
        in_specs=[pl.BlockSpec((None, tr, C), lambda s, r: (s, lax.axis_index("c") * nr + r, 0)),
                  pl.BlockSpec((None, tr, C), lambda s, r: (s, r, 0))],
        out_specs=pl.BlockSpec((None, tr, C), lambda s, r: (s, r, 0)),
        out_shape=jax.ShapeDtypeStruct((S, h, C), BF16),
        compiler_params=_params(("parallel", "parallel")),
    )(g, got)


def _chip_sum(part, got, *, name):
    S, h, C = part.shape
    tr = _row_block(h, 4 * C * 2 + C * 4, 16)
    nr = h // tr

    def body(own_ref, g0_ref, g1_ref, g2_ref, o_ref):
        acc = own_ref[...].astype(F32) + g0_ref[...].astype(F32)
        o_ref[...] = (acc + g1_ref[...].astype(F32)) + g2_ref[...].astype(F32)

    def piece(j):
        return pl.BlockSpec((None, tr, C), lambda r: (j, r, 0))

    return pl.pallas_call(
        body, name=name, grid=(nr,),
        in_specs=[pl.BlockSpec((None, tr, C), lambda r: (2 * lax.axis_index("x") + lax.axis_index("y"), r, 0)),
                  piece(0), piece(1), piece(2)],
        out_specs=pl.BlockSpec((tr, C), lambda r: (lax.axis_index("c") * nr + r, 0)),
        out_shape=jax.ShapeDtypeStruct((2 * h, C), F32),
        compiler_params=_params(("parallel",)),
    )(part, got, got, got)


def _pair_gather(totals, *, name):
    n = len(totals)

    def body(*refs):
        ins, outs = refs[:n], refs[n:2 * n]
        send, recv = refs[2 * n:]
        x, y, c, _ = _place()
        cps = []
        for w in range(n):
            h = totals[w].shape[0] // 2
            cps.append(pltpu.make_async_remote_copy(
                src_ref=ins[w].at[pl.ds(c * h, h), :], dst_ref=outs[w].at[pl.ds(c * h, h), :],
                send_sem=send.at[w], recv_sem=recv.at[w], device_id=(x, y, 1 - c), device_id_type=MESH))
            cps[-1].start()
        for w in range(n):
            h = totals[w].shape[0] // 2
            theirs = outs[w].at[pl.ds((1 - c) * h, h), :]
            pltpu.make_async_remote_copy(
                src_ref=theirs, dst_ref=theirs, send_sem=send.at[w], recv_sem=recv.at[w],
                device_id=(x, y, 1 - c), device_id_type=MESH).wait_recv()
        for cp in cps:
            cp.wait_send()

    return pl.pallas_call(
        body, name=name, in_specs=[ANY] * n, out_specs=[ANY] * n,
        out_shape=[jax.ShapeDtypeStruct(t.shape, t.dtype) for t in totals],
        input_output_aliases={w: w for w in range(n)},
        scratch_shapes=[pltpu.SemaphoreType.DMA((n,)), pltpu.SemaphoreType.DMA((n,))],
    )(*totals)


def _allreduce_small(v, *, name):
    R, C = v.shape
    ND = 8

    def body(v_ref, o_ref, all_ref, send, recv, loc):
        x, y, c, chips = _place()
        me, sib = (x, y, c), (x, y, 1 - c)

        def rows(px, py, pc):
            return all_ref.at[pl.ds((4 * px + 2 * py + pc) * R, R), :]

        def copy(k, block, to, src=None):
            return pltpu.make_async_remote_copy(
                src_ref=rows(*block) if src is None else src, dst_ref=rows(*block),
                send_sem=send.at[k], recv_sem=recv.at[k], device_id=to, device_id_type=MESH)

        mine = pltpu.make_async_copy(v_ref, rows(*me), loc)
        mine.start()
        first = [copy(0, me, sib, src=v_ref)]
        first += [copy(1 + j, me, (*chip, c), src=v_ref) for j, chip in enumerate(chips)]
        for cp in first:
            cp.start()
        passed = [copy(4 + j, (*chip, c), sib) for j, chip in enumerate(chips)]
        for j, chip in enumerate(chips):
            copy(1 + j, (*chip, c), me).wait_recv()
            passed[j].start()
        copy(0, sib, me).wait_recv()
        for j, chip in enumerate(chips):
            copy(4 + j, (*chip, 1 - c), me).wait_recv()
        for cp in first + passed:
            cp.wait_send()
        mine.wait()
        acc = all_ref[0:R, :]
        for d in range(1, ND):
            acc = acc + all_ref[d * R:(d + 1) * R, :]
        o_ref[...] = acc

    vm = pl.BlockSpec(memory_space=pltpu.VMEM)
    return pl.pallas_call(
        body, name=name, in_specs=[vm], out_specs=[vm, vm],
        out_shape=[jax.ShapeDtypeStruct((R, C), F32), jax.ShapeDtypeStruct((ND * R, C), F32)],
        scratch_shapes=[pltpu.SemaphoreType.DMA((7,)), pltpu.SemaphoreType.DMA((7,)), pltpu.SemaphoreType.DMA],
        compiler_params=pltpu.CompilerParams(vmem_limit_bytes=VMEM_LIMIT),
    )(v)[0]


HBM = pl.BlockSpec(memory_space=pltpu.HBM)
SEM = pl.BlockSpec(memory_space=pltpu.SEMAPHORE)
EFFECT = pltpu.SideEffectType.DATAFLOW_SIDE_EFFECTING


def _remote(src, dst, send, recv, k, to):
    return pltpu.make_async_remote_copy(src_ref=src, dst_ref=dst, send_sem=send.at[k], recv_sem=recv.at[k],
                                        device_id=to, device_id_type=MESH)


def _split_start(bufs, plan, ncopies, *, name, after=None):
    nb = len(bufs)
    extra = [] if after is None else [after]

    def body(*refs):
        pos = nb + len(extra)
        send, recv, token = refs[pos], refs[pos + 1], refs[-1]
        for k, (src, dst, to) in enumerate(plan(refs[:nb])):
            _remote(src, dst, send, recv, k, to).start()
        token[...] = jnp.zeros_like(token)

    outs = pl.pallas_call(
        body, name=name,
        out_shape=(pltpu.SemaphoreType.DMA((ncopies,)), pltpu.SemaphoreType.DMA((ncopies,)),
                   *[pltpu.HBM(b.shape, b.dtype) for b in bufs], jax.ShapeDtypeStruct((SUBLANES, LANES), F32)),
        in_specs=[HBM] * nb + [ANY] * len(extra),
        out_specs=(SEM, SEM, *[HBM] * nb, pl.BlockSpec(memory_space=pltpu.VMEM)),
        input_output_aliases={i: 2 + i for i in range(nb)},
        compiler_params=pltpu.CompilerParams(has_side_effects=EFFECT),
    )(*[pltpu.with_memory_space_constraint(b, pltpu.HBM) for b in bufs], *extra)
    return outs[0], outs[1], list(outs[2:2 + nb]), outs[-1]


def _split_wait(started, plan, after, *, name):
    send, recv, bufs, _ = started
    nb = len(bufs)

    def body(*refs):
        send_sem, recv_sem = refs[nb], refs[nb + 1]
        for k, (src, dst, to) in enumerate(plan(refs[:nb])):
            cp = _remote(src, dst, send_sem, recv_sem, k, to)
            cp.wait_send()
            cp.wait_recv()

    outs = pl.pallas_call(
        body, name=name,
        out_shape=tuple(pltpu.HBM(b.shape, b.dtype) for b in bufs),
        in_specs=[HBM] * nb + [SEM, SEM, ANY], out_specs=tuple([HBM] * nb),
        input_output_aliases={i: i for i in range(nb)},
        compiler_params=pltpu.CompilerParams(has_side_effects=EFFECT),
    )(*bufs, send, recv, after)
    return list(outs)


def _gather_chip_plan(shapes, by_cols):
    n = len(shapes)

    def plan(refs):
        srcs, lands = refs[:n], refs[n:]
        x, y, c, chips = _place()
        out = []
        for w in range(n):
            h = shapes[w][0] // 2
            mine = _owner_rows(lands[w], shapes[w], by_cols[w], 2 * x + y, c * h, h)
            for tx, ty in chips:
                out.append((srcs[w].at[pl.ds(c * h, h), :], mine, (tx, ty, c)))
        return out

    return plan


def _gather_pair_plan(shapes, by_cols):
    n = len(shapes)

    def plan(refs):
        srcs, lands = refs[:n], refs[n:]
        x, y, c, chips = _place()
        out = []
        for w in range(n):
            h = shapes[w][0] // 2
            for tx, ty in chips:
                half = _owner_rows(lands[w], shapes[w], by_cols[w], 2 * tx + ty, c * h, h)
                out.append((half, half, (x, y, 1 - c)))
            own = _owner_rows(lands[w], shapes[w], by_cols[w], 2 * x + y, 0, shapes[w][0])
            out.append((srcs[w], own, (x, y, 1 - c)))
        return out

    return plan


def _reduce_pair_plan(shapes):
    n = len(shapes)

    def plan(refs):
        local, lands = refs[:n], refs[n:]
        x, y, c, _ = _place()
        out = []
        for w in range(n):
            h = shapes[w][1] // 2
            out.append((local[w].at[:, pl.ds((1 - c) * h, h), :], lands[w], (x, y, 1 - c)))
        return out

    return plan


def _reduce_chip_plan(n):
    def plan(refs):
        parts, lands = refs[:n], refs[n:]
        x, y, c, chips = _place()
        return [(parts[w].at[2 * tx + ty], lands[w].at[j], (tx, ty, c))
                for w in range(n) for j, (tx, ty) in enumerate(chips)]

    return plan


def _as_operands(gathered):
    out = {}
    for n, g in gathered.items():
        if n in COL_SHARDED:
            out[n] = g
        elif n == "w_in":
            out[n] = jnp.transpose(g, (1, 0, 2)).reshape(D_MODEL, IN_COLS)
        else:
            out[n] = g.reshape(g.shape[0] * g.shape[1], g.shape[2])
    return out


def _by_owner(n, g):
    if n == "w_in":
        return jnp.transpose(g.reshape(D_MODEL, N_CHIPS, IN_COLS // N_CHIPS), (1, 0, 2))
    if g.ndim == 2:
        return g.reshape(N_CHIPS, g.shape[0] // N_CHIPS, g.shape[1])
    return g


class _Comm:
    def __init__(self, shards):
        self.shards = shards
        self.total = {}
        self._flight = {}

    def _layout(self, names):
        return [self.shards[n].shape for n in names], [n in COL_SHARDED for n in names]

    def gather_now(self, tag, names):
        _, by_cols = self._layout(names)
        got = _allgather_weights([self.shards[n] for n in names], by_cols, name=f"gather_{tag}")
        return _as_operands(dict(zip(names, got)))

    def gather_start(self, tag, names, after):
        shapes, by_cols = self._layout(names)
        srcs = [self.shards[n] for n in names]
        lands = [lax.empty(_gathered_shape(s.shape, bc), s.dtype) for s, bc in zip(srcs, by_cols)]
        started = _split_start(srcs + lands, _gather_chip_plan(shapes, by_cols), 3 * len(srcs),
                               after=after, name=f"gather_{tag}_chips_start")
        self._flight[tag] = (names, started)
        return started[3]

    def gather_landed(self, tag, after):
        names, started = self._flight[tag]
        shapes, by_cols = self._layout(names)
        bufs = _split_wait(started, _gather_chip_plan(shapes, by_cols), after, name=f"gather_{tag}_chips_wait")
        started = _split_start(bufs, _gather_pair_plan(shapes, by_cols), 4 * len(names),
                               name=f"gather_{tag}_pair_start")
        self._flight[tag] = (names, started)
        return started[3]

    def gather_ready(self, tag, after):
        names, started = self._flight.pop(tag)
        shapes, by_cols = self._layout(names)
        bufs = _split_wait(started, _gather_pair_plan(shapes, by_cols), after, name=f"gather_{tag}_pair_wait")
        return _as_operands(dict(zip(names, bufs[len(names):])))

    def reduce_start(self, tag, grads):
        names = list(grads)
        local = [_by_owner(n, grads[n]) for n in names]
        return self._chip_start(tag, names, local, _pair_exchange(local, name=f"pair_exchange_{tag}"))

    def reduce_pair_start(self, tag, grads):
        names = list(grads)
        local = [_by_owner(n, grads[n]) for n in names]
        lands = [lax.empty((N_CHIPS, g.shape[1] // 2, g.shape[2]), g.dtype) for g in local]
        started = _split_start(local + lands, _reduce_pair_plan([g.shape for g in local]), len(names),
                               name=f"pair_exchange_{tag}_start")
        self._flight[tag] = (names, started)
        return started[3]

    def reduce_pair_done(self, tag, after):
        names, started = self._flight.pop(tag)
        n = len(names)
        bufs = _split_wait(started, _reduce_pair_plan([b.shape for b in started[2][:n]]), after,
                           name=f"pair_exchange_{tag}_wait")
        return self._chip_start(tag, names, bufs[:n], bufs[n:])

    def _chip_start(self, tag, names, local, from_sib):
        parts = [_pair_sum(g, s, name=f"pair_sum_{n}") for n, g, s in zip(names, local, from_sib)]
        lands = [lax.empty((N_CHIPS - 1,) + p.shape[1:], p.dtype) for p in parts]
        self._flight[tag] = (names, _split_start(parts + lands, _reduce_chip_plan(len(names)), 3 * len(names),
                                                 name=f"chip_exchange_{tag}_start"))
        return self._flight[tag][1][3]

    def reduce_finish(self, tag, after):
        names, started = self._flight.pop(tag)
        n = len(names)
        bufs = _split_wait(started, _reduce_chip_plan(n), after, name=f"chip_exchange_{tag}_wait")
        totals = [_chip_sum(p, s, name=f"chip_sum_{nm}") for nm, p, s in zip(names, bufs[:n], bufs[n:])]
        self.total.update(zip(names, _pair_gather(totals, name=f"pair_gather_{tag}")))


def _adamw(w, g, m, v, *, name):
    R, C = w.shape
    tr = _row_block(R, 8 * C * 4, SUBLANES)

    def body(w_ref, g_ref, m_ref, v_ref, go_ref, d_ref, nm_ref, nv_ref):
        gg = g_ref[...]
        go_ref[...] = gg
        m_new = ADAM_B1 * m_ref[...] + (1.0 - ADAM_B1) * gg
        v_new = ADAM_B2 * v_ref[...] + (1.0 - ADAM_B2) * (gg * gg)
        m_hat = m_new / (1.0 - ADAM_B1 ** ADAM_STEP)
        v_hat = v_new / (1.0 - ADAM_B2 ** ADAM_STEP)
        d_ref[...] = -ADAM_LR * (m_hat / (jnp.sqrt(v_hat) + ADAM_EPS) + ADAM_WD * w_ref[...])
        nm_ref[...] = m_new
        nv_ref[...] = v_new

    blk = pl.BlockSpec((tr, C), lambda i: (i, 0))
    shp = jax.ShapeDtypeStruct((R, C), F32)
    return pl.pallas_call(
        body, name=name, grid=(R // tr,), in_specs=[blk] * 4, out_specs=[blk] * 4, out_shape=[shp] * 4,
        compiler_params=_params(("parallel",)),
    )(w, g, m, v)


def _to2d(a):
    flat = a.reshape(-1)
    pad = (-flat.shape[0]) % (SUBLANES * LANES)
    if pad:
        flat = jnp.pad(flat, (0, pad))
    return flat.reshape(-1, LANES)


def _small_rows(shape):
    return -(-math.prod(shape) // (SUBLANES * LANES)) * SUBLANES


def _pack_small(parts):
    rows = jnp.concatenate([_to2d(p) for p in parts], axis=0)
    pad = (-rows.shape[0]) % 256
    if pad:
        rows = jnp.concatenate([rows, jnp.zeros((pad, LANES), rows.dtype)], axis=0)
    return rows


def _unpack_small(rows, shapes):
    out, r = [], 0
    for shp in shapes:
        size = math.prod(shp)
        nrow = _small_rows(shp)
        out.append(rows[r:r + nrow].reshape(-1)[:size].reshape(shp))
        r += nrow
    return out


def kernel(x, mem, g_ffn1, w1_gate, w1_up, w1_down, g_mix, w_in, g_v, w_s, b_s, sinks, g_a_out, g_b_out, w_out, g_x, g_mem, w_xq, w_xkv, w_xo, g_ffn2, w2_gate, w2_up, w2_down, g_final, loss_target, m_g_ffn1, m_w1_gate, m_w1_up, m_w1_down, m_g_mix, m_w_in, m_g_v, m_w_s, m_b_s, m_sinks, m_g_a_out, m_g_b_out, m_w_out, m_g_x, m_g_mem, m_w_xq, m_w_xkv, m_w_xo, m_g_ffn2, m_w2_gate, m_w2_up, m_w2_down, m_g_final, v_g_ffn1, v_w1_gate, v_w1_up, v_w1_down, v_g_mix, v_w_in, v_g_v, v_w_s, v_b_s, v_sinks, v_g_a_out, v_g_b_out, v_w_out, v_g_x, v_g_mem, v_w_xq, v_w_xkv, v_w_xo, v_g_ffn2, v_w2_gate, v_w2_up, v_w2_down, v_g_final):
    args = dict(locals())
    Wp = {n: args[n] for n in ALL_W}
    Mp = {n: args["m_" + n] for n in ALL_W}
    Vp = {n: args["v_" + n] for n in ALL_W}

    comm = _Comm({n: Wp[n][0].astype(BF16) for n in BIG})
    W = {n: Wp[n] for n in SMALL}
    W["g_final"] = Wp["g_final"].reshape(1, D_MODEL)
    for n in ("w_s", "b_s"):
        W[n] = Wp[n][0]
    loss, dx, grads = _local_step(x[0], mem[0], loss_target[0], W, comm)
    big_grad = comm.total

    small_shapes = [Wp[n].shape for n in SMALL]
    packed = _pack_small([grads[n].reshape(Wp[n].shape) for n in SMALL] + [loss])
    summed = _allreduce_small(packed, name="allreduce_small")
    small_grad = dict(zip(SMALL, _unpack_small(summed, small_shapes)))
    nrows = sum(_small_rows(s) for s in small_shapes)
    loss_total = summed[nrows, 0]

    grad_out, delta, new_m, new_v = {}, {}, {}, {}
    for n in BIG:
        shp = Wp[n].shape
        g, d, nm, nv = _adamw(Wp[n][0], big_grad[n], Mp[n][0], Vp[n][0], name=f"adamw_{n}")
        grad_out[n], delta[n], new_m[n], new_v[n] = g.reshape(shp), d.reshape(shp), nm.reshape(shp), nv.reshape(shp)
    sw = _pack_small([Wp[n] for n in SMALL])
    sg = _pack_small([small_grad[n] for n in SMALL])
    sm = _pack_small([Mp[n] for n in SMALL])
    sv = _pack_small([Vp[n] for n in SMALL])
    _, d, nm, nv = _adamw(sw, sg, sm, sv, name="adamw_small")
    for n, dd, mm_, vv_ in zip(SMALL, _unpack_small(d, small_shapes), _unpack_small(nm, small_shapes),
                               _unpack_small(nv, small_shapes)):
        grad_out[n], delta[n], new_m[n], new_v[n] = small_grad[n], dd, mm_, vv_

    return (loss_total, dx[None], *[grad_out[n] for n in ALL_W], *[delta[n] for n in ALL_W],
            *[new_m[n] for n in ALL_W], *[new_v[n] for n in ALL_W])
```

```python
import functools
import math

import jax
import jax.numpy as jnp
from jax import lax
from jax.experimental import pallas as pl
from jax.experimental.pallas import tpu as pltpu

F32 = jnp.float32
BF16 = jnp.bfloat16
MESH = pl.DeviceIdType.MESH

D_MODEL = 2048
D_FF = 5632
D_A = 1024
D_B = 1024
CHUNK = 128
A_GROUPS = 8
HEAD_DIM = 64
B_Q_HEADS = 16
B_KV_HEADS = 2
X_HEADS = 4
X_HEAD_DIM = 512
IN_COLS = 3328
O_Q = 2 * D_A
O_K = O_Q + D_B
O_V = O_K + B_KV_HEADS * HEAD_DIM
N_CHIPS = 4
EPS = 1e-5
NEG = -1e30
ADAM_LR = 0.001
ADAM_B1 = 0.9
ADAM_B2 = 0.999
ADAM_EPS = 1e-08
ADAM_WD = 0.01
ADAM_STEP = 10

V7X_VMEM_BYTES = 64 * 1024 * 1024
VMEM_LIMIT = 56 * 1024 * 1024
LANES = 128
SUBLANES = 8


ANY = pl.BlockSpec(memory_space=pl.ANY)


def _params(sem, vmem=VMEM_LIMIT):
    return pltpu.CompilerParams(dimension_semantics=sem, vmem_limit_bytes=vmem)


def _matmul(pairs, *, M, N, K, tm, tn, tk, a_t=False, b_kind="n", out_kind="n", out_dtype=F32,
            scale=1.0, res=None, norm_g=None, order="ij", dep=None, name):
    tm, tn, tk = min(tm, M), min(tn, N), min(tk, K)
    assert M % tm == 0 and N % tn == 0 and K % tk == 0, (name, M, N, K, tm, tn, tk)
    nk = K // tk
    npairs = len(pairs)
    b_t = b_kind == "t"
    ns = N // N_CHIPS

    def ij(g0, g1):
        return (g0, g1) if order == "ij" else (g1, g0)

    def a_map(g0, g1, k):
        i, _ = ij(g0, g1)
        return (k, i) if a_t else (i, k)

    a_spec = pl.BlockSpec((tk, tm) if a_t else (tm, tk), a_map)

    if b_kind == "n":
        b_spec = pl.BlockSpec((tk, tn), lambda g0, g1, k: (k, ij(g0, g1)[1]))
    else:
        b_spec = pl.BlockSpec((tn, tk), lambda g0, g1, k: (ij(g0, g1)[1], k))

    if out_kind == "n":
        o_spec = pl.BlockSpec((tm, tn), lambda g0, g1, k: ij(g0, g1))
        o_shape = jax.ShapeDtypeStruct((M, N), out_dtype)
    else:
        assert tn % ns == 0
        o_spec = pl.BlockSpec((tn // ns, tm, ns), lambda g0, g1, k: (ij(g0, g1)[1], ij(g0, g1)[0], 0))
        o_shape = jax.ShapeDtypeStruct((N_CHIPS, M, ns), out_dtype)

    in_specs, args = [], []
    for a, b in pairs:
        in_specs += [a_spec, b_spec]
        args += [a, b]
    if res is not None:
        in_specs.append(pl.BlockSpec((tm, tn), lambda g0, g1, k: ij(g0, g1)))
        args.append(res)
    if norm_g is not None:
        assert tn == N and out_kind == "n"
        in_specs.append(pl.BlockSpec((1, N), lambda g0, g1, k: (0, 0)))
        args.append(norm_g)
    if dep is not None:
        in_specs.append(ANY)
        args.append(dep)

    dn = (((0,) if a_t else (1,), (1,) if b_t else (0,)), ((), ()))

    def body(*refs):
        pos = 2 * npairs
        res_ref = refs[pos] if res is not None else None
        pos += res is not None
        g_ref = refs[pos] if norm_g is not None else None
        pos += (norm_g is not None) + (dep is not None)
        o_ref = refs[pos]
        n_ref = refs[pos + 1] if norm_g is not None else None
        acc_ref = refs[-1] if nk > 1 else None
        part = None
        for p in range(npairs):
            d = lax.dot_general(refs[2 * p][...], refs[2 * p + 1][...], dn, preferred_element_type=F32)
            part = d if part is None else part + d

        def finish(acc):
            r = acc * scale if scale != 1.0 else acc
            if res_ref is not None:
                r = res_ref[...] + r
            if out_kind == "n":
                o_ref[...] = r.astype(out_dtype)
            else:
                for s in range(tn // ns):
                    o_ref[s] = r[:, s * ns:(s + 1) * ns].astype(out_dtype)
            if n_ref is not None:
                n_ref[...] = (r * _rstd(r) * g_ref[...]).astype(BF16)

        if nk == 1:
            finish(part)
        else:
            k = pl.program_id(2)

            @pl.when(k == 0)
            def _():
                acc_ref[...] = part

            @pl.when((k > 0) & (k < nk - 1))
            def _():
                acc_ref[...] += part

            @pl.when(k == nk - 1)
            def _():
                finish(acc_ref[...] + part)

    grid = (M // tm, N // tn, nk) if order == "ij" else (N // tn, M // tm, nk)
    out_specs, out_shape = o_spec, o_shape
    if norm_g is not None:
        out_specs = [o_spec, pl.BlockSpec((tm, tn), lambda g0, g1, k: ij(g0, g1))]
        out_shape = [o_shape, jax.ShapeDtypeStruct((M, N), BF16)]
    return pl.pallas_call(
        body, name=name, grid=grid, in_specs=in_specs, out_specs=out_specs, out_shape=out_shape,
        scratch_shapes=[pltpu.VMEM((tm, tn), F32)] if nk > 1 else [],
        compiler_params=_params(("parallel", "parallel", "arbitrary")),
    )(*args)


def _rstd(x):
    return lax.rsqrt(jnp.mean(x * x, axis=-1, keepdims=True) + EPS)


def _rms_bwd_math(x, g, dy):
    r = _rstd(x)
    gy = dy * g
    xr = x * r
    dx = r * (gy - xr * jnp.mean(gy * xr, axis=-1, keepdims=True))
    return dx, dy * xr


def _rms_fwd(h, g, *, name, tm=512, dep=None):
    T, Dm = h.shape
    tm = min(tm, T)

    def body(h_ref, g_ref, *rest):
        x = h_ref[...]
        rest[-1][...] = (x * _rstd(x) * g_ref[...]).astype(BF16)

    return pl.pallas_call(
        body, name=name, grid=(T // tm,),
        in_specs=[pl.BlockSpec((tm, Dm), lambda i: (i, 0)), pl.BlockSpec((1, Dm), lambda i: (0, 0))]
        + ([ANY] if dep is not None else []),
        out_specs=pl.BlockSpec((tm, Dm), lambda i: (i, 0)),
        out_shape=jax.ShapeDtypeStruct((T, Dm), BF16),
        compiler_params=_params(("parallel",)),
    )(h, g, *([dep] if dep is not None else []))


def _rms_bwd(h, g, dn, dres, *, name, tm=512, dep=None):
    T, Dm = h.shape
    tm = min(tm, T)
    has_res = dres is not None

    def body(*refs):
        h_ref, g_ref, dn_ref = refs[:3]
        pos = 3
        dres_ref = refs[pos] if has_res else None
        pos += has_res + (dep is not None)
        dh_ref, dhb_ref, dg_ref = refs[pos:pos + 3]
        dx, dgr = _rms_bwd_math(h_ref[...], g_ref[...], dn_ref[...].astype(F32))
        if has_res:
            dx = dres_ref[...] + dx
        dh_ref[...] = dx
        dhb_ref[...] = dx.astype(BF16)
        part = jnp.sum(dgr, axis=0, keepdims=True)

        @pl.when(pl.program_id(0) == 0)
        def _():
            dg_ref[...] = part

        @pl.when(pl.program_id(0) > 0)
        def _():
            dg_ref[...] += part

    row = pl.BlockSpec((tm, Dm), lambda i: (i, 0))
    vec = pl.BlockSpec((1, Dm), lambda i: (0, 0))
    args = [h, g, dn] + ([dres] if has_res else []) + ([dep] if dep is not None else [])
    return pl.pallas_call(
        body, name=name, grid=(T // tm,),
        in_specs=[row, vec, row] + ([row] if has_res else []) + ([ANY] if dep is not None else []),
        out_specs=[row, row, vec],
        out_shape=[jax.ShapeDtypeStruct((T, Dm), F32), jax.ShapeDtypeStruct((T, Dm), BF16),
                   jax.ShapeDtypeStruct((1, Dm), F32)],
        compiler_params=_params(("arbitrary",)),
    )(*args)


def _loss_head(h, g, tgt, *, name, tm=512):
    T, Dm = h.shape
    tm = min(tm, T)

    def body(h_ref, g_ref, t_ref, dh_ref, dhb_ref, dg_ref, loss_ref):
        x = h_ref[...]
        gv = g_ref[...]
        r = _rstd(x)
        diff = x * r * gv - t_ref[...]
        lpart = 0.5 * jnp.sum(jnp.mean(diff * diff, axis=-1, keepdims=True), axis=0, keepdims=True)
        dx, dgr = _rms_bwd_math(x, gv, diff * (1.0 / Dm))
        dh_ref[...] = dx
        dhb_ref[...] = dx.astype(BF16)
        part = jnp.sum(dgr, axis=0, keepdims=True)
        lrow = jnp.broadcast_to(lpart, (1, LANES))

        @pl.when(pl.program_id(0) == 0)
        def _():
            dg_ref[...] = part
            loss_ref[...] = lrow

        @pl.when(pl.program_id(0) > 0)
        def _():
            dg_ref[...] += part
            loss_ref[...] += lrow

    row = pl.BlockSpec((tm, Dm), lambda i: (i, 0))
    vec = pl.BlockSpec((1, Dm), lambda i: (0, 0))
    return pl.pallas_call(
        body, name=name, grid=(T // tm,),
        in_specs=[row, vec, row],
        out_specs=[row, row, vec, pl.BlockSpec((1, LANES), lambda i: (0, 0))],
        out_shape=[jax.ShapeDtypeStruct((T, Dm), F32), jax.ShapeDtypeStruct((T, Dm), BF16),
                   jax.ShapeDtypeStruct((1, Dm), F32), jax.ShapeDtypeStruct((1, LANES), F32)],
        compiler_params=_params(("arbitrary",)),
    )(h, g, tgt)


MXU_COLS = 256
FF_TILE = 2 * MXU_COLS


def _row_block(rows, row_bytes, align, budget=24 * 1024 * 1024):
    fits = [d for d in range(align, rows + 1, align) if rows % d == 0 and 2 * d * row_bytes <= budget]
    assert fits, (rows, row_bytes)
    return fits[-1]


def _swiglu_up(n, wg, wu, *, name, tm=1024, tn=FF_TILE):
    T, Dm = n.shape
    Fd = wg.shape[1]
    tm = min(tm, T)

    def body(n_ref, wg_ref, wu_ref, pg_ref, pu_ref, a_ref):
        x = n_ref[...]
        g = jnp.dot(x, wg_ref[...], preferred_element_type=F32)
        u = jnp.dot(x, wu_ref[...], preferred_element_type=F32)
        sg = jax.nn.sigmoid(g)
        silu = g * sg
        pg_ref[...] = ((sg + silu * (1.0 - sg)) * u).astype(BF16)
        pu_ref[...] = silu.astype(BF16)
        a_ref[...] = (silu * u).astype(BF16)

    wspec = pl.BlockSpec((Dm, tn), lambda j, i: (0, j))
    ospec = pl.BlockSpec((tm, tn), lambda j, i: (i, j))
    oshape = jax.ShapeDtypeStruct((T, Fd), BF16)
    return pl.pallas_call(
        body, name=name, grid=(Fd // tn, T // tm),
        in_specs=[pl.BlockSpec((tm, Dm), lambda j, i: (i, 0)), wspec, wspec],
        out_specs=[ospec, ospec, ospec], out_shape=[oshape, oshape, oshape],
        compiler_params=_params(("parallel", "parallel")),
    )(n, wg, wu)


def _swiglu_bwd_act(dhb, wd, PG, PU, *, name, tm=1024, tn=D_FF // N_CHIPS):
    T, Dm = dhb.shape
    Fd = wd.shape[0]
    tm, tn = min(tm, T), min(tn, Fd)

    def body(dh_ref, wd_ref, pg_ref, pu_ref, dg_ref, du_ref):
        da = 0.5 * lax.dot_general(dh_ref[...], wd_ref[...], (((1,), (1,)), ((), ())), preferred_element_type=F32)
        dg_ref[...] = (da * pg_ref[...].astype(F32)).astype(BF16)
        du_ref[...] = (da * pu_ref[...].astype(F32)).astype(BF16)

    blk = pl.BlockSpec((tm, tn), lambda j, i: (i, j))
    oshape = jax.ShapeDtypeStruct((T, Fd), BF16)
    return pl.pallas_call(
        body, name=name, grid=(Fd // tn, T // tm),
        in_specs=[pl.BlockSpec((tm, Dm), lambda j, i: (i, 0)), pl.BlockSpec((tn, Dm), lambda j, i: (j, 0)), blk, blk],
        out_specs=[blk, blk], out_shape=[oshape, oshape],
        compiler_params=_params(("parallel", "parallel")),
    )(dhb, wd, PG, PU)


_INV_SQRT2 = 0.7071067811865476
_INV_SQRT2PI = 0.3989422804014327


def _erf(x):
    ax = jnp.abs(x)
    t = 1.0 / (1.0 + 0.3275911 * ax)
    poly = t * (0.254829592 + t * (-0.284496736 + t * (1.421413741 + t * (-1.453152027 + t * 1.061405429))))
    y = 1.0 - poly * jnp.exp(-ax * ax)
    return jnp.where(x < 0, -y, y)


def _gelu_cdf(x):
    return 0.5 * (1.0 + _erf(x * _INV_SQRT2))


def _lane_lt64(shape):
    return lax.broadcasted_iota(jnp.int32, shape, len(shape) - 1) < HEAD_DIM


def _dup_half(x, kv):
    rolled = pltpu.roll(x, HEAD_DIM, 1)
    lo = _lane_lt64(x.shape)
    return jnp.where(lo, x, rolled) if kv == 0 else jnp.where(lo, rolled, x)


HEADS_PER_KV = B_Q_HEADS // B_KV_HEADS
PAIRS = HEADS_PER_KV // 2


def _attn_bias():
    shape = (2 * CHUNK, HEADS_PER_KV * CHUNK)
    qpos = (lax.broadcasted_iota(jnp.int32, shape, 1) & (CHUNK - 1)) + CHUNK
    kpos = lax.broadcasted_iota(jnp.int32, shape, 0)
    diff = qpos - kpos
    band = (diff >= 0) & (diff < CHUNK)
    return jnp.stack([jnp.where(band & (kpos >= CHUNK), 0.0, NEG), jnp.where(band, 0.0, NEG)]).astype(F32)


def _stack_heads(tiles, lo):
    parts = []
    for t in tiles:
        parts += [jnp.where(lo, t, 0.0), jnp.where(lo, 0.0, t)]
    return jnp.concatenate(parts, axis=0)


def _unstack_heads(s, lo):
    return [jnp.where(lo, s[2 * p * CHUNK:(2 * p + 1) * CHUNK], s[(2 * p + 1) * CHUNK:(2 * p + 2) * CHUNK])
            for p in range(PAIRS)]


def _stack_sinks(sk_ref, kv):
    return jnp.concatenate([jnp.broadcast_to(sk_ref[:, h:h + 1], (1, CHUNK))
                            for h in range(kv * HEADS_PER_KV, (kv + 1) * HEADS_PER_KV)], axis=1)


def _sgu_forward(z_ref, gv, wsm, bst):
    zu = z_ref[:, 0:D_A]
    zv = z_ref[:, D_A:2 * D_A]
    cu = _gelu_cdf(zu)
    cv = _gelu_cdf(zv)
    u = zu * cu
    v = zv * cv
    rv = _rstd(v)
    vn = (v * rv * gv).astype(BF16)
    svs = []
    for g in range(A_GROUPS):
        sl = slice(g * CHUNK, (g + 1) * CHUNK)
        svs.append(jnp.dot(wsm[g], vn[:, sl], preferred_element_type=F32) + bst[:, g:g + 1])
    sv = jnp.concatenate(svs, axis=1)
    return (zu, zv, cu, cv), u, v, rv, vn, sv


def _masked_ws(ws_ref):
    tril = lax.broadcasted_iota(jnp.int32, (CHUNK, CHUNK), 0) >= lax.broadcasted_iota(jnp.int32, (CHUNK, CHUNK), 1)
    return [jnp.where(tril, ws_ref[g], 0.0).astype(BF16) for g in range(A_GROUPS)], tril


def _attn_probs(qm, kkd, sink, bias):
    s = lax.dot_general(kkd, qm, (((1,), (1,)), ((), ())), preferred_element_type=F32) * (HEAD_DIM ** -0.5) + bias
    m = jnp.maximum(jnp.max(s, axis=0, keepdims=True), sink)
    e = jnp.exp(s - m)
    inv = 1.0 / (jnp.sum(e, axis=0, keepdims=True) + jnp.exp(sink - m))
    return e * inv


def _mixer_fwd(z, gv, ws, bst, sinks, ga, gb, *, name):
    T = z.shape[0]
    nb = T // CHUNK
    kvb = O_K // (2 * CHUNK)

    def body(z_ref, zp_ref, bias_ref, gv_ref, ws_ref, bst_ref, sk_ref, ga_ref, gb_ref, o_ref, p_ref):
        wsm, _ = _masked_ws(ws_ref)
        _, u, _, _, _, sv = _sgu_forward(z_ref, gv_ref[...], wsm, bst_ref[...])
        ya = u * sv
        o_ref[:, 0:D_A] = (ya * _rstd(ya) * ga_ref[...]).astype(BF16)

        mask = bias_ref[...]
        kk = jnp.concatenate([zp_ref[:, 0:CHUNK], z_ref[:, O_K:O_V]], axis=0)
        vv = jnp.concatenate([zp_ref[:, CHUNK:2 * CHUNK], z_ref[:, O_V:IN_COLS]], axis=0)
        lo = _lane_lt64((CHUNK, LANES))
        outs = []
        for kv in range(B_KV_HEADS):
            kkd = _dup_half(kk, kv).astype(BF16)
            vvd = _dup_half(vv, kv).astype(BF16)
            q = _stack_heads([z_ref[:, O_Q + (kv * PAIRS + pr) * LANES:O_Q + (kv * PAIRS + pr + 1) * LANES]
                              for pr in range(PAIRS)], lo).astype(BF16)
            p = _attn_probs(q, kkd, _stack_sinks(sk_ref, kv), mask)
            p_ref[kv] = p
            out = lax.dot_general(p.astype(BF16), vvd, (((0,), (0,)), ((), ())), preferred_element_type=F32)
            outs += _unstack_heads(out, lo)
        yb = jnp.concatenate(outs, axis=1)
        o_ref[:, D_A:D_A + D_B] = (yb * _rstd(yb) * gb_ref[...]).astype(BF16)

    full = lambda shape: pl.BlockSpec(shape, lambda i: (0,) * len(shape))
    pshape = (B_KV_HEADS, 2 * CHUNK, HEADS_PER_KV * CHUNK)
    return pl.pallas_call(
        body, name=name, grid=(nb,),
        in_specs=[pl.BlockSpec((CHUNK, IN_COLS), lambda i: (i, 0)),
                  pl.BlockSpec((CHUNK, 2 * CHUNK), lambda i: (jnp.maximum(i - 1, 0), kvb)),
                  pl.BlockSpec((None, 2 * CHUNK, HEADS_PER_KV * CHUNK), lambda i: (jnp.minimum(i, 1), 0, 0)),
                  full((1, D_A)), full((A_GROUPS, CHUNK, CHUNK)), full((CHUNK, A_GROUPS)), full((1, B_Q_HEADS)),
                  full((1, D_A)), full((1, D_B))],
        out_specs=[pl.BlockSpec((CHUNK, D_A + D_B), lambda i: (i, 0)),
                   pl.BlockSpec((None,) + pshape, lambda i: (i, 0, 0, 0))],
        out_shape=[jax.ShapeDtypeStruct((T, D_A + D_B), BF16), jax.ShapeDtypeStruct((nb,) + pshape, F32)],
        compiler_params=_params(("parallel",)),
    )(z, z, _attn_bias(), gv, ws, bst, sinks, ga, gb)


def _mixer_bwd(z, dyn, probs, gv, ws, bst, ga, gb, *, name):
    T = z.shape[0]
    nb = T // CHUNK
    kvb = O_K // (2 * CHUNK)
    NT = (((0,), (0,)), ((), ()))

    def body(z_ref, zp_ref, dy_ref, p_ref, gv_ref, ws_ref, bst_ref, ga_ref, gb_ref,
             dz_ref, dgv_ref, dws_ref, dbst_ref, dsk_ref, dga_ref, dgb_ref, carry_ref):
        step = pl.program_id(0)

        @pl.when(step == 0)
        def _():
            carry_ref[...] = jnp.zeros_like(carry_ref)
            dgv_ref[...] = jnp.zeros_like(dgv_ref)
            dws_ref[...] = jnp.zeros_like(dws_ref)
            dbst_ref[...] = jnp.zeros_like(dbst_ref)
            dsk_ref[...] = jnp.zeros_like(dsk_ref)
            dga_ref[...] = jnp.zeros_like(dga_ref)
            dgb_ref[...] = jnp.zeros_like(dgb_ref)

        wsm, tril = _masked_ws(ws_ref)
        gvv = gv_ref[...]
        (zu, zv, cu, cv), u, v, rv, vn, sv = _sgu_forward(z_ref, gvv, wsm, bst_ref[...])
        ya = u * sv
        dya, dga_rows = _rms_bwd_math(ya, ga_ref[...], dy_ref[:, 0:D_A].astype(F32))
        dga_ref[...] += jnp.sum(dga_rows, axis=0, keepdims=True)
        du = dya * sv
        dsv = dya * u
        dvn_parts = []
        for g in range(A_GROUPS):
            sl = slice(g * CHUNK, (g + 1) * CHUNK)
            dsv_g = dsv[:, sl]
            dsv_gb = dsv_g.astype(BF16)
            dw = lax.dot_general(dsv_gb, vn[:, sl], (((1,), (1,)), ((), ())), preferred_element_type=F32)
            dws_ref[g] += jnp.where(tril, dw, 0.0)
            dbst_ref[:, g:g + 1] += jnp.sum(dsv_g, axis=1, keepdims=True)
            dvn_parts.append(lax.dot_general(wsm[g], dsv_gb, NT, preferred_element_type=F32))
        dvn = jnp.concatenate(dvn_parts, axis=1)
        dv, dgv_rows = _rms_bwd_math(v, gvv, dvn)
        dgv_ref[...] += jnp.sum(dgv_rows, axis=0, keepdims=True)
        dz_ref[:, 0:D_A] = (du * (cu + zu * jnp.exp(-0.5 * zu * zu) * _INV_SQRT2PI)).astype(BF16)
        dz_ref[:, D_A:2 * D_A] = (dv * (cv + zv * jnp.exp(-0.5 * zv * zv) * _INV_SQRT2PI)).astype(BF16)

        kk = jnp.concatenate([zp_ref[:, 0:CHUNK], z_ref[:, O_K:O_V]], axis=0)
        vv = jnp.concatenate([zp_ref[:, CHUNK:2 * CHUNK], z_ref[:, O_V:IN_COLS]], axis=0)
        lo = _lane_lt64((CHUNK, LANES))
        kkd = [_dup_half(kk, kv).astype(BF16) for kv in range(B_KV_HEADS)]
        vvd = [_dup_half(vv, kv).astype(BF16) for kv in range(B_KV_HEADS)]
        outs = []
        for kv in range(B_KV_HEADS):
            outs += _unstack_heads(lax.dot_general(p_ref[kv].astype(BF16), vvd[kv], NT, preferred_element_type=F32), lo)
        yb = jnp.concatenate(outs, axis=1)
        dyb, dgb_rows = _rms_bwd_math(yb, gb_ref[...], dy_ref[:, D_A:D_A + D_B].astype(F32))
        dgb_ref[...] += jnp.sum(dgb_rows, axis=0, keepdims=True)

        dkk, dvv = [], []
        for kv in range(B_KV_HEADS):
            do = _stack_heads([dyb[:, (kv * PAIRS + pr) * LANES:(kv * PAIRS + pr + 1) * LANES]
                               for pr in range(PAIRS)], lo).astype(BF16)
            q = _stack_heads([z_ref[:, O_Q + (kv * PAIRS + pr) * LANES:O_Q + (kv * PAIRS + pr + 1) * LANES]
                              for pr in range(PAIRS)], lo).astype(BF16)
            p = p_ref[kv]
            dvv.append(jnp.dot(p.astype(BF16), do, preferred_element_type=F32))
            dp = lax.dot_general(vvd[kv], do, (((1,), (1,)), ((), ())), preferred_element_type=F32)
            delta = jnp.sum(p * dp, axis=0, keepdims=True)
            dsink = (jnp.sum(p, axis=0, keepdims=True) - 1.0) * delta
            for g in range(HEADS_PER_KV):
                h = kv * HEADS_PER_KV + g
                dsk_ref[:, h:h + 1] += jnp.sum(dsink[:, g * CHUNK:(g + 1) * CHUNK], axis=1, keepdims=True)
            ds = (p * (dp - delta) * (HEAD_DIM ** -0.5)).astype(BF16)
            dq = _unstack_heads(lax.dot_general(ds, kkd[kv], NT, preferred_element_type=F32), lo)
            for pr in range(PAIRS):
                c0 = O_Q + (kv * PAIRS + pr) * LANES
                dz_ref[:, c0:c0 + LANES] = dq[pr].astype(BF16)
            dkk.append(jnp.dot(ds, q, preferred_element_type=F32))

        def fold(parts):
            tot = [t + pltpu.roll(t, HEAD_DIM, 1) for t in parts]
            return jnp.where(_lane_lt64(tot[0].shape), tot[0], tot[1])

        dk_all = fold(dkk)
        dv_all = fold(dvv)
        dz_ref[:, O_K:O_V] = (dk_all[CHUNK:] + carry_ref[:, 0:CHUNK]).astype(BF16)
        dz_ref[:, O_V:IN_COLS] = (dv_all[CHUNK:] + carry_ref[:, CHUNK:2 * CHUNK]).astype(BF16)
        carry_ref[:, 0:CHUNK] = dk_all[:CHUNK]
        carry_ref[:, CHUNK:2 * CHUNK] = dv_all[:CHUNK]

    full = lambda shape: pl.BlockSpec(shape, lambda s: (0,) * len(shape))
    rev = lambda s: nb - 1 - s
    return pl.pallas_call(
        body, name=name, grid=(nb,),
        in_specs=[pl.BlockSpec((CHUNK, IN_COLS), lambda s: (rev(s), 0)),
                  pl.BlockSpec((CHUNK, 2 * CHUNK), lambda s: (jnp.maximum(rev(s) - 1, 0), kvb)),
                  pl.BlockSpec((CHUNK, D_A + D_B), lambda s: (rev(s), 0)),
                  pl.BlockSpec((None, B_KV_HEADS, 2 * CHUNK, HEADS_PER_KV * CHUNK), lambda s: (rev(s), 0, 0, 0)),
                  full((1, D_A)), full((A_GROUPS, CHUNK, CHUNK)), full((CHUNK, A_GROUPS)),
                  full((1, D_A)), full((1, D_B))],
        out_specs=[pl.BlockSpec((CHUNK, IN_COLS), lambda s: (rev(s), 0)),
                   full((1, D_A)), full((A_GROUPS, CHUNK, CHUNK)), full((CHUNK, A_GROUPS)), full((1, B_Q_HEADS)),
                   full((1, D_A)), full((1, D_B))],
        out_shape=[jax.ShapeDtypeStruct((T, IN_COLS), BF16), jax.ShapeDtypeStruct((1, D_A), F32),
                   jax.ShapeDtypeStruct((A_GROUPS, CHUNK, CHUNK), F32), jax.ShapeDtypeStruct((CHUNK, A_GROUPS), F32),
                   jax.ShapeDtypeStruct((1, B_Q_HEADS), F32), jax.ShapeDtypeStruct((1, D_A), F32),
                   jax.ShapeDtypeStruct((1, D_B), F32)],
        scratch_shapes=[pltpu.VMEM((CHUNK, 2 * CHUNK), F32)],
        compiler_params=_params(("arbitrary",)),
    )(z, z, dyn, probs, gv, ws, bst, ga, gb)


def _xattn_probs(qh, kh):
    s = lax.dot_general(kh, qh, (((1,), (1,)), ((), ())), preferred_element_type=F32) * (X_HEAD_DIM ** -0.5)
    e = jnp.exp(s - jnp.max(s, axis=0, keepdims=True))
    return e / jnp.sum(e, axis=0, keepdims=True)


def _xattn_fwd(q, kvm, *, name, tm=512):
    T = q.shape[0]
    Mm = kvm.shape[0]
    tm = min(tm, T)

    def body(q_ref, kv_ref, o_ref, p_ref):
        for h in range(X_HEADS):
            sl = slice(h * X_HEAD_DIM, (h + 1) * X_HEAD_DIM)
            kh = kv_ref[:, sl].astype(BF16)
            vh = kv_ref[:, D_MODEL + h * X_HEAD_DIM:D_MODEL + (h + 1) * X_HEAD_DIM].astype(BF16)
            p = _xattn_probs(q_ref[:, sl], kh)
            p_ref[h * Mm:(h + 1) * Mm, :] = p
            o_ref[:, sl] = lax.dot_general(p.astype(BF16), vh, (((0,), (0,)), ((), ())),
                                           preferred_element_type=F32).astype(BF16)

    return pl.pallas_call(
        body, name=name, grid=(T // tm,),
        in_specs=[pl.BlockSpec((tm, D_MODEL), lambda i: (i, 0)), pl.BlockSpec((Mm, 2 * D_MODEL), lambda i: (0, 0))],
        out_specs=[pl.BlockSpec((tm, D_MODEL), lambda i: (i, 0)), pl.BlockSpec((X_HEADS * Mm, tm), lambda i: (0, i))],
        out_shape=[jax.ShapeDtypeStruct((T, D_MODEL), BF16), jax.ShapeDtypeStruct((X_HEADS * Mm, T), F32)],
        compiler_params=_params(("parallel",)),
    )(q, kvm)


def _xattn_bwd(q, kvm, probs, do, *, name, tm=512):
    T = q.shape[0]
    Mm = kvm.shape[0]
    tm = min(tm, T)
    NT = (((0,), (0,)), ((), ()))

    def body(q_ref, kv_ref, p_ref, do_ref, dq_ref, dkv_ref):
        @pl.when(pl.program_id(0) == 0)
        def _():
            dkv_ref[...] = jnp.zeros_like(dkv_ref)

        for h in range(X_HEADS):
            sl = slice(h * X_HEAD_DIM, (h + 1) * X_HEAD_DIM)
            slv = slice(D_MODEL + h * X_HEAD_DIM, D_MODEL + (h + 1) * X_HEAD_DIM)
            kh = kv_ref[:, sl].astype(BF16)
            vh = kv_ref[:, slv].astype(BF16)
            qh = q_ref[:, sl]
            doh = do_ref[:, sl]
            p = p_ref[h * Mm:(h + 1) * Mm, :]
            dkv_ref[:, slv] += jnp.dot(p.astype(BF16), doh, preferred_element_type=F32)
            dp = lax.dot_general(vh, doh, (((1,), (1,)), ((), ())), preferred_element_type=F32)
            ds = (p * (dp - jnp.sum(p * dp, axis=0, keepdims=True)) * (X_HEAD_DIM ** -0.5)).astype(BF16)
            dq_ref[:, sl] = lax.dot_general(ds, kh, NT, preferred_element_type=F32).astype(BF16)
            dkv_ref[:, sl] += jnp.dot(ds, qh, preferred_element_type=F32)

    row = pl.BlockSpec((tm, D_MODEL), lambda i: (i, 0))
    kvs = pl.BlockSpec((Mm, 2 * D_MODEL), lambda i: (0, 0))
    return pl.pallas_call(
        body, name=name, grid=(T // tm,),
        in_specs=[row, kvs, pl.BlockSpec((X_HEADS * Mm, tm), lambda i: (0, i)), row], out_specs=[row, kvs],
        out_shape=[jax.ShapeDtypeStruct((T, D_MODEL), BF16), jax.ShapeDtypeStruct((Mm, 2 * D_MODEL), F32)],
        compiler_params=_params(("arbitrary",)),
    )(q, kvm, probs, do)


def _swiglu_bwd_weights(tag, n, PG, PU, A, wd, dhb):
    T = n.shape[0]
    dG, dU = _swiglu_bwd_act(dhb, wd, PG, PU, name=f"{tag}_bwd_act", tm=1024)
    dwd = _matmul([(A, dhb)], M=D_FF, N=D_MODEL, K=T, tm=1408, tn=1024, tk=2048, a_t=True, out_dtype=BF16,
                  scale=0.5, name=f"{tag}_dwd")
    dwg = _matmul([(n, dG)], M=D_MODEL, N=D_FF, K=T, tm=512, tn=D_FF // 2, tk=2048, a_t=True, out_kind="s",
                  out_dtype=BF16, order="ji", name=f"{tag}_dwg")
    dwu = _matmul([(n, dU)], M=D_MODEL, N=D_FF, K=T, tm=512, tn=D_FF // 2, tk=2048, a_t=True, out_kind="s",
                  out_dtype=BF16, order="ji", name=f"{tag}_dwu")
    return dG, dU, dwg, dwu, dwd


def _swiglu_bwd_input(tag, hin, g_norm, dG, dU, wg, wu, dh, dep):
    T = hin.shape[0]
    dn = _matmul([(dG, wg), (dU, wu)], M=T, N=D_MODEL, K=D_FF, tm=512, tn=D_MODEL // 2, tk=D_FF // 2, b_kind="t",
                 out_dtype=BF16, dep=dep, name=f"{tag}_dn")
    return _rms_bwd(hin, g_norm, dn, dh, name=f"{tag}_norm_bwd")


GROUP_FFN1 = ["w1_gate", "w1_up", "w1_down"]
GROUP_MID = ["w_in", "w_out", "w_xq", "w_xkv", "w_xo"]
GROUP_FFN2 = ["w2_gate", "w2_up", "w2_down"]


def _local_step(x, mem, tgt, W, comm):
    T = x.shape[0]
    Mm = mem.shape[0]
    mm = functools.partial(_matmul)

    W = {**W, **comm.gather_now("ffn1_up", ["w1_gate", "w1_up"])}
    tok = comm.gather_start("ffn1_down", ["w1_down"], after=W["w1_up"])
    tok = comm.gather_start("mid", GROUP_MID, after=tok)
    tok = comm.gather_start("ffn2", GROUP_FFN2, after=tok)
    n1 = _rms_fwd(x, W["g_ffn1"], dep=tok, name="f_norm1")
    PG1, PU1, A1 = _swiglu_up(n1, W["w1_gate"], W["w1_up"], name="f_ffn1_up")
    tok = comm.gather_landed("ffn1_down", after=A1)
    tok = comm.gather_landed("mid", after=tok)
    W = {**W, **comm.gather_ready("ffn1_down", after=tok)}
    h1 = mm([(A1, W["w1_down"])], M=T, N=D_MODEL, K=D_FF, tm=512, tn=D_MODEL // 2, tk=D_FF, scale=0.5, res=x,
            order="ji", name="f_ffn1_down")
    n2 = _rms_fwd(h1, W["g_mix"], name="f_norm2")
    W = {**W, **comm.gather_ready("mid", after=n2)}
    z = mm([(n2, W["w_in"])], M=T, N=IN_COLS, K=D_MODEL, tm=512, tn=IN_COLS // 2, tk=D_MODEL, name="f_w_in")
    bst = jnp.transpose(W["b_s"])
    yn, probs = _mixer_fwd(z, W["g_v"], W["w_s"], bst, W["sinks"], W["g_a_out"], W["g_b_out"], name="f_mixer")
    tok = comm.gather_landed("ffn2", after=yn)
    h2, n3 = mm([(yn, W["w_out"])], M=T, N=D_MODEL, K=D_MODEL, tm=512, tn=D_MODEL, tk=D_MODEL, res=h1,
                norm_g=W["g_x"], dep=tok, name="f_w_out")
    memn = _rms_fwd(mem, W["g_mem"], name="f_norm_mem")
    q3 = mm([(n3, W["w_xq"])], M=T, N=D_MODEL, K=D_MODEL, tm=1024, tn=D_MODEL, tk=D_MODEL, out_dtype=BF16,
            name="f_w_xq")
    kvm = mm([(memn, W["w_xkv"])], M=Mm, N=2 * D_MODEL, K=D_MODEL, tm=Mm, tn=1024, tk=D_MODEL, b_kind="n",
             name="f_w_xkv")
    o3, xprobs = _xattn_fwd(q3, kvm, name="f_xattn")
    h3, n4 = mm([(o3, W["w_xo"])], M=T, N=D_MODEL, K=D_MODEL, tm=512, tn=D_MODEL, tk=D_MODEL, res=h2,
                norm_g=W["g_ffn2"], name="f_w_xo")
    W = {**W, **comm.gather_ready("ffn2", after=n4)}
    PG2, PU2, A2 = _swiglu_up(n4, W["w2_gate"], W["w2_up"], name="f_ffn2_up")
    h4 = mm([(A2, W["w2_down"])], M=T, N=D_MODEL, K=D_FF, tm=512, tn=D_MODEL // 2, tk=D_FF, scale=0.5, res=h3,
            order="ji", name="f_ffn2_down")

    grads = {}
    dh4, dh4b, grads["g_final"], loss = _loss_head(h4, W["g_final"], tgt, name="loss_head")
    dG2, dU2, dwg, dwu, dwd = _swiglu_bwd_weights("b_ffn2", n4, PG2, PU2, A2, W["w2_down"], dh4b)
    tok = comm.reduce_pair_start("ffn2", {"w2_gate": dwg, "w2_up": dwu, "w2_down": dwd})
    dh3, dh3b, grads["g_ffn2"] = _swiglu_bwd_input("b_ffn2", h3, W["g_ffn2"], dG2, dU2, W["w2_gate"], W["w2_up"],
                                                   dh4, tok)
    tok = comm.reduce_pair_done("ffn2", after=dh3b)

    mid = {}
    do3 = mm([(dh3b, W["w_xo"])], M=T, N=D_MODEL, K=D_MODEL, tm=512, tn=D_MODEL, tk=D_MODEL, b_kind="t",
             out_dtype=BF16, dep=tok, name="b_do3")
    mid["w_xo"] = mm([(o3, dh3b)], M=D_MODEL, N=D_MODEL, K=T, tm=1024, tn=D_MODEL // 2, tk=4096, a_t=True,
                       out_dtype=BF16, name="b_dw_xo")
    dq3, dkvm = _xattn_bwd(q3, kvm, xprobs, do3, name="b_xattn")
    mid["w_xq"] = mm([(n3, dq3)], M=D_MODEL, N=D_MODEL, K=T, tm=1024, tn=D_MODEL // 2, tk=4096, a_t=True,
                       out_dtype=BF16, name="b_dw_xq")
    dn3 = mm([(dq3, W["w_xq"])], M=T, N=D_MODEL, K=D_MODEL, tm=512, tn=D_MODEL, tk=D_MODEL, b_kind="t",
             out_dtype=BF16, name="b_dn3")
    dh2, dh2b, grads["g_x"] = _rms_bwd(h2, W["g_x"], dn3, dh3, name="b_norm3")
    dkvmb = dkvm.astype(BF16)
    mid["w_xkv"] = mm([(memn, dkvmb)], M=D_MODEL, N=2 * D_MODEL, K=Mm, tm=D_MODEL, tn=1024, tk=Mm, a_t=True,
                        out_kind="s", out_dtype=BF16, name="b_dw_xkv")
    dmemn = mm([(dkvmb, W["w_xkv"])], M=Mm, N=D_MODEL, K=2 * D_MODEL, tm=Mm, tn=D_MODEL, tk=1024, b_kind="t",
               name="b_dmemn")
    _, _, grads["g_mem"] = _rms_bwd(mem, W["g_mem"], dmemn, None, name="b_norm_mem")
    comm.reduce_finish("ffn2", after=dh2b)

    dyn = mm([(dh2b, W["w_out"])], M=T, N=D_MODEL, K=D_MODEL, tm=1024, tn=D_MODEL, tk=D_MODEL, b_kind="t",
             out_dtype=BF16, name="b_dyn")
    mid["w_out"] = mm([(yn, dh2b)], M=D_MODEL, N=D_MODEL, K=T, tm=1024, tn=D_MODEL // 2, tk=4096, a_t=True,
                        out_dtype=BF16, name="b_dw_out")
    dz, grads["g_v"], grads["w_s"], dbst, grads["sinks"], grads["g_a_out"], grads["g_b_out"] = _mixer_bwd(
        z, dyn, probs, W["g_v"], W["w_s"], bst, W["g_a_out"], W["g_b_out"], name="b_mixer")
    grads["b_s"] = jnp.transpose(dbst)
    mid["w_in"] = mm([(n2, dz)], M=D_MODEL, N=IN_COLS, K=T, tm=1024, tn=IN_COLS, tk=1024, a_t=True,
                     out_dtype=BF16, name="b_dw_in")
    tok = comm.reduce_pair_start("mid", mid)
    dn2 = mm([(dz, W["w_in"])], M=T, N=D_MODEL, K=IN_COLS, tm=512, tn=D_MODEL, tk=IN_COLS, b_kind="t",
             out_dtype=BF16, dep=tok, name="b_dn2")
    tok = comm.reduce_pair_done("mid", after=dn2)
    dh1, dh1b, grads["g_mix"] = _rms_bwd(h1, W["g_mix"], dn2, dh2, dep=tok, name="b_norm2")

    dG1, dU1, dwg, dwu, dwd = _swiglu_bwd_weights("b_ffn1", n1, PG1, PU1, A1, W["w1_down"], dh1b)
    comm.reduce_finish("mid", after=dwu)
    tok = comm.reduce_start("ffn1", {"w1_gate": dwg, "w1_up": dwu, "w1_down": dwd})
    dx, _, grads["g_ffn1"] = _swiglu_bwd_input("b_ffn1", x, W["g_ffn1"], dG1, dU1, W["w1_gate"], W["w1_up"], dh1, tok)
    comm.reduce_finish("ffn1", after=dx)
    return loss, dx, grads


BIG = ["w1_gate", "w1_up", "w1_down", "w_in", "w_out", "w_xq", "w_xkv", "w_xo", "w2_gate", "w2_up", "w2_down"]
SMALL = ["g_ffn1", "g_mix", "g_v", "w_s", "b_s", "sinks", "g_a_out", "g_b_out", "g_x", "g_mem", "g_ffn2", "g_final"]
ALL_W = ["g_ffn1", "w1_gate", "w1_up", "w1_down", "g_mix", "w_in", "g_v", "w_s", "b_s", "sinks", "g_a_out",
         "g_b_out", "w_out", "g_x", "g_mem", "w_xq", "w_xkv", "w_xo", "g_ffn2", "w2_gate", "w2_up", "w2_down",
         "g_final"]
ANY = pl.BlockSpec(memory_space=pl.ANY)


def _place():
    x, y, c = lax.axis_index("x"), lax.axis_index("y"), lax.axis_index("c")
    chips = [(1 - x, y), (x, 1 - y), (1 - x, 1 - y)]
    return x, y, c, chips


COL_SHARDED = ("w1_gate", "w1_up", "w2_gate", "w2_up", "w_xkv")


def _gathered_shape(shape, by_cols):
    rows, cols = shape
    return (rows, N_CHIPS * cols) if by_cols else (N_CHIPS, rows, cols)


def _owner_rows(ref, shape, by_cols, slot, r0, rows):
    cols = shape[1]
    if by_cols:
        return ref.at[pl.ds(r0, rows), pl.ds(pl.multiple_of(slot * cols, LANES), cols)]
    return ref.at[slot, pl.ds(r0, rows), :]


def _allgather_weights(shards, by_cols, *, name):
    n = len(shards)

    def body(*refs):
        ins, outs = refs[:n], refs[n:2 * n]
        send, recv, loc = refs[2 * n:]
        x, y, c, chips = _place()
        me = 2 * x + y
        sib = (x, y, 1 - c)

        def half(w, slot, hc):
            h = shards[w].shape[0] // 2
            return _owner_rows(outs[w], shards[w].shape, by_cols[w], slot, hc * h, h)

        def copy(w, k, slot, hc, to, src=None):
            return pltpu.make_async_remote_copy(
                src_ref=half(w, slot, hc) if src is None else src, dst_ref=half(w, slot, hc),
                send_sem=send.at[6 * w + k], recv_sem=recv.at[6 * w + k], device_id=to, device_id_type=MESH)

        own = [pltpu.make_async_remote_copy(
            src_ref=ins[w], dst_ref=_owner_rows(outs[w], shards[w].shape, by_cols[w], me, 0, shards[w].shape[0]),
            send_sem=loc.at[w], recv_sem=loc.at[n + w], device_id=sib, device_id_type=MESH) for w in range(n)]
        for cp in own:
            cp.start()
        first = []
        for w in range(n):
            h = shards[w].shape[0] // 2
            for j, (tx, ty) in enumerate(chips):
                first.append(copy(w, j, me, c, (tx, ty, c), src=ins[w].at[pl.ds(c * h, h), :]))
                first[-1].start()
        passed = []
        for w in range(n):
            for j, (tx, ty) in enumerate(chips):
                slot = 2 * tx + ty
                copy(w, j, slot, c, (tx, ty, c)).wait_recv()
                passed.append(copy(w, 3 + j, slot, c, sib))
                passed[-1].start()
        for w in range(n):
            for j, (tx, ty) in enumerate(chips):
                copy(w, 3 + j, 2 * tx + ty, 1 - c, sib).wait_recv()
        for cp in first + passed:
            cp.wait_send()
        for cp in own:
            cp.wait()

    return pl.pallas_call(
        body, name=name, in_specs=[ANY] * n, out_specs=[ANY] * n,
        out_shape=[jax.ShapeDtypeStruct(_gathered_shape(s.shape, bc), s.dtype) for s, bc in zip(shards, by_cols)],
        scratch_shapes=[pltpu.SemaphoreType.DMA((6 * n,)), pltpu.SemaphoreType.DMA((6 * n,)),
                        pltpu.SemaphoreType.DMA((2 * n,))],
    )(*shards)


def _pair_exchange(grads, *, name):
    n = len(grads)

    def body(*refs):
        ins, outs = refs[:n], refs[n:2 * n]
        send, recv = refs[2 * n:]
        x, y, c, _ = _place()
        cps = []
        for w in range(n):
            h = grads[w].shape[1] // 2
            cps.append(pltpu.make_async_remote_copy(
                src_ref=ins[w].at[:, pl.ds((1 - c) * h, h), :], dst_ref=outs[w],
                send_sem=send.at[w], recv_sem=recv.at[w], device_id=(x, y, 1 - c), device_id_type=MESH))
            cps[-1].start()
        for cp in cps:
            cp.wait()

    return pl.pallas_call(
        body, name=name, in_specs=[ANY] * n, out_specs=[ANY] * n,
        out_shape=[jax.ShapeDtypeStruct((N_CHIPS, g.shape[1] // 2, g.shape[2]), g.dtype) for g in grads],
        scratch_shapes=[pltpu.SemaphoreType.DMA((n,)), pltpu.SemaphoreType.DMA((n,))],
    )(*grads)


def _pair_sum(g, got, *, name):
    S, R, C = g.shape
    h = R // 2
    tr = _row_block(h, 3 * C * 2, 16)
    nr = h // tr

    def body(a_ref, b_ref, o_ref):
        o_ref[...] = (a_ref[...].astype(F32) + b_ref[...].astype(F32)).astype(BF16)

    return pl.pallas_call(
        body, name=name, grid=(S, nr),
        in_specs=[pl.BlockSpec((None, tr, C), lambda s, r: (s, lax.axis_index("c") * nr + r, 0)),
                  pl.BlockSpec((None, tr, C), lambda s, r: (s, r, 0))],
        out_specs=pl.BlockSpec((None, tr, C), lambda s, r: (s, r, 0)),
        out_shape=jax.ShapeDtypeStruct((S, h, C), BF16),
        compiler_params=_params(("parallel", "parallel")),
    )(g, got)


def _chip_sum(part, got, *, name):
    S, h, C = part.shape
    tr = _row_block(h, 4 * C * 2 + C * 4, 16)
    nr = h // tr

    def body(own_ref, g0_ref, g1_ref, g2_ref, o_ref):
        acc = own_ref[...].astype(F32) + g0_ref[...].astype(F32)
        o_ref[...] = (acc + g1_ref[...].astype(F32)) + g2_ref[...].astype(F32)

    def piece(j):
        return pl.BlockSpec((None, tr, C), lambda r: (j, r, 0))

    return pl.pallas_call(
        body, name=name, grid=(nr,),
        in_specs=[pl.BlockSpec((None, tr, C), lambda r: (2 * lax.axis_index("x") + lax.axis_index("y"), r, 0)),
                  piece(0), piece(1), piece(2)],
        out_specs=pl.BlockSpec((tr, C), lambda r: (lax.axis_index("c") * nr + r, 0)),
        out_shape=jax.ShapeDtypeStruct((2 * h, C), F32),
        compiler_params=_params(("parallel",)),
    )(part, got, got, got)


def _pair_gather(totals, *, name):
    n = len(totals)

    def body(*refs):
        ins, outs = refs[:n], refs[n:2 * n]
        send, recv = refs[2 * n:]
        x, y, c, _ = _place()
        cps = []
        for w in range(n):
            h = totals[w].shape[0] // 2
            cps.append(pltpu.make_async_remote_copy(
                src_ref=ins[w].at[pl.ds(c * h, h), :], dst_ref=outs[w].at[pl.ds(c * h, h), :],
                send_sem=send.at[w], recv_sem=recv.at[w], device_id=(x, y, 1 - c), device_id_type=MESH))
            cps[-1].start()
        for w in range(n):
            h = totals[w].shape[0] // 2
            theirs = outs[w].at[pl.ds((1 - c) * h, h), :]
            pltpu.make_async_remote_copy(
                src_ref=theirs, dst_ref=theirs, send_sem=send.at[w], recv_sem=recv.at[w],
                device_id=(x, y, 1 - c), device_id_type=MESH).wait_recv()
        for cp in cps:
            cp.wait_send()

    return pl.pallas_call(
        body, name=name, in_specs=[ANY] * n, out_specs=[ANY] * n,
        out_shape=[jax.ShapeDtypeStruct(t.shape, t.dtype) for t in totals],
        input_output_aliases={w: w for w in range(n)},
        scratch_shapes=[pltpu.SemaphoreType.DMA((n,)), pltpu.SemaphoreType.DMA((n,))],
    )(*totals)


def _allreduce_small(v, *, name):
    R, C = v.shape
    ND = 8

    def body(v_ref, o_ref, all_ref, send, recv, loc):
        x, y, c, chips = _place()
        me, sib = (x, y, c), (x, y, 1 - c)

        def rows(px, py, pc):
            return all_ref.at[pl.ds((4 * px + 2 * py + pc) * R, R), :]

        def copy(k, block, to, src=None):
            return pltpu.make_async_remote_copy(
                src_ref=rows(*block) if src is None else src, dst_ref=rows(*block),
                send_sem=send.at[k], recv_sem=recv.at[k], device_id=to, device_id_type=MESH)

        mine = pltpu.make_async_copy(v_ref, rows(*me), loc)
        mine.start()
        first = [copy(0, me, sib, src=v_ref)]
        first += [copy(1 + j, me, (*chip, c), src=v_ref) for j, chip in enumerate(chips)]
        for cp in first:
            cp.start()
        passed = [copy(4 + j, (*chip, c), sib) for j, chip in enumerate(chips)]
        for j, chip in enumerate(chips):
            copy(1 + j, (*chip, c), me).wait_recv()
            passed[j].start()
        copy(0, sib, me).wait_recv()
        for j, chip in enumerate(chips):
            copy(4 + j, (*chip, 1 - c), me).wait_recv()
        for cp in first + passed:
            cp.wait_send()
        mine.wait()
        acc = all_ref[0:R, :]
        for d in range(1, ND):
            acc = acc + all_ref[d * R:(d + 1) * R, :]
        o_ref[...] = acc

    vm = pl.BlockSpec(memory_space=pltpu.VMEM)
    return pl.pallas_call(
        body, name=name, in_specs=[vm], out_specs=[vm, vm],
        out_shape=[jax.ShapeDtypeStruct((R, C), F32), jax.ShapeDtypeStruct((ND * R, C), F32)],
        scratch_shapes=[pltpu.SemaphoreType.DMA((7,)), pltpu.SemaphoreType.DMA((7,)), pltpu.SemaphoreType.DMA],
        compiler_params=pltpu.CompilerParams(vmem_limit_bytes=VMEM_LIMIT),
    )(v)[0]


HBM = pl.BlockSpec(memory_space=pltpu.HBM)
SEM = pl.BlockSpec(memory_space=pltpu.SEMAPHORE)
EFFECT = pltpu.SideEffectType.DATAFLOW_SIDE_EFFECTING


def _remote(src, dst, send, recv, k, to):
    return pltpu.make_async_remote_copy(src_ref=src, dst_ref=dst, send_sem=send.at[k], recv_sem=recv.at[k],
                                        device_id=to, device_id_type=MESH)


def _split_start(bufs, plan, ncopies, *, name, after=None):
    nb = len(bufs)
    extra = [] if after is None else [after]

    def body(*refs):
        pos = nb + len(extra)
        send, recv, token = refs[pos], refs[pos + 1], refs[-1]
        for k, (src, dst, to) in enumerate(plan(refs[:nb])):
            _remote(src, dst, send, recv, k, to).start()
        token[...] = jnp.zeros_like(token)

    outs = pl.pallas_call(
        body, name=name,
        out_shape=(pltpu.SemaphoreType.DMA((ncopies,)), pltpu.SemaphoreType.DMA((ncopies,)),
                   *[pltpu.HBM(b.shape, b.dtype) for b in bufs], jax.ShapeDtypeStruct((SUBLANES, LANES), F32)),
        in_specs=[HBM] * nb + [ANY] * len(extra),
        out_specs=(SEM, SEM, *[HBM] * nb, pl.BlockSpec(memory_space=pltpu.VMEM)),
        input_output_aliases={i: 2 + i for i in range(nb)},
        compiler_params=pltpu.CompilerParams(has_side_effects=EFFECT),
    )(*[pltpu.with_memory_space_constraint(b, pltpu.HBM) for b in bufs], *extra)
    return outs[0], outs[1], list(outs[2:2 + nb]), outs[-1]


def _split_wait(started, plan, after, *, name):
    send, recv, bufs, _ = started
    nb = len(bufs)

    def body(*refs):
        send_sem, recv_sem = refs[nb], refs[nb + 1]
        for k, (src, dst, to) in enumerate(plan(refs[:nb])):
            cp = _remote(src, dst, send_sem, recv_sem, k, to)
            cp.wait_send()
            cp.wait_recv()

    outs = pl.pallas_call(
        body, name=name,
        out_shape=tuple(pltpu.HBM(b.shape, b.dtype) for b in bufs),
        in_specs=[HBM] * nb + [SEM, SEM, ANY], out_specs=tuple([HBM] * nb),
        input_output_aliases={i: i for i in range(nb)},
        compiler_params=pltpu.CompilerParams(has_side_effects=EFFECT),
    )(*bufs, send, recv, after)
    return list(outs)


def _gather_chip_plan(shapes, by_cols):
    n = len(shapes)

    def plan(refs):
        srcs, lands = refs[:n], refs[n:]
        x, y, c, chips = _place()
        out = []
        for w in range(n):
            h = shapes[w][0] // 2
            mine = _owner_rows(lands[w], shapes[w], by_cols[w], 2 * x + y, c * h, h)
            for tx, ty in chips:
                out.append((srcs[w].at[pl.ds(c * h, h), :], mine, (tx, ty, c)))
        return out

    return plan


def _gather_pair_plan(shapes, by_cols):
    n = len(shapes)

    def plan(refs):
        srcs, lands = refs[:n], refs[n:]
        x, y, c, chips = _place()
        out = []
        for w in range(n):
            h = shapes[w][0] // 2
            for tx, ty in chips:
                half = _owner_rows(lands[w], shapes[w], by_cols[w], 2 * tx + ty, c * h, h)
                out.append((half, half, (x, y, 1 - c)))
            own = _owner_rows(lands[w], shapes[w], by_cols[w], 2 * x + y, 0, shapes[w][0])
            out.append((srcs[w], own, (x, y, 1 - c)))
        return out

    return plan


def _reduce_pair_plan(shapes):
    n = len(shapes)

    def plan(refs):
        local, lands = refs[:n], refs[n:]
        x, y, c, _ = _place()
        out = []
        for w in range(n):
            h = shapes[w][1] // 2
            out.append((local[w].at[:, pl.ds((1 - c) * h, h), :], lands[w], (x, y, 1 - c)))
        return out

    return plan


def _reduce_chip_plan(n):
    def plan(refs):
        parts, lands = refs[:n], refs[n:]
        x, y, c, chips = _place()
        return [(parts[w].at[2 * tx + ty], lands[w].at[j], (tx, ty, c))
                for w in range(n) for j, (tx, ty) in enumerate(chips)]

    return plan


def _as_operands(gathered):
    out = {}
    for n, g in gathered.items():
        if n in COL_SHARDED:
            out[n] = g
        elif n == "w_in":
            out[n] = jnp.transpose(g, (1, 0, 2)).reshape(D_MODEL, IN_COLS)
        else:
            out[n] = g.reshape(g.shape[0] * g.shape[1], g.shape[2])
    return out


def _by_owner(n, g):
    if n == "w_in":
        return jnp.transpose(g.reshape(D_MODEL, N_CHIPS, IN_COLS // N_CHIPS), (1, 0, 2))
    if g.ndim == 2:
        return g.reshape(N_CHIPS, g.shape[0] // N_CHIPS, g.shape[1])
    return g


class _Comm:
    def __init__(self, shards):
        self.shards = shards
        self.total = {}
        self._flight = {}

    def _layout(self, names):
        return [self.shards[n].shape for n in names], [n in COL_SHARDED for n in names]

    def gather_now(self, tag, names):
        _, by_cols = self._layout(names)
        got = _allgather_weights([self.shards[n] for n in names], by_cols, name=f"gather_{tag}")
        return _as_operands(dict(zip(names, got)))

    def gather_start(self, tag, names, after):
        shapes, by_cols = self._layout(names)
        srcs = [self.shards[n] for n in names]
        lands = [lax.empty(_gathered_shape(s.shape, bc), s.dtype) for s, bc in zip(srcs, by_cols)]
        started = _split_start(srcs + lands, _gather_chip_plan(shapes, by_cols), 3 * len(srcs),
                               after=after, name=f"gather_{tag}_chips_start")
        self._flight[tag] = (names, started)
        return started[3]

    def gather_landed(self, tag, after):
        names, started = self._flight[tag]
        shapes, by_cols = self._layout(names)
        bufs = _split_wait(started, _gather_chip_plan(shapes, by_cols), after, name=f"gather_{tag}_chips_wait")
        started = _split_start(bufs, _gather_pair_plan(shapes, by_cols), 4 * len(names),
                               name=f"gather_{tag}_pair_start")
        self._flight[tag] = (names, started)
        return started[3]

    def gather_ready(self, tag, after):
        names, started = self._flight.pop(tag)
        shapes, by_cols = self._layout(names)
        bufs = _split_wait(started, _gather_pair_plan(shapes, by_cols), after, name=f"gather_{tag}_pair_wait")
        return _as_operands(dict(zip(names, bufs[len(names):])))

    def reduce_start(self, tag, grads):
        names = list(grads)
        local = [_by_owner(n, grads[n]) for n in names]
        return self._chip_start(tag, names, local, _pair_exchange(local, name=f"pair_exchange_{tag}"))

    def reduce_pair_start(self, tag, grads):
        names = list(grads)
        local = [_by_owner(n, grads[n]) for n in names]
        lands = [lax.empty((N_CHIPS, g.shape[1] // 2, g.shape[2]), g.dtype) for g in local]
        started = _split_start(local + lands, _reduce_pair_plan([g.shape for g in local]), len(names),
                               name=f"pair_exchange_{tag}_start")
        self._flight[tag] = (names, started)
        return started[3]

    def reduce_pair_done(self, tag, after):
        names, started = self._flight.pop(tag)
        n = len(names)
        bufs = _split_wait(started, _reduce_pair_plan([b.shape for b in started[2][:n]]), after,
                           name=f"pair_exchange_{tag}_wait")
        return self._chip_start(tag, names, bufs[:n], bufs[n:])

    def _chip_start(self, tag, names, local, from_sib):
        parts = [_pair_sum(g, s, name=f"pair_sum_{n}") for n, g, s in zip(names, local, from_sib)]
        lands = [lax.empty((N_CHIPS - 1,) + p.shape[1:], p.dtype) for p in parts]
        self._flight[tag] = (names, _split_start(parts + lands, _reduce_chip_plan(len(names)), 3 * len(names),
                                                 name=f"chip_exchange_{tag}_start"))
        return self._flight[tag][1][3]

    def reduce_finish(self, tag, after):
        names, started = self._flight.pop(tag)
        n = len(names)
        bufs = _split_wait(started, _reduce_chip_plan(n), after, name=f"chip_exchange_{tag}_wait")
        totals = [_chip_sum(p, s, name=f"chip_sum_{nm}") for nm, p, s in zip(names, bufs[:n], bufs[n:])]
        self.total.update(zip(names, _pair_gather(totals, name=f"pair_gather_{tag}")))


def _adamw(w, g, m, v, *, name):
    R, C = w.shape
    tr = _row_block(R, 8 * C * 4, SUBLANES)

    def body(w_ref, g_ref, m_ref, v_ref, go_ref, d_ref, nm_ref, nv_ref):
        gg = g_ref[...]
        go_ref[...] = gg
        m_new = ADAM_B1 * m_ref[...] + (1.0 - ADAM_B1) * gg
        v_new = ADAM_B2 * v_ref[...] + (1.0 - ADAM_B2) * (gg * gg)
        m_hat = m_new / (1.0 - ADAM_B1 ** ADAM_STEP)
        v_hat = v_new / (1.0 - ADAM_B2 ** ADAM_STEP)
        d_ref[...] = -ADAM_LR * (m_hat / (jnp.sqrt(v_hat) + ADAM_EPS) + ADAM_WD * w_ref[...])
        nm_ref[...] = m_new
        nv_ref[...] = v_new

    blk = pl.BlockSpec((tr, C), lambda i: (i, 0))
    shp = jax.ShapeDtypeStruct((R, C), F32)
    return pl.pallas_call(
        body, name=name, grid=(R // tr,), in_specs=[blk] * 4, out_specs=[blk] * 4, out_shape=[shp] * 4,
        compiler_params=_params(("parallel",)),
    )(w, g, m, v)


def _to2d(a):
    flat = a.reshape(-1)
    pad = (-flat.shape[0]) % (SUBLANES * LANES)
    if pad:
        flat = jnp.pad(flat, (0, pad))
    return flat.reshape(-1, LANES)


def _small_rows(shape):
    return -(-math.prod(shape) // (SUBLANES * LANES)) * SUBLANES


def _pack_small(parts):
    rows = jnp.concatenate([_to2d(p) for p in parts], axis=0)
    pad = (-rows.shape[0]) % 256
    if pad:
        rows = jnp.concatenate([rows, jnp.zeros((pad, LANES), rows.dtype)], axis=0)
    return rows


def _unpack_small(rows, shapes):
    out, r = [], 0
    for shp in shapes:
        size = math.prod(shp)
        nrow = _small_rows(shp)
        out.append(rows[r:r + nrow].reshape(-1)[:size].reshape(shp))
        r += nrow
    return out


def kernel(x, mem, g_ffn1, w1_gate, w1_up, w1_down, g_mix, w_in, g_v, w_s, b_s, sinks, g_a_out, g_b_out, w_out, g_x, g_mem, w_xq, w_xkv, w_xo, g_ffn2, w2_gate, w2_up, w2_down, g_final, loss_target, m_g_ffn1, m_w1_gate, m_w1_up, m_w1_down, m_g_mix, m_w_in, m_g_v, m_w_s, m_b_s, m_sinks, m_g_a_out, m_g_b_out, m_w_out, m_g_x, m_g_mem, m_w_xq, m_w_xkv, m_w_xo, m_g_ffn2, m_w2_gate, m_w2_up, m_w2_down, m_g_final, v_g_ffn1, v_w1_gate, v_w1_up, v_w1_down, v_g_mix, v_w_in, v_g_v, v_w_s, v_b_s, v_sinks, v_g_a_out, v_g_b_out, v_w_out, v_g_x, v_g_mem, v_w_xq, v_w_xkv, v_w_xo, v_g_ffn2, v_w2_gate, v_w2_up, v_w2_down, v_g_final):
    args = dict(locals())
    Wp = {n: args[n] for n in ALL_W}
    Mp = {n: args["m_" + n] for n in ALL_W}
    Vp = {n: args["v_" + n] for n in ALL_W}

    comm = _Comm({n: Wp[n][0].astype(BF16) for n in BIG})
    W = {n: Wp[n] for n in SMALL}
    W["g_final"] = Wp["g_final"].reshape(1, D_MODEL)
    for n in ("w_s", "b_s"):
        W[n] = Wp[n][0]
    loss, dx, grads = _local_step(x[0], mem[0], loss_target[0], W, comm)
    big_grad = comm.total

    small_shapes = [Wp[n].shape for n in SMALL]
    packed = _pack_small([grads[n].reshape(Wp[n].shape) for n in SMALL] + [loss])
    summed = _allreduce_small(packed, name="allreduce_small")
    small_grad = dict(zip(SMALL, _unpack_small(summed, small_shapes)))
    nrows = sum(_small_rows(s) for s in small_shapes)
    loss_total = summed[nrows, 0]

    grad_out, delta, new_m, new_v = {}, {}, {}, {}
    for n in BIG:
        shp = Wp[n].shape
        g, d, nm, nv = _adamw(Wp[n][0], big_grad[n], Mp[n][0], Vp[n][0], name=f"adamw_{n}")
        grad_out[n], delta[n], new_m[n], new_v[n] = g.reshape(shp), d.reshape(shp), nm.reshape(shp), nv.reshape(shp)
    sw = _pack_small([Wp[n] for n in SMALL])
    sg = _pack_small([small_grad[n] for n in SMALL])
    sm = _pack_small([Mp[n] for n in SMALL])
    sv = _pack_small([Vp[n] for n in SMALL])
    _, d, nm, nv = _adamw(sw, sg, sm, sv, name="adamw_small")
    for n, dd, mm_, vv_ in zip(SMALL, _unpack_small(d, small_shapes), _unpack_small(nm, small_shapes),
                               _unpack_small(nv, small_shapes)):
        grad_out[n], delta[n], new_m[n], new_v[n] = small_grad[n], dd, mm_, vv_

    return (loss_total, dx[None], *[grad_out[n] for n in ALL_W], *[delta[n] for n in ALL_W],
            *[new_m[n] for n in ALL_W], *[new_v[n] for n in ALL_W])
```

```python
import functools
import math

import jax
import jax.numpy as jnp
from jax import lax
from jax.experimental import pallas as pl
from jax.experimental.pallas import tpu as pltpu

F32 = jnp.float32
BF16 = jnp.bfloat16
MESH = pl.DeviceIdType.MESH

D_MODEL = 2048
D_FF = 5632
D_A = 1024
D_B = 1024
CHUNK = 128
A_GROUPS = 8
HEAD_DIM = 64
B_Q_HEADS = 16
B_KV_HEADS = 2
X_HEADS = 4
X_HEAD_DIM = 512
IN_COLS = 3328
O_Q = 2 * D_A
O_K = O_Q + D_B
O_V = O_K + B_KV_HEADS * HEAD_DIM
N_CHIPS = 4
EPS = 1e-5
NEG = -1e30
ADAM_LR = 0.001
ADAM_B1 = 0.9
ADAM_B2 = 0.999
ADAM_EPS = 1e-08
ADAM_WD = 0.01
ADAM_STEP = 10

V7X_VMEM_BYTES = 64 * 1024 * 1024
VMEM_LIMIT = 56 * 1024 * 1024
LANES = 128
SUBLANES = 8


ANY = pl.BlockSpec(memory_space=pl.ANY)


def _params(sem, vmem=VMEM_LIMIT):
    return pltpu.CompilerParams(dimension_semantics=sem, vmem_limit_bytes=vmem)


def _matmul(pairs, *, M, N, K, tm, tn, tk, a_t=False, b_kind="n", out_kind="n", out_dtype=F32,
            scale=1.0, res=None, norm_g=None, order="ij", dep=None, name):
    tm, tn, tk = min(tm, M), min(tn, N), min(tk, K)
    assert M % tm == 0 and N % tn == 0 and K % tk == 0, (name, M, N, K, tm, tn, tk)
    nk = K // tk
    npairs = len(pairs)
    b_t = b_kind == "t"
    ns = N // N_CHIPS

    def ij(g0, g1):
        return (g0, g1) if order == "ij" else (g1, g0)

    def a_map(g0, g1, k):
        i, _ = ij(g0, g1)
        return (k, i) if a_t else (i, k)

    a_spec = pl.BlockSpec((tk, tm) if a_t else (tm, tk), a_map)

    if b_kind == "n":
        b_spec = pl.BlockSpec((tk, tn), lambda g0, g1, k: (k, ij(g0, g1)[1]))
    else:
        b_spec = pl.BlockSpec((tn, tk), lambda g0, g1, k: (ij(g0, g1)[1], k))

    if out_kind == "n":
        o_spec = pl.BlockSpec((tm, tn), lambda g0, g1, k: ij(g0, g1))
        o_shape = jax.ShapeDtypeStruct((M, N), out_dtype)
    else:
        assert tn % ns == 0
        o_spec = pl.BlockSpec((tn // ns, tm, ns), lambda g0, g1, k: (ij(g0, g1)[1], ij(g0, g1)[0], 0))
        o_shape = jax.ShapeDtypeStruct((N_CHIPS, M, ns), out_dtype)

    in_specs, args = [], []
    for a, b in pairs:
        in_specs += [a_spec, b_spec]
        args += [a, b]
    if res is not None:
        in_specs.append(pl.BlockSpec((tm, tn), lambda g0, g1, k: ij(g0, g1)))
        args.append(res)
    if norm_g is not None:
        assert tn == N and out_kind == "n"
        in_specs.append(pl.BlockSpec((1, N), lambda g0, g1, k: (0, 0)))
        args.append(norm_g)
    if dep is not None:
        in_specs.append(ANY)
        args.append(dep)

    dn = (((0,) if a_t else (1,), (1,) if b_t else (0,)), ((), ()))

    def body(*refs):
        pos = 2 * npairs
        res_ref = refs[pos] if res is not None else None
        pos += res is not None
        g_ref = refs[pos] if norm_g is not None else None
        pos += (norm_g is not None) + (dep is not None)
        o_ref = refs[pos]
        n_ref = refs[pos + 1] if norm_g is not None else None
        acc_ref = refs[-1] if nk > 1 else None
        part = None
        for p in range(npairs):
            d = lax.dot_general(refs[2 * p][...], refs[2 * p + 1][...], dn, preferred_element_type=F32)
            part = d if part is None else part + d

        def finish(acc):
            r = acc * scale if scale != 1.0 else acc
            if res_ref is not None:
                r = res_ref[...] + r
            if out_kind == "n":
                o_ref[...] = r.astype(out_dtype)
            else:
                for s in range(tn // ns):
                    o_ref[s] = r[:, s * ns:(s + 1) * ns].astype(out_dtype)
            if n_ref is not None:
                n_ref[...] = (r * _rstd(r) * g_ref[...]).astype(BF16)

        if nk == 1:
            finish(part)
        else:
            k = pl.program_id(2)

            @pl.when(k == 0)
            def _():
                acc_ref[...] = part

            @pl.when((k > 0) & (k < nk - 1))
            def _():
                acc_ref[...] += part

            @pl.when(k == nk - 1)
            def _():
                finish(acc_ref[...] + part)

    grid = (M // tm, N // tn, nk) if order == "ij" else (N // tn, M // tm, nk)
    out_specs, out_shape = o_spec, o_shape
    if norm_g is not None:
        out_specs = [o_spec, pl.BlockSpec((tm, tn), lambda g0, g1, k: ij(g0, g1))]
        out_shape = [o_shape, jax.ShapeDtypeStruct((M, N), BF16)]
    return pl.pallas_call(
        body, name=name, grid=grid, in_specs=in_specs, out_specs=out_specs, out_shape=out_shape,
        scratch_shapes=[pltpu.VMEM((tm, tn), F32)] if nk > 1 else [],
        compiler_params=_params(("parallel", "parallel", "arbitrary")),
    )(*args)


def _rstd(x):
    return lax.rsqrt(jnp.mean(x * x, axis=-1, keepdims=True) + EPS)


def _rms_bwd_math(x, g, dy):
    r = _rstd(x)
    gy = dy * g
    xr = x * r
    dx = r * (gy - xr * jnp.mean(gy * xr, axis=-1, keepdims=True))
    return dx, dy * xr


def _rms_fwd(h, g, *, name, tm=512, dep=None):
    T, Dm = h.shape
    tm = min(tm, T)

    def body(h_ref, g_ref, *rest):
        x = h_ref[...]
        rest[-1][...] = (x * _rstd(x) * g_ref[...]).astype(BF16)

    return pl.pallas_call(
        body, name=name, grid=(T // tm,),
        in_specs=[pl.BlockSpec((tm, Dm), lambda i: (i, 0)), pl.BlockSpec((1, Dm), lambda i: (0, 0))]
        + ([ANY] if dep is not None else []),
        out_specs=pl.BlockSpec((tm, Dm), lambda i: (i, 0)),
        out_shape=jax.ShapeDtypeStruct((T, Dm), BF16),
        compiler_params=_params(("parallel",)),
    )(h, g, *([dep] if dep is not None else []))


def _rms_bwd(h, g, dn, dres, *, name, tm=512, dep=None):
    T, Dm = h.shape
    tm = min(tm, T)
    has_res = dres is not None

    def body(*refs):
        h_ref, g_ref, dn_ref = refs[:3]
        pos = 3
        dres_ref = refs[pos] if has_res else None
        pos += has_res + (dep is not None)
        dh_ref, dhb_ref, dg_ref = refs[pos:pos + 3]
        dx, dgr = _rms_bwd_math(h_ref[...], g_ref[...], dn_ref[...].astype(F32))
        if has_res:
            dx = dres_ref[...] + dx
        dh_ref[...] = dx
        dhb_ref[...] = dx.astype(BF16)
        part = jnp.sum(dgr, axis=0, keepdims=True)

        @pl.when(pl.program_id(0) == 0)
        def _():
            dg_ref[...] = part

        @pl.when(pl.program_id(0) > 0)
        def _():
            dg_ref[...] += part

    row = pl.BlockSpec((tm, Dm), lambda i: (i, 0))
    vec = pl.BlockSpec((1, Dm), lambda i: (0, 0))
    args = [h, g, dn] + ([dres] if has_res else []) + ([dep] if dep is not None else [])
    return pl.pallas_call(
        body, name=name, grid=(T // tm,),
        in_specs=[row, vec, row] + ([row] if has_res else []) + ([ANY] if dep is not None else []),
        out_specs=[row, row, vec],
        out_shape=[jax.ShapeDtypeStruct((T, Dm), F32), jax.ShapeDtypeStruct((T, Dm), BF16),
                   jax.ShapeDtypeStruct((1, Dm), F32)],
        compiler_params=_params(("arbitrary",)),
    )(*args)


def _loss_head(h, g, tgt, *, name, tm=512):
    T, Dm = h.shape
    tm = min(tm, T)

    def body(h_ref, g_ref, t_ref, dh_ref, dhb_ref, dg_ref, loss_ref):
        x = h_ref[...]
        gv = g_ref[...]
        r = _rstd(x)
        diff = x * r * gv - t_ref[...]
        lpart = 0.5 * jnp.sum(jnp.mean(diff * diff, axis=-1, keepdims=True), axis=0, keepdims=True)
        dx, dgr = _rms_bwd_math(x, gv, diff * (1.0 / Dm))
        dh_ref[...] = dx
        dhb_ref[...] = dx.astype(BF16)
        part = jnp.sum(dgr, axis=0, keepdims=True)
        lrow = jnp.broadcast_to(lpart, (1, LANES))

        @pl.when(pl.program_id(0) == 0)
        def _():
            dg_ref[...] = part
            loss_ref[...] = lrow

        @pl.when(pl.program_id(0) > 0)
        def _():
            dg_ref[...] += part
            loss_ref[...] += lrow

    row = pl.BlockSpec((tm, Dm), lambda i: (i, 0))
    vec = pl.BlockSpec((1, Dm), lambda i: (0, 0))
    return pl.pallas_call(
        body, name=name, grid=(T // tm,),
        in_specs=[row, vec, row],
        out_specs=[row, row, vec, pl.BlockSpec((1, LANES), lambda i: (0, 0))],
        out_shape=[jax.ShapeDtypeStruct((T, Dm), F32), jax.ShapeDtypeStruct((T, Dm), BF16),
                   jax.ShapeDtypeStruct((1, Dm), F32), jax.ShapeDtypeStruct((1, LANES), F32)],
        compiler_params=_params(("arbitrary",)),
    )(h, g, tgt)


MXU_COLS = 256
FF_TILE = 2 * MXU_COLS


def _row_block(rows, row_bytes, align, budget=24 * 1024 * 1024):
    fits = [d for d in range(align, rows + 1, align) if rows % d == 0 and 2 * d * row_bytes <= budget]
    assert fits, (rows, row_bytes)
    return fits[-1]


def _swiglu_up(n, wg, wu, *, name, tm=1024, tn=FF_TILE):
    T, Dm = n.shape
    Fd = wg.shape[1]
    tm = min(tm, T)

    def body(n_ref, wg_ref, wu_ref, pg_ref, pu_ref, a_ref):
        x = n_ref[...]
        g = jnp.dot(x, wg_ref[...], preferred_element_type=F32)
        u = jnp.dot(x, wu_ref[...], preferred_element_type=F32)
        sg = jax.nn.sigmoid(g)
        silu = g * sg
        pg_ref[...] = ((sg + silu * (1.0 - sg)) * u).astype(BF16)
        pu_ref[...] = silu.astype(BF16)
        a_ref[...] = (silu * u).astype(BF16)

    wspec = pl.BlockSpec((Dm, tn), lambda j, i: (0, j))
    ospec = pl.BlockSpec((tm, tn), lambda j, i: (i, j))
    oshape = jax.ShapeDtypeStruct((T, Fd), BF16)
    return pl.pallas_call(
        body, name=name, grid=(Fd // tn, T // tm),
        in_specs=[pl.BlockSpec((tm, Dm), lambda j, i: (i, 0)), wspec, wspec],
        out_specs=[ospec, ospec, ospec], out_shape=[oshape, oshape, oshape],
        compiler_params=_params(("parallel", "parallel")),
    )(n, wg, wu)


def _swiglu_bwd_act(dhb, wd, PG, PU, *, name, tm=1024, tn=D_FF // N_CHIPS):
    T, Dm = dhb.shape
    Fd = wd.shape[0]
    tm, tn = min(tm, T), min(tn, Fd)

    def body(dh_ref, wd_ref, pg_ref, pu_ref, dg_ref, du_ref):
        da = 0.5 * lax.dot_general(dh_ref[...], wd_ref[...], (((1,), (1,)), ((), ())), preferred_element_type=F32)
        dg_ref[...] = (da * pg_ref[...].astype(F32)).astype(BF16)
        du_ref[...] = (da * pu_ref[...].astype(F32)).astype(BF16)

    blk = pl.BlockSpec((tm, tn), lambda j, i: (i, j))
    oshape = jax.ShapeDtypeStruct((T, Fd), BF16)
    return pl.pallas_call(
        body, name=name, grid=(Fd // tn, T // tm),
        in_specs=[pl.BlockSpec((tm, Dm), lambda j, i: (i, 0)), pl.BlockSpec((tn, Dm), lambda j, i: (j, 0)), blk, blk],
        out_specs=[blk, blk], out_shape=[oshape, oshape],
        compiler_params=_params(("parallel", "parallel")),
    )(dhb, wd, PG, PU)


_INV_SQRT2 = 0.7071067811865476
_INV_SQRT2PI = 0.3989422804014327


def _erf(x):
    ax = jnp.abs(x)
    t = 1.0 / (1.0 + 0.3275911 * ax)
    poly = t * (0.254829592 + t * (-0.284496736 + t * (1.421413741 + t * (-1.453152027 + t * 1.061405429))))
    y = 1.0 - poly * jnp.exp(-ax * ax)
    return jnp.where(x < 0, -y, y)


def _gelu_cdf(x):
    return 0.5 * (1.0 + _erf(x * _INV_SQRT2))


def _lane_lt64(shape):
    return lax.broadcasted_iota(jnp.int32, shape, len(shape) - 1) < HEAD_DIM


def _dup_half(x, kv):
    rolled = pltpu.roll(x, HEAD_DIM, 1)
    lo = _lane_lt64(x.shape)
    return jnp.where(lo, x, rolled) if kv == 0 else jnp.where(lo, rolled, x)


HEADS_PER_KV = B_Q_HEADS // B_KV_HEADS
PAIRS = HEADS_PER_KV // 2


def _attn_bias():
    shape = (2 * CHUNK, HEADS_PER_KV * CHUNK)
    qpos = (lax.broadcasted_iota(jnp.int32, shape, 1) & (CHUNK - 1)) + CHUNK
    kpos = lax.broadcasted_iota(jnp.int32, shape, 0)
    diff = qpos - kpos
    band = (diff >= 0) & (diff < CHUNK)
    return jnp.stack([jnp.where(band & (kpos >= CHUNK), 0.0, NEG), jnp.where(band, 0.0, NEG)]).astype(F32)


def _stack_heads(tiles, lo):
    parts = []
    for t in tiles:
        parts += [jnp.where(lo, t, 0.0), jnp.where(lo, 0.0, t)]
    return jnp.concatenate(parts, axis=0)


def _unstack_heads(s, lo):
    return [jnp.where(lo, s[2 * p * CHUNK:(2 * p + 1) * CHUNK], s[(2 * p + 1) * CHUNK:(2 * p + 2) * CHUNK])
            for p in range(PAIRS)]


def _stack_sinks(sk_ref, kv):
    return jnp.concatenate([jnp.broadcast_to(sk_ref[:, h:h + 1], (1, CHUNK))
                            for h in range(kv * HEADS_PER_KV, (kv + 1) * HEADS_PER_KV)], axis=1)


def _sgu_forward(z_ref, gv, wsm, bst):
    zu = z_ref[:, 0:D_A]
    zv = z_ref[:, D_A:2 * D_A]
    cu = _gelu_cdf(zu)
    cv = _gelu_cdf(zv)
    u = zu * cu
    v = zv * cv
    rv = _rstd(v)
    vn = (v * rv * gv).astype(BF16)
    svs = []
    for g in range(A_GROUPS):
        sl = slice(g * CHUNK, (g + 1) * CHUNK)
        svs.append(jnp.dot(wsm[g], vn[:, sl], preferred_element_type=F32) + bst[:, g:g + 1])
    sv = jnp.concatenate(svs, axis=1)
    return (zu, zv, cu, cv), u, v, rv, vn, sv


def _masked_ws(ws_ref):
    tril = lax.broadcasted_iota(jnp.int32, (CHUNK, CHUNK), 0) >= lax.broadcasted_iota(jnp.int32, (CHUNK, CHUNK), 1)
    return [jnp.where(tril, ws_ref[g], 0.0).astype(BF16) for g in range(A_GROUPS)], tril


def _attn_probs(qm, kkd, sink, bias):
    s = lax.dot_general(kkd, qm, (((1,), (1,)), ((), ())), preferred_element_type=F32) * (HEAD_DIM ** -0.5) + bias
    m = jnp.maximum(jnp.max(s, axis=0, keepdims=True), sink)
    e = jnp.exp(s - m)
    inv = 1.0 / (jnp.sum(e, axis=0, keepdims=True) + jnp.exp(sink - m))
    return e * inv


def _mixer_fwd(z, gv, ws, bst, sinks, ga, gb, *, name):
    T = z.shape[0]
    nb = T // CHUNK
    kvb = O_K // (2 * CHUNK)

    def body(z_ref, zp_ref, bias_ref, gv_ref, ws_ref, bst_ref, sk_ref, ga_ref, gb_ref, o_ref, p_ref):
        wsm, _ = _masked_ws(ws_ref)
        _, u, _, _, _, sv = _sgu_forward(z_ref, gv_ref[...], wsm, bst_ref[...])
        ya = u * sv
        o_ref[:, 0:D_A] = (ya * _rstd(ya) * ga_ref[...]).astype(BF16)

        mask = bias_ref[...]
        kk = jnp.concatenate([zp_ref[:, 0:CHUNK], z_ref[:, O_K:O_V]], axis=0)
        vv = jnp.concatenate([zp_ref[:, CHUNK:2 * CHUNK], z_ref[:, O_V:IN_COLS]], axis=0)
        lo = _lane_lt64((CHUNK, LANES))
        outs = []
        for kv in range(B_KV_HEADS):
            kkd = _dup_half(kk, kv).astype(BF16)
            vvd = _dup_half(vv, kv).astype(BF16)
            q = _stack_heads([z_ref[:, O_Q + (kv * PAIRS + pr) * LANES:O_Q + (kv * PAIRS + pr + 1) * LANES]
                              for pr in range(PAIRS)], lo).astype(BF16)
            p = _attn_probs(q, kkd, _stack_sinks(sk_ref, kv), mask)
            p_ref[kv] = p
            out = lax.dot_general(p.astype(BF16), vvd, (((0,), (0,)), ((), ())), preferred_element_type=F32)
            outs += _unstack_heads(out, lo)
        yb = jnp.concatenate(outs, axis=1)
        o_ref[:, D_A:D_A + D_B] = (yb * _rstd(yb) * gb_ref[...]).astype(BF16)

    full = lambda shape: pl.BlockSpec(shape, lambda i: (0,) * len(shape))
    pshape = (B_KV_HEADS, 2 * CHUNK, HEADS_PER_KV * CHUNK)
    return pl.pallas_call(
        body, name=name, grid=(nb,),
        in_specs=[pl.BlockSpec((CHUNK, IN_COLS), lambda i: (i, 0)),
                  pl.BlockSpec((CHUNK, 2 * CHUNK), lambda i: (jnp.maximum(i - 1, 0), kvb)),
                  pl.BlockSpec((None, 2 * CHUNK, HEADS_PER_KV * CHUNK), lambda i: (jnp.minimum(i, 1), 0, 0)),
                  full((1, D_A)), full((A_GROUPS, CHUNK, CHUNK)), full((CHUNK, A_GROUPS)), full((1, B_Q_HEADS)),
                  full((1, D_A)), full((1, D_B))],
        out_specs=[pl.BlockSpec((CHUNK, D_A + D_B), lambda i: (i, 0)),
                   pl.BlockSpec((None,) + pshape, lambda i: (i, 0, 0, 0))],
        out_shape=[jax.ShapeDtypeStruct((T, D_A + D_B), BF16), jax.ShapeDtypeStruct((nb,) + pshape, F32)],
        compiler_params=_params(("parallel",)),
    )(z, z, _attn_bias(), gv, ws, bst, sinks, ga, gb)


def _mixer_bwd(z, dyn, probs, gv, ws, bst, ga, gb, *, name):
    T = z.shape[0]
    nb = T // CHUNK
    kvb = O_K // (2 * CHUNK)
    NT = (((0,), (0,)), ((), ()))

    def body(z_ref, zp_ref, dy_ref, p_ref, gv_ref, ws_ref, bst_ref, ga_ref, gb_ref,
             dz_ref, dgv_ref, dws_ref, dbst_ref, dsk_ref, dga_ref, dgb_ref, carry_ref):
        step = pl.program_id(0)

        @pl.when(step == 0)
        def _():
            carry_ref[...] = jnp.zeros_like(carry_ref)
            dgv_ref[...] = jnp.zeros_like(dgv_ref)
            dws_ref[...] = jnp.zeros_like(dws_ref)
            dbst_ref[...] = jnp.zeros_like(dbst_ref)
            dsk_ref[...] = jnp.zeros_like(dsk_ref)
            dga_ref[...] = jnp.zeros_like(dga_ref)
            dgb_ref[...] = jnp.zeros_like(dgb_ref)

        wsm, tril = _masked_ws(ws_ref)
        gvv = gv_ref[...]
        (zu, zv, cu, cv), u, v, rv, vn, sv = _sgu_forward(z_ref, gvv, wsm, bst_ref[...])
        ya = u * sv
        dya, dga_rows = _rms_bwd_math(ya, ga_ref[...], dy_ref[:, 0:D_A].astype(F32))
        dga_ref[...] += jnp.sum(dga_rows, axis=0, keepdims=True)
        du = dya * sv
        dsv = dya * u
        dvn_parts = []
        for g in range(A_GROUPS):
            sl = slice(g * CHUNK, (g + 1) * CHUNK)
            dsv_g = dsv[:, sl]
            dsv_gb = dsv_g.astype(BF16)
            dw = lax.dot_general(dsv_gb, vn[:, sl], (((1,), (1,)), ((), ())), preferred_element_type=F32)
            dws_ref[g] += jnp.where(tril, dw, 0.0)
            dbst_ref[:, g:g + 1] += jnp.sum(dsv_g, axis=1, keepdims=True)
            dvn_parts.append(lax.dot_general(wsm[g], dsv_gb, NT, preferred_element_type=F32))
        dvn = jnp.concatenate(dvn_parts, axis=1)
        dv, dgv_rows = _rms_bwd_math(v, gvv, dvn)
        dgv_ref[...] += jnp.sum(dgv_rows, axis=0, keepdims=True)
        dz_ref[:, 0:D_A] = (du * (cu + zu * jnp.exp(-0.5 * zu * zu) * _INV_SQRT2PI)).astype(BF16)
        dz_ref[:, D_A:2 * D_A] = (dv * (cv + zv * jnp.exp(-0.5 * zv * zv) * _INV_SQRT2PI)).astype(BF16)

        kk = jnp.concatenate([zp_ref[:, 0:CHUNK], z_ref[:, O_K:O_V]], axis=0)
        vv = jnp.concatenate([zp_ref[:, CHUNK:2 * CHUNK], z_ref[:, O_V:IN_COLS]], axis=0)
        lo = _lane_lt64((CHUNK, LANES))
        kkd = [_dup_half(kk, kv).astype(BF16) for kv in range(B_KV_HEADS)]
        vvd = [_dup_half(vv, kv).astype(BF16) for kv in range(B_KV_HEADS)]
        outs = []
        for kv in range(B_KV_HEADS):
            outs += _unstack_heads(lax.dot_general(p_ref[kv].astype(BF16), vvd[kv], NT, preferred_element_type=F32), lo)
        yb = jnp.concatenate(outs, axis=1)
        dyb, dgb_rows = _rms_bwd_math(yb, gb_ref[...], dy_ref[:, D_A:D_A + D_B].astype(F32))
        dgb_ref[...] += jnp.sum(dgb_rows, axis=0, keepdims=True)

        dkk, dvv = [], []
        for kv in range(B_KV_HEADS):
            do = _stack_heads([dyb[:, (kv * PAIRS + pr) * LANES:(kv * PAIRS + pr + 1) * LANES]
                               for pr in range(PAIRS)], lo).astype(BF16)
            q = _stack_heads([z_ref[:, O_Q + (kv * PAIRS + pr) * LANES:O_Q + (kv * PAIRS + pr + 1) * LANES]
                              for pr in range(PAIRS)], lo).astype(BF16)
            p = p_ref[kv]
            dvv.append(jnp.dot(p.astype(BF16), do, preferred_element_type=F32))
            dp = lax.dot_general(vvd[kv], do, (((1,), (1,)), ((), ())), preferred_element_type=F32)
            delta = jnp.sum(p * dp, axis=0, keepdims=True)
            dsink = (jnp.sum(p, axis=0, keepdims=True) - 1.0) * delta
            for g in range(HEADS_PER_KV):
                h = kv * HEADS_PER_KV + g
                dsk_ref[:, h:h + 1] += jnp.sum(dsink[:, g * CHUNK:(g + 1) * CHUNK], axis=1, keepdims=True)
            ds = (p * (dp - delta) * (HEAD_DIM ** -0.5)).astype(BF16)
            dq = _unstack_heads(lax.dot_general(ds, kkd[kv], NT, preferred_element_type=F32), lo)
            for pr in range(PAIRS):
                c0 = O_Q + (kv * PAIRS + pr) * LANES
                dz_ref[:, c0:c0 + LANES] = dq[pr].astype(BF16)
            dkk.append(jnp.dot(ds, q, preferred_element_type=F32))

        def fold(parts):
            tot = [t + pltpu.roll(t, HEAD_DIM, 1) for t in parts]
            return jnp.where(_lane_lt64(tot[0].shape), tot[0], tot[1])

        dk_all = fold(dkk)
        dv_all = fold(dvv)
        dz_ref[:, O_K:O_V] = (dk_all[CHUNK:] + carry_ref[:, 0:CHUNK]).astype(BF16)
        dz_ref[:, O_V:IN_COLS] = (dv_all[CHUNK:] + carry_ref[:, CHUNK:2 * CHUNK]).astype(BF16)
        carry_ref[:, 0:CHUNK] = dk_all[:CHUNK]
        carry_ref[:, CHUNK:2 * CHUNK] = dv_all[:CHUNK]

    full = lambda shape: pl.BlockSpec(shape, lambda s: (0,) * len(shape))
    rev = lambda s: nb - 1 - s
    return pl.pallas_call(
        body, name=name, grid=(nb,),
        in_specs=[pl.BlockSpec((CHUNK, IN_COLS), lambda s: (rev(s), 0)),
                  pl.BlockSpec((CHUNK, 2 * CHUNK), lambda s: (jnp.maximum(rev(s) - 1, 0), kvb)),
                  pl.BlockSpec((CHUNK, D_A + D_B), lambda s: (rev(s), 0)),
                  pl.BlockSpec((None, B_KV_HEADS, 2 * CHUNK, HEADS_PER_KV * CHUNK), lambda s: (rev(s), 0, 0, 0)),
                  full((1, D_A)), full((A_GROUPS, CHUNK, CHUNK)), full((CHUNK, A_GROUPS)),
                  full((1, D_A)), full((1, D_B))],
        out_specs=[pl.BlockSpec((CHUNK, IN_COLS), lambda s: (rev(s), 0)),
                   full((1, D_A)), full((A_GROUPS, CHUNK, CHUNK)), full((CHUNK, A_GROUPS)), full((1, B_Q_HEADS)),
                   full((1, D_A)), full((1, D_B))],
        out_shape=[jax.ShapeDtypeStruct((T, IN_COLS), BF16), jax.ShapeDtypeStruct((1, D_A), F32),
                   jax.ShapeDtypeStruct((A_GROUPS, CHUNK, CHUNK), F32), jax.ShapeDtypeStruct((CHUNK, A_GROUPS), F32),
                   jax.ShapeDtypeStruct((1, B_Q_HEADS), F32), jax.ShapeDtypeStruct((1, D_A), F32),
                   jax.ShapeDtypeStruct((1, D_B), F32)],
        scratch_shapes=[pltpu.VMEM((CHUNK, 2 * CHUNK), F32)],
        compiler_params=_params(("arbitrary",)),
    )(z, z, dyn, probs, gv, ws, bst, ga, gb)


def _xattn_probs(qh, kh):
    s = lax.dot_general(kh, qh, (((1,), (1,)), ((), ())), preferred_element_type=F32) * (X_HEAD_DIM ** -0.5)
    e = jnp.exp(s - jnp.max(s, axis=0, keepdims=True))
    return e / jnp.sum(e, axis=0, keepdims=True)


def _xattn_fwd(q, kvm, *, name, tm=512):
    T = q.shape[0]
    Mm = kvm.shape[0]
    tm = min(tm, T)

    def body(q_ref, kv_ref, o_ref, p_ref):
        for h in range(X_HEADS):
            sl = slice(h * X_HEAD_DIM, (h + 1) * X_HEAD_DIM)
            kh = kv_ref[:, sl].astype(BF16)
            vh = kv_ref[:, D_MODEL + h * X_HEAD_DIM:D_MODEL + (h + 1) * X_HEAD_DIM].astype(BF16)
            p = _xattn_probs(q_ref[:, sl], kh)
            p_ref[h * Mm:(h + 1) * Mm, :] = p
            o_ref[:, sl] = lax.dot_general(p.astype(BF16), vh, (((0,), (0,)), ((), ())),
                                           preferred_element_type=F32).astype(BF16)

    return pl.pallas_call(
        body, name=name, grid=(T // tm,),
        in_specs=[pl.BlockSpec((tm, D_MODEL), lambda i: (i, 0)), pl.BlockSpec((Mm, 2 * D_MODEL), lambda i: (0, 0))],
        out_specs=[pl.BlockSpec((tm, D_MODEL), lambda i: (i, 0)), pl.BlockSpec((X_HEADS * Mm, tm), lambda i: (0, i))],
        out_shape=[jax.ShapeDtypeStruct((T, D_MODEL), BF16), jax.ShapeDtypeStruct((X_HEADS * Mm, T), F32)],
        compiler_params=_params(("parallel",)),
    )(q, kvm)


def _xattn_bwd(q, kvm, probs, do, *, name, tm=512):
    T = q.shape[0]
    Mm = kvm.shape[0]
    tm = min(tm, T)
    NT = (((0,), (0,)), ((), ()))

    def body(q_ref, kv_ref, p_ref, do_ref, dq_ref, dkv_ref):
        @pl.when(pl.program_id(0) == 0)
        def _():
            dkv_ref[...] = jnp.zeros_like(dkv_ref)

        for h in range(X_HEADS):
            sl = slice(h * X_HEAD_DIM, (h + 1) * X_HEAD_DIM)
            slv = slice(D_MODEL + h * X_HEAD_DIM, D_MODEL + (h + 1) * X_HEAD_DIM)
            kh = kv_ref[:, sl].astype(BF16)
            vh = kv_ref[:, slv].astype(BF16)
            qh = q_ref[:, sl]
            doh = do_ref[:, sl]
            p = p_ref[h * Mm:(h + 1) * Mm, :]
            dkv_ref[:, slv] += jnp.dot(p.astype(BF16), doh, preferred_element_type=F32)
            dp = lax.dot_general(vh, doh, (((1,), (1,)), ((), ())), preferred_element_type=F32)
            ds = (p * (dp - jnp.sum(p * dp, axis=0, keepdims=True)) * (X_HEAD_DIM ** -0.5)).astype(BF16)
            dq_ref[:, sl] = lax.dot_general(ds, kh, NT, preferred_element_type=F32).astype(BF16)
            dkv_ref[:, sl] += jnp.dot(ds, qh, preferred_element_type=F32)

    row = pl.BlockSpec((tm, D_MODEL), lambda i: (i, 0))
    kvs = pl.BlockSpec((Mm, 2 * D_MODEL), lambda i: (0, 0))
    return pl.pallas_call(
        body, name=name, grid=(T // tm,),
        in_specs=[row, kvs, pl.BlockSpec((X_HEADS * Mm, tm), lambda i: (0, i)), row], out_specs=[row, kvs],
        out_shape=[jax.ShapeDtypeStruct((T, D_MODEL), BF16), jax.ShapeDtypeStruct((Mm, 2 * D_MODEL), F32)],
        compiler_params=_params(("arbitrary",)),
    )(q, kvm, probs, do)


def _swiglu_bwd_weights(tag, n, PG, PU, A, wd, dhb):
    T = n.shape[0]
    dG, dU = _swiglu_bwd_act(dhb, wd, PG, PU, name=f"{tag}_bwd_act", tm=1024)
    dwd = _matmul([(A, dhb)], M=D_FF, N=D_MODEL, K=T, tm=FF_TILE, tn=512, tk=T, a_t=True, out_dtype=BF16,
                  scale=0.5, name=f"{tag}_dwd")
    dwg = _matmul([(n, dG)], M=D_MODEL, N=D_FF, K=T, tm=512, tn=D_FF // 2, tk=2048, a_t=True, out_kind="s",
                  out_dtype=BF16, order="ji", name=f"{tag}_dwg")
    dwu = _matmul([(n, dU)], M=D_MODEL, N=D_FF, K=T, tm=512, tn=D_FF // 2, tk=2048, a_t=True, out_kind="s",
                  out_dtype=BF16, order="ji", name=f"{tag}_dwu")
    return dG, dU, dwg, dwu, dwd


def _swiglu_bwd_input(tag, hin, g_norm, dG, dU, wg, wu, dh, dep):
    T = hin.shape[0]
    dn = _matmul([(dG, wg), (dU, wu)], M=T, N=D_MODEL, K=D_FF, tm=512, tn=D_MODEL // 2, tk=D_FF // 2, b_kind="t",
                 out_dtype=BF16, dep=dep, name=f"{tag}_dn")
    return _rms_bwd(hin, g_norm, dn, dh, name=f"{tag}_norm_bwd")


GROUP_FFN1 = ["w1_gate", "w1_up", "w1_down"]
GROUP_MID = ["w_in", "w_out", "w_xq", "w_xkv", "w_xo"]
GROUP_FFN2 = ["w2_gate", "w2_up", "w2_down"]


def _local_step(x, mem, tgt, W, comm):
    T = x.shape[0]
    Mm = mem.shape[0]
    mm = functools.partial(_matmul)

    W = {**W, **comm.gather_now("ffn1_up", ["w1_gate", "w1_up"])}
    tok = comm.gather_start("ffn1_down", ["w1_down"], after=W["w1_up"])
    tok = comm.gather_start("mid", GROUP_MID, after=tok)
    tok = comm.gather_start("ffn2", GROUP_FFN2, after=tok)
    n1 = _rms_fwd(x, W["g_ffn1"], dep=tok, name="f_norm1")
    PG1, PU1, A1 = _swiglu_up(n1, W["w1_gate"], W["w1_up"], name="f_ffn1_up")
    tok = comm.gather_landed("ffn1_down", after=A1)
    tok = comm.gather_landed("mid", after=tok)
    W = {**W, **comm.gather_ready("ffn1_down", after=tok)}
    h1 = mm([(A1, W["w1_down"])], M=T, N=D_MODEL, K=D_FF, tm=512, tn=D_MODEL // 2, tk=D_FF, scale=0.5, res=x,
            order="ji", name="f_ffn1_down")
    n2 = _rms_fwd(h1, W["g_mix"], name="f_norm2")
    W = {**W, **comm.gather_ready("mid", after=n2)}
    z = mm([(n2, W["w_in"])], M=T, N=IN_COLS, K=D_MODEL, tm=512, tn=IN_COLS // 2, tk=D_MODEL, name="f_w_in")
    bst = jnp.transpose(W["b_s"])
    yn, probs = _mixer_fwd(z, W["g_v"], W["w_s"], bst, W["sinks"], W["g_a_out"], W["g_b_out"], name="f_mixer")
    tok = comm.gather_landed("ffn2", after=yn)
    h2, n3 = mm([(yn, W["w_out"])], M=T, N=D_MODEL, K=D_MODEL, tm=512, tn=D_MODEL, tk=D_MODEL, res=h1,
                norm_g=W["g_x"], dep=tok, name="f_w_out")
    memn = _rms_fwd(mem, W["g_mem"], name="f_norm_mem")
    q3 = mm([(n3, W["w_xq"])], M=T, N=D_MODEL, K=D_MODEL, tm=1024, tn=D_MODEL, tk=D_MODEL, out_dtype=BF16,
            name="f_w_xq")
    kvm = mm([(memn, W["w_xkv"])], M=Mm, N=2 * D_MODEL, K=D_MODEL, tm=Mm, tn=1024, tk=D_MODEL, b_kind="n",
             name="f_w_xkv")
    o3, xprobs = _xattn_fwd(q3, kvm, name="f_xattn")
    h3, n4 = mm([(o3, W["w_xo"])], M=T, N=D_MODEL, K=D_MODEL, tm=512, tn=D_MODEL, tk=D_MODEL, res=h2,
                norm_g=W["g_ffn2"], name="f_w_xo")
    W = {**W, **comm.gather_ready("ffn2", after=n4)}
    PG2, PU2, A2 = _swiglu_up(n4, W["w2_gate"], W["w2_up"], name="f_ffn2_up")
    h4 = mm([(A2, W["w2_down"])], M=T, N=D_MODEL, K=D_FF, tm=512, tn=D_MODEL // 2, tk=D_FF, scale=0.5, res=h3,
            order="ji", name="f_ffn2_down")

    grads = {}
    dh4, dh4b, grads["g_final"], loss = _loss_head(h4, W["g_final"], tgt, name="loss_head")
    dG2, dU2, dwg, dwu, dwd = _swiglu_bwd_weights("b_ffn2", n4, PG2, PU2, A2, W["w2_down"], dh4b)
    tok = comm.reduce_pair_start("ffn2", {"w2_gate": dwg, "w2_up": dwu, "w2_down": dwd})
    dh3, dh3b, grads["g_ffn2"] = _swiglu_bwd_input("b_ffn2", h3, W["g_ffn2"], dG2, dU2, W["w2_gate"], W["w2_up"],
                                                   dh4, tok)
    tok = comm.reduce_pair_done("ffn2", after=dh3b)

    mid = {}
    do3 = mm([(dh3b, W["w_xo"])], M=T, N=D_MODEL, K=D_MODEL, tm=512, tn=D_MODEL, tk=D_MODEL, b_kind="t",
             out_dtype=BF16, dep=tok, name="b_do3")
    mid["w_xo"] = mm([(o3, dh3b)], M=D_MODEL, N=D_MODEL, K=T, tm=1024, tn=D_MODEL // 2, tk=4096, a_t=True,
                       out_dtype=BF16, name="b_dw_xo")
    dq3, dkvm = _xattn_bwd(q3, kvm, xprobs, do3, name="b_xattn")
    mid["w_xq"] = mm([(n3, dq3)], M=D_MODEL, N=D_MODEL, K=T, tm=1024, tn=D_MODEL // 2, tk=4096, a_t=True,
                       out_dtype=BF16, name="b_dw_xq")
    dn3 = mm([(dq3, W["w_xq"])], M=T, N=D_MODEL, K=D_MODEL, tm=512, tn=D_MODEL, tk=D_MODEL, b_kind="t",
             out_dtype=BF16, name="b_dn3")
    dh2, dh2b, grads["g_x"] = _rms_bwd(h2, W["g_x"], dn3, dh3, name="b_norm3")
    dkvmb = dkvm.astype(BF16)
    mid["w_xkv"] = mm([(memn, dkvmb)], M=D_MODEL, N=2 * D_MODEL, K=Mm, tm=D_MODEL, tn=1024, tk=Mm, a_t=True,
                        out_kind="s", out_dtype=BF16, name="b_dw_xkv")
    dmemn = mm([(dkvmb, W["w_xkv"])], M=Mm, N=D_MODEL, K=2 * D_MODEL, tm=Mm, tn=D_MODEL, tk=1024, b_kind="t",
               name="b_dmemn")
    _, _, grads["g_mem"] = _rms_bwd(mem, W["g_mem"], dmemn, None, name="b_norm_mem")
    comm.reduce_finish("ffn2", after=dh2b)

    dyn = mm([(dh2b, W["w_out"])], M=T, N=D_MODEL, K=D_MODEL, tm=1024, tn=D_MODEL, tk=D_MODEL, b_kind="t",
             out_dtype=BF16, name="b_dyn")
    mid["w_out"] = mm([(yn, dh2b)], M=D_MODEL, N=D_MODEL, K=T, tm=1024, tn=D_MODEL // 2, tk=4096, a_t=True,
                        out_dtype=BF16, name="b_dw_out")
    dz, grads["g_v"], grads["w_s"], dbst, grads["sinks"], grads["g_a_out"], grads["g_b_out"] = _mixer_bwd(
        z, dyn, probs, W["g_v"], W["w_s"], bst, W["g_a_out"], W["g_b_out"], name="b_mixer")
    grads["b_s"] = jnp.transpose(dbst)
    mid["w_in"] = mm([(n2, dz)], M=D_MODEL, N=IN_COLS, K=T, tm=1024, tn=IN_COLS, tk=1024, a_t=True,
                     out_dtype=BF16, name="b_dw_in")
    tok = comm.reduce_pair_start("mid", mid)
    dn2 = mm([(dz, W["w_in"])], M=T, N=D_MODEL, K=IN_COLS, tm=512, tn=D_MODEL, tk=IN_COLS, b_kind="t",
             out_dtype=BF16, dep=tok, name="b_dn2")
    tok = comm.reduce_pair_done("mid", after=dn2)
    dh1, dh1b, grads["g_mix"] = _rms_bwd(h1, W["g_mix"], dn2, dh2, dep=tok, name="b_norm2")

    dG1, dU1, dwg, dwu, dwd = _swiglu_bwd_weights("b_ffn1", n1, PG1, PU1, A1, W["w1_down"], dh1b)
    comm.reduce_finish("mid", after=dwu)
    tok = comm.reduce_start("ffn1", {"w1_gate": dwg, "w1_up": dwu, "w1_down": dwd})
    dx, _, grads["g_ffn1"] = _swiglu_bwd_input("b_ffn1", x, W["g_ffn1"], dG1, dU1, W["w1_gate"], W["w1_up"], dh1, tok)
    comm.reduce_finish("ffn1", after=dx)
    return loss, dx, grads


BIG = ["w1_gate", "w1_up", "w1_down", "w_in", "w_out", "w_xq", "w_xkv", "w_xo", "w2_gate", "w2_up", "w2_down"]
SMALL = ["g_ffn1", "g_mix", "g_v", "w_s", "b_s", "sinks", "g_a_out", "g_b_out", "g_x", "g_mem", "g_ffn2", "g_final"]
ALL_W = ["g_ffn1", "w1_gate", "w1_up", "w1_down", "g_mix", "w_in", "g_v", "w_s", "b_s", "sinks", "g_a_out",
         "g_b_out", "w_out", "g_x", "g_mem", "w_xq", "w_xkv", "w_xo", "g_ffn2", "w2_gate", "w2_up", "w2_down",
         "g_final"]
ANY = pl.BlockSpec(memory_space=pl.ANY)


def _place():
    x, y, c = lax.axis_index("x"), lax.axis_index("y"), lax.axis_index("c")
    chips = [(1 - x, y), (x, 1 - y), (1 - x, 1 - y)]
    return x, y, c, chips


COL_SHARDED = ("w1_gate", "w1_up", "w2_gate", "w2_up", "w_xkv")


def _gathered_shape(shape, by_cols):
    rows, cols = shape
    return (rows, N_CHIPS * cols) if by_cols else (N_CHIPS, rows, cols)


def _owner_rows(ref, shape, by_cols, slot, r0, rows):
    cols = shape[1]
    if by_cols:
        return ref.at[pl.ds(r0, rows), pl.ds(pl.multiple_of(slot * cols, LANES), cols)]
    return ref.at[slot, pl.ds(r0, rows), :]


def _allgather_weights(shards, by_cols, *, name):
    n = len(shards)

    def body(*refs):
        ins, outs = refs[:n], refs[n:2 * n]
        send, recv, loc = refs[2 * n:]
        x, y, c, chips = _place()
        me = 2 * x + y
        sib = (x, y, 1 - c)

        def half(w, slot, hc):
            h = shards[w].shape[0] // 2
            return _owner_rows(outs[w], shards[w].shape, by_cols[w], slot, hc * h, h)

        def copy(w, k, slot, hc, to, src=None):
            return pltpu.make_async_remote_copy(
                src_ref=half(w, slot, hc) if src is None else src, dst_ref=half(w, slot, hc),
                send_sem=send.at[6 * w + k], recv_sem=recv.at[6 * w + k], device_id=to, device_id_type=MESH)

        own = [pltpu.make_async_remote_copy(
            src_ref=ins[w], dst_ref=_owner_rows(outs[w], shards[w].shape, by_cols[w], me, 0, shards[w].shape[0]),
            send_sem=loc.at[w], recv_sem=loc.at[n + w], device_id=sib, device_id_type=MESH) for w in range(n)]
        for cp in own:
            cp.start()
        first = []
        for w in range(n):
            h = shards[w].shape[0] // 2
            for j, (tx, ty) in enumerate(chips):
                first.append(copy(w, j, me, c, (tx, ty, c), src=ins[w].at[pl.ds(c * h, h), :]))
                first[-1].start()
        passed = []
        for w in range(n):
            for j, (tx, ty) in enumerate(chips):
                slot = 2 * tx + ty
                copy(w, j, slot, c, (tx, ty, c)).wait_recv()
                passed.append(copy(w, 3 + j, slot, c, sib))
                passed[-1].start()
        for w in range(n):
            for j, (tx, ty) in enumerate(chips):
                copy(w, 3 + j, 2 * tx + ty, 1 - c, sib).wait_recv()
        for cp in first + passed:
            cp.wait_send()
        for cp in own:
            cp.wait()

    return pl.pallas_call(
        body, name=name, in_specs=[ANY] * n, out_specs=[ANY] * n,
        out_shape=[jax.ShapeDtypeStruct(_gathered_shape(s.shape, bc), s.dtype) for s, bc in zip(shards, by_cols)],
        scratch_shapes=[pltpu.SemaphoreType.DMA((6 * n,)), pltpu.SemaphoreType.DMA((6 * n,)),
                        pltpu.SemaphoreType.DMA((2 * n,))],
    )(*shards)


def _pair_exchange(grads, *, name):
    n = len(grads)

    def body(*refs):
        ins, outs = refs[:n], refs[n:2 * n]
        send, recv = refs[2 * n:]
        x, y, c, _ = _place()
        cps = []
        for w in range(n):
            h = grads[w].shape[1] // 2
            cps.append(pltpu.make_async_remote_copy(
                src_ref=ins[w].at[:, pl.ds((1 - c) * h, h), :], dst_ref=outs[w],
                send_sem=send.at[w], recv_sem=recv.at[w], device_id=(x, y, 1 - c), device_id_type=MESH))
            cps[-1].start()
        for cp in cps:
            cp.wait()

    return pl.pallas_call(
        body, name=name, in_specs=[ANY] * n, out_specs=[ANY] * n,
        out_shape=[jax.ShapeDtypeStruct((N_CHIPS, g.shape[1] // 2, g.shape[2]), g.dtype) for g in grads],
        scratch_shapes=[pltpu.SemaphoreType.DMA((n,)), pltpu.SemaphoreType.DMA((n,))],
    )(*grads)


def _pair_sum(g, got, *, name):
    S, R, C = g.shape
    h = R // 2
    tr = _row_block(h, 3 * C * 2, 16)
    nr = h // tr

    def body(a_ref, b_ref, o_ref):
        o_ref[...] = (a_ref[...].astype(F32) + b_ref[...].astype(F32)).astype(BF16)

    return pl.pallas_call(
        body, name=name, grid=(S, nr),
        in_specs=[pl.BlockSpec((None, tr, C), lambda s, r: (s, lax.axis_index("c") * nr + r, 0)),
                  pl.BlockSpec((None, tr, C), lambda s, r: (s, r, 0))],
        out_specs=pl.BlockSpec((None, tr, C), lambda s, r: (s, r, 0)),
        out_shape=jax.ShapeDtypeStruct((S, h, C), BF16),
        compiler_params=_params(("parallel", "parallel")),
    )(g, got)


def _chip_sum(part, got, *, name):
    S, h, C = part.shape
    tr = _row_block(h, 4 * C * 2 + C * 4, 16)
    nr = h // tr

    def body(own_ref, g0_ref, g1_ref, g2_ref, o_ref):
        acc = own_ref[...].astype(F32) + g0_ref[...].astype(F32)
        o_ref[...] = (acc + g1_ref[...].astype(F32)) + g2_ref[...].astype(F32)

    def piece(j):
        return pl.BlockSpec((None, tr, C), lambda r: (j, r, 0))

    return pl.pallas_call(
        body, name=name, grid=(nr,),
        in_specs=[pl.BlockSpec((None, tr, C), lambda r: (2 * lax.axis_index("x") + lax.axis_index("y"), r, 0)),
                  piece(0), piece(1), piece(2)],
        out_specs=pl.BlockSpec((tr, C), lambda r: (lax.axis_index("c") * nr + r, 0)),
        out_shape=jax.ShapeDtypeStruct((2 * h, C), F32),
        compiler_params=_params(("parallel",)),
    )(part, got, got, got)


def _pair_gather(totals, *, name):
    n = len(totals)

    def body(*refs):
        ins, outs = refs[:n], refs[n:2 * n]
        send, recv = refs[2 * n:]
        x, y, c, _ = _place()
        cps = []
        for w in range(n):
            h = totals[w].shape[0] // 2
            cps.append(pltpu.make_async_remote_copy(
                src_ref=ins[w].at[pl.ds(c * h, h), :], dst_ref=outs[w].at[pl.ds(c * h, h), :],
                send_sem=send.at[w], recv_sem=recv.at[w], device_id=(x, y, 1 - c), device_id_type=MESH))
            cps[-1].start()
        for w in range(n):
            h = totals[w].shape[0] // 2
            theirs = outs[w].at[pl.ds((1 - c) * h, h), :]
            pltpu.make_async_remote_copy(
                src_ref=theirs, dst_ref=theirs, send_sem=send.at[w], recv_sem=recv.at[w],
                device_id=(x, y, 1 - c), device_id_type=MESH).wait_recv()
        for cp in cps:
            cp.wait_send()

    return pl.pallas_call(
        body, name=name, in_specs=[ANY] * n, out_specs=[ANY] * n,
        out_shape=[jax.ShapeDtypeStruct(t.shape, t.dtype) for t in totals],
        input_output_aliases={w: w for w in range(n)},
        scratch_shapes=[pltpu.SemaphoreType.DMA((n,)), pltpu.SemaphoreType.DMA((n,))],
    )(*totals)


def _allreduce_small(v, *, name):
    R, C = v.shape
    ND = 8

    def body(v_ref, o_ref, all_ref, send, recv, loc):
        x, y, c, chips = _place()
        me, sib = (x, y, c), (x, y, 1 - c)

        def rows(px, py, pc):
            return all_ref.at[pl.ds((4 * px + 2 * py + pc) * R, R), :]

        def copy(k, block, to, src=None):
            return pltpu.make_async_remote_copy(
                src_ref=rows(*block) if src is None else src, dst_ref=rows(*block),
                send_sem=send.at[k], recv_sem=recv.at[k], device_id=to, device_id_type=MESH)

        mine = pltpu.make_async_copy(v_ref, rows(*me), loc)
        mine.start()
        first = [copy(0, me, sib, src=v_ref)]
        first += [copy(1 + j, me, (*chip, c), src=v_ref) for j, chip in enumerate(chips)]
        for cp in first:
            cp.start()
        passed = [copy(4 + j, (*chip, c), sib) for j, chip in enumerate(chips)]
        for j, chip in enumerate(chips):
            copy(1 + j, (*chip, c), me).wait_recv()
            passed[j].start()
        copy(0, sib, me).wait_recv()
        for j, chip in enumerate(chips):
            copy(4 + j, (*chip, 1 - c), me).wait_recv()
        for cp in first + passed:
            cp.wait_send()
        mine.wait()
        acc = all_ref[0:R, :]
        for d in range(1, ND):
            acc = acc + all_ref[d * R:(d + 1) * R, :]
        o_ref[...] = acc

    vm = pl.BlockSpec(memory_space=pltpu.VMEM)
    return pl.pallas_call(
        body, name=name, in_specs=[vm], out_specs=[vm, vm],
        out_shape=[jax.ShapeDtypeStruct((R, C), F32), jax.ShapeDtypeStruct((ND * R, C), F32)],
        scratch_shapes=[pltpu.SemaphoreType.DMA((7,)), pltpu.SemaphoreType.DMA((7,)), pltpu.SemaphoreType.DMA],
        compiler_params=pltpu.CompilerParams(vmem_limit_bytes=VMEM_LIMIT),
    )(v)[0]


HBM = pl.BlockSpec(memory_space=pltpu.HBM)
SEM = pl.BlockSpec(memory_space=pltpu.SEMAPHORE)
EFFECT = pltpu.SideEffectType.DATAFLOW_SIDE_EFFECTING


def _remote(src, dst, send, recv, k, to):
    return pltpu.make_async_remote_copy(src_ref=src, dst_ref=dst, send_sem=send.at[k], recv_sem=recv.at[k],
                                        device_id=to, device_id_type=MESH)


def _split_start(bufs, plan, ncopies, *, name, after=None):
    nb = len(bufs)
    extra = [] if after is None else [after]

    def body(*refs):
        pos = nb + len(extra)
        send, recv, token = refs[pos], refs[pos + 1], refs[-1]
        for k, (src, dst, to) in enumerate(plan(refs[:nb])):
            _remote(src, dst, send, recv, k, to).start()
        token[...] = jnp.zeros_like(token)

    outs = pl.pallas_call(
        body, name=name,
        out_shape=(pltpu.SemaphoreType.DMA((ncopies,)), pltpu.SemaphoreType.DMA((ncopies,)),
                   *[pltpu.HBM(b.shape, b.dtype) for b in bufs], jax.ShapeDtypeStruct((SUBLANES, LANES), F32)),
        in_specs=[HBM] * nb + [ANY] * len(extra),
        out_specs=(SEM, SEM, *[HBM] * nb, pl.BlockSpec(memory_space=pltpu.VMEM)),
        input_output_aliases={i: 2 + i for i in range(nb)},
        compiler_params=pltpu.CompilerParams(has_side_effects=EFFECT),
    )(*[pltpu.with_memory_space_constraint(b, pltpu.HBM) for b in bufs], *extra)
    return outs[0], outs[1], list(outs[2:2 + nb]), outs[-1]


def _split_wait(started, plan, after, *, name):
    send, recv, bufs, _ = started
    nb = len(bufs)

    def body(*refs):
        send_sem, recv_sem = refs[nb], refs[nb + 1]
        for k, (src, dst, to) in enumerate(plan(refs[:nb])):
            cp = _remote(src, dst, send_sem, recv_sem, k, to)
            cp.wait_send()
            cp.wait_recv()

    outs = pl.pallas_call(
        body, name=name,
        out_shape=tuple(pltpu.HBM(b.shape, b.dtype) for b in bufs),
        in_specs=[HBM] * nb + [SEM, SEM, ANY], out_specs=tuple([HBM] * nb),
        input_output_aliases={i: i for i in range(nb)},
        compiler_params=pltpu.CompilerParams(has_side_effects=EFFECT),
    )(*bufs, send, recv, after)
    return list(outs)


def _gather_chip_plan(shapes, by_cols):
    n = len(shapes)

    def plan(refs):
        srcs, lands = refs[:n], refs[n:]
        x, y, c, chips = _place()
        out = []
        for w in range(n):
            h = shapes[w][0] // 2
            mine = _owner_rows(lands[w], shapes[w], by_cols[w], 2 * x + y, c * h, h)
            for tx, ty in chips:
                out.append((srcs[w].at[pl.ds(c * h, h), :], mine, (tx, ty, c)))
        return out

    return plan


def _gather_pair_plan(shapes, by_cols):
    n = len(shapes)

    def plan(refs):
        srcs, lands = refs[:n], refs[n:]
        x, y, c, chips = _place()
        out = []
        for w in range(n):
            h = shapes[w][0] // 2
            for tx, ty in chips:
                half = _owner_rows(lands[w], shapes[w], by_cols[w], 2 * tx + ty, c * h, h)
                out.append((half, half, (x, y, 1 - c)))
            own = _owner_rows(lands[w], shapes[w], by_cols[w], 2 * x + y, 0, shapes[w][0])
            out.append((srcs[w], own, (x, y, 1 - c)))
        return out

    return plan


def _reduce_pair_plan(shapes):
    n = len(shapes)

    def plan(refs):
        local, lands = refs[:n], refs[n:]
        x, y, c, _ = _place()
        out = []
        for w in range(n):
            h = shapes[w][1] // 2
            out.append((local[w].at[:, pl.ds((1 - c) * h, h), :], lands[w], (x, y, 1 - c)))
        return out

    return plan


def _reduce_chip_plan(n):
    def plan(refs):
        parts, lands = refs[:n], refs[n:]
        x, y, c, chips = _place()
        return [(parts[w].at[2 * tx + ty], lands[w].at[j], (tx, ty, c))
                for w in range(n) for j, (tx, ty) in enumerate(chips)]

    return plan


def _as_operands(gathered):
    out = {}
    for n, g in gathered.items():
        if n in COL_SHARDED:
            out[n] = g
        elif n == "w_in":
            out[n] = jnp.transpose(g, (1, 0, 2)).reshape(D_MODEL, IN_COLS)
        else:
            out[n] = g.reshape(g.shape[0] * g.shape[1], g.shape[2])
    return out


def _by_owner(n, g):
    if n == "w_in":
        return jnp.transpose(g.reshape(D_MODEL, N_CHIPS, IN_COLS // N_CHIPS), (1, 0, 2))
    if g.ndim == 2:
        return g.reshape(N_CHIPS, g.shape[0] // N_CHIPS, g.shape[1])
    return g


class _Comm:
    def __init__(self, shards):
        self.shards = shards
        self.total = {}
        self._flight = {}

    def _layout(self, names):
        return [self.shards[n].shape for n in names], [n in COL_SHARDED for n in names]

    def gather_now(self, tag, names):
        _, by_cols = self._layout(names)
        got = _allgather_weights([self.shards[n] for n in names], by_cols, name=f"gather_{tag}")
        return _as_operands(dict(zip(names, got)))

    def gather_start(self, tag, names, after):
        shapes, by_cols = self._layout(names)
        srcs = [self.shards[n] for n in names]
        lands = [lax.empty(_gathered_shape(s.shape, bc), s.dtype) for s, bc in zip(srcs, by_cols)]
        started = _split_start(srcs + lands, _gather_chip_plan(shapes, by_cols), 3 * len(srcs),
                               after=after, name=f"gather_{tag}_chips_start")
        self._flight[tag] = (names, started)
        return started[3]

    def gather_landed(self, tag, after):
        names, started = self._flight[tag]
        shapes, by_cols = self._layout(names)
        bufs = _split_wait(started, _gather_chip_plan(shapes, by_cols), after, name=f"gather_{tag}_chips_wait")
        started = _split_start(bufs, _gather_pair_plan(shapes, by_cols), 4 * len(names),
                               name=f"gather_{tag}_pair_start")
        self._flight[tag] = (names, started)
        return started[3]

    def gather_ready(self, tag, after):
        names, started = self._flight.pop(tag)
        shapes, by_cols = self._layout(names)
        bufs = _split_wait(started, _gather_pair_plan(shapes, by_cols), after, name=f"gather_{tag}_pair_wait")
        return _as_operands(dict(zip(names, bufs[len(names):])))

    def reduce_start(self, tag, grads):
        names = list(grads)
        local = [_by_owner(n, grads[n]) for n in names]
        return self._chip_start(tag, names, local, _pair_exchange(local, name=f"pair_exchange_{tag}"))

    def reduce_pair_start(self, tag, grads):
        names = list(grads)
        local = [_by_owner(n, grads[n]) for n in names]
        lands = [lax.empty((N_CHIPS, g.shape[1] // 2, g.shape[2]), g.dtype) for g in local]
        started = _split_start(local + lands, _reduce_pair_plan([g.shape for g in local]), len(names),
                               name=f"pair_exchange_{tag}_start")
        self._flight[tag] = (names, started)
        return started[3]

    def reduce_pair_done(self, tag, after):
        names, started = self._flight.pop(tag)
        n = len(names)
        bufs = _split_wait(started, _reduce_pair_plan([b.shape for b in started[2][:n]]), after,
                           name=f"pair_exchange_{tag}_wait")
        return self._chip_start(tag, names, bufs[:n], bufs[n:])

    def _chip_start(self, tag, names, local, from_sib):
        parts = [_pair_sum(g, s, name=f"pair_sum_{n}") for n, g, s in zip(names, local, from_sib)]
        lands = [lax.empty((N_CHIPS - 1,) + p.shape[1:], p.dtype) for p in parts]
        self._flight[tag] = (names, _split_start(parts + lands, _reduce_chip_plan(len(names)), 3 * len(names),
                                                 name=f"chip_exchange_{tag}_start"))
        return self._flight[tag][1][3]

    def reduce_finish(self, tag, after):
        names, started = self._flight.pop(tag)
        n = len(names)
        bufs = _split_wait(started, _reduce_chip_plan(n), after, name=f"chip_exchange_{tag}_wait")
        totals = [_chip_sum(p, s, name=f"chip_sum_{nm}") for nm, p, s in zip(names, bufs[:n], bufs[n:])]
        self.total.update(zip(names, _pair_gather(totals, name=f"pair_gather_{tag}")))


def _adamw(w, g, m, v, *, name):
    R, C = w.shape
    tr = _row_block(R, 8 * C * 4, SUBLANES)

    def body(w_ref, g_ref, m_ref, v_ref, go_ref, d_ref, nm_ref, nv_ref):
        gg = g_ref[...]
        go_ref[...] = gg
        m_new = ADAM_B1 * m_ref[...] + (1.0 - ADAM_B1) * gg
        v_new = ADAM_B2 * v_ref[...] + (1.0 - ADAM_B2) * (gg * gg)
        m_hat = m_new / (1.0 - ADAM_B1 ** ADAM_STEP)
        v_hat = v_new / (1.0 - ADAM_B2 ** ADAM_STEP)
        d_ref[...] = -ADAM_LR * (m_hat / (jnp.sqrt(v_hat) + ADAM_EPS) + ADAM_WD * w_ref[...])
        nm_ref[...] = m_new
        nv_ref[...] = v_new

    blk = pl.BlockSpec((tr, C), lambda i: (i, 0))
    shp = jax.ShapeDtypeStruct((R, C), F32)
    return pl.pallas_call(
        body, name=name, grid=(R // tr,), in_specs=[blk] * 4, out_specs=[blk] * 4, out_shape=[shp] * 4,
        compiler_params=_params(("parallel",)),
    )(w, g, m, v)


def _to2d(a):
    flat = a.reshape(-1)
    pad = (-flat.shape[0]) % (SUBLANES * LANES)
    if pad:
        flat = jnp.pad(flat, (0, pad))
    return flat.reshape(-1, LANES)


def _small_rows(shape):
    return -(-math.prod(shape) // (SUBLANES * LANES)) * SUBLANES


def _pack_small(parts):
    rows = jnp.concatenate([_to2d(p) for p in parts], axis=0)
    pad = (-rows.shape[0]) % 256
    if pad:
        rows = jnp.concatenate([rows, jnp.zeros((pad, LANES), rows.dtype)], axis=0)
    return rows


def _unpack_small(rows, shapes):
    out, r = [], 0
    for shp in shapes:
        size = math.prod(shp)
        nrow = _small_rows(shp)
        out.append(rows[r:r + nrow].reshape(-1)[:size].reshape(shp))
        r += nrow
    return out


def kernel(x, mem, g_ffn1, w1_gate, w1_up, w1_down, g_mix, w_in, g_v, w_s, b_s, sinks, g_a_out, g_b_out, w_out, g_x, g_mem, w_xq, w_xkv, w_xo, g_ffn2, w2_gate, w2_up, w2_down, g_final, loss_target, m_g_ffn1, m_w1_gate, m_w1_up, m_w1_down, m_g_mix, m_w_in, m_g_v, m_w_s, m_b_s, m_sinks, m_g_a_out, m_g_b_out, m_w_out, m_g_x, m_g_mem, m_w_xq, m_w_xkv, m_w_xo, m_g_ffn2, m_w2_gate, m_w2_up, m_w2_down, m_g_final, v_g_ffn1, v_w1_gate, v_w1_up, v_w1_down, v_g_mix, v_w_in, v_g_v, v_w_s, v_b_s, v_sinks, v_g_a_out, v_g_b_out, v_w_out, v_g_x, v_g_mem, v_w_xq, v_w_xkv, v_w_xo, v_g_ffn2, v_w2_gate, v_w2_up, v_w2_down, v_g_final):
    args = dict(locals())
    Wp = {n: args[n] for n in ALL_W}
    Mp = {n: args["m_" + n] for n in ALL_W}
    Vp = {n: args["v_" + n] for n in ALL_W}

    comm = _Comm({n: Wp[n][0].astype(BF16) for n in BIG})
    W = {n: Wp[n] for n in SMALL}
    W["g_final"] = Wp["g_final"].reshape(1, D_MODEL)
    for n in ("w_s", "b_s"):
        W[n] = Wp[n][0]
    loss, dx, grads = _local_step(x[0], mem[0], loss_target[0], W, comm)
    big_grad = comm.total

    small_shapes = [Wp[n].shape for n in SMALL]
    packed = _pack_small([grads[n].reshape(Wp[n].shape) for n in SMALL] + [loss])
    summed = _allreduce_small(packed, name="allreduce_small")
    small_grad = dict(zip(SMALL, _unpack_small(summed, small_shapes)))
    nrows = sum(_small_rows(s) for s in small_shapes)
    loss_total = summed[nrows, 0]

    grad_out, delta, new_m, new_v = {}, {}, {}, {}
    for n in BIG:
        shp = Wp[n].shape
        g, d, nm, nv = _adamw(Wp[n][0], big_grad[n], Mp[n][0], Vp[n][0], name=f"adamw_{n}")
        grad_out[n], delta[n], new_m[n], new_v[n] = g.reshape(shp), d.reshape(shp), nm.reshape(shp), nv.reshape(shp)
    sw = _pack_small([Wp[n] for n in SMALL])
    sg = _pack_small([small_grad[n] for n in SMALL])
    sm = _pack_small([Mp[n] for n in SMALL])
    sv = _pack_small([Vp[n] for n in SMALL])
    _, d, nm, nv = _adamw(sw, sg, sm, sv, name="adamw_small")
    for n, dd, mm_, vv_ in zip(SMALL, _unpack_small(d, small_shapes), _unpack_small(nm, small_shapes),
                               _unpack_small(nv, small_shapes)):
        grad_out[n], delta[n], new_m[n], new_v[n] = small_grad[n], dd, mm_, vv_

    return (loss_total, dx[None], *[grad_out[n] for n in ALL_W], *[delta[n] for n in ALL_W],
            *[new_m[n] for n in ALL_W], *[new_v[n] for n in ALL_W])
```

```python
import functools
import math

import jax
import jax.numpy as jnp
from jax import lax
from jax.experimental import pallas as pl
from jax.experimental.pallas import tpu as pltpu

F32 = jnp.float32
BF16 = jnp.bfloat16
MESH = pl.DeviceIdType.MESH

D_MODEL = 2048
D_FF = 5632
D_A = 1024
D_B = 1024
CHUNK = 128
A_GROUPS = 8
HEAD_DIM = 64
B_Q_HEADS = 16
B_KV_HEADS = 2
X_HEADS = 4
X_HEAD_DIM = 512
IN_COLS = 3328
O_Q = 2 * D_A
O_K = O_Q + D_B
O_V = O_K + B_KV_HEADS * HEAD_DIM
N_CHIPS = 4
EPS = 1e-5
NEG = -1e30
ADAM_LR = 0.001
ADAM_B1 = 0.9
ADAM_B2 = 0.999
ADAM_EPS = 1e-08
ADAM_WD = 0.01
ADAM_STEP = 10

V7X_VMEM_BYTES = 64 * 1024 * 1024
VMEM_LIMIT = 56 * 1024 * 1024
LANES = 128
SUBLANES = 8


ANY = pl.BlockSpec(memory_space=pl.ANY)


def _params(sem, vmem=VMEM_LIMIT):
    return pltpu.CompilerParams(dimension_semantics=sem, vmem_limit_bytes=vmem)


def _matmul(pairs, *, M, N, K, tm, tn, tk, a_t=False, b_kind="n", out_kind="n", out_dtype=F32,
            scale=1.0, res=None, norm_g=None, order="ij", dep=None, name):
    tm, tn, tk = min(tm, M), min(tn, N), min(tk, K)
    assert M % tm == 0 and N % tn == 0 and K % tk == 0, (name, M, N, K, tm, tn, tk)
    nk = K // tk
    npairs = len(pairs)
    b_t = b_kind == "t"
    ns = N // N_CHIPS

    def ij(g0, g1):
        return (g0, g1) if order == "ij" else (g1, g0)

    def a_map(g0, g1, k):
        i, _ = ij(g0, g1)
        return (k, i) if a_t else (i, k)

    a_spec = pl.BlockSpec((tk, tm) if a_t else (tm, tk), a_map)

    if b_kind == "n":
        b_spec = pl.BlockSpec((tk, tn), lambda g0, g1, k: (k, ij(g0, g1)[1]))
    else:
        b_spec = pl.BlockSpec((tn, tk), lambda g0, g1, k: (ij(g0, g1)[1], k))

    if out_kind == "n":
        o_spec = pl.BlockSpec((tm, tn), lambda g0, g1, k: ij(g0, g1))
        o_shape = jax.ShapeDtypeStruct((M, N), out_dtype)
    else:
        assert tn % ns == 0
        o_spec = pl.BlockSpec((tn // ns, tm, ns), lambda g0, g1, k: (ij(g0, g1)[1], ij(g0, g1)[0], 0))
        o_shape = jax.ShapeDtypeStruct((N_CHIPS, M, ns), out_dtype)

    in_specs, args = [], []
    for a, b in pairs:
        in_specs += [a_spec, b_spec]
        args += [a, b]
    if res is not None:
        in_specs.append(pl.BlockSpec((tm, tn), lambda g0, g1, k: ij(g0, g1)))
        args.append(res)
    if norm_g is not None:
        assert tn == N and out_kind == "n"
        in_specs.append(pl.BlockSpec((1, N), lambda g0, g1, k: (0, 0)))
        args.append(norm_g)
    if dep is not None:
        in_specs.append(ANY)
        args.append(dep)

    dn = (((0,) if a_t else (1,), (1,) if b_t else (0,)), ((), ()))

    def body(*refs):
        pos = 2 * npairs
        res_ref = refs[pos] if res is not None else None
        pos += res is not None
        g_ref = refs[pos] if norm_g is not None else None
        pos += (norm_g is not None) + (dep is not None)
        o_ref = refs[pos]
        n_ref = refs[pos + 1] if norm_g is not None else None
        acc_ref = refs[-1] if nk > 1 else None
        part = None
        for p in range(npairs):
            d = lax.dot_general(refs[2 * p][...], refs[2 * p + 1][...], dn, preferred_element_type=F32)
            part = d if part is None else part + d

        def finish(acc):
            r = acc * scale if scale != 1.0 else acc
            if res_ref is not None:
                r = res_ref[...] + r
            if out_kind == "n":
                o_ref[...] = r.astype(out_dtype)
            else:
                for s in range(tn // ns):
                    o_ref[s] = r[:, s * ns:(s + 1) * ns].astype(out_dtype)
            if n_ref is not None:
                n_ref[...] = (r * _rstd(r) * g_ref[...]).astype(BF16)

        if nk == 1:
            finish(part)
        else:
            k = pl.program_id(2)

            @pl.when(k == 0)
            def _():
                acc_ref[...] = part

            @pl.when((k > 0) & (k < nk - 1))
            def _():
                acc_ref[...] += part

            @pl.when(k == nk - 1)
            def _():
                finish(acc_ref[...] + part)

    grid = (M // tm, N // tn, nk) if order == "ij" else (N // tn, M // tm, nk)
    out_specs, out_shape = o_spec, o_shape
    if norm_g is not None:
        out_specs = [o_spec, pl.BlockSpec((tm, tn), lambda g0, g1, k: ij(g0, g1))]
        out_shape = [o_shape, jax.ShapeDtypeStruct((M, N), BF16)]
    return pl.pallas_call(
        body, name=name, grid=grid, in_specs=in_specs, out_specs=out_specs, out_shape=out_shape,
        scratch_shapes=[pltpu.VMEM((tm, tn), F32)] if nk > 1 else [],
        compiler_params=_params(("parallel", "parallel", "arbitrary")),
    )(*args)


def _rstd(x):
    return lax.rsqrt(jnp.mean(x * x, axis=-1, keepdims=True) + EPS)


def _rms_bwd_math(x, g, dy):
    r = _rstd(x)
    gy = dy * g
    xr = x * r
    dx = r * (gy - xr * jnp.mean(gy * xr, axis=-1, keepdims=True))
    return dx, dy * xr


def _rms_fwd(h, g, *, name, tm=512, dep=None):
    T, Dm = h.shape
    tm = min(tm, T)

    def body(h_ref, g_ref, *rest):
        x = h_ref[...]
        rest[-1][...] = (x * _rstd(x) * g_ref[...]).astype(BF16)

    return pl.pallas_call(
        body, name=name, grid=(T // tm,),
        in_specs=[pl.BlockSpec((tm, Dm), lambda i: (i, 0)), pl.BlockSpec((1, Dm), lambda i: (0, 0))]
        + ([ANY] if dep is not None else []),
        out_specs=pl.BlockSpec((tm, Dm), lambda i: (i, 0)),
        out_shape=jax.ShapeDtypeStruct((T, Dm), BF16),
        compiler_params=_params(("parallel",)),
    )(h, g, *([dep] if dep is not None else []))


def _rms_bwd(h, g, dn, dres, *, name, tm=512, dep=None):
    T, Dm = h.shape
    tm = min(tm, T)
    has_res = dres is not None

    def body(*refs):
        h_ref, g_ref, dn_ref = refs[:3]
        pos = 3
        dres_ref = refs[pos] if has_res else None
        pos += has_res + (dep is not None)
        dh_ref, dhb_ref, dg_ref = refs[pos:pos + 3]
        dx, dgr = _rms_bwd_math(h_ref[...], g_ref[...], dn_ref[...].astype(F32))
        if has_res:
            dx = dres_ref[...] + dx
        dh_ref[...] = dx
        dhb_ref[...] = dx.astype(BF16)
        part = jnp.sum(dgr, axis=0, keepdims=True)

        @pl.when(pl.program_id(0) == 0)
        def _():
            dg_ref[...] = part

        @pl.when(pl.program_id(0) > 0)
        def _():
            dg_ref[...] += part

    row = pl.BlockSpec((tm, Dm), lambda i: (i, 0))
    vec = pl.BlockSpec((1, Dm), lambda i: (0, 0))
    args = [h, g, dn] + ([dres] if has_res else []) + ([dep] if dep is not None else [])
    return pl.pallas_call(
        body, name=name, grid=(T // tm,),
        in_specs=[row, vec, row] + ([row] if has_res else []) + ([ANY] if dep is not None else []),
        out_specs=[row, row, vec],
        out_shape=[jax.ShapeDtypeStruct((T, Dm), F32), jax.ShapeDtypeStruct((T, Dm), BF16),
                   jax.ShapeDtypeStruct((1, Dm), F32)],
        compiler_params=_params(("arbitrary",)),
    )(*args)


def _loss_head(h, g, tgt, *, name, tm=512):
    T, Dm = h.shape
    tm = min(tm, T)

    def body(h_ref, g_ref, t_ref, dh_ref, dhb_ref, dg_ref, loss_ref):
        x = h_ref[...]
        gv = g_ref[...]
        r = _rstd(x)
        diff = x * r * gv - t_ref[...]
        lpart = 0.5 * jnp.sum(jnp.mean(diff * diff, axis=-1, keepdims=True), axis=0, keepdims=True)
        dx, dgr = _rms_bwd_math(x, gv, diff * (1.0 / Dm))
        dh_ref[...] = dx
        dhb_ref[...] = dx.astype(BF16)
        part = jnp.sum(dgr, axis=0, keepdims=True)
        lrow = jnp.broadcast_to(lpart, (1, LANES))

        @pl.when(pl.program_id(0) == 0)
        def _():
            dg_ref[...] = part
            loss_ref[...] = lrow

        @pl.when(pl.program_id(0) > 0)
        def _():
            dg_ref[...] += part
            loss_ref[...] += lrow

    row = pl.BlockSpec((tm, Dm), lambda i: (i, 0))
    vec = pl.BlockSpec((1, Dm), lambda i: (0, 0))
    return pl.pallas_call(
        body, name=name, grid=(T // tm,),
        in_specs=[row, vec, row],
        out_specs=[row, row, vec, pl.BlockSpec((1, LANES), lambda i: (0, 0))],
        out_shape=[jax.ShapeDtypeStruct((T, Dm), F32), jax.ShapeDtypeStruct((T, Dm), BF16),
                   jax.ShapeDtypeStruct((1, Dm), F32), jax.ShapeDtypeStruct((1, LANES), F32)],
        compiler_params=_params(("arbitrary",)),
    )(h, g, tgt)


MXU_COLS = 256
FF_TILE = 2 * MXU_COLS


def _row_block(rows, row_bytes, align, budget=24 * 1024 * 1024):
    fits = [d for d in range(align, rows + 1, align) if rows % d == 0 and 2 * d * row_bytes <= budget]
    assert fits, (rows, row_bytes)
    return fits[-1]


def _swiglu_up(n, wg, wu, *, name, tm=1024, tn=FF_TILE):
    T, Dm = n.shape
    Fd = wg.shape[1]
    tm = min(tm, T)

    def body(n_ref, wg_ref, wu_ref, pg_ref, pu_ref, a_ref):
        x = n_ref[...]
        g = jnp.dot(x, wg_ref[...], preferred_element_type=F32)
        u = jnp.dot(x, wu_ref[...], preferred_element_type=F32)
        sg = jax.nn.sigmoid(g)
        silu = g * sg
        pg_ref[...] = ((sg + silu * (1.0 - sg)) * u).astype(BF16)
        pu_ref[...] = silu.astype(BF16)
        a_ref[...] = (silu * u).astype(BF16)

    wspec = pl.BlockSpec((Dm, tn), lambda j, i: (0, j))
    ospec = pl.BlockSpec((tm, tn), lambda j, i: (i, j))
    oshape = jax.ShapeDtypeStruct((T, Fd), BF16)
    return pl.pallas_call(
        body, name=name, grid=(Fd // tn, T // tm),
        in_specs=[pl.BlockSpec((tm, Dm), lambda j, i: (i, 0)), wspec, wspec],
        out_specs=[ospec, ospec, ospec], out_shape=[oshape, oshape, oshape],
        compiler_params=_params(("parallel", "parallel")),
    )(n, wg, wu)


def _swiglu_bwd_act(dhb, wd, PG, PU, *, name, tm=1024, tn=FF_TILE):
    T, Dm = dhb.shape
    Fd = wd.shape[0]
    tm, tn = min(tm, T), min(tn, Fd)

    def body(dh_ref, wd_ref, pg_ref, pu_ref, dg_ref, du_ref):
        da = 0.5 * lax.dot_general(dh_ref[...], wd_ref[...], (((1,), (1,)), ((), ())), preferred_element_type=F32)
        dg_ref[...] = (da * pg_ref[...].astype(F32)).astype(BF16)
        du_ref[...] = (da * pu_ref[...].astype(F32)).astype(BF16)

    blk = pl.BlockSpec((tm, tn), lambda j, i: (i, j))
    oshape = jax.ShapeDtypeStruct((T, Fd), BF16)
    return pl.pallas_call(
        body, name=name, grid=(Fd // tn, T // tm),
        in_specs=[pl.BlockSpec((tm, Dm), lambda j, i: (i, 0)), pl.BlockSpec((tn, Dm), lambda j, i: (j, 0)), blk, blk],
        out_specs=[blk, blk], out_shape=[oshape, oshape],
        compiler_params=_params(("parallel", "parallel")),
    )(dhb, wd, PG, PU)


_INV_SQRT2 = 0.7071067811865476
_INV_SQRT2PI = 0.3989422804014327


def _erf(x):
    ax = jnp.abs(x)
    t = 1.0 / (1.0 + 0.3275911 * ax)
    poly = t * (0.254829592 + t * (-0.284496736 + t * (1.421413741 + t * (-1.453152027 + t * 1.061405429))))
    y = 1.0 - poly * jnp.exp(-ax * ax)
    return jnp.where(x < 0, -y, y)


def _gelu_cdf(x):
    return 0.5 * (1.0 + _erf(x * _INV_SQRT2))


def _lane_lt64(shape):
    return lax.broadcasted_iota(jnp.int32, shape, len(shape) - 1) < HEAD_DIM


def _dup_half(x, kv):
    rolled = pltpu.roll(x, HEAD_DIM, 1)
    lo = _lane_lt64(x.shape)
    return jnp.where(lo, x, rolled) if kv == 0 else jnp.where(lo, rolled, x)


HEADS_PER_KV = B_Q_HEADS // B_KV_HEADS
PAIRS = HEADS_PER_KV // 2


def _attn_bias():
    shape = (2 * CHUNK, HEADS_PER_KV * CHUNK)
    qpos = (lax.broadcasted_iota(jnp.int32, shape, 1) & (CHUNK - 1)) + CHUNK
    kpos = lax.broadcasted_iota(jnp.int32, shape, 0)
    diff = qpos - kpos
    band = (diff >= 0) & (diff < CHUNK)
    return jnp.stack([jnp.where(band & (kpos >= CHUNK), 0.0, NEG), jnp.where(band, 0.0, NEG)]).astype(F32)


def _stack_heads(tiles, lo):
    parts = []
    for t in tiles:
        parts += [jnp.where(lo, t, 0.0), jnp.where(lo, 0.0, t)]
    return jnp.concatenate(parts, axis=0)


def _unstack_heads(s, lo):
    return [jnp.where(lo, s[2 * p * CHUNK:(2 * p + 1) * CHUNK], s[(2 * p + 1) * CHUNK:(2 * p + 2) * CHUNK])
            for p in range(PAIRS)]


def _stack_sinks(sk_ref, kv):
    return jnp.concatenate([jnp.broadcast_to(sk_ref[:, h:h + 1], (1, CHUNK))
                            for h in range(kv * HEADS_PER_KV, (kv + 1) * HEADS_PER_KV)], axis=1)


def _sgu_forward(z_ref, gv, wsm, bst):
    zu = z_ref[:, 0:D_A]
    zv = z_ref[:, D_A:2 * D_A]
    cu = _gelu_cdf(zu)
    cv = _gelu_cdf(zv)
    u = zu * cu
    v = zv * cv
    rv = _rstd(v)
    vn = (v * rv * gv).astype(BF16)
    svs = []
    for g in range(A_GROUPS):
        sl = slice(g * CHUNK, (g + 1) * CHUNK)
        svs.append(jnp.dot(wsm[g], vn[:, sl], preferred_element_type=F32) + bst[:, g:g + 1])
    sv = jnp.concatenate(svs, axis=1)
    return (zu, zv, cu, cv), u, v, rv, vn, sv


def _masked_ws(ws_ref):
    tril = lax.broadcasted_iota(jnp.int32, (CHUNK, CHUNK), 0) >= lax.broadcasted_iota(jnp.int32, (CHUNK, CHUNK), 1)
    return [jnp.where(tril, ws_ref[g], 0.0).astype(BF16) for g in range(A_GROUPS)], tril


def _attn_probs(qm, kkd, sink, bias):
    s = lax.dot_general(kkd, qm, (((1,), (1,)), ((), ())), preferred_element_type=F32) * (HEAD_DIM ** -0.5) + bias
    m = jnp.maximum(jnp.max(s, axis=0, keepdims=True), sink)
    e = jnp.exp(s - m)
    inv = 1.0 / (jnp.sum(e, axis=0, keepdims=True) + jnp.exp(sink - m))
    return e * inv


def _mixer_fwd(z, gv, ws, bst, sinks, ga, gb, *, name):
    T = z.shape[0]
    nb = T // CHUNK
    kvb = O_K // (2 * CHUNK)

    def body(z_ref, zp_ref, bias_ref, gv_ref, ws_ref, bst_ref, sk_ref, ga_ref, gb_ref, o_ref, p_ref):
        wsm, _ = _masked_ws(ws_ref)
        _, u, _, _, _, sv = _sgu_forward(z_ref, gv_ref[...], wsm, bst_ref[...])
        ya = u * sv
        o_ref[:, 0:D_A] = (ya * _rstd(ya) * ga_ref[...]).astype(BF16)

        mask = bias_ref[...]
        kk = jnp.concatenate([zp_ref[:, 0:CHUNK], z_ref[:, O_K:O_V]], axis=0)
        vv = jnp.concatenate([zp_ref[:, CHUNK:2 * CHUNK], z_ref[:, O_V:IN_COLS]], axis=0)
        lo = _lane_lt64((CHUNK, LANES))
        outs = []
        for kv in range(B_KV_HEADS):
            kkd = _dup_half(kk, kv).astype(BF16)
            vvd = _dup_half(vv, kv).astype(BF16)
            q = _stack_heads([z_ref[:, O_Q + (kv * PAIRS + pr) * LANES:O_Q + (kv * PAIRS + pr + 1) * LANES]
                              for pr in range(PAIRS)], lo).astype(BF16)
            p = _attn_probs(q, kkd, _stack_sinks(sk_ref, kv), mask)
            p_ref[kv] = p
            out = lax.dot_general(p.astype(BF16), vvd, (((0,), (0,)), ((), ())), preferred_element_type=F32)
            outs += _unstack_heads(out, lo)
        yb = jnp.concatenate(outs, axis=1)
        o_ref[:, D_A:D_A + D_B] = (yb * _rstd(yb) * gb_ref[...]).astype(BF16)

    full = lambda shape: pl.BlockSpec(shape, lambda i: (0,) * len(shape))
    pshape = (B_KV_HEADS, 2 * CHUNK, HEADS_PER_KV * CHUNK)
    return pl.pallas_call(
        body, name=name, grid=(nb,),
        in_specs=[pl.BlockSpec((CHUNK, IN_COLS), lambda i: (i, 0)),
                  pl.BlockSpec((CHUNK, 2 * CHUNK), lambda i: (jnp.maximum(i - 1, 0), kvb)),
                  pl.BlockSpec((None, 2 * CHUNK, HEADS_PER_KV * CHUNK), lambda i: (jnp.minimum(i, 1), 0, 0)),
                  full((1, D_A)), full((A_GROUPS, CHUNK, CHUNK)), full((CHUNK, A_GROUPS)), full((1, B_Q_HEADS)),
                  full((1, D_A)), full((1, D_B))],
        out_specs=[pl.BlockSpec((CHUNK, D_A + D_B), lambda i: (i, 0)),
                   pl.BlockSpec((None,) + pshape, lambda i: (i, 0, 0, 0))],
        out_shape=[jax.ShapeDtypeStruct((T, D_A + D_B), BF16), jax.ShapeDtypeStruct((nb,) + pshape, F32)],
        compiler_params=_params(("parallel",)),
    )(z, z, _attn_bias(), gv, ws, bst, sinks, ga, gb)


def _mixer_bwd(z, dyn, probs, gv, ws, bst, ga, gb, *, name):
    T = z.shape[0]
    nb = T // CHUNK
    kvb = O_K // (2 * CHUNK)
    NT = (((0,), (0,)), ((), ()))

    def body(z_ref, zp_ref, dy_ref, p_ref, gv_ref, ws_ref, bst_ref, ga_ref, gb_ref,
             dz_ref, dgv_ref, dws_ref, dbst_ref, dsk_ref, dga_ref, dgb_ref, carry_ref):
        step = pl.program_id(0)

        @pl.when(step == 0)
        def _():
            carry_ref[...] = jnp.zeros_like(carry_ref)
            dgv_ref[...] = jnp.zeros_like(dgv_ref)
            dws_ref[...] = jnp.zeros_like(dws_ref)
            dbst_ref[...] = jnp.zeros_like(dbst_ref)
            dsk_ref[...] = jnp.zeros_like(dsk_ref)
            dga_ref[...] = jnp.zeros_like(dga_ref)
            dgb_ref[...] = jnp.zeros_like(dgb_ref)

        wsm, tril = _masked_ws(ws_ref)
        gvv = gv_ref[...]
        (zu, zv, cu, cv), u, v, rv, vn, sv = _sgu_forward(z_ref, gvv, wsm, bst_ref[...])
        ya = u * sv
        dya, dga_rows = _rms_bwd_math(ya, ga_ref[...], dy_ref[:, 0:D_A].astype(F32))
        dga_ref[...] += jnp.sum(dga_rows, axis=0, keepdims=True)
        du = dya * sv
        dsv = dya * u
        dvn_parts = []
        for g in range(A_GROUPS):
            sl = slice(g * CHUNK, (g + 1) * CHUNK)
            dsv_g = dsv[:, sl]
            dsv_gb = dsv_g.astype(BF16)
            dw = lax.dot_general(dsv_gb, vn[:, sl], (((1,), (1,)), ((), ())), preferred_element_type=F32)
            dws_ref[g] += jnp.where(tril, dw, 0.0)
            dbst_ref[:, g:g + 1] += jnp.sum(dsv_g, axis=1, keepdims=True)
            dvn_parts.append(lax.dot_general(wsm[g], dsv_gb, NT, preferred_element_type=F32))
        dvn = jnp.concatenate(dvn_parts, axis=1)
        dv, dgv_rows = _rms_bwd_math(v, gvv, dvn)
        dgv_ref[...] += jnp.sum(dgv_rows, axis=0, keepdims=True)
        dz_ref[:, 0:D_A] = (du * (cu + zu * jnp.exp(-0.5 * zu * zu) * _INV_SQRT2PI)).astype(BF16)
        dz_ref[:, D_A:2 * D_A] = (dv * (cv + zv * jnp.exp(-0.5 * zv * zv) * _INV_SQRT2PI)).astype(BF16)

        kk = jnp.concatenate([zp_ref[:, 0:CHUNK], z_ref[:, O_K:O_V]], axis=0)
        vv = jnp.concatenate([zp_ref[:, CHUNK:2 * CHUNK], z_ref[:, O_V:IN_COLS]], axis=0)
        lo = _lane_lt64((CHUNK, LANES))
        kkd = [_dup_half(kk, kv).astype(BF16) for kv in range(B_KV_HEADS)]
        vvd = [_dup_half(vv, kv).astype(BF16) for kv in range(B_KV_HEADS)]
        outs = []
        for kv in range(B_KV_HEADS):
            outs += _unstack_heads(lax.dot_general(p_ref[kv].astype(BF16), vvd[kv], NT, preferred_element_type=F32), lo)
        yb = jnp.concatenate(outs, axis=1)
        dyb, dgb_rows = _rms_bwd_math(yb, gb_ref[...], dy_ref[:, D_A:D_A + D_B].astype(F32))
        dgb_ref[...] += jnp.sum(dgb_rows, axis=0, keepdims=True)

        dkk, dvv = [], []
        for kv in range(B_KV_HEADS):
            do = _stack_heads([dyb[:, (kv * PAIRS + pr) * LANES:(kv * PAIRS + pr + 1) * LANES]
                               for pr in range(PAIRS)], lo).astype(BF16)
            q = _stack_heads([z_ref[:, O_Q + (kv * PAIRS + pr) * LANES:O_Q + (kv * PAIRS + pr + 1) * LANES]
                              for pr in range(PAIRS)], lo).astype(BF16)
            p = p_ref[kv]
            dvv.append(jnp.dot(p.astype(BF16), do, preferred_element_type=F32))
            dp = lax.dot_general(vvd[kv], do, (((1,), (1,)), ((), ())), preferred_element_type=F32)
            delta = jnp.sum(p * dp, axis=0, keepdims=True)
            dsink = (jnp.sum(p, axis=0, keepdims=True) - 1.0) * delta
            for g in range(HEADS_PER_KV):
                h = kv * HEADS_PER_KV + g
                dsk_ref[:, h:h + 1] += jnp.sum(dsink[:, g * CHUNK:(g + 1) * CHUNK], axis=1, keepdims=True)
            ds = (p * (dp - delta) * (HEAD_DIM ** -0.5)).astype(BF16)
            dq = _unstack_heads(lax.dot_general(ds, kkd[kv], NT, preferred_element_type=F32), lo)
            for pr in range(PAIRS):
                c0 = O_Q + (kv * PAIRS + pr) * LANES
                dz_ref[:, c0:c0 + LANES] = dq[pr].astype(BF16)
            dkk.append(jnp.dot(ds, q, preferred_element_type=F32))

        def fold(parts):
            tot = [t + pltpu.roll(t, HEAD_DIM, 1) for t in parts]
            return jnp.where(_lane_lt64(tot[0].shape), tot[0], tot[1])

        dk_all = fold(dkk)
        dv_all = fold(dvv)
        dz_ref[:, O_K:O_V] = (dk_all[CHUNK:] + carry_ref[:, 0:CHUNK]).astype(BF16)
        dz_ref[:, O_V:IN_COLS] = (dv_all[CHUNK:] + carry_ref[:, CHUNK:2 * CHUNK]).astype(BF16)
        carry_ref[:, 0:CHUNK] = dk_all[:CHUNK]
        carry_ref[:, CHUNK:2 * CHUNK] = dv_all[:CHUNK]

    full = lambda shape: pl.BlockSpec(shape, lambda s: (0,) * len(shape))
    rev = lambda s: nb - 1 - s
    return pl.pallas_call(
        body, name=name, grid=(nb,),
        in_specs=[pl.BlockSpec((CHUNK, IN_COLS), lambda s: (rev(s), 0)),
                  pl.BlockSpec((CHUNK, 2 * CHUNK), lambda s: (jnp.maximum(rev(s) - 1, 0), kvb)),
                  pl.BlockSpec((CHUNK, D_A + D_B), lambda s: (rev(s), 0)),
                  pl.BlockSpec((None, B_KV_HEADS, 2 * CHUNK, HEADS_PER_KV * CHUNK), lambda s: (rev(s), 0, 0, 0)),
                  full((1, D_A)), full((A_GROUPS, CHUNK, CHUNK)), full((CHUNK, A_GROUPS)),
                  full((1, D_A)), full((1, D_B))],
        out_specs=[pl.BlockSpec((CHUNK, IN_COLS), lambda s: (rev(s), 0)),
                   full((1, D_A)), full((A_GROUPS, CHUNK, CHUNK)), full((CHUNK, A_GROUPS)), full((1, B_Q_HEADS)),
                   full((1, D_A)), full((1, D_B))],
        out_shape=[jax.ShapeDtypeStruct((T, IN_COLS), BF16), jax.ShapeDtypeStruct((1, D_A), F32),
                   jax.ShapeDtypeStruct((A_GROUPS, CHUNK, CHUNK), F32), jax.ShapeDtypeStruct((CHUNK, A_GROUPS), F32),
                   jax.ShapeDtypeStruct((1, B_Q_HEADS), F32), jax.ShapeDtypeStruct((1, D_A), F32),
                   jax.ShapeDtypeStruct((1, D_B), F32)],
        scratch_shapes=[pltpu.VMEM((CHUNK, 2 * CHUNK), F32)],
        compiler_params=_params(("arbitrary",)),
    )(z, z, dyn, probs, gv, ws, bst, ga, gb)


def _xattn_probs(qh, kh):
    s = lax.dot_general(kh, qh, (((1,), (1,)), ((), ())), preferred_element_type=F32) * (X_HEAD_DIM ** -0.5)
    e = jnp.exp(s - jnp.max(s, axis=0, keepdims=True))
    return e / jnp.sum(e, axis=0, keepdims=True)


def _xattn_fwd(q, kvm, *, name, tm=512):
    T = q.shape[0]
    Mm = kvm.shape[0]
    tm = min(tm, T)

    def body(q_ref, kv_ref, o_ref, p_ref):
        for h in range(X_HEADS):
            sl = slice(h * X_HEAD_DIM, (h + 1) * X_HEAD_DIM)
            kh = kv_ref[:, sl].astype(BF16)
            vh = kv_ref[:, D_MODEL + h * X_HEAD_DIM:D_MODEL + (h + 1) * X_HEAD_DIM].astype(BF16)
            p = _xattn_probs(q_ref[:, sl], kh)
            p_ref[h * Mm:(h + 1) * Mm, :] = p
            o_ref[:, sl] = lax.dot_general(p.astype(BF16), vh, (((0,), (0,)), ((), ())),
                                           preferred_element_type=F32).astype(BF16)

    return pl.pallas_call(
        body, name=name, grid=(T // tm,),
        in_specs=[pl.BlockSpec((tm, D_MODEL), lambda i: (i, 0)), pl.BlockSpec((Mm, 2 * D_MODEL), lambda i: (0, 0))],
        out_specs=[pl.BlockSpec((tm, D_MODEL), lambda i: (i, 0)), pl.BlockSpec((X_HEADS * Mm, tm), lambda i: (0, i))],
        out_shape=[jax.ShapeDtypeStruct((T, D_MODEL), BF16), jax.ShapeDtypeStruct((X_HEADS * Mm, T), F32)],
        compiler_params=_params(("parallel",)),
    )(q, kvm)


def _xattn_bwd(q, kvm, probs, do, *, name, tm=512):
    T = q.shape[0]
    Mm = kvm.shape[0]
    tm = min(tm, T)
    NT = (((0,), (0,)), ((), ()))

    def body(q_ref, kv_ref, p_ref, do_ref, dq_ref, dkv_ref):
        @pl.when(pl.program_id(0) == 0)
        def _():
            dkv_ref[...] = jnp.zeros_like(dkv_ref)

        for h in range(X_HEADS):
            sl = slice(h * X_HEAD_DIM, (h + 1) * X_HEAD_DIM)
            slv = slice(D_MODEL + h * X_HEAD_DIM, D_MODEL + (h + 1) * X_HEAD_DIM)
            kh = kv_ref[:, sl].astype(BF16)
            vh = kv_ref[:, slv].astype(BF16)
            qh = q_ref[:, sl]
            doh = do_ref[:, sl]
            p = p_ref[h * Mm:(h + 1) * Mm, :]
            dkv_ref[:, slv] += jnp.dot(p.astype(BF16), doh, preferred_element_type=F32)
            dp = lax.dot_general(vh, doh, (((1,), (1,)), ((), ())), preferred_element_type=F32)
            ds = (p * (dp - jnp.sum(p * dp, axis=0, keepdims=True)) * (X_HEAD_DIM ** -0.5)).astype(BF16)
            dq_ref[:, sl] = lax.dot_general(ds, kh, NT, preferred_element_type=F32).astype(BF16)
            dkv_ref[:, sl] += jnp.dot(ds, qh, preferred_element_type=F32)

    row = pl.BlockSpec((tm, D_MODEL), lambda i: (i, 0))
    kvs = pl.BlockSpec((Mm, 2 * D_MODEL), lambda i: (0, 0))
    return pl.pallas_call(
        body, name=name, grid=(T // tm,),
        in_specs=[row, kvs, pl.BlockSpec((X_HEADS * Mm, tm), lambda i: (0, i)), row], out_specs=[row, kvs],
        out_shape=[jax.ShapeDtypeStruct((T, D_MODEL), BF16), jax.ShapeDtypeStruct((Mm, 2 * D_MODEL), F32)],
        compiler_params=_params(("arbitrary",)),
    )(q, kvm, probs, do)


def _swiglu_bwd_weights(tag, n, PG, PU, A, wd, dhb):
    T = n.shape[0]
    dG, dU = _swiglu_bwd_act(dhb, wd, PG, PU, name=f"{tag}_bwd_act", tm=1024)
    dwd = _matmul([(A, dhb)], M=D_FF, N=D_MODEL, K=T, tm=FF_TILE, tn=512, tk=T, a_t=True, out_dtype=BF16,
                  scale=0.5, name=f"{tag}_dwd")
    dwg = _matmul([(n, dG)], M=D_MODEL, N=D_FF, K=T, tm=512, tn=D_FF // 2, tk=2048, a_t=True, out_kind="s",
                  out_dtype=BF16, order="ji", name=f"{tag}_dwg")
    dwu = _matmul([(n, dU)], M=D_MODEL, N=D_FF, K=T, tm=512, tn=D_FF // 2, tk=2048, a_t=True, out_kind="s",
                  out_dtype=BF16, order="ji", name=f"{tag}_dwu")
    return dG, dU, dwg, dwu, dwd


def _swiglu_bwd_input(tag, hin, g_norm, dG, dU, wg, wu, dh, dep):
    T = hin.shape[0]
    dn = _matmul([(dG, wg), (dU, wu)], M=T, N=D_MODEL, K=D_FF, tm=512, tn=D_MODEL // 2, tk=D_FF // 2, b_kind="t",
                 out_dtype=BF16, dep=dep, name=f"{tag}_dn")
    return _rms_bwd(hin, g_norm, dn, dh, name=f"{tag}_norm_bwd")


GROUP_FFN1 = ["w1_gate", "w1_up", "w1_down"]
GROUP_MID = ["w_in", "w_out", "w_xq", "w_xkv", "w_xo"]
GROUP_FFN2 = ["w2_gate", "w2_up", "w2_down"]


def _local_step(x, mem, tgt, W, comm):
    T = x.shape[0]
    Mm = mem.shape[0]
    mm = functools.partial(_matmul)

    W = {**W, **comm.gather_now("ffn1_up", ["w1_gate", "w1_up"])}
    tok = comm.gather_start("ffn1_down", ["w1_down"], after=W["w1_up"])
    tok = comm.gather_start("mid", GROUP_MID, after=tok)
    tok = comm.gather_start("ffn2", GROUP_FFN2, after=tok)
    n1 = _rms_fwd(x, W["g_ffn1"], dep=tok, name="f_norm1")
    PG1, PU1, A1 = _swiglu_up(n1, W["w1_gate"], W["w1_up"], name="f_ffn1_up")
    tok = comm.gather_landed("ffn1_down", after=A1)
    tok = comm.gather_landed("mid", after=tok)
    W = {**W, **comm.gather_ready("ffn1_down", after=tok)}
    h1 = mm([(A1, W["w1_down"])], M=T, N=D_MODEL, K=D_FF, tm=512, tn=D_MODEL // 2, tk=D_FF, scale=0.5, res=x,
            order="ji", name="f_ffn1_down")
    n2 = _rms_fwd(h1, W["g_mix"], name="f_norm2")
    W = {**W, **comm.gather_ready("mid", after=n2)}
    z = mm([(n2, W["w_in"])], M=T, N=IN_COLS, K=D_MODEL, tm=512, tn=IN_COLS // 2, tk=D_MODEL, name="f_w_in")
    bst = jnp.transpose(W["b_s"])
    yn, probs = _mixer_fwd(z, W["g_v"], W["w_s"], bst, W["sinks"], W["g_a_out"], W["g_b_out"], name="f_mixer")
    tok = comm.gather_landed("ffn2", after=yn)
    h2, n3 = mm([(yn, W["w_out"])], M=T, N=D_MODEL, K=D_MODEL, tm=512, tn=D_MODEL, tk=D_MODEL, res=h1,
                norm_g=W["g_x"], dep=tok, name="f_w_out")
    memn = _rms_fwd(mem, W["g_mem"], name="f_norm_mem")
    q3 = mm([(n3, W["w_xq"])], M=T, N=D_MODEL, K=D_MODEL, tm=1024, tn=D_MODEL, tk=D_MODEL, out_dtype=BF16,
            name="f_w_xq")
    kvm = mm([(memn, W["w_xkv"])], M=Mm, N=2 * D_MODEL, K=D_MODEL, tm=Mm, tn=1024, tk=D_MODEL, b_kind="n",
             name="f_w_xkv")
    o3, xprobs = _xattn_fwd(q3, kvm, name="f_xattn")
    h3, n4 = mm([(o3, W["w_xo"])], M=T, N=D_MODEL, K=D_MODEL, tm=512, tn=D_MODEL, tk=D_MODEL, res=h2,
                norm_g=W["g_ffn2"], name="f_w_xo")
    W = {**W, **comm.gather_ready("ffn2", after=n4)}
    PG2, PU2, A2 = _swiglu_up(n4, W["w2_gate"], W["w2_up"], name="f_ffn2_up")
    h4 = mm([(A2, W["w2_down"])], M=T, N=D_MODEL, K=D_FF, tm=512, tn=D_MODEL // 2, tk=D_FF, scale=0.5, res=h3,
            order="ji", name="f_ffn2_down")

    grads = {}
    dh4, dh4b, grads["g_final"], loss = _loss_head(h4, W["g_final"], tgt, name="loss_head")
    dG2, dU2, dwg, dwu, dwd = _swiglu_bwd_weights("b_ffn2", n4, PG2, PU2, A2, W["w2_down"], dh4b)
    tok = comm.reduce_pair_start("ffn2", {"w2_gate": dwg, "w2_up": dwu, "w2_down": dwd})
    dh3, dh3b, grads["g_ffn2"] = _swiglu_bwd_input("b_ffn2", h3, W["g_ffn2"], dG2, dU2, W["w2_gate"], W["w2_up"],
                                                   dh4, tok)
    tok = comm.reduce_pair_done("ffn2", after=dh3b)

    mid = {}
    do3 = mm([(dh3b, W["w_xo"])], M=T, N=D_MODEL, K=D_MODEL, tm=512, tn=D_MODEL, tk=D_MODEL, b_kind="t",
             out_dtype=BF16, dep=tok, name="b_do3")
    mid["w_xo"] = mm([(o3, dh3b)], M=D_MODEL, N=D_MODEL, K=T, tm=1024, tn=D_MODEL // 2, tk=4096, a_t=True,
                       out_dtype=BF16, name="b_dw_xo")
    dq3, dkvm = _xattn_bwd(q3, kvm, xprobs, do3, name="b_xattn")
    mid["w_xq"] = mm([(n3, dq3)], M=D_MODEL, N=D_MODEL, K=T, tm=1024, tn=D_MODEL // 2, tk=4096, a_t=True,
                       out_dtype=BF16, name="b_dw_xq")
    dn3 = mm([(dq3, W["w_xq"])], M=T, N=D_MODEL, K=D_MODEL, tm=512, tn=D_MODEL, tk=D_MODEL, b_kind="t",
             out_dtype=BF16, name="b_dn3")
    dh2, dh2b, grads["g_x"] = _rms_bwd(h2, W["g_x"], dn3, dh3, name="b_norm3")
    dkvmb = dkvm.astype(BF16)
    mid["w_xkv"] = mm([(memn, dkvmb)], M=D_MODEL, N=2 * D_MODEL, K=Mm, tm=D_MODEL, tn=1024, tk=Mm, a_t=True,
                        out_kind="s", out_dtype=BF16, name="b_dw_xkv")
    dmemn = mm([(dkvmb, W["w_xkv"])], M=Mm, N=D_MODEL, K=2 * D_MODEL, tm=Mm, tn=D_MODEL, tk=1024, b_kind="t",
               name="b_dmemn")
    _, _, grads["g_mem"] = _rms_bwd(mem, W["g_mem"], dmemn, None, name="b_norm_mem")
    comm.reduce_finish("ffn2", after=dh2b)

    dyn = mm([(dh2b, W["w_out"])], M=T, N=D_MODEL, K=D_MODEL, tm=1024, tn=D_MODEL, tk=D_MODEL, b_kind="t",
             out_dtype=BF16, name="b_dyn")
    mid["w_out"] = mm([(yn, dh2b)], M=D_MODEL, N=D_MODEL, K=T, tm=1024, tn=D_MODEL // 2, tk=4096, a_t=True,
                        out_dtype=BF16, name="b_dw_out")
    dz, grads["g_v"], grads["w_s"], dbst, grads["sinks"], grads["g_a_out"], grads["g_b_out"] = _mixer_bwd(
        z, dyn, probs, W["g_v"], W["w_s"], bst, W["g_a_out"], W["g_b_out"], name="b_mixer")
    grads["b_s"] = jnp.transpose(dbst)
    mid["w_in"] = mm([(n2, dz)], M=D_MODEL, N=IN_COLS, K=T, tm=1024, tn=IN_COLS, tk=1024, a_t=True,
                     out_dtype=BF16, name="b_dw_in")
    tok = comm.reduce_pair_start("mid", mid)
    dn2 = mm([(dz, W["w_in"])], M=T, N=D_MODEL, K=IN_COLS, tm=512, tn=D_MODEL, tk=IN_COLS, b_kind="t",
             out_dtype=BF16, dep=tok, name="b_dn2")
    tok = comm.reduce_pair_done("mid", after=dn2)
    dh1, dh1b, grads["g_mix"] = _rms_bwd(h1, W["g_mix"], dn2, dh2, dep=tok, name="b_norm2")

    dG1, dU1, dwg, dwu, dwd = _swiglu_bwd_weights("b_ffn1", n1, PG1, PU1, A1, W["w1_down"], dh1b)
    comm.reduce_finish("mid", after=dwu)
    tok = comm.reduce_start("ffn1", {"w1_gate": dwg, "w1_up": dwu, "w1_down": dwd})
    dx, _, grads["g_ffn1"] = _swiglu_bwd_input("b_ffn1", x, W["g_ffn1"], dG1, dU1, W["w1_gate"], W["w1_up"], dh1, tok)
    comm.reduce_finish("ffn1", after=dx)
    return loss, dx, grads


BIG = ["w1_gate", "w1_up", "w1_down", "w_in", "w_out", "w_xq", "w_xkv", "w_xo", "w2_gate", "w2_up", "w2_down"]
SMALL = ["g_ffn1", "g_mix", "g_v", "w_s", "b_s", "sinks", "g_a_out", "g_b_out", "g_x", "g_mem", "g_ffn2", "g_final"]
ALL_W = ["g_ffn1", "w1_gate", "w1_up", "w1_down", "g_mix", "w_in", "g_v", "w_s", "b_s", "sinks", "g_a_out",
         "g_b_out", "w_out", "g_x", "g_mem", "w_xq", "w_xkv", "w_xo", "g_ffn2", "w2_gate", "w2_up", "w2_down",
         "g_final"]
ANY = pl.BlockSpec(memory_space=pl.ANY)


def _place():
    x, y, c = lax.axis_index("x"), lax.axis_index("y"), lax.axis_index("c")
    chips = [(1 - x, y), (x, 1 - y), (1 - x, 1 - y)]
    return x, y, c, chips


COL_SHARDED = ("w1_gate", "w1_up", "w2_gate", "w2_up", "w_xkv")


def _gathered_shape(shape, by_cols):
    rows, cols = shape
    return (rows, N_CHIPS * cols) if by_cols else (N_CHIPS, rows, cols)


def _owner_rows(ref, shape, by_cols, slot, r0, rows):
    cols = shape[1]
    if by_cols:
        return ref.at[pl.ds(r0, rows), pl.ds(pl.multiple_of(slot * cols, LANES), cols)]
    return ref.at[slot, pl.ds(r0, rows), :]


def _allgather_weights(shards, by_cols, *, name):
    n = len(shards)

    def body(*refs):
        ins, outs = refs[:n], refs[n:2 * n]
        send, recv, loc = refs[2 * n:]
        x, y, c, chips = _place()
        me = 2 * x + y
        sib = (x, y, 1 - c)

        def half(w, slot, hc):
            h = shards[w].shape[0] // 2
            return _owner_rows(outs[w], shards[w].shape, by_cols[w], slot, hc * h, h)

        def copy(w, k, slot, hc, to, src=None):
            return pltpu.make_async_remote_copy(
                src_ref=half(w, slot, hc) if src is None else src, dst_ref=half(w, slot, hc),
                send_sem=send.at[6 * w + k], recv_sem=recv.at[6 * w + k], device_id=to, device_id_type=MESH)

        own = [pltpu.make_async_remote_copy(
            src_ref=ins[w], dst_ref=_owner_rows(outs[w], shards[w].shape, by_cols[w], me, 0, shards[w].shape[0]),
            send_sem=loc.at[w], recv_sem=loc.at[n + w], device_id=sib, device_id_type=MESH) for w in range(n)]
        for cp in own:
            cp.start()
        first = []
        for w in range(n):
            h = shards[w].shape[0] // 2
            for j, (tx, ty) in enumerate(chips):
                first.append(copy(w, j, me, c, (tx, ty, c), src=ins[w].at[pl.ds(c * h, h), :]))
                first[-1].start()
        passed = []
        for w in range(n):
            for j, (tx, ty) in enumerate(chips):
                slot = 2 * tx + ty
                copy(w, j, slot, c, (tx, ty, c)).wait_recv()
                passed.append(copy(w, 3 + j, slot, c, sib))
                passed[-1].start()
        for w in range(n):
            for j, (tx, ty) in enumerate(chips):
                copy(w, 3 + j, 2 * tx + ty, 1 - c, sib).wait_recv()
        for cp in first + passed:
            cp.wait_send()
        for cp in own:
            cp.wait()

    return pl.pallas_call(
        body, name=name, in_specs=[ANY] * n, out_specs=[ANY] * n,
        out_shape=[jax.ShapeDtypeStruct(_gathered_shape(s.shape, bc), s.dtype) for s, bc in zip(shards, by_cols)],
        scratch_shapes=[pltpu.SemaphoreType.DMA((6 * n,)), pltpu.SemaphoreType.DMA((6 * n,)),
                        pltpu.SemaphoreType.DMA((2 * n,))],
    )(*shards)


def _pair_exchange(grads, *, name):
    n = len(grads)

    def body(*refs):
        ins, outs = refs[:n], refs[n:2 * n]
        send, recv = refs[2 * n:]
        x, y, c, _ = _place()
        cps = []
        for w in range(n):
            h = grads[w].shape[1] // 2
            cps.append(pltpu.make_async_remote_copy(
                src_ref=ins[w].at[:, pl.ds((1 - c) * h, h), :], dst_ref=outs[w],
                send_sem=send.at[w], recv_sem=recv.at[w], device_id=(x, y, 1 - c), device_id_type=MESH))
            cps[-1].start()
        for cp in cps:
            cp.wait()

    return pl.pallas_call(
        body, name=name, in_specs=[ANY] * n, out_specs=[ANY] * n,
        out_shape=[jax.ShapeDtypeStruct((N_CHIPS, g.shape[1] // 2, g.shape[2]), g.dtype) for g in grads],
        scratch_shapes=[pltpu.SemaphoreType.DMA((n,)), pltpu.SemaphoreType.DMA((n,))],
    )(*grads)


def _pair_sum(g, got, *, name):
    S, R, C = g.shape
    h = R // 2
    tr = _row_block(h, 3 * C * 2, 16)
    nr = h // tr

    def body(a_ref, b_ref, o_ref):
        o_ref[...] = (a_ref[...].astype(F32) + b_ref[...].astype(F32)).astype(BF16)

    return pl.pallas_call(
        body, name=name, grid=(S, nr),
        in_specs=[pl.BlockSpec((None, tr, C), lambda s, r: (s, lax.axis_index("c") * nr + r, 0)),
                  pl.BlockSpec((None, tr, C), lambda s, r: (s, r, 0))],
        out_specs=pl.BlockSpec((None, tr, C), lambda s, r: (s, r, 0)),
        out_shape=jax.ShapeDtypeStruct((S, h, C), BF16),
        compiler_params=_params(("parallel", "parallel")),
    )(g, got)


def _chip_sum(part, got, *, name):
    S, h, C = part.shape
    tr = _row_block(h, 4 * C * 2 + C * 4, 16)
    nr = h // tr

    def body(own_ref, g0_ref, g1_ref, g2_ref, o_ref):
        acc = own_ref[...].astype(F32) + g0_ref[...].astype(F32)
        o_ref[...] = (acc + g1_ref[...].astype(F32)) + g2_ref[...].astype(F32)

    def piece(j):
        return pl.BlockSpec((None, tr, C), lambda r: (j, r, 0))

    return pl.pallas_call(
        body, name=name, grid=(nr,),
        in_specs=[pl.BlockSpec((None, tr, C), lambda r: (2 * lax.axis_index("x") + lax.axis_index("y"), r, 0)),
                  piece(0), piece(1), piece(2)],
        out_specs=pl.BlockSpec((tr, C), lambda r: (lax.axis_index("c") * nr + r, 0)),
        out_shape=jax.ShapeDtypeStruct((2 * h, C), F32),
        compiler_params=_params(("parallel",)),
    )(part, got, got, got)


def _pair_gather(totals, *, name):
    n = len(totals)

    def body(*refs):
        ins, outs = refs[:n], refs[n:2 * n]
        send, recv = refs[2 * n:]
        x, y, c, _ = _place()
        cps = []
        for w in range(n):
            h = totals[w].shape[0] // 2
            cps.append(pltpu.make_async_remote_copy(
                src_ref=ins[w].at[pl.ds(c * h, h), :], dst_ref=outs[w].at[pl.ds(c * h, h), :],
                send_sem=send.at[w], recv_sem=recv.at[w], device_id=(x, y, 1 - c), device_id_type=MESH))
            cps[-1].start()
        for w in range(n):
            h = totals[w].shape[0] // 2
            theirs = outs[w].at[pl.ds((1 - c) * h, h), :]
            pltpu.make_async_remote_copy(
                src_ref=theirs, dst_ref=theirs, send_sem=send.at[w], recv_sem=recv.at[w],
                device_id=(x, y, 1 - c), device_id_type=MESH).wait_recv()
        for cp in cps:
            cp.wait_send()

    return pl.pallas_call(
        body, name=name, in_specs=[ANY] * n, out_specs=[ANY] * n,
        out_shape=[jax.ShapeDtypeStruct(t.shape, t.dtype) for t in totals],
        input_output_aliases={w: w for w in range(n)},
        scratch_shapes=[pltpu.SemaphoreType.DMA((n,)), pltpu.SemaphoreType.DMA((n,))],
    )(*totals)


def _allreduce_small(v, *, name):
    R, C = v.shape
    ND = 8

    def body(v_ref, o_ref, all_ref, send, recv, loc):
        x, y, c, chips = _place()
        me, sib = (x, y, c), (x, y, 1 - c)

        def rows(px, py, pc):
            return all_ref.at[pl.ds((4 * px + 2 * py + pc) * R, R), :]

        def copy(k, block, to, src=None):
            return pltpu.make_async_remote_copy(
                src_ref=rows(*block) if src is None else src, dst_ref=rows(*block),
                send_sem=send.at[k], recv_sem=recv.at[k], device_id=to, device_id_type=MESH)

        mine = pltpu.make_async_copy(v_ref, rows(*me), loc)
        mine.start()
        first = [copy(0, me, sib, src=v_ref)]
        first += [copy(1 + j, me, (*chip, c), src=v_ref) for j, chip in enumerate(chips)]
        for cp in first:
            cp.start()
        passed = [copy(4 + j, (*chip, c), sib) for j, chip in enumerate(chips)]
        for j, chip in enumerate(chips):
            copy(1 + j, (*chip, c), me).wait_recv()
            passed[j].start()
        copy(0, sib, me).wait_recv()
        for j, chip in enumerate(chips):
            copy(4 + j, (*chip, 1 - c), me).wait_recv()
        for cp in first + passed:
            cp.wait_send()
        mine.wait()
        acc = all_ref[0:R, :]
        for d in range(1, ND):
            acc = acc + all_ref[d * R:(d + 1) * R, :]
        o_ref[...] = acc

    vm = pl.BlockSpec(memory_space=pltpu.VMEM)
    return pl.pallas_call(
        body, name=name, in_specs=[vm], out_specs=[vm, vm],
        out_shape=[jax.ShapeDtypeStruct((R, C), F32), jax.ShapeDtypeStruct((ND * R, C), F32)],
        scratch_shapes=[pltpu.SemaphoreType.DMA((7,)), pltpu.SemaphoreType.DMA((7,)), pltpu.SemaphoreType.DMA],
        compiler_params=pltpu.CompilerParams(vmem_limit_bytes=VMEM_LIMIT),
    )(v)[0]


HBM = pl.BlockSpec(memory_space=pltpu.HBM)
SEM = pl.BlockSpec(memory_space=pltpu.SEMAPHORE)
EFFECT = pltpu.SideEffectType.DATAFLOW_SIDE_EFFECTING


def _remote(src, dst, send, recv, k, to):
    return pltpu.make_async_remote_copy(src_ref=src, dst_ref=dst, send_sem=send.at[k], recv_sem=recv.at[k],
                                        device_id=to, device_id_type=MESH)


def _split_start(bufs, plan, ncopies, *, name, after=None):
    nb = len(bufs)
    extra = [] if after is None else [after]

    def body(*refs):
        pos = nb + len(extra)
        send, recv, token = refs[pos], refs[pos + 1], refs[-1]
        for k, (src, dst, to) in enumerate(plan(refs[:nb])):
            _remote(src, dst, send, recv, k, to).start()
        token[...] = jnp.zeros_like(token)

    outs = pl.pallas_call(
        body, name=name,
        out_shape=(pltpu.SemaphoreType.DMA((ncopies,)), pltpu.SemaphoreType.DMA((ncopies,)),
                   *[pltpu.HBM(b.shape, b.dtype) for b in bufs], jax.ShapeDtypeStruct((SUBLANES, LANES), F32)),
        in_specs=[HBM] * nb + [ANY] * len(extra),
        out_specs=(SEM, SEM, *[HBM] * nb, pl.BlockSpec(memory_space=pltpu.VMEM)),
        input_output_aliases={i: 2 + i for i in range(nb)},
        compiler_params=pltpu.CompilerParams(has_side_effects=EFFECT),
    )(*[pltpu.with_memory_space_constraint(b, pltpu.HBM) for b in bufs], *extra)
    return outs[0], outs[1], list(outs[2:2 + nb]), outs[-1]


def _split_wait(started, plan, after, *, name):
    send, recv, bufs, _ = started
    nb = len(bufs)

    def body(*refs):
        send_sem, recv_sem = refs[nb], refs[nb + 1]
        for k, (src, dst, to) in enumerate(plan(refs[:nb])):
            cp = _remote(src, dst, send_sem, recv_sem, k, to)
            cp.wait_send()
            cp.wait_recv()

    outs = pl.pallas_call(
        body, name=name,
        out_shape=tuple(pltpu.HBM(b.shape, b.dtype) for b in bufs),
        in_specs=[HBM] * nb + [SEM, SEM, ANY], out_specs=tuple([HBM] * nb),
        input_output_aliases={i: i for i in range(nb)},
        compiler_params=pltpu.CompilerParams(has_side_effects=EFFECT),
    )(*bufs, send, recv, after)
    return list(outs)


def _gather_chip_plan(shapes, by_cols):
    n = len(shapes)

    def plan(refs):
        srcs, lands = refs[:n], refs[n:]
        x, y, c, chips = _place()
        out = []
        for w in range(n):
            h = shapes[w][0] // 2
            mine = _owner_rows(lands[w], shapes[w], by_cols[w], 2 * x + y, c * h, h)
            for tx, ty in chips:
                out.append((srcs[w].at[pl.ds(c * h, h), :], mine, (tx, ty, c)))
        return out

    return plan


def _gather_pair_plan(shapes, by_cols):
    n = len(shapes)

    def plan(refs):
        srcs, lands = refs[:n], refs[n:]
        x, y, c, chips = _place()
        out = []
        for w in range(n):
            h = shapes[w][0] // 2
            for tx, ty in chips:
                half = _owner_rows(lands[w], shapes[w], by_cols[w], 2 * tx + ty, c * h, h)
                out.append((half, half, (x, y, 1 - c)))
            own = _owner_rows(lands[w], shapes[w], by_cols[w], 2 * x + y, 0, shapes[w][0])
            out.append((srcs[w], own, (x, y, 1 - c)))
        return out

    return plan


def _reduce_pair_plan(shapes):
    n = len(shapes)

    def plan(refs):
        local, lands = refs[:n], refs[n:]
        x, y, c, _ = _place()
        out = []
        for w in range(n):
            h = shapes[w][1] // 2
            out.append((local[w].at[:, pl.ds((1 - c) * h, h), :], lands[w], (x, y, 1 - c)))
        return out

    return plan


def _reduce_chip_plan(n):
    def plan(refs):
        parts, lands = refs[:n], refs[n:]
        x, y, c, chips = _place()
        return [(parts[w].at[2 * tx + ty], lands[w].at[j], (tx, ty, c))
                for w in range(n) for j, (tx, ty) in enumerate(chips)]

    return plan


def _as_operands(gathered):
    out = {}
    for n, g in gathered.items():
        if n in COL_SHARDED:
            out[n] = g
        elif n == "w_in":
            out[n] = jnp.transpose(g, (1, 0, 2)).reshape(D_MODEL, IN_COLS)
        else:
            out[n] = g.reshape(g.shape[0] * g.shape[1], g.shape[2])
    return out


def _by_owner(n, g):
    if n == "w_in":
        return jnp.transpose(g.reshape(D_MODEL, N_CHIPS, IN_COLS // N_CHIPS), (1, 0, 2))
    if g.ndim == 2:
        return g.reshape(N_CHIPS, g.shape[0] // N_CHIPS, g.shape[1])
    return g


class _Comm:
    def __init__(self, shards):
        self.shards = shards
        self.total = {}
        self._flight = {}

    def _layout(self, names):
        return [self.shards[n].shape for n in names], [n in COL_SHARDED for n in names]

    def gather_now(self, tag, names):
        _, by_cols = self._layout(names)
        got = _allgather_weights([self.shards[n] for n in names], by_cols, name=f"gather_{tag}")
        return _as_operands(dict(zip(names, got)))

    def gather_start(self, tag, names, after):
        shapes, by_cols = self._layout(names)
        srcs = [self.shards[n] for n in names]
        lands = [lax.empty(_gathered_shape(s.shape, bc), s.dtype) for s, bc in zip(srcs, by_cols)]
        started = _split_start(srcs + lands, _gather_chip_plan(shapes, by_cols), 3 * len(srcs),
                               after=after, name=f"gather_{tag}_chips_start")
        self._flight[tag] = (names, started)
        return started[3]

    def gather_landed(self, tag, after):
        names, started = self._flight[tag]
        shapes, by_cols = self._layout(names)
        bufs = _split_wait(started, _gather_chip_plan(shapes, by_cols), after, name=f"gather_{tag}_chips_wait")
        started = _split_start(bufs, _gather_pair_plan(shapes, by_cols), 4 * len(names),
                               name=f"gather_{tag}_pair_start")
        self._flight[tag] = (names, started)
        return started[3]

    def gather_ready(self, tag, after):
        names, started = self._flight.pop(tag)
        shapes, by_cols = self._layout(names)
        bufs = _split_wait(started, _gather_pair_plan(shapes, by_cols), after, name=f"gather_{tag}_pair_wait")
        return _as_operands(dict(zip(names, bufs[len(names):])))

    def reduce_start(self, tag, grads):
        names = list(grads)
        local = [_by_owner(n, grads[n]) for n in names]
        return self._chip_start(tag, names, local, _pair_exchange(local, name=f"pair_exchange_{tag}"))

    def reduce_pair_start(self, tag, grads):
        names = list(grads)
        local = [_by_owner(n, grads[n]) for n in names]
        lands = [lax.empty((N_CHIPS, g.shape[1] // 2, g.shape[2]), g.dtype) for g in local]
        started = _split_start(local + lands, _reduce_pair_plan([g.shape for g in local]), len(names),
                               name=f"pair_exchange_{tag}_start")
        self._flight[tag] = (names, started)
        return started[3]

    def reduce_pair_done(self, tag, after):
        names, started = self._flight.pop(tag)
        n = len(names)
        bufs = _split_wait(started, _reduce_pair_plan([b.shape for b in started[2][:n]]), after,
                           name=f"pair_exchange_{tag}_wait")
        return self._chip_start(tag, names, bufs[:n], bufs[n:])

    def _chip_start(self, tag, names, local, from_sib):
        parts = [_pair_sum(g, s, name=f"pair_sum_{n}") for n, g, s in zip(names, local, from_sib)]
        lands = [lax.empty((N_CHIPS - 1,) + p.shape[1:], p.dtype) for p in parts]
        self._flight[tag] = (names, _split_start(parts + lands, _reduce_chip_plan(len(names)), 3 * len(names),
                                                 name=f"chip_exchange_{tag}_start"))
        return self._flight[tag][1][3]

    def reduce_finish(self, tag, after):
        names, started = self._flight.pop(tag)
        n = len(names)
        bufs = _split_wait(started, _reduce_chip_plan(n), after, name=f"chip_exchange_{tag}_wait")
        totals = [_chip_sum(p, s, name=f"chip_sum_{nm}") for nm, p, s in zip(names, bufs[:n], bufs[n:])]
        self.total.update(zip(names, _pair_gather(totals, name=f"pair_gather_{tag}")))


def _adamw(w, g, m, v, *, name):
    R, C = w.shape
    tr = _row_block(R, 8 * C * 4, SUBLANES)

    def body(w_ref, g_ref, m_ref, v_ref, go_ref, d_ref, nm_ref, nv_ref):
        gg = g_ref[...]
        go_ref[...] = gg
        m_new = ADAM_B1 * m_ref[...] + (1.0 - ADAM_B1) * gg
        v_new = ADAM_B2 * v_ref[...] + (1.0 - ADAM_B2) * (gg * gg)
        m_hat = m_new / (1.0 - ADAM_B1 ** ADAM_STEP)
        v_hat = v_new / (1.0 - ADAM_B2 ** ADAM_STEP)
        d_ref[...] = -ADAM_LR * (m_hat / (jnp.sqrt(v_hat) + ADAM_EPS) + ADAM_WD * w_ref[...])
        nm_ref[...] = m_new
        nv_ref[...] = v_new

    blk = pl.BlockSpec((tr, C), lambda i: (i, 0))
    shp = jax.ShapeDtypeStruct((R, C), F32)
    return pl.pallas_call(
        body, name=name, grid=(R // tr,), in_specs=[blk] * 4, out_specs=[blk] * 4, out_shape=[shp] * 4,
        compiler_params=_params(("parallel",)),
    )(w, g, m, v)


def _to2d(a):
    flat = a.reshape(-1)
    pad = (-flat.shape[0]) % (SUBLANES * LANES)
    if pad:
        flat = jnp.pad(flat, (0, pad))
    return flat.reshape(-1, LANES)


def _small_rows(shape):
    return -(-math.prod(shape) // (SUBLANES * LANES)) * SUBLANES


def _pack_small(parts):
    rows = jnp.concatenate([_to2d(p) for p in parts], axis=0)
    pad = (-rows.shape[0]) % 256
    if pad:
        rows = jnp.concatenate([rows, jnp.zeros((pad, LANES), rows.dtype)], axis=0)
    return rows


def _unpack_small(rows, shapes):
    out, r = [], 0
    for shp in shapes:
        size = math.prod(shp)
        nrow = _small_rows(shp)
        out.append(rows[r:r + nrow].reshape(-1)[:size].reshape(shp))
        r += nrow
    return out


def kernel(x, mem, g_ffn1, w1_gate, w1_up, w1_down, g_mix, w_in, g_v, w_s, b_s, sinks, g_a_out, g_b_out, w_out, g_x, g_mem, w_xq, w_xkv, w_xo, g_ffn2, w2_gate, w2_up, w2_down, g_final, loss_target, m_g_ffn1, m_w1_gate, m_w1_up, m_w1_down, m_g_mix, m_w_in, m_g_v, m_w_s, m_b_s, m_sinks, m_g_a_out, m_g_b_out, m_w_out, m_g_x, m_g_mem, m_w_xq, m_w_xkv, m_w_xo, m_g_ffn2, m_w2_gate, m_w2_up, m_w2_down, m_g_final, v_g_ffn1, v_w1_gate, v_w1_up, v_w1_down, v_g_mix, v_w_in, v_g_v, v_w_s, v_b_s, v_sinks, v_g_a_out, v_g_b_out, v_w_out, v_g_x, v_g_mem, v_w_xq, v_w_xkv, v_w_xo, v_g_ffn2, v_w2_gate, v_w2_up, v_w2_down, v_g_final):
    args = dict(locals())
    Wp = {n: args[n] for n in ALL_W}
    Mp = {n: args["m_" + n] for n in ALL_W}
    Vp = {n: args["v_" + n] for n in ALL_W}

    comm = _Comm({n: Wp[n][0].astype(BF16) for n in BIG})
    W = {n: Wp[n] for n in SMALL}
    W["g_final"] = Wp["g_final"].reshape(1, D_MODEL)
    for n in ("w_s", "b_s"):
        W[n] = Wp[n][0]
    loss, dx, grads = _local_step(x[0], mem[0], loss_target[0], W, comm)
    big_grad = comm.total

    small_shapes = [Wp[n].shape for n in SMALL]
    packed = _pack_small([grads[n].reshape(Wp[n].shape) for n in SMALL] + [loss])
    summed = _allreduce_small(packed, name="allreduce_small")
    small_grad = dict(zip(SMALL, _unpack_small(summed, small_shapes)))
    nrows = sum(_small_rows(s) for s in small_shapes)
    loss_total = summed[nrows, 0]

    grad_out, delta, new_m, new_v = {}, {}, {}, {}
    for n in BIG:
        shp = Wp[n].shape
        g, d, nm, nv = _adamw(Wp[n][0], big_grad[n], Mp[n][0], Vp[n][0], name=f"adamw_{n}")
        grad_out[n], delta[n], new_m[n], new_v[n] = g.reshape(shp), d.reshape(shp), nm.reshape(shp), nv.reshape(shp)
    sw = _pack_small([Wp[n] for n in SMALL])
    sg = _pack_small([small_grad[n] for n in SMALL])
    sm = _pack_small([Mp[n] for n in SMALL])
    sv = _pack_small([Vp[n] for n in SMALL])
    _, d, nm, nv = _adamw(sw, sg, sm, sv, name="adamw_small")
    for n, dd, mm_, vv_ in zip(SMALL, _unpack_small(d, small_shapes), _unpack_small(nm, small_shapes),
                               _unpack_small(nv, small_shapes)):
        grad_out[n], delta[n], new_m[n], new_v[n] = small_grad[n], dd, mm_, vv_

    return (loss_total, dx[None], *[grad_out[n] for n in ALL_W], *[delta[n] for n in ALL_W],
            *[new_m[n] for n in ALL_W], *[new_v[n] for n in ALL_W])
```

```python
import functools
import math

import jax
import jax.numpy as jnp
from jax import lax
from jax.experimental import pallas as pl
from jax.experimental.pallas import tpu as pltpu

F32 = jnp.float32
BF16 = jnp.bfloat16
MESH = pl.DeviceIdType.MESH

D_MODEL = 2048
D_FF = 5632
D_A = 1024
D_B = 1024
CHUNK = 128
A_GROUPS = 8
HEAD_DIM = 64
B_Q_HEADS = 16
B_KV_HEADS = 2
X_HEADS = 4
X_HEAD_DIM = 512
IN_COLS = 3328
O_Q = 2 * D_A
O_K = O_Q + D_B
O_V = O_K + B_KV_HEADS * HEAD_DIM
N_CHIPS = 4
EPS = 1e-5
NEG = -1e30
ADAM_LR = 0.001
ADAM_B1 = 0.9
ADAM_B2 = 0.999
ADAM_EPS = 1e-08
ADAM_WD = 0.01
ADAM_STEP = 10

V7X_VMEM_BYTES = 64 * 1024 * 1024
VMEM_LIMIT = 56 * 1024 * 1024
LANES = 128
SUBLANES = 8


ANY = pl.BlockSpec(memory_space=pl.ANY)


def _params(sem, vmem=VMEM_LIMIT):
    return pltpu.CompilerParams(dimension_semantics=sem, vmem_limit_bytes=vmem)


def _matmul(pairs, *, M, N, K, tm, tn, tk, a_t=False, b_kind="n", out_kind="n", out_dtype=F32,
            scale=1.0, res=None, norm_g=None, order="ij", dep=None, name):
    tm, tn, tk = min(tm, M), min(tn, N), min(tk, K)
    assert M % tm == 0 and N % tn == 0 and K % tk == 0, (name, M, N, K, tm, tn, tk)
    nk = K // tk
    npairs = len(pairs)
    b_t = b_kind == "t"
    ns = N // N_CHIPS

    def ij(g0, g1):
        return (g0, g1) if order == "ij" else (g1, g0)

    def a_map(g0, g1, k):
        i, _ = ij(g0, g1)
        return (k, i) if a_t else (i, k)

    a_spec = pl.BlockSpec((tk, tm) if a_t else (tm, tk), a_map)

    if b_kind == "n":
        b_spec = pl.BlockSpec((tk, tn), lambda g0, g1, k: (k, ij(g0, g1)[1]))
    else:
        b_spec = pl.BlockSpec((tn, tk), lambda g0, g1, k: (ij(g0, g1)[1], k))

    if out_kind == "n":
        o_spec = pl.BlockSpec((tm, tn), lambda g0, g1, k: ij(g0, g1))
        o_shape = jax.ShapeDtypeStruct((M, N), out_dtype)
    else:
        assert tn % ns == 0
        o_spec = pl.BlockSpec((tn // ns, tm, ns), lambda g0, g1, k: (ij(g0, g1)[1], ij(g0, g1)[0], 0))
        o_shape = jax.ShapeDtypeStruct((N_CHIPS, M, ns), out_dtype)

    in_specs, args = [], []
    for a, b in pairs:
        in_specs += [a_spec, b_spec]
        args += [a, b]
    if res is not None:
        in_specs.append(pl.BlockSpec((tm, tn), lambda g0, g1, k: ij(g0, g1)))
        args.append(res)
    if norm_g is not None:
        assert tn == N and out_kind == "n"
        in_specs.append(pl.BlockSpec((1, N), lambda g0, g1, k: (0, 0)))
        args.append(norm_g)
    if dep is not None:
        in_specs.append(ANY)
        args.append(dep)

    dn = (((0,) if a_t else (1,), (1,) if b_t else (0,)), ((), ()))

    def body(*refs):
        pos = 2 * npairs
        res_ref = refs[pos] if res is not None else None
        pos += res is not None
        g_ref = refs[pos] if norm_g is not None else None
        pos += (norm_g is not None) + (dep is not None)
        o_ref = refs[pos]
        n_ref = refs[pos + 1] if norm_g is not None else None
        acc_ref = refs[-1] if nk > 1 else None
        part = None
        for p in range(npairs):
            d = lax.dot_general(refs[2 * p][...], refs[2 * p + 1][...], dn, preferred_element_type=F32)
            part = d if part is None else part + d

        def finish(acc):
            r = acc * scale if scale != 1.0 else acc
            if res_ref is not None:
                r = res_ref[...] + r
            if out_kind == "n":
                o_ref[...] = r.astype(out_dtype)
            else:
                for s in range(tn // ns):
                    o_ref[s] = r[:, s * ns:(s + 1) * ns].astype(out_dtype)
            if n_ref is not None:
                n_ref[...] = (r * _rstd(r) * g_ref[...]).astype(BF16)

        if nk == 1:
            finish(part)
        else:
            k = pl.program_id(2)

            @pl.when(k == 0)
            def _():
                acc_ref[...] = part

            @pl.when((k > 0) & (k < nk - 1))
            def _():
                acc_ref[...] += part

            @pl.when(k == nk - 1)
            def _():
                finish(acc_ref[...] + part)

    grid = (M // tm, N // tn, nk) if order == "ij" else (N // tn, M // tm, nk)
    out_specs, out_shape = o_spec, o_shape
    if norm_g is not None:
        out_specs = [o_spec, pl.BlockSpec((tm, tn), lambda g0, g1, k: ij(g0, g1))]
        out_shape = [o_shape, jax.ShapeDtypeStruct((M, N), BF16)]
    return pl.pallas_call(
        body, name=name, grid=grid, in_specs=in_specs, out_specs=out_specs, out_shape=out_shape,
        scratch_shapes=[pltpu.VMEM((tm, tn), F32)] if nk > 1 else [],
        compiler_params=_params(("parallel", "parallel", "arbitrary")),
    )(*args)


def _rstd(x):
    return lax.rsqrt(jnp.mean(x * x, axis=-1, keepdims=True) + EPS)


def _rms_bwd_math(x, g, dy):
    r = _rstd(x)
    gy = dy * g
    xr = x * r
    dx = r * (gy - xr * jnp.mean(gy * xr, axis=-1, keepdims=True))
    return dx, dy * xr


def _rms_fwd(h, g, *, name, tm=512, dep=None):
    T, Dm = h.shape
    tm = min(tm, T)

    def body(h_ref, g_ref, *rest):
        x = h_ref[...]
        rest[-1][...] = (x * _rstd(x) * g_ref[...]).astype(BF16)

    return pl.pallas_call(
        body, name=name, grid=(T // tm,),
        in_specs=[pl.BlockSpec((tm, Dm), lambda i: (i, 0)), pl.BlockSpec((1, Dm), lambda i: (0, 0))]
        + ([ANY] if dep is not None else []),
        out_specs=pl.BlockSpec((tm, Dm), lambda i: (i, 0)),
        out_shape=jax.ShapeDtypeStruct((T, Dm), BF16),
        compiler_params=_params(("parallel",)),
    )(h, g, *([dep] if dep is not None else []))


def _rms_bwd(h, g, dn, dres, *, name, tm=512, dep=None):
    T, Dm = h.shape
    tm = min(tm, T)
    has_res = dres is not None

    def body(*refs):
        h_ref, g_ref, dn_ref = refs[:3]
        pos = 3
        dres_ref = refs[pos] if has_res else None
        pos += has_res + (dep is not None)
        dh_ref, dhb_ref, dg_ref = refs[pos:pos + 3]
        dx, dgr = _rms_bwd_math(h_ref[...], g_ref[...], dn_ref[...].astype(F32))
        if has_res:
            dx = dres_ref[...] + dx
        dh_ref[...] = dx
        dhb_ref[...] = dx.astype(BF16)
        part = jnp.sum(dgr, axis=0, keepdims=True)

        @pl.when(pl.program_id(0) == 0)
        def _():
            dg_ref[...] = part

        @pl.when(pl.program_id(0) > 0)
        def _():
            dg_ref[...] += part

    row = pl.BlockSpec((tm, Dm), lambda i: (i, 0))
    vec = pl.BlockSpec((1, Dm), lambda i: (0, 0))
    args = [h, g, dn] + ([dres] if has_res else []) + ([dep] if dep is not None else [])
    return pl.pallas_call(
        body, name=name, grid=(T // tm,),
        in_specs=[row, vec, row] + ([row] if has_res else []) + ([ANY] if dep is not None else []),
        out_specs=[row, row, vec],
        out_shape=[jax.ShapeDtypeStruct((T, Dm), F32), jax.ShapeDtypeStruct((T, Dm), BF16),
                   jax.ShapeDtypeStruct((1, Dm), F32)],
        compiler_params=_params(("arbitrary",)),
    )(*args)


def _loss_head(h, g, tgt, *, name, tm=512):
    T, Dm = h.shape
    tm = min(tm, T)

    def body(h_ref, g_ref, t_ref, dh_ref, dhb_ref, dg_ref, loss_ref):
        x = h_ref[...]
        gv = g_ref[...]
        r = _rstd(x)
        diff = x * r * gv - t_ref[...]
        lpart = 0.5 * jnp.sum(jnp.mean(diff * diff, axis=-1, keepdims=True), axis=0, keepdims=True)
        dx, dgr = _rms_bwd_math(x, gv, diff * (1.0 / Dm))
        dh_ref[...] = dx
        dhb_ref[...] = dx.astype(BF16)
        part = jnp.sum(dgr, axis=0, keepdims=True)
        lrow = jnp.broadcast_to(lpart, (1, LANES))

        @pl.when(pl.program_id(0) == 0)
        def _():
            dg_ref[...] = part
            loss_ref[...] = lrow

        @pl.when(pl.program_id(0) > 0)
        def _():
            dg_ref[...] += part
            loss_ref[...] += lrow

    row = pl.BlockSpec((tm, Dm), lambda i: (i, 0))
    vec = pl.BlockSpec((1, Dm), lambda i: (0, 0))
    return pl.pallas_call(
        body, name=name, grid=(T // tm,),
        in_specs=[row, vec, row],
        out_specs=[row, row, vec, pl.BlockSpec((1, LANES), lambda i: (0, 0))],
        out_shape=[jax.ShapeDtypeStruct((T, Dm), F32), jax.ShapeDtypeStruct((T, Dm), BF16),
                   jax.ShapeDtypeStruct((1, Dm), F32), jax.ShapeDtypeStruct((1, LANES), F32)],
        compiler_params=_params(("arbitrary",)),
    )(h, g, tgt)


MXU_COLS = 256
FF_TILE = 2 * MXU_COLS


def _row_block(rows, row_bytes, align, budget=24 * 1024 * 1024):
    fits = [d for d in range(align, rows + 1, align) if rows % d == 0 and 2 * d * row_bytes <= budget]
    assert fits, (rows, row_bytes)
    return fits[-1]


def _swiglu_up(n, wg, wu, *, name, tm=1024, tn=FF_TILE):
    T, Dm = n.shape
    Fd = wg.shape[1]
    tm = min(tm, T)

    def body(n_ref, wg_ref, wu_ref, pg_ref, pu_ref, a_ref):
        x = n_ref[...]
        g = jnp.dot(x, wg_ref[...], preferred_element_type=F32)
        u = jnp.dot(x, wu_ref[...], preferred_element_type=F32)
        sg = jax.nn.sigmoid(g)
        silu = g * sg
        pg_ref[...] = ((sg + silu * (1.0 - sg)) * u).astype(BF16)
        pu_ref[...] = silu.astype(BF16)
        a_ref[...] = (silu * u).astype(BF16)

    wspec = pl.BlockSpec((Dm, tn), lambda j, i: (0, j))
    ospec = pl.BlockSpec((tm, tn), lambda j, i: (i, j))
    oshape = jax.ShapeDtypeStruct((T, Fd), BF16)
    return pl.pallas_call(
        body, name=name, grid=(Fd // tn, T // tm),
        in_specs=[pl.BlockSpec((tm, Dm), lambda j, i: (i, 0)), wspec, wspec],
        out_specs=[ospec, ospec, ospec], out_shape=[oshape, oshape, oshape],
        compiler_params=_params(("parallel", "parallel")),
    )(n, wg, wu)


def _swiglu_bwd_act(dhb, wd, PG, PU, *, name, tm=1024, tn=D_FF // N_CHIPS):
    T, Dm = dhb.shape
    Fd = wd.shape[0]
    tm, tn = min(tm, T), min(tn, Fd)

    def body(dh_ref, wd_ref, pg_ref, pu_ref, dg_ref, du_ref):
        da = 0.5 * lax.dot_general(dh_ref[...], wd_ref[...], (((1,), (1,)), ((), ())), preferred_element_type=F32)
        dg_ref[...] = (da * pg_ref[...].astype(F32)).astype(BF16)
        du_ref[...] = (da * pu_ref[...].astype(F32)).astype(BF16)

    blk = pl.BlockSpec((tm, tn), lambda j, i: (i, j))
    oshape = jax.ShapeDtypeStruct((T, Fd), BF16)
    return pl.pallas_call(
        body, name=name, grid=(Fd // tn, T // tm),
        in_specs=[pl.BlockSpec((tm, Dm), lambda j, i: (i, 0)), pl.BlockSpec((tn, Dm), lambda j, i: (j, 0)), blk, blk],
        out_specs=[blk, blk], out_shape=[oshape, oshape],
        compiler_params=_params(("parallel", "parallel")),
    )(dhb, wd, PG, PU)


_INV_SQRT2 = 0.7071067811865476
_INV_SQRT2PI = 0.3989422804014327


def _erf(x):
    ax = jnp.abs(x)
    t = 1.0 / (1.0 + 0.3275911 * ax)
    poly = t * (0.254829592 + t * (-0.284496736 + t * (1.421413741 + t * (-1.453152027 + t * 1.061405429))))
    y = 1.0 - poly * jnp.exp(-ax * ax)
    return jnp.where(x < 0, -y, y)


def _gelu_cdf(x):
    return 0.5 * (1.0 + _erf(x * _INV_SQRT2))


def _lane_lt64(shape):
    return lax.broadcasted_iota(jnp.int32, shape, len(shape) - 1) < HEAD_DIM


def _dup_half(x, kv):
    rolled = pltpu.roll(x, HEAD_DIM, 1)
    lo = _lane_lt64(x.shape)
    return jnp.where(lo, x, rolled) if kv == 0 else jnp.where(lo, rolled, x)


HEADS_PER_KV = B_Q_HEADS // B_KV_HEADS
PAIRS = HEADS_PER_KV // 2


def _attn_bias():
    shape = (2 * CHUNK, HEADS_PER_KV * CHUNK)
    qpos = (lax.broadcasted_iota(jnp.int32, shape, 1) & (CHUNK - 1)) + CHUNK
    kpos = lax.broadcasted_iota(jnp.int32, shape, 0)
    diff = qpos - kpos
    band = (diff >= 0) & (diff < CHUNK)
    return jnp.stack([jnp.where(band & (kpos >= CHUNK), 0.0, NEG), jnp.where(band, 0.0, NEG)]).astype(F32)


def _stack_heads(tiles, lo):
    parts = []
    for t in tiles:
        parts += [jnp.where(lo, t, 0.0), jnp.where(lo, 0.0, t)]
    return jnp.concatenate(parts, axis=0)


def _unstack_heads(s, lo):
    return [jnp.where(lo, s[2 * p * CHUNK:(2 * p + 1) * CHUNK], s[(2 * p + 1) * CHUNK:(2 * p + 2) * CHUNK])
            for p in range(PAIRS)]


def _stack_sinks(sk_ref, kv):
    return jnp.concatenate([jnp.broadcast_to(sk_ref[:, h:h + 1], (1, CHUNK))
                            for h in range(kv * HEADS_PER_KV, (kv + 1) * HEADS_PER_KV)], axis=1)


def _sgu_forward(z_ref, gv, wsm, bst):
    zu = z_ref[:, 0:D_A]
    zv = z_ref[:, D_A:2 * D_A]
    cu = _gelu_cdf(zu)
    cv = _gelu_cdf(zv)
    u = zu * cu
    v = zv * cv
    rv = _rstd(v)
    vn = (v * rv * gv).astype(BF16)
    svs = []
    for g in range(A_GROUPS):
        sl = slice(g * CHUNK, (g + 1) * CHUNK)
        svs.append(jnp.dot(wsm[g], vn[:, sl], preferred_element_type=F32) + bst[:, g:g + 1])
    sv = jnp.concatenate(svs, axis=1)
    return (zu, zv, cu, cv), u, v, rv, vn, sv


def _masked_ws(ws_ref):
    tril = lax.broadcasted_iota(jnp.int32, (CHUNK, CHUNK), 0) >= lax.broadcasted_iota(jnp.int32, (CHUNK, CHUNK), 1)
    return [jnp.where(tril, ws_ref[g], 0.0).astype(BF16) for g in range(A_GROUPS)], tril


def _attn_probs(qm, kkd, sink, bias):
    s = lax.dot_general(kkd, qm, (((1,), (1,)), ((), ())), preferred_element_type=F32) * (HEAD_DIM ** -0.5) + bias
    m = jnp.maximum(jnp.max(s, axis=0, keepdims=True), sink)
    e = jnp.exp(s - m)
    inv = 1.0 / (jnp.sum(e, axis=0, keepdims=True) + jnp.exp(sink - m))
    return e * inv


def _mixer_fwd(z, gv, ws, bst, sinks, ga, gb, *, name):
    T = z.shape[0]
    nb = T // CHUNK
    kvb = O_K // (2 * CHUNK)

    def body(z_ref, zp_ref, bias_ref, gv_ref, ws_ref, bst_ref, sk_ref, ga_ref, gb_ref, o_ref, p_ref):
        wsm, _ = _masked_ws(ws_ref)
        _, u, _, _, _, sv = _sgu_forward(z_ref, gv_ref[...], wsm, bst_ref[...])
        ya = u * sv
        o_ref[:, 0:D_A] = (ya * _rstd(ya) * ga_ref[...]).astype(BF16)

        mask = bias_ref[...]
        kk = jnp.concatenate([zp_ref[:, 0:CHUNK], z_ref[:, O_K:O_V]], axis=0)
        vv = jnp.concatenate([zp_ref[:, CHUNK:2 * CHUNK], z_ref[:, O_V:IN_COLS]], axis=0)
        lo = _lane_lt64((CHUNK, LANES))
        outs = []
        for kv in range(B_KV_HEADS):
            kkd = _dup_half(kk, kv).astype(BF16)
            vvd = _dup_half(vv, kv).astype(BF16)
            q = _stack_heads([z_ref[:, O_Q + (kv * PAIRS + pr) * LANES:O_Q + (kv * PAIRS + pr + 1) * LANES]
                              for pr in range(PAIRS)], lo).astype(BF16)
            p = _attn_probs(q, kkd, _stack_sinks(sk_ref, kv), mask)
            p_ref[kv] = p
            out = lax.dot_general(p.astype(BF16), vvd, (((0,), (0,)), ((), ())), preferred_element_type=F32)
            outs += _unstack_heads(out, lo)
        yb = jnp.concatenate(outs, axis=1)
        o_ref[:, D_A:D_A + D_B] = (yb * _rstd(yb) * gb_ref[...]).astype(BF16)

    full = lambda shape: pl.BlockSpec(shape, lambda i: (0,) * len(shape))
    pshape = (B_KV_HEADS, 2 * CHUNK, HEADS_PER_KV * CHUNK)
    return pl.pallas_call(
        body, name=name, grid=(nb,),
        in_specs=[pl.BlockSpec((CHUNK, IN_COLS), lambda i: (i, 0)),
                  pl.BlockSpec((CHUNK, 2 * CHUNK), lambda i: (jnp.maximum(i - 1, 0), kvb)),
                  pl.BlockSpec((None, 2 * CHUNK, HEADS_PER_KV * CHUNK), lambda i: (jnp.minimum(i, 1), 0, 0)),
                  full((1, D_A)), full((A_GROUPS, CHUNK, CHUNK)), full((CHUNK, A_GROUPS)), full((1, B_Q_HEADS)),
                  full((1, D_A)), full((1, D_B))],
        out_specs=[pl.BlockSpec((CHUNK, D_A + D_B), lambda i: (i, 0)),
                   pl.BlockSpec((None,) + pshape, lambda i: (i, 0, 0, 0))],
        out_shape=[jax.ShapeDtypeStruct((T, D_A + D_B), BF16), jax.ShapeDtypeStruct((nb,) + pshape, F32)],
        compiler_params=_params(("parallel",)),
    )(z, z, _attn_bias(), gv, ws, bst, sinks, ga, gb)


def _mixer_bwd(z, dyn, probs, gv, ws, bst, ga, gb, *, name):
    T = z.shape[0]
    nb = T // CHUNK
    kvb = O_K // (2 * CHUNK)
    NT = (((0,), (0,)), ((), ()))

    def body(z_ref, zp_ref, dy_ref, p_ref, gv_ref, ws_ref, bst_ref, ga_ref, gb_ref,
             dz_ref, dgv_ref, dws_ref, dbst_ref, dsk_ref, dga_ref, dgb_ref, carry_ref):
        step = pl.program_id(0)

        @pl.when(step == 0)
        def _():
            carry_ref[...] = jnp.zeros_like(carry_ref)
            dgv_ref[...] = jnp.zeros_like(dgv_ref)
            dws_ref[...] = jnp.zeros_like(dws_ref)
            dbst_ref[...] = jnp.zeros_like(dbst_ref)
            dsk_ref[...] = jnp.zeros_like(dsk_ref)
            dga_ref[...] = jnp.zeros_like(dga_ref)
            dgb_ref[...] = jnp.zeros_like(dgb_ref)

        wsm, tril = _masked_ws(ws_ref)
        gvv = gv_ref[...]
        (zu, zv, cu, cv), u, v, rv, vn, sv = _sgu_forward(z_ref, gvv, wsm, bst_ref[...])
        ya = u * sv
        dya, dga_rows = _rms_bwd_math(ya, ga_ref[...], dy_ref[:, 0:D_A].astype(F32))
        dga_ref[...] += jnp.sum(dga_rows, axis=0, keepdims=True)
        du = dya * sv
        dsv = dya * u
        dvn_parts = []
        for g in range(A_GROUPS):
            sl = slice(g * CHUNK, (g + 1) * CHUNK)
            dsv_g = dsv[:, sl]
            dsv_gb = dsv_g.astype(BF16)
            dw = lax.dot_general(dsv_gb, vn[:, sl], (((1,), (1,)), ((), ())), preferred_element_type=F32)
            dws_ref[g] += jnp.where(tril, dw, 0.0)
            dbst_ref[:, g:g + 1] += jnp.sum(dsv_g, axis=1, keepdims=True)
            dvn_parts.append(lax.dot_general(wsm[g], dsv_gb, NT, preferred_element_type=F32))
        dvn = jnp.concatenate(dvn_parts, axis=1)
        dv, dgv_rows = _rms_bwd_math(v, gvv, dvn)
        dgv_ref[...] += jnp.sum(dgv_rows, axis=0, keepdims=True)
        dz_ref[:, 0:D_A] = (du * (cu + zu * jnp.exp(-0.5 * zu * zu) * _INV_SQRT2PI)).astype(BF16)
        dz_ref[:, D_A:2 * D_A] = (dv * (cv + zv * jnp.exp(-0.5 * zv * zv) * _INV_SQRT2PI)).astype(BF16)

        kk = jnp.concatenate([zp_ref[:, 0:CHUNK], z_ref[:, O_K:O_V]], axis=0)
        vv = jnp.concatenate([zp_ref[:, CHUNK:2 * CHUNK], z_ref[:, O_V:IN_COLS]], axis=0)
        lo = _lane_lt64((CHUNK, LANES))
        kkd = [_dup_half(kk, kv).astype(BF16) for kv in range(B_KV_HEADS)]
        vvd = [_dup_half(vv, kv).astype(BF16) for kv in range(B_KV_HEADS)]
        outs = []
        for kv in range(B_KV_HEADS):
            outs += _unstack_heads(lax.dot_general(p_ref[kv].astype(BF16), vvd[kv], NT, preferred_element_type=F32), lo)
        yb = jnp.concatenate(outs, axis=1)
        dyb, dgb_rows = _rms_bwd_math(yb, gb_ref[...], dy_ref[:, D_A:D_A + D_B].astype(F32))
        dgb_ref[...] += jnp.sum(dgb_rows, axis=0, keepdims=True)

        dkk, dvv = [], []
        for kv in range(B_KV_HEADS):
            do = _stack_heads([dyb[:, (kv * PAIRS + pr) * LANES:(kv * PAIRS + pr + 1) * LANES]
                               for pr in range(PAIRS)], lo).astype(BF16)
            q = _stack_heads([z_ref[:, O_Q + (kv * PAIRS + pr) * LANES:O_Q + (kv * PAIRS + pr + 1) * LANES]
                              for pr in range(PAIRS)], lo).astype(BF16)
            p = p_ref[kv]
            dvv.append(jnp.dot(p.astype(BF16), do, preferred_element_type=F32))
            dp = lax.dot_general(vvd[kv], do, (((1,), (1,)), ((), ())), preferred_element_type=F32)
            delta = jnp.sum(p * dp, axis=0, keepdims=True)
            dsink = (jnp.sum(p, axis=0, keepdims=True) - 1.0) * delta
            for g in range(HEADS_PER_KV):
                h = kv * HEADS_PER_KV + g
                dsk_ref[:, h:h + 1] += jnp.sum(dsink[:, g * CHUNK:(g + 1) * CHUNK], axis=1, keepdims=True)
            ds = (p * (dp - delta) * (HEAD_DIM ** -0.5)).astype(BF16)
            dq = _unstack_heads(lax.dot_general(ds, kkd[kv], NT, preferred_element_type=F32), lo)
            for pr in range(PAIRS):
                c0 = O_Q + (kv * PAIRS + pr) * LANES
                dz_ref[:, c0:c0 + LANES] = dq[pr].astype(BF16)
            dkk.append(jnp.dot(ds, q, preferred_element_type=F32))

        def fold(parts):
            tot = [t + pltpu.roll(t, HEAD_DIM, 1) for t in parts]
            return jnp.where(_lane_lt64(tot[0].shape), tot[0], tot[1])

        dk_all = fold(dkk)
        dv_all = fold(dvv)
        dz_ref[:, O_K:O_V] = (dk_all[CHUNK:] + carry_ref[:, 0:CHUNK]).astype(BF16)
        dz_ref[:, O_V:IN_COLS] = (dv_all[CHUNK:] + carry_ref[:, CHUNK:2 * CHUNK]).astype(BF16)
        carry_ref[:, 0:CHUNK] = dk_all[:CHUNK]
        carry_ref[:, CHUNK:2 * CHUNK] = dv_all[:CHUNK]

    full = lambda shape: pl.BlockSpec(shape, lambda s: (0,) * len(shape))
    rev = lambda s: nb - 1 - s
    return pl.pallas_call(
        body, name=name, grid=(nb,),
        in_specs=[pl.BlockSpec((CHUNK, IN_COLS), lambda s: (rev(s), 0)),
                  pl.BlockSpec((CHUNK, 2 * CHUNK), lambda s: (jnp.maximum(rev(s) - 1, 0), kvb)),
                  pl.BlockSpec((CHUNK, D_A + D_B), lambda s: (rev(s), 0)),
                  pl.BlockSpec((None, B_KV_HEADS, 2 * CHUNK, HEADS_PER_KV * CHUNK), lambda s: (rev(s), 0, 0, 0)),
                  full((1, D_A)), full((A_GROUPS, CHUNK, CHUNK)), full((CHUNK, A_GROUPS)),
                  full((1, D_A)), full((1, D_B))],
        out_specs=[pl.BlockSpec((CHUNK, IN_COLS), lambda s: (rev(s), 0)),
                   full((1, D_A)), full((A_GROUPS, CHUNK, CHUNK)), full((CHUNK, A_GROUPS)), full((1, B_Q_HEADS)),
                   full((1, D_A)), full((1, D_B))],
        out_shape=[jax.ShapeDtypeStruct((T, IN_COLS), BF16), jax.ShapeDtypeStruct((1, D_A), F32),
                   jax.ShapeDtypeStruct((A_GROUPS, CHUNK, CHUNK), F32), jax.ShapeDtypeStruct((CHUNK, A_GROUPS), F32),
                   jax.ShapeDtypeStruct((1, B_Q_HEADS), F32), jax.ShapeDtypeStruct((1, D_A), F32),
                   jax.ShapeDtypeStruct((1, D_B), F32)],
        scratch_shapes=[pltpu.VMEM((CHUNK, 2 * CHUNK), F32)],
        compiler_params=_params(("arbitrary",)),
    )(z, z, dyn, probs, gv, ws, bst, ga, gb)


def _xattn_probs(qh, kh):
    s = lax.dot_general(kh, qh, (((1,), (1,)), ((), ())), preferred_element_type=F32) * (X_HEAD_DIM ** -0.5)
    e = jnp.exp(s - jnp.max(s, axis=0, keepdims=True))
    return e / jnp.sum(e, axis=0, keepdims=True)


def _xattn_fwd(q, kvm, *, name, tm=512):
    T = q.shape[0]
    Mm = kvm.shape[0]
    tm = min(tm, T)

    def body(q_ref, kv_ref, o_ref, p_ref):
        for h in range(X_HEADS):
            sl = slice(h * X_HEAD_DIM, (h + 1) * X_HEAD_DIM)
            kh = kv_ref[:, sl].astype(BF16)
            vh = kv_ref[:, D_MODEL + h * X_HEAD_DIM:D_MODEL + (h + 1) * X_HEAD_DIM].astype(BF16)
            p = _xattn_probs(q_ref[:, sl], kh)
            p_ref[h * Mm:(h + 1) * Mm, :] = p
            o_ref[:, sl] = lax.dot_general(p.astype(BF16), vh, (((0,), (0,)), ((), ())),
                                           preferred_element_type=F32).astype(BF16)

    return pl.pallas_call(
        body, name=name, grid=(T // tm,),
        in_specs=[pl.BlockSpec((tm, D_MODEL), lambda i: (i, 0)), pl.BlockSpec((Mm, 2 * D_MODEL), lambda i: (0, 0))],
        out_specs=[pl.BlockSpec((tm, D_MODEL), lambda i: (i, 0)), pl.BlockSpec((X_HEADS * Mm, tm), lambda i: (0, i))],
        out_shape=[jax.ShapeDtypeStruct((T, D_MODEL), BF16), jax.ShapeDtypeStruct((X_HEADS * Mm, T), F32)],
        compiler_params=_params(("parallel",)),
    )(q, kvm)


def _xattn_bwd(q, kvm, probs, do, *, name, tm=512):
    T = q.shape[0]
    Mm = kvm.shape[0]
    tm = min(tm, T)
    NT = (((0,), (0,)), ((), ()))

    def body(q_ref, kv_ref, p_ref, do_ref, dq_ref, dkv_ref):
        @pl.when(pl.program_id(0) == 0)
        def _():
            dkv_ref[...] = jnp.zeros_like(dkv_ref)

        for h in range(X_HEADS):
            sl = slice(h * X_HEAD_DIM, (h + 1) * X_HEAD_DIM)
            slv = slice(D_MODEL + h * X_HEAD_DIM, D_MODEL + (h + 1) * X_HEAD_DIM)
            kh = kv_ref[:, sl].astype(BF16)
            vh = kv_ref[:, slv].astype(BF16)
            qh = q_ref[:, sl]
            doh = do_ref[:, sl]
            p = p_ref[h * Mm:(h + 1) * Mm, :]
            dkv_ref[:, slv] += jnp.dot(p.astype(BF16), doh, preferred_element_type=F32)
            dp = lax.dot_general(vh, doh, (((1,), (1,)), ((), ())), preferred_element_type=F32)
            ds = (p * (dp - jnp.sum(p * dp, axis=0, keepdims=True)) * (X_HEAD_DIM ** -0.5)).astype(BF16)
            dq_ref[:, sl] = lax.dot_general(ds, kh, NT, preferred_element_type=F32).astype(BF16)
            dkv_ref[:, sl] += jnp.dot(ds, qh, preferred_element_type=F32)

    row = pl.BlockSpec((tm, D_MODEL), lambda i: (i, 0))
    kvs = pl.BlockSpec((Mm, 2 * D_MODEL), lambda i: (0, 0))
    return pl.pallas_call(
        body, name=name, grid=(T // tm,),
        in_specs=[row, kvs, pl.BlockSpec((X_HEADS * Mm, tm), lambda i: (0, i)), row], out_specs=[row, kvs],
        out_shape=[jax.ShapeDtypeStruct((T, D_MODEL), BF16), jax.ShapeDtypeStruct((Mm, 2 * D_MODEL), F32)],
        compiler_params=_params(("arbitrary",)),
    )(q, kvm, probs, do)


def _dw_pair(n, dG, dU, *, name, tm=512, tk=2048):
    T, Dm = n.shape
    ns = dG.shape[1] // N_CHIPS
    tm, tk = min(tm, Dm), min(tk, T)
    nk = T // tk
    TN = (((0,), (0,)), ((), ()))

    def body(n_ref, dg_ref, du_ref, og_ref, ou_ref, accg_ref, accu_ref):
        k = pl.program_id(2)
        x = n_ref[...]
        pg = lax.dot_general(x, dg_ref[...], TN, preferred_element_type=F32)
        pu = lax.dot_general(x, du_ref[...], TN, preferred_element_type=F32)

        @pl.when(k == 0)
        def _():
            accg_ref[...] = pg
            accu_ref[...] = pu

        @pl.when((k > 0) & (k < nk - 1))
        def _():
            accg_ref[...] += pg
            accu_ref[...] += pu

        @pl.when(k == nk - 1)
        def _():
            og_ref[...] = (accg_ref[...] + pg).astype(BF16)
            ou_ref[...] = (accu_ref[...] + pu).astype(BF16)

    assert nk > 1
    bspec = pl.BlockSpec((tk, ns), lambda j, i, k: (k, j))
    ospec = pl.BlockSpec((None, tm, ns), lambda j, i, k: (j, i, 0))
    oshape = jax.ShapeDtypeStruct((N_CHIPS, Dm, ns), BF16)
    return pl.pallas_call(
        body, name=name, grid=(N_CHIPS, Dm // tm, nk),
        in_specs=[pl.BlockSpec((tk, tm), lambda j, i, k: (k, i)), bspec, bspec],
        out_specs=[ospec, ospec], out_shape=[oshape, oshape],
        scratch_shapes=[pltpu.VMEM((tm, ns), F32), pltpu.VMEM((tm, ns), F32)],
        compiler_params=_params(("parallel", "parallel", "arbitrary")),
    )(n, dG, dU)


def _swiglu_bwd_weights(tag, n, PG, PU, A, wd, dhb):
    T = n.shape[0]
    dG, dU = _swiglu_bwd_act(dhb, wd, PG, PU, name=f"{tag}_bwd_act", tm=1024)
    dwd = _matmul([(A, dhb)], M=D_FF, N=D_MODEL, K=T, tm=FF_TILE, tn=512, tk=T, a_t=True, out_dtype=BF16,
                  scale=0.5, name=f"{tag}_dwd")
    dwg, dwu = _dw_pair(n, dG, dU, name=f"{tag}_dw_gate_up")
    return dG, dU, dwg, dwu, dwd


def _swiglu_bwd_input(tag, hin, g_norm, dG, dU, wg, wu, dh, dep):
    T = hin.shape[0]
    dn = _matmul([(dG, wg), (dU, wu)], M=T, N=D_MODEL, K=D_FF, tm=512, tn=D_MODEL // 2, tk=D_FF // 2, b_kind="t",
                 out_dtype=BF16, dep=dep, name=f"{tag}_dn")
    return _rms_bwd(hin, g_norm, dn, dh, name=f"{tag}_norm_bwd")


GROUP_FFN1 = ["w1_gate", "w1_up", "w1_down"]
GROUP_MID = ["w_in", "w_out", "w_xq", "w_xkv", "w_xo"]
GROUP_FFN2 = ["w2_gate", "w2_up", "w2_down"]


def _local_step(x, mem, tgt, W, comm):
    T = x.shape[0]
    Mm = mem.shape[0]
    mm = functools.partial(_matmul)

    W = {**W, **comm.gather_now("ffn1_up", ["w1_gate", "w1_up"])}
    tok = comm.gather_start("ffn1_down", ["w1_down"], after=W["w1_up"])
    tok = comm.gather_start("mid", GROUP_MID, after=tok)
    tok = comm.gather_start("ffn2", GROUP_FFN2, after=tok)
    n1 = _rms_fwd(x, W["g_ffn1"], dep=tok, name="f_norm1")
    PG1, PU1, A1 = _swiglu_up(n1, W["w1_gate"], W["w1_up"], name="f_ffn1_up")
    tok = comm.gather_landed("ffn1_down", after=A1)
    tok = comm.gather_landed("mid", after=tok)
    W = {**W, **comm.gather_ready("ffn1_down", after=tok)}
    h1 = mm([(A1, W["w1_down"])], M=T, N=D_MODEL, K=D_FF, tm=512, tn=D_MODEL // 2, tk=D_FF, scale=0.5, res=x,
            order="ji", name="f_ffn1_down")
    n2 = _rms_fwd(h1, W["g_mix"], name="f_norm2")
    W = {**W, **comm.gather_ready("mid", after=n2)}
    z = mm([(n2, W["w_in"])], M=T, N=IN_COLS, K=D_MODEL, tm=512, tn=IN_COLS // 2, tk=D_MODEL, name="f_w_in")
    bst = jnp.transpose(W["b_s"])
    yn, probs = _mixer_fwd(z, W["g_v"], W["w_s"], bst, W["sinks"], W["g_a_out"], W["g_b_out"], name="f_mixer")
    tok = comm.gather_landed("ffn2", after=yn)
    h2, n3 = mm([(yn, W["w_out"])], M=T, N=D_MODEL, K=D_MODEL, tm=512, tn=D_MODEL, tk=D_MODEL, res=h1,
                norm_g=W["g_x"], dep=tok, name="f_w_out")
    memn = _rms_fwd(mem, W["g_mem"], name="f_norm_mem")
    q3 = mm([(n3, W["w_xq"])], M=T, N=D_MODEL, K=D_MODEL, tm=1024, tn=D_MODEL, tk=D_MODEL, out_dtype=BF16,
            name="f_w_xq")
    kvm = mm([(memn, W["w_xkv"])], M=Mm, N=2 * D_MODEL, K=D_MODEL, tm=Mm, tn=1024, tk=D_MODEL, b_kind="n",
             name="f_w_xkv")
    o3, xprobs = _xattn_fwd(q3, kvm, name="f_xattn")
    h3, n4 = mm([(o3, W["w_xo"])], M=T, N=D_MODEL, K=D_MODEL, tm=512, tn=D_MODEL, tk=D_MODEL, res=h2,
                norm_g=W["g_ffn2"], name="f_w_xo")
    W = {**W, **comm.gather_ready("ffn2", after=n4)}
    PG2, PU2, A2 = _swiglu_up(n4, W["w2_gate"], W["w2_up"], name="f_ffn2_up")
    h4 = mm([(A2, W["w2_down"])], M=T, N=D_MODEL, K=D_FF, tm=512, tn=D_MODEL // 2, tk=D_FF, scale=0.5, res=h3,
            order="ji", name="f_ffn2_down")

    grads = {}
    dh4, dh4b, grads["g_final"], loss = _loss_head(h4, W["g_final"], tgt, name="loss_head")
    dG2, dU2, dwg, dwu, dwd = _swiglu_bwd_weights("b_ffn2", n4, PG2, PU2, A2, W["w2_down"], dh4b)
    tok = comm.reduce_pair_start("ffn2", {"w2_gate": dwg, "w2_up": dwu, "w2_down": dwd})
    dh3, dh3b, grads["g_ffn2"] = _swiglu_bwd_input("b_ffn2", h3, W["g_ffn2"], dG2, dU2, W["w2_gate"], W["w2_up"],
                                                   dh4, tok)
    tok = comm.reduce_pair_done("ffn2", after=dh3b)

    mid = {}
    do3 = mm([(dh3b, W["w_xo"])], M=T, N=D_MODEL, K=D_MODEL, tm=512, tn=D_MODEL, tk=D_MODEL, b_kind="t",
             out_dtype=BF16, dep=tok, name="b_do3")
    mid["w_xo"] = mm([(o3, dh3b)], M=D_MODEL, N=D_MODEL, K=T, tm=1024, tn=D_MODEL // 2, tk=4096, a_t=True,
                       out_dtype=BF16, name="b_dw_xo")
    dq3, dkvm = _xattn_bwd(q3, kvm, xprobs, do3, name="b_xattn")
    mid["w_xq"] = mm([(n3, dq3)], M=D_MODEL, N=D_MODEL, K=T, tm=1024, tn=D_MODEL // 2, tk=4096, a_t=True,
                       out_dtype=BF16, name="b_dw_xq")
    dn3 = mm([(dq3, W["w_xq"])], M=T, N=D_MODEL, K=D_MODEL, tm=512, tn=D_MODEL, tk=D_MODEL, b_kind="t",
             out_dtype=BF16, name="b_dn3")
    dh2, dh2b, grads["g_x"] = _rms_bwd(h2, W["g_x"], dn3, dh3, name="b_norm3")
    dkvmb = dkvm.astype(BF16)
    mid["w_xkv"] = mm([(memn, dkvmb)], M=D_MODEL, N=2 * D_MODEL, K=Mm, tm=D_MODEL, tn=1024, tk=Mm, a_t=True,
                        out_kind="s", out_dtype=BF16, name="b_dw_xkv")
    dmemn = mm([(dkvmb, W["w_xkv"])], M=Mm, N=D_MODEL, K=2 * D_MODEL, tm=Mm, tn=D_MODEL, tk=1024, b_kind="t",
               name="b_dmemn")
    _, _, grads["g_mem"] = _rms_bwd(mem, W["g_mem"], dmemn, None, name="b_norm_mem")
    comm.reduce_finish("ffn2", after=dh2b)

    dyn = mm([(dh2b, W["w_out"])], M=T, N=D_MODEL, K=D_MODEL, tm=1024, tn=D_MODEL, tk=D_MODEL, b_kind="t",
             out_dtype=BF16, name="b_dyn")
    mid["w_out"] = mm([(yn, dh2b)], M=D_MODEL, N=D_MODEL, K=T, tm=1024, tn=D_MODEL // 2, tk=4096, a_t=True,
                        out_dtype=BF16, name="b_dw_out")
    dz, grads["g_v"], grads["w_s"], dbst, grads["sinks"], grads["g_a_out"], grads["g_b_out"] = _mixer_bwd(
        z, dyn, probs, W["g_v"], W["w_s"], bst, W["g_a_out"], W["g_b_out"], name="b_mixer")
    grads["b_s"] = jnp.transpose(dbst)
    mid["w_in"] = mm([(n2, dz)], M=D_MODEL, N=IN_COLS, K=T, tm=1024, tn=IN_COLS, tk=1024, a_t=True,
                     out_dtype=BF16, name="b_dw_in")
    tok = comm.reduce_pair_start("mid", mid)
    dn2 = mm([(dz, W["w_in"])], M=T, N=D_MODEL, K=IN_COLS, tm=512, tn=D_MODEL, tk=IN_COLS, b_kind="t",
             out_dtype=BF16, dep=tok, name="b_dn2")
    tok = comm.reduce_pair_done("mid", after=dn2)
    dh1, dh1b, grads["g_mix"] = _rms_bwd(h1, W["g_mix"], dn2, dh2, dep=tok, name="b_norm2")

    dG1, dU1, dwg, dwu, dwd = _swiglu_bwd_weights("b_ffn1", n1, PG1, PU1, A1, W["w1_down"], dh1b)
    comm.reduce_finish("mid", after=dwu)
    tok = comm.reduce_start("ffn1", {"w1_gate": dwg, "w1_up": dwu, "w1_down": dwd})
    dx, _, grads["g_ffn1"] = _swiglu_bwd_input("b_ffn1", x, W["g_ffn1"], dG1, dU1, W["w1_gate"], W["w1_up"], dh1, tok)
    comm.reduce_finish("ffn1", after=dx)
    return loss, dx, grads


BIG = ["w1_gate", "w1_up", "w1_down", "w_in", "w_out", "w_xq", "w_xkv", "w_xo", "w2_gate", "w2_up", "w2_down"]
SMALL = ["g_ffn1", "g_mix", "g_v", "w_s", "b_s", "sinks", "g_a_out", "g_b_out", "g_x", "g_mem", "g_ffn2", "g_final"]
ALL_W = ["g_ffn1", "w1_gate", "w1_up", "w1_down", "g_mix", "w_in", "g_v", "w_s", "b_s", "sinks", "g_a_out",
         "g_b_out", "w_out", "g_x", "g_mem", "w_xq", "w_xkv", "w_xo", "g_ffn2", "w2_gate", "w2_up", "w2_down",
         "g_final"]
ANY = pl.BlockSpec(memory_space=pl.ANY)


def _place():
    x, y, c = lax.axis_index("x"), lax.axis_index("y"), lax.axis_index("c")
    chips = [(1 - x, y), (x, 1 - y), (1 - x, 1 - y)]
    return x, y, c, chips


COL_SHARDED = ("w1_gate", "w1_up", "w2_gate", "w2_up", "w_xkv")


def _gathered_shape(shape, by_cols):
    rows, cols = shape
    return (rows, N_CHIPS * cols) if by_cols else (N_CHIPS, rows, cols)


def _owner_rows(ref, shape, by_cols, slot, r0, rows):
    cols = shape[1]
    if by_cols:
        return ref.at[pl.ds(r0, rows), pl.ds(pl.multiple_of(slot * cols, LANES), cols)]
    return ref.at[slot, pl.ds(r0, rows), :]


def _allgather_weights(shards, by_cols, *, name):
    n = len(shards)

    def body(*refs):
        ins, outs = refs[:n], refs[n:2 * n]
        send, recv, loc = refs[2 * n:]
        x, y, c, chips = _place()
        me = 2 * x + y
        sib = (x, y, 1 - c)

        def half(w, slot, hc):
            h = shards[w].shape[0] // 2
            return _owner_rows(outs[w], shards[w].shape, by_cols[w], slot, hc * h, h)

        def copy(w, k, slot, hc, to, src=None):
            return pltpu.make_async_remote_copy(
                src_ref=half(w, slot, hc) if src is None else src, dst_ref=half(w, slot, hc),
                send_sem=send.at[6 * w + k], recv_sem=recv.at[6 * w + k], device_id=to, device_id_type=MESH)

        own = [pltpu.make_async_remote_copy(
            src_ref=ins[w], dst_ref=_owner_rows(outs[w], shards[w].shape, by_cols[w], me, 0, shards[w].shape[0]),
            send_sem=loc.at[w], recv_sem=loc.at[n + w], device_id=sib, device_id_type=MESH) for w in range(n)]
        for cp in own:
            cp.start()
        first = []
        for w in range(n):
            h = shards[w].shape[0] // 2
            for j, (tx, ty) in enumerate(chips):
                first.append(copy(w, j, me, c, (tx, ty, c), src=ins[w].at[pl.ds(c * h, h), :]))
                first[-1].start()
        passed = []
        for w in range(n):
            for j, (tx, ty) in enumerate(chips):
                slot = 2 * tx + ty
                copy(w, j, slot, c, (tx, ty, c)).wait_recv()
                passed.append(copy(w, 3 + j, slot, c, sib))
                passed[-1].start()
        for w in range(n):
            for j, (tx, ty) in enumerate(chips):
                copy(w, 3 + j, 2 * tx + ty, 1 - c, sib).wait_recv()
        for cp in first + passed:
            cp.wait_send()
        for cp in own:
            cp.wait()

    return pl.pallas_call(
        body, name=name, in_specs=[ANY] * n, out_specs=[ANY] * n,
        out_shape=[jax.ShapeDtypeStruct(_gathered_shape(s.shape, bc), s.dtype) for s, bc in zip(shards, by_cols)],
        scratch_shapes=[pltpu.SemaphoreType.DMA((6 * n,)), pltpu.SemaphoreType.DMA((6 * n,)),
                        pltpu.SemaphoreType.DMA((2 * n,))],
    )(*shards)


def _pair_exchange(grads, *, name):
    n = len(grads)

    def body(*refs):
        ins, outs = refs[:n], refs[n:2 * n]
        send, recv = refs[2 * n:]
        x, y, c, _ = _place()
        cps = []
        for w in range(n):
            h = grads[w].shape[1] // 2
            cps.append(pltpu.make_async_remote_copy(
                src_ref=ins[w].at[:, pl.ds((1 - c) * h, h), :], dst_ref=outs[w],
                send_sem=send.at[w], recv_sem=recv.at[w], device_id=(x, y, 1 - c), device_id_type=MESH))
            cps[-1].start()
        for cp in cps:
            cp.wait()

    return pl.pallas_call(
        body, name=name, in_specs=[ANY] * n, out_specs=[ANY] * n,
        out_shape=[jax.ShapeDtypeStruct((N_CHIPS, g.shape[1] // 2, g.shape[2]), g.dtype) for g in grads],
        scratch_shapes=[pltpu.SemaphoreType.DMA((n,)), pltpu.SemaphoreType.DMA((n,))],
    )(*grads)


def _pair_sum(g, got, *, name):
    S, R, C = g.shape
    h = R // 2
    tr = _row_block(h, 3 * C * 2, 16)
    nr = h // tr

    def body(a_ref, b_ref, o_ref):
        o_ref[...] = (a_ref[...].astype(F32) + b_ref[...].astype(F32)).astype(BF16)

    return pl.pallas_call(
        body, name=name, grid=(S, nr),
        in_specs=[pl.BlockSpec((None, tr, C), lambda s, r: (s, lax.axis_index("c") * nr + r, 0)),
                  pl.BlockSpec((None, tr, C), lambda s, r: (s, r, 0))],
        out_specs=pl.BlockSpec((None, tr, C), lambda s, r: (s, r, 0)),
        out_shape=jax.ShapeDtypeStruct((S, h, C), BF16),
        compiler_params=_params(("parallel", "parallel")),
    )(g, got)


def _chip_sum(part, got, *, name):
    S, h, C = part.shape
    tr = _row_block(h, 4 * C * 2 + C * 4, 16)
    nr = h // tr

    def body(own_ref, g0_ref, g1_ref, g2_ref, o_ref):
        acc = own_ref[...].astype(F32) + g0_ref[...].astype(F32)
        o_ref[...] = (acc + g1_ref[...].astype(F32)) + g2_ref[...].astype(F32)

    def piece(j):
        return pl.BlockSpec((None, tr, C), lambda r: (j, r, 0))

    return pl.pallas_call(
        body, name=name, grid=(nr,),
        in_specs=[pl.BlockSpec((None, tr, C), lambda r: (2 * lax.axis_index("x") + lax.axis_index("y"), r, 0)),
                  piece(0), piece(1), piece(2)],
        out_specs=pl.BlockSpec((tr, C), lambda r: (lax.axis_index("c") * nr + r, 0)),
        out_shape=jax.ShapeDtypeStruct((2 * h, C), F32),
        compiler_params=_params(("parallel",)),
    )(part, got, got, got)


def _pair_gather(totals, *, name):
    n = len(totals)

    def body(*refs):
        ins, outs = refs[:n], refs[n:2 * n]
        send, recv = refs[2 * n:]
        x, y, c, _ = _place()
        cps = []
        for w in range(n):
            h = totals[w].shape[0] // 2
            cps.append(pltpu.make_async_remote_copy(
                src_ref=ins[w].at[pl.ds(c * h, h), :], dst_ref=outs[w].at[pl.ds(c * h, h), :],
                send_sem=send.at[w], recv_sem=recv.at[w], device_id=(x, y, 1 - c), device_id_type=MESH))
            cps[-1].start()
        for w in range(n):
            h = totals[w].shape[0] // 2
            theirs = outs[w].at[pl.ds((1 - c) * h, h), :]
            pltpu.make_async_remote_copy(
                src_ref=theirs, dst_ref=theirs, send_sem=send.at[w], recv_sem=recv.at[w],
                device_id=(x, y, 1 - c), device_id_type=MESH).wait_recv()
        for cp in cps:
            cp.wait_send()

    return pl.pallas_call(
        body, name=name, in_specs=[ANY] * n, out_specs=[ANY] * n,
        out_shape=[jax.ShapeDtypeStruct(t.shape, t.dtype) for t in totals],
        input_output_aliases={w: w for w in range(n)},
        scratch_shapes=[pltpu.SemaphoreType.DMA((n,)), pltpu.SemaphoreType.DMA((n,))],
    )(*totals)


def _allreduce_small(v, *, name):
    R, C = v.shape
    ND = 8

    def body(v_ref, o_ref, all_ref, send, recv, loc):
        x, y, c, chips = _place()
        me, sib = (x, y, c), (x, y, 1 - c)

        def rows(px, py, pc):
            return all_ref.at[pl.ds((4 * px + 2 * py + pc) * R, R), :]

        def copy(k, block, to, src=None):
            return pltpu.make_async_remote_copy(
                src_ref=rows(*block) if src is None else src, dst_ref=rows(*block),
                send_sem=send.at[k], recv_sem=recv.at[k], device_id=to, device_id_type=MESH)

        mine = pltpu.make_async_copy(v_ref, rows(*me), loc)
        mine.start()
        first = [copy(0, me, sib, src=v_ref)]
        first += [copy(1 + j, me, (*chip, c), src=v_ref) for j, chip in enumerate(chips)]
        for cp in first:
            cp.start()
        passed = [copy(4 + j, (*chip, c), sib) for j, chip in enumerate(chips)]
        for j, chip in enumerate(chips):
            copy(1 + j, (*chip, c), me).wait_recv()
            passed[j].start()
        copy(0, sib, me).wait_recv()
        for j, chip in enumerate(chips):
            copy(4 + j, (*chip, 1 - c), me).wait_recv()
        for cp in first + passed:
            cp.wait_send()
        mine.wait()
        acc = all_ref[0:R, :]
        for d in range(1, ND):
            acc = acc + all_ref[d * R:(d + 1) * R, :]
        o_ref[...] = acc

    vm = pl.BlockSpec(memory_space=pltpu.VMEM)
    return pl.pallas_call(
        body, name=name, in_specs=[vm], out_specs=[vm, vm],
        out_shape=[jax.ShapeDtypeStruct((R, C), F32), jax.ShapeDtypeStruct((ND * R, C), F32)],
        scratch_shapes=[pltpu.SemaphoreType.DMA((7,)), pltpu.SemaphoreType.DMA((7,)), pltpu.SemaphoreType.DMA],
        compiler_params=pltpu.CompilerParams(vmem_limit_bytes=VMEM_LIMIT),
    )(v)[0]


HBM = pl.BlockSpec(memory_space=pltpu.HBM)
SEM = pl.BlockSpec(memory_space=pltpu.SEMAPHORE)
EFFECT = pltpu.SideEffectType.DATAFLOW_SIDE_EFFECTING


def _remote(src, dst, send, recv, k, to):
    return pltpu.make_async_remote_copy(src_ref=src, dst_ref=dst, send_sem=send.at[k], recv_sem=recv.at[k],
                                        device_id=to, device_id_type=MESH)


def _split_start(bufs, plan, ncopies, *, name, after=None):
    nb = len(bufs)
    extra = [] if after is None else [after]

    def body(*refs):
        pos = nb + len(extra)
        send, recv, token = refs[pos], refs[pos + 1], refs[-1]
        for k, (src, dst, to) in enumerate(plan(refs[:nb])):
            _remote(src, dst, send, recv, k, to).start()
        token[...] = jnp.zeros_like(token)

    outs = pl.pallas_call(
        body, name=name,
        out_shape=(pltpu.SemaphoreType.DMA((ncopies,)), pltpu.SemaphoreType.DMA((ncopies,)),
                   *[pltpu.HBM(b.shape, b.dtype) for b in bufs], jax.ShapeDtypeStruct((SUBLANES, LANES), F32)),
        in_specs=[HBM] * nb + [ANY] * len(extra),
        out_specs=(SEM, SEM, *[HBM] * nb, pl.BlockSpec(memory_space=pltpu.VMEM)),
        input_output_aliases={i: 2 + i for i in range(nb)},
        compiler_params=pltpu.CompilerParams(has_side_effects=EFFECT),
    )(*[pltpu.with_memory_space_constraint(b, pltpu.HBM) for b in bufs], *extra)
    return outs[0], outs[1], list(outs[2:2 + nb]), outs[-1]


def _split_wait(started, plan, after, *, name):
    send, recv, bufs, _ = started
    nb = len(bufs)

    def body(*refs):
        send_sem, recv_sem = refs[nb], refs[nb + 1]
        for k, (src, dst, to) in enumerate(plan(refs[:nb])):
            cp = _remote(src, dst, send_sem, recv_sem, k, to)
            cp.wait_send()
            cp.wait_recv()

    outs = pl.pallas_call(
        body, name=name,
        out_shape=tuple(pltpu.HBM(b.shape, b.dtype) for b in bufs),
        in_specs=[HBM] * nb + [SEM, SEM, ANY], out_specs=tuple([HBM] * nb),
        input_output_aliases={i: i for i in range(nb)},
        compiler_params=pltpu.CompilerParams(has_side_effects=EFFECT),
    )(*bufs, send, recv, after)
    return list(outs)


def _gather_chip_plan(shapes, by_cols):
    n = len(shapes)

    def plan(refs):
        srcs, lands = refs[:n], refs[n:]
        x, y, c, chips = _place()
        out = []
        for w in range(n):
            h = shapes[w][0] // 2
            mine = _owner_rows(lands[w], shapes[w], by_cols[w], 2 * x + y, c * h, h)
            for tx, ty in chips:
                out.append((srcs[w].at[pl.ds(c * h, h), :], mine, (tx, ty, c)))
        return out

    return plan


def _gather_pair_plan(shapes, by_cols):
    n = len(shapes)

    def plan(refs):
        srcs, lands = refs[:n], refs[n:]
        x, y, c, chips = _place()
        out = []
        for w in range(n):
            h = shapes[w][0] // 2
            for tx, ty in chips:
                half = _owner_rows(lands[w], shapes[w], by_cols[w], 2 * tx + ty, c * h, h)
                out.append((half, half, (x, y, 1 - c)))
            own = _owner_rows(lands[w], shapes[w], by_cols[w], 2 * x + y, 0, shapes[w][0])
            out.append((srcs[w], own, (x, y, 1 - c)))
        return out

    return plan


def _reduce_pair_plan(shapes):
    n = len(shapes)

    def plan(refs):
        local, lands = refs[:n], refs[n:]
        x, y, c, _ = _place()
        out = []
        for w in range(n):
            h = shapes[w][1] // 2
            out.append((local[w].at[:, pl.ds((1 - c) * h, h), :], lands[w], (x, y, 1 - c)))
        return out

    return plan


def _reduce_chip_plan(n):
    def plan(refs):
        parts, lands = refs[:n], refs[n:]
        x, y, c, chips = _place()
        return [(parts[w].at[2 * tx + ty], lands[w].at[j], (tx, ty, c))
                for w in range(n) for j, (tx, ty) in enumerate(chips)]

    return plan


def _as_operands(gathered):
    out = {}
    for n, g in gathered.items():
        if n in COL_SHARDED:
            out[n] = g
        elif n == "w_in":
            out[n] = jnp.transpose(g, (1, 0, 2)).reshape(D_MODEL, IN_COLS)
        else:
            out[n] = g.reshape(g.shape[0] * g.shape[1], g.shape[2])
    return out


def _by_owner(n, g):
    if n == "w_in":
        return jnp.transpose(g.reshape(D_MODEL, N_CHIPS, IN_COLS // N_CHIPS), (1, 0, 2))
    if g.ndim == 2:
        return g.reshape(N_CHIPS, g.shape[0] // N_CHIPS, g.shape[1])
    return g


class _Comm:
    def __init__(self, shards):
        self.shards = shards
        self.total = {}
        self._flight = {}

    def _layout(self, names):
        return [self.shards[n].shape for n in names], [n in COL_SHARDED for n in names]

    def gather_now(self, tag, names):
        _, by_cols = self._layout(names)
        got = _allgather_weights([self.shards[n] for n in names], by_cols, name=f"gather_{tag}")
        return _as_operands(dict(zip(names, got)))

    def gather_start(self, tag, names, after):
        shapes, by_cols = self._layout(names)
        srcs = [self.shards[n] for n in names]
        lands = [lax.empty(_gathered_shape(s.shape, bc), s.dtype) for s, bc in zip(srcs, by_cols)]
        started = _split_start(srcs + lands, _gather_chip_plan(shapes, by_cols), 3 * len(srcs),
                               after=after, name=f"gather_{tag}_chips_start")
        self._flight[tag] = (names, started)
        return started[3]

    def gather_landed(self, tag, after):
        names, started = self._flight[tag]
        shapes, by_cols = self._layout(names)
        bufs = _split_wait(started, _gather_chip_plan(shapes, by_cols), after, name=f"gather_{tag}_chips_wait")
        started = _split_start(bufs, _gather_pair_plan(shapes, by_cols), 4 * len(names),
                               name=f"gather_{tag}_pair_start")
        self._flight[tag] = (names, started)
        return started[3]

    def gather_ready(self, tag, after):
        names, started = self._flight.pop(tag)
        shapes, by_cols = self._layout(names)
        bufs = _split_wait(started, _gather_pair_plan(shapes, by_cols), after, name=f"gather_{tag}_pair_wait")
        return _as_operands(dict(zip(names, bufs[len(names):])))

    def reduce_start(self, tag, grads):
        names = list(grads)
        local = [_by_owner(n, grads[n]) for n in names]
        return self._chip_start(tag, names, local, _pair_exchange(local, name=f"pair_exchange_{tag}"))

    def reduce_pair_start(self, tag, grads):
        names = list(grads)
        local = [_by_owner(n, grads[n]) for n in names]
        lands = [lax.empty((N_CHIPS, g.shape[1] // 2, g.shape[2]), g.dtype) for g in local]
        started = _split_start(local + lands, _reduce_pair_plan([g.shape for g in local]), len(names),
                               name=f"pair_exchange_{tag}_start")
        self._flight[tag] = (names, started)
        return started[3]

    def reduce_pair_done(self, tag, after):
        names, started = self._flight.pop(tag)
        n = len(names)
        bufs = _split_wait(started, _reduce_pair_plan([b.shape for b in started[2][:n]]), after,
                           name=f"pair_exchange_{tag}_wait")
        return self._chip_start(tag, names, bufs[:n], bufs[n:])

    def _chip_start(self, tag, names, local, from_sib):
        parts = [_pair_sum(g, s, name=f"pair_sum_{n}") for n, g, s in zip(names, local, from_sib)]
        lands = [lax.empty((N_CHIPS - 1,) + p.shape[1:], p.dtype) for p in parts]
        self._flight[tag] = (names, _split_start(parts + lands, _reduce_chip_plan(len(names)), 3 * len(names),
                                                 name=f"chip_exchange_{tag}_start"))
        return self._flight[tag][1][3]

    def reduce_finish(self, tag, after):
        names, started = self._flight.pop(tag)
        n = len(names)
        bufs = _split_wait(started, _reduce_chip_plan(n), after, name=f"chip_exchange_{tag}_wait")
        totals = [_chip_sum(p, s, name=f"chip_sum_{nm}") for nm, p, s in zip(names, bufs[:n], bufs[n:])]
        self.total.update(zip(names, _pair_gather(totals, name=f"pair_gather_{tag}")))


def _adamw(w, g, m, v, *, name):
    R, C = w.shape
    tr = _row_block(R, 8 * C * 4, SUBLANES)

    def body(w_ref, g_ref, m_ref, v_ref, go_ref, d_ref, nm_ref, nv_ref):
        gg = g_ref[...]
        go_ref[...] = gg
        m_new = ADAM_B1 * m_ref[...] + (1.0 - ADAM_B1) * gg
        v_new = ADAM_B2 * v_ref[...] + (1.0 - ADAM_B2) * (gg * gg)
        m_hat = m_new / (1.0 - ADAM_B1 ** ADAM_STEP)
        v_hat = v_new / (1.0 - ADAM_B2 ** ADAM_STEP)
        d_ref[...] = -ADAM_LR * (m_hat / (jnp.sqrt(v_hat) + ADAM_EPS) + ADAM_WD * w_ref[...])
        nm_ref[...] = m_new
        nv_ref[...] = v_new

    blk = pl.BlockSpec((tr, C), lambda i: (i, 0))
    shp = jax.ShapeDtypeStruct((R, C), F32)
    return pl.pallas_call(
        body, name=name, grid=(R // tr,), in_specs=[blk] * 4, out_specs=[blk] * 4, out_shape=[shp] * 4,
        compiler_params=_params(("parallel",)),
    )(w, g, m, v)


def _to2d(a):
    flat = a.reshape(-1)
    pad = (-flat.shape[0]) % (SUBLANES * LANES)
    if pad:
        flat = jnp.pad(flat, (0, pad))
    return flat.reshape(-1, LANES)


def _small_rows(shape):
    return -(-math.prod(shape) // (SUBLANES * LANES)) * SUBLANES


def _pack_small(parts):
    rows = jnp.concatenate([_to2d(p) for p in parts], axis=0)
    pad = (-rows.shape[0]) % 256
    if pad:
        rows = jnp.concatenate([rows, jnp.zeros((pad, LANES), rows.dtype)], axis=0)
    return rows


def _unpack_small(rows, shapes):
    out, r = [], 0
    for shp in shapes:
        size = math.prod(shp)
        nrow = _small_rows(shp)
        out.append(rows[r:r + nrow].reshape(-1)[:size].reshape(shp))
        r += nrow
    return out


def kernel(x, mem, g_ffn1, w1_gate, w1_up, w1_down, g_mix, w_in, g_v, w_s, b_s, sinks, g_a_out, g_b_out, w_out, g_x, g_mem, w_xq, w_xkv, w_xo, g_ffn2, w2_gate, w2_up, w2_down, g_final, loss_target, m_g_ffn1, m_w1_gate, m_w1_up, m_w1_down, m_g_mix, m_w_in, m_g_v, m_w_s, m_b_s, m_sinks, m_g_a_out, m_g_b_out, m_w_out, m_g_x, m_g_mem, m_w_xq, m_w_xkv, m_w_xo, m_g_ffn2, m_w2_gate, m_w2_up, m_w2_down, m_g_final, v_g_ffn1, v_w1_gate, v_w1_up, v_w1_down, v_g_mix, v_w_in, v_g_v, v_w_s, v_b_s, v_sinks, v_g_a_out, v_g_b_out, v_w_out, v_g_x, v_g_mem, v_w_xq, v_w_xkv, v_w_xo, v_g_ffn2, v_w2_gate, v_w2_up, v_w2_down, v_g_final):
    args = dict(locals())
    Wp = {n: args[n] for n in ALL_W}
    Mp = {n: args["m_" + n] for n in ALL_W}
    Vp = {n: args["v_" + n] for n in ALL_W}

    comm = _Comm({n: Wp[n][0].astype(BF16) for n in BIG})
    W = {n: Wp[n] for n in SMALL}
    W["g_final"] = Wp["g_final"].reshape(1, D_MODEL)
    for n in ("w_s", "b_s"):
        W[n] = Wp[n][0]
    loss, dx, grads = _local_step(x[0], mem[0], loss_target[0], W, comm)
    big_grad = comm.total

    small_shapes = [Wp[n].shape for n in SMALL]
    packed = _pack_small([grads[n].reshape(Wp[n].shape) for n in SMALL] + [loss])
    summed = _allreduce_small(packed, name="allreduce_small")
    small_grad = dict(zip(SMALL, _unpack_small(summed, small_shapes)))
    nrows = sum(_small_rows(s) for s in small_shapes)
    loss_total = summed[nrows, 0]

    grad_out, delta, new_m, new_v = {}, {}, {}, {}
    for n in BIG:
        shp = Wp[n].shape
        g, d, nm, nv = _adamw(Wp[n][0], big_grad[n], Mp[n][0], Vp[n][0], name=f"adamw_{n}")
        grad_out[n], delta[n], new_m[n], new_v[n] = g.reshape(shp), d.reshape(shp), nm.reshape(shp), nv.reshape(shp)
    sw = _pack_small([Wp[n] for n in SMALL])
    sg = _pack_small([small_grad[n] for n in SMALL])
    sm = _pack_small([Mp[n] for n in SMALL])
    sv = _pack_small([Vp[n] for n in SMALL])
    _, d, nm, nv = _adamw(sw, sg, sm, sv, name="adamw_small")
    for n, dd, mm_, vv_ in zip(SMALL, _unpack_small(d, small_shapes), _unpack_small(nm, small_shapes),
                               _unpack_small(nv, small_shapes)):
        grad_out[n], delta[n], new_m[n], new_v[n] = small_grad[n], dd, mm_, vv_

    return (loss_total, dx[None], *[grad_out[n] for n in ALL_W], *[delta[n] for n in ALL_W],
            *[new_m[n] for n in ALL_W], *[new_v[n] for n in ALL_W])
```
